```python
import math
import jax, jax.numpy as jnp
from jax import lax
import numpy as np

D_MODEL = 1024
BATCH = 8
SEQ = 8192
DEPTH = 2

N_MIXERS = 2
N_ATTN_LAYERS = (DEPTH + 1) // 2
N_DELTA_LAYERS = DEPTH // 2
PLE_DIM = 256
EPS = 1e-6

SWA_GROUPS = ((128, 1), (512, 4), (2048, 16))
N_GROUPS = len(SWA_GROUPS)
A_HEADS = 8
A_HEAD_DIM = 64
A_OUT_WIDTH = A_HEADS * A_HEAD_DIM
A_QKV_WIDTH = N_GROUPS * 3 * A_OUT_WIDTH
ROPE_DIM = A_HEAD_DIM // 4
ROPE_THETA = 500000.0
BAND_BLOCK = 128

DN_HEADS = 8
DN_HEAD_DIM = 128
DN_WIDTH = DN_HEADS * DN_HEAD_DIM
DN_IN_WIDTH = 3 * DN_WIDTH + 2 * DN_HEADS + DN_WIDTH
CONV_WIDTH = 4
CHUNK = 64

D_FF = 4 * D_MODEL

kernel_name = 'hybrid_dilated_swa_gated_deltanet'


def rmsnorm(x, gain):
    xf = x.astype(jnp.float32)
    y = xf * lax.rsqrt(jnp.mean(xf * xf, axis=-1, keepdims=True) + EPS)
    return (y * gain.astype(jnp.float32)).astype(x.dtype)


def l2norm(x):
    xf = x.astype(jnp.float32)
    return xf * lax.rsqrt(jnp.sum(xf * xf, axis=-1, keepdims=True) + EPS)


def rope_tables(positions):
    inv_freq = ROPE_THETA ** (-jnp.arange(0, ROPE_DIM, 2, dtype=jnp.float32) / ROPE_DIM)
    ang = positions.astype(jnp.float32)[..., None] * inv_freq
    return jnp.cos(ang)[:, :, None, :], jnp.sin(ang)[:, :, None, :]


def apply_partial_rope(x, cos, sin):
    half = ROPE_DIM // 2
    xf = x.astype(jnp.float32)
    x1, x2 = xf[..., :half], xf[..., half:ROPE_DIM]
    out = jnp.concatenate([x1 * cos - x2 * sin, x2 * cos + x1 * sin, xf[..., ROPE_DIM:]], axis=-1)
    return out.astype(x.dtype)


def dilated_band_attention(q, k, v, window, dilation):
    B, S, H, E = q.shape
    L = S // dilation
    span = window // dilation
    n_prev = -(-span // BAND_BLOCK)
    nblk = -(-L // BAND_BLOCK)
    Lp = nblk * BAND_BLOCK

    def by_residue(t):
        t = t.reshape(B, L, dilation, H, E).transpose(0, 2, 1, 3, 4)
        t = jnp.pad(t, ((0, 0), (0, 0), (0, Lp - L), (0, 0), (0, 0)))
        return t.reshape(B, dilation, nblk, BAND_BLOCK, H, E)

    def band(t):
        tp = jnp.pad(t, ((0, 0), (0, 0), (n_prev, 0), (0, 0), (0, 0), (0, 0)))
        return jnp.concatenate([tp[:, :, s:s + nblk] for s in range(n_prev + 1)], axis=3)

    qb = by_residue(q)
    kw = band(by_residue(k))
    vw = band(by_residue(v))
    qi = jnp.arange(BAND_BLOCK)[:, None]
    kj = jnp.arange((n_prev + 1) * BAND_BLOCK)[None, :]
    dist = qi + n_prev * BAND_BLOCK - kj
    k_abs = (jnp.arange(nblk)[:, None, None] - n_prev) * BAND_BLOCK + kj[None]
    valid = (dist >= 0)[None] & (dist <= span)[None] & (k_abs >= 0)

    s = jnp.einsum('bdnqhe,bdnkhe->bdnhqk', qb, kw).astype(jnp.float32) * (E ** -0.5)
    s = jnp.where(valid[None, None, :, None], s, -jnp.inf)
    m = jnp.max(s, axis=-1, keepdims=True)
    e = jnp.exp(s - m)
    l = jnp.sum(e, axis=-1, keepdims=True)
    o = jnp.einsum('bdnhqk,bdnkhe->bdnqhe', (e / l).astype(v.dtype), vw)
    lse = (m + jnp.log(l))[..., 0]
    o = o.reshape(B, dilation, Lp, H, E)[:, :, :L].transpose(0, 2, 1, 3, 4).reshape(B, S, H, E)
    lse = lse.transpose(0, 1, 2, 4, 3).reshape(B, dilation, Lp, H)[:, :, :L]
    lse = lse.transpose(0, 2, 1, 3).reshape(B, S, H)
    return o, lse


def dilated_attention_mixer(h, cos, sin, w_qkv, q_gain, k_gain, w_o):
    B, S, _ = h.shape
    qkv = (h @ w_qkv).reshape(B, S, N_GROUPS, 3, A_HEADS, A_HEAD_DIM)
    outs, lses = [], []
    for g, (window, dilation) in enumerate(SWA_GROUPS):
        q = apply_partial_rope(rmsnorm(qkv[:, :, g, 0], q_gain[g]), cos, sin)
        k = apply_partial_rope(rmsnorm(qkv[:, :, g, 1], k_gain[g]), cos, sin)
        o, lse = dilated_band_attention(q, k, qkv[:, :, g, 2], window, dilation)
        outs.append(o)
        lses.append(lse)
    wts = jax.nn.softmax(jnp.stack(lses, axis=0), axis=0)
    o = jnp.einsum('gbsh,gbshe->bshe', wts.astype(h.dtype), jnp.stack(outs, axis=0))
    return o.reshape(B, S, A_OUT_WIDTH) @ w_o


def causal_depthwise_conv(x, w):
    C = x.shape[-1]
    return lax.conv_general_dilated(
        x, w[:, None, :].astype(x.dtype), window_strides=(1,),
        padding=[(CONV_WIDTH - 1, 0)], dimension_numbers=('NWC', 'WIO', 'NWC'),
        feature_group_count=C)


def to_chunks(t):
    B, S = t.shape[:2]
    t = t.reshape(B, S // CHUNK, CHUNK, *t.shape[2:])
    return jnp.moveaxis(t, 2, 3)


def chunk_gated_delta_rule(q, k, v, g, beta):
    B, S, H, K = q.shape
    V = v.shape[-1]
    qc, kc, vc = to_chunks(q), to_chunks(k), to_chunks(v)
    gc = jnp.cumsum(to_chunks(g), axis=-1)
    bc = to_chunks(beta)[..., None]
    lower = jnp.tril(jnp.ones((CHUNK, CHUNK), dtype=bool))
    strict = jnp.tril(jnp.ones((CHUNK, CHUNK), dtype=bool), -1)
    decay = jnp.exp(jnp.where(lower, gc[..., :, None] - gc[..., None, :], -jnp.inf))
    kb = kc * bc
    a_mat = jnp.where(strict, jnp.einsum('bnhik,bnhjk->bnhij', kb, kc) * decay, 0.0)
    rhs = jnp.concatenate([vc * bc, kb * jnp.exp(gc)[..., None]], axis=-1)
    sol = lax.linalg.triangular_solve(a_mat + jnp.eye(CHUNK, dtype=a_mat.dtype), rhs,
                                      left_side=True, lower=True, unit_diagonal=True)
    u, w = sol[..., :V], sol[..., V:]
    attn = jnp.einsum('bnhik,bnhjk->bnhij', qc, kc) * decay
    q_dec = qc * jnp.exp(gc)[..., None]
    k_dec = kc * jnp.exp(gc[..., -1:] - gc)[..., None]
    c_dec = jnp.exp(gc[..., -1])

    def step(state, inp):
        u_i, w_i, qd_i, kd_i, at_i, cd_i = inp
        v_new = u_i - jnp.einsum('bhck,bhkv->bhcv', w_i, state)
        o_i = jnp.einsum('bhck,bhkv->bhcv', qd_i, state) + jnp.einsum('bhij,bhjv->bhiv', at_i, v_new)
        state = state * cd_i[..., None, None] + jnp.einsum('bhck,bhcv->bhkv', kd_i, v_new)
        return state, o_i

    xs = tuple(jnp.moveaxis(t, 1, 0) for t in (u, w, q_dec, k_dec, attn, c_dec))
    state0 = jnp.zeros((B, H, K, V), dtype=jnp.float32)
    _, o = lax.scan(step, state0, xs)
    return jnp.moveaxis(o, 0, 1).swapaxes(2, 3).reshape(B, S, H, V)


def gated_deltanet_mixer(h, w_in, conv_w, a_log, dt_bias, o_gain, w_o):
    B, S, _ = h.shape
    proj = h @ w_in
    c = 3 * DN_WIDTH
    qkv = proj[..., :c]
    a = proj[..., c:c + DN_HEADS]
    b = proj[..., c + DN_HEADS:c + 2 * DN_HEADS]
    z = proj[..., c + 2 * DN_HEADS:]
    qkv = jax.nn.silu(causal_depthwise_conv(qkv, conv_w))
    q, k, v = jnp.split(qkv, 3, axis=-1)
    q = l2norm(q.reshape(B, S, DN_HEADS, DN_HEAD_DIM)) * (DN_HEAD_DIM ** -0.5)
    k = l2norm(k.reshape(B, S, DN_HEADS, DN_HEAD_DIM))
    v = v.reshape(B, S, DN_HEADS, DN_HEAD_DIM).astype(jnp.float32)
    beta = jax.nn.sigmoid(b.astype(jnp.float32))
    g = -jnp.exp(a_log.astype(jnp.float32)) * jax.nn.softplus(a.astype(jnp.float32) + dt_bias.astype(jnp.float32))
    o = chunk_gated_delta_rule(q, k, v, g, beta)
    o = rmsnorm(o, o_gain) * jax.nn.silu(z.reshape(B, S, DN_HEADS, DN_HEAD_DIM).astype(jnp.float32))
    return o.reshape(B, S, DN_WIDTH).astype(h.dtype) @ w_o


def _fwd_setup_inputs(seed: int = 0) -> dict:
    key = jax.random.key(seed)
    ks = jax.random.split(key, 20)
    NA, NB = N_ATTN_LAYERS, N_DELTA_LAYERS

    def nrm(k, shape, fan_in):
        return jax.random.normal(k, shape, jnp.float32) * (fan_in ** -0.5)

    def gain(k, shape):
        return 1.0 + 0.02 * jax.random.normal(k, shape, jnp.float32)

    dt = jnp.exp(jax.random.uniform(ks[9], (NB, DN_HEADS), jnp.float32,
                                    minval=math.log(1e-3), maxval=math.log(1e-1)))
    return {
        'x': jax.random.normal(ks[0], (BATCH, SEQ, D_MODEL), jnp.float32),
        'p': jax.random.normal(ks[1], (DEPTH, BATCH, SEQ, PLE_DIM), jnp.float32),
        'positions': jnp.broadcast_to(jnp.arange(SEQ, dtype=jnp.int32), (BATCH, SEQ)),
        'mix_norm': gain(ks[2], (DEPTH, D_MODEL)),
        'attn_w_qkv': nrm(ks[3], (NA, D_MODEL, A_QKV_WIDTH), D_MODEL),
        'attn_q_gain': gain(ks[4], (NA, N_GROUPS, A_HEAD_DIM)),
        'attn_k_gain': gain(ks[5], (NA, N_GROUPS, A_HEAD_DIM)),
        'attn_w_o': nrm(ks[6], (NA, A_OUT_WIDTH, D_MODEL), A_OUT_WIDTH),
        'dn_w_in': nrm(ks[7], (NB, D_MODEL, DN_IN_WIDTH), D_MODEL),
        'dn_conv': nrm(ks[8], (NB, CONV_WIDTH, 3 * DN_WIDTH), CONV_WIDTH),
        'dn_a_log': jnp.log(jax.random.uniform(ks[10], (NB, DN_HEADS), jnp.float32, minval=1.0, maxval=16.0)),
        'dn_dt_bias': dt + jnp.log(-jnp.expm1(-dt)),
        'dn_o_gain': gain(ks[11], (NB, DN_HEAD_DIM)),
        'dn_w_o': nrm(ks[12], (NB, DN_WIDTH, D_MODEL), DN_WIDTH),
        'mlp_norm': gain(ks[13], (DEPTH, D_MODEL)),
        'w_up': nrm(ks[14], (DEPTH, D_MODEL, D_FF), D_MODEL),
        'w_down': nrm(ks[15], (DEPTH, D_FF, D_MODEL), D_FF),
        'ple_norm': gain(ks[16], (DEPTH, D_MODEL)),
        'w_ple': nrm(ks[17], (DEPTH, PLE_DIM, D_MODEL), PLE_DIM),
        'w_ple_gate': nrm(ks[18], (DEPTH, D_MODEL, D_MODEL), D_MODEL),
    }


def _fwd_reference(x, p, positions, mix_norm, attn_w_qkv, attn_q_gain, attn_k_gain, attn_w_o,
              dn_w_in, dn_conv, dn_a_log, dn_dt_bias, dn_o_gain, dn_w_o,
              mlp_norm, w_up, w_down, ple_norm, w_ple, w_ple_gate):
    cos, sin = rope_tables(positions)
    for i in range(DEPTH):
        j = i // N_MIXERS
        hn = rmsnorm(x, mix_norm[i])
        if i % N_MIXERS == 0:
            mix = dilated_attention_mixer(hn, cos, sin, attn_w_qkv[j], attn_q_gain[j],
                                          attn_k_gain[j], attn_w_o[j])
        else:
            mix = gated_deltanet_mixer(hn, dn_w_in[j], dn_conv[j], dn_a_log[j], dn_dt_bias[j],
                                       dn_o_gain[j], dn_w_o[j])
        x = x + mix
        hn = rmsnorm(x, mlp_norm[i])
        x = x + jnp.square(jax.nn.relu(hn @ w_up[i])) @ w_down[i]
        gate = jax.nn.sigmoid((rmsnorm(x, ple_norm[i]) @ w_ple_gate[i]).astype(jnp.float32)).astype(x.dtype)
        x = x + (p[i] @ w_ple[i]) * gate
    return x


import jax as _jax
import jax.numpy as _jnp

TWIN_FORMAT = 'train_step'
FWD_PARAMS = ['x', 'p', 'positions', 'mix_norm', 'attn_w_qkv', 'attn_q_gain', 'attn_k_gain', 'attn_w_o', 'dn_w_in', 'dn_conv', 'dn_a_log', 'dn_dt_bias', 'dn_o_gain', 'dn_w_o', 'mlp_norm', 'w_up', 'w_down', 'ple_norm', 'w_ple', 'w_ple_gate']
TWIN_WEIGHTS = ['mix_norm', 'attn_w_qkv', 'attn_q_gain', 'attn_k_gain', 'attn_w_o', 'dn_w_in', 'dn_conv', 'dn_a_log', 'dn_dt_bias', 'dn_o_gain', 'dn_w_o', 'mlp_norm', 'w_up', 'w_down', 'ple_norm', 'w_ple', 'w_ple_gate']
TWIN_DIFF_INPUT = 'x'
TWIN_INPUTS = ['x', 'p', 'positions', 'mix_norm', 'attn_w_qkv', 'attn_q_gain', 'attn_k_gain', 'attn_w_o', 'dn_w_in', 'dn_conv', 'dn_a_log', 'dn_dt_bias', 'dn_o_gain', 'dn_w_o', 'mlp_norm', 'w_up', 'w_down', 'ple_norm', 'w_ple', 'w_ple_gate', 'loss_target', 'm_mix_norm', 'm_attn_w_qkv', 'm_attn_q_gain', 'm_attn_k_gain', 'm_attn_w_o', 'm_dn_w_in', 'm_dn_conv', 'm_dn_a_log', 'm_dn_dt_bias', 'm_dn_o_gain', 'm_dn_w_o', 'm_mlp_norm', 'm_w_up', 'm_w_down', 'm_ple_norm', 'm_w_ple', 'm_w_ple_gate', 'v_mix_norm', 'v_attn_w_qkv', 'v_attn_q_gain', 'v_attn_k_gain', 'v_attn_w_o', 'v_dn_w_in', 'v_dn_conv', 'v_dn_a_log', 'v_dn_dt_bias', 'v_dn_o_gain', 'v_dn_w_o', 'v_mlp_norm', 'v_w_up', 'v_w_down', 'v_ple_norm', 'v_w_ple', 'v_w_ple_gate']
TWIN_OUTPUTS = ['loss', 'grad_x', 'grad_mix_norm', 'grad_attn_w_qkv', 'grad_attn_q_gain', 'grad_attn_k_gain', 'grad_attn_w_o', 'grad_dn_w_in', 'grad_dn_conv', 'grad_dn_a_log', 'grad_dn_dt_bias', 'grad_dn_o_gain', 'grad_dn_w_o', 'grad_mlp_norm', 'grad_w_up', 'grad_w_down', 'grad_ple_norm', 'grad_w_ple', 'grad_w_ple_gate', 'delta_mix_norm', 'delta_attn_w_qkv', 'delta_attn_q_gain', 'delta_attn_k_gain', 'delta_attn_w_o', 'delta_dn_w_in', 'delta_dn_conv', 'delta_dn_a_log', 'delta_dn_dt_bias', 'delta_dn_o_gain', 'delta_dn_w_o', 'delta_mlp_norm', 'delta_w_up', 'delta_w_down', 'delta_ple_norm', 'delta_w_ple', 'delta_w_ple_gate', 'new_m_mix_norm', 'new_m_attn_w_qkv', 'new_m_attn_q_gain', 'new_m_attn_k_gain', 'new_m_attn_w_o', 'new_m_dn_w_in', 'new_m_dn_conv', 'new_m_dn_a_log', 'new_m_dn_dt_bias', 'new_m_dn_o_gain', 'new_m_dn_w_o', 'new_m_mlp_norm', 'new_m_w_up', 'new_m_w_down', 'new_m_ple_norm', 'new_m_w_ple', 'new_m_w_ple_gate', 'new_v_mix_norm', 'new_v_attn_w_qkv', 'new_v_attn_q_gain', 'new_v_attn_k_gain', 'new_v_attn_w_o', 'new_v_dn_w_in', 'new_v_dn_conv', 'new_v_dn_a_log', 'new_v_dn_dt_bias', 'new_v_dn_o_gain', 'new_v_dn_w_o', 'new_v_mlp_norm', 'new_v_w_up', 'new_v_w_down', 'new_v_ple_norm', 'new_v_w_ple', 'new_v_w_ple_gate']
TWIN_LEAF_KINDS = {'loss': 'loss', 'grad_x': 'grad_x', 'grad_mix_norm': 'grad_w', 'grad_attn_w_qkv': 'grad_w', 'grad_attn_q_gain': 'grad_w', 'grad_attn_k_gain': 'grad_w', 'grad_attn_w_o': 'grad_w', 'grad_dn_w_in': 'grad_w', 'grad_dn_conv': 'grad_w', 'grad_dn_a_log': 'grad_w', 'grad_dn_dt_bias': 'grad_w', 'grad_dn_o_gain': 'grad_w', 'grad_dn_w_o': 'grad_w', 'grad_mlp_norm': 'grad_w', 'grad_w_up': 'grad_w', 'grad_w_down': 'grad_w', 'grad_ple_norm': 'grad_w', 'grad_w_ple': 'grad_w', 'grad_w_ple_gate': 'grad_w', 'delta_mix_norm': 'delta_w', 'delta_attn_w_qkv': 'delta_w', 'delta_attn_q_gain': 'delta_w', 'delta_attn_k_gain': 'delta_w', 'delta_attn_w_o': 'delta_w', 'delta_dn_w_in': 'delta_w', 'delta_dn_conv': 'delta_w', 'delta_dn_a_log': 'delta_w', 'delta_dn_dt_bias': 'delta_w', 'delta_dn_o_gain': 'delta_w', 'delta_dn_w_o': 'delta_w', 'delta_mlp_norm': 'delta_w', 'delta_w_up': 'delta_w', 'delta_w_down': 'delta_w', 'delta_ple_norm': 'delta_w', 'delta_w_ple': 'delta_w', 'delta_w_ple_gate': 'delta_w', 'new_m_mix_norm': 'new_m', 'new_m_attn_w_qkv': 'new_m', 'new_m_attn_q_gain': 'new_m', 'new_m_attn_k_gain': 'new_m', 'new_m_attn_w_o': 'new_m', 'new_m_dn_w_in': 'new_m', 'new_m_dn_conv': 'new_m', 'new_m_dn_a_log': 'new_m', 'new_m_dn_dt_bias': 'new_m', 'new_m_dn_o_gain': 'new_m', 'new_m_dn_w_o': 'new_m', 'new_m_mlp_norm': 'new_m', 'new_m_w_up': 'new_m', 'new_m_w_down': 'new_m', 'new_m_ple_norm': 'new_m', 'new_m_w_ple': 'new_m', 'new_m_w_ple_gate': 'new_m', 'new_v_mix_norm': 'new_v', 'new_v_attn_w_qkv': 'new_v', 'new_v_attn_q_gain': 'new_v', 'new_v_attn_k_gain': 'new_v', 'new_v_attn_w_o': 'new_v', 'new_v_dn_w_in': 'new_v', 'new_v_dn_conv': 'new_v', 'new_v_dn_a_log': 'new_v', 'new_v_dn_dt_bias': 'new_v', 'new_v_dn_o_gain': 'new_v', 'new_v_dn_w_o': 'new_v', 'new_v_mlp_norm': 'new_v', 'new_v_w_up': 'new_v', 'new_v_w_down': 'new_v', 'new_v_ple_norm': 'new_v', 'new_v_w_ple': 'new_v', 'new_v_w_ple_gate': 'new_v'}


def _forward(args):
    return _fwd_reference(*[args[k] for k in FWD_PARAMS])


def _output_shape():
    def fwd():
        inp = _fwd_setup_inputs(0)
        return _fwd_reference(*[inp[k] for k in FWD_PARAMS])
    out = _jax.eval_shape(fwd)
    return out.shape, out.dtype

N_MICROBATCH = 1
ADAM_LR = 0.001
ADAM_B1 = 0.9
ADAM_B2 = 0.999
ADAM_EPS = 1e-08
ADAM_WD = 0.01
ADAM_STEP = 10
PER_EXAMPLE_BATCH_AXIS = {'x': 0, 'p': 1, 'positions': 0, 'loss_target': 0}
SHARED_INPUTS = []
_WEIGHT_DTYPES = {'mix_norm': _jnp.float32, 'attn_w_qkv': _jnp.float32, 'attn_q_gain': _jnp.float32, 'attn_k_gain': _jnp.float32, 'attn_w_o': _jnp.float32, 'dn_w_in': _jnp.float32, 'dn_conv': _jnp.float32, 'dn_a_log': _jnp.float32, 'dn_dt_bias': _jnp.float32, 'dn_o_gain': _jnp.float32, 'dn_w_o': _jnp.float32, 'mlp_norm': _jnp.float32, 'w_up': _jnp.float32, 'w_down': _jnp.float32, 'ple_norm': _jnp.float32, 'w_ple': _jnp.float32, 'w_ple_gate': _jnp.float32}
MOMENT_SCALE = {'mix_norm': 2.418279e+01, 'attn_w_qkv': 3.517086e-01, 'attn_q_gain': 1.758552e+00, 'attn_k_gain': 1.747237e+00, 'attn_w_o': 5.619683e-01, 'dn_w_in': 8.973695e+00, 'dn_conv': 7.387927e+00, 'dn_a_log': 7.028538e+01, 'dn_dt_bias': 6.645398e+01, 'dn_o_gain': 2.420591e+02, 'dn_w_o': 1.150776e+01, 'mlp_norm': 1.945709e+02, 'w_up': 7.503388e+00, 'w_down': 3.270816e+01, 'ple_norm': 2.982791e+00, 'w_ple': 1.017329e+00, 'w_ple_gate': 2.062356e+00}


def _to_microbatches(a, axis):
    t = _jnp.moveaxis(a, axis, 0)
    t = t.reshape((N_MICROBATCH, t.shape[0] // N_MICROBATCH) + t.shape[1:])
    return _jnp.moveaxis(t, 1, axis + 1)


def setup_inputs(seed: int = 0) -> dict:
    inp = _fwd_setup_inputs(seed)
    key = _jax.random.fold_in(_jax.random.key(seed), 7919)
    shape, _ = _output_shape()
    out = dict(inp)
    out["loss_target"] = _jax.random.normal(_jax.random.fold_in(key, 0), shape, _jnp.float32)
    for i, name in enumerate(TWIN_WEIGHTS):
        w = inp[name].astype(_jnp.float32)
        if MOMENT_SCALE is None:
            s = _jnp.sqrt(_jnp.mean(_jnp.square(w)) + 1e-30)
        else:
            s = MOMENT_SCALE[name]
        km, kv = _jax.random.split(_jax.random.fold_in(key, i + 1))
        out[name] = w
        out["m_" + name] = s * _jax.random.normal(km, w.shape, _jnp.float32)
        out["v_" + name] = (s * s) * _jax.random.uniform(kv, w.shape, _jnp.float32, 0.5, 1.5)
    if N_MICROBATCH > 1:
        for name, axis in PER_EXAMPLE_BATCH_AXIS.items():
            out[name] = _to_microbatches(out[name], axis)
    return {'x': out['x'], 'p': out['p'], 'positions': out['positions'], 'mix_norm': out['mix_norm'], 'attn_w_qkv': out['attn_w_qkv'], 'attn_q_gain': out['attn_q_gain'], 'attn_k_gain': out['attn_k_gain'], 'attn_w_o': out['attn_w_o'], 'dn_w_in': out['dn_w_in'], 'dn_conv': out['dn_conv'], 'dn_a_log': out['dn_a_log'], 'dn_dt_bias': out['dn_dt_bias'], 'dn_o_gain': out['dn_o_gain'], 'dn_w_o': out['dn_w_o'], 'mlp_norm': out['mlp_norm'], 'w_up': out['w_up'], 'w_down': out['w_down'], 'ple_norm': out['ple_norm'], 'w_ple': out['w_ple'], 'w_ple_gate': out['w_ple_gate'], 'loss_target': out['loss_target'], 'm_mix_norm': out['m_mix_norm'], 'm_attn_w_qkv': out['m_attn_w_qkv'], 'm_attn_q_gain': out['m_attn_q_gain'], 'm_attn_k_gain': out['m_attn_k_gain'], 'm_attn_w_o': out['m_attn_w_o'], 'm_dn_w_in': out['m_dn_w_in'], 'm_dn_conv': out['m_dn_conv'], 'm_dn_a_log': out['m_dn_a_log'], 'm_dn_dt_bias': out['m_dn_dt_bias'], 'm_dn_o_gain': out['m_dn_o_gain'], 'm_dn_w_o': out['m_dn_w_o'], 'm_mlp_norm': out['m_mlp_norm'], 'm_w_up': out['m_w_up'], 'm_w_down': out['m_w_down'], 'm_ple_norm': out['m_ple_norm'], 'm_w_ple': out['m_w_ple'], 'm_w_ple_gate': out['m_w_ple_gate'], 'v_mix_norm': out['v_mix_norm'], 'v_attn_w_qkv': out['v_attn_w_qkv'], 'v_attn_q_gain': out['v_attn_q_gain'], 'v_attn_k_gain': out['v_attn_k_gain'], 'v_attn_w_o': out['v_attn_w_o'], 'v_dn_w_in': out['v_dn_w_in'], 'v_dn_conv': out['v_dn_conv'], 'v_dn_a_log': out['v_dn_a_log'], 'v_dn_dt_bias': out['v_dn_dt_bias'], 'v_dn_o_gain': out['v_dn_o_gain'], 'v_dn_w_o': out['v_dn_w_o'], 'v_mlp_norm': out['v_mlp_norm'], 'v_w_up': out['v_w_up'], 'v_w_down': out['v_w_down'], 'v_ple_norm': out['v_ple_norm'], 'v_w_ple': out['v_w_ple'], 'v_w_ple_gate': out['v_w_ple_gate']}


def _loss(weights, diff, rest, loss_target):
    with _jax.named_scope("forward"):
        args = {**rest, TWIN_DIFF_INPUT: diff, **{k: w.astype(_WEIGHT_DTYPES[k]) for k, w in weights.items()}}
        y = _forward(args)
    with _jax.named_scope("loss_head"):
        err = _jnp.square(y.astype(_jnp.float32) - loss_target)
        return 0.5 * _jnp.sum(_jnp.mean(err, axis=-1)) if err.ndim else 0.5 * err


def _adamw(w, g, m, v):
    m = ADAM_B1 * m + (1.0 - ADAM_B1) * g
    v = ADAM_B2 * v + (1.0 - ADAM_B2) * _jnp.square(g)
    m_hat = m / (1.0 - ADAM_B1 ** ADAM_STEP)
    v_hat = v / (1.0 - ADAM_B2 ** ADAM_STEP)
    delta = -ADAM_LR * (m_hat / (_jnp.sqrt(v_hat) + ADAM_EPS) + ADAM_WD * w)
    return delta, m, v


def reference(x, p, positions, mix_norm, attn_w_qkv, attn_q_gain, attn_k_gain, attn_w_o, dn_w_in, dn_conv, dn_a_log, dn_dt_bias, dn_o_gain, dn_w_o, mlp_norm, w_up, w_down, ple_norm, w_ple, w_ple_gate, loss_target, m_mix_norm, m_attn_w_qkv, m_attn_q_gain, m_attn_k_gain, m_attn_w_o, m_dn_w_in, m_dn_conv, m_dn_a_log, m_dn_dt_bias, m_dn_o_gain, m_dn_w_o, m_mlp_norm, m_w_up, m_w_down, m_ple_norm, m_w_ple, m_w_ple_gate, v_mix_norm, v_attn_w_qkv, v_attn_q_gain, v_attn_k_gain, v_attn_w_o, v_dn_w_in, v_dn_conv, v_dn_a_log, v_dn_dt_bias, v_dn_o_gain, v_dn_w_o, v_mlp_norm, v_w_up, v_w_down, v_ple_norm, v_w_ple, v_w_ple_gate):
    given = dict(x=x, p=p, positions=positions, mix_norm=mix_norm, attn_w_qkv=attn_w_qkv, attn_q_gain=attn_q_gain, attn_k_gain=attn_k_gain, attn_w_o=attn_w_o, dn_w_in=dn_w_in, dn_conv=dn_conv, dn_a_log=dn_a_log, dn_dt_bias=dn_dt_bias, dn_o_gain=dn_o_gain, dn_w_o=dn_w_o, mlp_norm=mlp_norm, w_up=w_up, w_down=w_down, ple_norm=ple_norm, w_ple=w_ple, w_ple_gate=w_ple_gate, loss_target=loss_target, m_mix_norm=m_mix_norm, m_attn_w_qkv=m_attn_w_qkv, m_attn_q_gain=m_attn_q_gain, m_attn_k_gain=m_attn_k_gain, m_attn_w_o=m_attn_w_o, m_dn_w_in=m_dn_w_in, m_dn_conv=m_dn_conv, m_dn_a_log=m_dn_a_log, m_dn_dt_bias=m_dn_dt_bias, m_dn_o_gain=m_dn_o_gain, m_dn_w_o=m_dn_w_o, m_mlp_norm=m_mlp_norm, m_w_up=m_w_up, m_w_down=m_w_down, m_ple_norm=m_ple_norm, m_w_ple=m_w_ple, m_w_ple_gate=m_w_ple_gate, v_mix_norm=v_mix_norm, v_attn_w_qkv=v_attn_w_qkv, v_attn_q_gain=v_attn_q_gain, v_attn_k_gain=v_attn_k_gain, v_attn_w_o=v_attn_w_o, v_dn_w_in=v_dn_w_in, v_dn_conv=v_dn_conv, v_dn_a_log=v_dn_a_log, v_dn_dt_bias=v_dn_dt_bias, v_dn_o_gain=v_dn_o_gain, v_dn_w_o=v_dn_w_o, v_mlp_norm=v_mlp_norm, v_w_up=v_w_up, v_w_down=v_w_down, v_ple_norm=v_ple_norm, v_w_ple=v_w_ple, v_w_ple_gate=v_w_ple_gate)
    weights = {n: given[n] for n in TWIN_WEIGHTS}
    shared = {n: given[n] for n in SHARED_INPUTS}
    per_example = {n: given[n] for n in ['x', 'p', 'positions']}
    grad_fn = _jax.value_and_grad(_loss, argnums=(0, 1))

    def one_microbatch(ex, loss_target):
        ex = dict(ex)
        diff = ex.pop(TWIN_DIFF_INPUT)
        return grad_fn(weights, diff, {**shared, **ex}, loss_target)

    if N_MICROBATCH == 1:
        loss, (grad_w, grad_x) = one_microbatch(per_example, given["loss_target"])
    else:
        def body(carry, xs):
            loss_sum, grad_sum = carry
            l_k, (gw_k, gx_k) = one_microbatch(xs[0], xs[1])
            with _jax.named_scope("update"):
                return (loss_sum + l_k, _jax.tree.map(_jnp.add, grad_sum, gw_k)), gx_k

        init = (_jnp.zeros((), _jnp.float32), _jax.tree.map(_jnp.zeros_like, weights))
        (loss, grad_w), grad_x = _jax.lax.scan(body, init, (per_example, given["loss_target"]))
    with _jax.named_scope("update"):
        delta_w, new_m, new_v = {}, {}, {}
        for n in TWIN_WEIGHTS:
            delta_w[n], new_m[n], new_v[n] = _adamw(weights[n], grad_w[n], given["m_" + n], given["v_" + n])
    return (loss, grad_x, *[grad_w[n] for n in TWIN_WEIGHTS], *[delta_w[n] for n in TWIN_WEIGHTS],
            *[new_m[n] for n in TWIN_WEIGHTS], *[new_v[n] for n in TWIN_WEIGHTS])
```

```python
import functools
import math

import jax
import jax.numpy as jnp
from jax import lax
from jax.experimental import pallas as pl
from jax.experimental.pallas import tpu as pltpu

F32 = jnp.float32
BF16 = jnp.bfloat16
HIGHEST = lax.Precision.HIGHEST

N_DEV = 8
D_MODEL = 1024
EPS = 1e-6
SWA_GROUPS = ((128, 1), (512, 4), (2048, 16))
A_HEADS = 8
A_HEAD_DIM = 64
A_WIDTH = A_HEADS * A_HEAD_DIM
A_QKV = 3 * 3 * A_WIDTH
ROPE_DIM = 16
ROPE_HALF = 8
ROPE_THETA = 500000.0
BAND = 128
DN_HEADS = 8
DN_DIM = 128
DN_WIDTH = DN_HEADS * DN_DIM
CONV_W = 4
CHUNK = 64
D_FF = 4 * D_MODEL
PLE_DIM = 256
LR, B1, B2, ADAM_EPS, WD, STEP = 0.001, 0.9, 0.999, 1e-08, 0.01, 10

VMEM_LIMIT = 56 * 1024 * 1024
MXU_TILE = 1024
LANE = 128
SUBLANE = 8


def _cparams(sem):
    return pltpu.CompilerParams(dimension_semantics=sem, vmem_limit_bytes=VMEM_LIMIT)


def _tile(n, pref):
    if n <= pref:
        return n
    t = (pref // LANE) * LANE
    while t >= LANE:
        if n % t == 0:
            return t
        t -= LANE
    raise ValueError(f"no tile for {n}")


def _dot(a, b, ca=1, cb=0, precision=None):
    return lax.dot_general(a, b, (((ca,), (cb,)), ((), ())), precision=precision,
                           preferred_element_type=F32)


def _bdot(a, b, ca=1, cb=0):
    return _dot(a.astype(BF16), b.astype(BF16), ca, cb)


def _mm(name, a, b, *, ta=False, tb=False, epilogue=None, extras=(), out_dtypes=(F32,),
        tm_pref=MXU_TILE, tn_pref=1536, tk_pref=MXU_TILE):
    M, K = (a.shape[1], a.shape[0]) if ta else a.shape
    N = b.shape[0] if tb else b.shape[1]
    assert (b.shape[1] if tb else b.shape[0]) == K
    tm, tn, tk = _tile(M, tm_pref), _tile(N, tn_pref), _tile(K, tk_pref)
    nk = K // tk
    n_out = len(out_dtypes)
    n_ext = len(extras)

    def body(*refs):
        a_ref, b_ref = refs[0], refs[1]
        ext = refs[2:2 + n_ext]
        outs = refs[2 + n_ext:2 + n_ext + n_out]
        acc = refs[-1]
        k = pl.program_id(2)

        @pl.when(k == 0)
        def _():
            acc[...] = jnp.zeros_like(acc)

        acc[...] += _bdot(a_ref[...], b_ref[...], 0 if ta else 1, 1 if tb else 0)

        @pl.when(k == nk - 1)
        def _():
            r = acc[...]
            res = (r,) if epilogue is None else epilogue(r, *[e[...] for e in ext])
            for o, v in zip(outs, res):
                o[...] = v.astype(o.dtype)

    a_spec = pl.BlockSpec((tk, tm), lambda i, j, k: (k, i)) if ta else pl.BlockSpec((tm, tk), lambda i, j, k: (i, k))
    b_spec = pl.BlockSpec((tn, tk), lambda i, j, k: (j, k)) if tb else pl.BlockSpec((tk, tn), lambda i, j, k: (k, j))
    ext_specs = []
    for e in extras:
        if e.shape[0] == 1 and M != 1:
            ext_specs.append(pl.BlockSpec((1, tn), lambda i, j, k: (0, j)))
        else:
            ext_specs.append(pl.BlockSpec((tm, tn), lambda i, j, k: (i, j)))
    out = pl.pallas_call(
        body, name=name,
        grid=(M // tm, N // tn, nk),
        in_specs=[a_spec, b_spec] + ext_specs,
        out_specs=[pl.BlockSpec((tm, tn), lambda i, j, k: (i, j)) for _ in range(n_out)],
        out_shape=[jax.ShapeDtypeStruct((M, N), dt) for dt in out_dtypes],
        scratch_shapes=[pltpu.VMEM((tm, tn), F32)],
        compiler_params=_cparams(("parallel", "parallel", "arbitrary")),
    )(a, b, *extras)
    return out[0] if n_out == 1 else tuple(out)


def _rows(name, fn, ins, outs, *, tr, accs=()):
    ins = [(e[0], e[1]) + (e[2] if len(e) > 2 else (0, e[0].shape[-1])) for e in ins]
    n_rows = next(e[0].shape[0] for e in ins if e[1] == "row")
    assert n_rows % tr == 0 and tr % SUBLANE == 0
    steps = n_rows // tr
    t8 = tr // SUBLANE
    n8 = n_rows // SUBLANE
    n_in, n_out, n_acc = len(ins), len(outs), len(accs)

    def body(*refs):
        i = pl.program_id(0)
        vals = fn(i, steps, *[r[...] for r in refs[:n_in]])
        if not isinstance(vals, (tuple, list)):
            vals = (vals,)
        assert len(vals) == n_out + n_acc
        for o, v in zip(refs[n_in:n_in + n_out], vals[:n_out]):
            o[...] = v.astype(o.dtype)
        if n_acc:
            acc_refs = refs[n_in + n_out:]

            @pl.when(i == 0)
            def _():
                for r in acc_refs:
                    r[...] = jnp.zeros_like(r)

            for r, v in zip(acc_refs, vals[n_out:]):
                r[...] += v.astype(r.dtype)

    in_specs = []
    for a, kind, cb, c in ins:
        if kind == "row":
            in_specs.append(pl.BlockSpec((tr, c), lambda i, cb=cb: (i, cb)))
        elif kind == "full":
            in_specs.append(pl.BlockSpec(a.shape, lambda i, z=(0,) * a.ndim: z))
        elif kind == "prev8":
            in_specs.append(pl.BlockSpec((SUBLANE, c), lambda i, cb=cb: (jnp.maximum(i * t8 - 1, 0), cb)))
        elif kind == "next8":
            in_specs.append(pl.BlockSpec((SUBLANE, c), lambda i, cb=cb: (jnp.minimum((i + 1) * t8, n8 - 1), cb)))
        else:
            raise ValueError(kind)
    out_specs = [pl.BlockSpec((tr, c), lambda i: (i, 0)) for c, _ in outs]
    out_specs += [pl.BlockSpec(s, lambda i, z=(0,) * len(s): z) for s, _ in accs]
    out_shape = [jax.ShapeDtypeStruct((n_rows, c), dt) for c, dt in outs]
    out_shape += [jax.ShapeDtypeStruct(s, dt) for s, dt in accs]
    res = pl.pallas_call(
        body, name=name, grid=(steps,), in_specs=in_specs, out_specs=out_specs, out_shape=out_shape,
        compiler_params=_cparams(("arbitrary",) if n_acc else ("parallel",)),
    )(*[e[0] for e in ins])
    return res[0] if len(res) == 1 else tuple(res)


def _colsum(x):
    return jnp.sum(x, axis=0, keepdims=True)


def _sum_all(x):
    return jnp.sum(jnp.sum(x, axis=1, keepdims=True), axis=0, keepdims=True)


def _rmsnorm_fwd(name, x, gain):
    def fn(i, n, xt, g):
        r = lax.rsqrt(jnp.mean(xt * xt, axis=-1, keepdims=True) + EPS)
        return (xt * r * g,)
    return _rows(name, fn, [(x, "row"), (gain, "full")], [(x.shape[1], BF16)], tr=512)


def _rmsnorm_bwd(name, x, gain, dh, dres):
    def fn(i, n, xt, g, dht, drt):
        r = lax.rsqrt(jnp.mean(xt * xt, axis=-1, keepdims=True) + EPS)
        xh = xt * r
        dxn = dht * g
        dx = r * (dxn - xh * jnp.mean(dxn * xh, axis=-1, keepdims=True))
        return drt + dx, _colsum(dht * xh)
    D = x.shape[1]
    return _rows(name, fn, [(x, "row"), (gain, "full"), (dh, "row"), (dres, "row")], [(D, F32)],
                 tr=256, accs=[((1, D), F32)])


def _head_consts():
    import numpy as np
    e = np.arange(A_WIDTH) % A_HEAD_DIM
    inv = (np.float32(ROPE_THETA) ** (-np.arange(0, ROPE_DIM, 2, dtype=np.float32) / np.float32(ROPE_DIM))).astype(np.float32)
    c = np.zeros((8, A_WIDTH), np.float32)
    c[0] = np.where(e < ROPE_DIM, inv[e % ROPE_HALF], 0.0)
    c[1] = np.where(e < ROPE_HALF, -1.0, np.where(e < ROPE_DIM, 1.0, 0.0))
    c[2] = (e < ROPE_HALF).astype(np.float32)
    c[3] = (e < ROPE_DIM).astype(np.float32)
    return jnp.asarray(c)


def _block_diag(scale):
    import numpy as np
    h = np.arange(A_WIDTH) // A_HEAD_DIM
    return jnp.asarray((h[:, None] == h[None, :]).astype(np.float32) * scale, dtype=BF16)


def _seg_sum(x, bd):
    hi = x.astype(BF16)
    lo = (x - hi.astype(F32)).astype(BF16)
    return _dot(hi, bd) + _dot(lo, bd)


def _rope_tables(positions, consts):
    def fn(i, n, pos, c):
        ang = pos.astype(F32) * c[0:1, :]
        return jnp.cos(ang), jnp.sin(ang) * c[1:2, :]
    return _rows("rope_tables", fn, [(positions, "row"), (consts, "full")],
                 [(A_WIDTH, F32), (A_WIDTH, F32)], tr=512)


def _rope_apply(y, ct, st, low):
    rolled = jnp.where(low, pltpu.roll(y, A_WIDTH - ROPE_HALF, 1), pltpu.roll(y, ROPE_HALF, 1))
    return y * ct + rolled * st


def _rope_apply_bwd(dout, ct, st, low, in16):
    t = dout * st
    back = jnp.where(low, pltpu.roll(t, A_WIDTH - ROPE_HALF, 1), jnp.where(in16, pltpu.roll(t, ROPE_HALF, 1), 0.0))
    return dout * ct + back


def _attn_prep(qkv, gains, ct, st, consts, bd):
    def fn(i, n, t, g, c_t, s_t, c, b):
        low = c[2:3, :] > 0.5
        cols = []
        for grp in range(3):
            for which in range(3):
                off = (grp * 3 + which) * A_WIDTH
                x = t[:, off:off + A_WIDTH]
                if which == 2:
                    cols.append(x.astype(BF16))
                    continue
                r = lax.rsqrt(_seg_sum(x * x, b) + EPS)
                y = x * r * g[grp * 2 + which:grp * 2 + which + 1, :]
                cols.append(_rope_apply(y, c_t, s_t, low).astype(BF16))
        return (jnp.concatenate(cols, axis=1),)
    return _rows("attn_prep", fn, [(qkv, "row"), (gains, "full"), (ct, "row"), (st, "row"), (consts, "full"), (bd, "full")],
                 [(A_QKV, BF16)], tr=256)


def _band_mask(n):
    row = lax.broadcasted_iota(jnp.int32, (BAND, 2 * BAND), 0)
    col = lax.broadcasted_iota(jnp.int32, (BAND, 2 * BAND), 1)
    dist = row + BAND - col
    return (dist >= 0) & (dist <= BAND) & ((col >= BAND) | (n > 0))


def _attn_fwd(qkvn, grp):
    S = qkvn.shape[0]
    d = SWA_GROUPS[grp][1]
    L = S // d
    nblk = L // BAND
    assert L % BAND == 0
    view = qkvn.reshape(L, d * A_QKV)
    base = 3 * grp

    def body(q_ref, kc_ref, kp_ref, vc_ref, vp_ref, o_ref, lse_ref):
        n = pl.program_id(1)
        valid = _band_mask(n)
        first = lax.broadcasted_iota(jnp.int32, (BAND, LANE), 1) < A_HEAD_DIM
        o_cols, l_cols = [], []
        for pr in range(A_WIDTH // LANE):
            sl = slice(pr * LANE, (pr + 1) * LANE)
            qp = q_ref[:, sl]
            kcat = jnp.concatenate([kp_ref[:, sl], kc_ref[:, sl]], axis=0)
            vcat = jnp.concatenate([vp_ref[:, sl], vc_ref[:, sl]], axis=0)
            res = []
            for m in (first, jnp.logical_not(first)):
                s = _dot(jnp.where(m, qp, jnp.zeros_like(qp)), kcat, 1, 1) * (A_HEAD_DIM ** -0.5)
                s = jnp.where(valid, s, -1e30)
                mx = jnp.max(s, axis=-1, keepdims=True)
                e = jnp.exp(s - mx)
                l = jnp.sum(e, axis=-1, keepdims=True)
                res.append((_dot((e / l).astype(BF16), vcat), mx + jnp.log(l)))
            o_cols.append(jnp.where(first, res[0][0], res[1][0]))
            l_cols.append(jnp.where(first, res[0][1], res[1][1]))
        o_ref[...] = jnp.concatenate(o_cols, axis=1)
        lse_ref[...] = jnp.concatenate(l_cols, axis=1)

    blk = (BAND, A_WIDTH)
    o, lse = pl.pallas_call(
        body, name=f"attn_fwd_g{grp}", grid=(d, nblk),
        in_specs=[pl.BlockSpec(blk, lambda r, n: (n, r * 9 + base)),
                  pl.BlockSpec(blk, lambda r, n: (n, r * 9 + base + 1)),
                  pl.BlockSpec(blk, lambda r, n: (jnp.maximum(n - 1, 0), r * 9 + base + 1)),
                  pl.BlockSpec(blk, lambda r, n: (n, r * 9 + base + 2)),
                  pl.BlockSpec(blk, lambda r, n: (jnp.maximum(n - 1, 0), r * 9 + base + 2))],
        out_specs=[pl.BlockSpec(blk, lambda r, n: (n, r)), pl.BlockSpec(blk, lambda r, n: (n, r))],
        out_shape=[jax.ShapeDtypeStruct((L, d * A_WIDTH), F32)] * 2,
        compiler_params=_cparams(("parallel", "parallel")),
    )(view, view, view, view, view)
    return o.reshape(S, A_WIDTH), lse.reshape(S, A_WIDTH)


def _merge_weights(l0, l1, l2):
    mx = jnp.maximum(jnp.maximum(l0, l1), l2)
    e0, e1, e2 = jnp.exp(l0 - mx), jnp.exp(l1 - mx), jnp.exp(l2 - mx)
    inv = 1.0 / (e0 + e1 + e2)
    return e0 * inv, e1 * inv, e2 * inv


def _attn_merge(os_, lses):
    def fn(i, n, o0, o1, o2, l0, l1, l2):
        w0, w1, w2 = _merge_weights(l0, l1, l2)
        return (w0 * o0 + w1 * o1 + w2 * o2,)
    ins = [(a, "row") for a in (*os_, *lses)]
    return _rows("attn_merge", fn, ins, [(A_WIDTH, BF16)], tr=512)


def _attn_merge_bwd(do, os_, lses, bd1):
    def fn(i, n, dot_, o0, o1, o2, l0, l1, l2, b):
        w0, w1, w2 = _merge_weights(l0, l1, l2)
        o = w0 * o0 + w1 * o1 + w2 * o2
        dsum = _seg_sum(dot_ * o, b)
        return (w0 * dot_, w1 * dot_, w2 * dot_, -w0 * dsum, -w1 * dsum, -w2 * dsum)
    ins = [(do, "row")] + [(a, "row") for a in (*os_, *lses)] + [(bd1, "full")]
    res = _rows("attn_merge_bwd", fn, ins, [(A_WIDTH, BF16)] * 3 + [(A_WIDTH, F32)] * 3, tr=256)
    return res[:3], res[3:]


def _lane_pick(x, lane_idx, lane):
    return jnp.sum(jnp.where(lane_idx == lane, x, 0.0), axis=-1, keepdims=True)


def _attn_bwd(qkvn, grp, do_g, lse, c_g):
    S = qkvn.shape[0]
    d = SWA_GROUPS[grp][1]
    L = S // d
    nblk = L // BAND
    view = qkvn.reshape(L, d * A_QKV)
    base = 3 * grp
    dov, lsev, cv = (t.reshape(L, d * A_WIDTH) for t in (do_g, lse, c_g))

    def body(q_ref, kc_ref, kp_ref, vc_ref, vp_ref, do_ref, lse_ref, c_ref, dq_ref, dk_ref, dv_ref, ck, cv_):
        n = pl.program_id(1)

        @pl.when(n == 0)
        def _():
            ck[...] = jnp.zeros_like(ck)
            cv_[...] = jnp.zeros_like(cv_)

        @pl.when(n < nblk)
        def _():
            valid = _band_mask(n)
            lane = lax.broadcasted_iota(jnp.int32, (BAND, LANE), 1)
            first = lane < A_HEAD_DIM
            lane2 = lax.broadcasted_iota(jnp.int32, (2 * BAND, LANE), 1) < A_HEAD_DIM
            for pr in range(A_WIDTH // LANE):
                sl = slice(pr * LANE, (pr + 1) * LANE)
                qp = q_ref[:, sl]
                dop = do_ref[:, sl]
                kcat = jnp.concatenate([kp_ref[:, sl], kc_ref[:, sl]], axis=0)
                vcat = jnp.concatenate([vp_ref[:, sl], vc_ref[:, sl]], axis=0)
                lsep = lse_ref[:, sl]
                cp = c_ref[:, sl]
                res = []
                for hh, m in enumerate((first, jnp.logical_not(first))):
                    lse_h = _lane_pick(lsep, lane, hh * A_HEAD_DIM)
                    c_h = _lane_pick(cp, lane, hh * A_HEAD_DIM)
                    s = _dot(jnp.where(m, qp, jnp.zeros_like(qp)), kcat, 1, 1) * (A_HEAD_DIM ** -0.5)
                    p = jnp.where(valid, jnp.exp(s - lse_h), 0.0)
                    dp = _dot(jnp.where(m, dop, jnp.zeros_like(dop)), vcat, 1, 1)
                    ds = (p * (dp + c_h) * (A_HEAD_DIM ** -0.5)).astype(BF16)
                    pb = p.astype(BF16)
                    res.append((_dot(ds, kcat), _dot(ds, qp, 0, 0), _dot(pb, dop, 0, 0)))
                dq_ref[:, sl] = jnp.where(first, res[0][0], res[1][0])
                dkc = jnp.where(lane2, res[0][1], res[1][1])
                dvc = jnp.where(lane2, res[0][2], res[1][2])
                dk_ref[:, sl] = ck[:, sl] + dkc[:BAND]
                dv_ref[:, sl] = cv_[:, sl] + dvc[:BAND]
                ck[:, sl] = dkc[BAND:]
                cv_[:, sl] = dvc[BAND:]

        @pl.when(n == nblk)
        def _():
            dk_ref[...] = ck[...]
            dv_ref[...] = cv_[...]

    blk = (BAND, A_WIDTH)
    last = nblk - 1
    qn = lambda n: jnp.minimum(n, last)
    pn = lambda n: jnp.clip(n - 1, 0, last)
    dq, dk, dv = pl.pallas_call(
        body, name=f"attn_bwd_g{grp}", grid=(d, nblk + 1),
        in_specs=[pl.BlockSpec(blk, lambda r, n: (qn(n), r * 9 + base)),
                  pl.BlockSpec(blk, lambda r, n: (qn(n), r * 9 + base + 1)),
                  pl.BlockSpec(blk, lambda r, n: (pn(n), r * 9 + base + 1)),
                  pl.BlockSpec(blk, lambda r, n: (qn(n), r * 9 + base + 2)),
                  pl.BlockSpec(blk, lambda r, n: (pn(n), r * 9 + base + 2)),
                  pl.BlockSpec(blk, lambda r, n: (qn(n), r)),
                  pl.BlockSpec(blk, lambda r, n: (qn(n), r)),
                  pl.BlockSpec(blk, lambda r, n: (qn(n), r))],
        out_specs=[pl.BlockSpec(blk, lambda r, n: (qn(n), r)),
                   pl.BlockSpec(blk, lambda r, n: (pn(n), r)),
                   pl.BlockSpec(blk, lambda r, n: (pn(n), r))],
        out_shape=[jax.ShapeDtypeStruct((L, d * A_WIDTH), F32)] * 3,
        scratch_shapes=[pltpu.VMEM(blk, F32), pltpu.VMEM(blk, F32)],
        compiler_params=_cparams(("parallel", "arbitrary")),
    )(view, view, view, view, view, dov, lsev, cv)
    return tuple(t.reshape(S, A_WIDTH) for t in (dq, dk, dv))


def _attn_prep_bwd(qkv, grads, gains, ct, st, consts, bd):
    def fn(i, n, t, g, c_t, s_t, c, b, *gr):
        low = c[2:3, :] > 0.5
        in16 = c[3:4, :] > 0.5
        cols, dgs = [], []
        for grp in range(3):
            for which in range(3):
                dout = gr[grp * 3 + which]
                if which == 2:
                    cols.append(dout.astype(BF16))
                    continue
                off = (grp * 3 + which) * A_WIDTH
                x = t[:, off:off + A_WIDTH]
                gain = g[grp * 2 + which:grp * 2 + which + 1, :]
                r = lax.rsqrt(_seg_sum(x * x, b) + EPS)
                xh = x * r
                dy = _rope_apply_bwd(dout, c_t, s_t, low, in16)
                dyn = dy * gain
                dx = r * (dyn - xh * _seg_sum(dyn * xh, b))
                cols.append(dx.astype(BF16))
                dgs.append(_colsum(dy * xh))
        return (jnp.concatenate(cols, axis=1), *dgs)
    ins = [(qkv, "row"), (gains, "full"), (ct, "row"), (st, "row"), (consts, "full"), (bd, "full")] + [(a, "row") for a in grads]
    res = _rows("attn_prep_bwd", fn, ins, [(A_QKV, BF16)], tr=128, accs=[((1, A_WIDTH), F32)] * 6)
    return res[0], res[1:]


DN_QKV = 3 * DN_WIDTH
DN_QKVZ = DN_QKV + DN_WIDTH


def _sigmoid(x):
    return 1.0 / (1.0 + jnp.exp(-x))


def _softplus(x):
    return jnp.maximum(x, 0.0) + jnp.log(1.0 + jnp.exp(-jnp.abs(x)))


def _conv_taps(xs, w, tr):
    acc = None
    for j in range(CONV_W):
        sh = CONV_W - 1 - j
        term = (pltpu.roll(xs, sh, 0) if sh else xs)[SUBLANE:] * w[j:j + 1, :]
        acc = term if acc is None else acc + term
    return acc


def _dn_prep(qkvz, ab, convw, alog_row, dt_row):
    tr = 256

    def fn(i, n, x, xp, abt, w, al, dt):
        xp = jnp.where(i > 0, xp, 0.0)
        u = _conv_taps(jnp.concatenate([xp, x], axis=0), w, tr)
        y = u * _sigmoid(u)
        qs, ks = [], []
        for h in range(DN_HEADS):
            for dst, base, sc in ((qs, 0, DN_DIM ** -0.5), (ks, DN_WIDTH, 1.0)):
                seg = y[:, base + h * DN_DIM:base + (h + 1) * DN_DIM]
                dst.append(seg * (lax.rsqrt(jnp.sum(seg * seg, axis=-1, keepdims=True) + EPS) * sc))
        lane = lax.broadcasted_iota(jnp.int32, abt.shape, 1)
        g = -jnp.exp(al) * _softplus(abt + dt)
        gb = jnp.where(lane < DN_HEADS, g, jnp.where(lane < 2 * DN_HEADS, _sigmoid(abt), 0.0))
        return u, jnp.concatenate(qs, axis=1), jnp.concatenate(ks, axis=1), y[:, 2 * DN_WIDTH:], gb

    ins = [(qkvz, "row", (0, DN_QKV)), (qkvz, "prev8", (0, DN_QKV)), (ab, "row"), (convw, "full"),
           (alog_row, "full"), (dt_row, "full")]
    return _rows("dn_prep", fn, ins, [(DN_QKV, F32), (DN_WIDTH, F32), (DN_WIDTH, F32), (DN_WIDTH, F32), (LANE, F32)], tr=tr)


def _tri_masks():
    row = lax.broadcasted_iota(jnp.int32, (CHUNK, CHUNK), 0)
    col = lax.broadcasted_iota(jnp.int32, (CHUNK, CHUNK), 1)
    return row >= col, row > col, row == col


def _unit_lower_inverse(a, eye):
    t = eye - a
    ak = a
    for _ in range(5):
        ak = _dot(ak, ak, precision=HIGHEST)
        t = t + _dot(t, ak, precision=HIGHEST)
    return t


def _dn_gates(gb):
    lower, _, _ = _tri_masks()
    gc = _dot(lower.astype(F32), gb, precision=HIGHEST)
    return gc, jnp.transpose(gc)


def _dn_head_terms(q, k, v, gb, gc, gct, h, lane):
    lower, strict, diag = _tri_masks()
    bcol = _lane_pick(gb, lane, DN_HEADS + h)
    gcol = _lane_pick(gc, lane, h)
    grow = gct[h:h + 1, :]
    is_last = lax.broadcasted_iota(jnp.int32, (CHUNK, 1), 0) == CHUNK - 1
    glast = jnp.sum(jnp.where(is_last, gcol, 0.0), axis=0, keepdims=True)
    decay = jnp.exp(jnp.where(lower, gcol - grow, -1e30))
    kb = k * bcol
    a = jnp.where(strict, _bdot(kb, k, 1, 1) * decay, 0.0)
    t = _unit_lower_inverse(a, diag.astype(F32))
    eg = jnp.exp(gcol)
    egl = jnp.exp(glast - gcol)
    rhs_u = v * bcol
    rhs_w = kb * eg
    u = _dot(t, rhs_u, precision=HIGHEST)
    w = _dot(t, rhs_w, precision=HIGHEST)
    qk = _bdot(q, k, 1, 1)
    return dict(bcol=bcol, decay=decay, kb=kb, a=a, t=t, eg=eg, egl=egl, rhs_w=rhs_w, u=u, w=w, qk=qk,
                attn=qk * decay, q_dec=q * eg, k_dec=k * egl, c_dec=jnp.exp(glast), lower=lower, strict=strict)


def _dn_chunk_fwd(q, k, v, gb):
    S = q.shape[0]
    N = S // CHUNK

    def body(q_ref, k_ref, v_ref, gb_ref, o_ref, st_ref, state):
        @pl.when(pl.program_id(0) == 0)
        def _():
            state[...] = jnp.zeros_like(state)

        gbt = gb_ref[...]
        gc, gct = _dn_gates(gbt)
        lane = lax.broadcasted_iota(jnp.int32, (CHUNK, LANE), 1)
        for h in range(DN_HEADS):
            sl = slice(h * DN_DIM, (h + 1) * DN_DIM)
            f = _dn_head_terms(q_ref[:, sl], k_ref[:, sl], v_ref[:, sl], gbt, gc, gct, h, lane)
            s = state[h]
            st_ref[0, h] = s
            v_new = f["u"] - _bdot(f["w"], s)
            o_ref[:, sl] = _bdot(f["q_dec"], s) + _bdot(f["attn"], v_new)
            state[h] = s * f["c_dec"] + _bdot(f["k_dec"], v_new, 0, 0)

    blk = pl.BlockSpec((CHUNK, DN_WIDTH), lambda n: (n, 0))
    st_blk = pl.BlockSpec((1, DN_HEADS, DN_DIM, DN_DIM), lambda n: (n, 0, 0, 0))
    return pl.pallas_call(
        body, name="dn_chunk_fwd", grid=(N,),
        in_specs=[blk, blk, blk, pl.BlockSpec((CHUNK, LANE), lambda n: (n, 0))],
        out_specs=[blk, st_blk],
        out_shape=[jax.ShapeDtypeStruct((S, DN_WIDTH), F32), jax.ShapeDtypeStruct((N, DN_HEADS, DN_DIM, DN_DIM), F32)],
        scratch_shapes=[pltpu.VMEM((DN_HEADS, DN_DIM, DN_DIM), F32)],
        compiler_params=_cparams(("arbitrary",)),
    )(q, k, v, gb)


def _dn_chunk_bwd(q, k, v, gb, states, do):
    S = q.shape[0]
    N = S // CHUNK

    def body(q_ref, k_ref, v_ref, gb_ref, st_ref, do_ref, dq_ref, dk_ref, dv_ref, dgb_ref, dstate):
        @pl.when(pl.program_id(0) == 0)
        def _():
            dstate[...] = jnp.zeros_like(dstate)

        gbt = gb_ref[...]
        gc, gct = _dn_gates(gbt)
        lane = lax.broadcasted_iota(jnp.int32, (CHUNK, LANE), 1)
        ones = jnp.ones((CHUNK, LANE), F32)
        is_last = lax.broadcasted_iota(jnp.int32, (CHUNK, 1), 0) == CHUNK - 1
        dgc_all = jnp.zeros((CHUNK, LANE), F32)
        dbeta_all = jnp.zeros((CHUNK, LANE), F32)
        for h in range(DN_HEADS):
            sl = slice(h * DN_DIM, (h + 1) * DN_DIM)
            qh, kh, vh, doh = q_ref[:, sl], k_ref[:, sl], v_ref[:, sl], do_ref[:, sl]
            f = _dn_head_terms(qh, kh, vh, gbt, gc, gct, h, lane)
            s = st_ref[0, h]
            dsn = dstate[h]
            v_new = f["u"] - _bdot(f["w"], s)
            dv_new = _bdot(f["attn"], doh, 0, 0) + _bdot(f["k_dec"], dsn)
            dattn = jnp.where(f["lower"], _bdot(doh, v_new, 1, 1), 0.0)
            dq_dec = _bdot(doh, s, 1, 1)
            dk_dec = _bdot(v_new, dsn, 1, 1)
            dc_dec = _sum_all(dsn * s)
            dstate[h] = dsn * f["c_dec"] + _bdot(f["q_dec"], doh, 0, 0) - _bdot(f["w"], dv_new, 0, 0)
            dw = -_bdot(dv_new, s, 1, 1)
            drhs_u = _dot(f["t"], dv_new, 0, 0, precision=HIGHEST)
            drhs_w = _dot(f["t"], dw, 0, 0, precision=HIGHEST)
            da = jnp.where(f["strict"], -(_bdot(drhs_u, f["u"], 1, 1) + _bdot(drhs_w, f["w"], 1, 1)), 0.0)
            dkk = da * f["decay"]
            m = da * f["a"] + dattn * f["attn"]
            dkb = _bdot(dkk, kh) + drhs_w * f["eg"]
            dqk = dattn * f["decay"]
            dq_ref[:, sl] = _bdot(dqk, kh) + dq_dec * f["eg"]
            dk_ref[:, sl] = _bdot(dkk, f["kb"], 0, 0) + _bdot(dqk, qh, 0, 0) + dk_dec * f["egl"] + dkb * f["bcol"]
            dv_ref[:, sl] = drhs_u * f["bcol"]
            kdec_term = jnp.sum(dk_dec * f["k_dec"], axis=-1, keepdims=True)
            col_m = _dot(m, ones, 0, 0, precision=HIGHEST)[:, 0:1]
            dgc = (jnp.sum(m, axis=-1, keepdims=True) - col_m
                   + jnp.sum(dq_dec * f["q_dec"], axis=-1, keepdims=True) - kdec_term
                   + jnp.sum(drhs_w * f["rhs_w"], axis=-1, keepdims=True))
            last_extra = jnp.sum(kdec_term, axis=0, keepdims=True) + dc_dec * f["c_dec"]
            dgc = dgc + jnp.where(is_last, last_extra, 0.0)
            dbeta = jnp.sum(drhs_u * vh, axis=-1, keepdims=True) + jnp.sum(dkb * kh, axis=-1, keepdims=True)
            dgc_all = jnp.where(lane == h, dgc, dgc_all)
            dbeta_all = jnp.where(lane == DN_HEADS + h, dbeta, dbeta_all)
        dg_all = _dot(_tri_masks()[0].astype(F32), dgc_all, 0, 0, precision=HIGHEST)
        dgb_ref[...] = jnp.where(lane < DN_HEADS, dg_all, dbeta_all)

    rev = lambda n: (N - 1 - n, 0)
    blk = pl.BlockSpec((CHUNK, DN_WIDTH), rev)
    gblk = pl.BlockSpec((CHUNK, LANE), rev)
    st_blk = pl.BlockSpec((1, DN_HEADS, DN_DIM, DN_DIM), lambda n: (N - 1 - n, 0, 0, 0))
    return pl.pallas_call(
        body, name="dn_chunk_bwd", grid=(N,),
        in_specs=[blk, blk, blk, gblk, st_blk, blk],
        out_specs=[blk, blk, blk, gblk],
        out_shape=[jax.ShapeDtypeStruct((S, DN_WIDTH), F32)] * 3 + [jax.ShapeDtypeStruct((S, LANE), F32)],
        scratch_shapes=[pltpu.VMEM((DN_HEADS, DN_DIM, DN_DIM), F32)],
        compiler_params=_cparams(("arbitrary",)),
    )(q, k, v, gb, states, do)


def _dn_post(o, qkvz, gain_row):
    def fn(i, n, ot, z, g):
        cols = []
        for h in range(DN_HEADS):
            seg = ot[:, h * DN_DIM:(h + 1) * DN_DIM]
            cols.append(seg * lax.rsqrt(jnp.mean(seg * seg, axis=-1, keepdims=True) + EPS) * g)
        return (jnp.concatenate(cols, axis=1) * (z * _sigmoid(z)),)
    return _rows("dn_post", fn, [(o, "row"), (qkvz, "row", (3, DN_WIDTH)), (gain_row, "full")], [(DN_WIDTH, BF16)], tr=512)


def _dn_post_bwd(don, o, qkvz, gain_row):
    def fn(i, n, dy, ot, z, g):
        sg = _sigmoid(z)
        sz = z * sg
        dos, ohs = [], []
        dg = jnp.zeros((1, DN_DIM), F32)
        for h in range(DN_HEADS):
            sl = slice(h * DN_DIM, (h + 1) * DN_DIM)
            seg = ot[:, sl]
            r = lax.rsqrt(jnp.mean(seg * seg, axis=-1, keepdims=True) + EPS)
            oh = seg * r
            dno = dy[:, sl] * sz[:, sl]
            dg = dg + _colsum(dno * oh)
            dn = dno * g
            dos.append(r * (dn - oh * jnp.mean(dn * oh, axis=-1, keepdims=True)))
            ohs.append(oh * g)
        dz = dy * jnp.concatenate(ohs, axis=1) * (sg * (1.0 + z * (1.0 - sg)))
        return jnp.concatenate(dos, axis=1), dz, dg
    ins = [(don, "row"), (o, "row"), (qkvz, "row", (3, DN_WIDTH)), (gain_row, "full")]
    return _rows("dn_post_bwd", fn, ins, [(DN_WIDTH, F32), (DN_WIDTH, F32)], tr=256, accs=[((1, DN_DIM), F32)])


def _dn_prep_bwd(dq, dk, dv, dgb, u, ab, alog_row, dt_row):
    def fn(i, n, dqt, dkt, dvt, dgbt, ut, abt, al, dt):
        sg = _sigmoid(ut)
        y = ut * sg
        dys = []
        for grad, base, sc in ((dqt, 0, DN_DIM ** -0.5), (dkt, DN_WIDTH, 1.0)):
            for h in range(DN_HEADS):
                seg = y[:, base + h * DN_DIM:base + (h + 1) * DN_DIM]
                gr = grad[:, h * DN_DIM:(h + 1) * DN_DIM]
                r = lax.rsqrt(jnp.sum(seg * seg, axis=-1, keepdims=True) + EPS)
                xh = seg * r
                dys.append((r * sc) * (gr - xh * jnp.sum(gr * xh, axis=-1, keepdims=True)))
        dy = jnp.concatenate(dys + [dvt], axis=1)
        du = dy * (sg * (1.0 + ut * (1.0 - sg)))
        lane = lax.broadcasted_iota(jnp.int32, abt.shape, 1)
        is_g = lane < DN_HEADS
        ea = jnp.exp(al)
        x = abt + dt
        slope = -ea * _sigmoid(x)
        gval = -ea * _softplus(x)
        dg = jnp.where(is_g, dgbt, 0.0)
        beta = _sigmoid(abt)
        dab = jnp.where(is_g, dg * slope, jnp.where(lane < 2 * DN_HEADS, dgbt * beta * (1.0 - beta), 0.0))
        return du, dab, _colsum(dg * gval), _colsum(dg * slope)
    ins = [(dq, "row"), (dk, "row"), (dv, "row"), (dgb, "row"), (u, "row"), (ab, "row"), (alog_row, "full"), (dt_row, "full")]
    return _rows("dn_prep_bwd", fn, ins, [(DN_QKV, F32), (LANE, BF16)], tr=256, accs=[((1, LANE), F32)] * 2)


def _dn_conv_bwd(du, dz, qkvz, convw):
    tr = 256

    def fn(i, n, dut, dun, dzt, x, xp, w):
        dun = jnp.where(i < n - 1, dun, 0.0)
        dus = jnp.concatenate([dut, dun], axis=0)
        xs = jnp.concatenate([jnp.where(i > 0, xp, 0.0), x], axis=0)
        dx = None
        dws = []
        for j in range(CONV_W):
            sh = CONV_W - 1 - j
            term = (pltpu.roll(dus, tr + SUBLANE - sh, 0) if sh else dus)[:tr] * w[j:j + 1, :]
            dx = term if dx is None else dx + term
            dws.append(_colsum(dut * (pltpu.roll(xs, sh, 0) if sh else xs)[SUBLANE:]))
        return (jnp.concatenate([dx.astype(BF16), dzt.astype(BF16)], axis=1), *dws)

    ins = [(du, "row"), (du, "next8"), (dz, "row"), (qkvz, "row", (0, DN_QKV)), (qkvz, "prev8", (0, DN_QKV)), (convw, "full")]
    res = _rows("dn_conv_bwd", fn, ins, [(DN_QKVZ, BF16)], tr=tr, accs=[((1, DN_QKV), F32)] * CONV_W)
    return res[0], res[1:]


def _add(acc, r):
    return (r + acc,)


def _mlp_ple_fwd(i, x1, p_i, mlp_gain, ple_gain, w_up, w_down, w_ple, w_gate):
    hm = _rmsnorm_fwd(f"mlp_norm{i}", x1, mlp_gain)
    u, a = _mm(f"mlp_up{i}", hm, w_up, epilogue=lambda acc: (acc, jnp.square(jnp.maximum(acc, 0.0))),
               out_dtypes=(F32, BF16))
    x2 = _mm(f"mlp_down{i}", a, w_down, epilogue=_add, extras=(x1,))
    hp = _rmsnorm_fwd(f"ple_norm{i}", x2, ple_gain)
    pp = _mm(f"ple_proj{i}", p_i, w_ple)

    def gate_epilogue(acc, x2t, ppt):
        gate = _sigmoid(acc)
        return x2t + ppt * gate, gate

    x3, gate = _mm(f"ple_gate{i}", hp, w_gate, epilogue=gate_epilogue, extras=(x2, pp), out_dtypes=(F32, F32))
    return x3, dict(x1=x1, hm=hm, u=u, a=a, x2=x2, hp=hp, pp=pp, gate=gate, p=p_i)


def _mlp_ple_bwd(i, dx3, sv, mlp_gain, ple_gain, w_up, w_down, w_gate):
    def fn(_i, _n, d, g, pp):
        return d * g, d * pp * g * (1.0 - g)
    dpp, dzg = _rows(f"ple_gate_bwd{i}", fn, [(dx3, "row"), (sv["gate"], "row"), (sv["pp"], "row")],
                     [(D_MODEL, BF16), (D_MODEL, BF16)], tr=512)
    d_w_ple = _mm(f"ple_proj_dw{i}", sv["p"], dpp, ta=True, out_dtypes=(BF16,))
    d_w_gate = _mm(f"ple_gate_dw{i}", sv["hp"], dzg, ta=True, out_dtypes=(BF16,))
    dhp = _mm(f"ple_gate_dx{i}", dzg, w_gate, tb=True)
    dx2, d_ple_gain = _rmsnorm_bwd(f"ple_norm_bwd{i}", sv["x2"], ple_gain, dhp, dx3)
    d_w_down = _mm(f"mlp_down_dw{i}", sv["a"], dx2, ta=True, out_dtypes=(BF16,))
    du = _mm(f"mlp_down_dx{i}", dx2, w_down, tb=True, epilogue=lambda acc, ut: (acc * (2.0 * jnp.maximum(ut, 0.0)),),
             extras=(sv["u"],), out_dtypes=(BF16,))
    d_w_up = _mm(f"mlp_up_dw{i}", sv["hm"], du, ta=True, out_dtypes=(BF16,))
    dhm = _mm(f"mlp_up_dx{i}", du, w_up, tb=True)
    dx1, d_mlp_gain = _rmsnorm_bwd(f"mlp_norm_bwd{i}", sv["x1"], mlp_gain, dhm, dx2)
    return dx1, dict(w_ple=d_w_ple, w_ple_gate=d_w_gate, w_down=d_w_down, w_up=d_w_up,
                     ple_norm=d_ple_gain, mlp_norm=d_mlp_gain)


def _loss_fwd_bwd(y, target):
    D = y.shape[1]

    def fn(i, n, yt, tt):
        e = yt - tt
        return e * (1.0 / D), _colsum(e * e)
    dy, sq = _rows("loss", fn, [(y, "row"), (target, "row")], [(D, F32)], tr=512, accs=[((1, D), F32)])
    return sq, dy


def _local_step(x, p, positions, target, W, P):
    consts = _head_consts()
    bd = _block_diag(1.0 / A_HEAD_DIM)
    bd1 = _block_diag(1.0)
    ct, st = _rope_tables(positions, consts)
    gains = jnp.stack([jnp.tile(v, A_HEADS) for g in range(3) for v in (P["attn_q_gain"][g], P["attn_k_gain"][g])])
    pad = LANE - DN_HEADS
    alog_row = jnp.pad(P["dn_a_log"].reshape(1, DN_HEADS), ((0, 0), (0, pad)))
    dt_row = jnp.pad(P["dn_dt_bias"].reshape(1, DN_HEADS), ((0, 0), (0, pad)))
    ogain_row = P["dn_o_gain"].reshape(1, DN_DIM)
    row = lambda name, i: P[name][i:i + 1]

    h0 = _rmsnorm_fwd("mix_norm0", x, row("mix_norm", 0))
    qkv = _mm("attn_qkv", h0, W["attn_w_qkv"])
    qkvn = _attn_prep(qkv, gains, ct, st, consts, bd)
    os_, lses = zip(*[_attn_fwd(qkvn, g) for g in range(3)])
    o_attn = _attn_merge(os_, lses)
    x1 = _mm("attn_out", o_attn, W["attn_w_o"], epilogue=_add, extras=(x,))
    x3, sv0 = _mlp_ple_fwd(0, x1, p[0], row("mlp_norm", 0), row("ple_norm", 0),
                           W["w_up"][0], W["w_down"][0], W["w_ple"][0], W["w_ple_gate"][0])
    h1 = _rmsnorm_fwd("mix_norm1", x3, row("mix_norm", 1))
    qkvz = _mm("dn_in_qkvz", h1, W["dn_w_qkvz"])
    ab = _mm("dn_in_ab", h1, W["dn_w_ab"])
    u, q, k, v, gb = _dn_prep(qkvz, ab, W["dn_conv"], alog_row, dt_row)
    o_dn, states = _dn_chunk_fwd(q, k, v, gb)
    on = _dn_post(o_dn, qkvz, ogain_row)
    x4 = _mm("dn_out", on, W["dn_w_o"], epilogue=_add, extras=(x3,))
    x6, sv1 = _mlp_ple_fwd(1, x4, p[1], row("mlp_norm", 1), row("ple_norm", 1),
                           W["w_up"][1], W["w_down"][1], W["w_ple"][1], W["w_ple_gate"][1])
    sq, dy = _loss_fwd_bwd(x6, target)

    dx4, g1 = _mlp_ple_bwd(1, dy, sv1, row("mlp_norm", 1), row("ple_norm", 1),
                           W["w_up"][1], W["w_down"][1], W["w_ple_gate"][1])
    don = _mm("dn_out_dx", dx4, W["dn_w_o"], tb=True)
    d_dn_w_o = _mm("dn_out_dw", on, dx4, ta=True, out_dtypes=(BF16,))
    do_dn, dz, d_ogain = _dn_post_bwd(don, o_dn, qkvz, ogain_row)
    dq, dk, dv, dgb = _dn_chunk_bwd(q, k, v, gb, states, do_dn)
    du, dab, d_alog, d_dt = _dn_prep_bwd(dq, dk, dv, dgb, u, ab, alog_row, dt_row)
    dqkvz, d_conv = _dn_conv_bwd(du, dz, qkvz, W["dn_conv"])
    dh1 = _mm("dn_in_qkvz_dx", dqkvz, W["dn_w_qkvz"], tb=True)
    dh1 = _mm("dn_in_ab_dx", dab, W["dn_w_ab"], tb=True, epilogue=_add, extras=(dh1,))
    d_w_qkvz = _mm("dn_in_qkvz_dw", h1, dqkvz, ta=True, out_dtypes=(BF16,))
    d_w_ab = _mm("dn_in_ab_dw", h1, dab, ta=True, out_dtypes=(BF16,))
    dx3, d_mix1 = _rmsnorm_bwd("mix_norm_bwd1", x3, row("mix_norm", 1), dh1, dx4)
    dx1, g0 = _mlp_ple_bwd(0, dx3, sv0, row("mlp_norm", 0), row("ple_norm", 0),
                           W["w_up"][0], W["w_down"][0], W["w_ple_gate"][0])
    do_attn = _mm("attn_out_dx", dx1, W["attn_w_o"], tb=True)
    d_attn_w_o = _mm("attn_out_dw", o_attn, dx1, ta=True, out_dtypes=(BF16,))
    dos, cs = _attn_merge_bwd(do_attn, os_, lses, bd1)
    grads9 = []
    for g in range(3):
        grads9 += list(_attn_bwd(qkvn, g, dos[g], lses[g], cs[g]))
    dqkv, dgains = _attn_prep_bwd(qkv, grads9, gains, ct, st, consts, bd)
    dh0 = _mm("attn_qkv_dx", dqkv, W["attn_w_qkv"], tb=True)
    d_attn_w_qkv = _mm("attn_qkv_dw", h0, dqkv, ta=True, out_dtypes=(BF16,))
    dx0, d_mix0 = _rmsnorm_bwd("mix_norm_bwd0", x, row("mix_norm", 0), dh0, dx1)

    dg = jnp.stack([t.reshape(A_HEADS, A_HEAD_DIM).sum(0) for t in dgains])
    d_w_in = jnp.concatenate([d_w_qkvz[:, :DN_QKV], d_w_ab[:, :2 * DN_HEADS], d_w_qkvz[:, DN_QKV:]], axis=1)
    grads = dict(
        mix_norm=jnp.concatenate([d_mix0, d_mix1], 0),
        attn_w_qkv=d_attn_w_qkv[None], attn_q_gain=dg[0::2][None], attn_k_gain=dg[1::2][None],
        attn_w_o=d_attn_w_o[None], dn_w_in=d_w_in[None],
        dn_conv=jnp.concatenate(d_conv, 0)[None],
        dn_a_log=d_alog[:, :DN_HEADS], dn_dt_bias=d_dt[:, :DN_HEADS], dn_o_gain=d_ogain, dn_w_o=d_dn_w_o[None],
        mlp_norm=jnp.concatenate([g0["mlp_norm"], g1["mlp_norm"]], 0),
        w_up=jnp.stack([g0["w_up"], g1["w_up"]]), w_down=jnp.stack([g0["w_down"], g1["w_down"]]),
        ple_norm=jnp.concatenate([g0["ple_norm"], g1["ple_norm"]], 0),
        w_ple=jnp.stack([g0["w_ple"], g1["w_ple"]]), w_ple_gate=jnp.stack([g0["w_ple_gate"], g1["w_ple_gate"]]),
    )
    return sq, dx0, grads


MESH_IDS = pl.DeviceIdType.MESH
ANY = pl.BlockSpec(memory_space=pl.ANY)


def _place():
    return lax.axis_index("x"), lax.axis_index("y"), lax.axis_index("c")


def _all_gather(name, buf):
    R, C = buf.shape

    def body(x_ref, out_ref, send_sems, recv_sems, local_sem):
        x, y, c = _place()
        me, sibling = (x, y, c), (x, y, 1 - c)
        chips = [(1 - x, y), (x, 1 - y), (1 - x, 1 - y)]

        def slot(px, py, pc):
            return out_ref.at[4 * px + 2 * py + pc]

        def copy(k, block, to, src=None):
            return pltpu.make_async_remote_copy(
                src_ref=slot(*block) if src is None else src, dst_ref=slot(*block),
                send_sem=send_sems.at[k], recv_sem=recv_sems.at[k], device_id=to, device_id_type=MESH_IDS)

        mine = pltpu.make_async_copy(x_ref, slot(*me), local_sem)
        mine.start()
        first = [copy(0, me, sibling, src=x_ref)]
        first += [copy(1 + j, me, (*chip, c), src=x_ref) for j, chip in enumerate(chips)]
        for cp in first:
            cp.start()
        passed = [copy(4 + j, (*chip, c), sibling) for j, chip in enumerate(chips)]
        for j, chip in enumerate(chips):
            copy(1 + j, (*chip, c), me).wait_recv()
            passed[j].start()
        copy(0, sibling, me).wait_recv()
        for j, chip in enumerate(chips):
            copy(4 + j, (*chip, 1 - c), me).wait_recv()
        for cp in first + passed:
            cp.wait_send()
        mine.wait()

    return pl.pallas_call(
        body, name=name, out_shape=jax.ShapeDtypeStruct((N_DEV, R, C), buf.dtype),
        in_specs=[ANY], out_specs=ANY,
        scratch_shapes=[pltpu.SemaphoreType.DMA((7,)), pltpu.SemaphoreType.DMA((7,)), pltpu.SemaphoreType.DMA],
    )(buf)


def _all_to_all(name, send):
    _, R, C = send.shape

    def body(send_ref, recv_ref, send_sems, recv_sems, local_sem):
        x, y, c = _place()
        me = 4 * x + 2 * y + c
        mine = pltpu.make_async_copy(send_ref.at[me], recv_ref.at[me], local_sem)
        mine.start()
        copies, arrivals = [], []
        for k in range(1, N_DEV):
            px = 1 - x if k & 4 else x
            py = 1 - y if k & 2 else y
            pc = 1 - c if k & 1 else c
            peer = 4 * px + 2 * py + pc
            copies.append(pltpu.make_async_remote_copy(
                src_ref=send_ref.at[peer], dst_ref=recv_ref.at[me], send_sem=send_sems.at[k - 1],
                recv_sem=recv_sems.at[k - 1], device_id=(px, py, pc), device_id_type=MESH_IDS))
            arrivals.append(pltpu.make_async_remote_copy(
                src_ref=send_ref.at[peer], dst_ref=recv_ref.at[peer], send_sem=send_sems.at[k - 1],
                recv_sem=recv_sems.at[k - 1], device_id=(px, py, pc), device_id_type=MESH_IDS))
        for cp in copies:
            cp.start()
        for cp in arrivals:
            cp.wait_recv()
        for cp in copies:
            cp.wait_send()
        mine.wait()

    return pl.pallas_call(
        body, name=name, out_shape=jax.ShapeDtypeStruct(send.shape, send.dtype),
        in_specs=[ANY], out_specs=ANY,
        scratch_shapes=[pltpu.SemaphoreType.DMA((7,)), pltpu.SemaphoreType.DMA((7,)), pltpu.SemaphoreType.DMA],
    )(send)


def _adamw(name, parts, w, m, v, tr):
    R = w.shape[0]
    assert R % tr == 0
    c1 = 1.0 - B1 ** STEP
    c2 = 1.0 - B2 ** STEP

    def body(p_ref, w_ref, m_ref, v_ref, g_ref, d_ref, nm_ref, nv_ref):
        g = p_ref[0].astype(F32)
        for dev in range(1, N_DEV):
            g = g + p_ref[dev].astype(F32)
        nm = B1 * m_ref[...] + (1.0 - B1) * g
        nv = B2 * v_ref[...] + (1.0 - B2) * jnp.square(g)
        g_ref[...] = g
        nm_ref[...] = nm
        nv_ref[...] = nv
        d_ref[...] = -LR * ((nm / c1) / (jnp.sqrt(nv / c2) + ADAM_EPS) + WD * w_ref[...])

    blk = pl.BlockSpec((tr, LANE), lambda i: (i, 0))
    return pl.pallas_call(
        body, name=name, grid=(R // tr,),
        in_specs=[pl.BlockSpec((N_DEV, tr, LANE), lambda i: (0, i, 0)), blk, blk, blk],
        out_specs=[blk] * 4, out_shape=[jax.ShapeDtypeStruct((R, LANE), F32)] * 4,
        compiler_params=_cparams(("parallel",)),
    )(parts, w, m, v)


BIG = (("attn_w_qkv", 2), ("attn_w_o", 2), ("dn_w_in", 2), ("dn_conv", 2), ("dn_w_o", 1),
       ("w_up", 2), ("w_down", 1), ("w_ple", 2), ("w_ple_gate", 1))
SMALL = ("mix_norm", "attn_q_gain", "attn_k_gain", "dn_a_log", "dn_dt_bias", "dn_o_gain", "mlp_norm", "ple_norm")
WEIGHTS = ("mix_norm", "attn_w_qkv", "attn_q_gain", "attn_k_gain", "attn_w_o", "dn_w_in", "dn_conv", "dn_a_log",
           "dn_dt_bias", "dn_o_gain", "dn_w_o", "mlp_norm", "w_up", "w_down", "ple_norm", "w_ple", "w_ple_gate")
FLAT_TILE = 512


def _to_rows(flat, multiple):
    n = flat.shape[-1]
    rows = -(-n // (LANE * multiple)) * multiple
    padw = [(0, 0)] * (flat.ndim - 1) + [(0, rows * LANE - n)]
    return jnp.pad(flat, padw).reshape(flat.shape[:-1] + (rows, LANE))


def _pack_shards(shards):
    return _to_rows(jnp.concatenate([shards[n].reshape(-1) for n, _ in BIG]), FLAT_TILE)


def _unpack_shards(buf, like):
    out, off = {}, 0
    flat = buf.reshape(-1)
    for n, _ in BIG:
        sz = math.prod(like[n].shape)
        out[n] = flat[off:off + sz].reshape(like[n].shape)
        off += sz
    return out


def _split_for_devices(full, axis):
    s = full.shape
    t = full.reshape(s[:axis] + (N_DEV, s[axis] // N_DEV) + s[axis + 1:])
    return jnp.moveaxis(t, axis, 0).reshape(N_DEV, -1)


def _join_from_devices(gathered, shard_shape, axis):
    t = jnp.moveaxis(gathered.reshape((N_DEV,) + tuple(shard_shape)), 0, axis)
    s = tuple(shard_shape)
    return t.reshape(s[:axis] + (N_DEV * s[axis],) + s[axis + 1:])


SMALL_ROWS = 56


def _pack_small(vals, loss_row):
    rows = []
    for n in SMALL:
        rows.append(_to_rows(vals[n].reshape(-1), 1))
    rows.append(loss_row)
    buf = jnp.concatenate(rows, 0)
    assert buf.shape == (SMALL_ROWS, LANE)
    return buf


def _unpack_small(buf, like):
    out, r = {}, 0
    for n in SMALL:
        sz = math.prod(like[n].shape)
        nr = -(-sz // LANE)
        out[n] = buf[r:r + nr].reshape(-1)[:sz].reshape(like[n].shape)
        r += nr
    return out, buf[r]


def kernel(x, p, positions, mix_norm, attn_w_qkv, attn_q_gain, attn_k_gain, attn_w_o, dn_w_in, dn_conv, dn_a_log, dn_dt_bias, dn_o_gain, dn_w_o, mlp_norm, w_up, w_down, ple_norm, w_ple, w_ple_gate, loss_target, m_mix_norm, m_attn_w_qkv, m_attn_q_gain, m_attn_k_gain, m_attn_w_o, m_dn_w_in, m_dn_conv, m_dn_a_log, m_dn_dt_bias, m_dn_o_gain, m_dn_w_o, m_mlp_norm, m_w_up, m_w_down, m_ple_norm, m_w_ple, m_w_ple_gate, v_mix_norm, v_attn_w_qkv, v_attn_q_gain, v_attn_k_gain, v_attn_w_o, v_dn_w_in, v_dn_conv, v_dn_a_log, v_dn_dt_bias, v_dn_o_gain, v_dn_w_o, v_mlp_norm, v_w_up, v_w_down, v_ple_norm, v_w_ple, v_w_ple_gate):
    w = dict(mix_norm=mix_norm, attn_w_qkv=attn_w_qkv, attn_q_gain=attn_q_gain, attn_k_gain=attn_k_gain, attn_w_o=attn_w_o,
             dn_w_in=dn_w_in, dn_conv=dn_conv, dn_a_log=dn_a_log, dn_dt_bias=dn_dt_bias, dn_o_gain=dn_o_gain, dn_w_o=dn_w_o,
             mlp_norm=mlp_norm, w_up=w_up, w_down=w_down, ple_norm=ple_norm, w_ple=w_ple, w_ple_gate=w_ple_gate)
    m = dict(mix_norm=m_mix_norm, attn_w_qkv=m_attn_w_qkv, attn_q_gain=m_attn_q_gain, attn_k_gain=m_attn_k_gain,
             attn_w_o=m_attn_w_o, dn_w_in=m_dn_w_in, dn_conv=m_dn_conv, dn_a_log=m_dn_a_log, dn_dt_bias=m_dn_dt_bias,
             dn_o_gain=m_dn_o_gain, dn_w_o=m_dn_w_o, mlp_norm=m_mlp_norm, w_up=m_w_up, w_down=m_w_down,
             ple_norm=m_ple_norm, w_ple=m_w_ple, w_ple_gate=m_w_ple_gate)
    v = dict(mix_norm=v_mix_norm, attn_w_qkv=v_attn_w_qkv, attn_q_gain=v_attn_q_gain, attn_k_gain=v_attn_k_gain,
             attn_w_o=v_attn_w_o, dn_w_in=v_dn_w_in, dn_conv=v_dn_conv, dn_a_log=v_dn_a_log, dn_dt_bias=v_dn_dt_bias,
             dn_o_gain=v_dn_o_gain, dn_w_o=v_dn_w_o, mlp_norm=v_mlp_norm, w_up=v_w_up, w_down=v_w_down,
             ple_norm=v_ple_norm, w_ple=v_w_ple, w_ple_gate=v_w_ple_gate)
    S = x.shape[1]

    segs = []
    for n, _ in BIG:
        if n == "dn_conv":
            segs.append(lax.bitcast_convert_type(w[n].reshape(-1), BF16).reshape(-1))
        else:
            segs.append(w[n].astype(BF16).reshape(-1))
    gathered = _all_gather("gather_weights", _to_rows(jnp.concatenate(segs), 16)).reshape(N_DEV, -1)
    W, off = {}, 0
    for n, axis in BIG:
        sz = math.prod(w[n].shape)
        if n == "dn_conv":
            bits = gathered[:, off:off + 2 * sz].reshape(N_DEV, sz, 2)
            W[n] = _join_from_devices(lax.bitcast_convert_type(bits, F32), w[n].shape, axis)[0]
            off += 2 * sz
        else:
            W[n] = _join_from_devices(gathered[:, off:off + sz], w[n].shape, axis)
            off += sz
    w_in = W.pop("dn_w_in")[0]
    W["dn_w_qkvz"] = jnp.concatenate([w_in[:, :DN_QKV], w_in[:, DN_QKV + 2 * DN_HEADS:]], axis=1)
    W["dn_w_ab"] = jnp.pad(w_in[:, DN_QKV:DN_QKV + 2 * DN_HEADS], ((0, 0), (0, LANE - 2 * DN_HEADS)))
    for n in ("attn_w_qkv", "attn_w_o", "dn_w_o"):
        W[n] = W[n][0]
    P = dict(mix_norm=mix_norm, attn_q_gain=attn_q_gain[0], attn_k_gain=attn_k_gain[0], dn_a_log=dn_a_log[0],
             dn_dt_bias=dn_dt_bias[0], dn_o_gain=dn_o_gain[0], mlp_norm=mlp_norm, ple_norm=ple_norm)

    sq, dx0, grads = _local_step(x[0], p[:, 0], positions.reshape(S, 1), loss_target[0], W, P)

    send = _to_rows(jnp.concatenate([_split_for_devices(grads[n].astype(BF16), axis) for n, axis in BIG], axis=1), FLAT_TILE)
    parts = _all_to_all("exchange_grads", send)
    big = _adamw("adamw_big", parts, _pack_shards(w), _pack_shards(m), _pack_shards(v), FLAT_TILE)
    big = [_unpack_shards(b, w) for b in big]

    loss_row = jnp.pad((0.5 / D_MODEL) * jnp.sum(sq, axis=1, keepdims=True), ((0, 0), (0, LANE - 1)))
    small_like = {n: w[n] for n in SMALL}
    parts_s = _all_gather("gather_small", _pack_small({n: grads[n] for n in SMALL}, loss_row))
    zero_row = jnp.zeros((1, LANE), F32)
    small = _adamw("adamw_small", parts_s, _pack_small(w, zero_row), _pack_small(m, zero_row), _pack_small(v, zero_row), SMALL_ROWS)
    loss = small[0][SMALL_ROWS - 1, 0]
    small = [_unpack_small(b, small_like)[0] for b in small]

    outs = [loss, dx0[None]]
    for k in range(4):
        for n in WEIGHTS:
            outs.append(small[k][n] if n in SMALL else big[k][n])
    return tuple(outs)
```

```python
import functools
import math

import jax
import jax.numpy as jnp
from jax import lax
from jax.experimental import pallas as pl
from jax.experimental.pallas import tpu as pltpu

F32 = jnp.float32
BF16 = jnp.bfloat16
HIGHEST = lax.Precision.HIGHEST

N_DEV = 8
D_MODEL = 1024
EPS = 1e-6
SWA_GROUPS = ((128, 1), (512, 4), (2048, 16))
A_HEADS = 8
A_HEAD_DIM = 64
A_WIDTH = A_HEADS * A_HEAD_DIM
A_QKV = 3 * 3 * A_WIDTH
ROPE_DIM = 16
ROPE_HALF = 8
ROPE_THETA = 500000.0
BAND = 128
DN_HEADS = 8
DN_DIM = 128
DN_WIDTH = DN_HEADS * DN_DIM
CONV_W = 4
CHUNK = 64
D_FF = 4 * D_MODEL
PLE_DIM = 256
LR, B1, B2, ADAM_EPS, WD, STEP = 0.001, 0.9, 0.999, 1e-08, 0.01, 10

VMEM_LIMIT = 56 * 1024 * 1024
MXU_TILE = 1024
LANE = 128
SUBLANE = 8


def _cparams(sem):
    return pltpu.CompilerParams(dimension_semantics=sem, vmem_limit_bytes=VMEM_LIMIT)


def _tile(n, pref):
    if n <= pref:
        return n
    t = (pref // LANE) * LANE
    while t >= LANE:
        if n % t == 0:
            return t
        t -= LANE
    raise ValueError(f"no tile for {n}")


def _dot(a, b, ca=1, cb=0, precision=None):
    return lax.dot_general(a, b, (((ca,), (cb,)), ((), ())), precision=precision,
                           preferred_element_type=F32)


def _bdot(a, b, ca=1, cb=0):
    return _dot(a.astype(BF16), b.astype(BF16), ca, cb)


def _mm(name, a, b, *, ta=False, tb=False, epilogue=None, extras=(), out_dtypes=(F32,),
        tm_pref=MXU_TILE, tn_pref=1536, tk_pref=MXU_TILE):
    M, K = (a.shape[1], a.shape[0]) if ta else a.shape
    N = b.shape[0] if tb else b.shape[1]
    assert (b.shape[1] if tb else b.shape[0]) == K
    tm, tn, tk = _tile(M, tm_pref), _tile(N, tn_pref), _tile(K, tk_pref)
    nk = K // tk
    n_out = len(out_dtypes)
    n_ext = len(extras)

    def body(*refs):
        a_ref, b_ref = refs[0], refs[1]
        ext = refs[2:2 + n_ext]
        outs = refs[2 + n_ext:2 + n_ext + n_out]
        acc = refs[-1]
        k = pl.program_id(2)

        @pl.when(k == 0)
        def _():
            acc[...] = jnp.zeros_like(acc)

        acc[...] += _bdot(a_ref[...], b_ref[...], 0 if ta else 1, 1 if tb else 0)

        @pl.when(k == nk - 1)
        def _():
            r = acc[...]
            res = (r,) if epilogue is None else epilogue(r, *[e[...] for e in ext])
            for o, v in zip(outs, res):
                o[...] = v.astype(o.dtype)

    a_spec = pl.BlockSpec((tk, tm), lambda i, j, k: (k, i)) if ta else pl.BlockSpec((tm, tk), lambda i, j, k: (i, k))
    b_spec = pl.BlockSpec((tn, tk), lambda i, j, k: (j, k)) if tb else pl.BlockSpec((tk, tn), lambda i, j, k: (k, j))
    ext_specs = []
    for e in extras:
        if e.shape[0] == 1 and M != 1:
            ext_specs.append(pl.BlockSpec((1, tn), lambda i, j, k: (0, j)))
        else:
            ext_specs.append(pl.BlockSpec((tm, tn), lambda i, j, k: (i, j)))
    out = pl.pallas_call(
        body, name=name,
        grid=(M // tm, N // tn, nk),
        in_specs=[a_spec, b_spec] + ext_specs,
        out_specs=[pl.BlockSpec((tm, tn), lambda i, j, k: (i, j)) for _ in range(n_out)],
        out_shape=[jax.ShapeDtypeStruct((M, N), dt) for dt in out_dtypes],
        scratch_shapes=[pltpu.VMEM((tm, tn), F32)],
        compiler_params=_cparams(("parallel", "parallel", "arbitrary")),
    )(a, b, *extras)
    return out[0] if n_out == 1 else tuple(out)


def _rows(name, fn, ins, outs, *, tr, accs=()):
    ins = [(e[0], e[1]) + (e[2] if len(e) > 2 else (0, e[0].shape[-1])) for e in ins]
    n_rows = next(e[0].shape[0] for e in ins if e[1] == "row")
    assert n_rows % tr == 0 and tr % SUBLANE == 0
    steps = n_rows // tr
    t8 = tr // SUBLANE
    n8 = n_rows // SUBLANE
    n_in, n_out, n_acc = len(ins), len(outs), len(accs)

    def body(*refs):
        i = pl.program_id(0)
        vals = fn(i, steps, *[r[...] for r in refs[:n_in]])
        if not isinstance(vals, (tuple, list)):
            vals = (vals,)
        assert len(vals) == n_out + n_acc
        for o, v in zip(refs[n_in:n_in + n_out], vals[:n_out]):
            o[...] = v.astype(o.dtype)
        if n_acc:
            acc_refs = refs[n_in + n_out:]

            @pl.when(i == 0)
            def _():
                for r in acc_refs:
                    r[...] = jnp.zeros_like(r)

            for r, v in zip(acc_refs, vals[n_out:]):
                r[...] += v.astype(r.dtype)

    in_specs = []
    for a, kind, cb, c in ins:
        if kind == "row":
            in_specs.append(pl.BlockSpec((tr, c), lambda i, cb=cb: (i, cb)))
        elif kind == "full":
            in_specs.append(pl.BlockSpec(a.shape, lambda i, z=(0,) * a.ndim: z))
        elif kind == "prev8":
            in_specs.append(pl.BlockSpec((SUBLANE, c), lambda i, cb=cb: (jnp.maximum(i * t8 - 1, 0), cb)))
        elif kind == "next8":
            in_specs.append(pl.BlockSpec((SUBLANE, c), lambda i, cb=cb: (jnp.minimum((i + 1) * t8, n8 - 1), cb)))
        else:
            raise ValueError(kind)
    out_specs = [pl.BlockSpec((tr, c), lambda i: (i, 0)) for c, _ in outs]
    out_specs += [pl.BlockSpec(s, lambda i, z=(0,) * len(s): z) for s, _ in accs]
    out_shape = [jax.ShapeDtypeStruct((n_rows, c), dt) for c, dt in outs]
    out_shape += [jax.ShapeDtypeStruct(s, dt) for s, dt in accs]
    res = pl.pallas_call(
        body, name=name, grid=(steps,), in_specs=in_specs, out_specs=out_specs, out_shape=out_shape,
        compiler_params=_cparams(("arbitrary",) if n_acc else ("parallel",)),
    )(*[e[0] for e in ins])
    return res[0] if len(res) == 1 else tuple(res)


def _colsum(x):
    return jnp.sum(x, axis=0, keepdims=True)


def _sum_all(x):
    return jnp.sum(jnp.sum(x, axis=1, keepdims=True), axis=0, keepdims=True)


def _rmsnorm_fwd(name, x, gain):
    def fn(i, n, xt, g):
        r = lax.rsqrt(jnp.mean(xt * xt, axis=-1, keepdims=True) + EPS)
        return (xt * r * g,)
    return _rows(name, fn, [(x, "row"), (gain, "full")], [(x.shape[1], BF16)], tr=512)


def _rmsnorm_bwd(name, x, gain, dh, dres):
    def fn(i, n, xt, g, dht, drt):
        r = lax.rsqrt(jnp.mean(xt * xt, axis=-1, keepdims=True) + EPS)
        xh = xt * r
        dxn = dht * g
        dx = r * (dxn - xh * jnp.mean(dxn * xh, axis=-1, keepdims=True))
        return drt + dx, _colsum(dht * xh)
    D = x.shape[1]
    return _rows(name, fn, [(x, "row"), (gain, "full"), (dh, "row"), (dres, "row")], [(D, F32)],
                 tr=256, accs=[((1, D), F32)])


def _head_consts():
    import numpy as np
    e = np.arange(A_WIDTH) % A_HEAD_DIM
    inv = (np.float32(ROPE_THETA) ** (-np.arange(0, ROPE_DIM, 2, dtype=np.float32) / np.float32(ROPE_DIM))).astype(np.float32)
    c = np.zeros((8, A_WIDTH), np.float32)
    c[0] = np.where(e < ROPE_DIM, inv[e % ROPE_HALF], 0.0)
    c[1] = np.where(e < ROPE_HALF, -1.0, np.where(e < ROPE_DIM, 1.0, 0.0))
    c[2] = (e < ROPE_HALF).astype(np.float32)
    c[3] = (e < ROPE_DIM).astype(np.float32)
    return jnp.asarray(c)


def _block_diag(scale):
    import numpy as np
    h = np.arange(A_WIDTH) // A_HEAD_DIM
    return jnp.asarray((h[:, None] == h[None, :]).astype(np.float32) * scale, dtype=BF16)


def _seg_sum(x, bd):
    hi = x.astype(BF16)
    lo = (x - hi.astype(F32)).astype(BF16)
    return _dot(hi, bd) + _dot(lo, bd)


def _rope_tables(positions, consts):
    def fn(i, n, pos, c):
        ang = pos.astype(F32) * c[0:1, :]
        return jnp.cos(ang), jnp.sin(ang) * c[1:2, :]
    return _rows("rope_tables", fn, [(positions, "row"), (consts, "full")],
                 [(A_WIDTH, F32), (A_WIDTH, F32)], tr=512)


def _rope_apply(y, ct, st, low):
    rolled = jnp.where(low, pltpu.roll(y, A_WIDTH - ROPE_HALF, 1), pltpu.roll(y, ROPE_HALF, 1))
    return y * ct + rolled * st


def _rope_apply_bwd(dout, ct, st, low, in16):
    t = dout * st
    back = jnp.where(low, pltpu.roll(t, A_WIDTH - ROPE_HALF, 1), jnp.where(in16, pltpu.roll(t, ROPE_HALF, 1), 0.0))
    return dout * ct + back


def _attn_prep(qkv, gains, ct, st, consts, bd):
    def fn(i, n, t, g, c_t, s_t, c, b):
        low = c[2:3, :] > 0.5
        cols = []
        for grp in range(3):
            for which in range(3):
                off = (grp * 3 + which) * A_WIDTH
                x = t[:, off:off + A_WIDTH]
                if which == 2:
                    cols.append(x.astype(BF16))
                    continue
                r = lax.rsqrt(_seg_sum(x * x, b) + EPS)
                y = x * r * g[grp * 2 + which:grp * 2 + which + 1, :]
                cols.append(_rope_apply(y, c_t, s_t, low).astype(BF16))
        return (jnp.concatenate(cols, axis=1),)
    return _rows("attn_prep", fn, [(qkv, "row"), (gains, "full"), (ct, "row"), (st, "row"), (consts, "full"), (bd, "full")],
                 [(A_QKV, BF16)], tr=256)


def _band_mask(n):
    row = lax.broadcasted_iota(jnp.int32, (BAND, 2 * BAND), 0)
    col = lax.broadcasted_iota(jnp.int32, (BAND, 2 * BAND), 1)
    dist = row + BAND - col
    return (dist >= 0) & (dist <= BAND) & ((col >= BAND) | (n > 0))


def _attn_fwd(qkvn, grp):
    S = qkvn.shape[0]
    d = SWA_GROUPS[grp][1]
    L = S // d
    nblk = L // BAND
    assert L % BAND == 0
    view = qkvn.reshape(L, d * A_QKV)
    base = 3 * grp

    def body(q_ref, kc_ref, kp_ref, vc_ref, vp_ref, o_ref, lse_ref):
        n = pl.program_id(1)
        valid = _band_mask(n)
        first = lax.broadcasted_iota(jnp.int32, (BAND, LANE), 1) < A_HEAD_DIM
        o_cols, l_cols = [], []
        for pr in range(A_WIDTH // LANE):
            sl = slice(pr * LANE, (pr + 1) * LANE)
            qp = q_ref[:, sl]
            kcat = jnp.concatenate([kp_ref[:, sl], kc_ref[:, sl]], axis=0)
            vcat = jnp.concatenate([vp_ref[:, sl], vc_ref[:, sl]], axis=0)
            res = []
            for m in (first, jnp.logical_not(first)):
                s = _dot(jnp.where(m, qp, jnp.zeros_like(qp)), kcat, 1, 1) * (A_HEAD_DIM ** -0.5)
                s = jnp.where(valid, s, -1e30)
                mx = jnp.max(s, axis=-1, keepdims=True)
                e = jnp.exp(s - mx)
                l = jnp.sum(e, axis=-1, keepdims=True)
                res.append((_dot((e / l).astype(BF16), vcat), mx + jnp.log(l)))
            o_cols.append(jnp.where(first, res[0][0], res[1][0]))
            l_cols.append(jnp.where(first, res[0][1], res[1][1]))
        o_ref[...] = jnp.concatenate(o_cols, axis=1)
        lse_ref[...] = jnp.concatenate(l_cols, axis=1)

    blk = (BAND, A_WIDTH)
    o, lse = pl.pallas_call(
        body, name=f"attn_fwd_g{grp}", grid=(d, nblk),
        in_specs=[pl.BlockSpec(blk, lambda r, n: (n, r * 9 + base)),
                  pl.BlockSpec(blk, lambda r, n: (n, r * 9 + base + 1)),
                  pl.BlockSpec(blk, lambda r, n: (jnp.maximum(n - 1, 0), r * 9 + base + 1)),
                  pl.BlockSpec(blk, lambda r, n: (n, r * 9 + base + 2)),
                  pl.BlockSpec(blk, lambda r, n: (jnp.maximum(n - 1, 0), r * 9 + base + 2))],
        out_specs=[pl.BlockSpec(blk, lambda r, n: (n, r)), pl.BlockSpec(blk, lambda r, n: (n, r))],
        out_shape=[jax.ShapeDtypeStruct((L, d * A_WIDTH), F32)] * 2,
        compiler_params=_cparams(("parallel", "parallel")),
    )(view, view, view, view, view)
    return o.reshape(S, A_WIDTH), lse.reshape(S, A_WIDTH)


def _merge_weights(l0, l1, l2):
    mx = jnp.maximum(jnp.maximum(l0, l1), l2)
    e0, e1, e2 = jnp.exp(l0 - mx), jnp.exp(l1 - mx), jnp.exp(l2 - mx)
    inv = 1.0 / (e0 + e1 + e2)
    return e0 * inv, e1 * inv, e2 * inv


def _attn_merge(os_, lses):
    def fn(i, n, o0, o1, o2, l0, l1, l2):
        w0, w1, w2 = _merge_weights(l0, l1, l2)
        return (w0 * o0 + w1 * o1 + w2 * o2,)
    ins = [(a, "row") for a in (*os_, *lses)]
    return _rows("attn_merge", fn, ins, [(A_WIDTH, BF16)], tr=512)


def _attn_merge_bwd(do, os_, lses, bd1):
    def fn(i, n, dot_, o0, o1, o2, l0, l1, l2, b):
        w0, w1, w2 = _merge_weights(l0, l1, l2)
        o = w0 * o0 + w1 * o1 + w2 * o2
        dsum = _seg_sum(dot_ * o, b)
        return (w0 * dot_, w1 * dot_, w2 * dot_, -w0 * dsum, -w1 * dsum, -w2 * dsum)
    ins = [(do, "row")] + [(a, "row") for a in (*os_, *lses)] + [(bd1, "full")]
    res = _rows("attn_merge_bwd", fn, ins, [(A_WIDTH, BF16)] * 3 + [(A_WIDTH, F32)] * 3, tr=256)
    return res[:3], res[3:]


def _lane_pick(x, lane_idx, lane):
    return jnp.sum(jnp.where(lane_idx == lane, x, 0.0), axis=-1, keepdims=True)


def _attn_bwd(qkvn, grp, do_g, lse, c_g):
    S = qkvn.shape[0]
    d = SWA_GROUPS[grp][1]
    L = S // d
    nblk = L // BAND
    view = qkvn.reshape(L, d * A_QKV)
    base = 3 * grp
    dov, lsev, cv = (t.reshape(L, d * A_WIDTH) for t in (do_g, lse, c_g))

    def body(q_ref, kc_ref, kp_ref, vc_ref, vp_ref, do_ref, lse_ref, c_ref, dq_ref, dk_ref, dv_ref, ck, cv_):
        n = pl.program_id(1)

        @pl.when(n == 0)
        def _():
            ck[...] = jnp.zeros_like(ck)
            cv_[...] = jnp.zeros_like(cv_)

        @pl.when(n < nblk)
        def _():
            valid = _band_mask(n)
            lane = lax.broadcasted_iota(jnp.int32, (BAND, LANE), 1)
            first = lane < A_HEAD_DIM
            lane2 = lax.broadcasted_iota(jnp.int32, (2 * BAND, LANE), 1) < A_HEAD_DIM
            for pr in range(A_WIDTH // LANE):
                sl = slice(pr * LANE, (pr + 1) * LANE)
                qp = q_ref[:, sl]
                dop = do_ref[:, sl]
                kcat = jnp.concatenate([kp_ref[:, sl], kc_ref[:, sl]], axis=0)
                vcat = jnp.concatenate([vp_ref[:, sl], vc_ref[:, sl]], axis=0)
                lsep = lse_ref[:, sl]
                cp = c_ref[:, sl]
                res = []
                for hh, m in enumerate((first, jnp.logical_not(first))):
                    lse_h = _lane_pick(lsep, lane, hh * A_HEAD_DIM)
                    c_h = _lane_pick(cp, lane, hh * A_HEAD_DIM)
                    s = _dot(jnp.where(m, qp, jnp.zeros_like(qp)), kcat, 1, 1) * (A_HEAD_DIM ** -0.5)
                    p = jnp.where(valid, jnp.exp(s - lse_h), 0.0)
                    dp = _dot(jnp.where(m, dop, jnp.zeros_like(dop)), vcat, 1, 1)
                    ds = (p * (dp + c_h) * (A_HEAD_DIM ** -0.5)).astype(BF16)
                    pb = p.astype(BF16)
                    res.append((_dot(ds, kcat), _dot(ds, qp, 0, 0), _dot(pb, dop, 0, 0)))
                dq_ref[:, sl] = jnp.where(first, res[0][0], res[1][0])
                dkc = jnp.where(lane2, res[0][1], res[1][1])
                dvc = jnp.where(lane2, res[0][2], res[1][2])
                dk_ref[:, sl] = ck[:, sl] + dkc[:BAND]
                dv_ref[:, sl] = cv_[:, sl] + dvc[:BAND]
                ck[:, sl] = dkc[BAND:]
                cv_[:, sl] = dvc[BAND:]

        @pl.when(n == nblk)
        def _():
            dk_ref[...] = ck[...]
            dv_ref[...] = cv_[...]

    blk = (BAND, A_WIDTH)
    last = nblk - 1
    qn = lambda n: jnp.minimum(n, last)
    pn = lambda n: jnp.clip(n - 1, 0, last)
    dq, dk, dv = pl.pallas_call(
        body, name=f"attn_bwd_g{grp}", grid=(d, nblk + 1),
        in_specs=[pl.BlockSpec(blk, lambda r, n: (qn(n), r * 9 + base)),
                  pl.BlockSpec(blk, lambda r, n: (qn(n), r * 9 + base + 1)),
                  pl.BlockSpec(blk, lambda r, n: (pn(n), r * 9 + base + 1)),
                  pl.BlockSpec(blk, lambda r, n: (qn(n), r * 9 + base + 2)),
                  pl.BlockSpec(blk, lambda r, n: (pn(n), r * 9 + base + 2)),
                  pl.BlockSpec(blk, lambda r, n: (qn(n), r)),
                  pl.BlockSpec(blk, lambda r, n: (qn(n), r)),
                  pl.BlockSpec(blk, lambda r, n: (qn(n), r))],
        out_specs=[pl.BlockSpec(blk, lambda r, n: (qn(n), r)),
                   pl.BlockSpec(blk, lambda r, n: (pn(n), r)),
                   pl.BlockSpec(blk, lambda r, n: (pn(n), r))],
        out_shape=[jax.ShapeDtypeStruct((L, d * A_WIDTH), F32)] * 3,
        scratch_shapes=[pltpu.VMEM(blk, F32), pltpu.VMEM(blk, F32)],
        compiler_params=_cparams(("parallel", "arbitrary")),
    )(view, view, view, view, view, dov, lsev, cv)
    return tuple(t.reshape(S, A_WIDTH) for t in (dq, dk, dv))


def _attn_prep_bwd(qkv, grads, gains, ct, st, consts, bd):
    def fn(i, n, t, g, c_t, s_t, c, b, *gr):
        low = c[2:3, :] > 0.5
        in16 = c[3:4, :] > 0.5
        cols, dgs = [], []
        for grp in range(3):
            for which in range(3):
                dout = gr[grp * 3 + which]
                if which == 2:
                    cols.append(dout.astype(BF16))
                    continue
                off = (grp * 3 + which) * A_WIDTH
                x = t[:, off:off + A_WIDTH]
                gain = g[grp * 2 + which:grp * 2 + which + 1, :]
                r = lax.rsqrt(_seg_sum(x * x, b) + EPS)
                xh = x * r
                dy = _rope_apply_bwd(dout, c_t, s_t, low, in16)
                dyn = dy * gain
                dx = r * (dyn - xh * _seg_sum(dyn * xh, b))
                cols.append(dx.astype(BF16))
                dgs.append(_colsum(dy * xh))
        return (jnp.concatenate(cols, axis=1), *dgs)
    ins = [(qkv, "row"), (gains, "full"), (ct, "row"), (st, "row"), (consts, "full"), (bd, "full")] + [(a, "row") for a in grads]
    res = _rows("attn_prep_bwd", fn, ins, [(A_QKV, BF16)], tr=128, accs=[((1, A_WIDTH), F32)] * 6)
    return res[0], res[1:]


DN_QKV = 3 * DN_WIDTH
DN_QKVZ = DN_QKV + DN_WIDTH


def _sigmoid(x):
    return 1.0 / (1.0 + jnp.exp(-x))


def _softplus(x):
    return jnp.maximum(x, 0.0) + jnp.log(1.0 + jnp.exp(-jnp.abs(x)))


def _conv_taps(xs, w, tr):
    acc = None
    for j in range(CONV_W):
        sh = CONV_W - 1 - j
        term = (pltpu.roll(xs, sh, 0) if sh else xs)[SUBLANE:] * w[j:j + 1, :]
        acc = term if acc is None else acc + term
    return acc


def _dn_prep(qkvz, ab, convw, alog_row, dt_row):
    tr = 256

    def fn(i, n, x, xp, abt, w, al, dt):
        xp = jnp.where(i > 0, xp, 0.0)
        u = _conv_taps(jnp.concatenate([xp, x], axis=0), w, tr)
        y = u * _sigmoid(u)
        qs, ks = [], []
        for h in range(DN_HEADS):
            for dst, base, sc in ((qs, 0, DN_DIM ** -0.5), (ks, DN_WIDTH, 1.0)):
                seg = y[:, base + h * DN_DIM:base + (h + 1) * DN_DIM]
                dst.append(seg * (lax.rsqrt(jnp.sum(seg * seg, axis=-1, keepdims=True) + EPS) * sc))
        lane = lax.broadcasted_iota(jnp.int32, abt.shape, 1)
        g = -jnp.exp(al) * _softplus(abt + dt)
        gb = jnp.where(lane < DN_HEADS, g, jnp.where(lane < 2 * DN_HEADS, _sigmoid(abt), 0.0))
        return u, jnp.concatenate(qs, axis=1), jnp.concatenate(ks, axis=1), y[:, 2 * DN_WIDTH:], gb

    ins = [(qkvz, "row", (0, DN_QKV)), (qkvz, "prev8", (0, DN_QKV)), (ab, "row"), (convw, "full"),
           (alog_row, "full"), (dt_row, "full")]
    return _rows("dn_prep", fn, ins, [(DN_QKV, F32), (DN_WIDTH, F32), (DN_WIDTH, F32), (DN_WIDTH, F32), (LANE, F32)], tr=tr)


def _tri_masks():
    row = lax.broadcasted_iota(jnp.int32, (CHUNK, CHUNK), 0)
    col = lax.broadcasted_iota(jnp.int32, (CHUNK, CHUNK), 1)
    return row >= col, row > col, row == col


def _heads(fn, *lists):
    return [fn(*xs) for xs in zip(*lists)]


def _unit_lower_inverse(a_list, eye):
    ts = [eye - a for a in a_list]
    aks = a_list
    for _ in range(5):
        aks = [_dot(ak, ak, precision=HIGHEST) for ak in aks]
        ts = [t + _dot(t, ak, precision=HIGHEST) for t, ak in zip(ts, aks)]
    return ts


def _dn_terms(qs, ks, vs, gb):
    lower, strict, diag = _tri_masks()
    lane = lax.broadcasted_iota(jnp.int32, (CHUNK, LANE), 1)
    is_last = lax.broadcasted_iota(jnp.int32, (CHUNK, 1), 0) == CHUNK - 1
    hs = range(DN_HEADS)
    gc = _dot(lower.astype(F32), gb, precision=HIGHEST)
    gct = jnp.transpose(gc)
    bcol = [_lane_pick(gb, lane, DN_HEADS + h) for h in hs]
    gcol = [_lane_pick(gc, lane, h) for h in hs]
    glast = [jnp.sum(jnp.where(is_last, g, 0.0), axis=0, keepdims=True) for g in gcol]
    decay = [jnp.exp(jnp.where(lower, gcol[h] - gct[h:h + 1, :], -1e30)) for h in hs]
    kb = _heads(lambda k, b: k * b, ks, bcol)
    kk = _heads(lambda x, k: _bdot(x, k, 1, 1), kb, ks)
    qk = _heads(lambda q, k: _bdot(q, k, 1, 1), qs, ks)
    a = _heads(lambda x, d: jnp.where(strict, x * d, 0.0), kk, decay)
    t = _unit_lower_inverse(a, diag.astype(F32))
    eg = [jnp.exp(g) for g in gcol]
    egl = _heads(lambda gl, g: jnp.exp(gl - g), glast, gcol)
    rhs_w = _heads(lambda x, e: x * e, kb, eg)
    u = _heads(lambda tt, v, b: _dot(tt, v * b, precision=HIGHEST), t, vs, bcol)
    w = _heads(lambda tt, r: _dot(tt, r, precision=HIGHEST), t, rhs_w)
    return dict(bcol=bcol, decay=decay, kb=kb, a=a, t=t, eg=eg, egl=egl, rhs_w=rhs_w, u=u, w=w,
                attn=_heads(lambda x, d: x * d, qk, decay), q_dec=_heads(lambda q, e: q * e, qs, eg),
                k_dec=_heads(lambda k, e: k * e, ks, egl), c_dec=[jnp.exp(g) for g in glast],
                lower=lower, strict=strict, lane=lane, is_last=is_last)


def _head_slices(ref):
    return [ref[:, h * DN_DIM:(h + 1) * DN_DIM] for h in range(DN_HEADS)]


def _dn_chunk_fwd(q, k, v, gb):
    S = q.shape[0]
    N = S // CHUNK

    def body(q_ref, k_ref, v_ref, gb_ref, o_ref, st_ref, state):
        @pl.when(pl.program_id(0) == 0)
        def _():
            state[...] = jnp.zeros_like(state)

        f = _dn_terms(_head_slices(q_ref), _head_slices(k_ref), _head_slices(v_ref), gb_ref[...])
        s = [state[h] for h in range(DN_HEADS)]
        for h in range(DN_HEADS):
            st_ref[0, h] = s[h]
        sb = [x.astype(BF16) for x in s]
        v_new = _heads(lambda u, w, x: u - _bdot(w, x), f["u"], f["w"], sb)
        o = _heads(lambda qd, x, at, vn: _bdot(qd, x) + _bdot(at, vn), f["q_dec"], sb, f["attn"], v_new)
        new_s = _heads(lambda x, c, kd, vn: x * c + _bdot(kd, vn, 0, 0), s, f["c_dec"], f["k_dec"], v_new)
        for h in range(DN_HEADS):
            o_ref[:, h * DN_DIM:(h + 1) * DN_DIM] = o[h]
            state[h] = new_s[h]

    blk = pl.BlockSpec((CHUNK, DN_WIDTH), lambda n: (n, 0))
    st_blk = pl.BlockSpec((1, DN_HEADS, DN_DIM, DN_DIM), lambda n: (n, 0, 0, 0))
    return pl.pallas_call(
        body, name="dn_chunk_fwd", grid=(N,),
        in_specs=[blk, blk, blk, pl.BlockSpec((CHUNK, LANE), lambda n: (n, 0))],
        out_specs=[blk, st_blk],
        out_shape=[jax.ShapeDtypeStruct((S, DN_WIDTH), F32), jax.ShapeDtypeStruct((N, DN_HEADS, DN_DIM, DN_DIM), F32)],
        scratch_shapes=[pltpu.VMEM((DN_HEADS, DN_DIM, DN_DIM), F32)],
        compiler_params=_cparams(("arbitrary",)),
    )(q, k, v, gb)


def _dn_chunk_bwd(q, k, v, gb, states, do):
    S = q.shape[0]
    N = S // CHUNK

    def body(q_ref, k_ref, v_ref, gb_ref, st_ref, do_ref, dq_ref, dk_ref, dv_ref, dgb_ref, dstate):
        @pl.when(pl.program_id(0) == 0)
        def _():
            dstate[...] = jnp.zeros_like(dstate)

        hs = range(DN_HEADS)
        qs, ks, vs, dos = (_head_slices(r) for r in (q_ref, k_ref, v_ref, do_ref))
        f = _dn_terms(qs, ks, vs, gb_ref[...])
        lane, is_last = f["lane"], f["is_last"]
        rowsum = lambda x: jnp.sum(x, axis=-1, keepdims=True)
        s = [st_ref[0, h] for h in hs]
        dsn = [dstate[h] for h in hs]
        sb = [x.astype(BF16) for x in s]
        dsb = [x.astype(BF16) for x in dsn]
        dob = [x.astype(BF16) for x in dos]
        v_new = _heads(lambda u, w, x: u - _bdot(w, x), f["u"], f["w"], sb)
        dv_new = _heads(lambda at, d, kd, x: _bdot(at, d, 0, 0) + _bdot(kd, x), f["attn"], dob, f["k_dec"], dsb)
        dattn = _heads(lambda d, vn: _bdot(d, vn, 1, 1), dob, v_new)
        dq_dec = _heads(lambda d, x: _bdot(d, x, 1, 1), dob, sb)
        dk_dec = _heads(lambda vn, x: _bdot(vn, x, 1, 1), v_new, dsb)
        dw = _heads(lambda dv_, x: -_bdot(dv_, x, 1, 1), dv_new, sb)
        new_ds = _heads(lambda x, c, qd, d, w, dv_: x * c + _bdot(qd, d, 0, 0) - _bdot(w, dv_, 0, 0),
                        dsn, f["c_dec"], f["q_dec"], dob, f["w"], dv_new)
        for h in hs:
            dstate[h] = new_ds[h]
        drhs_u = _heads(lambda tt, x: _dot(tt, x, 0, 0, precision=HIGHEST), f["t"], dv_new)
        drhs_w = _heads(lambda tt, x: _dot(tt, x, 0, 0, precision=HIGHEST), f["t"], dw)
        da = _heads(lambda du_, u, dw_, w: jnp.where(f["strict"], -(_bdot(du_, u, 1, 1) + _bdot(dw_, w, 1, 1)), 0.0),
                    drhs_u, f["u"], drhs_w, f["w"])
        dkk = _heads(lambda x, d: x * d, da, f["decay"])
        dqk = _heads(lambda x, d: x * d, dattn, f["decay"])
        dkb = _heads(lambda x, k_, dw_, e: _bdot(x, k_) + dw_ * e, dkk, ks, drhs_w, f["eg"])
        dq = _heads(lambda x, k_, dqd, e: _bdot(x, k_) + dqd * e, dqk, ks, dq_dec, f["eg"])
        dk = _heads(lambda x, kb_, y, q_, dkd, el, dkb_, b: _bdot(x, kb_, 0, 0) + _bdot(y, q_, 0, 0) + dkd * el + dkb_ * b,
                    dkk, f["kb"], dqk, qs, dk_dec, f["egl"], dkb, f["bcol"])
        m = _heads(lambda x, a_, y, at: x * a_ + y * at, da, f["a"], dattn, f["attn"])
        ones = jnp.ones((CHUNK, LANE), F32)
        col_m = [_dot(x, ones, 0, 0, precision=HIGHEST)[:, 0:1] for x in m]
        dgc_all = jnp.zeros((CHUNK, LANE), F32)
        dbeta_all = jnp.zeros((CHUNK, LANE), F32)
        for h in hs:
            dq_ref[:, h * DN_DIM:(h + 1) * DN_DIM] = dq[h]
            dk_ref[:, h * DN_DIM:(h + 1) * DN_DIM] = dk[h]
            dv_ref[:, h * DN_DIM:(h + 1) * DN_DIM] = drhs_u[h] * f["bcol"][h]
            kdec_term = rowsum(dk_dec[h] * f["k_dec"][h])
            dc_dec = _sum_all(dsn[h] * s[h])
            dgc = (rowsum(m[h]) - col_m[h] + rowsum(dq_dec[h] * f["q_dec"][h]) - kdec_term
                   + rowsum(drhs_w[h] * f["rhs_w"][h]))
            last_extra = jnp.sum(kdec_term, axis=0, keepdims=True) + dc_dec * f["c_dec"][h]
            dgc = dgc + jnp.where(is_last, last_extra, 0.0)
            dbeta = rowsum(drhs_u[h] * vs[h]) + rowsum(dkb[h] * ks[h])
            dgc_all = jnp.where(lane == h, dgc, dgc_all)
            dbeta_all = jnp.where(lane == DN_HEADS + h, dbeta, dbeta_all)
        dg_all = _dot(f["lower"].astype(F32), dgc_all, 0, 0, precision=HIGHEST)
        dgb_ref[...] = jnp.where(lane < DN_HEADS, dg_all, dbeta_all)

    rev = lambda n: (N - 1 - n, 0)
    blk = pl.BlockSpec((CHUNK, DN_WIDTH), rev)
    gblk = pl.BlockSpec((CHUNK, LANE), rev)
    st_blk = pl.BlockSpec((1, DN_HEADS, DN_DIM, DN_DIM), lambda n: (N - 1 - n, 0, 0, 0))
    return pl.pallas_call(
        body, name="dn_chunk_bwd", grid=(N,),
        in_specs=[blk, blk, blk, gblk, st_blk, blk],
        out_specs=[blk, blk, blk, gblk],
        out_shape=[jax.ShapeDtypeStruct((S, DN_WIDTH), F32)] * 3 + [jax.ShapeDtypeStruct((S, LANE), F32)],
        scratch_shapes=[pltpu.VMEM((DN_HEADS, DN_DIM, DN_DIM), F32)],
        compiler_params=_cparams(("arbitrary",)),
    )(q, k, v, gb, states, do)


def _dn_post(o, qkvz, gain_row):
    def fn(i, n, ot, z, g):
        cols = []
        for h in range(DN_HEADS):
            seg = ot[:, h * DN_DIM:(h + 1) * DN_DIM]
            cols.append(seg * lax.rsqrt(jnp.mean(seg * seg, axis=-1, keepdims=True) + EPS) * g)
        return (jnp.concatenate(cols, axis=1) * (z * _sigmoid(z)),)
    return _rows("dn_post", fn, [(o, "row"), (qkvz, "row", (3, DN_WIDTH)), (gain_row, "full")], [(DN_WIDTH, BF16)], tr=512)


def _dn_post_bwd(don, o, qkvz, gain_row):
    def fn(i, n, dy, ot, z, g):
        sg = _sigmoid(z)
        sz = z * sg
        dos, ohs = [], []
        dg = jnp.zeros((1, DN_DIM), F32)
        for h in range(DN_HEADS):
            sl = slice(h * DN_DIM, (h + 1) * DN_DIM)
            seg = ot[:, sl]
            r = lax.rsqrt(jnp.mean(seg * seg, axis=-1, keepdims=True) + EPS)
            oh = seg * r
            dno = dy[:, sl] * sz[:, sl]
            dg = dg + _colsum(dno * oh)
            dn = dno * g
            dos.append(r * (dn - oh * jnp.mean(dn * oh, axis=-1, keepdims=True)))
            ohs.append(oh * g)
        dz = dy * jnp.concatenate(ohs, axis=1) * (sg * (1.0 + z * (1.0 - sg)))
        return jnp.concatenate(dos, axis=1), dz, dg
    ins = [(don, "row"), (o, "row"), (qkvz, "row", (3, DN_WIDTH)), (gain_row, "full")]
    return _rows("dn_post_bwd", fn, ins, [(DN_WIDTH, F32), (DN_WIDTH, F32)], tr=256, accs=[((1, DN_DIM), F32)])


def _dn_prep_bwd(dq, dk, dv, dgb, u, ab, alog_row, dt_row):
    def fn(i, n, dqt, dkt, dvt, dgbt, ut, abt, al, dt):
        sg = _sigmoid(ut)
        y = ut * sg
        dys = []
        for grad, base, sc in ((dqt, 0, DN_DIM ** -0.5), (dkt, DN_WIDTH, 1.0)):
            for h in range(DN_HEADS):
                seg = y[:, base + h * DN_DIM:base + (h + 1) * DN_DIM]
                gr = grad[:, h * DN_DIM:(h + 1) * DN_DIM]
                r = lax.rsqrt(jnp.sum(seg * seg, axis=-1, keepdims=True) + EPS)
                xh = seg * r
                dys.append((r * sc) * (gr - xh * jnp.sum(gr * xh, axis=-1, keepdims=True)))
        dy = jnp.concatenate(dys + [dvt], axis=1)
        du = dy * (sg * (1.0 + ut * (1.0 - sg)))
        lane = lax.broadcasted_iota(jnp.int32, abt.shape, 1)
        is_g = lane < DN_HEADS
        ea = jnp.exp(al)
        x = abt + dt
        slope = -ea * _sigmoid(x)
        gval = -ea * _softplus(x)
        dg = jnp.where(is_g, dgbt, 0.0)
        beta = _sigmoid(abt)
        dab = jnp.where(is_g, dg * slope, jnp.where(lane < 2 * DN_HEADS, dgbt * beta * (1.0 - beta), 0.0))
        return du, dab, _colsum(dg * gval), _colsum(dg * slope)
    ins = [(dq, "row"), (dk, "row"), (dv, "row"), (dgb, "row"), (u, "row"), (ab, "row"), (alog_row, "full"), (dt_row, "full")]
    return _rows("dn_prep_bwd", fn, ins, [(DN_QKV, F32), (LANE, BF16)], tr=256, accs=[((1, LANE), F32)] * 2)


def _dn_conv_bwd(du, dz, qkvz, convw):
    tr = 256

    def fn(i, n, dut, dun, dzt, x, xp, w):
        dun = jnp.where(i < n - 1, dun, 0.0)
        dus = jnp.concatenate([dut, dun], axis=0)
        xs = jnp.concatenate([jnp.where(i > 0, xp, 0.0), x], axis=0)
        dx = None
        dws = []
        for j in range(CONV_W):
            sh = CONV_W - 1 - j
            term = (pltpu.roll(dus, tr + SUBLANE - sh, 0) if sh else dus)[:tr] * w[j:j + 1, :]
            dx = term if dx is None else dx + term
            dws.append(_colsum(dut * (pltpu.roll(xs, sh, 0) if sh else xs)[SUBLANE:]))
        return (jnp.concatenate([dx.astype(BF16), dzt.astype(BF16)], axis=1), *dws)

    ins = [(du, "row"), (du, "next8"), (dz, "row"), (qkvz, "row", (0, DN_QKV)), (qkvz, "prev8", (0, DN_QKV)), (convw, "full")]
    res = _rows("dn_conv_bwd", fn, ins, [(DN_QKVZ, BF16)], tr=tr, accs=[((1, DN_QKV), F32)] * CONV_W)
    return res[0], res[1:]


def _add(acc, r):
    return (r + acc,)


def _mlp_ple_fwd(i, x1, p_i, mlp_gain, ple_gain, w_up, w_down, w_ple, w_gate):
    hm = _rmsnorm_fwd(f"mlp_norm{i}", x1, mlp_gain)
    u, a = _mm(f"mlp_up{i}", hm, w_up, epilogue=lambda acc: (acc, jnp.square(jnp.maximum(acc, 0.0))),
               out_dtypes=(F32, BF16))
    x2 = _mm(f"mlp_down{i}", a, w_down, epilogue=_add, extras=(x1,))
    hp = _rmsnorm_fwd(f"ple_norm{i}", x2, ple_gain)
    pp = _mm(f"ple_proj{i}", p_i, w_ple)

    def gate_epilogue(acc, x2t, ppt):
        gate = _sigmoid(acc)
        return x2t + ppt * gate, gate

    x3, gate = _mm(f"ple_gate{i}", hp, w_gate, epilogue=gate_epilogue, extras=(x2, pp), out_dtypes=(F32, F32))
    return x3, dict(x1=x1, hm=hm, u=u, a=a, x2=x2, hp=hp, pp=pp, gate=gate, p=p_i)


def _mlp_ple_bwd(i, dx3, sv, mlp_gain, ple_gain, w_up, w_down, w_gate):
    def fn(_i, _n, d, g, pp):
        return d * g, d * pp * g * (1.0 - g)
    dpp, dzg = _rows(f"ple_gate_bwd{i}", fn, [(dx3, "row"), (sv["gate"], "row"), (sv["pp"], "row")],
                     [(D_MODEL, BF16), (D_MODEL, BF16)], tr=512)
    d_w_ple = _mm(f"ple_proj_dw{i}", sv["p"], dpp, ta=True, out_dtypes=(BF16,))
    d_w_gate = _mm(f"ple_gate_dw{i}", sv["hp"], dzg, ta=True, out_dtypes=(BF16,))
    dhp = _mm(f"ple_gate_dx{i}", dzg, w_gate, tb=True)
    dx2, d_ple_gain = _rmsnorm_bwd(f"ple_norm_bwd{i}", sv["x2"], ple_gain, dhp, dx3)
    d_w_down = _mm(f"mlp_down_dw{i}", sv["a"], dx2, ta=True, out_dtypes=(BF16,))
    du = _mm(f"mlp_down_dx{i}", dx2, w_down, tb=True, epilogue=lambda acc, ut: (acc * (2.0 * jnp.maximum(ut, 0.0)),),
             extras=(sv["u"],), out_dtypes=(BF16,))
    d_w_up = _mm(f"mlp_up_dw{i}", sv["hm"], du, ta=True, out_dtypes=(BF16,))
    dhm = _mm(f"mlp_up_dx{i}", du, w_up, tb=True)
    dx1, d_mlp_gain = _rmsnorm_bwd(f"mlp_norm_bwd{i}", sv["x1"], mlp_gain, dhm, dx2)
    return dx1, dict(w_ple=d_w_ple, w_ple_gate=d_w_gate, w_down=d_w_down, w_up=d_w_up,
                     ple_norm=d_ple_gain, mlp_norm=d_mlp_gain)


def _loss_fwd_bwd(y, target):
    D = y.shape[1]

    def fn(i, n, yt, tt):
        e = yt - tt
        return e * (1.0 / D), _colsum(e * e)
    dy, sq = _rows("loss", fn, [(y, "row"), (target, "row")], [(D, F32)], tr=512, accs=[((1, D), F32)])
    return sq, dy


def _local_step(x, p, positions, target, W, P):
    consts = _head_consts()
    bd = _block_diag(1.0 / A_HEAD_DIM)
    bd1 = _block_diag(1.0)
    ct, st = _rope_tables(positions, consts)
    gains = jnp.stack([jnp.tile(v, A_HEADS) for g in range(3) for v in (P["attn_q_gain"][g], P["attn_k_gain"][g])])
    pad = LANE - DN_HEADS
    alog_row = jnp.pad(P["dn_a_log"].reshape(1, DN_HEADS), ((0, 0), (0, pad)))
    dt_row = jnp.pad(P["dn_dt_bias"].reshape(1, DN_HEADS), ((0, 0), (0, pad)))
    ogain_row = P["dn_o_gain"].reshape(1, DN_DIM)
    row = lambda name, i: P[name][i:i + 1]

    h0 = _rmsnorm_fwd("mix_norm0", x, row("mix_norm", 0))
    qkv = _mm("attn_qkv", h0, W["attn_w_qkv"])
    qkvn = _attn_prep(qkv, gains, ct, st, consts, bd)
    os_, lses = zip(*[_attn_fwd(qkvn, g) for g in range(3)])
    o_attn = _attn_merge(os_, lses)
    x1 = _mm("attn_out", o_attn, W["attn_w_o"], epilogue=_add, extras=(x,))
    x3, sv0 = _mlp_ple_fwd(0, x1, p[0], row("mlp_norm", 0), row("ple_norm", 0),
                           W["w_up"][0], W["w_down"][0], W["w_ple"][0], W["w_ple_gate"][0])
    h1 = _rmsnorm_fwd("mix_norm1", x3, row("mix_norm", 1))
    qkvz = _mm("dn_in_qkvz", h1, W["dn_w_qkvz"])
    ab = _mm("dn_in_ab", h1, W["dn_w_ab"])
    u, q, k, v, gb = _dn_prep(qkvz, ab, W["dn_conv"], alog_row, dt_row)
    o_dn, states = _dn_chunk_fwd(q, k, v, gb)
    on = _dn_post(o_dn, qkvz, ogain_row)
    x4 = _mm("dn_out", on, W["dn_w_o"], epilogue=_add, extras=(x3,))
    x6, sv1 = _mlp_ple_fwd(1, x4, p[1], row("mlp_norm", 1), row("ple_norm", 1),
                           W["w_up"][1], W["w_down"][1], W["w_ple"][1], W["w_ple_gate"][1])
    sq, dy = _loss_fwd_bwd(x6, target)

    dx4, g1 = _mlp_ple_bwd(1, dy, sv1, row("mlp_norm", 1), row("ple_norm", 1),
                           W["w_up"][1], W["w_down"][1], W["w_ple_gate"][1])
    don = _mm("dn_out_dx", dx4, W["dn_w_o"], tb=True)
    d_dn_w_o = _mm("dn_out_dw", on, dx4, ta=True, out_dtypes=(BF16,))
    do_dn, dz, d_ogain = _dn_post_bwd(don, o_dn, qkvz, ogain_row)
    dq, dk, dv, dgb = _dn_chunk_bwd(q, k, v, gb, states, do_dn)
    du, dab, d_alog, d_dt = _dn_prep_bwd(dq, dk, dv, dgb, u, ab, alog_row, dt_row)
    dqkvz, d_conv = _dn_conv_bwd(du, dz, qkvz, W["dn_conv"])
    dh1 = _mm("dn_in_qkvz_dx", dqkvz, W["dn_w_qkvz"], tb=True)
    dh1 = _mm("dn_in_ab_dx", dab, W["dn_w_ab"], tb=True, epilogue=_add, extras=(dh1,))
    d_w_qkvz = _mm("dn_in_qkvz_dw", h1, dqkvz, ta=True, out_dtypes=(BF16,))
    d_w_ab = _mm("dn_in_ab_dw", h1, dab, ta=True, out_dtypes=(BF16,))
    dx3, d_mix1 = _rmsnorm_bwd("mix_norm_bwd1", x3, row("mix_norm", 1), dh1, dx4)
    dx1, g0 = _mlp_ple_bwd(0, dx3, sv0, row("mlp_norm", 0), row("ple_norm", 0),
                           W["w_up"][0], W["w_down"][0], W["w_ple_gate"][0])
    do_attn = _mm("attn_out_dx", dx1, W["attn_w_o"], tb=True)
    d_attn_w_o = _mm("attn_out_dw", o_attn, dx1, ta=True, out_dtypes=(BF16,))
    dos, cs = _attn_merge_bwd(do_attn, os_, lses, bd1)
    grads9 = []
    for g in range(3):
        grads9 += list(_attn_bwd(qkvn, g, dos[g], lses[g], cs[g]))
    dqkv, dgains = _attn_prep_bwd(qkv, grads9, gains, ct, st, consts, bd)
    dh0 = _mm("attn_qkv_dx", dqkv, W["attn_w_qkv"], tb=True)
    d_attn_w_qkv = _mm("attn_qkv_dw", h0, dqkv, ta=True, out_dtypes=(BF16,))
    dx0, d_mix0 = _rmsnorm_bwd("mix_norm_bwd0", x, row("mix_norm", 0), dh0, dx1)

    dg = jnp.stack([t.reshape(A_HEADS, A_HEAD_DIM).sum(0) for t in dgains])
    d_w_in = jnp.concatenate([d_w_qkvz[:, :DN_QKV], d_w_ab[:, :2 * DN_HEADS], d_w_qkvz[:, DN_QKV:]], axis=1)
    grads = dict(
        mix_norm=jnp.concatenate([d_mix0, d_mix1], 0),
        attn_w_qkv=d_attn_w_qkv[None], attn_q_gain=dg[0::2][None], attn_k_gain=dg[1::2][None],
        attn_w_o=d_attn_w_o[None], dn_w_in=d_w_in[None],
        dn_conv=jnp.concatenate(d_conv, 0)[None],
        dn_a_log=d_alog[:, :DN_HEADS], dn_dt_bias=d_dt[:, :DN_HEADS], dn_o_gain=d_ogain, dn_w_o=d_dn_w_o[None],
        mlp_norm=jnp.concatenate([g0["mlp_norm"], g1["mlp_norm"]], 0),
        w_up=jnp.stack([g0["w_up"], g1["w_up"]]), w_down=jnp.stack([g0["w_down"], g1["w_down"]]),
        ple_norm=jnp.concatenate([g0["ple_norm"], g1["ple_norm"]], 0),
        w_ple=jnp.stack([g0["w_ple"], g1["w_ple"]]), w_ple_gate=jnp.stack([g0["w_ple_gate"], g1["w_ple_gate"]]),
    )
    return sq, dx0, grads


MESH_IDS = pl.DeviceIdType.MESH
ANY = pl.BlockSpec(memory_space=pl.ANY)


def _place():
    return lax.axis_index("x"), lax.axis_index("y"), lax.axis_index("c")


def _all_gather(name, buf):
    R, C = buf.shape

    def body(x_ref, out_ref, send_sems, recv_sems, local_sem):
        x, y, c = _place()
        me, sibling = (x, y, c), (x, y, 1 - c)
        chips = [(1 - x, y), (x, 1 - y), (1 - x, 1 - y)]

        def slot(px, py, pc):
            return out_ref.at[4 * px + 2 * py + pc]

        def copy(k, block, to, src=None):
            return pltpu.make_async_remote_copy(
                src_ref=slot(*block) if src is None else src, dst_ref=slot(*block),
                send_sem=send_sems.at[k], recv_sem=recv_sems.at[k], device_id=to, device_id_type=MESH_IDS)

        mine = pltpu.make_async_copy(x_ref, slot(*me), local_sem)
        mine.start()
        first = [copy(0, me, sibling, src=x_ref)]
        first += [copy(1 + j, me, (*chip, c), src=x_ref) for j, chip in enumerate(chips)]
        for cp in first:
            cp.start()
        passed = [copy(4 + j, (*chip, c), sibling) for j, chip in enumerate(chips)]
        for j, chip in enumerate(chips):
            copy(1 + j, (*chip, c), me).wait_recv()
            passed[j].start()
        copy(0, sibling, me).wait_recv()
        for j, chip in enumerate(chips):
            copy(4 + j, (*chip, 1 - c), me).wait_recv()
        for cp in first + passed:
            cp.wait_send()
        mine.wait()

    return pl.pallas_call(
        body, name=name, out_shape=jax.ShapeDtypeStruct((N_DEV, R, C), buf.dtype),
        in_specs=[ANY], out_specs=ANY,
        scratch_shapes=[pltpu.SemaphoreType.DMA((7,)), pltpu.SemaphoreType.DMA((7,)), pltpu.SemaphoreType.DMA],
    )(buf)


def _all_to_all(name, send):
    _, R, C = send.shape

    def body(send_ref, recv_ref, send_sems, recv_sems, local_sem):
        x, y, c = _place()
        me = 4 * x + 2 * y + c
        mine = pltpu.make_async_copy(send_ref.at[me], recv_ref.at[me], local_sem)
        mine.start()
        copies, arrivals = [], []
        for k in range(1, N_DEV):
            px = 1 - x if k & 4 else x
            py = 1 - y if k & 2 else y
            pc = 1 - c if k & 1 else c
            peer = 4 * px + 2 * py + pc
            copies.append(pltpu.make_async_remote_copy(
                src_ref=send_ref.at[peer], dst_ref=recv_ref.at[me], send_sem=send_sems.at[k - 1],
                recv_sem=recv_sems.at[k - 1], device_id=(px, py, pc), device_id_type=MESH_IDS))
            arrivals.append(pltpu.make_async_remote_copy(
                src_ref=send_ref.at[peer], dst_ref=recv_ref.at[peer], send_sem=send_sems.at[k - 1],
                recv_sem=recv_sems.at[k - 1], device_id=(px, py, pc), device_id_type=MESH_IDS))
        for cp in copies:
            cp.start()
        for cp in arrivals:
            cp.wait_recv()
        for cp in copies:
            cp.wait_send()
        mine.wait()

    return pl.pallas_call(
        body, name=name, out_shape=jax.ShapeDtypeStruct(send.shape, send.dtype),
        in_specs=[ANY], out_specs=ANY,
        scratch_shapes=[pltpu.SemaphoreType.DMA((7,)), pltpu.SemaphoreType.DMA((7,)), pltpu.SemaphoreType.DMA],
    )(send)


def _adamw(name, parts, w, m, v, tr):
    R = w.shape[0]
    assert R % tr == 0
    c1 = 1.0 - B1 ** STEP
    c2 = 1.0 - B2 ** STEP

    def body(p_ref, w_ref, m_ref, v_ref, g_ref, d_ref, nm_ref, nv_ref):
        g = p_ref[0].astype(F32)
        for dev in range(1, N_DEV):
            g = g + p_ref[dev].astype(F32)
        nm = B1 * m_ref[...] + (1.0 - B1) * g
        nv = B2 * v_ref[...] + (1.0 - B2) * jnp.square(g)
        g_ref[...] = g
        nm_ref[...] = nm
        nv_ref[...] = nv
        d_ref[...] = -LR * ((nm / c1) / (jnp.sqrt(nv / c2) + ADAM_EPS) + WD * w_ref[...])

    blk = pl.BlockSpec((tr, LANE), lambda i: (i, 0))
    return pl.pallas_call(
        body, name=name, grid=(R // tr,),
        in_specs=[pl.BlockSpec((N_DEV, tr, LANE), lambda i: (0, i, 0)), blk, blk, blk],
        out_specs=[blk] * 4, out_shape=[jax.ShapeDtypeStruct((R, LANE), F32)] * 4,
        compiler_params=_cparams(("parallel",)),
    )(parts, w, m, v)


BIG = (("attn_w_qkv", 2), ("attn_w_o", 2), ("dn_w_in", 2), ("dn_conv", 2), ("dn_w_o", 1),
       ("w_up", 2), ("w_down", 1), ("w_ple", 2), ("w_ple_gate", 1))
SMALL = ("mix_norm", "attn_q_gain", "attn_k_gain", "dn_a_log", "dn_dt_bias", "dn_o_gain", "mlp_norm", "ple_norm")
WEIGHTS = ("mix_norm", "attn_w_qkv", "attn_q_gain", "attn_k_gain", "attn_w_o", "dn_w_in", "dn_conv", "dn_a_log",
           "dn_dt_bias", "dn_o_gain", "dn_w_o", "mlp_norm", "w_up", "w_down", "ple_norm", "w_ple", "w_ple_gate")
FLAT_TILE = 512


def _to_rows(flat, multiple):
    n = flat.shape[-1]
    rows = -(-n // (LANE * multiple)) * multiple
    padw = [(0, 0)] * (flat.ndim - 1) + [(0, rows * LANE - n)]
    return jnp.pad(flat, padw).reshape(flat.shape[:-1] + (rows, LANE))


def _pack_shards(shards):
    return _to_rows(jnp.concatenate([shards[n].reshape(-1) for n, _ in BIG]), FLAT_TILE)


def _unpack_shards(buf, like):
    out, off = {}, 0
    flat = buf.reshape(-1)
    for n, _ in BIG:
        sz = math.prod(like[n].shape)
        out[n] = flat[off:off + sz].reshape(like[n].shape)
        off += sz
    return out


def _split_for_devices(full, axis):
    s = full.shape
    t = full.reshape(s[:axis] + (N_DEV, s[axis] // N_DEV) + s[axis + 1:])
    return jnp.moveaxis(t, axis, 0).reshape(N_DEV, -1)


def _join_from_devices(gathered, shard_shape, axis):
    t = jnp.moveaxis(gathered.reshape((N_DEV,) + tuple(shard_shape)), 0, axis)
    s = tuple(shard_shape)
    return t.reshape(s[:axis] + (N_DEV * s[axis],) + s[axis + 1:])


SMALL_ROWS = 56


def _pack_small(vals, loss_row):
    rows = []
    for n in SMALL:
        rows.append(_to_rows(vals[n].reshape(-1), 1))
    rows.append(loss_row)
    buf = jnp.concatenate(rows, 0)
    assert buf.shape == (SMALL_ROWS, LANE)
    return buf


def _unpack_small(buf, like):
    out, r = {}, 0
    for n in SMALL:
        sz = math.prod(like[n].shape)
        nr = -(-sz // LANE)
        out[n] = buf[r:r + nr].reshape(-1)[:sz].reshape(like[n].shape)
        r += nr
    return out, buf[r]


def kernel(x, p, positions, mix_norm, attn_w_qkv, attn_q_gain, attn_k_gain, attn_w_o, dn_w_in, dn_conv, dn_a_log, dn_dt_bias, dn_o_gain, dn_w_o, mlp_norm, w_up, w_down, ple_norm, w_ple, w_ple_gate, loss_target, m_mix_norm, m_attn_w_qkv, m_attn_q_gain, m_attn_k_gain, m_attn_w_o, m_dn_w_in, m_dn_conv, m_dn_a_log, m_dn_dt_bias, m_dn_o_gain, m_dn_w_o, m_mlp_norm, m_w_up, m_w_down, m_ple_norm, m_w_ple, m_w_ple_gate, v_mix_norm, v_attn_w_qkv, v_attn_q_gain, v_attn_k_gain, v_attn_w_o, v_dn_w_in, v_dn_conv, v_dn_a_log, v_dn_dt_bias, v_dn_o_gain, v_dn_w_o, v_mlp_norm, v_w_up, v_w_down, v_ple_norm, v_w_ple, v_w_ple_gate):
    w = dict(mix_norm=mix_norm, attn_w_qkv=attn_w_qkv, attn_q_gain=attn_q_gain, attn_k_gain=attn_k_gain, attn_w_o=attn_w_o,
             dn_w_in=dn_w_in, dn_conv=dn_conv, dn_a_log=dn_a_log, dn_dt_bias=dn_dt_bias, dn_o_gain=dn_o_gain, dn_w_o=dn_w_o,
             mlp_norm=mlp_norm, w_up=w_up, w_down=w_down, ple_norm=ple_norm, w_ple=w_ple, w_ple_gate=w_ple_gate)
    m = dict(mix_norm=m_mix_norm, attn_w_qkv=m_attn_w_qkv, attn_q_gain=m_attn_q_gain, attn_k_gain=m_attn_k_gain,
             attn_w_o=m_attn_w_o, dn_w_in=m_dn_w_in, dn_conv=m_dn_conv, dn_a_log=m_dn_a_log, dn_dt_bias=m_dn_dt_bias,
             dn_o_gain=m_dn_o_gain, dn_w_o=m_dn_w_o, mlp_norm=m_mlp_norm, w_up=m_w_up, w_down=m_w_down,
             ple_norm=m_ple_norm, w_ple=m_w_ple, w_ple_gate=m_w_ple_gate)
    v = dict(mix_norm=v_mix_norm, attn_w_qkv=v_attn_w_qkv, attn_q_gain=v_attn_q_gain, attn_k_gain=v_attn_k_gain,
             attn_w_o=v_attn_w_o, dn_w_in=v_dn_w_in, dn_conv=v_dn_conv, dn_a_log=v_dn_a_log, dn_dt_bias=v_dn_dt_bias,
             dn_o_gain=v_dn_o_gain, dn_w_o=v_dn_w_o, mlp_norm=v_mlp_norm, w_up=v_w_up, w_down=v_w_down,
             ple_norm=v_ple_norm, w_ple=v_w_ple, w_ple_gate=v_w_ple_gate)
    S = x.shape[1]

    segs = []
    for n, _ in BIG:
        if n == "dn_conv":
            segs.append(lax.bitcast_convert_type(w[n].reshape(-1), BF16).reshape(-1))
        else:
            segs.append(w[n].astype(BF16).reshape(-1))
    gathered = _all_gather("gather_weights", _to_rows(jnp.concatenate(segs), 16)).reshape(N_DEV, -1)
    W, off = {}, 0
    for n, axis in BIG:
        sz = math.prod(w[n].shape)
        if n == "dn_conv":
            bits = gathered[:, off:off + 2 * sz].reshape(N_DEV, sz, 2)
            W[n] = _join_from_devices(lax.bitcast_convert_type(bits, F32), w[n].shape, axis)[0]
            off += 2 * sz
        else:
            W[n] = _join_from_devices(gathered[:, off:off + sz], w[n].shape, axis)
            off += sz
    w_in = W.pop("dn_w_in")[0]
    W["dn_w_qkvz"] = jnp.concatenate([w_in[:, :DN_QKV], w_in[:, DN_QKV + 2 * DN_HEADS:]], axis=1)
    W["dn_w_ab"] = jnp.pad(w_in[:, DN_QKV:DN_QKV + 2 * DN_HEADS], ((0, 0), (0, LANE - 2 * DN_HEADS)))
    for n in ("attn_w_qkv", "attn_w_o", "dn_w_o"):
        W[n] = W[n][0]
    P = dict(mix_norm=mix_norm, attn_q_gain=attn_q_gain[0], attn_k_gain=attn_k_gain[0], dn_a_log=dn_a_log[0],
             dn_dt_bias=dn_dt_bias[0], dn_o_gain=dn_o_gain[0], mlp_norm=mlp_norm, ple_norm=ple_norm)

    sq, dx0, grads = _local_step(x[0], p[:, 0], positions.reshape(S, 1), loss_target[0], W, P)

    send = _to_rows(jnp.concatenate([_split_for_devices(grads[n].astype(BF16), axis) for n, axis in BIG], axis=1), FLAT_TILE)
    parts = _all_to_all("exchange_grads", send)
    big = _adamw("adamw_big", parts, _pack_shards(w), _pack_shards(m), _pack_shards(v), FLAT_TILE)
    big = [_unpack_shards(b, w) for b in big]

    loss_row = jnp.pad((0.5 / D_MODEL) * jnp.sum(sq, axis=1, keepdims=True), ((0, 0), (0, LANE - 1)))
    small_like = {n: w[n] for n in SMALL}
    parts_s = _all_gather("gather_small", _pack_small({n: grads[n] for n in SMALL}, loss_row))
    zero_row = jnp.zeros((1, LANE), F32)
    small = _adamw("adamw_small", parts_s, _pack_small(w, zero_row), _pack_small(m, zero_row), _pack_small(v, zero_row), SMALL_ROWS)
    loss = small[0][SMALL_ROWS - 1, 0]
    small = [_unpack_small(b, small_like)[0] for b in small]

    outs = [loss, dx0[None]]
    for k in range(4):
        for n in WEIGHTS:
            outs.append(small[k][n] if n in SMALL else big[k][n])
    return tuple(outs)
```

```python
import functools
import math

import jax
import jax.numpy as jnp
from jax import lax
from jax.experimental import pallas as pl
from jax.experimental.pallas import tpu as pltpu

F32 = jnp.float32
BF16 = jnp.bfloat16
HIGHEST = lax.Precision.HIGHEST

N_DEV = 8
D_MODEL = 1024
EPS = 1e-6
SWA_GROUPS = ((128, 1), (512, 4), (2048, 16))
A_HEADS = 8
A_HEAD_DIM = 64
A_WIDTH = A_HEADS * A_HEAD_DIM
A_QKV = 3 * 3 * A_WIDTH
ROPE_DIM = 16
ROPE_HALF = 8
ROPE_THETA = 500000.0
BAND = 128
DN_HEADS = 8
DN_DIM = 128
DN_WIDTH = DN_HEADS * DN_DIM
CONV_W = 4
CHUNK = 64
D_FF = 4 * D_MODEL
PLE_DIM = 256
LR, B1, B2, ADAM_EPS, WD, STEP = 0.001, 0.9, 0.999, 1e-08, 0.01, 10

VMEM_LIMIT = 56 * 1024 * 1024
MXU_TILE = 1024
LANE = 128
SUBLANE = 8


def _cparams(sem):
    return pltpu.CompilerParams(dimension_semantics=sem, vmem_limit_bytes=VMEM_LIMIT)


def _tile(n, pref):
    if n <= pref:
        return n
    t = (pref // LANE) * LANE
    while t >= LANE:
        if n % t == 0:
            return t
        t -= LANE
    raise ValueError(f"no tile for {n}")


def _dot(a, b, ca=1, cb=0, precision=None):
    return lax.dot_general(a, b, (((ca,), (cb,)), ((), ())), precision=precision,
                           preferred_element_type=F32)


def _bdot(a, b, ca=1, cb=0):
    return _dot(a.astype(BF16), b.astype(BF16), ca, cb)


def _mm(name, a, b, *, ta=False, tb=False, epilogue=None, extras=(), out_dtypes=(F32,),
        tm_pref=MXU_TILE, tn_pref=1536, tk_pref=MXU_TILE):
    M, K = (a.shape[1], a.shape[0]) if ta else a.shape
    N = b.shape[0] if tb else b.shape[1]
    assert (b.shape[1] if tb else b.shape[0]) == K
    tm, tn, tk = _tile(M, tm_pref), _tile(N, tn_pref), _tile(K, tk_pref)
    nk = K // tk
    n_out = len(out_dtypes)
    n_ext = len(extras)

    def body(*refs):
        a_ref, b_ref = refs[0], refs[1]
        ext = refs[2:2 + n_ext]
        outs = refs[2 + n_ext:2 + n_ext + n_out]
        acc = refs[-1]
        k = pl.program_id(2)

        @pl.when(k == 0)
        def _():
            acc[...] = jnp.zeros_like(acc)

        acc[...] += _bdot(a_ref[...], b_ref[...], 0 if ta else 1, 1 if tb else 0)

        @pl.when(k == nk - 1)
        def _():
            r = acc[...]
            res = (r,) if epilogue is None else epilogue(r, *[e[...] for e in ext])
            for o, v in zip(outs, res):
                o[...] = v.astype(o.dtype)

    a_spec = pl.BlockSpec((tk, tm), lambda i, j, k: (k, i)) if ta else pl.BlockSpec((tm, tk), lambda i, j, k: (i, k))
    b_spec = pl.BlockSpec((tn, tk), lambda i, j, k: (j, k)) if tb else pl.BlockSpec((tk, tn), lambda i, j, k: (k, j))
    ext_specs = []
    for e in extras:
        if e.shape[0] == 1 and M != 1:
            ext_specs.append(pl.BlockSpec((1, tn), lambda i, j, k: (0, j)))
        else:
            ext_specs.append(pl.BlockSpec((tm, tn), lambda i, j, k: (i, j)))
    out = pl.pallas_call(
        body, name=name,
        grid=(M // tm, N // tn, nk),
        in_specs=[a_spec, b_spec] + ext_specs,
        out_specs=[pl.BlockSpec((tm, tn), lambda i, j, k: (i, j)) for _ in range(n_out)],
        out_shape=[jax.ShapeDtypeStruct((M, N), dt) for dt in out_dtypes],
        scratch_shapes=[pltpu.VMEM((tm, tn), F32)],
        compiler_params=_cparams(("parallel", "parallel", "arbitrary")),
    )(a, b, *extras)
    return out[0] if n_out == 1 else tuple(out)


def _rows(name, fn, ins, outs, *, tr, accs=()):
    ins = [(e[0], e[1]) + (e[2] if len(e) > 2 else (0, e[0].shape[-1])) for e in ins]
    n_rows = next(e[0].shape[0] for e in ins if e[1] == "row")
    assert n_rows % tr == 0 and tr % SUBLANE == 0
    steps = n_rows // tr
    t8 = tr // SUBLANE
    n8 = n_rows // SUBLANE
    n_in, n_out, n_acc = len(ins), len(outs), len(accs)

    def body(*refs):
        i = pl.program_id(0)
        vals = fn(i, steps, *[r[...] for r in refs[:n_in]])
        if not isinstance(vals, (tuple, list)):
            vals = (vals,)
        assert len(vals) == n_out + n_acc
        for o, v in zip(refs[n_in:n_in + n_out], vals[:n_out]):
            o[...] = v.astype(o.dtype)
        if n_acc:
            acc_refs = refs[n_in + n_out:]

            @pl.when(i == 0)
            def _():
                for r in acc_refs:
                    r[...] = jnp.zeros_like(r)

            for r, v in zip(acc_refs, vals[n_out:]):
                r[...] += v.astype(r.dtype)

    in_specs = []
    for a, kind, cb, c in ins:
        if kind == "row":
            in_specs.append(pl.BlockSpec((tr, c), lambda i, cb=cb: (i, cb)))
        elif kind == "full":
            in_specs.append(pl.BlockSpec(a.shape, lambda i, z=(0,) * a.ndim: z))
        elif kind == "prev8":
            in_specs.append(pl.BlockSpec((SUBLANE, c), lambda i, cb=cb: (jnp.maximum(i * t8 - 1, 0), cb)))
        elif kind == "next8":
            in_specs.append(pl.BlockSpec((SUBLANE, c), lambda i, cb=cb: (jnp.minimum((i + 1) * t8, n8 - 1), cb)))
        else:
            raise ValueError(kind)
    out_specs = [pl.BlockSpec((tr, c), lambda i: (i, 0)) for c, _ in outs]
    out_specs += [pl.BlockSpec(s, lambda i, z=(0,) * len(s): z) for s, _ in accs]
    out_shape = [jax.ShapeDtypeStruct((n_rows, c), dt) for c, dt in outs]
    out_shape += [jax.ShapeDtypeStruct(s, dt) for s, dt in accs]
    res = pl.pallas_call(
        body, name=name, grid=(steps,), in_specs=in_specs, out_specs=out_specs, out_shape=out_shape,
        compiler_params=_cparams(("arbitrary",) if n_acc else ("parallel",)),
    )(*[e[0] for e in ins])
    return res[0] if len(res) == 1 else tuple(res)


def _colsum(x):
    return jnp.sum(x, axis=0, keepdims=True)


def _sum_all(x):
    return jnp.sum(jnp.sum(x, axis=1, keepdims=True), axis=0, keepdims=True)


def _rmsnorm_fwd(name, x, gain):
    def fn(i, n, xt, g):
        r = lax.rsqrt(jnp.mean(xt * xt, axis=-1, keepdims=True) + EPS)
        return (xt * r * g,)
    return _rows(name, fn, [(x, "row"), (gain, "full")], [(x.shape[1], BF16)], tr=512)


def _rmsnorm_bwd(name, x, gain, dh, dres):
    def fn(i, n, xt, g, dht, drt):
        r = lax.rsqrt(jnp.mean(xt * xt, axis=-1, keepdims=True) + EPS)
        xh = xt * r
        dxn = dht * g
        dx = r * (dxn - xh * jnp.mean(dxn * xh, axis=-1, keepdims=True))
        return drt + dx, _colsum(dht * xh)
    D = x.shape[1]
    return _rows(name, fn, [(x, "row"), (gain, "full"), (dh, "row"), (dres, "row")], [(D, F32)],
                 tr=256, accs=[((1, D), F32)])


def _head_consts():
    import numpy as np
    e = np.arange(A_WIDTH) % A_HEAD_DIM
    inv = (np.float32(ROPE_THETA) ** (-np.arange(0, ROPE_DIM, 2, dtype=np.float32) / np.float32(ROPE_DIM))).astype(np.float32)
    c = np.zeros((8, A_WIDTH), np.float32)
    c[0] = np.where(e < ROPE_DIM, inv[e % ROPE_HALF], 0.0)
    c[1] = np.where(e < ROPE_HALF, -1.0, np.where(e < ROPE_DIM, 1.0, 0.0))
    c[2] = (e < ROPE_HALF).astype(np.float32)
    c[3] = (e < ROPE_DIM).astype(np.float32)
    return jnp.asarray(c)


def _block_diag(scale):
    import numpy as np
    h = np.arange(A_WIDTH) // A_HEAD_DIM
    return jnp.asarray((h[:, None] == h[None, :]).astype(np.float32) * scale, dtype=BF16)


def _seg_sum(x, bd):
    hi = x.astype(BF16)
    lo = (x - hi.astype(F32)).astype(BF16)
    return _dot(hi, bd) + _dot(lo, bd)


def _rope_tables(positions, consts):
    def fn(i, n, pos, c):
        ang = pos.astype(F32) * c[0:1, :]
        return jnp.cos(ang), jnp.sin(ang) * c[1:2, :]
    return _rows("rope_tables", fn, [(positions, "row"), (consts, "full")],
                 [(A_WIDTH, F32), (A_WIDTH, F32)], tr=512)


def _rope_apply(y, ct, st, low):
    rolled = jnp.where(low, pltpu.roll(y, A_WIDTH - ROPE_HALF, 1), pltpu.roll(y, ROPE_HALF, 1))
    return y * ct + rolled * st


def _rope_apply_bwd(dout, ct, st, low, in16):
    t = dout * st
    back = jnp.where(low, pltpu.roll(t, A_WIDTH - ROPE_HALF, 1), jnp.where(in16, pltpu.roll(t, ROPE_HALF, 1), 0.0))
    return dout * ct + back


def _attn_prep(qkv, gains, ct, st, consts, bd):
    def fn(i, n, t, g, c_t, s_t, c, b):
        low = c[2:3, :] > 0.5
        cols = []
        for grp in range(3):
            for which in range(3):
                off = (grp * 3 + which) * A_WIDTH
                x = t[:, off:off + A_WIDTH]
                if which == 2:
                    cols.append(x.astype(BF16))
                    continue
                r = lax.rsqrt(_seg_sum(x * x, b) + EPS)
                y = x * r * g[grp * 2 + which:grp * 2 + which + 1, :]
                cols.append(_rope_apply(y, c_t, s_t, low).astype(BF16))
        return (jnp.concatenate(cols, axis=1),)
    return _rows("attn_prep", fn, [(qkv, "row"), (gains, "full"), (ct, "row"), (st, "row"), (consts, "full"), (bd, "full")],
                 [(A_QKV, BF16)], tr=256)


def _band_mask(n):
    row = lax.broadcasted_iota(jnp.int32, (BAND, 2 * BAND), 0)
    col = lax.broadcasted_iota(jnp.int32, (BAND, 2 * BAND), 1)
    dist = row + BAND - col
    return (dist >= 0) & (dist <= BAND) & ((col >= BAND) | (n > 0))


def _attn_fwd(qkvn, grp):
    S = qkvn.shape[0]
    d = SWA_GROUPS[grp][1]
    L = S // d
    nblk = L // BAND
    assert L % BAND == 0
    view = qkvn.reshape(L, d * A_QKV)
    base = 3 * grp

    def body(q_ref, kc_ref, kp_ref, vc_ref, vp_ref, o_ref, lse_ref):
        n = pl.program_id(1)
        valid = _band_mask(n)
        first = lax.broadcasted_iota(jnp.int32, (BAND, LANE), 1) < A_HEAD_DIM
        o_cols, l_cols = [], []
        for pr in range(A_WIDTH // LANE):
            sl = slice(pr * LANE, (pr + 1) * LANE)
            qp = q_ref[:, sl]
            kcat = jnp.concatenate([kp_ref[:, sl], kc_ref[:, sl]], axis=0)
            vcat = jnp.concatenate([vp_ref[:, sl], vc_ref[:, sl]], axis=0)
            res = []
            for m in (first, jnp.logical_not(first)):
                s = _dot(jnp.where(m, qp, jnp.zeros_like(qp)), kcat, 1, 1) * (A_HEAD_DIM ** -0.5)
                s = jnp.where(valid, s, -1e30)
                mx = jnp.max(s, axis=-1, keepdims=True)
                e = jnp.exp(s - mx)
                l = jnp.sum(e, axis=-1, keepdims=True)
                res.append((_dot((e / l).astype(BF16), vcat), mx + jnp.log(l)))
            o_cols.append(jnp.where(first, res[0][0], res[1][0]))
            l_cols.append(jnp.where(first, res[0][1], res[1][1]))
        o_ref[...] = jnp.concatenate(o_cols, axis=1)
        lse_ref[...] = jnp.concatenate(l_cols, axis=1)

    blk = (BAND, A_WIDTH)
    o, lse = pl.pallas_call(
        body, name=f"attn_fwd_g{grp}", grid=(d, nblk),
        in_specs=[pl.BlockSpec(blk, lambda r, n: (n, r * 9 + base)),
                  pl.BlockSpec(blk, lambda r, n: (n, r * 9 + base + 1)),
                  pl.BlockSpec(blk, lambda r, n: (jnp.maximum(n - 1, 0), r * 9 + base + 1)),
                  pl.BlockSpec(blk, lambda r, n: (n, r * 9 + base + 2)),
                  pl.BlockSpec(blk, lambda r, n: (jnp.maximum(n - 1, 0), r * 9 + base + 2))],
        out_specs=[pl.BlockSpec(blk, lambda r, n: (n, r)), pl.BlockSpec(blk, lambda r, n: (n, r))],
        out_shape=[jax.ShapeDtypeStruct((L, d * A_WIDTH), F32)] * 2,
        compiler_params=_cparams(("parallel", "parallel")),
    )(view, view, view, view, view)
    return o.reshape(S, A_WIDTH), lse.reshape(S, A_WIDTH)


def _merge_weights(l0, l1, l2):
    mx = jnp.maximum(jnp.maximum(l0, l1), l2)
    e0, e1, e2 = jnp.exp(l0 - mx), jnp.exp(l1 - mx), jnp.exp(l2 - mx)
    inv = 1.0 / (e0 + e1 + e2)
    return e0 * inv, e1 * inv, e2 * inv


def _attn_merge(os_, lses):
    def fn(i, n, o0, o1, o2, l0, l1, l2):
        w0, w1, w2 = _merge_weights(l0, l1, l2)
        return (w0 * o0 + w1 * o1 + w2 * o2,)
    ins = [(a, "row") for a in (*os_, *lses)]
    return _rows("attn_merge", fn, ins, [(A_WIDTH, BF16)], tr=512)


def _attn_merge_bwd(do, os_, lses, bd1):
    def fn(i, n, dot_, o0, o1, o2, l0, l1, l2, b):
        w0, w1, w2 = _merge_weights(l0, l1, l2)
        o = w0 * o0 + w1 * o1 + w2 * o2
        dsum = _seg_sum(dot_ * o, b)
        return (w0 * dot_, w1 * dot_, w2 * dot_, -w0 * dsum, -w1 * dsum, -w2 * dsum)
    ins = [(do, "row")] + [(a, "row") for a in (*os_, *lses)] + [(bd1, "full")]
    res = _rows("attn_merge_bwd", fn, ins, [(A_WIDTH, BF16)] * 3 + [(A_WIDTH, F32)] * 3, tr=256)
    return res[:3], res[3:]


def _lane_pick(x, lane_idx, lane):
    return jnp.sum(jnp.where(lane_idx == lane, x, 0.0), axis=-1, keepdims=True)


def _attn_bwd(qkvn, grp, do_g, lse, c_g):
    S = qkvn.shape[0]
    d = SWA_GROUPS[grp][1]
    L = S // d
    nblk = L // BAND
    view = qkvn.reshape(L, d * A_QKV)
    base = 3 * grp
    dov, lsev, cv = (t.reshape(L, d * A_WIDTH) for t in (do_g, lse, c_g))

    def body(q_ref, kc_ref, kp_ref, vc_ref, vp_ref, do_ref, lse_ref, c_ref, dq_ref, dk_ref, dv_ref, ck, cv_):
        n = pl.program_id(1)

        @pl.when(n == 0)
        def _():
            ck[...] = jnp.zeros_like(ck)
            cv_[...] = jnp.zeros_like(cv_)

        @pl.when(n < nblk)
        def _():
            valid = _band_mask(n)
            lane = lax.broadcasted_iota(jnp.int32, (BAND, LANE), 1)
            first = lane < A_HEAD_DIM
            lane2 = lax.broadcasted_iota(jnp.int32, (2 * BAND, LANE), 1) < A_HEAD_DIM
            for pr in range(A_WIDTH // LANE):
                sl = slice(pr * LANE, (pr + 1) * LANE)
                qp = q_ref[:, sl]
                dop = do_ref[:, sl]
                kcat = jnp.concatenate([kp_ref[:, sl], kc_ref[:, sl]], axis=0)
                vcat = jnp.concatenate([vp_ref[:, sl], vc_ref[:, sl]], axis=0)
                lsep = lse_ref[:, sl]
                cp = c_ref[:, sl]
                res = []
                for hh, m in enumerate((first, jnp.logical_not(first))):
                    lse_h = _lane_pick(lsep, lane, hh * A_HEAD_DIM)
                    c_h = _lane_pick(cp, lane, hh * A_HEAD_DIM)
                    s = _dot(jnp.where(m, qp, jnp.zeros_like(qp)), kcat, 1, 1) * (A_HEAD_DIM ** -0.5)
                    p = jnp.where(valid, jnp.exp(s - lse_h), 0.0)
                    dp = _dot(jnp.where(m, dop, jnp.zeros_like(dop)), vcat, 1, 1)
                    ds = (p * (dp + c_h) * (A_HEAD_DIM ** -0.5)).astype(BF16)
                    pb = p.astype(BF16)
                    res.append((_dot(ds, kcat), _dot(ds, qp, 0, 0), _dot(pb, dop, 0, 0)))
                dq_ref[:, sl] = jnp.where(first, res[0][0], res[1][0])
                dkc = jnp.where(lane2, res[0][1], res[1][1])
                dvc = jnp.where(lane2, res[0][2], res[1][2])
                dk_ref[:, sl] = ck[:, sl] + dkc[:BAND]
                dv_ref[:, sl] = cv_[:, sl] + dvc[:BAND]
                ck[:, sl] = dkc[BAND:]
                cv_[:, sl] = dvc[BAND:]

        @pl.when(n == nblk)
        def _():
            dk_ref[...] = ck[...]
            dv_ref[...] = cv_[...]

    blk = (BAND, A_WIDTH)
    last = nblk - 1
    qn = lambda n: jnp.minimum(n, last)
    pn = lambda n: jnp.clip(n - 1, 0, last)
    dq, dk, dv = pl.pallas_call(
        body, name=f"attn_bwd_g{grp}", grid=(d, nblk + 1),
        in_specs=[pl.BlockSpec(blk, lambda r, n: (qn(n), r * 9 + base)),
                  pl.BlockSpec(blk, lambda r, n: (qn(n), r * 9 + base + 1)),
                  pl.BlockSpec(blk, lambda r, n: (pn(n), r * 9 + base + 1)),
                  pl.BlockSpec(blk, lambda r, n: (qn(n), r * 9 + base + 2)),
                  pl.BlockSpec(blk, lambda r, n: (pn(n), r * 9 + base + 2)),
                  pl.BlockSpec(blk, lambda r, n: (qn(n), r)),
                  pl.BlockSpec(blk, lambda r, n: (qn(n), r)),
                  pl.BlockSpec(blk, lambda r, n: (qn(n), r))],
        out_specs=[pl.BlockSpec(blk, lambda r, n: (qn(n), r)),
                   pl.BlockSpec(blk, lambda r, n: (pn(n), r)),
                   pl.BlockSpec(blk, lambda r, n: (pn(n), r))],
        out_shape=[jax.ShapeDtypeStruct((L, d * A_WIDTH), F32)] * 3,
        scratch_shapes=[pltpu.VMEM(blk, F32), pltpu.VMEM(blk, F32)],
        compiler_params=_cparams(("parallel", "arbitrary")),
    )(view, view, view, view, view, dov, lsev, cv)
    return tuple(t.reshape(S, A_WIDTH) for t in (dq, dk, dv))


def _attn_prep_bwd(qkv, grads, gains, ct, st, consts, bd):
    def fn(i, n, t, g, c_t, s_t, c, b, *gr):
        low = c[2:3, :] > 0.5
        in16 = c[3:4, :] > 0.5
        cols, dgs = [], []
        for grp in range(3):
            for which in range(3):
                dout = gr[grp * 3 + which]
                if which == 2:
                    cols.append(dout.astype(BF16))
                    continue
                off = (grp * 3 + which) * A_WIDTH
                x = t[:, off:off + A_WIDTH]
                gain = g[grp * 2 + which:grp * 2 + which + 1, :]
                r = lax.rsqrt(_seg_sum(x * x, b) + EPS)
                xh = x * r
                dy = _rope_apply_bwd(dout, c_t, s_t, low, in16)
                dyn = dy * gain
                dx = r * (dyn - xh * _seg_sum(dyn * xh, b))
                cols.append(dx.astype(BF16))
                dgs.append(_colsum(dy * xh))
        return (jnp.concatenate(cols, axis=1), *dgs)
    ins = [(qkv, "row"), (gains, "full"), (ct, "row"), (st, "row"), (consts, "full"), (bd, "full")] + [(a, "row") for a in grads]
    res = _rows("attn_prep_bwd", fn, ins, [(A_QKV, BF16)], tr=128, accs=[((1, A_WIDTH), F32)] * 6)
    return res[0], res[1:]


DN_QKV = 3 * DN_WIDTH
DN_QKVZ = DN_QKV + DN_WIDTH


def _sigmoid(x):
    return 1.0 / (1.0 + jnp.exp(-x))


def _softplus(x):
    return jnp.maximum(x, 0.0) + jnp.log(1.0 + jnp.exp(-jnp.abs(x)))


def _conv_taps(xs, w, tr):
    acc = None
    for j in range(CONV_W):
        sh = CONV_W - 1 - j
        term = (pltpu.roll(xs, sh, 0) if sh else xs)[SUBLANE:] * w[j:j + 1, :]
        acc = term if acc is None else acc + term
    return acc


def _dn_prep(qkvz, ab, convw, alog_row, dt_row):
    tr = 256

    def fn(i, n, x, xp, abt, w, al, dt):
        xp = jnp.where(i > 0, xp, 0.0)
        u = _conv_taps(jnp.concatenate([xp, x], axis=0), w, tr)
        y = u * _sigmoid(u)
        qs, ks = [], []
        for h in range(DN_HEADS):
            for dst, base, sc in ((qs, 0, DN_DIM ** -0.5), (ks, DN_WIDTH, 1.0)):
                seg = y[:, base + h * DN_DIM:base + (h + 1) * DN_DIM]
                dst.append(seg * (lax.rsqrt(jnp.sum(seg * seg, axis=-1, keepdims=True) + EPS) * sc))
        lane = lax.broadcasted_iota(jnp.int32, abt.shape, 1)
        g = -jnp.exp(al) * _softplus(abt + dt)
        gb = jnp.where(lane < DN_HEADS, g, jnp.where(lane < 2 * DN_HEADS, _sigmoid(abt), 0.0))
        return u, jnp.concatenate(qs, axis=1), jnp.concatenate(ks, axis=1), y[:, 2 * DN_WIDTH:], gb

    ins = [(qkvz, "row", (0, DN_QKV)), (qkvz, "prev8", (0, DN_QKV)), (ab, "row"), (convw, "full"),
           (alog_row, "full"), (dt_row, "full")]
    return _rows("dn_prep", fn, ins, [(DN_QKV, F32), (DN_WIDTH, F32), (DN_WIDTH, F32), (DN_WIDTH, F32), (LANE, F32)], tr=tr)


def _tri_masks():
    row = lax.broadcasted_iota(jnp.int32, (CHUNK, CHUNK), 0)
    col = lax.broadcasted_iota(jnp.int32, (CHUNK, CHUNK), 1)
    return row >= col, row > col, row == col


def _heads(fn, *lists):
    return [fn(*xs) for xs in zip(*lists)]


def _unit_lower_inverse(a_list, eye):
    ts = [eye - a for a in a_list]
    aks = a_list
    for _ in range(5):
        aks = [_dot(ak, ak, precision=HIGHEST) for ak in aks]
        ts = [t + _dot(t, ak, precision=HIGHEST) for t, ak in zip(ts, aks)]
    return ts


def _dn_terms(qs, ks, vs, gb):
    lower, strict, diag = _tri_masks()
    lane = lax.broadcasted_iota(jnp.int32, (CHUNK, LANE), 1)
    is_last = lax.broadcasted_iota(jnp.int32, (CHUNK, 1), 0) == CHUNK - 1
    hs = range(DN_HEADS)
    gc = _dot(lower.astype(F32), gb, precision=HIGHEST)
    gct = jnp.transpose(gc)
    bcol = [_lane_pick(gb, lane, DN_HEADS + h) for h in hs]
    gcol = [_lane_pick(gc, lane, h) for h in hs]
    glast = [jnp.sum(jnp.where(is_last, g, 0.0), axis=0, keepdims=True) for g in gcol]
    decay = [jnp.exp(jnp.where(lower, gcol[h] - gct[h:h + 1, :], -1e30)) for h in hs]
    kb = _heads(lambda k, b: k * b, ks, bcol)
    kk = _heads(lambda x, k: _bdot(x, k, 1, 1), kb, ks)
    qk = _heads(lambda q, k: _bdot(q, k, 1, 1), qs, ks)
    a = _heads(lambda x, d: jnp.where(strict, x * d, 0.0), kk, decay)
    t = _unit_lower_inverse(a, diag.astype(F32))
    eg = [jnp.exp(g) for g in gcol]
    egl = _heads(lambda gl, g: jnp.exp(gl - g), glast, gcol)
    rhs_w = _heads(lambda x, e: x * e, kb, eg)
    u = _heads(lambda tt, v, b: _dot(tt, v * b, precision=HIGHEST), t, vs, bcol)
    w = _heads(lambda tt, r: _dot(tt, r, precision=HIGHEST), t, rhs_w)
    return dict(bcol=bcol, decay=decay, kb=kb, a=a, t=t, eg=eg, egl=egl, rhs_w=rhs_w, u=u, w=w,
                attn=_heads(lambda x, d: x * d, qk, decay), q_dec=_heads(lambda q, e: q * e, qs, eg),
                k_dec=_heads(lambda k, e: k * e, ks, egl), c_dec=[jnp.exp(g) for g in glast],
                lower=lower, strict=strict, lane=lane, is_last=is_last)


def _head_slices(ref):
    return [ref[:, h * DN_DIM:(h + 1) * DN_DIM] for h in range(DN_HEADS)]


def _dn_chunk_fwd(q, k, v, gb):
    S = q.shape[0]
    N = S // CHUNK

    def body(q_ref, k_ref, v_ref, gb_ref, o_ref, st_ref, state):
        @pl.when(pl.program_id(0) == 0)
        def _():
            state[...] = jnp.zeros_like(state)

        f = _dn_terms(_head_slices(q_ref), _head_slices(k_ref), _head_slices(v_ref), gb_ref[...])
        s = [state[h] for h in range(DN_HEADS)]
        for h in range(DN_HEADS):
            st_ref[0, h] = s[h]
        sb = [x.astype(BF16) for x in s]
        v_new = _heads(lambda u, w, x: u - _bdot(w, x), f["u"], f["w"], sb)
        o = _heads(lambda qd, x, at, vn: _bdot(qd, x) + _bdot(at, vn), f["q_dec"], sb, f["attn"], v_new)
        new_s = _heads(lambda x, c, kd, vn: x * c + _bdot(kd, vn, 0, 0), s, f["c_dec"], f["k_dec"], v_new)
        for h in range(DN_HEADS):
            o_ref[:, h * DN_DIM:(h + 1) * DN_DIM] = o[h]
            state[h] = new_s[h]

    blk = pl.BlockSpec((CHUNK, DN_WIDTH), lambda n: (n, 0))
    st_blk = pl.BlockSpec((1, DN_HEADS, DN_DIM, DN_DIM), lambda n: (n, 0, 0, 0))
    return pl.pallas_call(
        body, name="dn_chunk_fwd", grid=(N,),
        in_specs=[blk, blk, blk, pl.BlockSpec((CHUNK, LANE), lambda n: (n, 0))],
        out_specs=[blk, st_blk],
        out_shape=[jax.ShapeDtypeStruct((S, DN_WIDTH), F32), jax.ShapeDtypeStruct((N, DN_HEADS, DN_DIM, DN_DIM), F32)],
        scratch_shapes=[pltpu.VMEM((DN_HEADS, DN_DIM, DN_DIM), F32)],
        compiler_params=_cparams(("arbitrary",)),
    )(q, k, v, gb)


def _dn_chunk_bwd(q, k, v, gb, states, do):
    S = q.shape[0]
    N = S // CHUNK

    def body(q_ref, k_ref, v_ref, gb_ref, st_ref, do_ref, dq_ref, dk_ref, dv_ref, dgb_ref, dstate):
        @pl.when(pl.program_id(0) == 0)
        def _():
            dstate[...] = jnp.zeros_like(dstate)

        hs = range(DN_HEADS)
        qs, ks, vs, dos = (_head_slices(r) for r in (q_ref, k_ref, v_ref, do_ref))
        f = _dn_terms(qs, ks, vs, gb_ref[...])
        lane, is_last = f["lane"], f["is_last"]
        rowsum = lambda x: jnp.sum(x, axis=-1, keepdims=True)
        s = [st_ref[0, h] for h in hs]
        dsn = [dstate[h] for h in hs]
        sb = [x.astype(BF16) for x in s]
        dsb = [x.astype(BF16) for x in dsn]
        dob = [x.astype(BF16) for x in dos]
        v_new = _heads(lambda u, w, x: u - _bdot(w, x), f["u"], f["w"], sb)
        dv_new = _heads(lambda at, d, kd, x: _bdot(at, d, 0, 0) + _bdot(kd, x), f["attn"], dob, f["k_dec"], dsb)
        dattn = _heads(lambda d, vn: _bdot(d, vn, 1, 1), dob, v_new)
        dq_dec = _heads(lambda d, x: _bdot(d, x, 1, 1), dob, sb)
        dk_dec = _heads(lambda vn, x: _bdot(vn, x, 1, 1), v_new, dsb)
        dw = _heads(lambda dv_, x: -_bdot(dv_, x, 1, 1), dv_new, sb)
        new_ds = _heads(lambda x, c, qd, d, w, dv_: x * c + _bdot(qd, d, 0, 0) - _bdot(w, dv_, 0, 0),
                        dsn, f["c_dec"], f["q_dec"], dob, f["w"], dv_new)
        for h in hs:
            dstate[h] = new_ds[h]
        drhs_u = _heads(lambda tt, x: _dot(tt, x, 0, 0, precision=HIGHEST), f["t"], dv_new)
        drhs_w = _heads(lambda tt, x: _dot(tt, x, 0, 0, precision=HIGHEST), f["t"], dw)
        da = _heads(lambda du_, u, dw_, w: jnp.where(f["strict"], -(_bdot(du_, u, 1, 1) + _bdot(dw_, w, 1, 1)), 0.0),
                    drhs_u, f["u"], drhs_w, f["w"])
        dkk = _heads(lambda x, d: x * d, da, f["decay"])
        dqk = _heads(lambda x, d: x * d, dattn, f["decay"])
        dkb = _heads(lambda x, k_, dw_, e: _bdot(x, k_) + dw_ * e, dkk, ks, drhs_w, f["eg"])
        dq = _heads(lambda x, k_, dqd, e: _bdot(x, k_) + dqd * e, dqk, ks, dq_dec, f["eg"])
        dk = _heads(lambda x, kb_, y, q_, dkd, el, dkb_, b: _bdot(x, kb_, 0, 0) + _bdot(y, q_, 0, 0) + dkd * el + dkb_ * b,
                    dkk, f["kb"], dqk, qs, dk_dec, f["egl"], dkb, f["bcol"])
        m = _heads(lambda x, a_, y, at: x * a_ + y * at, da, f["a"], dattn, f["attn"])
        ones = jnp.ones((CHUNK, LANE), F32)
        col_m = [_dot(x, ones, 0, 0, precision=HIGHEST)[:, 0:1] for x in m]
        dgc_all = jnp.zeros((CHUNK, LANE), F32)
        dbeta_all = jnp.zeros((CHUNK, LANE), F32)
        for h in hs:
            dq_ref[:, h * DN_DIM:(h + 1) * DN_DIM] = dq[h]
            dk_ref[:, h * DN_DIM:(h + 1) * DN_DIM] = dk[h]
            dv_ref[:, h * DN_DIM:(h + 1) * DN_DIM] = drhs_u[h] * f["bcol"][h]
            kdec_term = rowsum(dk_dec[h] * f["k_dec"][h])
            dc_dec = _sum_all(dsn[h] * s[h])
            dgc = (rowsum(m[h]) - col_m[h] + rowsum(dq_dec[h] * f["q_dec"][h]) - kdec_term
                   + rowsum(drhs_w[h] * f["rhs_w"][h]))
            last_extra = jnp.sum(kdec_term, axis=0, keepdims=True) + dc_dec * f["c_dec"][h]
            dgc = dgc + jnp.where(is_last, last_extra, 0.0)
            dbeta = rowsum(drhs_u[h] * vs[h]) + rowsum(dkb[h] * ks[h])
            dgc_all = jnp.where(lane == h, dgc, dgc_all)
            dbeta_all = jnp.where(lane == DN_HEADS + h, dbeta, dbeta_all)
        dg_all = _dot(f["lower"].astype(F32), dgc_all, 0, 0, precision=HIGHEST)
        dgb_ref[...] = jnp.where(lane < DN_HEADS, dg_all, dbeta_all)

    rev = lambda n: (N - 1 - n, 0)
    blk = pl.BlockSpec((CHUNK, DN_WIDTH), rev)
    gblk = pl.BlockSpec((CHUNK, LANE), rev)
    st_blk = pl.BlockSpec((1, DN_HEADS, DN_DIM, DN_DIM), lambda n: (N - 1 - n, 0, 0, 0))
    return pl.pallas_call(
        body, name="dn_chunk_bwd", grid=(N,),
        in_specs=[blk, blk, blk, gblk, st_blk, blk],
        out_specs=[blk, blk, blk, gblk],
        out_shape=[jax.ShapeDtypeStruct((S, DN_WIDTH), F32)] * 3 + [jax.ShapeDtypeStruct((S, LANE), F32)],
        scratch_shapes=[pltpu.VMEM((DN_HEADS, DN_DIM, DN_DIM), F32)],
        compiler_params=_cparams(("arbitrary",)),
    )(q, k, v, gb, states, do)


def _dn_post(o, qkvz, gain_row):
    def fn(i, n, ot, z, g):
        cols = []
        for h in range(DN_HEADS):
            seg = ot[:, h * DN_DIM:(h + 1) * DN_DIM]
            cols.append(seg * lax.rsqrt(jnp.mean(seg * seg, axis=-1, keepdims=True) + EPS) * g)
        return (jnp.concatenate(cols, axis=1) * (z * _sigmoid(z)),)
    return _rows("dn_post", fn, [(o, "row"), (qkvz, "row", (3, DN_WIDTH)), (gain_row, "full")], [(DN_WIDTH, BF16)], tr=512)


def _dn_post_bwd(don, o, qkvz, gain_row):
    def fn(i, n, dy, ot, z, g):
        sg = _sigmoid(z)
        sz = z * sg
        dos, ohs = [], []
        dg = jnp.zeros((1, DN_DIM), F32)
        for h in range(DN_HEADS):
            sl = slice(h * DN_DIM, (h + 1) * DN_DIM)
            seg = ot[:, sl]
            r = lax.rsqrt(jnp.mean(seg * seg, axis=-1, keepdims=True) + EPS)
            oh = seg * r
            dno = dy[:, sl] * sz[:, sl]
            dg = dg + _colsum(dno * oh)
            dn = dno * g
            dos.append(r * (dn - oh * jnp.mean(dn * oh, axis=-1, keepdims=True)))
            ohs.append(oh * g)
        dz = dy * jnp.concatenate(ohs, axis=1) * (sg * (1.0 + z * (1.0 - sg)))
        return jnp.concatenate(dos, axis=1), dz, dg
    ins = [(don, "row"), (o, "row"), (qkvz, "row", (3, DN_WIDTH)), (gain_row, "full")]
    return _rows("dn_post_bwd", fn, ins, [(DN_WIDTH, F32), (DN_WIDTH, F32)], tr=256, accs=[((1, DN_DIM), F32)])


def _dn_prep_bwd(dq, dk, dv, dgb, u, ab, alog_row, dt_row):
    def fn(i, n, dqt, dkt, dvt, dgbt, ut, abt, al, dt):
        sg = _sigmoid(ut)
        y = ut * sg
        dys = []
        for grad, base, sc in ((dqt, 0, DN_DIM ** -0.5), (dkt, DN_WIDTH, 1.0)):
            for h in range(DN_HEADS):
                seg = y[:, base + h * DN_DIM:base + (h + 1) * DN_DIM]
                gr = grad[:, h * DN_DIM:(h + 1) * DN_DIM]
                r = lax.rsqrt(jnp.sum(seg * seg, axis=-1, keepdims=True) + EPS)
                xh = seg * r
                dys.append((r * sc) * (gr - xh * jnp.sum(gr * xh, axis=-1, keepdims=True)))
        dy = jnp.concatenate(dys + [dvt], axis=1)
        du = dy * (sg * (1.0 + ut * (1.0 - sg)))
        lane = lax.broadcasted_iota(jnp.int32, abt.shape, 1)
        is_g = lane < DN_HEADS
        ea = jnp.exp(al)
        x = abt + dt
        slope = -ea * _sigmoid(x)
        gval = -ea * _softplus(x)
        dg = jnp.where(is_g, dgbt, 0.0)
        beta = _sigmoid(abt)
        dab = jnp.where(is_g, dg * slope, jnp.where(lane < 2 * DN_HEADS, dgbt * beta * (1.0 - beta), 0.0))
        return du, dab, _colsum(dg * gval), _colsum(dg * slope)
    ins = [(dq, "row"), (dk, "row"), (dv, "row"), (dgb, "row"), (u, "row"), (ab, "row"), (alog_row, "full"), (dt_row, "full")]
    return _rows("dn_prep_bwd", fn, ins, [(DN_QKV, F32), (LANE, BF16)], tr=256, accs=[((1, LANE), F32)] * 2)


def _dn_conv_bwd(du, dz, qkvz, convw):
    tr = 256

    def fn(i, n, dut, dun, dzt, x, xp, w):
        dun = jnp.where(i < n - 1, dun, 0.0)
        dus = jnp.concatenate([dut, dun], axis=0)
        xs = jnp.concatenate([jnp.where(i > 0, xp, 0.0), x], axis=0)
        dx = None
        dws = []
        for j in range(CONV_W):
            sh = CONV_W - 1 - j
            term = (pltpu.roll(dus, tr + SUBLANE - sh, 0) if sh else dus)[:tr] * w[j:j + 1, :]
            dx = term if dx is None else dx + term
            dws.append(_colsum(dut * (pltpu.roll(xs, sh, 0) if sh else xs)[SUBLANE:]))
        return (jnp.concatenate([dx.astype(BF16), dzt.astype(BF16)], axis=1), *dws)

    ins = [(du, "row"), (du, "next8"), (dz, "row"), (qkvz, "row", (0, DN_QKV)), (qkvz, "prev8", (0, DN_QKV)), (convw, "full")]
    res = _rows("dn_conv_bwd", fn, ins, [(DN_QKVZ, BF16)], tr=tr, accs=[((1, DN_QKV), F32)] * CONV_W)
    return res[0], res[1:]


def _add(acc, r):
    return (r + acc,)


def _mlp_ple_fwd(i, x1, p_i, mlp_gain, ple_gain, w_up, w_down, w_ple, w_gate):
    hm = _rmsnorm_fwd(f"mlp_norm{i}", x1, mlp_gain)
    u, a = _mm(f"mlp_up{i}", hm, w_up, epilogue=lambda acc: (acc, jnp.square(jnp.maximum(acc, 0.0))),
               out_dtypes=(F32, BF16))
    x2 = _mm(f"mlp_down{i}", a, w_down, epilogue=_add, extras=(x1,))
    hp = _rmsnorm_fwd(f"ple_norm{i}", x2, ple_gain)
    pp = _mm(f"ple_proj{i}", p_i, w_ple)

    def gate_epilogue(acc, x2t, ppt):
        gate = _sigmoid(acc)
        return x2t + ppt * gate, gate

    x3, gate = _mm(f"ple_gate{i}", hp, w_gate, epilogue=gate_epilogue, extras=(x2, pp), out_dtypes=(F32, F32))
    return x3, dict(x1=x1, hm=hm, u=u, a=a, x2=x2, hp=hp, pp=pp, gate=gate, p=p_i)


def _mlp_ple_bwd(i, dx3, sv, mlp_gain, ple_gain, w_up, w_down, w_gate):
    def fn(_i, _n, d, g, pp):
        return d * g, d * pp * g * (1.0 - g)
    dpp, dzg = _rows(f"ple_gate_bwd{i}", fn, [(dx3, "row"), (sv["gate"], "row"), (sv["pp"], "row")],
                     [(D_MODEL, BF16), (D_MODEL, BF16)], tr=512)
    d_w_ple = _mm(f"ple_proj_dw{i}", sv["p"], dpp, ta=True, out_dtypes=(BF16,))
    d_w_gate = _mm(f"ple_gate_dw{i}", sv["hp"], dzg, ta=True, out_dtypes=(BF16,))
    dhp = _mm(f"ple_gate_dx{i}", dzg, w_gate, tb=True)
    dx2, d_ple_gain = _rmsnorm_bwd(f"ple_norm_bwd{i}", sv["x2"], ple_gain, dhp, dx3)
    d_w_down = _mm(f"mlp_down_dw{i}", sv["a"], dx2, ta=True, out_dtypes=(BF16,))
    du = _mm(f"mlp_down_dx{i}", dx2, w_down, tb=True, epilogue=lambda acc, ut: (acc * (2.0 * jnp.maximum(ut, 0.0)),),
             extras=(sv["u"],), out_dtypes=(BF16,))
    d_w_up = _mm(f"mlp_up_dw{i}", sv["hm"], du, ta=True, out_dtypes=(BF16,))
    dhm = _mm(f"mlp_up_dx{i}", du, w_up, tb=True)
    dx1, d_mlp_gain = _rmsnorm_bwd(f"mlp_norm_bwd{i}", sv["x1"], mlp_gain, dhm, dx2)
    return dx1, dict(w_ple=d_w_ple, w_ple_gate=d_w_gate, w_down=d_w_down, w_up=d_w_up,
                     ple_norm=d_ple_gain, mlp_norm=d_mlp_gain)


def _loss_fwd_bwd(y, target):
    D = y.shape[1]

    def fn(i, n, yt, tt):
        e = yt - tt
        return e * (1.0 / D), _colsum(e * e)
    dy, sq = _rows("loss", fn, [(y, "row"), (target, "row")], [(D, F32)], tr=512, accs=[((1, D), F32)])
    return sq, dy


def _local_step(x, p, positions, target, W, P):
    consts = _head_consts()
    bd = _block_diag(1.0 / A_HEAD_DIM)
    bd1 = _block_diag(1.0)
    ct, st = _rope_tables(positions, consts)
    gains = jnp.stack([jnp.tile(v, A_HEADS) for g in range(3) for v in (P["attn_q_gain"][g], P["attn_k_gain"][g])])
    pad = LANE - DN_HEADS
    alog_row = jnp.pad(P["dn_a_log"].reshape(1, DN_HEADS), ((0, 0), (0, pad)))
    dt_row = jnp.pad(P["dn_dt_bias"].reshape(1, DN_HEADS), ((0, 0), (0, pad)))
    ogain_row = P["dn_o_gain"].reshape(1, DN_DIM)
    row = lambda name, i: P[name][i:i + 1]

    h0 = _rmsnorm_fwd("mix_norm0", x, row("mix_norm", 0))
    qkv = _mm("attn_qkv", h0, W["attn_w_qkv"])
    qkvn = _attn_prep(qkv, gains, ct, st, consts, bd)
    os_, lses = zip(*[_attn_fwd(qkvn, g) for g in range(3)])
    o_attn = _attn_merge(os_, lses)
    x1 = _mm("attn_out", o_attn, W["attn_w_o"], epilogue=_add, extras=(x,))
    x3, sv0 = _mlp_ple_fwd(0, x1, p[0], row("mlp_norm", 0), row("ple_norm", 0),
                           W["w_up"][0], W["w_down"][0], W["w_ple"][0], W["w_ple_gate"][0])
    h1 = _rmsnorm_fwd("mix_norm1", x3, row("mix_norm", 1))
    qkvz = _mm("dn_in_qkvz", h1, W["dn_w_qkvz"])
    ab = _mm("dn_in_ab", h1, W["dn_w_ab"])
    u, q, k, v, gb = _dn_prep(qkvz, ab, W["dn_conv"], alog_row, dt_row)
    o_dn, states = _dn_chunk_fwd(q, k, v, gb)
    on = _dn_post(o_dn, qkvz, ogain_row)
    x4 = _mm("dn_out", on, W["dn_w_o"], epilogue=_add, extras=(x3,))
    x6, sv1 = _mlp_ple_fwd(1, x4, p[1], row("mlp_norm", 1), row("ple_norm", 1),
                           W["w_up"][1], W["w_down"][1], W["w_ple"][1], W["w_ple_gate"][1])
    sq, dy = _loss_fwd_bwd(x6, target)

    dx4, g1 = _mlp_ple_bwd(1, dy, sv1, row("mlp_norm", 1), row("ple_norm", 1),
                           W["w_up"][1], W["w_down"][1], W["w_ple_gate"][1])
    don = _mm("dn_out_dx", dx4, W["dn_w_o"], tb=True)
    d_dn_w_o = _mm("dn_out_dw", on, dx4, ta=True, out_dtypes=(BF16,))
    do_dn, dz, d_ogain = _dn_post_bwd(don, o_dn, qkvz, ogain_row)
    dq, dk, dv, dgb = _dn_chunk_bwd(q, k, v, gb, states, do_dn)
    du, dab, d_alog, d_dt = _dn_prep_bwd(dq, dk, dv, dgb, u, ab, alog_row, dt_row)
    dqkvz, d_conv = _dn_conv_bwd(du, dz, qkvz, W["dn_conv"])
    dh1 = _mm("dn_in_qkvz_dx", dqkvz, W["dn_w_qkvz"], tb=True)
    dh1 = _mm("dn_in_ab_dx", dab, W["dn_w_ab"], tb=True, epilogue=_add, extras=(dh1,))
    d_w_qkvz = _mm("dn_in_qkvz_dw", h1, dqkvz, ta=True, out_dtypes=(BF16,))
    d_w_ab = _mm("dn_in_ab_dw", h1, dab, ta=True, out_dtypes=(BF16,))
    dx3, d_mix1 = _rmsnorm_bwd("mix_norm_bwd1", x3, row("mix_norm", 1), dh1, dx4)
    dx1, g0 = _mlp_ple_bwd(0, dx3, sv0, row("mlp_norm", 0), row("ple_norm", 0),
                           W["w_up"][0], W["w_down"][0], W["w_ple_gate"][0])
    do_attn = _mm("attn_out_dx", dx1, W["attn_w_o"], tb=True)
    d_attn_w_o = _mm("attn_out_dw", o_attn, dx1, ta=True, out_dtypes=(BF16,))
    dos, cs = _attn_merge_bwd(do_attn, os_, lses, bd1)
    grads9 = []
    for g in range(3):
        grads9 += list(_attn_bwd(qkvn, g, dos[g], lses[g], cs[g]))
    dqkv, dgains = _attn_prep_bwd(qkv, grads9, gains, ct, st, consts, bd)
    dh0 = _mm("attn_qkv_dx", dqkv, W["attn_w_qkv"], tb=True)
    d_attn_w_qkv = _mm("attn_qkv_dw", h0, dqkv, ta=True, out_dtypes=(BF16,))
    dx0, d_mix0 = _rmsnorm_bwd("mix_norm_bwd0", x, row("mix_norm", 0), dh0, dx1)

    dg = jnp.stack([t.reshape(A_HEADS, A_HEAD_DIM).sum(0) for t in dgains])
    small = dict(
        mix_norm=jnp.concatenate([d_mix0, d_mix1], 0),
        attn_q_gain=dg[0::2][None], attn_k_gain=dg[1::2][None],
        dn_a_log=d_alog[:, :DN_HEADS], dn_dt_bias=d_dt[:, :DN_HEADS], dn_o_gain=d_ogain,
        mlp_norm=jnp.concatenate([g0["mlp_norm"], g1["mlp_norm"]], 0),
        ple_norm=jnp.concatenate([g0["ple_norm"], g1["ple_norm"]], 0),
    )
    big = dict(attn_w_qkv=d_attn_w_qkv, attn_w_o=d_attn_w_o, dn_w_qkvz=d_w_qkvz, dn_w_ab=d_w_ab,
               dn_conv=jnp.concatenate(d_conv, 0), dn_w_o=d_dn_w_o,
               w_up=[g0["w_up"], g1["w_up"]], w_down=[g0["w_down"], g1["w_down"]],
               w_ple=[g0["w_ple"], g1["w_ple"]], w_ple_gate=[g0["w_ple_gate"], g1["w_ple_gate"]])
    return sq, dx0, small, big


MESH_IDS = pl.DeviceIdType.MESH
ANY = pl.BlockSpec(memory_space=pl.ANY)


def _place():
    return lax.axis_index("x"), lax.axis_index("y"), lax.axis_index("c")


def _sem_scratch(n_streams):
    return [pltpu.SemaphoreType.DMA((n_streams, N_DEV - 1)), pltpu.SemaphoreType.DMA((n_streams, N_DEV - 1)),
            pltpu.SemaphoreType.DMA((n_streams,))]


def _all_gather(name, arrays, streams):
    n_in, n_st = len(arrays), len(streams)
    shapes = [arrays[a].shape if li is None else arrays[a].shape[1:] for a, li in streams]

    def body(*refs):
        in_refs, out_refs = refs[:n_in], refs[n_in:n_in + n_st]
        send_sems, recv_sems, local_sems = refs[n_in + n_st:]
        x, y, c = _place()
        me, sibling = (x, y, c), (x, y, 1 - c)
        chips = [(1 - x, y), (x, 1 - y), (1 - x, 1 - y)]

        def copy(s, k, block, to, own=False):
            a, li = streams[s]
            dst = out_refs[s].at[4 * block[0] + 2 * block[1] + block[2]]
            src = (in_refs[a] if li is None else in_refs[a].at[li]) if own else dst
            return pltpu.make_async_remote_copy(src_ref=src, dst_ref=dst, send_sem=send_sems.at[s, k],
                                                recv_sem=recv_sems.at[s, k], device_id=to, device_id_type=MESH_IDS)

        started = []
        for s, (a, li) in enumerate(streams):
            src = in_refs[a] if li is None else in_refs[a].at[li]
            mine = pltpu.make_async_copy(src, out_refs[s].at[4 * x + 2 * y + c], local_sems.at[s])
            mine.start()
            started.append(mine)
        sends = []
        for s in range(n_st):
            first = [copy(s, 0, me, sibling, own=True)]
            first += [copy(s, 1 + j, me, (*chip, c), own=True) for j, chip in enumerate(chips)]
            for cp in first:
                cp.start()
            sends += first
        for j, chip in enumerate(chips):
            for s in range(n_st):
                copy(s, 1 + j, (*chip, c), me).wait_recv()
                fwd = copy(s, 4 + j, (*chip, c), sibling)
                fwd.start()
                sends.append(fwd)
        for s in range(n_st):
            copy(s, 0, sibling, me).wait_recv()
            for j, chip in enumerate(chips):
                copy(s, 4 + j, (*chip, 1 - c), me).wait_recv()
        for cp in sends:
            cp.wait_send()
        for cp in started:
            cp.wait()

    return pl.pallas_call(
        body, name=name,
        out_shape=[jax.ShapeDtypeStruct((N_DEV,) + tuple(sh), arrays[a].dtype) for sh, (a, _) in zip(shapes, streams)],
        in_specs=[ANY] * n_in, out_specs=[ANY] * n_st, scratch_shapes=_sem_scratch(n_st),
    )(*arrays)


def _exchange(name, sends, recv_shapes, placement):
    n_st, n_out = len(sends), len(recv_shapes)

    def body(*refs):
        send_refs, recv_refs = refs[:n_st], refs[n_st:n_st + n_out]
        send_sems, recv_sems, local_sems = refs[n_st + n_out:]
        x, y, c = _place()
        me = 4 * x + 2 * y + c

        def landing(s, slot):
            r, off = placement[s]
            rows = sends[s].shape[1]
            if rows == recv_shapes[r][1]:
                return recv_refs[r].at[slot]
            return recv_refs[r].at[slot, pl.ds(off, rows)]

        local, copies, arrivals = [], [], []
        for s in range(n_st):
            cp = pltpu.make_async_copy(send_refs[s].at[me], landing(s, me), local_sems.at[s])
            cp.start()
            local.append(cp)
        for k in range(1, N_DEV):
            px = 1 - x if k & 4 else x
            py = 1 - y if k & 2 else y
            pc = 1 - c if k & 1 else c
            peer = 4 * px + 2 * py + pc
            for s in range(n_st):
                copies.append(pltpu.make_async_remote_copy(
                    src_ref=send_refs[s].at[peer], dst_ref=landing(s, me), send_sem=send_sems.at[s, k - 1],
                    recv_sem=recv_sems.at[s, k - 1], device_id=(px, py, pc), device_id_type=MESH_IDS))
                arrivals.append(pltpu.make_async_remote_copy(
                    src_ref=send_refs[s].at[peer], dst_ref=landing(s, peer), send_sem=send_sems.at[s, k - 1],
                    recv_sem=recv_sems.at[s, k - 1], device_id=(px, py, pc), device_id_type=MESH_IDS))
        for cp in copies:
            cp.start()
        for cp in arrivals:
            cp.wait_recv()
        for cp in copies:
            cp.wait_send()
        for cp in local:
            cp.wait()

    dtypes = {r: sends[s].dtype for s, (r, _) in enumerate(placement)}
    return pl.pallas_call(
        body, name=name, out_shape=[jax.ShapeDtypeStruct(sh, dtypes[r]) for r, sh in enumerate(recv_shapes)],
        in_specs=[ANY] * n_st, out_specs=[ANY] * n_out, scratch_shapes=_sem_scratch(n_st),
    )(*sends)


def _dn_in_pieces():
    n = (DN_QKVZ + 2 * DN_HEADS) // N_DEV
    segs = ((0, DN_QKV, 0, 0), (DN_QKV, DN_QKV + 2 * DN_HEADS, 1, 0), (DN_QKV + 2 * DN_HEADS, DN_QKVZ + 2 * DN_HEADS, 0, DN_QKV))
    out = []
    for d in range(N_DEV):
        lo, hi = d * n, (d + 1) * n
        for s0, s1, tgt, t0 in segs:
            a, b = max(lo, s0), min(hi, s1)
            if a < b:
                out.append((d, a - lo, b - lo, tgt, t0 + a - s0))
    return out


def _unpack_cols(name, g):
    _, K, n = g.shape
    tr = 256

    def body(g_ref, o_ref):
        for d in range(N_DEV):
            o_ref[:, d * n:(d + 1) * n] = g_ref[d]

    return pl.pallas_call(
        body, name=name, grid=(K // tr,), in_specs=[pl.BlockSpec((N_DEV, tr, n), lambda i: (0, i, 0))],
        out_specs=pl.BlockSpec((tr, N_DEV * n), lambda i: (i, 0)),
        out_shape=jax.ShapeDtypeStruct((K, N_DEV * n), g.dtype), compiler_params=_cparams(("parallel",)),
    )(g)


def _pack_cols(name, w):
    K, n = w.shape[0], w.shape[1] // N_DEV
    tr = 256

    def body(w_ref, o_ref):
        for d in range(N_DEV):
            o_ref[d] = w_ref[:, d * n:(d + 1) * n]

    return pl.pallas_call(
        body, name=name, grid=(K // tr,), in_specs=[pl.BlockSpec((tr, N_DEV * n), lambda i: (i, 0))],
        out_specs=pl.BlockSpec((N_DEV, tr, n), lambda i: (0, i, 0)),
        out_shape=jax.ShapeDtypeStruct((N_DEV, K, n), w.dtype), compiler_params=_cparams(("parallel",)),
    )(w)


def _unpack_dn_in(g):
    _, K, n = g.shape
    tr = 256

    def body(g_ref, qkvz_ref, ab_ref):
        ab_ref[...] = jnp.zeros_like(ab_ref)
        for d, c0, c1, tgt, t0 in _dn_in_pieces():
            (qkvz_ref, ab_ref)[tgt][:, t0:t0 + c1 - c0] = g_ref[d, :, c0:c1]

    return pl.pallas_call(
        body, name="unpack_dn_in", grid=(K // tr,), in_specs=[pl.BlockSpec((N_DEV, tr, n), lambda i: (0, i, 0))],
        out_specs=[pl.BlockSpec((tr, DN_QKVZ), lambda i: (i, 0)), pl.BlockSpec((tr, LANE), lambda i: (i, 0))],
        out_shape=[jax.ShapeDtypeStruct((K, DN_QKVZ), g.dtype), jax.ShapeDtypeStruct((K, LANE), g.dtype)],
        compiler_params=_cparams(("parallel",)),
    )(g)


def _pack_dn_in(d_qkvz, d_ab):
    K = d_qkvz.shape[0]
    n = (DN_QKVZ + 2 * DN_HEADS) // N_DEV
    tr = 256

    def body(qkvz_ref, ab_ref, o_ref):
        for d, c0, c1, tgt, t0 in _dn_in_pieces():
            o_ref[d, :, c0:c1] = (qkvz_ref, ab_ref)[tgt][:, t0:t0 + c1 - c0]

    return pl.pallas_call(
        body, name="pack_dn_in", grid=(K // tr,),
        in_specs=[pl.BlockSpec((tr, DN_QKVZ), lambda i: (i, 0)), pl.BlockSpec((tr, LANE), lambda i: (i, 0))],
        out_specs=pl.BlockSpec((N_DEV, tr, n), lambda i: (0, i, 0)),
        out_shape=jax.ShapeDtypeStruct((N_DEV, K, n), d_qkvz.dtype), compiler_params=_cparams(("parallel",)),
    )(d_qkvz, d_ab)


ADAMW_ROWS = 256


def _adamw(name, parts, w, m, v):
    R, C = w.shape
    tr = min(R, ADAMW_ROWS)
    assert R % tr == 0 and parts.shape == (N_DEV, R, C)
    c1 = 1.0 - B1 ** STEP
    c2 = 1.0 - B2 ** STEP

    def body(p_ref, w_ref, m_ref, v_ref, g_ref, d_ref, nm_ref, nv_ref):
        g = p_ref[0].astype(F32)
        for dev in range(1, N_DEV):
            g = g + p_ref[dev].astype(F32)
        nm = B1 * m_ref[...] + (1.0 - B1) * g
        nv = B2 * v_ref[...] + (1.0 - B2) * jnp.square(g)
        g_ref[...] = g
        nm_ref[...] = nm
        nv_ref[...] = nv
        d_ref[...] = -LR * ((nm / c1) / (jnp.sqrt(nv / c2) + ADAM_EPS) + WD * w_ref[...])

    blk = pl.BlockSpec((tr, C), lambda i: (i, 0))
    return pl.pallas_call(
        body, name=name, grid=(R // tr,),
        in_specs=[pl.BlockSpec((N_DEV, tr, C), lambda i: (0, i, 0)), blk, blk, blk],
        out_specs=[blk] * 4, out_shape=[jax.ShapeDtypeStruct((R, C), F32)] * 4,
        compiler_params=_cparams(("parallel",)),
    )(parts, w, m, v)


SMALL = ("mix_norm", "attn_q_gain", "attn_k_gain", "dn_a_log", "dn_dt_bias", "dn_o_gain", "mlp_norm", "ple_norm")
WEIGHTS = ("mix_norm", "attn_w_qkv", "attn_q_gain", "attn_k_gain", "attn_w_o", "dn_w_in", "dn_conv", "dn_a_log",
           "dn_dt_bias", "dn_o_gain", "dn_w_o", "mlp_norm", "w_up", "w_down", "ple_norm", "w_ple", "w_ple_gate")


def _to_rows(flat, multiple):
    n = flat.shape[-1]
    rows = -(-n // (LANE * multiple)) * multiple
    return jnp.pad(flat, [(0, rows * LANE - n)]).reshape(rows, LANE)


def _cols_to_devices(w):
    K, N = w.shape
    return jnp.transpose(w.reshape(K, N_DEV, N // N_DEV), (1, 0, 2))


def _cols_from_devices(g):
    _, K, n = g.shape
    return jnp.transpose(g, (1, 0, 2)).reshape(K, N_DEV * n)


SMALL_ROWS = 56


def _pack_small(vals, loss_row):
    rows = []
    for n in SMALL:
        rows.append(_to_rows(vals[n].reshape(-1), 1))
    rows.append(loss_row)
    buf = jnp.concatenate(rows, 0)
    assert buf.shape == (SMALL_ROWS, LANE)
    return buf


def _unpack_small(buf, like):
    out, r = {}, 0
    for n in SMALL:
        sz = math.prod(like[n].shape)
        nr = -(-sz // LANE)
        out[n] = buf[r:r + nr].reshape(-1)[:sz].reshape(like[n].shape)
        r += nr
    return out, buf[r]


def kernel(x, p, positions, mix_norm, attn_w_qkv, attn_q_gain, attn_k_gain, attn_w_o, dn_w_in, dn_conv, dn_a_log, dn_dt_bias, dn_o_gain, dn_w_o, mlp_norm, w_up, w_down, ple_norm, w_ple, w_ple_gate, loss_target, m_mix_norm, m_attn_w_qkv, m_attn_q_gain, m_attn_k_gain, m_attn_w_o, m_dn_w_in, m_dn_conv, m_dn_a_log, m_dn_dt_bias, m_dn_o_gain, m_dn_w_o, m_mlp_norm, m_w_up, m_w_down, m_ple_norm, m_w_ple, m_w_ple_gate, v_mix_norm, v_attn_w_qkv, v_attn_q_gain, v_attn_k_gain, v_attn_w_o, v_dn_w_in, v_dn_conv, v_dn_a_log, v_dn_dt_bias, v_dn_o_gain, v_dn_w_o, v_mlp_norm, v_w_up, v_w_down, v_ple_norm, v_w_ple, v_w_ple_gate):
    w = dict(mix_norm=mix_norm, attn_w_qkv=attn_w_qkv, attn_q_gain=attn_q_gain, attn_k_gain=attn_k_gain, attn_w_o=attn_w_o,
             dn_w_in=dn_w_in, dn_conv=dn_conv, dn_a_log=dn_a_log, dn_dt_bias=dn_dt_bias, dn_o_gain=dn_o_gain, dn_w_o=dn_w_o,
             mlp_norm=mlp_norm, w_up=w_up, w_down=w_down, ple_norm=ple_norm, w_ple=w_ple, w_ple_gate=w_ple_gate)
    m = dict(mix_norm=m_mix_norm, attn_w_qkv=m_attn_w_qkv, attn_q_gain=m_attn_q_gain, attn_k_gain=m_attn_k_gain,
             attn_w_o=m_attn_w_o, dn_w_in=m_dn_w_in, dn_conv=m_dn_conv, dn_a_log=m_dn_a_log, dn_dt_bias=m_dn_dt_bias,
             dn_o_gain=m_dn_o_gain, dn_w_o=m_dn_w_o, mlp_norm=m_mlp_norm, w_up=m_w_up, w_down=m_w_down,
             ple_norm=m_ple_norm, w_ple=m_w_ple, w_ple_gate=m_w_ple_gate)
    v = dict(mix_norm=v_mix_norm, attn_w_qkv=v_attn_w_qkv, attn_q_gain=v_attn_q_gain, attn_k_gain=v_attn_k_gain,
             attn_w_o=v_attn_w_o, dn_w_in=v_dn_w_in, dn_conv=v_dn_conv, dn_a_log=v_dn_a_log, dn_dt_bias=v_dn_dt_bias,
             dn_o_gain=v_dn_o_gain, dn_w_o=v_dn_w_o, mlp_norm=v_mlp_norm, w_up=v_w_up, w_down=v_w_down,
             ple_norm=v_ple_norm, w_ple=v_w_ple, w_ple_gate=v_w_ple_gate)
    S = x.shape[1]

    bf = lambda a: a.astype(BF16)
    shards = [bf(attn_w_qkv[0]), bf(attn_w_o[0]), bf(dn_w_in[0]), bf(dn_w_o[0]), bf(w_up), bf(w_down), bf(w_ple),
              bf(w_ple_gate), dn_conv[0]]
    streams = [(0, None), (1, None), (2, None), (3, None), (4, 0), (4, 1), (5, 0), (5, 1), (6, 0), (6, 1), (7, 0), (7, 1),
               (8, None)]
    (g_qkv, g_ao, g_in, g_do, g_up0, g_up1, g_dn0, g_dn1, g_pl0, g_pl1, g_gt0, g_gt1, g_conv) = _all_gather(
        "gather_weights", shards, streams)
    W = dict(
        attn_w_qkv=_unpack_cols("unpack_attn_qkv", g_qkv), attn_w_o=_cols_from_devices(g_ao),
        dn_conv=jnp.transpose(g_conv, (1, 0, 2)).reshape(CONV_W, DN_QKV), dn_w_o=g_do.reshape(DN_WIDTH, D_MODEL),
        w_up=[_cols_from_devices(g_up0), _cols_from_devices(g_up1)],
        w_down=[g_dn0.reshape(D_FF, D_MODEL), g_dn1.reshape(D_FF, D_MODEL)],
        w_ple=[_cols_from_devices(g_pl0), _cols_from_devices(g_pl1)],
        w_ple_gate=[g_gt0.reshape(D_MODEL, D_MODEL), g_gt1.reshape(D_MODEL, D_MODEL)])
    W["dn_w_qkvz"], W["dn_w_ab"] = _unpack_dn_in(g_in)
    P = dict(mix_norm=mix_norm, attn_q_gain=attn_q_gain[0], attn_k_gain=attn_k_gain[0], dn_a_log=dn_a_log[0],
             dn_dt_bias=dn_dt_bias[0], dn_o_gain=dn_o_gain[0], mlp_norm=mlp_norm, ple_norm=ple_norm)

    sq, dx0, small_g, big_g = _local_step(x[0], p[:, 0], positions.reshape(S, 1), loss_target[0], W, P)

    rows_to_devices = lambda t: t.reshape(N_DEV, t.shape[0] // N_DEV, t.shape[1])
    conv_send = jnp.transpose(big_g["dn_conv"].reshape(CONV_W, N_DEV, DN_QKV // N_DEV), (1, 0, 2))
    sends = [_pack_cols("pack_attn_qkv", big_g["attn_w_qkv"]), _cols_to_devices(big_g["attn_w_o"]),
             _pack_dn_in(big_g["dn_w_qkvz"], big_g["dn_w_ab"]), conv_send, rows_to_devices(big_g["dn_w_o"]),
             _cols_to_devices(big_g["w_up"][0]), _cols_to_devices(big_g["w_up"][1]),
             rows_to_devices(big_g["w_down"][0]), rows_to_devices(big_g["w_down"][1]),
             _cols_to_devices(big_g["w_ple"][0]), _cols_to_devices(big_g["w_ple"][1]),
             rows_to_devices(big_g["w_ple_gate"][0]), rows_to_devices(big_g["w_ple_gate"][1])]
    names = ("attn_w_qkv", "attn_w_o", "dn_w_in", "dn_conv", "dn_w_o", "w_up", "w_down", "w_ple", "w_ple_gate")
    shard2d = {n: (math.prod(w[n].shape[:-1]), w[n].shape[-1]) for n in names}
    placement = [(0, 0), (1, 0), (2, 0), (3, 0), (4, 0), (5, 0), (5, D_MODEL), (6, 0), (6, D_FF // N_DEV),
                 (7, 0), (7, PLE_DIM), (8, 0), (8, D_MODEL // N_DEV)]
    recvs = _exchange("exchange_grads", sends, [(N_DEV,) + shard2d[n] for n in names], placement)
    big = {}
    for n, parts in zip(names, recvs):
        res = _adamw("adamw_" + n, parts, w[n].reshape(shard2d[n]), m[n].reshape(shard2d[n]), v[n].reshape(shard2d[n]))
        big[n] = [r.reshape(w[n].shape) for r in res]

    loss_row = jnp.pad((0.5 / D_MODEL) * jnp.sum(sq, axis=1, keepdims=True), ((0, 0), (0, LANE - 1)))
    small_like = {n: w[n] for n in SMALL}
    parts_s = _all_gather("gather_small", [_pack_small(small_g, loss_row)], [(0, None)])[0]
    zero_row = jnp.zeros((1, LANE), F32)
    small = _adamw("adamw_small", parts_s, _pack_small(w, zero_row), _pack_small(m, zero_row), _pack_small(v, zero_row))
    loss = small[0][SMALL_ROWS - 1, 0]
    small = [_unpack_small(b, small_like)[0] for b in small]

    outs = [loss, dx0[None]]
    for k in range(4):
        for n in WEIGHTS:
            outs.append(small[k][n] if n in SMALL else big[n][k])
    return tuple(outs)
```

```python
import functools
import math

import jax
import jax.numpy as jnp
from jax import lax
from jax.experimental import pallas as pl
from jax.experimental.pallas import tpu as pltpu

F32 = jnp.float32
BF16 = jnp.bfloat16
HIGHEST = lax.Precision.HIGHEST

N_DEV = 8
D_MODEL = 1024
EPS = 1e-6
SWA_GROUPS = ((128, 1), (512, 4), (2048, 16))
A_HEADS = 8
A_HEAD_DIM = 64
A_WIDTH = A_HEADS * A_HEAD_DIM
A_QKV = 3 * 3 * A_WIDTH
ROPE_DIM = 16
ROPE_HALF = 8
ROPE_THETA = 500000.0
BAND = 128
DN_HEADS = 8
DN_DIM = 128
DN_WIDTH = DN_HEADS * DN_DIM
CONV_W = 4
CHUNK = 64
D_FF = 4 * D_MODEL
PLE_DIM = 256
LR, B1, B2, ADAM_EPS, WD, STEP = 0.001, 0.9, 0.999, 1e-08, 0.01, 10

VMEM_LIMIT = 56 * 1024 * 1024
MXU_TILE = 1024
LANE = 128
SUBLANE = 8


def _cparams(sem):
    return pltpu.CompilerParams(dimension_semantics=sem, vmem_limit_bytes=VMEM_LIMIT)


def _tile(n, pref):
    if n <= pref:
        return n
    t = (pref // LANE) * LANE
    while t >= LANE:
        if n % t == 0:
            return t
        t -= LANE
    raise ValueError(f"no tile for {n}")


def _dot(a, b, ca=1, cb=0, precision=None):
    return lax.dot_general(a, b, (((ca,), (cb,)), ((), ())), precision=precision,
                           preferred_element_type=F32)


def _bdot(a, b, ca=1, cb=0):
    return _dot(a.astype(BF16), b.astype(BF16), ca, cb)


def _mm(name, a, b, *, ta=False, tb=False, epilogue=None, extras=(), out_dtypes=(F32,),
        tm_pref=MXU_TILE, tn_pref=1536, tk_pref=MXU_TILE):
    M, K = (a.shape[1], a.shape[0]) if ta else a.shape
    N = b.shape[0] if tb else b.shape[1]
    assert (b.shape[1] if tb else b.shape[0]) == K
    tm, tn, tk = _tile(M, tm_pref), _tile(N, tn_pref), _tile(K, tk_pref)
    nk = K // tk
    n_out = len(out_dtypes)
    n_ext = len(extras)

    def body(*refs):
        a_ref, b_ref = refs[0], refs[1]
        ext = refs[2:2 + n_ext]
        outs = refs[2 + n_ext:2 + n_ext + n_out]
        k = pl.program_id(2)
        prod = _bdot(a_ref[...], b_ref[...], 0 if ta else 1, 1 if tb else 0)

        def finish(r):
            res = (r,) if epilogue is None else epilogue(r, *[e[...] for e in ext])
            for o, v in zip(outs, res):
                o[...] = v.astype(o.dtype)

        if nk == 1:
            finish(prod)
            return
        acc = refs[-1]

        @pl.when(k == 0)
        def _():
            acc[...] = prod

        @pl.when((k > 0) & (k < nk - 1))
        def _():
            acc[...] += prod

        @pl.when(k == nk - 1)
        def _():
            finish(acc[...] + prod)

    a_spec = pl.BlockSpec((tk, tm), lambda i, j, k: (k, i)) if ta else pl.BlockSpec((tm, tk), lambda i, j, k: (i, k))
    b_spec = pl.BlockSpec((tn, tk), lambda i, j, k: (j, k)) if tb else pl.BlockSpec((tk, tn), lambda i, j, k: (k, j))
    ext_specs = []
    for e in extras:
        if e.shape[0] == 1 and M != 1:
            ext_specs.append(pl.BlockSpec((1, tn), lambda i, j, k: (0, j)))
        else:
            ext_specs.append(pl.BlockSpec((tm, tn), lambda i, j, k: (i, j)))
    out = pl.pallas_call(
        body, name=name,
        grid=(M // tm, N // tn, nk),
        in_specs=[a_spec, b_spec] + ext_specs,
        out_specs=[pl.BlockSpec((tm, tn), lambda i, j, k: (i, j)) for _ in range(n_out)],
        out_shape=[jax.ShapeDtypeStruct((M, N), dt) for dt in out_dtypes],
        scratch_shapes=[pltpu.VMEM((tm, tn), F32)] if nk > 1 else [],
        compiler_params=_cparams(("parallel", "parallel", "arbitrary")),
    )(a, b, *extras)
    return out[0] if n_out == 1 else tuple(out)


def _rows(name, fn, ins, outs, *, tr, accs=()):
    ins = [(e[0], e[1]) + (e[2] if len(e) > 2 else (0, e[0].shape[-1])) for e in ins]
    n_rows = next(e[0].shape[0] for e in ins if e[1] == "row")
    assert n_rows % tr == 0 and tr % SUBLANE == 0
    steps = n_rows // tr
    t8 = tr // SUBLANE
    n8 = n_rows // SUBLANE
    n_in, n_out, n_acc = len(ins), len(outs), len(accs)

    def body(*refs):
        i = pl.program_id(0)
        vals = fn(i, steps, *[r[...] for r in refs[:n_in]])
        if not isinstance(vals, (tuple, list)):
            vals = (vals,)
        assert len(vals) == n_out + n_acc
        for o, v in zip(refs[n_in:n_in + n_out], vals[:n_out]):
            o[...] = v.astype(o.dtype)
        if n_acc:
            acc_refs = refs[n_in + n_out:]

            @pl.when(i == 0)
            def _():
                for r in acc_refs:
                    r[...] = jnp.zeros_like(r)

            for r, v in zip(acc_refs, vals[n_out:]):
                r[...] += v.astype(r.dtype)

    in_specs = []
    for a, kind, cb, c in ins:
        if kind == "row":
            in_specs.append(pl.BlockSpec((tr, c), lambda i, cb=cb: (i, cb)))
        elif kind == "full":
            in_specs.append(pl.BlockSpec(a.shape, lambda i, z=(0,) * a.ndim: z))
        elif kind == "prev8":
            in_specs.append(pl.BlockSpec((SUBLANE, c), lambda i, cb=cb: (jnp.maximum(i * t8 - 1, 0), cb)))
        elif kind == "next8":
            in_specs.append(pl.BlockSpec((SUBLANE, c), lambda i, cb=cb: (jnp.minimum((i + 1) * t8, n8 - 1), cb)))
        else:
            raise ValueError(kind)
    out_specs = [pl.BlockSpec((tr, c), lambda i: (i, 0)) for c, _ in outs]
    out_specs += [pl.BlockSpec(s, lambda i, z=(0,) * len(s): z) for s, _ in accs]
    out_shape = [jax.ShapeDtypeStruct((n_rows, c), dt) for c, dt in outs]
    out_shape += [jax.ShapeDtypeStruct(s, dt) for s, dt in accs]
    res = pl.pallas_call(
        body, name=name, grid=(steps,), in_specs=in_specs, out_specs=out_specs, out_shape=out_shape,
        compiler_params=_cparams(("arbitrary",) if n_acc else ("parallel",)),
    )(*[e[0] for e in ins])
    return res[0] if len(res) == 1 else tuple(res)


def _colsum(x):
    return jnp.sum(x, axis=0, keepdims=True)


def _sum_all(x):
    return jnp.sum(jnp.sum(x, axis=1, keepdims=True), axis=0, keepdims=True)


def _rmsnorm_fwd(name, x, gain):
    def fn(i, n, xt, g):
        r = lax.rsqrt(jnp.mean(xt * xt, axis=-1, keepdims=True) + EPS)
        return (xt * r * g,)
    return _rows(name, fn, [(x, "row"), (gain, "full")], [(x.shape[1], BF16)], tr=512)


def _rmsnorm_bwd(name, x, gain, dh, dres):
    def fn(i, n, xt, g, dht, drt):
        r = lax.rsqrt(jnp.mean(xt * xt, axis=-1, keepdims=True) + EPS)
        xh = xt * r
        dxn = dht * g
        dx = r * (dxn - xh * jnp.mean(dxn * xh, axis=-1, keepdims=True))
        return drt + dx, _colsum(dht * xh)
    D = x.shape[1]
    return _rows(name, fn, [(x, "row"), (gain, "full"), (dh, "row"), (dres, "row")], [(D, F32)],
                 tr=256, accs=[((1, D), F32)])


def _head_consts():
    import numpy as np
    e = np.arange(A_WIDTH) % A_HEAD_DIM
    inv = (np.float32(ROPE_THETA) ** (-np.arange(0, ROPE_DIM, 2, dtype=np.float32) / np.float32(ROPE_DIM))).astype(np.float32)
    c = np.zeros((8, A_WIDTH), np.float32)
    c[0] = np.where(e < ROPE_DIM, inv[e % ROPE_HALF], 0.0)
    c[1] = np.where(e < ROPE_HALF, -1.0, np.where(e < ROPE_DIM, 1.0, 0.0))
    c[2] = (e < ROPE_HALF).astype(np.float32)
    c[3] = (e < ROPE_DIM).astype(np.float32)
    return jnp.asarray(c)


def _block_diag(scale):
    import numpy as np
    h = np.arange(A_WIDTH) // A_HEAD_DIM
    return jnp.asarray((h[:, None] == h[None, :]).astype(np.float32) * scale, dtype=BF16)


def _seg_sum(x, bd):
    hi = x.astype(BF16)
    lo = (x - hi.astype(F32)).astype(BF16)
    return _dot(hi, bd) + _dot(lo, bd)


def _rope_tables(positions, consts):
    def fn(i, n, pos, c):
        ang = pos.astype(F32) * c[0:1, :]
        return jnp.cos(ang), jnp.sin(ang) * c[1:2, :]
    return _rows("rope_tables", fn, [(positions, "row"), (consts, "full")],
                 [(A_WIDTH, F32), (A_WIDTH, F32)], tr=512)


def _rope_apply(y, ct, st, low):
    rolled = jnp.where(low, pltpu.roll(y, A_WIDTH - ROPE_HALF, 1), pltpu.roll(y, ROPE_HALF, 1))
    return y * ct + rolled * st


def _rope_apply_bwd(dout, ct, st, low, in16):
    t = dout * st
    back = jnp.where(low, pltpu.roll(t, A_WIDTH - ROPE_HALF, 1), jnp.where(in16, pltpu.roll(t, ROPE_HALF, 1), 0.0))
    return dout * ct + back


def _attn_prep(qkv, gains, ct, st, consts, bd):
    def fn(i, n, t, g, c_t, s_t, c, b):
        low = c[2:3, :] > 0.5
        groups = []
        for grp in range(3):
            cols = []
            for which in range(3):
                off = (grp * 3 + which) * A_WIDTH
                x = t[:, off:off + A_WIDTH]
                if which == 2:
                    cols.append(x.astype(BF16))
                    continue
                r = lax.rsqrt(_seg_sum(x * x, b) + EPS)
                y = x * r * g[grp * 2 + which:grp * 2 + which + 1, :]
                cols.append(_rope_apply(y, c_t, s_t, low).astype(BF16))
            groups.append(jnp.concatenate(cols, axis=1))
        return tuple(groups)
    return _rows("attn_prep", fn, [(qkv, "row"), (gains, "full"), (ct, "row"), (st, "row"), (consts, "full"), (bd, "full")],
                 [(3 * A_WIDTH, BF16)] * 3, tr=256)


def _band_mask(n):
    row = lax.broadcasted_iota(jnp.int32, (BAND, 2 * BAND), 0)
    col = lax.broadcasted_iota(jnp.int32, (BAND, 2 * BAND), 1)
    dist = row + BAND - col
    return (dist >= 0) & (dist <= BAND) & ((col >= BAND) | (n > 0))


def _attn_fwd(qkvn, grp):
    S = qkvn.shape[0]
    d = SWA_GROUPS[grp][1]
    L = S // d
    nblk = L // BAND
    assert L % BAND == 0
    view = qkvn.reshape(L, d * 3 * A_WIDTH)

    def body(q_ref, kc_ref, kp_ref, vc_ref, vp_ref, o_ref, lse_ref):
        n = pl.program_id(1)
        valid = _band_mask(n)
        first = lax.broadcasted_iota(jnp.int32, (BAND, LANE), 1) < A_HEAD_DIM
        pairs = [slice(pr * LANE, (pr + 1) * LANE) for pr in range(A_WIDTH // LANE)]
        halves = (first, jnp.logical_not(first))
        qps = [q_ref[:, sl] for sl in pairs]
        kcats = [jnp.concatenate([kp_ref[:, sl], kc_ref[:, sl]], axis=0) for sl in pairs]
        vcats = [jnp.concatenate([vp_ref[:, sl], vc_ref[:, sl]], axis=0) for sl in pairs]
        heads = [(pr, m) for pr in range(len(pairs)) for m in halves]
        ss = [_dot(jnp.where(m, qps[pr], jnp.zeros_like(qps[pr])), kcats[pr], 1, 1) for pr, m in heads]
        ps, lses = [], []
        for s in ss:
            s = jnp.where(valid, s * (A_HEAD_DIM ** -0.5), -1e30)
            mx = jnp.max(s, axis=-1, keepdims=True)
            e = jnp.exp(s - mx)
            l = jnp.sum(e, axis=-1, keepdims=True)
            ps.append((e / l).astype(BF16))
            lses.append(mx + jnp.log(l))
        os_ = [_dot(p, vcats[pr]) for p, (pr, _) in zip(ps, heads)]
        o_ref[...] = jnp.concatenate([jnp.where(first, os_[2 * pr], os_[2 * pr + 1]) for pr in range(len(pairs))], axis=1)
        lse_ref[...] = jnp.concatenate([jnp.where(first, lses[2 * pr], lses[2 * pr + 1]) for pr in range(len(pairs))], axis=1)

    blk = (BAND, A_WIDTH)
    o, lse = pl.pallas_call(
        body, name=f"attn_fwd_g{grp}", grid=(d, nblk),
        in_specs=[pl.BlockSpec(blk, lambda r, n: (n, r * 3)),
                  pl.BlockSpec(blk, lambda r, n: (n, r * 3 + 1)),
                  pl.BlockSpec(blk, lambda r, n: (jnp.maximum(n - 1, 0), r * 3 + 1)),
                  pl.BlockSpec(blk, lambda r, n: (n, r * 3 + 2)),
                  pl.BlockSpec(blk, lambda r, n: (jnp.maximum(n - 1, 0), r * 3 + 2))],
        out_specs=[pl.BlockSpec(blk, lambda r, n: (n, r)), pl.BlockSpec(blk, lambda r, n: (n, r))],
        out_shape=[jax.ShapeDtypeStruct((L, d * A_WIDTH), F32)] * 2,
        compiler_params=_cparams(("parallel", "parallel")),
    )(view, view, view, view, view)
    return o.reshape(S, A_WIDTH), lse.reshape(S, A_WIDTH)


def _merge_weights(l0, l1, l2):
    mx = jnp.maximum(jnp.maximum(l0, l1), l2)
    e0, e1, e2 = jnp.exp(l0 - mx), jnp.exp(l1 - mx), jnp.exp(l2 - mx)
    inv = 1.0 / (e0 + e1 + e2)
    return e0 * inv, e1 * inv, e2 * inv


def _attn_merge(os_, lses):
    def fn(i, n, o0, o1, o2, l0, l1, l2):
        w0, w1, w2 = _merge_weights(l0, l1, l2)
        return (w0 * o0 + w1 * o1 + w2 * o2,)
    ins = [(a, "row") for a in (*os_, *lses)]
    return _rows("attn_merge", fn, ins, [(A_WIDTH, BF16)], tr=512)


def _attn_merge_bwd(do, os_, lses, bd1):
    def fn(i, n, dot_, o0, o1, o2, l0, l1, l2, b):
        w0, w1, w2 = _merge_weights(l0, l1, l2)
        o = w0 * o0 + w1 * o1 + w2 * o2
        dsum = _seg_sum(dot_ * o, b)
        return (w0 * dot_, w1 * dot_, w2 * dot_, -w0 * dsum, -w1 * dsum, -w2 * dsum)
    ins = [(do, "row")] + [(a, "row") for a in (*os_, *lses)] + [(bd1, "full")]
    res = _rows("attn_merge_bwd", fn, ins, [(A_WIDTH, BF16)] * 3 + [(A_WIDTH, F32)] * 3, tr=256)
    return res[:3], res[3:]


def _lane_pick(x, lane_idx, lane):
    return jnp.sum(jnp.where(lane_idx == lane, x, 0.0), axis=-1, keepdims=True)


def _attn_bwd(qkvn, grp, do_g, lse, c_g):
    S = qkvn.shape[0]
    d = SWA_GROUPS[grp][1]
    L = S // d
    nblk = L // BAND
    view = qkvn.reshape(L, d * 3 * A_WIDTH)
    dov, lsev, cv = (t.reshape(L, d * A_WIDTH) for t in (do_g, lse, c_g))

    def body(q_ref, kc_ref, kp_ref, vc_ref, vp_ref, do_ref, lse_ref, c_ref, dq_ref, dk_ref, dv_ref, ck, cv_):
        n = pl.program_id(1)

        @pl.when(n == 0)
        def _():
            ck[...] = jnp.zeros_like(ck)
            cv_[...] = jnp.zeros_like(cv_)

        @pl.when(n < nblk)
        def _():
            valid = _band_mask(n)
            lane = lax.broadcasted_iota(jnp.int32, (BAND, LANE), 1)
            first = lane < A_HEAD_DIM
            lane2 = lax.broadcasted_iota(jnp.int32, (2 * BAND, LANE), 1) < A_HEAD_DIM
            pairs = [slice(pr * LANE, (pr + 1) * LANE) for pr in range(A_WIDTH // LANE)]
            halves = (first, jnp.logical_not(first))
            qps = [q_ref[:, sl] for sl in pairs]
            dops = [do_ref[:, sl] for sl in pairs]
            kcats = [jnp.concatenate([kp_ref[:, sl], kc_ref[:, sl]], axis=0) for sl in pairs]
            vcats = [jnp.concatenate([vp_ref[:, sl], vc_ref[:, sl]], axis=0) for sl in pairs]
            heads = [(pr, hh) for pr in range(len(pairs)) for hh in range(2)]
            zero = jnp.zeros_like(qps[0])
            ss = [_dot(jnp.where(halves[hh], qps[pr], zero), kcats[pr], 1, 1) for pr, hh in heads]
            dps = [_dot(jnp.where(halves[hh], dops[pr], zero), vcats[pr], 1, 1) for pr, hh in heads]
            dss, pbs = [], []
            for (pr, hh), s, dp in zip(heads, ss, dps):
                lse_h = _lane_pick(lse_ref[:, pairs[pr]], lane, hh * A_HEAD_DIM)
                c_h = _lane_pick(c_ref[:, pairs[pr]], lane, hh * A_HEAD_DIM)
                p = jnp.where(valid, jnp.exp(s * (A_HEAD_DIM ** -0.5) - lse_h), 0.0)
                dss.append((p * (dp + c_h) * (A_HEAD_DIM ** -0.5)).astype(BF16))
                pbs.append(p.astype(BF16))
            dqs = [_dot(ds, kcats[pr]) for ds, (pr, _) in zip(dss, heads)]
            dks = [_dot(ds, qps[pr], 0, 0) for ds, (pr, _) in zip(dss, heads)]
            dvs = [_dot(pb, dops[pr], 0, 0) for pb, (pr, _) in zip(pbs, heads)]
            for pr, sl in enumerate(pairs):
                dq_ref[:, sl] = jnp.where(first, dqs[2 * pr], dqs[2 * pr + 1])
                dkc = jnp.where(lane2, dks[2 * pr], dks[2 * pr + 1])
                dvc = jnp.where(lane2, dvs[2 * pr], dvs[2 * pr + 1])
                dk_ref[:, sl] = ck[:, sl] + dkc[:BAND]
                dv_ref[:, sl] = cv_[:, sl] + dvc[:BAND]
                ck[:, sl] = dkc[BAND:]
                cv_[:, sl] = dvc[BAND:]

        @pl.when(n == nblk)
        def _():
            dk_ref[...] = ck[...]
            dv_ref[...] = cv_[...]

    blk = (BAND, A_WIDTH)
    last = nblk - 1
    qn = lambda n: jnp.minimum(n, last)
    pn = lambda n: jnp.clip(n - 1, 0, last)
    dq, dk, dv = pl.pallas_call(
        body, name=f"attn_bwd_g{grp}", grid=(d, nblk + 1),
        in_specs=[pl.BlockSpec(blk, lambda r, n: (qn(n), r * 3)),
                  pl.BlockSpec(blk, lambda r, n: (qn(n), r * 3 + 1)),
                  pl.BlockSpec(blk, lambda r, n: (pn(n), r * 3 + 1)),
                  pl.BlockSpec(blk, lambda r, n: (qn(n), r * 3 + 2)),
                  pl.BlockSpec(blk, lambda r, n: (pn(n), r * 3 + 2)),
                  pl.BlockSpec(blk, lambda r, n: (qn(n), r)),
                  pl.BlockSpec(blk, lambda r, n: (qn(n), r)),
                  pl.BlockSpec(blk, lambda r, n: (qn(n), r))],
        out_specs=[pl.BlockSpec(blk, lambda r, n: (qn(n), r)),
                   pl.BlockSpec(blk, lambda r, n: (pn(n), r)),
                   pl.BlockSpec(blk, lambda r, n: (pn(n), r))],
        out_shape=[jax.ShapeDtypeStruct((L, d * A_WIDTH), F32)] * 3,
        scratch_shapes=[pltpu.VMEM(blk, F32), pltpu.VMEM(blk, F32)],
        compiler_params=_cparams(("parallel", "arbitrary")),
    )(view, view, view, view, view, dov, lsev, cv)
    return tuple(t.reshape(S, A_WIDTH) for t in (dq, dk, dv))


def _attn_prep_bwd(qkv, grads, gains, ct, st, consts, bd):
    def fn(i, n, t, g, c_t, s_t, c, b, *gr):
        low = c[2:3, :] > 0.5
        in16 = c[3:4, :] > 0.5
        cols, dgs = [], []
        for grp in range(3):
            for which in range(3):
                dout = gr[grp * 3 + which]
                if which == 2:
                    cols.append(dout.astype(BF16))
                    continue
                off = (grp * 3 + which) * A_WIDTH
                x = t[:, off:off + A_WIDTH]
                gain = g[grp * 2 + which:grp * 2 + which + 1, :]
                r = lax.rsqrt(_seg_sum(x * x, b) + EPS)
                xh = x * r
                dy = _rope_apply_bwd(dout, c_t, s_t, low, in16)
                dyn = dy * gain
                dx = r * (dyn - xh * _seg_sum(dyn * xh, b))
                cols.append(dx.astype(BF16))
                dgs.append(_colsum(dy * xh))
        return (jnp.concatenate(cols, axis=1), *dgs)
    ins = [(qkv, "row"), (gains, "full"), (ct, "row"), (st, "row"), (consts, "full"), (bd, "full")] + [(a, "row") for a in grads]
    res = _rows("attn_prep_bwd", fn, ins, [(A_QKV, BF16)], tr=128, accs=[((1, A_WIDTH), F32)] * 6)
    return res[0], res[1:]


DN_QKV = 3 * DN_WIDTH
DN_QKVZ = DN_QKV + DN_WIDTH


def _sigmoid(x):
    return 1.0 / (1.0 + jnp.exp(-x))


def _softplus(x):
    return jnp.maximum(x, 0.0) + jnp.log(1.0 + jnp.exp(-jnp.abs(x)))


def _conv_taps(xs, w, tr):
    acc = None
    for j in range(CONV_W):
        sh = CONV_W - 1 - j
        term = (pltpu.roll(xs, sh, 0) if sh else xs)[SUBLANE:] * w[j:j + 1, :]
        acc = term if acc is None else acc + term
    return acc


def _dn_prep(qkvz, ab, convw, alog_row, dt_row):
    tr = 256

    def fn(i, n, x, xp, abt, w, al, dt):
        xp = jnp.where(i > 0, xp, 0.0)
        u = _conv_taps(jnp.concatenate([xp, x], axis=0), w, tr)
        y = u * _sigmoid(u)
        qs, ks = [], []
        for h in range(DN_HEADS):
            for dst, base, sc in ((qs, 0, DN_DIM ** -0.5), (ks, DN_WIDTH, 1.0)):
                seg = y[:, base + h * DN_DIM:base + (h + 1) * DN_DIM]
                dst.append(seg * (lax.rsqrt(jnp.sum(seg * seg, axis=-1, keepdims=True) + EPS) * sc))
        lane = lax.broadcasted_iota(jnp.int32, abt.shape, 1)
        g = -jnp.exp(al) * _softplus(abt + dt)
        gb = jnp.where(lane < DN_HEADS, g, jnp.where(lane < 2 * DN_HEADS, _sigmoid(abt), 0.0))
        return u, jnp.concatenate(qs, axis=1), jnp.concatenate(ks, axis=1), y[:, 2 * DN_WIDTH:], gb

    ins = [(qkvz, "row", (0, DN_QKV)), (qkvz, "prev8", (0, DN_QKV)), (ab, "row"), (convw, "full"),
           (alog_row, "full"), (dt_row, "full")]
    return _rows("dn_prep", fn, ins, [(DN_QKV, F32), (DN_WIDTH, F32), (DN_WIDTH, F32), (DN_WIDTH, F32), (LANE, F32)], tr=tr)


def _tri_masks():
    row = lax.broadcasted_iota(jnp.int32, (CHUNK, CHUNK), 0)
    col = lax.broadcasted_iota(jnp.int32, (CHUNK, CHUNK), 1)
    return row >= col, row > col, row == col


def _heads(fn, *lists):
    return [fn(*xs) for xs in zip(*lists)]


def _split(x):
    hi = x.astype(BF16)
    return hi, (x - hi.astype(F32)).astype(BF16)


def _dot3(a, b, ca=1, cb=0):
    (ah, al), (bh, bl) = a, b
    return _dot(ah, bh, ca, cb) + (_dot(ah, bl, ca, cb) + _dot(al, bh, ca, cb))


def _unit_lower_inverse(a_list, eye):
    ts = [eye - a for a in a_list]
    parts = [_split(a) for a in a_list]
    for _ in range(5):
        parts = [_split(_dot3(p, p)) for p in parts]
        ts = [t + _dot3(_split(t), p) for t, p in zip(ts, parts)]
    return ts


def _dn_terms(qs, ks, vs, gb):
    lower, strict, diag = _tri_masks()
    lane = lax.broadcasted_iota(jnp.int32, (CHUNK, LANE), 1)
    is_last = lax.broadcasted_iota(jnp.int32, (CHUNK, 1), 0) == CHUNK - 1
    hs = range(DN_HEADS)
    gc = _dot(lower.astype(F32), gb, precision=HIGHEST)
    gct = jnp.transpose(gc)
    bcol = [_lane_pick(gb, lane, DN_HEADS + h) for h in hs]
    gcol = [_lane_pick(gc, lane, h) for h in hs]
    glast = [jnp.sum(jnp.where(is_last, g, 0.0), axis=0, keepdims=True) for g in gcol]
    decay = [jnp.exp(jnp.where(lower, gcol[h] - gct[h:h + 1, :], -1e30)) for h in hs]
    kb = _heads(lambda k, b: k * b, ks, bcol)
    kk = _heads(lambda x, k: _bdot(x, k, 1, 1), kb, ks)
    qk = _heads(lambda q, k: _bdot(q, k, 1, 1), qs, ks)
    a = _heads(lambda x, d: jnp.where(strict, x * d, 0.0), kk, decay)
    t = [_split(x) for x in _unit_lower_inverse(a, diag.astype(F32))]
    eg = [jnp.exp(g) for g in gcol]
    egl = _heads(lambda gl, g: jnp.exp(gl - g), glast, gcol)
    rhs_w = _heads(lambda x, e: x * e, kb, eg)
    u = _heads(lambda tt, v, b: _dot3(tt, _split(v * b)), t, vs, bcol)
    w = _heads(lambda tt, r: _dot3(tt, _split(r)), t, rhs_w)
    return dict(bcol=bcol, decay=decay, kb=kb, a=a, t=t, eg=eg, egl=egl, rhs_w=rhs_w, u=u, w=w,
                attn=_heads(lambda x, d: x * d, qk, decay), q_dec=_heads(lambda q, e: q * e, qs, eg),
                k_dec=_heads(lambda k, e: k * e, ks, egl), c_dec=[jnp.exp(g) for g in glast],
                lower=lower, strict=strict, lane=lane, is_last=is_last)


def _head_slices(ref):
    return [ref[:, h * DN_DIM:(h + 1) * DN_DIM] for h in range(DN_HEADS)]


def _dn_chunk_fwd(q, k, v, gb):
    S = q.shape[0]
    N = S // CHUNK

    def body(q_ref, k_ref, v_ref, gb_ref, o_ref, st_ref, state):
        @pl.when(pl.program_id(0) == 0)
        def _():
            state[...] = jnp.zeros_like(state)

        f = _dn_terms(_head_slices(q_ref), _head_slices(k_ref), _head_slices(v_ref), gb_ref[...])
        s = [state[h] for h in range(DN_HEADS)]
        for h in range(DN_HEADS):
            st_ref[0, h] = s[h]
        sb = [x.astype(BF16) for x in s]
        v_new = _heads(lambda u, w, x: u - _bdot(w, x), f["u"], f["w"], sb)
        o = _heads(lambda qd, x, at, vn: _bdot(qd, x) + _bdot(at, vn), f["q_dec"], sb, f["attn"], v_new)
        new_s = _heads(lambda x, c, kd, vn: x * c + _bdot(kd, vn, 0, 0), s, f["c_dec"], f["k_dec"], v_new)
        for h in range(DN_HEADS):
            o_ref[:, h * DN_DIM:(h + 1) * DN_DIM] = o[h]
            state[h] = new_s[h]

    blk = pl.BlockSpec((CHUNK, DN_WIDTH), lambda n: (n, 0))
    st_blk = pl.BlockSpec((1, DN_HEADS, DN_DIM, DN_DIM), lambda n: (n, 0, 0, 0))
    return pl.pallas_call(
        body, name="dn_chunk_fwd", grid=(N,),
        in_specs=[blk, blk, blk, pl.BlockSpec((CHUNK, LANE), lambda n: (n, 0))],
        out_specs=[blk, st_blk],
        out_shape=[jax.ShapeDtypeStruct((S, DN_WIDTH), F32), jax.ShapeDtypeStruct((N, DN_HEADS, DN_DIM, DN_DIM), F32)],
        scratch_shapes=[pltpu.VMEM((DN_HEADS, DN_DIM, DN_DIM), F32)],
        compiler_params=_cparams(("arbitrary",)),
    )(q, k, v, gb)


def _dn_chunk_bwd(q, k, v, gb, states, do):
    S = q.shape[0]
    N = S // CHUNK

    def body(q_ref, k_ref, v_ref, gb_ref, st_ref, do_ref, dq_ref, dk_ref, dv_ref, dgb_ref, dstate):
        @pl.when(pl.program_id(0) == 0)
        def _():
            dstate[...] = jnp.zeros_like(dstate)

        hs = range(DN_HEADS)
        qs, ks, vs, dos = (_head_slices(r) for r in (q_ref, k_ref, v_ref, do_ref))
        f = _dn_terms(qs, ks, vs, gb_ref[...])
        lane, is_last = f["lane"], f["is_last"]
        rowsum = lambda x: jnp.sum(x, axis=-1, keepdims=True)
        s = [st_ref[0, h] for h in hs]
        dsn = [dstate[h] for h in hs]
        sb = [x.astype(BF16) for x in s]
        dsb = [x.astype(BF16) for x in dsn]
        dob = [x.astype(BF16) for x in dos]
        v_new = _heads(lambda u, w, x: u - _bdot(w, x), f["u"], f["w"], sb)
        dv_new = _heads(lambda at, d, kd, x: _bdot(at, d, 0, 0) + _bdot(kd, x), f["attn"], dob, f["k_dec"], dsb)
        dattn = _heads(lambda d, vn: _bdot(d, vn, 1, 1), dob, v_new)
        dq_dec = _heads(lambda d, x: _bdot(d, x, 1, 1), dob, sb)
        dk_dec = _heads(lambda vn, x: _bdot(vn, x, 1, 1), v_new, dsb)
        dw = _heads(lambda dv_, x: -_bdot(dv_, x, 1, 1), dv_new, sb)
        new_ds = _heads(lambda x, c, qd, d, w, dv_: x * c + _bdot(qd, d, 0, 0) - _bdot(w, dv_, 0, 0),
                        dsn, f["c_dec"], f["q_dec"], dob, f["w"], dv_new)
        for h in hs:
            dstate[h] = new_ds[h]
        drhs_u = _heads(lambda tt, x: _dot3(tt, _split(x), 0, 0), f["t"], dv_new)
        drhs_w = _heads(lambda tt, x: _dot3(tt, _split(x), 0, 0), f["t"], dw)
        da = _heads(lambda du_, u, dw_, w: jnp.where(f["strict"], -(_bdot(du_, u, 1, 1) + _bdot(dw_, w, 1, 1)), 0.0),
                    drhs_u, f["u"], drhs_w, f["w"])
        dkk = _heads(lambda x, d: x * d, da, f["decay"])
        dqk = _heads(lambda x, d: x * d, dattn, f["decay"])
        dkb = _heads(lambda x, k_, dw_, e: _bdot(x, k_) + dw_ * e, dkk, ks, drhs_w, f["eg"])
        dq = _heads(lambda x, k_, dqd, e: _bdot(x, k_) + dqd * e, dqk, ks, dq_dec, f["eg"])
        dk = _heads(lambda x, kb_, y, q_, dkd, el, dkb_, b: _bdot(x, kb_, 0, 0) + _bdot(y, q_, 0, 0) + dkd * el + dkb_ * b,
                    dkk, f["kb"], dqk, qs, dk_dec, f["egl"], dkb, f["bcol"])
        m = _heads(lambda x, a_, y, at: x * a_ + y * at, da, f["a"], dattn, f["attn"])
        ones = jnp.ones((CHUNK, LANE), BF16)
        col_m = [(_dot(mh, ones, 0, 0) + _dot(ml, ones, 0, 0))[:, 0:1] for mh, ml in map(_split, m)]
        dgc_all = jnp.zeros((CHUNK, LANE), F32)
        dbeta_all = jnp.zeros((CHUNK, LANE), F32)
        for h in hs:
            dq_ref[:, h * DN_DIM:(h + 1) * DN_DIM] = dq[h]
            dk_ref[:, h * DN_DIM:(h + 1) * DN_DIM] = dk[h]
            dv_ref[:, h * DN_DIM:(h + 1) * DN_DIM] = drhs_u[h] * f["bcol"][h]
            kdec_term = rowsum(dk_dec[h] * f["k_dec"][h])
            dc_dec = _sum_all(dsn[h] * s[h])
            dgc = (rowsum(m[h]) - col_m[h] + rowsum(dq_dec[h] * f["q_dec"][h]) - kdec_term
                   + rowsum(drhs_w[h] * f["rhs_w"][h]))
            last_extra = jnp.sum(kdec_term, axis=0, keepdims=True) + dc_dec * f["c_dec"][h]
            dgc = dgc + jnp.where(is_last, last_extra, 0.0)
            dbeta = rowsum(drhs_u[h] * vs[h]) + rowsum(dkb[h] * ks[h])
            dgc_all = jnp.where(lane == h, dgc, dgc_all)
            dbeta_all = jnp.where(lane == DN_HEADS + h, dbeta, dbeta_all)
        dg_all = _dot(f["lower"].astype(F32), dgc_all, 0, 0, precision=HIGHEST)
        dgb_ref[...] = jnp.where(lane < DN_HEADS, dg_all, dbeta_all)

    rev = lambda n: (N - 1 - n, 0)
    blk = pl.BlockSpec((CHUNK, DN_WIDTH), rev)
    gblk = pl.BlockSpec((CHUNK, LANE), rev)
    st_blk = pl.BlockSpec((1, DN_HEADS, DN_DIM, DN_DIM), lambda n: (N - 1 - n, 0, 0, 0))
    return pl.pallas_call(
        body, name="dn_chunk_bwd", grid=(N,),
        in_specs=[blk, blk, blk, gblk, st_blk, blk],
        out_specs=[blk, blk, blk, gblk],
        out_shape=[jax.ShapeDtypeStruct((S, DN_WIDTH), F32)] * 3 + [jax.ShapeDtypeStruct((S, LANE), F32)],
        scratch_shapes=[pltpu.VMEM((DN_HEADS, DN_DIM, DN_DIM), F32)],
        compiler_params=_cparams(("arbitrary",)),
    )(q, k, v, gb, states, do)


def _dn_post(o, qkvz, gain_row):
    def fn(i, n, ot, z, g):
        cols = []
        for h in range(DN_HEADS):
            seg = ot[:, h * DN_DIM:(h + 1) * DN_DIM]
            cols.append(seg * lax.rsqrt(jnp.mean(seg * seg, axis=-1, keepdims=True) + EPS) * g)
        return (jnp.concatenate(cols, axis=1) * (z * _sigmoid(z)),)
    return _rows("dn_post", fn, [(o, "row"), (qkvz, "row", (3, DN_WIDTH)), (gain_row, "full")], [(DN_WIDTH, BF16)], tr=512)


def _dn_post_bwd(don, o, qkvz, gain_row):
    def fn(i, n, dy, ot, z, g):
        sg = _sigmoid(z)
        sz = z * sg
        dos, ohs = [], []
        dg = jnp.zeros((1, DN_DIM), F32)
        for h in range(DN_HEADS):
            sl = slice(h * DN_DIM, (h + 1) * DN_DIM)
            seg = ot[:, sl]
            r = lax.rsqrt(jnp.mean(seg * seg, axis=-1, keepdims=True) + EPS)
            oh = seg * r
            dno = dy[:, sl] * sz[:, sl]
            dg = dg + _colsum(dno * oh)
            dn = dno * g
            dos.append(r * (dn - oh * jnp.mean(dn * oh, axis=-1, keepdims=True)))
            ohs.append(oh * g)
        dz = dy * jnp.concatenate(ohs, axis=1) * (sg * (1.0 + z * (1.0 - sg)))
        return jnp.concatenate(dos, axis=1), dz, dg
    ins = [(don, "row"), (o, "row"), (qkvz, "row", (3, DN_WIDTH)), (gain_row, "full")]
    return _rows("dn_post_bwd", fn, ins, [(DN_WIDTH, F32), (DN_WIDTH, F32)], tr=256, accs=[((1, DN_DIM), F32)])


def _dn_prep_bwd(dq, dk, dv, dgb, u, ab, alog_row, dt_row):
    def fn(i, n, dqt, dkt, dvt, dgbt, ut, abt, al, dt):
        sg = _sigmoid(ut)
        y = ut * sg
        dys = []
        for grad, base, sc in ((dqt, 0, DN_DIM ** -0.5), (dkt, DN_WIDTH, 1.0)):
            for h in range(DN_HEADS):
                seg = y[:, base + h * DN_DIM:base + (h + 1) * DN_DIM]
                gr = grad[:, h * DN_DIM:(h + 1) * DN_DIM]
                r = lax.rsqrt(jnp.sum(seg * seg, axis=-1, keepdims=True) + EPS)
                xh = seg * r
                dys.append((r * sc) * (gr - xh * jnp.sum(gr * xh, axis=-1, keepdims=True)))
        dy = jnp.concatenate(dys + [dvt], axis=1)
        du = dy * (sg * (1.0 + ut * (1.0 - sg)))
        lane = lax.broadcasted_iota(jnp.int32, abt.shape, 1)
        is_g = lane < DN_HEADS
        ea = jnp.exp(al)
        x = abt + dt
        slope = -ea * _sigmoid(x)
        gval = -ea * _softplus(x)
        dg = jnp.where(is_g, dgbt, 0.0)
        beta = _sigmoid(abt)
        dab = jnp.where(is_g, dg * slope, jnp.where(lane < 2 * DN_HEADS, dgbt * beta * (1.0 - beta), 0.0))
        return du, dab, _colsum(dg * gval), _colsum(dg * slope)
    ins = [(dq, "row"), (dk, "row"), (dv, "row"), (dgb, "row"), (u, "row"), (ab, "row"), (alog_row, "full"), (dt_row, "full")]
    return _rows("dn_prep_bwd", fn, ins, [(DN_QKV, F32), (LANE, BF16)], tr=256, accs=[((1, LANE), F32)] * 2)


def _dn_conv_bwd(du, dz, qkvz, convw):
    tr = 256

    def fn(i, n, dut, dun, dzt, x, xp, w):
        dun = jnp.where(i < n - 1, dun, 0.0)
        dus = jnp.concatenate([dut, dun], axis=0)
        xs = jnp.concatenate([jnp.where(i > 0, xp, 0.0), x], axis=0)
        dx = None
        dws = []
        for j in range(CONV_W):
            sh = CONV_W - 1 - j
            term = (pltpu.roll(dus, tr + SUBLANE - sh, 0) if sh else dus)[:tr] * w[j:j + 1, :]
            dx = term if dx is None else dx + term
            dws.append(_colsum(dut * (pltpu.roll(xs, sh, 0) if sh else xs)[SUBLANE:]))
        return (jnp.concatenate([dx.astype(BF16), dzt.astype(BF16)], axis=1), *dws)

    ins = [(du, "row"), (du, "next8"), (dz, "row"), (qkvz, "row", (0, DN_QKV)), (qkvz, "prev8", (0, DN_QKV)), (convw, "full")]
    res = _rows("dn_conv_bwd", fn, ins, [(DN_QKVZ, BF16)], tr=tr, accs=[((1, DN_QKV), F32)] * CONV_W)
    return res[0], res[1:]


def _add(acc, r):
    return (r + acc,)


def _mlp_ple_fwd(i, x1, p_i, mlp_gain, ple_gain, w_up, w_down, w_ple, w_gate):
    hm = _rmsnorm_fwd(f"mlp_norm{i}", x1, mlp_gain)
    u, a = _mm(f"mlp_up{i}", hm, w_up, epilogue=lambda acc: (acc, jnp.square(jnp.maximum(acc, 0.0))),
               out_dtypes=(F32, BF16))
    x2 = _mm(f"mlp_down{i}", a, w_down, epilogue=_add, extras=(x1,))
    hp = _rmsnorm_fwd(f"ple_norm{i}", x2, ple_gain)
    pp = _mm(f"ple_proj{i}", p_i, w_ple)

    def gate_epilogue(acc, x2t, ppt):
        gate = _sigmoid(acc)
        return x2t + ppt * gate, gate

    x3, gate = _mm(f"ple_gate{i}", hp, w_gate, epilogue=gate_epilogue, extras=(x2, pp), out_dtypes=(F32, F32))
    return x3, dict(x1=x1, hm=hm, u=u, a=a, x2=x2, hp=hp, pp=pp, gate=gate, p=p_i)


def _mlp_ple_bwd(i, dx3, sv, mlp_gain, ple_gain, w_up, w_down, w_gate):
    def fn(_i, _n, d, g, pp):
        return d * g, d * pp * g * (1.0 - g)
    dpp, dzg = _rows(f"ple_gate_bwd{i}", fn, [(dx3, "row"), (sv["gate"], "row"), (sv["pp"], "row")],
                     [(D_MODEL, BF16), (D_MODEL, BF16)], tr=512)
    d_w_ple = _mm(f"ple_proj_dw{i}", sv["p"], dpp, ta=True, out_dtypes=(BF16,))
    d_w_gate = _mm(f"ple_gate_dw{i}", sv["hp"], dzg, ta=True, out_dtypes=(BF16,))
    dhp = _mm(f"ple_gate_dx{i}", dzg, w_gate, tb=True)
    dx2, d_ple_gain = _rmsnorm_bwd(f"ple_norm_bwd{i}", sv["x2"], ple_gain, dhp, dx3)
    d_w_down = _mm(f"mlp_down_dw{i}", sv["a"], dx2, ta=True, out_dtypes=(BF16,))
    du = _mm(f"mlp_down_dx{i}", dx2, w_down, tb=True, epilogue=lambda acc, ut: (acc * (2.0 * jnp.maximum(ut, 0.0)),),
             extras=(sv["u"],), out_dtypes=(BF16,))
    d_w_up = _mm(f"mlp_up_dw{i}", sv["hm"], du, ta=True, out_dtypes=(BF16,))
    dhm = _mm(f"mlp_up_dx{i}", du, w_up, tb=True)
    dx1, d_mlp_gain = _rmsnorm_bwd(f"mlp_norm_bwd{i}", sv["x1"], mlp_gain, dhm, dx2)
    return dx1, dict(w_ple=d_w_ple, w_ple_gate=d_w_gate, w_down=d_w_down, w_up=d_w_up,
                     ple_norm=d_ple_gain, mlp_norm=d_mlp_gain)


def _loss_fwd_bwd(y, target):
    D = y.shape[1]

    def fn(i, n, yt, tt):
        e = yt - tt
        return e * (1.0 / D), _colsum(e * e)
    dy, sq = _rows("loss", fn, [(y, "row"), (target, "row")], [(D, F32)], tr=512, accs=[((1, D), F32)])
    return sq, dy


def _local_step(x, p, positions, target, W, P):
    consts = _head_consts()
    bd = _block_diag(1.0 / A_HEAD_DIM)
    bd1 = _block_diag(1.0)
    ct, st = _rope_tables(positions, consts)
    gains = jnp.stack([jnp.tile(v, A_HEADS) for g in range(3) for v in (P["attn_q_gain"][g], P["attn_k_gain"][g])])
    pad = LANE - DN_HEADS
    alog_row = jnp.pad(P["dn_a_log"].reshape(1, DN_HEADS), ((0, 0), (0, pad)))
    dt_row = jnp.pad(P["dn_dt_bias"].reshape(1, DN_HEADS), ((0, 0), (0, pad)))
    ogain_row = P["dn_o_gain"].reshape(1, DN_DIM)
    row = lambda name, i: P[name][i:i + 1]

    h0 = _rmsnorm_fwd("mix_norm0", x, row("mix_norm", 0))
    qkv = _mm("attn_qkv", h0, W["attn_w_qkv"])
    qkvn = _attn_prep(qkv, gains, ct, st, consts, bd)
    os_, lses = zip(*[_attn_fwd(qkvn[g], g) for g in range(3)])
    o_attn = _attn_merge(os_, lses)
    x1 = _mm("attn_out", o_attn, W["attn_w_o"], epilogue=_add, extras=(x,))
    x3, sv0 = _mlp_ple_fwd(0, x1, p[0], row("mlp_norm", 0), row("ple_norm", 0),
                           W["w_up"][0], W["w_down"][0], W["w_ple"][0], W["w_ple_gate"][0])
    h1 = _rmsnorm_fwd("mix_norm1", x3, row("mix_norm", 1))
    qkvz = _mm("dn_in_qkvz", h1, W["dn_w_qkvz"])
    ab = _mm("dn_in_ab", h1, W["dn_w_ab"])
    u, q, k, v, gb = _dn_prep(qkvz, ab, W["dn_conv"], alog_row, dt_row)
    o_dn, states = _dn_chunk_fwd(q, k, v, gb)
    on = _dn_post(o_dn, qkvz, ogain_row)
    x4 = _mm("dn_out", on, W["dn_w_o"], epilogue=_add, extras=(x3,))
    x6, sv1 = _mlp_ple_fwd(1, x4, p[1], row("mlp_norm", 1), row("ple_norm", 1),
                           W["w_up"][1], W["w_down"][1], W["w_ple"][1], W["w_ple_gate"][1])
    sq, dy = _loss_fwd_bwd(x6, target)

    dx4, g1 = _mlp_ple_bwd(1, dy, sv1, row("mlp_norm", 1), row("ple_norm", 1),
                           W["w_up"][1], W["w_down"][1], W["w_ple_gate"][1])
    don = _mm("dn_out_dx", dx4, W["dn_w_o"], tb=True)
    d_dn_w_o = _mm("dn_out_dw", on, dx4, ta=True, out_dtypes=(BF16,))
    do_dn, dz, d_ogain = _dn_post_bwd(don, o_dn, qkvz, ogain_row)
    dq, dk, dv, dgb = _dn_chunk_bwd(q, k, v, gb, states, do_dn)
    du, dab, d_alog, d_dt = _dn_prep_bwd(dq, dk, dv, dgb, u, ab, alog_row, dt_row)
    dqkvz, d_conv = _dn_conv_bwd(du, dz, qkvz, W["dn_conv"])
    dh1 = _mm("dn_in_qkvz_dx", dqkvz, W["dn_w_qkvz"], tb=True)
    dh1 = _mm("dn_in_ab_dx", dab, W["dn_w_ab"], tb=True, epilogue=_add, extras=(dh1,))
    d_w_qkvz = _mm("dn_in_qkvz_dw", h1, dqkvz, ta=True, out_dtypes=(BF16,))
    d_w_ab = _mm("dn_in_ab_dw", h1, dab, ta=True, out_dtypes=(BF16,))
    dx3, d_mix1 = _rmsnorm_bwd("mix_norm_bwd1", x3, row("mix_norm", 1), dh1, dx4)
    dx1, g0 = _mlp_ple_bwd(0, dx3, sv0, row("mlp_norm", 0), row("ple_norm", 0),
                           W["w_up"][0], W["w_down"][0], W["w_ple_gate"][0])
    do_attn = _mm("attn_out_dx", dx1, W["attn_w_o"], tb=True)
    d_attn_w_o = _mm("attn_out_dw", o_attn, dx1, ta=True, out_dtypes=(BF16,))
    dos, cs = _attn_merge_bwd(do_attn, os_, lses, bd1)
    grads9 = []
    for g in range(3):
        grads9 += list(_attn_bwd(qkvn[g], g, dos[g], lses[g], cs[g]))
    dqkv, dgains = _attn_prep_bwd(qkv, grads9, gains, ct, st, consts, bd)
    dh0 = _mm("attn_qkv_dx", dqkv, W["attn_w_qkv"], tb=True)
    d_attn_w_qkv = _mm("attn_qkv_dw", h0, dqkv, ta=True, out_dtypes=(BF16,))
    dx0, d_mix0 = _rmsnorm_bwd("mix_norm_bwd0", x, row("mix_norm", 0), dh0, dx1)

    dg = jnp.stack([t.reshape(A_HEADS, A_HEAD_DIM).sum(0) for t in dgains])
    small = dict(
        mix_norm=jnp.concatenate([d_mix0, d_mix1], 0),
        attn_q_gain=dg[0::2][None], attn_k_gain=dg[1::2][None],
        dn_a_log=d_alog[:, :DN_HEADS], dn_dt_bias=d_dt[:, :DN_HEADS], dn_o_gain=d_ogain,
        mlp_norm=jnp.concatenate([g0["mlp_norm"], g1["mlp_norm"]], 0),
        ple_norm=jnp.concatenate([g0["ple_norm"], g1["ple_norm"]], 0),
    )
    big = dict(attn_w_qkv=d_attn_w_qkv, attn_w_o=d_attn_w_o, dn_w_qkvz=d_w_qkvz, dn_w_ab=d_w_ab,
               dn_conv=jnp.concatenate(d_conv, 0), dn_w_o=d_dn_w_o,
               w_up=[g0["w_up"], g1["w_up"]], w_down=[g0["w_down"], g1["w_down"]],
               w_ple=[g0["w_ple"], g1["w_ple"]], w_ple_gate=[g0["w_ple_gate"], g1["w_ple_gate"]])
    return sq, dx0, small, big


MESH_IDS = pl.DeviceIdType.MESH
ANY = pl.BlockSpec(memory_space=pl.ANY)


def _place():
    return lax.axis_index("x"), lax.axis_index("y"), lax.axis_index("c")


def _sem_scratch(n_streams):
    return [pltpu.SemaphoreType.DMA((n_streams, N_DEV - 1)), pltpu.SemaphoreType.DMA((n_streams, N_DEV - 1)),
            pltpu.SemaphoreType.DMA((n_streams,))]


def _all_gather(name, arrays, streams):
    n_in, n_st = len(arrays), len(streams)
    shapes = [arrays[a].shape if li is None else arrays[a].shape[1:] for a, li in streams]

    def body(*refs):
        in_refs, out_refs = refs[:n_in], refs[n_in:n_in + n_st]
        send_sems, recv_sems, local_sems = refs[n_in + n_st:]
        x, y, c = _place()
        me, sibling = (x, y, c), (x, y, 1 - c)
        chips = [(1 - x, y), (x, 1 - y), (1 - x, 1 - y)]

        def copy(s, k, block, to, own=False):
            a, li = streams[s]
            dst = out_refs[s].at[4 * block[0] + 2 * block[1] + block[2]]
            src = (in_refs[a] if li is None else in_refs[a].at[li]) if own else dst
            return pltpu.make_async_remote_copy(src_ref=src, dst_ref=dst, send_sem=send_sems.at[s, k],
                                                recv_sem=recv_sems.at[s, k], device_id=to, device_id_type=MESH_IDS)

        started = []
        for s, (a, li) in enumerate(streams):
            src = in_refs[a] if li is None else in_refs[a].at[li]
            mine = pltpu.make_async_copy(src, out_refs[s].at[4 * x + 2 * y + c], local_sems.at[s])
            mine.start()
            started.append(mine)
        sends = []
        for s in range(n_st):
            first = [copy(s, 0, me, sibling, own=True)]
            first += [copy(s, 1 + j, me, (*chip, c), own=True) for j, chip in enumerate(chips)]
            for cp in first:
                cp.start()
            sends += first
        for j, chip in enumerate(chips):
            for s in range(n_st):
                copy(s, 1 + j, (*chip, c), me).wait_recv()
                fwd = copy(s, 4 + j, (*chip, c), sibling)
                fwd.start()
                sends.append(fwd)
        for s in range(n_st):
            copy(s, 0, sibling, me).wait_recv()
            for j, chip in enumerate(chips):
                copy(s, 4 + j, (*chip, 1 - c), me).wait_recv()
        for cp in sends:
            cp.wait_send()
        for cp in started:
            cp.wait()

    return pl.pallas_call(
        body, name=name,
        out_shape=[jax.ShapeDtypeStruct((N_DEV,) + tuple(sh), arrays[a].dtype) for sh, (a, _) in zip(shapes, streams)],
        in_specs=[ANY] * n_in, out_specs=[ANY] * n_st, scratch_shapes=_sem_scratch(n_st),
    )(*arrays)


def _exchange(name, sends, recv_shapes, placement):
    n_st, n_out = len(sends), len(recv_shapes)

    def body(*refs):
        send_refs, recv_refs = refs[:n_st], refs[n_st:n_st + n_out]
        send_sems, recv_sems, local_sems = refs[n_st + n_out:]
        x, y, c = _place()
        me = 4 * x + 2 * y + c

        def landing(s, slot):
            r, off = placement[s]
            rows = sends[s].shape[1]
            if rows == recv_shapes[r][1]:
                return recv_refs[r].at[slot]
            return recv_refs[r].at[slot, pl.ds(off, rows)]

        local, copies, arrivals = [], [], []
        for s in range(n_st):
            cp = pltpu.make_async_copy(send_refs[s].at[me], landing(s, me), local_sems.at[s])
            cp.start()
            local.append(cp)
        for k in range(1, N_DEV):
            px = 1 - x if k & 4 else x
            py = 1 - y if k & 2 else y
            pc = 1 - c if k & 1 else c
            peer = 4 * px + 2 * py + pc
            for s in range(n_st):
                copies.append(pltpu.make_async_remote_copy(
                    src_ref=send_refs[s].at[peer], dst_ref=landing(s, me), send_sem=send_sems.at[s, k - 1],
                    recv_sem=recv_sems.at[s, k - 1], device_id=(px, py, pc), device_id_type=MESH_IDS))
                arrivals.append(pltpu.make_async_remote_copy(
                    src_ref=send_refs[s].at[peer], dst_ref=landing(s, peer), send_sem=send_sems.at[s, k - 1],
                    recv_sem=recv_sems.at[s, k - 1], device_id=(px, py, pc), device_id_type=MESH_IDS))
        for cp in copies:
            cp.start()
        for cp in arrivals:
            cp.wait_recv()
        for cp in copies:
            cp.wait_send()
        for cp in local:
            cp.wait()

    dtypes = {r: sends[s].dtype for s, (r, _) in enumerate(placement)}
    return pl.pallas_call(
        body, name=name, out_shape=[jax.ShapeDtypeStruct(sh, dtypes[r]) for r, sh in enumerate(recv_shapes)],
        in_specs=[ANY] * n_st, out_specs=[ANY] * n_out, scratch_shapes=_sem_scratch(n_st),
    )(*sends)


def _dn_in_pieces():
    n = (DN_QKVZ + 2 * DN_HEADS) // N_DEV
    segs = ((0, DN_QKV, 0, 0), (DN_QKV, DN_QKV + 2 * DN_HEADS, 1, 0), (DN_QKV + 2 * DN_HEADS, DN_QKVZ + 2 * DN_HEADS, 0, DN_QKV))
    out = []
    for d in range(N_DEV):
        lo, hi = d * n, (d + 1) * n
        for s0, s1, tgt, t0 in segs:
            a, b = max(lo, s0), min(hi, s1)
            if a < b:
                out.append((d, a - lo, b - lo, tgt, t0 + a - s0))
    return out


def _unpack_cols(name, g):
    _, K, n = g.shape
    tr = 256

    def body(g_ref, o_ref):
        for d in range(N_DEV):
            o_ref[:, d * n:(d + 1) * n] = g_ref[d]

    return pl.pallas_call(
        body, name=name, grid=(K // tr,), in_specs=[pl.BlockSpec((N_DEV, tr, n), lambda i: (0, i, 0))],
        out_specs=pl.BlockSpec((tr, N_DEV * n), lambda i: (i, 0)),
        out_shape=jax.ShapeDtypeStruct((K, N_DEV * n), g.dtype), compiler_params=_cparams(("parallel",)),
    )(g)


def _pack_cols(name, w):
    K, n = w.shape[0], w.shape[1] // N_DEV
    tr = 256

    def body(w_ref, o_ref):
        for d in range(N_DEV):
            o_ref[d] = w_ref[:, d * n:(d + 1) * n]

    return pl.pallas_call(
        body, name=name, grid=(K // tr,), in_specs=[pl.BlockSpec((tr, N_DEV * n), lambda i: (i, 0))],
        out_specs=pl.BlockSpec((N_DEV, tr, n), lambda i: (0, i, 0)),
        out_shape=jax.ShapeDtypeStruct((N_DEV, K, n), w.dtype), compiler_params=_cparams(("parallel",)),
    )(w)


def _unpack_dn_in(g):
    _, K, n = g.shape
    tr = 256

    def body(g_ref, qkvz_ref, ab_ref):
        ab_ref[...] = jnp.zeros_like(ab_ref)
        for d, c0, c1, tgt, t0 in _dn_in_pieces():
            (qkvz_ref, ab_ref)[tgt][:, t0:t0 + c1 - c0] = g_ref[d, :, c0:c1]

    return pl.pallas_call(
        body, name="unpack_dn_in", grid=(K // tr,), in_specs=[pl.BlockSpec((N_DEV, tr, n), lambda i: (0, i, 0))],
        out_specs=[pl.BlockSpec((tr, DN_QKVZ), lambda i: (i, 0)), pl.BlockSpec((tr, LANE), lambda i: (i, 0))],
        out_shape=[jax.ShapeDtypeStruct((K, DN_QKVZ), g.dtype), jax.ShapeDtypeStruct((K, LANE), g.dtype)],
        compiler_params=_cparams(("parallel",)),
    )(g)


def _pack_dn_in(d_qkvz, d_ab):
    K = d_qkvz.shape[0]
    n = (DN_QKVZ + 2 * DN_HEADS) // N_DEV
    tr = 256

    def body(qkvz_ref, ab_ref, o_ref):
        for d, c0, c1, tgt, t0 in _dn_in_pieces():
            o_ref[d, :, c0:c1] = (qkvz_ref, ab_ref)[tgt][:, t0:t0 + c1 - c0]

    return pl.pallas_call(
        body, name="pack_dn_in", grid=(K // tr,),
        in_specs=[pl.BlockSpec((tr, DN_QKVZ), lambda i: (i, 0)), pl.BlockSpec((tr, LANE), lambda i: (i, 0))],
        out_specs=pl.BlockSpec((N_DEV, tr, n), lambda i: (0, i, 0)),
        out_shape=jax.ShapeDtypeStruct((N_DEV, K, n), d_qkvz.dtype), compiler_params=_cparams(("parallel",)),
    )(d_qkvz, d_ab)


ADAMW_ROWS = 256


def _adamw(name, parts, w, m, v):
    R, C = w.shape
    tr = min(R, ADAMW_ROWS)
    assert R % tr == 0 and parts.shape == (N_DEV, R, C)
    c1 = 1.0 - B1 ** STEP
    c2 = 1.0 - B2 ** STEP

    def body(p_ref, w_ref, m_ref, v_ref, g_ref, d_ref, nm_ref, nv_ref):
        g = p_ref[0].astype(F32)
        for dev in range(1, N_DEV):
            g = g + p_ref[dev].astype(F32)
        nm = B1 * m_ref[...] + (1.0 - B1) * g
        nv = B2 * v_ref[...] + (1.0 - B2) * jnp.square(g)
        g_ref[...] = g
        nm_ref[...] = nm
        nv_ref[...] = nv
        d_ref[...] = -LR * ((nm / c1) / (jnp.sqrt(nv / c2) + ADAM_EPS) + WD * w_ref[...])

    blk = pl.BlockSpec((tr, C), lambda i: (i, 0))
    return pl.pallas_call(
        body, name=name, grid=(R // tr,),
        in_specs=[pl.BlockSpec((N_DEV, tr, C), lambda i: (0, i, 0)), blk, blk, blk],
        out_specs=[blk] * 4, out_shape=[jax.ShapeDtypeStruct((R, C), F32)] * 4,
        compiler_params=_cparams(("parallel",)),
    )(parts, w, m, v)


SMALL = ("mix_norm", "attn_q_gain", "attn_k_gain", "dn_a_log", "dn_dt_bias", "dn_o_gain", "mlp_norm", "ple_norm")
WEIGHTS = ("mix_norm", "attn_w_qkv", "attn_q_gain", "attn_k_gain", "attn_w_o", "dn_w_in", "dn_conv", "dn_a_log",
           "dn_dt_bias", "dn_o_gain", "dn_w_o", "mlp_norm", "w_up", "w_down", "ple_norm", "w_ple", "w_ple_gate")


def _to_rows(flat, multiple):
    n = flat.shape[-1]
    rows = -(-n // (LANE * multiple)) * multiple
    return jnp.pad(flat, [(0, rows * LANE - n)]).reshape(rows, LANE)


def _cols_to_devices(w):
    K, N = w.shape
    return jnp.transpose(w.reshape(K, N_DEV, N // N_DEV), (1, 0, 2))


def _cols_from_devices(g):
    _, K, n = g.shape
    return jnp.transpose(g, (1, 0, 2)).reshape(K, N_DEV * n)


SMALL_ROWS = 56


def _pack_small(vals, loss_row):
    rows = []
    for n in SMALL:
        rows.append(_to_rows(vals[n].reshape(-1), 1))
    rows.append(loss_row)
    buf = jnp.concatenate(rows, 0)
    assert buf.shape == (SMALL_ROWS, LANE)
    return buf


def _unpack_small(buf, like):
    out, r = {}, 0
    for n in SMALL:
        sz = math.prod(like[n].shape)
        nr = -(-sz // LANE)
        out[n] = buf[r:r + nr].reshape(-1)[:sz].reshape(like[n].shape)
        r += nr
    return out, buf[r]


def kernel(x, p, positions, mix_norm, attn_w_qkv, attn_q_gain, attn_k_gain, attn_w_o, dn_w_in, dn_conv, dn_a_log, dn_dt_bias, dn_o_gain, dn_w_o, mlp_norm, w_up, w_down, ple_norm, w_ple, w_ple_gate, loss_target, m_mix_norm, m_attn_w_qkv, m_attn_q_gain, m_attn_k_gain, m_attn_w_o, m_dn_w_in, m_dn_conv, m_dn_a_log, m_dn_dt_bias, m_dn_o_gain, m_dn_w_o, m_mlp_norm, m_w_up, m_w_down, m_ple_norm, m_w_ple, m_w_ple_gate, v_mix_norm, v_attn_w_qkv, v_attn_q_gain, v_attn_k_gain, v_attn_w_o, v_dn_w_in, v_dn_conv, v_dn_a_log, v_dn_dt_bias, v_dn_o_gain, v_dn_w_o, v_mlp_norm, v_w_up, v_w_down, v_ple_norm, v_w_ple, v_w_ple_gate):
    w = dict(mix_norm=mix_norm, attn_w_qkv=attn_w_qkv, attn_q_gain=attn_q_gain, attn_k_gain=attn_k_gain, attn_w_o=attn_w_o,
             dn_w_in=dn_w_in, dn_conv=dn_conv, dn_a_log=dn_a_log, dn_dt_bias=dn_dt_bias, dn_o_gain=dn_o_gain, dn_w_o=dn_w_o,
             mlp_norm=mlp_norm, w_up=w_up, w_down=w_down, ple_norm=ple_norm, w_ple=w_ple, w_ple_gate=w_ple_gate)
    m = dict(mix_norm=m_mix_norm, attn_w_qkv=m_attn_w_qkv, attn_q_gain=m_attn_q_gain, attn_k_gain=m_attn_k_gain,
             attn_w_o=m_attn_w_o, dn_w_in=m_dn_w_in, dn_conv=m_dn_conv, dn_a_log=m_dn_a_log, dn_dt_bias=m_dn_dt_bias,
             dn_o_gain=m_dn_o_gain, dn_w_o=m_dn_w_o, mlp_norm=m_mlp_norm, w_up=m_w_up, w_down=m_w_down,
             ple_norm=m_ple_norm, w_ple=m_w_ple, w_ple_gate=m_w_ple_gate)
    v = dict(mix_norm=v_mix_norm, attn_w_qkv=v_attn_w_qkv, attn_q_gain=v_attn_q_gain, attn_k_gain=v_attn_k_gain,
             attn_w_o=v_attn_w_o, dn_w_in=v_dn_w_in, dn_conv=v_dn_conv, dn_a_log=v_dn_a_log, dn_dt_bias=v_dn_dt_bias,
             dn_o_gain=v_dn_o_gain, dn_w_o=v_dn_w_o, mlp_norm=v_mlp_norm, w_up=v_w_up, w_down=v_w_down,
             ple_norm=v_ple_norm, w_ple=v_w_ple, w_ple_gate=v_w_ple_gate)
    S = x.shape[1]

    bf = lambda a: a.astype(BF16)
    shards = [bf(attn_w_qkv[0]), bf(attn_w_o[0]), bf(dn_w_in[0]), bf(dn_w_o[0]), bf(w_up), bf(w_down), bf(w_ple),
              bf(w_ple_gate), dn_conv[0]]
    streams = [(0, None), (1, None), (2, None), (3, None), (4, 0), (4, 1), (5, 0), (5, 1), (6, 0), (6, 1), (7, 0), (7, 1),
               (8, None)]
    (g_qkv, g_ao, g_in, g_do, g_up0, g_up1, g_dn0, g_dn1, g_pl0, g_pl1, g_gt0, g_gt1, g_conv) = _all_gather(
        "gather_weights", shards, streams)
    W = dict(
        attn_w_qkv=_unpack_cols("unpack_attn_qkv", g_qkv), attn_w_o=_cols_from_devices(g_ao),
        dn_conv=jnp.transpose(g_conv, (1, 0, 2)).reshape(CONV_W, DN_QKV), dn_w_o=g_do.reshape(DN_WIDTH, D_MODEL),
        w_up=[_cols_from_devices(g_up0), _cols_from_devices(g_up1)],
        w_down=[g_dn0.reshape(D_FF, D_MODEL), g_dn1.reshape(D_FF, D_MODEL)],
        w_ple=[_cols_from_devices(g_pl0), _cols_from_devices(g_pl1)],
        w_ple_gate=[g_gt0.reshape(D_MODEL, D_MODEL), g_gt1.reshape(D_MODEL, D_MODEL)])
    W["dn_w_qkvz"], W["dn_w_ab"] = _unpack_dn_in(g_in)
    P = dict(mix_norm=mix_norm, attn_q_gain=attn_q_gain[0], attn_k_gain=attn_k_gain[0], dn_a_log=dn_a_log[0],
             dn_dt_bias=dn_dt_bias[0], dn_o_gain=dn_o_gain[0], mlp_norm=mlp_norm, ple_norm=ple_norm)

    sq, dx0, small_g, big_g = _local_step(x[0], p[:, 0], positions.reshape(S, 1), loss_target[0], W, P)

    rows_to_devices = lambda t: t.reshape(N_DEV, t.shape[0] // N_DEV, t.shape[1])
    conv_send = jnp.transpose(big_g["dn_conv"].reshape(CONV_W, N_DEV, DN_QKV // N_DEV), (1, 0, 2))
    sends = [_pack_cols("pack_attn_qkv", big_g["attn_w_qkv"]), _cols_to_devices(big_g["attn_w_o"]),
             _pack_dn_in(big_g["dn_w_qkvz"], big_g["dn_w_ab"]), conv_send, rows_to_devices(big_g["dn_w_o"]),
             _cols_to_devices(big_g["w_up"][0]), _cols_to_devices(big_g["w_up"][1]),
             rows_to_devices(big_g["w_down"][0]), rows_to_devices(big_g["w_down"][1]),
             _cols_to_devices(big_g["w_ple"][0]), _cols_to_devices(big_g["w_ple"][1]),
             rows_to_devices(big_g["w_ple_gate"][0]), rows_to_devices(big_g["w_ple_gate"][1])]
    names = ("attn_w_qkv", "attn_w_o", "dn_w_in", "dn_conv", "dn_w_o", "w_up", "w_down", "w_ple", "w_ple_gate")
    shard2d = {n: (math.prod(w[n].shape[:-1]), w[n].shape[-1]) for n in names}
    placement = [(0, 0), (1, 0), (2, 0), (3, 0), (4, 0), (5, 0), (5, D_MODEL), (6, 0), (6, D_FF // N_DEV),
                 (7, 0), (7, PLE_DIM), (8, 0), (8, D_MODEL // N_DEV)]
    recvs = _exchange("exchange_grads", sends, [(N_DEV,) + shard2d[n] for n in names], placement)
    big = {}
    for n, parts in zip(names, recvs):
        res = _adamw("adamw_" + n, parts, w[n].reshape(shard2d[n]), m[n].reshape(shard2d[n]), v[n].reshape(shard2d[n]))
        big[n] = [r.reshape(w[n].shape) for r in res]

    loss_row = jnp.pad((0.5 / D_MODEL) * jnp.sum(sq, axis=1, keepdims=True), ((0, 0), (0, LANE - 1)))
    small_like = {n: w[n] for n in SMALL}
    parts_s = _all_gather("gather_small", [_pack_small(small_g, loss_row)], [(0, None)])[0]
    zero_row = jnp.zeros((1, LANE), F32)
    small = _adamw("adamw_small", parts_s, _pack_small(w, zero_row), _pack_small(m, zero_row), _pack_small(v, zero_row))
    loss = small[0][SMALL_ROWS - 1, 0]
    small = [_unpack_small(b, small_like)[0] for b in small]

    outs = [loss, dx0[None]]
    for k in range(4):
        for n in WEIGHTS:
            outs.append(small[k][n] if n in SMALL else big[n][k])
    return tuple(outs)
```

```python
import functools
import math

import jax
import jax.numpy as jnp
from jax import lax
from jax.experimental import pallas as pl
from jax.experimental.pallas import tpu as pltpu

F32 = jnp.float32
BF16 = jnp.bfloat16
HIGHEST = lax.Precision.HIGHEST

N_DEV = 8
D_MODEL = 1024
EPS = 1e-6
SWA_GROUPS = ((128, 1), (512, 4), (2048, 16))
A_HEADS = 8
A_HEAD_DIM = 64
A_WIDTH = A_HEADS * A_HEAD_DIM
A_QKV = 3 * 3 * A_WIDTH
ROPE_DIM = 16
ROPE_HALF = 8
ROPE_THETA = 500000.0
BAND = 128
DN_HEADS = 8
DN_DIM = 128
DN_WIDTH = DN_HEADS * DN_DIM
CONV_W = 4
CHUNK = 64
D_FF = 4 * D_MODEL
PLE_DIM = 256
LR, B1, B2, ADAM_EPS, WD, STEP = 0.001, 0.9, 0.999, 1e-08, 0.01, 10

VMEM_LIMIT = 56 * 1024 * 1024
MXU_TILE = 1024
LANE = 128
SUBLANE = 8


def _cparams(sem):
    return pltpu.CompilerParams(dimension_semantics=sem, vmem_limit_bytes=VMEM_LIMIT)


def _tile(n, pref):
    if n <= pref:
        return n
    t = (pref // LANE) * LANE
    while t >= LANE:
        if n % t == 0:
            return t
        t -= LANE
    raise ValueError(f"no tile for {n}")


def _dot(a, b, ca=1, cb=0, precision=None):
    return lax.dot_general(a, b, (((ca,), (cb,)), ((), ())), precision=precision,
                           preferred_element_type=F32)


def _bdot(a, b, ca=1, cb=0):
    return _dot(a.astype(BF16), b.astype(BF16), ca, cb)


def _mm(name, a, b, *, ta=False, tb=False, epilogue=None, extras=(), out_dtypes=(F32,),
        tm_pref=MXU_TILE, tn_pref=1536, tk_pref=MXU_TILE):
    M, K = (a.shape[1], a.shape[0]) if ta else a.shape
    N = b.shape[0] if tb else b.shape[1]
    assert (b.shape[1] if tb else b.shape[0]) == K
    tm, tn, tk = _tile(M, tm_pref), _tile(N, tn_pref), _tile(K, tk_pref)
    nk = K // tk
    n_out = len(out_dtypes)
    n_ext = len(extras)

    def body(*refs):
        a_ref, b_ref = refs[0], refs[1]
        ext = refs[2:2 + n_ext]
        outs = refs[2 + n_ext:2 + n_ext + n_out]
        k = pl.program_id(2)
        prod = _bdot(a_ref[...], b_ref[...], 0 if ta else 1, 1 if tb else 0)

        def finish(r):
            res = (r,) if epilogue is None else epilogue(r, *[e[...] for e in ext])
            for o, v in zip(outs, res):
                o[...] = v.astype(o.dtype)

        if nk == 1:
            finish(prod)
            return
        acc = refs[-1]

        @pl.when(k == 0)
        def _():
            acc[...] = jnp.zeros_like(acc)

        acc[...] += prod

        @pl.when(k == nk - 1)
        def _():
            finish(acc[...])

    a_spec = pl.BlockSpec((tk, tm), lambda i, j, k: (k, i)) if ta else pl.BlockSpec((tm, tk), lambda i, j, k: (i, k))
    b_spec = pl.BlockSpec((tn, tk), lambda i, j, k: (j, k)) if tb else pl.BlockSpec((tk, tn), lambda i, j, k: (k, j))
    ext_specs = []
    for e in extras:
        if e.shape[0] == 1 and M != 1:
            ext_specs.append(pl.BlockSpec((1, tn), lambda i, j, k: (0, j)))
        else:
            ext_specs.append(pl.BlockSpec((tm, tn), lambda i, j, k: (i, j)))
    out = pl.pallas_call(
        body, name=name,
        grid=(M // tm, N // tn, nk),
        in_specs=[a_spec, b_spec] + ext_specs,
        out_specs=[pl.BlockSpec((tm, tn), lambda i, j, k: (i, j)) for _ in range(n_out)],
        out_shape=[jax.ShapeDtypeStruct((M, N), dt) for dt in out_dtypes],
        scratch_shapes=[pltpu.VMEM((tm, tn), F32)] if nk > 1 else [],
        compiler_params=_cparams(("parallel", "parallel", "arbitrary")),
    )(a, b, *extras)
    return out[0] if n_out == 1 else tuple(out)


def _rows(name, fn, ins, outs, *, tr, accs=()):
    ins = [(e[0], e[1]) + (e[2] if len(e) > 2 else (0, e[0].shape[-1])) for e in ins]
    n_rows = next(e[0].shape[0] for e in ins if e[1] == "row")
    assert n_rows % tr == 0 and tr % SUBLANE == 0
    steps = n_rows // tr
    t8 = tr // SUBLANE
    n8 = n_rows // SUBLANE
    n_in, n_out, n_acc = len(ins), len(outs), len(accs)

    def body(*refs):
        i = pl.program_id(0)
        vals = fn(i, steps, *[r[...] for r in refs[:n_in]])
        if not isinstance(vals, (tuple, list)):
            vals = (vals,)
        assert len(vals) == n_out + n_acc
        for o, v in zip(refs[n_in:n_in + n_out], vals[:n_out]):
            o[...] = v.astype(o.dtype)
        if n_acc:
            acc_refs = refs[n_in + n_out:]

            @pl.when(i == 0)
            def _():
                for r in acc_refs:
                    r[...] = jnp.zeros_like(r)

            for r, v in zip(acc_refs, vals[n_out:]):
                r[...] += v.astype(r.dtype)

    in_specs = []
    for a, kind, cb, c in ins:
        if kind == "row":
            in_specs.append(pl.BlockSpec((tr, c), lambda i, cb=cb: (i, cb)))
        elif kind == "full":
            in_specs.append(pl.BlockSpec(a.shape, lambda i, z=(0,) * a.ndim: z))
        elif kind == "prev8":
            in_specs.append(pl.BlockSpec((SUBLANE, c), lambda i, cb=cb: (jnp.maximum(i * t8 - 1, 0), cb)))
        elif kind == "next8":
            in_specs.append(pl.BlockSpec((SUBLANE, c), lambda i, cb=cb: (jnp.minimum((i + 1) * t8, n8 - 1), cb)))
        else:
            raise ValueError(kind)
    out_specs = [pl.BlockSpec((tr, c), lambda i: (i, 0)) for c, _ in outs]
    out_specs += [pl.BlockSpec(s, lambda i, z=(0,) * len(s): z) for s, _ in accs]
    out_shape = [jax.ShapeDtypeStruct((n_rows, c), dt) for c, dt in outs]
    out_shape += [jax.ShapeDtypeStruct(s, dt) for s, dt in accs]
    res = pl.pallas_call(
        body, name=name, grid=(steps,), in_specs=in_specs, out_specs=out_specs, out_shape=out_shape,
        compiler_params=_cparams(("arbitrary",) if n_acc else ("parallel",)),
    )(*[e[0] for e in ins])
    return res[0] if len(res) == 1 else tuple(res)


def _colsum(x):
    return jnp.sum(x, axis=0, keepdims=True)


def _sum_all(x):
    return jnp.sum(jnp.sum(x, axis=1, keepdims=True), axis=0, keepdims=True)


def _rmsnorm_fwd(name, x, gain):
    def fn(i, n, xt, g):
        r = lax.rsqrt(jnp.mean(xt * xt, axis=-1, keepdims=True) + EPS)
        return (xt * r * g,)
    return _rows(name, fn, [(x, "row"), (gain, "full")], [(x.shape[1], BF16)], tr=512)


def _rmsnorm_bwd(name, x, gain, dh, dres):
    def fn(i, n, xt, g, dht, drt):
        r = lax.rsqrt(jnp.mean(xt * xt, axis=-1, keepdims=True) + EPS)
        xh = xt * r
        dxn = dht * g
        dx = r * (dxn - xh * jnp.mean(dxn * xh, axis=-1, keepdims=True))
        return drt + dx, _colsum(dht * xh)
    D = x.shape[1]
    return _rows(name, fn, [(x, "row"), (gain, "full"), (dh, "row"), (dres, "row")], [(D, F32)],
                 tr=256, accs=[((1, D), F32)])


def _head_consts():
    import numpy as np
    e = np.arange(A_WIDTH) % A_HEAD_DIM
    inv = (np.float32(ROPE_THETA) ** (-np.arange(0, ROPE_DIM, 2, dtype=np.float32) / np.float32(ROPE_DIM))).astype(np.float32)
    c = np.zeros((8, A_WIDTH), np.float32)
    c[0] = np.where(e < ROPE_DIM, inv[e % ROPE_HALF], 0.0)
    c[1] = np.where(e < ROPE_HALF, -1.0, np.where(e < ROPE_DIM, 1.0, 0.0))
    c[2] = (e < ROPE_HALF).astype(np.float32)
    c[3] = (e < ROPE_DIM).astype(np.float32)
    return jnp.asarray(c)


def _block_diag(scale):
    import numpy as np
    h = np.arange(A_WIDTH) // A_HEAD_DIM
    return jnp.asarray((h[:, None] == h[None, :]).astype(np.float32) * scale, dtype=BF16)


def _seg_sum(x, bd):
    hi = x.astype(BF16)
    lo = (x - hi.astype(F32)).astype(BF16)
    return _dot(hi, bd) + _dot(lo, bd)


def _rope_tables(positions, consts):
    def fn(i, n, pos, c):
        ang = pos.astype(F32) * c[0:1, :]
        return jnp.cos(ang), jnp.sin(ang) * c[1:2, :]
    return _rows("rope_tables", fn, [(positions, "row"), (consts, "full")],
                 [(A_WIDTH, F32), (A_WIDTH, F32)], tr=512)


def _rope_apply(y, ct, st, low):
    rolled = jnp.where(low, pltpu.roll(y, A_WIDTH - ROPE_HALF, 1), pltpu.roll(y, ROPE_HALF, 1))
    return y * ct + rolled * st


def _rope_apply_bwd(dout, ct, st, low, in16):
    t = dout * st
    back = jnp.where(low, pltpu.roll(t, A_WIDTH - ROPE_HALF, 1), jnp.where(in16, pltpu.roll(t, ROPE_HALF, 1), 0.0))
    return dout * ct + back


def _attn_prep(qkv, gains, ct, st, consts, bd):
    def fn(i, n, t, g, c_t, s_t, c, b):
        low = c[2:3, :] > 0.5
        groups = []
        for grp in range(3):
            cols = []
            for which in range(3):
                off = (grp * 3 + which) * A_WIDTH
                x = t[:, off:off + A_WIDTH]
                if which == 2:
                    cols.append(x.astype(BF16))
                    continue
                r = lax.rsqrt(_seg_sum(x * x, b) + EPS)
                y = x * r * g[grp * 2 + which:grp * 2 + which + 1, :]
                cols.append(_rope_apply(y, c_t, s_t, low).astype(BF16))
            groups.append(jnp.concatenate(cols, axis=1))
        return tuple(groups)
    return _rows("attn_prep", fn, [(qkv, "row"), (gains, "full"), (ct, "row"), (st, "row"), (consts, "full"), (bd, "full")],
                 [(3 * A_WIDTH, BF16)] * 3, tr=256)


def _band_mask(n):
    row = lax.broadcasted_iota(jnp.int32, (BAND, 2 * BAND), 0)
    col = lax.broadcasted_iota(jnp.int32, (BAND, 2 * BAND), 1)
    dist = row + BAND - col
    return (dist >= 0) & (dist <= BAND) & ((col >= BAND) | (n > 0))


def _attn_fwd(qkvn, grp):
    S = qkvn.shape[0]
    d = SWA_GROUPS[grp][1]
    L = S // d
    nblk = L // BAND
    assert L % BAND == 0
    view = qkvn.reshape(L, d * 3 * A_WIDTH)

    def body(q_ref, kc_ref, kp_ref, vc_ref, vp_ref, o_ref, lse_ref):
        n = pl.program_id(1)
        valid = _band_mask(n)
        first = lax.broadcasted_iota(jnp.int32, (BAND, LANE), 1) < A_HEAD_DIM
        pairs = [slice(pr * LANE, (pr + 1) * LANE) for pr in range(A_WIDTH // LANE)]
        halves = (first, jnp.logical_not(first))
        qps = [q_ref[:, sl] for sl in pairs]
        kcats = [jnp.concatenate([kp_ref[:, sl], kc_ref[:, sl]], axis=0) for sl in pairs]
        vcats = [jnp.concatenate([vp_ref[:, sl], vc_ref[:, sl]], axis=0) for sl in pairs]
        heads = [(pr, m) for pr in range(len(pairs)) for m in halves]
        ss = [_dot(jnp.where(m, qps[pr], jnp.zeros_like(qps[pr])), kcats[pr], 1, 1) for pr, m in heads]
        ps, lses = [], []
        for s in ss:
            s = jnp.where(valid, s * (A_HEAD_DIM ** -0.5), -1e30)
            mx = jnp.max(s, axis=-1, keepdims=True)
            e = jnp.exp(s - mx)
            l = jnp.sum(e, axis=-1, keepdims=True)
            ps.append((e / l).astype(BF16))
            lses.append(mx + jnp.log(l))
        os_ = [_dot(p, vcats[pr]) for p, (pr, _) in zip(ps, heads)]
        o_ref[...] = jnp.concatenate([jnp.where(first, os_[2 * pr], os_[2 * pr + 1]) for pr in range(len(pairs))], axis=1)
        lse_ref[...] = jnp.concatenate([jnp.where(first, lses[2 * pr], lses[2 * pr + 1]) for pr in range(len(pairs))], axis=1)

    blk = (BAND, A_WIDTH)
    o, lse = pl.pallas_call(
        body, name=f"attn_fwd_g{grp}", grid=(d, nblk),
        in_specs=[pl.BlockSpec(blk, lambda r, n: (n, r * 3)),
                  pl.BlockSpec(blk, lambda r, n: (n, r * 3 + 1)),
                  pl.BlockSpec(blk, lambda r, n: (jnp.maximum(n - 1, 0), r * 3 + 1)),
                  pl.BlockSpec(blk, lambda r, n: (n, r * 3 + 2)),
                  pl.BlockSpec(blk, lambda r, n: (jnp.maximum(n - 1, 0), r * 3 + 2))],
        out_specs=[pl.BlockSpec(blk, lambda r, n: (n, r)), pl.BlockSpec(blk, lambda r, n: (n, r))],
        out_shape=[jax.ShapeDtypeStruct((L, d * A_WIDTH), F32)] * 2,
        compiler_params=_cparams(("parallel", "parallel")),
    )(view, view, view, view, view)
    return o.reshape(S, A_WIDTH), lse.reshape(S, A_WIDTH)


def _merge_weights(l0, l1, l2):
    mx = jnp.maximum(jnp.maximum(l0, l1), l2)
    e0, e1, e2 = jnp.exp(l0 - mx), jnp.exp(l1 - mx), jnp.exp(l2 - mx)
    inv = 1.0 / (e0 + e1 + e2)
    return e0 * inv, e1 * inv, e2 * inv


def _attn_merge(os_, lses):
    def fn(i, n, o0, o1, o2, l0, l1, l2):
        w0, w1, w2 = _merge_weights(l0, l1, l2)
        return (w0 * o0 + w1 * o1 + w2 * o2,)
    ins = [(a, "row") for a in (*os_, *lses)]
    return _rows("attn_merge", fn, ins, [(A_WIDTH, BF16)], tr=512)


def _attn_merge_bwd(do, os_, lses, bd1):
    def fn(i, n, dot_, o0, o1, o2, l0, l1, l2, b):
        w0, w1, w2 = _merge_weights(l0, l1, l2)
        o = w0 * o0 + w1 * o1 + w2 * o2
        dsum = _seg_sum(dot_ * o, b)
        return (w0 * dot_, w1 * dot_, w2 * dot_, -w0 * dsum, -w1 * dsum, -w2 * dsum)
    ins = [(do, "row")] + [(a, "row") for a in (*os_, *lses)] + [(bd1, "full")]
    res = _rows("attn_merge_bwd", fn, ins, [(A_WIDTH, BF16)] * 3 + [(A_WIDTH, F32)] * 3, tr=256)
    return res[:3], res[3:]


def _lane_pick(x, lane_idx, lane):
    return jnp.sum(jnp.where(lane_idx == lane, x, 0.0), axis=-1, keepdims=True)


def _attn_bwd(qkvn, grp, do_g, lse, c_g):
    S = qkvn.shape[0]
    d = SWA_GROUPS[grp][1]
    L = S // d
    nblk = L // BAND
    view = qkvn.reshape(L, d * 3 * A_WIDTH)
    dov, lsev, cv = (t.reshape(L, d * A_WIDTH) for t in (do_g, lse, c_g))

    def body(q_ref, kc_ref, kp_ref, vc_ref, vp_ref, do_ref, lse_ref, c_ref, dq_ref, dk_ref, dv_ref, ck, cv_):
        n = pl.program_id(1)

        @pl.when(n == 0)
        def _():
            ck[...] = jnp.zeros_like(ck)
            cv_[...] = jnp.zeros_like(cv_)

        @pl.when(n < nblk)
        def _():
            valid = _band_mask(n)
            lane = lax.broadcasted_iota(jnp.int32, (BAND, LANE), 1)
            first = lane < A_HEAD_DIM
            lane2 = lax.broadcasted_iota(jnp.int32, (2 * BAND, LANE), 1) < A_HEAD_DIM
            pairs = [slice(pr * LANE, (pr + 1) * LANE) for pr in range(A_WIDTH // LANE)]
            halves = (first, jnp.logical_not(first))
            qps = [q_ref[:, sl] for sl in pairs]
            dops = [do_ref[:, sl] for sl in pairs]
            kcats = [jnp.concatenate([kp_ref[:, sl], kc_ref[:, sl]], axis=0) for sl in pairs]
            vcats = [jnp.concatenate([vp_ref[:, sl], vc_ref[:, sl]], axis=0) for sl in pairs]
            heads = [(pr, hh) for pr in range(len(pairs)) for hh in range(2)]
            zero = jnp.zeros_like(qps[0])
            ss = [_dot(jnp.where(halves[hh], qps[pr], zero), kcats[pr], 1, 1) for pr, hh in heads]
            dps = [_dot(jnp.where(halves[hh], dops[pr], zero), vcats[pr], 1, 1) for pr, hh in heads]
            dss, pbs = [], []
            for (pr, hh), s, dp in zip(heads, ss, dps):
                lse_h = _lane_pick(lse_ref[:, pairs[pr]], lane, hh * A_HEAD_DIM)
                c_h = _lane_pick(c_ref[:, pairs[pr]], lane, hh * A_HEAD_DIM)
                p = jnp.where(valid, jnp.exp(s * (A_HEAD_DIM ** -0.5) - lse_h), 0.0)
                dss.append((p * (dp + c_h) * (A_HEAD_DIM ** -0.5)).astype(BF16))
                pbs.append(p.astype(BF16))
            dqs = [_dot(ds, kcats[pr]) for ds, (pr, _) in zip(dss, heads)]
            dks = [_dot(ds, qps[pr], 0, 0) for ds, (pr, _) in zip(dss, heads)]
            dvs = [_dot(pb, dops[pr], 0, 0) for pb, (pr, _) in zip(pbs, heads)]
            for pr, sl in enumerate(pairs):
                dq_ref[:, sl] = jnp.where(first, dqs[2 * pr], dqs[2 * pr + 1])
                dkc = jnp.where(lane2, dks[2 * pr], dks[2 * pr + 1])
                dvc = jnp.where(lane2, dvs[2 * pr], dvs[2 * pr + 1])
                dk_ref[:, sl] = ck[:, sl] + dkc[:BAND]
                dv_ref[:, sl] = cv_[:, sl] + dvc[:BAND]
                ck[:, sl] = dkc[BAND:]
                cv_[:, sl] = dvc[BAND:]

        @pl.when(n == nblk)
        def _():
            dk_ref[...] = ck[...]
            dv_ref[...] = cv_[...]

    blk = (BAND, A_WIDTH)
    last = nblk - 1
    qn = lambda n: jnp.minimum(n, last)
    pn = lambda n: jnp.clip(n - 1, 0, last)
    dq, dk, dv = pl.pallas_call(
        body, name=f"attn_bwd_g{grp}", grid=(d, nblk + 1),
        in_specs=[pl.BlockSpec(blk, lambda r, n: (qn(n), r * 3)),
                  pl.BlockSpec(blk, lambda r, n: (qn(n), r * 3 + 1)),
                  pl.BlockSpec(blk, lambda r, n: (pn(n), r * 3 + 1)),
                  pl.BlockSpec(blk, lambda r, n: (qn(n), r * 3 + 2)),
                  pl.BlockSpec(blk, lambda r, n: (pn(n), r * 3 + 2)),
                  pl.BlockSpec(blk, lambda r, n: (qn(n), r)),
                  pl.BlockSpec(blk, lambda r, n: (qn(n), r)),
                  pl.BlockSpec(blk, lambda r, n: (qn(n), r))],
        out_specs=[pl.BlockSpec(blk, lambda r, n: (qn(n), r)),
                   pl.BlockSpec(blk, lambda r, n: (pn(n), r)),
                   pl.BlockSpec(blk, lambda r, n: (pn(n), r))],
        out_shape=[jax.ShapeDtypeStruct((L, d * A_WIDTH), F32)] * 3,
        scratch_shapes=[pltpu.VMEM(blk, F32), pltpu.VMEM(blk, F32)],
        compiler_params=_cparams(("parallel", "arbitrary")),
    )(view, view, view, view, view, dov, lsev, cv)
    return tuple(t.reshape(S, A_WIDTH) for t in (dq, dk, dv))


def _attn_prep_bwd(qkv, grads, gains, ct, st, consts, bd):
    def fn(i, n, t, g, c_t, s_t, c, b, *gr):
        low = c[2:3, :] > 0.5
        in16 = c[3:4, :] > 0.5
        cols, dgs = [], []
        for grp in range(3):
            for which in range(3):
                dout = gr[grp * 3 + which]
                if which == 2:
                    cols.append(dout.astype(BF16))
                    continue
                off = (grp * 3 + which) * A_WIDTH
                x = t[:, off:off + A_WIDTH]
                gain = g[grp * 2 + which:grp * 2 + which + 1, :]
                r = lax.rsqrt(_seg_sum(x * x, b) + EPS)
                xh = x * r
                dy = _rope_apply_bwd(dout, c_t, s_t, low, in16)
                dyn = dy * gain
                dx = r * (dyn - xh * _seg_sum(dyn * xh, b))
                cols.append(dx.astype(BF16))
                dgs.append(_colsum(dy * xh))
        return (jnp.concatenate(cols, axis=1), *dgs)
    ins = [(qkv, "row"), (gains, "full"), (ct, "row"), (st, "row"), (consts, "full"), (bd, "full")] + [(a, "row") for a in grads]
    res = _rows("attn_prep_bwd", fn, ins, [(A_QKV, BF16)], tr=128, accs=[((1, A_WIDTH), F32)] * 6)
    return res[0], res[1:]


DN_QKV = 3 * DN_WIDTH
DN_QKVZ = DN_QKV + DN_WIDTH


def _sigmoid(x):
    return 1.0 / (1.0 + jnp.exp(-x))


def _softplus(x):
    return jnp.maximum(x, 0.0) + jnp.log(1.0 + jnp.exp(-jnp.abs(x)))


def _conv_taps(xs, w, tr):
    acc = None
    for j in range(CONV_W):
        sh = CONV_W - 1 - j
        term = (pltpu.roll(xs, sh, 0) if sh else xs)[SUBLANE:] * w[j:j + 1, :]
        acc = term if acc is None else acc + term
    return acc


def _dn_prep(qkvz, ab, convw, alog_row, dt_row):
    tr = 256

    def fn(i, n, x, xp, abt, w, al, dt):
        xp = jnp.where(i > 0, xp, 0.0)
        u = _conv_taps(jnp.concatenate([xp, x], axis=0), w, tr)
        y = u * _sigmoid(u)
        qs, ks = [], []
        for h in range(DN_HEADS):
            for dst, base, sc in ((qs, 0, DN_DIM ** -0.5), (ks, DN_WIDTH, 1.0)):
                seg = y[:, base + h * DN_DIM:base + (h + 1) * DN_DIM]
                dst.append(seg * (lax.rsqrt(jnp.sum(seg * seg, axis=-1, keepdims=True) + EPS) * sc))
        lane = lax.broadcasted_iota(jnp.int32, abt.shape, 1)
        g = -jnp.exp(al) * _softplus(abt + dt)
        gb = jnp.where(lane < DN_HEADS, g, jnp.where(lane < 2 * DN_HEADS, _sigmoid(abt), 0.0))
        return u, jnp.concatenate(qs, axis=1), jnp.concatenate(ks, axis=1), y[:, 2 * DN_WIDTH:], gb

    ins = [(qkvz, "row", (0, DN_QKV)), (qkvz, "prev8", (0, DN_QKV)), (ab, "row"), (convw, "full"),
           (alog_row, "full"), (dt_row, "full")]
    return _rows("dn_prep", fn, ins, [(DN_QKV, F32), (DN_WIDTH, F32), (DN_WIDTH, F32), (DN_WIDTH, F32), (LANE, F32)], tr=tr)


def _tri_masks():
    row = lax.broadcasted_iota(jnp.int32, (CHUNK, CHUNK), 0)
    col = lax.broadcasted_iota(jnp.int32, (CHUNK, CHUNK), 1)
    return row >= col, row > col, row == col


def _heads(fn, *lists):
    return [fn(*xs) for xs in zip(*lists)]


def _split(x):
    hi = x.astype(BF16)
    return hi, (x - hi.astype(F32)).astype(BF16)


def _dot3(a, b, ca=1, cb=0):
    (ah, al), (bh, bl) = a, b
    return _dot(ah, bh, ca, cb) + (_dot(ah, bl, ca, cb) + _dot(al, bh, ca, cb))


def _unit_lower_inverse(a_list, eye):
    ts = [eye - a for a in a_list]
    parts = [_split(a) for a in a_list]
    for _ in range(5):
        parts = [_split(_dot3(p, p)) for p in parts]
        ts = [t + _dot3(_split(t), p) for t, p in zip(ts, parts)]
    return ts


def _dn_terms(qs, ks, vs, gb):
    lower, strict, diag = _tri_masks()
    lane = lax.broadcasted_iota(jnp.int32, (CHUNK, LANE), 1)
    is_last = lax.broadcasted_iota(jnp.int32, (CHUNK, 1), 0) == CHUNK - 1
    hs = range(DN_HEADS)
    gc = _dot(lower.astype(F32), gb, precision=HIGHEST)
    gct = jnp.transpose(gc)
    bcol = [_lane_pick(gb, lane, DN_HEADS + h) for h in hs]
    gcol = [_lane_pick(gc, lane, h) for h in hs]
    glast = [jnp.sum(jnp.where(is_last, g, 0.0), axis=0, keepdims=True) for g in gcol]
    decay = [jnp.exp(jnp.where(lower, gcol[h] - gct[h:h + 1, :], -1e30)) for h in hs]
    kb = _heads(lambda k, b: k * b, ks, bcol)
    kk = _heads(lambda x, k: _bdot(x, k, 1, 1), kb, ks)
    qk = _heads(lambda q, k: _bdot(q, k, 1, 1), qs, ks)
    a = _heads(lambda x, d: jnp.where(strict, x * d, 0.0), kk, decay)
    t = [_split(x) for x in _unit_lower_inverse(a, diag.astype(F32))]
    eg = [jnp.exp(g) for g in gcol]
    egl = _heads(lambda gl, g: jnp.exp(gl - g), glast, gcol)
    rhs_w = _heads(lambda x, e: x * e, kb, eg)
    u = _heads(lambda tt, v, b: _dot3(tt, _split(v * b)), t, vs, bcol)
    w = _heads(lambda tt, r: _dot3(tt, _split(r)), t, rhs_w)
    return dict(bcol=bcol, decay=decay, kb=kb, a=a, t=t, eg=eg, egl=egl, rhs_w=rhs_w, u=u, w=w,
                attn=_heads(lambda x, d: x * d, qk, decay), q_dec=_heads(lambda q, e: q * e, qs, eg),
                k_dec=_heads(lambda k, e: k * e, ks, egl), c_dec=[jnp.exp(g) for g in glast],
                lower=lower, strict=strict, lane=lane, is_last=is_last)


def _head_slices(ref):
    return [ref[:, h * DN_DIM:(h + 1) * DN_DIM] for h in range(DN_HEADS)]


def _dn_chunk_fwd(q, k, v, gb):
    S = q.shape[0]
    N = S // CHUNK

    def body(q_ref, k_ref, v_ref, gb_ref, o_ref, st_ref, state):
        @pl.when(pl.program_id(0) == 0)
        def _():
            state[...] = jnp.zeros_like(state)

        f = _dn_terms(_head_slices(q_ref), _head_slices(k_ref), _head_slices(v_ref), gb_ref[...])
        s = [state[h] for h in range(DN_HEADS)]
        for h in range(DN_HEADS):
            st_ref[0, h] = s[h]
        sb = [x.astype(BF16) for x in s]
        v_new = _heads(lambda u, w, x: u - _bdot(w, x), f["u"], f["w"], sb)
        o = _heads(lambda qd, x, at, vn: _bdot(qd, x) + _bdot(at, vn), f["q_dec"], sb, f["attn"], v_new)
        new_s = _heads(lambda x, c, kd, vn: x * c + _bdot(kd, vn, 0, 0), s, f["c_dec"], f["k_dec"], v_new)
        for h in range(DN_HEADS):
            o_ref[:, h * DN_DIM:(h + 1) * DN_DIM] = o[h]
            state[h] = new_s[h]

    blk = pl.BlockSpec((CHUNK, DN_WIDTH), lambda n: (n, 0))
    st_blk = pl.BlockSpec((1, DN_HEADS, DN_DIM, DN_DIM), lambda n: (n, 0, 0, 0))
    return pl.pallas_call(
        body, name="dn_chunk_fwd", grid=(N,),
        in_specs=[blk, blk, blk, pl.BlockSpec((CHUNK, LANE), lambda n: (n, 0))],
        out_specs=[blk, st_blk],
        out_shape=[jax.ShapeDtypeStruct((S, DN_WIDTH), F32), jax.ShapeDtypeStruct((N, DN_HEADS, DN_DIM, DN_DIM), F32)],
        scratch_shapes=[pltpu.VMEM((DN_HEADS, DN_DIM, DN_DIM), F32)],
        compiler_params=_cparams(("arbitrary",)),
    )(q, k, v, gb)


def _dn_chunk_bwd(q, k, v, gb, states, do):
    S = q.shape[0]
    N = S // CHUNK

    def body(q_ref, k_ref, v_ref, gb_ref, st_ref, do_ref, dq_ref, dk_ref, dv_ref, dgb_ref, dstate):
        @pl.when(pl.program_id(0) == 0)
        def _():
            dstate[...] = jnp.zeros_like(dstate)

        hs = range(DN_HEADS)
        qs, ks, vs, dos = (_head_slices(r) for r in (q_ref, k_ref, v_ref, do_ref))
        f = _dn_terms(qs, ks, vs, gb_ref[...])
        lane, is_last = f["lane"], f["is_last"]
        rowsum = lambda x: jnp.sum(x, axis=-1, keepdims=True)
        s = [st_ref[0, h] for h in hs]
        dsn = [dstate[h] for h in hs]
        sb = [x.astype(BF16) for x in s]
        dsb = [x.astype(BF16) for x in dsn]
        dob = [x.astype(BF16) for x in dos]
        v_new = _heads(lambda u, w, x: u - _bdot(w, x), f["u"], f["w"], sb)
        dv_new = _heads(lambda at, d, kd, x: _bdot(at, d, 0, 0) + _bdot(kd, x), f["attn"], dob, f["k_dec"], dsb)
        dattn = _heads(lambda d, vn: _bdot(d, vn, 1, 1), dob, v_new)
        dq_dec = _heads(lambda d, x: _bdot(d, x, 1, 1), dob, sb)
        dk_dec = _heads(lambda vn, x: _bdot(vn, x, 1, 1), v_new, dsb)
        dw = _heads(lambda dv_, x: -_bdot(dv_, x, 1, 1), dv_new, sb)
        new_ds = _heads(lambda x, c, qd, d, w, dv_: x * c + _bdot(qd, d, 0, 0) - _bdot(w, dv_, 0, 0),
                        dsn, f["c_dec"], f["q_dec"], dob, f["w"], dv_new)
        for h in hs:
            dstate[h] = new_ds[h]
        drhs_u = _heads(lambda tt, x: _dot3(tt, _split(x), 0, 0), f["t"], dv_new)
        drhs_w = _heads(lambda tt, x: _dot3(tt, _split(x), 0, 0), f["t"], dw)
        da = _heads(lambda du_, u, dw_, w: jnp.where(f["strict"], -(_bdot(du_, u, 1, 1) + _bdot(dw_, w, 1, 1)), 0.0),
                    drhs_u, f["u"], drhs_w, f["w"])
        dkk = _heads(lambda x, d: x * d, da, f["decay"])
        dqk = _heads(lambda x, d: x * d, dattn, f["decay"])
        dkb = _heads(lambda x, k_, dw_, e: _bdot(x, k_) + dw_ * e, dkk, ks, drhs_w, f["eg"])
        dq = _heads(lambda x, k_, dqd, e: _bdot(x, k_) + dqd * e, dqk, ks, dq_dec, f["eg"])
        dk = _heads(lambda x, kb_, y, q_, dkd, el, dkb_, b: _bdot(x, kb_, 0, 0) + _bdot(y, q_, 0, 0) + dkd * el + dkb_ * b,
                    dkk, f["kb"], dqk, qs, dk_dec, f["egl"], dkb, f["bcol"])
        m = _heads(lambda x, a_, y, at: x * a_ + y * at, da, f["a"], dattn, f["attn"])
        ones = jnp.ones((CHUNK, LANE), BF16)
        col_m = [(_dot(mh, ones, 0, 0) + _dot(ml, ones, 0, 0))[:, 0:1] for mh, ml in map(_split, m)]
        dgc_all = jnp.zeros((CHUNK, LANE), F32)
        dbeta_all = jnp.zeros((CHUNK, LANE), F32)
        for h in hs:
            dq_ref[:, h * DN_DIM:(h + 1) * DN_DIM] = dq[h]
            dk_ref[:, h * DN_DIM:(h + 1) * DN_DIM] = dk[h]
            dv_ref[:, h * DN_DIM:(h + 1) * DN_DIM] = drhs_u[h] * f["bcol"][h]
            kdec_term = rowsum(dk_dec[h] * f["k_dec"][h])
            dc_dec = _sum_all(dsn[h] * s[h])
            dgc = (rowsum(m[h]) - col_m[h] + rowsum(dq_dec[h] * f["q_dec"][h]) - kdec_term
                   + rowsum(drhs_w[h] * f["rhs_w"][h]))
            last_extra = jnp.sum(kdec_term, axis=0, keepdims=True) + dc_dec * f["c_dec"][h]
            dgc = dgc + jnp.where(is_last, last_extra, 0.0)
            dbeta = rowsum(drhs_u[h] * vs[h]) + rowsum(dkb[h] * ks[h])
            dgc_all = jnp.where(lane == h, dgc, dgc_all)
            dbeta_all = jnp.where(lane == DN_HEADS + h, dbeta, dbeta_all)
        dg_all = _dot(f["lower"].astype(F32), dgc_all, 0, 0, precision=HIGHEST)
        dgb_ref[...] = jnp.where(lane < DN_HEADS, dg_all, dbeta_all)

    rev = lambda n: (N - 1 - n, 0)
    blk = pl.BlockSpec((CHUNK, DN_WIDTH), rev)
    gblk = pl.BlockSpec((CHUNK, LANE), rev)
    st_blk = pl.BlockSpec((1, DN_HEADS, DN_DIM, DN_DIM), lambda n: (N - 1 - n, 0, 0, 0))
    return pl.pallas_call(
        body, name="dn_chunk_bwd", grid=(N,),
        in_specs=[blk, blk, blk, gblk, st_blk, blk],
        out_specs=[blk, blk, blk, gblk],
        out_shape=[jax.ShapeDtypeStruct((S, DN_WIDTH), F32)] * 3 + [jax.ShapeDtypeStruct((S, LANE), F32)],
        scratch_shapes=[pltpu.VMEM((DN_HEADS, DN_DIM, DN_DIM), F32)],
        compiler_params=_cparams(("arbitrary",)),
    )(q, k, v, gb, states, do)


def _dn_post(o, qkvz, gain_row):
    def fn(i, n, ot, z, g):
        cols = []
        for h in range(DN_HEADS):
            seg = ot[:, h * DN_DIM:(h + 1) * DN_DIM]
            cols.append(seg * lax.rsqrt(jnp.mean(seg * seg, axis=-1, keepdims=True) + EPS) * g)
        return (jnp.concatenate(cols, axis=1) * (z * _sigmoid(z)),)
    return _rows("dn_post", fn, [(o, "row"), (qkvz, "row", (3, DN_WIDTH)), (gain_row, "full")], [(DN_WIDTH, BF16)], tr=512)


def _dn_post_bwd(don, o, qkvz, gain_row):
    def fn(i, n, dy, ot, z, g):
        sg = _sigmoid(z)
        sz = z * sg
        dos, ohs = [], []
        dg = jnp.zeros((1, DN_DIM), F32)
        for h in range(DN_HEADS):
            sl = slice(h * DN_DIM, (h + 1) * DN_DIM)
            seg = ot[:, sl]
            r = lax.rsqrt(jnp.mean(seg * seg, axis=-1, keepdims=True) + EPS)
            oh = seg * r
            dno = dy[:, sl] * sz[:, sl]
            dg = dg + _colsum(dno * oh)
            dn = dno * g
            dos.append(r * (dn - oh * jnp.mean(dn * oh, axis=-1, keepdims=True)))
            ohs.append(oh * g)
        dz = dy * jnp.concatenate(ohs, axis=1) * (sg * (1.0 + z * (1.0 - sg)))
        return jnp.concatenate(dos, axis=1), dz, dg
    ins = [(don, "row"), (o, "row"), (qkvz, "row", (3, DN_WIDTH)), (gain_row, "full")]
    return _rows("dn_post_bwd", fn, ins, [(DN_WIDTH, F32), (DN_WIDTH, F32)], tr=256, accs=[((1, DN_DIM), F32)])


def _dn_prep_bwd(dq, dk, dv, dgb, u, ab, alog_row, dt_row):
    def fn(i, n, dqt, dkt, dvt, dgbt, ut, abt, al, dt):
        sg = _sigmoid(ut)
        y = ut * sg
        dys = []
        for grad, base, sc in ((dqt, 0, DN_DIM ** -0.5), (dkt, DN_WIDTH, 1.0)):
            for h in range(DN_HEADS):
                seg = y[:, base + h * DN_DIM:base + (h + 1) * DN_DIM]
                gr = grad[:, h * DN_DIM:(h + 1) * DN_DIM]
                r = lax.rsqrt(jnp.sum(seg * seg, axis=-1, keepdims=True) + EPS)
                xh = seg * r
                dys.append((r * sc) * (gr - xh * jnp.sum(gr * xh, axis=-1, keepdims=True)))
        dy = jnp.concatenate(dys + [dvt], axis=1)
        du = dy * (sg * (1.0 + ut * (1.0 - sg)))
        lane = lax.broadcasted_iota(jnp.int32, abt.shape, 1)
        is_g = lane < DN_HEADS
        ea = jnp.exp(al)
        x = abt + dt
        slope = -ea * _sigmoid(x)
        gval = -ea * _softplus(x)
        dg = jnp.where(is_g, dgbt, 0.0)
        beta = _sigmoid(abt)
        dab = jnp.where(is_g, dg * slope, jnp.where(lane < 2 * DN_HEADS, dgbt * beta * (1.0 - beta), 0.0))
        return du, dab, _colsum(dg * gval), _colsum(dg * slope)
    ins = [(dq, "row"), (dk, "row"), (dv, "row"), (dgb, "row"), (u, "row"), (ab, "row"), (alog_row, "full"), (dt_row, "full")]
    return _rows("dn_prep_bwd", fn, ins, [(DN_QKV, F32), (LANE, BF16)], tr=256, accs=[((1, LANE), F32)] * 2)


def _dn_conv_bwd(du, dz, qkvz, convw):
    tr = 256

    def fn(i, n, dut, dun, dzt, x, xp, w):
        dun = jnp.where(i < n - 1, dun, 0.0)
        dus = jnp.concatenate([dut, dun], axis=0)
        xs = jnp.concatenate([jnp.where(i > 0, xp, 0.0), x], axis=0)
        dx = None
        dws = []
        for j in range(CONV_W):
            sh = CONV_W - 1 - j
            term = (pltpu.roll(dus, tr + SUBLANE - sh, 0) if sh else dus)[:tr] * w[j:j + 1, :]
            dx = term if dx is None else dx + term
            dws.append(_colsum(dut * (pltpu.roll(xs, sh, 0) if sh else xs)[SUBLANE:]))
        return (jnp.concatenate([dx.astype(BF16), dzt.astype(BF16)], axis=1), *dws)

    ins = [(du, "row"), (du, "next8"), (dz, "row"), (qkvz, "row", (0, DN_QKV)), (qkvz, "prev8", (0, DN_QKV)), (convw, "full")]
    res = _rows("dn_conv_bwd", fn, ins, [(DN_QKVZ, BF16)], tr=tr, accs=[((1, DN_QKV), F32)] * CONV_W)
    return res[0], res[1:]


def _add(acc, r):
    return (r + acc,)


def _mlp_ple_fwd(i, x1, p_i, mlp_gain, ple_gain, w_up, w_down, w_ple, w_gate):
    hm = _rmsnorm_fwd(f"mlp_norm{i}", x1, mlp_gain)
    u, a = _mm(f"mlp_up{i}", hm, w_up, epilogue=lambda acc: (acc, jnp.square(jnp.maximum(acc, 0.0))),
               out_dtypes=(F32, BF16))
    x2 = _mm(f"mlp_down{i}", a, w_down, epilogue=_add, extras=(x1,))
    hp = _rmsnorm_fwd(f"ple_norm{i}", x2, ple_gain)
    pp = _mm(f"ple_proj{i}", p_i, w_ple)

    def gate_epilogue(acc, x2t, ppt):
        gate = _sigmoid(acc)
        return x2t + ppt * gate, gate

    x3, gate = _mm(f"ple_gate{i}", hp, w_gate, epilogue=gate_epilogue, extras=(x2, pp), out_dtypes=(F32, F32))
    return x3, dict(x1=x1, hm=hm, u=u, a=a, x2=x2, hp=hp, pp=pp, gate=gate, p=p_i)


def _mlp_ple_bwd(i, dx3, sv, mlp_gain, ple_gain, w_up, w_down, w_gate):
    def fn(_i, _n, d, g, pp):
        return d * g, d * pp * g * (1.0 - g)
    dpp, dzg = _rows(f"ple_gate_bwd{i}", fn, [(dx3, "row"), (sv["gate"], "row"), (sv["pp"], "row")],
                     [(D_MODEL, BF16), (D_MODEL, BF16)], tr=512)
    d_w_ple = _mm(f"ple_proj_dw{i}", sv["p"], dpp, ta=True, out_dtypes=(BF16,))
    d_w_gate = _mm(f"ple_gate_dw{i}", sv["hp"], dzg, ta=True, out_dtypes=(BF16,))
    dhp = _mm(f"ple_gate_dx{i}", dzg, w_gate, tb=True)
    dx2, d_ple_gain = _rmsnorm_bwd(f"ple_norm_bwd{i}", sv["x2"], ple_gain, dhp, dx3)
    d_w_down = _mm(f"mlp_down_dw{i}", sv["a"], dx2, ta=True, out_dtypes=(BF16,))
    du = _mm(f"mlp_down_dx{i}", dx2, w_down, tb=True, epilogue=lambda acc, ut: (acc * (2.0 * jnp.maximum(ut, 0.0)),),
             extras=(sv["u"],), out_dtypes=(BF16,))
    d_w_up = _mm(f"mlp_up_dw{i}", sv["hm"], du, ta=True, out_dtypes=(BF16,))
    dhm = _mm(f"mlp_up_dx{i}", du, w_up, tb=True)
    dx1, d_mlp_gain = _rmsnorm_bwd(f"mlp_norm_bwd{i}", sv["x1"], mlp_gain, dhm, dx2)
    return dx1, dict(w_ple=d_w_ple, w_ple_gate=d_w_gate, w_down=d_w_down, w_up=d_w_up,
                     ple_norm=d_ple_gain, mlp_norm=d_mlp_gain)


def _loss_fwd_bwd(y, target):
    D = y.shape[1]

    def fn(i, n, yt, tt):
        e = yt - tt
        return e * (1.0 / D), _colsum(e * e)
    dy, sq = _rows("loss", fn, [(y, "row"), (target, "row")], [(D, F32)], tr=512, accs=[((1, D), F32)])
    return sq, dy


def _local_step(x, p, positions, target, W, P, rest_of_weights, send_layer1, send_mlp0):
    consts = _head_consts()
    bd = _block_diag(1.0 / A_HEAD_DIM)
    bd1 = _block_diag(1.0)
    ct, st = _rope_tables(positions, consts)
    gains = jnp.stack([jnp.tile(v, A_HEADS) for g in range(3) for v in (P["attn_q_gain"][g], P["attn_k_gain"][g])])
    pad = LANE - DN_HEADS
    alog_row = jnp.pad(P["dn_a_log"].reshape(1, DN_HEADS), ((0, 0), (0, pad)))
    dt_row = jnp.pad(P["dn_dt_bias"].reshape(1, DN_HEADS), ((0, 0), (0, pad)))
    ogain_row = P["dn_o_gain"].reshape(1, DN_DIM)
    row = lambda name, i: P[name][i:i + 1]

    h0 = _rmsnorm_fwd("mix_norm0", x, row("mix_norm", 0))
    qkv = _mm("attn_qkv", h0, W["attn_w_qkv"])
    qkvn = _attn_prep(qkv, gains, ct, st, consts, bd)
    os_, lses = zip(*[_attn_fwd(qkvn[g], g) for g in range(3)])
    o_attn = _attn_merge(os_, lses)
    x1 = _mm("attn_out", o_attn, W["attn_w_o"], epilogue=_add, extras=(x,))
    W = {**W, **rest_of_weights(x1)}
    x3, sv0 = _mlp_ple_fwd(0, x1, p[0], row("mlp_norm", 0), row("ple_norm", 0),
                           W["w_up"][0], W["w_down"][0], W["w_ple"][0], W["w_ple_gate"][0])
    h1 = _rmsnorm_fwd("mix_norm1", x3, row("mix_norm", 1))
    qkvz = _mm("dn_in_qkvz", h1, W["dn_w_qkvz"])
    ab = _mm("dn_in_ab", h1, W["dn_w_ab"])
    u, q, k, v, gb = _dn_prep(qkvz, ab, W["dn_conv"], alog_row, dt_row)
    o_dn, states = _dn_chunk_fwd(q, k, v, gb)
    on = _dn_post(o_dn, qkvz, ogain_row)
    x4 = _mm("dn_out", on, W["dn_w_o"], epilogue=_add, extras=(x3,))
    x6, sv1 = _mlp_ple_fwd(1, x4, p[1], row("mlp_norm", 1), row("ple_norm", 1),
                           W["w_up"][1], W["w_down"][1], W["w_ple"][1], W["w_ple_gate"][1])
    sq, dy = _loss_fwd_bwd(x6, target)

    dx4, g1 = _mlp_ple_bwd(1, dy, sv1, row("mlp_norm", 1), row("ple_norm", 1),
                           W["w_up"][1], W["w_down"][1], W["w_ple_gate"][1])
    don = _mm("dn_out_dx", dx4, W["dn_w_o"], tb=True)
    d_dn_w_o = _mm("dn_out_dw", on, dx4, ta=True, out_dtypes=(BF16,))
    do_dn, dz, d_ogain = _dn_post_bwd(don, o_dn, qkvz, ogain_row)
    dq, dk, dv, dgb = _dn_chunk_bwd(q, k, v, gb, states, do_dn)
    du, dab, d_alog, d_dt = _dn_prep_bwd(dq, dk, dv, dgb, u, ab, alog_row, dt_row)
    dqkvz, d_conv = _dn_conv_bwd(du, dz, qkvz, W["dn_conv"])
    dh1 = _mm("dn_in_qkvz_dx", dqkvz, W["dn_w_qkvz"], tb=True)
    dh1 = _mm("dn_in_ab_dx", dab, W["dn_w_ab"], tb=True, epilogue=_add, extras=(dh1,))
    d_w_qkvz = _mm("dn_in_qkvz_dw", h1, dqkvz, ta=True, out_dtypes=(BF16,))
    d_w_ab = _mm("dn_in_ab_dw", h1, dab, ta=True, out_dtypes=(BF16,))
    dx3, d_mix1 = _rmsnorm_bwd("mix_norm_bwd1", x3, row("mix_norm", 1), dh1, dx4)
    tie = send_layer1(dict(dn_w_qkvz=d_w_qkvz, dn_w_ab=d_w_ab, dn_conv=jnp.concatenate(d_conv, 0), dn_w_o=d_dn_w_o,
                           w_up=g1["w_up"], w_down=g1["w_down"], w_ple=g1["w_ple"], w_ple_gate=g1["w_ple_gate"]))
    dx1, g0 = _mlp_ple_bwd(0, dx3, sv0, row("mlp_norm", 0), row("ple_norm", 0) + tie,
                           W["w_up"][0], W["w_down"][0], W["w_ple_gate"][0])
    tie = send_mlp0(dict(w_up=g0["w_up"], w_down=g0["w_down"], w_ple=g0["w_ple"], w_ple_gate=g0["w_ple_gate"]))
    do_attn = _mm("attn_out_dx", dx1, W["attn_w_o"], tb=True)
    d_attn_w_o = _mm("attn_out_dw", o_attn, dx1, ta=True, out_dtypes=(BF16,))
    dos, cs = _attn_merge_bwd(do_attn, os_, lses, bd1 + tie.astype(BF16))
    grads9 = []
    for g in range(3):
        grads9 += list(_attn_bwd(qkvn[g], g, dos[g], lses[g], cs[g]))
    dqkv, dgains = _attn_prep_bwd(qkv, grads9, gains, ct, st, consts, bd)
    dh0 = _mm("attn_qkv_dx", dqkv, W["attn_w_qkv"], tb=True)
    d_attn_w_qkv = _mm("attn_qkv_dw", h0, dqkv, ta=True, out_dtypes=(BF16,))
    dx0, d_mix0 = _rmsnorm_bwd("mix_norm_bwd0", x, row("mix_norm", 0), dh0, dx1)

    dg = jnp.stack([t.reshape(A_HEADS, A_HEAD_DIM).sum(0) for t in dgains])
    small = dict(
        mix_norm=jnp.concatenate([d_mix0, d_mix1], 0),
        attn_q_gain=dg[0::2][None], attn_k_gain=dg[1::2][None],
        dn_a_log=d_alog[:, :DN_HEADS], dn_dt_bias=d_dt[:, :DN_HEADS], dn_o_gain=d_ogain,
        mlp_norm=jnp.concatenate([g0["mlp_norm"], g1["mlp_norm"]], 0),
        ple_norm=jnp.concatenate([g0["ple_norm"], g1["ple_norm"]], 0),
    )
    return sq, dx0, small, dict(attn_w_qkv=d_attn_w_qkv, attn_w_o=d_attn_w_o)


MESH_IDS = pl.DeviceIdType.MESH
ANY = pl.BlockSpec(memory_space=pl.ANY)


def _place():
    return lax.axis_index("x"), lax.axis_index("y"), lax.axis_index("c")


def _sem_scratch(n_streams):
    return [pltpu.SemaphoreType.DMA((n_streams, N_DEV - 1)), pltpu.SemaphoreType.DMA((n_streams, N_DEV - 1)),
            pltpu.SemaphoreType.DMA((n_streams,))]


def _all_gather(name, arrays, streams):
    n_in, n_st = len(arrays), len(streams)
    shapes = [arrays[a].shape if li is None else arrays[a].shape[1:] for a, li in streams]

    def body(*refs):
        in_refs, out_refs = refs[:n_in], refs[n_in:n_in + n_st]
        send_sems, recv_sems, local_sems = refs[n_in + n_st:]
        x, y, c = _place()
        me, sibling = (x, y, c), (x, y, 1 - c)
        chips = [(1 - x, y), (x, 1 - y), (1 - x, 1 - y)]

        def copy(s, k, block, to, own=False):
            a, li = streams[s]
            dst = out_refs[s].at[4 * block[0] + 2 * block[1] + block[2]]
            src = (in_refs[a] if li is None else in_refs[a].at[li]) if own else dst
            return pltpu.make_async_remote_copy(src_ref=src, dst_ref=dst, send_sem=send_sems.at[s, k],
                                                recv_sem=recv_sems.at[s, k], device_id=to, device_id_type=MESH_IDS)

        started = []
        for s, (a, li) in enumerate(streams):
            src = in_refs[a] if li is None else in_refs[a].at[li]
            mine = pltpu.make_async_copy(src, out_refs[s].at[4 * x + 2 * y + c], local_sems.at[s])
            mine.start()
            started.append(mine)
        sends = []
        for s in range(n_st):
            first = [copy(s, 0, me, sibling, own=True)]
            first += [copy(s, 1 + j, me, (*chip, c), own=True) for j, chip in enumerate(chips)]
            for cp in first:
                cp.start()
            sends += first
        for j, chip in enumerate(chips):
            for s in range(n_st):
                copy(s, 1 + j, (*chip, c), me).wait_recv()
                fwd = copy(s, 4 + j, (*chip, c), sibling)
                fwd.start()
                sends.append(fwd)
        for s in range(n_st):
            copy(s, 0, sibling, me).wait_recv()
            for j, chip in enumerate(chips):
                copy(s, 4 + j, (*chip, 1 - c), me).wait_recv()
        for cp in sends:
            cp.wait_send()
        for cp in started:
            cp.wait()

    return pl.pallas_call(
        body, name=name,
        out_shape=[jax.ShapeDtypeStruct((N_DEV,) + tuple(sh), arrays[a].dtype) for sh, (a, _) in zip(shapes, streams)],
        in_specs=[ANY] * n_in, out_specs=[ANY] * n_st, scratch_shapes=_sem_scratch(n_st),
    )(*arrays)


def _exchange(name, sends, recv_shapes, placement):
    n_st, n_out = len(sends), len(recv_shapes)

    def body(*refs):
        send_refs, recv_refs = refs[:n_st], refs[n_st:n_st + n_out]
        send_sems, recv_sems, local_sems = refs[n_st + n_out:]
        x, y, c = _place()
        me = 4 * x + 2 * y + c

        def landing(s, slot):
            r, off = placement[s]
            rows = sends[s].shape[1]
            if rows == recv_shapes[r][1]:
                return recv_refs[r].at[slot]
            return recv_refs[r].at[slot, pl.ds(off, rows)]

        local, copies, arrivals = [], [], []
        for s in range(n_st):
            cp = pltpu.make_async_copy(send_refs[s].at[me], landing(s, me), local_sems.at[s])
            cp.start()
            local.append(cp)
        for k in range(1, N_DEV):
            px = 1 - x if k & 4 else x
            py = 1 - y if k & 2 else y
            pc = 1 - c if k & 1 else c
            peer = 4 * px + 2 * py + pc
            for s in range(n_st):
                copies.append(pltpu.make_async_remote_copy(
                    src_ref=send_refs[s].at[peer], dst_ref=landing(s, me), send_sem=send_sems.at[s, k - 1],
                    recv_sem=recv_sems.at[s, k - 1], device_id=(px, py, pc), device_id_type=MESH_IDS))
                arrivals.append(pltpu.make_async_remote_copy(
                    src_ref=send_refs[s].at[peer], dst_ref=landing(s, peer), send_sem=send_sems.at[s, k - 1],
                    recv_sem=recv_sems.at[s, k - 1], device_id=(px, py, pc), device_id_type=MESH_IDS))
        for cp in copies:
            cp.start()
        for cp in arrivals:
            cp.wait_recv()
        for cp in copies:
            cp.wait_send()
        for cp in local:
            cp.wait()

    dtypes = {r: sends[s].dtype for s, (r, _) in enumerate(placement)}
    return pl.pallas_call(
        body, name=name, out_shape=[jax.ShapeDtypeStruct(sh, dtypes[r]) for r, sh in enumerate(recv_shapes)],
        in_specs=[ANY] * n_st, out_specs=[ANY] * n_out, scratch_shapes=_sem_scratch(n_st),
    )(*sends)


HBM = pl.BlockSpec(memory_space=pltpu.HBM)
SEM = pl.BlockSpec(memory_space=pltpu.SEMAPHORE)
FLOWS = pltpu.CompilerParams(has_side_effects=pltpu.SideEffectType.DATAFLOW_SIDE_EFFECTING)
GATHER_PEERS = 4


def _in_hbm(a):
    return pltpu.with_memory_space_constraint(a, pltpu.HBM)


def _hbm_like(a):
    return pltpu.HBM(a.shape, a.dtype)


def _gather_peers(x, y, c):
    return [(x, y, 1 - c), (1 - x, y, c), (x, 1 - y, c), (1 - x, 1 - y, c)]


def _gather_start(name, arrays, streams):
    n_in, n_st = len(arrays), len(streams)
    lands = [lax.empty((N_DEV,) + tuple(arrays[a].shape if li is None else arrays[a].shape[1:]), arrays[a].dtype)
             for a, li in streams]

    def body(*refs):
        arr, land = refs[:n_in], refs[n_in:n_in + n_st]
        send_sems, recv_sems = refs[n_in + n_st:n_in + n_st + 2]
        token = refs[-1]
        x, y, c = _place()
        me = 4 * x + 2 * y + c
        for s, (a, li) in enumerate(streams):
            src = arr[a] if li is None else arr[a].at[li]
            for k, peer in enumerate(_gather_peers(x, y, c)):
                pltpu.make_async_remote_copy(
                    src_ref=src, dst_ref=land[s].at[me], send_sem=send_sems.at[s * GATHER_PEERS + k],
                    recv_sem=recv_sems.at[s * GATHER_PEERS + k], device_id=peer, device_id_type=MESH_IDS).start()
        token[...] = jnp.zeros_like(token)

    n_sem = n_st * GATHER_PEERS
    res = pl.pallas_call(
        body, name=name,
        out_shape=(pltpu.SemaphoreType.DMA((n_sem,)), pltpu.SemaphoreType.DMA((n_sem,)),
                   *[_hbm_like(a) for a in arrays], *[_hbm_like(t) for t in lands], jax.ShapeDtypeStruct((SUBLANE, LANE), F32)),
        in_specs=[HBM] * (n_in + n_st),
        out_specs=(SEM, SEM, *[HBM] * (n_in + n_st), pl.BlockSpec(memory_space=pltpu.VMEM)),
        input_output_aliases={i: 2 + i for i in range(n_in + n_st)}, compiler_params=FLOWS,
    )(*[_in_hbm(a) for a in arrays], *[_in_hbm(t) for t in lands])
    return res[0], res[1], list(res[2:2 + n_in]), list(res[2 + n_in:2 + n_in + n_st]), res[-1]


def _gather_wait(name, send_sems, recv_sems, arrays, lands, streams, after):
    n_in, n_st = len(arrays), len(streams)

    def body(*refs):
        arr, land = refs[:n_in], refs[n_in:n_in + n_st]
        s_sems, r_sems = refs[n_in + n_st:n_in + n_st + 2]
        x, y, c = _place()
        for s, (a, li) in enumerate(streams):
            src = arr[a] if li is None else arr[a].at[li]
            for k, (px, py, pc) in enumerate(_gather_peers(x, y, c)):
                cp = pltpu.make_async_remote_copy(
                    src_ref=src, dst_ref=land[s].at[4 * px + 2 * py + pc], send_sem=s_sems.at[s * GATHER_PEERS + k],
                    recv_sem=r_sems.at[s * GATHER_PEERS + k], device_id=(px, py, pc), device_id_type=MESH_IDS)
                cp.wait_send()
                cp.wait_recv()

    res = pl.pallas_call(
        body, name=name, out_shape=tuple(_hbm_like(a) for a in (*arrays, *lands)),
        in_specs=[HBM] * (n_in + n_st) + [SEM, SEM, ANY], out_specs=tuple([HBM] * (n_in + n_st)),
        input_output_aliases={i: i for i in range(n_in + n_st)}, compiler_params=FLOWS,
    )(*arrays, *lands, send_sems, recv_sems, after)
    return list(res[:n_in]), list(res[n_in:])


def _gather_finish(name, arrays, lands, streams):
    n_in, n_st = len(arrays), len(streams)

    def body(*refs):
        arr, land = refs[:n_in], refs[n_in:n_in + n_st]
        send_sems, recv_sems, local_sems = refs[2 * (n_in + n_st) - n_in:]
        x, y, c = _place()
        sibling = (x, y, 1 - c)
        chips = [(1 - x, y), (x, 1 - y), (1 - x, 1 - y)]
        started = []
        for s, (a, li) in enumerate(streams):
            src = arr[a] if li is None else arr[a].at[li]
            own = pltpu.make_async_copy(src, land[s].at[4 * x + 2 * y + c], local_sems.at[s])
            own.start()
            started.append(own)
        fwd = []
        for s in range(n_st):
            for j, (px, py) in enumerate(chips):
                slot = land[s].at[4 * px + 2 * py + c]
                cp = pltpu.make_async_remote_copy(src_ref=slot, dst_ref=slot, send_sem=send_sems.at[s, j],
                                                  recv_sem=recv_sems.at[s, j], device_id=sibling, device_id_type=MESH_IDS)
                cp.start()
                fwd.append(cp)
        for s in range(n_st):
            for j, (px, py) in enumerate(chips):
                slot = land[s].at[4 * px + 2 * py + 1 - c]
                pltpu.make_async_remote_copy(src_ref=slot, dst_ref=slot, send_sem=send_sems.at[s, j],
                                             recv_sem=recv_sems.at[s, j], device_id=sibling, device_id_type=MESH_IDS).wait_recv()
        for cp in fwd:
            cp.wait_send()
        for cp in started:
            cp.wait()

    res = pl.pallas_call(
        body, name=name, out_shape=[jax.ShapeDtypeStruct(t.shape, t.dtype) for t in lands],
        in_specs=[ANY] * (n_in + n_st), out_specs=[ANY] * n_st,
        input_output_aliases={n_in + s: s for s in range(n_st)},
        scratch_shapes=[pltpu.SemaphoreType.DMA((n_st, 3)), pltpu.SemaphoreType.DMA((n_st, 3)), pltpu.SemaphoreType.DMA((n_st,))],
    )(*arrays, *lands)
    return list(res)


def _exchange_start(name, sends):
    n_st = len(sends)
    recvs = [lax.empty(t.shape, t.dtype) for t in sends]

    def body(*refs):
        snd, rcv = refs[:n_st], refs[n_st:2 * n_st]
        send_sems, recv_sems = refs[2 * n_st:2 * n_st + 2]
        token = refs[-1]
        x, y, c = _place()
        me = 4 * x + 2 * y + c
        for k in range(1, N_DEV):
            px = 1 - x if k & 4 else x
            py = 1 - y if k & 2 else y
            pc = 1 - c if k & 1 else c
            for s in range(n_st):
                pltpu.make_async_remote_copy(
                    src_ref=snd[s].at[4 * px + 2 * py + pc], dst_ref=rcv[s].at[me], send_sem=send_sems.at[s * (N_DEV - 1) + k - 1],
                    recv_sem=recv_sems.at[s * (N_DEV - 1) + k - 1], device_id=(px, py, pc), device_id_type=MESH_IDS).start()
        token[...] = jnp.zeros_like(token)

    n_sem = n_st * (N_DEV - 1)
    res = pl.pallas_call(
        body, name=name,
        out_shape=(pltpu.SemaphoreType.DMA((n_sem,)), pltpu.SemaphoreType.DMA((n_sem,)),
                   *[_hbm_like(t) for t in sends], *[_hbm_like(t) for t in recvs], jax.ShapeDtypeStruct((SUBLANE, LANE), F32)),
        in_specs=[HBM] * (2 * n_st), out_specs=(SEM, SEM, *[HBM] * (2 * n_st), pl.BlockSpec(memory_space=pltpu.VMEM)),
        input_output_aliases={i: 2 + i for i in range(2 * n_st)}, compiler_params=FLOWS,
    )(*[_in_hbm(t) for t in sends], *[_in_hbm(t) for t in recvs])
    return res[0], res[1], list(res[2:2 + n_st]), list(res[2 + n_st:2 + 2 * n_st]), res[-1]


def _exchange_wait(name, send_sems, recv_sems, sends, recvs, after):
    n_st = len(sends)

    def body(*refs):
        snd, rcv = refs[:n_st], refs[n_st:2 * n_st]
        s_sems, r_sems = refs[2 * n_st:2 * n_st + 2]
        x, y, c = _place()
        for k in range(1, N_DEV):
            px = 1 - x if k & 4 else x
            py = 1 - y if k & 2 else y
            pc = 1 - c if k & 1 else c
            peer = 4 * px + 2 * py + pc
            for s in range(n_st):
                cp = pltpu.make_async_remote_copy(
                    src_ref=snd[s].at[peer], dst_ref=rcv[s].at[peer], send_sem=s_sems.at[s * (N_DEV - 1) + k - 1],
                    recv_sem=r_sems.at[s * (N_DEV - 1) + k - 1], device_id=(px, py, pc), device_id_type=MESH_IDS)
                cp.wait_send()
                cp.wait_recv()

    res = pl.pallas_call(
        body, name=name, out_shape=tuple(_hbm_like(t) for t in (*sends, *recvs)),
        in_specs=[HBM] * (2 * n_st) + [SEM, SEM, ANY], out_specs=tuple([HBM] * (2 * n_st)),
        input_output_aliases={i: i for i in range(2 * n_st)}, compiler_params=FLOWS,
    )(*sends, *recvs, send_sems, recv_sems, after)
    return list(res[:n_st]), list(res[n_st:])


def _place_own(name, sends, recvs):
    n_st = len(sends)

    def body(*refs):
        snd, rcv = refs[:n_st], refs[n_st:2 * n_st]
        sems = refs[-1]
        x, y, c = _place()
        me = 4 * x + 2 * y + c
        cps = [pltpu.make_async_copy(snd[s].at[me], rcv[s].at[me], sems.at[s]) for s in range(n_st)]
        for cp in cps:
            cp.start()
        for cp in cps:
            cp.wait()

    res = pl.pallas_call(
        body, name=name, out_shape=[jax.ShapeDtypeStruct(t.shape, t.dtype) for t in recvs],
        in_specs=[ANY] * (2 * n_st), out_specs=[ANY] * n_st, input_output_aliases={n_st + s: s for s in range(n_st)},
        scratch_shapes=[pltpu.SemaphoreType.DMA((n_st,))],
    )(*sends, *recvs)
    return list(res)


def _dn_in_pieces():
    n = (DN_QKVZ + 2 * DN_HEADS) // N_DEV
    segs = ((0, DN_QKV, 0, 0), (DN_QKV, DN_QKV + 2 * DN_HEADS, 1, 0), (DN_QKV + 2 * DN_HEADS, DN_QKVZ + 2 * DN_HEADS, 0, DN_QKV))
    out = []
    for d in range(N_DEV):
        lo, hi = d * n, (d + 1) * n
        for s0, s1, tgt, t0 in segs:
            a, b = max(lo, s0), min(hi, s1)
            if a < b:
                out.append((d, a - lo, b - lo, tgt, t0 + a - s0))
    return out


def _unpack_cols(name, g):
    _, K, n = g.shape
    tr = 256

    def body(g_ref, o_ref):
        for d in range(N_DEV):
            o_ref[:, d * n:(d + 1) * n] = g_ref[d]

    return pl.pallas_call(
        body, name=name, grid=(K // tr,), in_specs=[pl.BlockSpec((N_DEV, tr, n), lambda i: (0, i, 0))],
        out_specs=pl.BlockSpec((tr, N_DEV * n), lambda i: (i, 0)),
        out_shape=jax.ShapeDtypeStruct((K, N_DEV * n), g.dtype), compiler_params=_cparams(("parallel",)),
    )(g)


def _pack_cols(name, w):
    K, n = w.shape[0], w.shape[1] // N_DEV
    tr = 256

    def body(w_ref, o_ref):
        for d in range(N_DEV):
            o_ref[d] = w_ref[:, d * n:(d + 1) * n]

    return pl.pallas_call(
        body, name=name, grid=(K // tr,), in_specs=[pl.BlockSpec((tr, N_DEV * n), lambda i: (i, 0))],
        out_specs=pl.BlockSpec((N_DEV, tr, n), lambda i: (0, i, 0)),
        out_shape=jax.ShapeDtypeStruct((N_DEV, K, n), w.dtype), compiler_params=_cparams(("parallel",)),
    )(w)


def _unpack_dn_in(g):
    _, K, n = g.shape
    tr = 256

    def body(g_ref, qkvz_ref, ab_ref):
        ab_ref[...] = jnp.zeros_like(ab_ref)
        for d, c0, c1, tgt, t0 in _dn_in_pieces():
            (qkvz_ref, ab_ref)[tgt][:, t0:t0 + c1 - c0] = g_ref[d, :, c0:c1]

    return pl.pallas_call(
        body, name="unpack_dn_in", grid=(K // tr,), in_specs=[pl.BlockSpec((N_DEV, tr, n), lambda i: (0, i, 0))],
        out_specs=[pl.BlockSpec((tr, DN_QKVZ), lambda i: (i, 0)), pl.BlockSpec((tr, LANE), lambda i: (i, 0))],
        out_shape=[jax.ShapeDtypeStruct((K, DN_QKVZ), g.dtype), jax.ShapeDtypeStruct((K, LANE), g.dtype)],
        compiler_params=_cparams(("parallel",)),
    )(g)


def _pack_dn_in(d_qkvz, d_ab):
    K = d_qkvz.shape[0]
    n = (DN_QKVZ + 2 * DN_HEADS) // N_DEV
    tr = 256

    def body(qkvz_ref, ab_ref, o_ref):
        for d, c0, c1, tgt, t0 in _dn_in_pieces():
            o_ref[d, :, c0:c1] = (qkvz_ref, ab_ref)[tgt][:, t0:t0 + c1 - c0]

    return pl.pallas_call(
        body, name="pack_dn_in", grid=(K // tr,),
        in_specs=[pl.BlockSpec((tr, DN_QKVZ), lambda i: (i, 0)), pl.BlockSpec((tr, LANE), lambda i: (i, 0))],
        out_specs=pl.BlockSpec((N_DEV, tr, n), lambda i: (0, i, 0)),
        out_shape=jax.ShapeDtypeStruct((N_DEV, K, n), d_qkvz.dtype), compiler_params=_cparams(("parallel",)),
    )(d_qkvz, d_ab)


ADAMW_ROWS = 256


def _adamw(name, parts, w, m, v):
    R, C = w.shape
    tr = min(R, ADAMW_ROWS)
    assert R % tr == 0 and parts.shape == (N_DEV, R, C)
    c1 = 1.0 - B1 ** STEP
    c2 = 1.0 - B2 ** STEP

    def body(p_ref, w_ref, m_ref, v_ref, g_ref, d_ref, nm_ref, nv_ref):
        g = p_ref[0].astype(F32)
        for dev in range(1, N_DEV):
            g = g + p_ref[dev].astype(F32)
        nm = B1 * m_ref[...] + (1.0 - B1) * g
        nv = B2 * v_ref[...] + (1.0 - B2) * jnp.square(g)
        g_ref[...] = g
        nm_ref[...] = nm
        nv_ref[...] = nv
        d_ref[...] = -LR * ((nm / c1) / (jnp.sqrt(nv / c2) + ADAM_EPS) + WD * w_ref[...])

    blk = pl.BlockSpec((tr, C), lambda i: (i, 0))
    return pl.pallas_call(
        body, name=name, grid=(R // tr,),
        in_specs=[pl.BlockSpec((N_DEV, tr, C), lambda i: (0, i, 0)), blk, blk, blk],
        out_specs=[blk] * 4, out_shape=[jax.ShapeDtypeStruct((R, C), F32)] * 4,
        compiler_params=_cparams(("parallel",)),
    )(parts, w, m, v)


SMALL = ("mix_norm", "attn_q_gain", "attn_k_gain", "dn_a_log", "dn_dt_bias", "dn_o_gain", "mlp_norm", "ple_norm")
WEIGHTS = ("mix_norm", "attn_w_qkv", "attn_q_gain", "attn_k_gain", "attn_w_o", "dn_w_in", "dn_conv", "dn_a_log",
           "dn_dt_bias", "dn_o_gain", "dn_w_o", "mlp_norm", "w_up", "w_down", "ple_norm", "w_ple", "w_ple_gate")


def _to_rows(flat, multiple):
    n = flat.shape[-1]
    rows = -(-n // (LANE * multiple)) * multiple
    return jnp.pad(flat, [(0, rows * LANE - n)]).reshape(rows, LANE)


def _cols_to_devices(w):
    K, N = w.shape
    return jnp.transpose(w.reshape(K, N_DEV, N // N_DEV), (1, 0, 2))


def _cols_from_devices(g):
    _, K, n = g.shape
    return jnp.transpose(g, (1, 0, 2)).reshape(K, N_DEV * n)


SMALL_ROWS = 96


def _pack_small(vals, loss_rows):
    rows = [_to_rows(vals[n].reshape(-1), SUBLANE) for n in SMALL] + [loss_rows]
    buf = jnp.concatenate(rows, 0)
    assert buf.shape == (SMALL_ROWS, LANE)
    return buf


def _unpack_small(buf, like):
    out, r = {}, 0
    for n in SMALL:
        sz = math.prod(like[n].shape)
        out[n] = buf[r:r + -(-sz // LANE)].reshape(-1)[:sz].reshape(like[n].shape)
        r += -(-sz // (LANE * SUBLANE)) * SUBLANE
    return out


def kernel(x, p, positions, mix_norm, attn_w_qkv, attn_q_gain, attn_k_gain, attn_w_o, dn_w_in, dn_conv, dn_a_log, dn_dt_bias, dn_o_gain, dn_w_o, mlp_norm, w_up, w_down, ple_norm, w_ple, w_ple_gate, loss_target, m_mix_norm, m_attn_w_qkv, m_attn_q_gain, m_attn_k_gain, m_attn_w_o, m_dn_w_in, m_dn_conv, m_dn_a_log, m_dn_dt_bias, m_dn_o_gain, m_dn_w_o, m_mlp_norm, m_w_up, m_w_down, m_ple_norm, m_w_ple, m_w_ple_gate, v_mix_norm, v_attn_w_qkv, v_attn_q_gain, v_attn_k_gain, v_attn_w_o, v_dn_w_in, v_dn_conv, v_dn_a_log, v_dn_dt_bias, v_dn_o_gain, v_dn_w_o, v_mlp_norm, v_w_up, v_w_down, v_ple_norm, v_w_ple, v_w_ple_gate):
    w = dict(mix_norm=mix_norm, attn_w_qkv=attn_w_qkv, attn_q_gain=attn_q_gain, attn_k_gain=attn_k_gain, attn_w_o=attn_w_o,
             dn_w_in=dn_w_in, dn_conv=dn_conv, dn_a_log=dn_a_log, dn_dt_bias=dn_dt_bias, dn_o_gain=dn_o_gain, dn_w_o=dn_w_o,
             mlp_norm=mlp_norm, w_up=w_up, w_down=w_down, ple_norm=ple_norm, w_ple=w_ple, w_ple_gate=w_ple_gate)
    m = dict(mix_norm=m_mix_norm, attn_w_qkv=m_attn_w_qkv, attn_q_gain=m_attn_q_gain, attn_k_gain=m_attn_k_gain,
             attn_w_o=m_attn_w_o, dn_w_in=m_dn_w_in, dn_conv=m_dn_conv, dn_a_log=m_dn_a_log, dn_dt_bias=m_dn_dt_bias,
             dn_o_gain=m_dn_o_gain, dn_w_o=m_dn_w_o, mlp_norm=m_mlp_norm, w_up=m_w_up, w_down=m_w_down,
             ple_norm=m_ple_norm, w_ple=m_w_ple, w_ple_gate=m_w_ple_gate)
    v = dict(mix_norm=v_mix_norm, attn_w_qkv=v_attn_w_qkv, attn_q_gain=v_attn_q_gain, attn_k_gain=v_attn_k_gain,
             attn_w_o=v_attn_w_o, dn_w_in=v_dn_w_in, dn_conv=v_dn_conv, dn_a_log=v_dn_a_log, dn_dt_bias=v_dn_dt_bias,
             dn_o_gain=v_dn_o_gain, dn_w_o=v_dn_w_o, mlp_norm=v_mlp_norm, w_up=v_w_up, w_down=v_w_down,
             ple_norm=v_ple_norm, w_ple=v_w_ple, w_ple_gate=v_w_ple_gate)
    S = x.shape[1]

    bf = lambda a: a.astype(BF16)
    rows_to_devices = lambda t: t.reshape(N_DEV, t.shape[0] // N_DEV, t.shape[1])

    rest_streams = [(0, None), (1, None), (2, 0), (2, 1), (3, 0), (3, 1), (4, 0), (4, 1), (5, 0), (5, 1), (6, None)]
    g_ssem, g_rsem, rest_shards, lands, token = _gather_start(
        "gather_rest_start", [bf(dn_w_in[0]), bf(dn_w_o[0]), bf(w_up), bf(w_down), bf(w_ple), bf(w_ple_gate), dn_conv[0]],
        rest_streams)
    g_qkv, g_ao = _all_gather("gather_attn", [bf(attn_w_qkv[0]), bf(attn_w_o[0] + token[0:1, 0:1])], [(0, None), (1, None)])
    W = dict(attn_w_qkv=_unpack_cols("unpack_attn_qkv", g_qkv), attn_w_o=_cols_from_devices(g_ao))

    def rest_of_weights(after):
        arrs, lnd = _gather_wait("gather_rest_wait", g_ssem, g_rsem, rest_shards, lands, rest_streams, after)
        g_in, g_do, g_up0, g_up1, g_dn0, g_dn1, g_pl0, g_pl1, g_gt0, g_gt1, g_conv = _gather_finish(
            "gather_rest_finish", arrs, lnd, rest_streams)
        rest = dict(
            dn_conv=jnp.transpose(g_conv, (1, 0, 2)).reshape(CONV_W, DN_QKV), dn_w_o=g_do.reshape(DN_WIDTH, D_MODEL),
            w_up=[_cols_from_devices(g_up0), _cols_from_devices(g_up1)],
            w_down=[g_dn0.reshape(D_FF, D_MODEL), g_dn1.reshape(D_FF, D_MODEL)],
            w_ple=[_cols_from_devices(g_pl0), _cols_from_devices(g_pl1)],
            w_ple_gate=[g_gt0.reshape(D_MODEL, D_MODEL), g_gt1.reshape(D_MODEL, D_MODEL)])
        rest["dn_w_qkvz"], rest["dn_w_ab"] = _unpack_dn_in(g_in)
        return rest

    pending = {}

    def mlp_sends(g):
        return [_cols_to_devices(g["w_up"]), rows_to_devices(g["w_down"]), _cols_to_devices(g["w_ple"]),
                rows_to_devices(g["w_ple_gate"])]

    def start(tag, sends):
        *state, tok = _exchange_start(f"exchange_{tag}_start", sends)
        pending[tag] = state
        return tok[0:1, 0:1]

    def send_layer1(g):
        conv_send = jnp.transpose(g["dn_conv"].reshape(CONV_W, N_DEV, DN_QKV // N_DEV), (1, 0, 2))
        return start("layer1", [_pack_dn_in(g["dn_w_qkvz"], g["dn_w_ab"]), conv_send, rows_to_devices(g["dn_w_o"])] + mlp_sends(g))

    def send_mlp0(g):
        return start("mlp0", mlp_sends(g))

    def arrived(tag, after):
        ssem, rsem, snd, rcv = pending[tag]
        snd, rcv = _exchange_wait(f"exchange_{tag}_wait", ssem, rsem, snd, rcv, after)
        return _place_own(f"exchange_{tag}_own", snd, rcv)

    P = dict(mix_norm=mix_norm, attn_q_gain=attn_q_gain[0], attn_k_gain=attn_k_gain[0], dn_a_log=dn_a_log[0],
             dn_dt_bias=dn_dt_bias[0], dn_o_gain=dn_o_gain[0], mlp_norm=mlp_norm, ple_norm=ple_norm)

    sq, dx0, small_g, attn_g = _local_step(x[0], p[:, 0], positions.reshape(S, 1), loss_target[0], W, P,
                                           rest_of_weights, send_layer1, send_mlp0)

    r_in, r_conv, r_do, r_up1, r_dn1, r_pl1, r_gt1 = arrived("layer1", dx0)
    r_up0, r_dn0, r_pl0, r_gt0 = arrived("mlp0", dx0)
    attn_sends = [_pack_cols("pack_attn_qkv", attn_g["attn_w_qkv"]), _cols_to_devices(attn_g["attn_w_o"])]
    r_qkv, r_ao = _exchange("exchange_attn", attn_sends, [t.shape for t in attn_sends], [(0, 0), (1, 0)])
    big = {}
    for n, parts in (("attn_w_qkv", [r_qkv]), ("attn_w_o", [r_ao]), ("dn_w_in", [r_in]), ("dn_conv", [r_conv]),
                     ("dn_w_o", [r_do]), ("w_up", [r_up0, r_up1]), ("w_down", [r_dn0, r_dn1]),
                     ("w_ple", [r_pl0, r_pl1]), ("w_ple_gate", [r_gt0, r_gt1])):
        layers = [_adamw(f"adamw_{n}{l}", pt, w[n][l], m[n][l], v[n][l]) for l, pt in enumerate(parts)]
        big[n] = [jnp.stack([res[k] for res in layers]) for k in range(4)]

    loss_rows = jnp.pad((0.5 / D_MODEL) * jnp.sum(sq, axis=1, keepdims=True), ((0, SUBLANE - 1), (0, LANE - 1)))
    small_like = {n: w[n] for n in SMALL}
    parts_s = _all_gather("gather_small", [_pack_small(small_g, loss_rows)], [(0, None)])[0]
    zero_rows = jnp.zeros((SUBLANE, LANE), F32)
    small = _adamw("adamw_small", parts_s, _pack_small(w, zero_rows), _pack_small(m, zero_rows), _pack_small(v, zero_rows))
    loss = small[0][SMALL_ROWS - SUBLANE, 0]
    small = [_unpack_small(b, small_like) for b in small]

    outs = [loss, dx0[None]]
    for k in range(4):
        for n in WEIGHTS:
            outs.append(small[k][n] if n in SMALL else big[n][k])
    return tuple(outs)
```

```python
import functools
import math

import jax
import jax.numpy as jnp
from jax import lax
from jax.experimental import pallas as pl
from jax.experimental.pallas import tpu as pltpu

F32 = jnp.float32
BF16 = jnp.bfloat16
HIGHEST = lax.Precision.HIGHEST

N_DEV = 8
D_MODEL = 1024
EPS = 1e-6
SWA_GROUPS = ((128, 1), (512, 4), (2048, 16))
A_HEADS = 8
A_HEAD_DIM = 64
A_WIDTH = A_HEADS * A_HEAD_DIM
A_QKV = 3 * 3 * A_WIDTH
ROPE_DIM = 16
ROPE_HALF = 8
ROPE_THETA = 500000.0
BAND = 128
DN_HEADS = 8
DN_DIM = 128
DN_WIDTH = DN_HEADS * DN_DIM
CONV_W = 4
CHUNK = 64
D_FF = 4 * D_MODEL
PLE_DIM = 256
LR, B1, B2, ADAM_EPS, WD, STEP = 0.001, 0.9, 0.999, 1e-08, 0.01, 10

VMEM_LIMIT = 56 * 1024 * 1024
MXU_TILE = 1024
LANE = 128
SUBLANE = 8


def _cparams(sem):
    return pltpu.CompilerParams(dimension_semantics=sem, vmem_limit_bytes=VMEM_LIMIT)


def _tile(n, pref):
    if n <= pref:
        return n
    t = (pref // LANE) * LANE
    while t >= LANE:
        if n % t == 0:
            return t
        t -= LANE
    raise ValueError(f"no tile for {n}")


def _dot(a, b, ca=1, cb=0, precision=None):
    return lax.dot_general(a, b, (((ca,), (cb,)), ((), ())), precision=precision,
                           preferred_element_type=F32)


def _bdot(a, b, ca=1, cb=0):
    return _dot(a.astype(BF16), b.astype(BF16), ca, cb)


def _mm(name, a, b, *, ta=False, tb=False, epilogue=None, extras=(), out_dtypes=(F32,),
        tm_pref=MXU_TILE, tn_pref=1536, tk_pref=MXU_TILE):
    M, K = (a.shape[1], a.shape[0]) if ta else a.shape
    N = b.shape[0] if tb else b.shape[1]
    assert (b.shape[1] if tb else b.shape[0]) == K
    tm, tn, tk = _tile(M, tm_pref), _tile(N, tn_pref), _tile(K, tk_pref)
    nk = K // tk
    n_out = len(out_dtypes)
    n_ext = len(extras)

    def body(*refs):
        a_ref, b_ref = refs[0], refs[1]
        ext = refs[2:2 + n_ext]
        outs = refs[2 + n_ext:2 + n_ext + n_out]
        k = pl.program_id(2)
        prod = _bdot(a_ref[...], b_ref[...], 0 if ta else 1, 1 if tb else 0)

        def finish(r):
            res = (r,) if epilogue is None else epilogue(r, *[e[...] for e in ext])
            for o, v in zip(outs, res):
                o[...] = v.astype(o.dtype)

        if nk == 1:
            finish(prod)
            return
        acc = refs[-1]

        @pl.when(k == 0)
        def _():
            acc[...] = jnp.zeros_like(acc)

        acc[...] += prod

        @pl.when(k == nk - 1)
        def _():
            finish(acc[...])

    a_spec = pl.BlockSpec((tk, tm), lambda i, j, k: (k, i)) if ta else pl.BlockSpec((tm, tk), lambda i, j, k: (i, k))
    b_spec = pl.BlockSpec((tn, tk), lambda i, j, k: (j, k)) if tb else pl.BlockSpec((tk, tn), lambda i, j, k: (k, j))
    ext_specs = []
    for e in extras:
        if e.shape[0] == 1 and M != 1:
            ext_specs.append(pl.BlockSpec((1, tn), lambda i, j, k: (0, j)))
        else:
            ext_specs.append(pl.BlockSpec((tm, tn), lambda i, j, k: (i, j)))
    out = pl.pallas_call(
        body, name=name,
        grid=(M // tm, N // tn, nk),
        in_specs=[a_spec, b_spec] + ext_specs,
        out_specs=[pl.BlockSpec((tm, tn), lambda i, j, k: (i, j)) for _ in range(n_out)],
        out_shape=[jax.ShapeDtypeStruct((M, N), dt) for dt in out_dtypes],
        scratch_shapes=[pltpu.VMEM((tm, tn), F32)] if nk > 1 else [],
        compiler_params=_cparams(("parallel", "parallel", "arbitrary")),
    )(a, b, *extras)
    return out[0] if n_out == 1 else tuple(out)


def _rows(name, fn, ins, outs, *, tr, accs=()):
    ins = [(e[0], e[1]) + (e[2] if len(e) > 2 else (0, e[0].shape[-1])) for e in ins]
    n_rows = next(e[0].shape[0] for e in ins if e[1] == "row")
    assert n_rows % tr == 0 and tr % SUBLANE == 0
    steps = n_rows // tr
    t8 = tr // SUBLANE
    n8 = n_rows // SUBLANE
    n_in, n_out, n_acc = len(ins), len(outs), len(accs)

    def body(*refs):
        i = pl.program_id(0)
        vals = fn(i, steps, *[r[...] for r in refs[:n_in]])
        if not isinstance(vals, (tuple, list)):
            vals = (vals,)
        assert len(vals) == n_out + n_acc
        for o, v in zip(refs[n_in:n_in + n_out], vals[:n_out]):
            o[...] = v.astype(o.dtype)
        if n_acc:
            acc_refs = refs[n_in + n_out:]

            @pl.when(i == 0)
            def _():
                for r in acc_refs:
                    r[...] = jnp.zeros_like(r)

            for r, v in zip(acc_refs, vals[n_out:]):
                r[...] += v.astype(r.dtype)

    in_specs = []
    for a, kind, cb, c in ins:
        if kind == "row":
            in_specs.append(pl.BlockSpec((tr, c), lambda i, cb=cb: (i, cb)))
        elif kind == "full":
            in_specs.append(pl.BlockSpec(a.shape, lambda i, z=(0,) * a.ndim: z))
        elif kind == "prev8":
            in_specs.append(pl.BlockSpec((SUBLANE, c), lambda i, cb=cb: (jnp.maximum(i * t8 - 1, 0), cb)))
        elif kind == "next8":
            in_specs.append(pl.BlockSpec((SUBLANE, c), lambda i, cb=cb: (jnp.minimum((i + 1) * t8, n8 - 1), cb)))
        else:
            raise ValueError(kind)
    out_specs = [pl.BlockSpec((tr, c), lambda i: (i, 0)) for c, _ in outs]
    out_specs += [pl.BlockSpec(s, lambda i, z=(0,) * len(s): z) for s, _ in accs]
    out_shape = [jax.ShapeDtypeStruct((n_rows, c), dt) for c, dt in outs]
    out_shape += [jax.ShapeDtypeStruct(s, dt) for s, dt in accs]
    res = pl.pallas_call(
        body, name=name, grid=(steps,), in_specs=in_specs, out_specs=out_specs, out_shape=out_shape,
        compiler_params=_cparams(("arbitrary",) if n_acc else ("parallel",)),
    )(*[e[0] for e in ins])
    return res[0] if len(res) == 1 else tuple(res)


def _colsum(x):
    return jnp.sum(x, axis=0, keepdims=True)


def _sum_all(x):
    return jnp.sum(jnp.sum(x, axis=1, keepdims=True), axis=0, keepdims=True)


def _rmsnorm_fwd(name, x, gain):
    def fn(i, n, xt, g):
        r = lax.rsqrt(jnp.mean(xt * xt, axis=-1, keepdims=True) + EPS)
        return (xt * r * g,)
    return _rows(name, fn, [(x, "row"), (gain, "full")], [(x.shape[1], BF16)], tr=512)


def _rmsnorm_bwd(name, x, gain, dh, dres):
    def fn(i, n, xt, g, dht, drt):
        r = lax.rsqrt(jnp.mean(xt * xt, axis=-1, keepdims=True) + EPS)
        xh = xt * r
        dxn = dht * g
        dx = r * (dxn - xh * jnp.mean(dxn * xh, axis=-1, keepdims=True))
        return drt + dx, _colsum(dht * xh)
    D = x.shape[1]
    return _rows(name, fn, [(x, "row"), (gain, "full"), (dh, "row"), (dres, "row")], [(D, F32)],
                 tr=256, accs=[((1, D), F32)])


def _head_consts():
    import numpy as np
    e = np.arange(A_WIDTH) % A_HEAD_DIM
    inv = (np.float32(ROPE_THETA) ** (-np.arange(0, ROPE_DIM, 2, dtype=np.float32) / np.float32(ROPE_DIM))).astype(np.float32)
    c = np.zeros((8, A_WIDTH), np.float32)
    c[0] = np.where(e < ROPE_DIM, inv[e % ROPE_HALF], 0.0)
    c[1] = np.where(e < ROPE_HALF, -1.0, np.where(e < ROPE_DIM, 1.0, 0.0))
    c[2] = (e < ROPE_HALF).astype(np.float32)
    c[3] = (e < ROPE_DIM).astype(np.float32)
    return jnp.asarray(c)


def _block_diag(scale):
    import numpy as np
    h = np.arange(A_WIDTH) // A_HEAD_DIM
    return jnp.asarray((h[:, None] == h[None, :]).astype(np.float32) * scale, dtype=BF16)


def _seg_sum(x, bd):
    hi = x.astype(BF16)
    lo = (x - hi.astype(F32)).astype(BF16)
    return _dot(hi, bd) + _dot(lo, bd)


def _rope_tables(positions, consts):
    def fn(i, n, pos, c):
        ang = pos.astype(F32) * c[0:1, :]
        return jnp.cos(ang), jnp.sin(ang) * c[1:2, :]
    return _rows("rope_tables", fn, [(positions, "row"), (consts, "full")],
                 [(A_WIDTH, F32), (A_WIDTH, F32)], tr=512)


def _rope_apply(y, ct, st, low):
    rolled = jnp.where(low, pltpu.roll(y, A_WIDTH - ROPE_HALF, 1), pltpu.roll(y, ROPE_HALF, 1))
    return y * ct + rolled * st


def _rope_apply_bwd(dout, ct, st, low, in16):
    t = dout * st
    back = jnp.where(low, pltpu.roll(t, A_WIDTH - ROPE_HALF, 1), jnp.where(in16, pltpu.roll(t, ROPE_HALF, 1), 0.0))
    return dout * ct + back


def _attn_prep(qkv, gains, ct, st, consts, bd):
    def fn(i, n, t, g, c_t, s_t, c, b):
        low = c[2:3, :] > 0.5
        groups = []
        for grp in range(3):
            cols = []
            for which in range(3):
                off = (grp * 3 + which) * A_WIDTH
                x = t[:, off:off + A_WIDTH]
                if which == 2:
                    cols.append(x.astype(BF16))
                    continue
                r = lax.rsqrt(_seg_sum(x * x, b) + EPS)
                y = x * r * g[grp * 2 + which:grp * 2 + which + 1, :]
                cols.append(_rope_apply(y, c_t, s_t, low).astype(BF16))
            groups.append(jnp.concatenate(cols, axis=1))
        return tuple(groups)
    return _rows("attn_prep", fn, [(qkv, "row"), (gains, "full"), (ct, "row"), (st, "row"), (consts, "full"), (bd, "full")],
                 [(3 * A_WIDTH, BF16)] * 3, tr=256)


def _band_mask(n):
    row = lax.broadcasted_iota(jnp.int32, (BAND, 2 * BAND), 0)
    col = lax.broadcasted_iota(jnp.int32, (BAND, 2 * BAND), 1)
    dist = row + BAND - col
    return (dist >= 0) & (dist <= BAND) & ((col >= BAND) | (n > 0))


def _attn_fwd(qkvn, grp):
    S = qkvn.shape[0]
    d = SWA_GROUPS[grp][1]
    L = S // d
    nblk = L // BAND
    assert L % BAND == 0
    view = qkvn.reshape(L, d * 3 * A_WIDTH)

    def body(q_ref, kc_ref, kp_ref, vc_ref, vp_ref, o_ref, lse_ref):
        n = pl.program_id(1)
        valid = _band_mask(n)
        first = lax.broadcasted_iota(jnp.int32, (BAND, LANE), 1) < A_HEAD_DIM
        pairs = [slice(pr * LANE, (pr + 1) * LANE) for pr in range(A_WIDTH // LANE)]
        halves = (first, jnp.logical_not(first))
        qps = [q_ref[:, sl] for sl in pairs]
        kcats = [jnp.concatenate([kp_ref[:, sl], kc_ref[:, sl]], axis=0) for sl in pairs]
        vcats = [jnp.concatenate([vp_ref[:, sl], vc_ref[:, sl]], axis=0) for sl in pairs]
        heads = [(pr, m) for pr in range(len(pairs)) for m in halves]
        ss = [_dot(jnp.where(m, qps[pr], jnp.zeros_like(qps[pr])), kcats[pr], 1, 1) for pr, m in heads]
        ps, lses = [], []
        for s in ss:
            s = jnp.where(valid, s * (A_HEAD_DIM ** -0.5), -1e30)
            mx = jnp.max(s, axis=-1, keepdims=True)
            e = jnp.exp(s - mx)
            l = jnp.sum(e, axis=-1, keepdims=True)
            ps.append((e / l).astype(BF16))
            lses.append(mx + jnp.log(l))
        os_ = [_dot(p, vcats[pr]) for p, (pr, _) in zip(ps, heads)]
        o_ref[...] = jnp.concatenate([jnp.where(first, os_[2 * pr], os_[2 * pr + 1]) for pr in range(len(pairs))], axis=1)
        lse_ref[...] = jnp.concatenate([jnp.where(first, lses[2 * pr], lses[2 * pr + 1]) for pr in range(len(pairs))], axis=1)

    blk = (BAND, A_WIDTH)
    o, lse = pl.pallas_call(
        body, name=f"attn_fwd_g{grp}", grid=(d, nblk),
        in_specs=[pl.BlockSpec(blk, lambda r, n: (n, r * 3)),
                  pl.BlockSpec(blk, lambda r, n: (n, r * 3 + 1)),
                  pl.BlockSpec(blk, lambda r, n: (jnp.maximum(n - 1, 0), r * 3 + 1)),
                  pl.BlockSpec(blk, lambda r, n: (n, r * 3 + 2)),
                  pl.BlockSpec(blk, lambda r, n: (jnp.maximum(n - 1, 0), r * 3 + 2))],
        out_specs=[pl.BlockSpec(blk, lambda r, n: (n, r)), pl.BlockSpec(blk, lambda r, n: (n, r))],
        out_shape=[jax.ShapeDtypeStruct((L, d * A_WIDTH), F32)] * 2,
        compiler_params=_cparams(("parallel", "parallel")),
    )(view, view, view, view, view)
    return o.reshape(S, A_WIDTH), lse.reshape(S, A_WIDTH)


def _merge_weights(l0, l1, l2):
    mx = jnp.maximum(jnp.maximum(l0, l1), l2)
    e0, e1, e2 = jnp.exp(l0 - mx), jnp.exp(l1 - mx), jnp.exp(l2 - mx)
    inv = 1.0 / (e0 + e1 + e2)
    return e0 * inv, e1 * inv, e2 * inv


def _attn_merge(os_, lses):
    def fn(i, n, o0, o1, o2, l0, l1, l2):
        w0, w1, w2 = _merge_weights(l0, l1, l2)
        return (w0 * o0 + w1 * o1 + w2 * o2,)
    ins = [(a, "row") for a in (*os_, *lses)]
    return _rows("attn_merge", fn, ins, [(A_WIDTH, BF16)], tr=512)


def _attn_merge_bwd(do, os_, lses, bd1):
    def fn(i, n, dot_, o0, o1, o2, l0, l1, l2, b):
        w0, w1, w2 = _merge_weights(l0, l1, l2)
        o = w0 * o0 + w1 * o1 + w2 * o2
        dsum = _seg_sum(dot_ * o, b)
        return (w0 * dot_, w1 * dot_, w2 * dot_, -w0 * dsum, -w1 * dsum, -w2 * dsum)
    ins = [(do, "row")] + [(a, "row") for a in (*os_, *lses)] + [(bd1, "full")]
    res = _rows("attn_merge_bwd", fn, ins, [(A_WIDTH, BF16)] * 3 + [(A_WIDTH, F32)] * 3, tr=256)
    return res[:3], res[3:]


def _lane_pick(x, lane_idx, lane):
    return jnp.sum(jnp.where(lane_idx == lane, x, 0.0), axis=-1, keepdims=True)


def _attn_bwd(qkvn, grp, do_g, lse, c_g):
    S = qkvn.shape[0]
    d = SWA_GROUPS[grp][1]
    L = S // d
    nblk = L // BAND
    view = qkvn.reshape(L, d * 3 * A_WIDTH)
    dov, lsev, cv = (t.reshape(L, d * A_WIDTH) for t in (do_g, lse, c_g))

    def body(q_ref, kc_ref, kp_ref, vc_ref, vp_ref, do_ref, lse_ref, c_ref, dq_ref, dk_ref, dv_ref, ck, cv_):
        n = pl.program_id(1)

        @pl.when(n == 0)
        def _():
            ck[...] = jnp.zeros_like(ck)
            cv_[...] = jnp.zeros_like(cv_)

        @pl.when(n < nblk)
        def _():
            valid = _band_mask(n)
            lane = lax.broadcasted_iota(jnp.int32, (BAND, LANE), 1)
            first = lane < A_HEAD_DIM
            lane2 = lax.broadcasted_iota(jnp.int32, (2 * BAND, LANE), 1) < A_HEAD_DIM
            pairs = [slice(pr * LANE, (pr + 1) * LANE) for pr in range(A_WIDTH // LANE)]
            halves = (first, jnp.logical_not(first))
            qps = [q_ref[:, sl] for sl in pairs]
            dops = [do_ref[:, sl] for sl in pairs]
            kcats = [jnp.concatenate([kp_ref[:, sl], kc_ref[:, sl]], axis=0) for sl in pairs]
            vcats = [jnp.concatenate([vp_ref[:, sl], vc_ref[:, sl]], axis=0) for sl in pairs]
            heads = [(pr, hh) for pr in range(len(pairs)) for hh in range(2)]
            zero = jnp.zeros_like(qps[0])
            ss = [_dot(jnp.where(halves[hh], qps[pr], zero), kcats[pr], 1, 1) for pr, hh in heads]
            dps = [_dot(jnp.where(halves[hh], dops[pr], zero), vcats[pr], 1, 1) for pr, hh in heads]
            dss, pbs = [], []
            for (pr, hh), s, dp in zip(heads, ss, dps):
                lse_h = _lane_pick(lse_ref[:, pairs[pr]], lane, hh * A_HEAD_DIM)
                c_h = _lane_pick(c_ref[:, pairs[pr]], lane, hh * A_HEAD_DIM)
                p = jnp.where(valid, jnp.exp(s * (A_HEAD_DIM ** -0.5) - lse_h), 0.0)
                dss.append((p * (dp + c_h) * (A_HEAD_DIM ** -0.5)).astype(BF16))
                pbs.append(p.astype(BF16))
            dqs = [_dot(ds, kcats[pr]) for ds, (pr, _) in zip(dss, heads)]
            dks = [_dot(ds, qps[pr], 0, 0) for ds, (pr, _) in zip(dss, heads)]
            dvs = [_dot(pb, dops[pr], 0, 0) for pb, (pr, _) in zip(pbs, heads)]
            for pr, sl in enumerate(pairs):
                dq_ref[:, sl] = jnp.where(first, dqs[2 * pr], dqs[2 * pr + 1])
                dkc = jnp.where(lane2, dks[2 * pr], dks[2 * pr + 1])
                dvc = jnp.where(lane2, dvs[2 * pr], dvs[2 * pr + 1])
                dk_ref[:, sl] = ck[:, sl] + dkc[:BAND]
                dv_ref[:, sl] = cv_[:, sl] + dvc[:BAND]
                ck[:, sl] = dkc[BAND:]
                cv_[:, sl] = dvc[BAND:]

        @pl.when(n == nblk)
        def _():
            dk_ref[...] = ck[...]
            dv_ref[...] = cv_[...]

    blk = (BAND, A_WIDTH)
    last = nblk - 1
    qn = lambda n: jnp.minimum(n, last)
    pn = lambda n: jnp.clip(n - 1, 0, last)
    dq, dk, dv = pl.pallas_call(
        body, name=f"attn_bwd_g{grp}", grid=(d, nblk + 1),
        in_specs=[pl.BlockSpec(blk, lambda r, n: (qn(n), r * 3)),
                  pl.BlockSpec(blk, lambda r, n: (qn(n), r * 3 + 1)),
                  pl.BlockSpec(blk, lambda r, n: (pn(n), r * 3 + 1)),
                  pl.BlockSpec(blk, lambda r, n: (qn(n), r * 3 + 2)),
                  pl.BlockSpec(blk, lambda r, n: (pn(n), r * 3 + 2)),
                  pl.BlockSpec(blk, lambda r, n: (qn(n), r)),
                  pl.BlockSpec(blk, lambda r, n: (qn(n), r)),
                  pl.BlockSpec(blk, lambda r, n: (qn(n), r))],
        out_specs=[pl.BlockSpec(blk, lambda r, n: (qn(n), r)),
                   pl.BlockSpec(blk, lambda r, n: (pn(n), r)),
                   pl.BlockSpec(blk, lambda r, n: (pn(n), r))],
        out_shape=[jax.ShapeDtypeStruct((L, d * A_WIDTH), F32)] * 3,
        scratch_shapes=[pltpu.VMEM(blk, F32), pltpu.VMEM(blk, F32)],
        compiler_params=_cparams(("parallel", "arbitrary")),
    )(view, view, view, view, view, dov, lsev, cv)
    return tuple(t.reshape(S, A_WIDTH) for t in (dq, dk, dv))


def _attn_prep_bwd(qkv, grads, gains, ct, st, consts, bd):
    def fn(i, n, t, g, c_t, s_t, c, b, *gr):
        low = c[2:3, :] > 0.5
        in16 = c[3:4, :] > 0.5
        cols, dgs = [], []
        for grp in range(3):
            for which in range(3):
                dout = gr[grp * 3 + which]
                if which == 2:
                    cols.append(dout.astype(BF16))
                    continue
                off = (grp * 3 + which) * A_WIDTH
                x = t[:, off:off + A_WIDTH]
                gain = g[grp * 2 + which:grp * 2 + which + 1, :]
                r = lax.rsqrt(_seg_sum(x * x, b) + EPS)
                xh = x * r
                dy = _rope_apply_bwd(dout, c_t, s_t, low, in16)
                dyn = dy * gain
                dx = r * (dyn - xh * _seg_sum(dyn * xh, b))
                cols.append(dx.astype(BF16))
                dgs.append(_colsum(dy * xh))
        return (jnp.concatenate(cols, axis=1), *dgs)
    ins = [(qkv, "row"), (gains, "full"), (ct, "row"), (st, "row"), (consts, "full"), (bd, "full")] + [(a, "row") for a in grads]
    res = _rows("attn_prep_bwd", fn, ins, [(A_QKV, BF16)], tr=128, accs=[((1, A_WIDTH), F32)] * 6)
    return res[0], res[1:]


DN_QKV = 3 * DN_WIDTH
DN_QKVZ = DN_QKV + DN_WIDTH


def _sigmoid(x):
    return 1.0 / (1.0 + jnp.exp(-x))


def _softplus(x):
    return jnp.maximum(x, 0.0) + jnp.log(1.0 + jnp.exp(-jnp.abs(x)))


def _conv_taps(xs, w, tr):
    acc = None
    for j in range(CONV_W):
        sh = CONV_W - 1 - j
        term = (pltpu.roll(xs, sh, 0) if sh else xs)[SUBLANE:] * w[j:j + 1, :]
        acc = term if acc is None else acc + term
    return acc


def _dn_prep(qkvz, ab, convw, alog_row, dt_row):
    tr = 256

    def fn(i, n, x, xp, abt, w, al, dt):
        xp = jnp.where(i > 0, xp, 0.0)
        u = _conv_taps(jnp.concatenate([xp, x], axis=0), w, tr)
        y = u * _sigmoid(u)
        qs, ks = [], []
        for h in range(DN_HEADS):
            for dst, base, sc in ((qs, 0, DN_DIM ** -0.5), (ks, DN_WIDTH, 1.0)):
                seg = y[:, base + h * DN_DIM:base + (h + 1) * DN_DIM]
                dst.append(seg * (lax.rsqrt(jnp.sum(seg * seg, axis=-1, keepdims=True) + EPS) * sc))
        lane = lax.broadcasted_iota(jnp.int32, abt.shape, 1)
        g = -jnp.exp(al) * _softplus(abt + dt)
        gb = jnp.where(lane < DN_HEADS, g, jnp.where(lane < 2 * DN_HEADS, _sigmoid(abt), 0.0))
        return u, jnp.concatenate(qs, axis=1), jnp.concatenate(ks, axis=1), y[:, 2 * DN_WIDTH:], gb

    ins = [(qkvz, "row", (0, DN_QKV)), (qkvz, "prev8", (0, DN_QKV)), (ab, "row"), (convw, "full"),
           (alog_row, "full"), (dt_row, "full")]
    return _rows("dn_prep", fn, ins, [(DN_QKV, F32), (DN_WIDTH, F32), (DN_WIDTH, F32), (DN_WIDTH, F32), (LANE, F32)], tr=tr)


def _tri_masks():
    row = lax.broadcasted_iota(jnp.int32, (CHUNK, CHUNK), 0)
    col = lax.broadcasted_iota(jnp.int32, (CHUNK, CHUNK), 1)
    return row >= col, row > col, row == col


def _heads(fn, *lists):
    return [fn(*xs) for xs in zip(*lists)]


def _split(x):
    hi = x.astype(BF16)
    return hi, (x - hi.astype(F32)).astype(BF16)


def _dot3(a, b, ca=1, cb=0):
    (ah, al), (bh, bl) = a, b
    return _dot(ah, bh, ca, cb) + (_dot(ah, bl, ca, cb) + _dot(al, bh, ca, cb))


def _unit_lower_inverse(a_list, eye):
    ts = [eye - a for a in a_list]
    parts = [_split(a) for a in a_list]
    for _ in range(5):
        parts = [_split(_dot3(p, p)) for p in parts]
        ts = [t + _dot3(_split(t), p) for t, p in zip(ts, parts)]
    return ts


def _dn_terms(qs, ks, vs, gb):
    lower, strict, diag = _tri_masks()
    lane = lax.broadcasted_iota(jnp.int32, (CHUNK, LANE), 1)
    is_last = lax.broadcasted_iota(jnp.int32, (CHUNK, 1), 0) == CHUNK - 1
    hs = range(DN_HEADS)
    gc = _dot(lower.astype(F32), gb, precision=HIGHEST)
    gct = jnp.transpose(gc)
    bcol = [_lane_pick(gb, lane, DN_HEADS + h) for h in hs]
    gcol = [_lane_pick(gc, lane, h) for h in hs]
    glast = [jnp.sum(jnp.where(is_last, g, 0.0), axis=0, keepdims=True) for g in gcol]
    decay = [jnp.exp(jnp.where(lower, gcol[h] - gct[h:h + 1, :], -1e30)) for h in hs]
    kb = _heads(lambda k, b: k * b, ks, bcol)
    kk = _heads(lambda x, k: _bdot(x, k, 1, 1), kb, ks)
    qk = _heads(lambda q, k: _bdot(q, k, 1, 1), qs, ks)
    a = _heads(lambda x, d: jnp.where(strict, x * d, 0.0), kk, decay)
    t = [_split(x) for x in _unit_lower_inverse(a, diag.astype(F32))]
    eg = [jnp.exp(g) for g in gcol]
    egl = _heads(lambda gl, g: jnp.exp(gl - g), glast, gcol)
    rhs_w = _heads(lambda x, e: x * e, kb, eg)
    u = _heads(lambda tt, v, b: _dot3(tt, _split(v * b)), t, vs, bcol)
    w = _heads(lambda tt, r: _dot3(tt, _split(r)), t, rhs_w)
    return dict(bcol=bcol, decay=decay, kb=kb, a=a, t=t, eg=eg, egl=egl, rhs_w=rhs_w, u=u, w=w,
                attn=_heads(lambda x, d: x * d, qk, decay), q_dec=_heads(lambda q, e: q * e, qs, eg),
                k_dec=_heads(lambda k, e: k * e, ks, egl), c_dec=[jnp.exp(g) for g in glast],
                lower=lower, strict=strict, lane=lane, is_last=is_last)


def _head_slices(ref):
    return [ref[:, h * DN_DIM:(h + 1) * DN_DIM] for h in range(DN_HEADS)]


def _dn_chunk_fwd(q, k, v, gb):
    S = q.shape[0]
    N = S // CHUNK

    def body(q_ref, k_ref, v_ref, gb_ref, o_ref, st_ref, state):
        @pl.when(pl.program_id(0) == 0)
        def _():
            state[...] = jnp.zeros_like(state)

        f = _dn_terms(_head_slices(q_ref), _head_slices(k_ref), _head_slices(v_ref), gb_ref[...])
        s = [state[h] for h in range(DN_HEADS)]
        for h in range(DN_HEADS):
            st_ref[0, h] = s[h]
        sb = [x.astype(BF16) for x in s]
        v_new = _heads(lambda u, w, x: u - _bdot(w, x), f["u"], f["w"], sb)
        o = _heads(lambda qd, x, at, vn: _bdot(qd, x) + _bdot(at, vn), f["q_dec"], sb, f["attn"], v_new)
        new_s = _heads(lambda x, c, kd, vn: x * c + _bdot(kd, vn, 0, 0), s, f["c_dec"], f["k_dec"], v_new)
        for h in range(DN_HEADS):
            o_ref[:, h * DN_DIM:(h + 1) * DN_DIM] = o[h]
            state[h] = new_s[h]

    blk = pl.BlockSpec((CHUNK, DN_WIDTH), lambda n: (n, 0))
    st_blk = pl.BlockSpec((1, DN_HEADS, DN_DIM, DN_DIM), lambda n: (n, 0, 0, 0))
    return pl.pallas_call(
        body, name="dn_chunk_fwd", grid=(N,),
        in_specs=[blk, blk, blk, pl.BlockSpec((CHUNK, LANE), lambda n: (n, 0))],
        out_specs=[blk, st_blk],
        out_shape=[jax.ShapeDtypeStruct((S, DN_WIDTH), F32), jax.ShapeDtypeStruct((N, DN_HEADS, DN_DIM, DN_DIM), F32)],
        scratch_shapes=[pltpu.VMEM((DN_HEADS, DN_DIM, DN_DIM), F32)],
        compiler_params=_cparams(("arbitrary",)),
    )(q, k, v, gb)


def _dn_chunk_bwd(q, k, v, gb, states, do):
    S = q.shape[0]
    N = S // CHUNK

    def body(q_ref, k_ref, v_ref, gb_ref, st_ref, do_ref, dq_ref, dk_ref, dv_ref, dgb_ref, dstate):
        @pl.when(pl.program_id(0) == 0)
        def _():
            dstate[...] = jnp.zeros_like(dstate)

        hs = range(DN_HEADS)
        qs, ks, vs, dos = (_head_slices(r) for r in (q_ref, k_ref, v_ref, do_ref))
        f = _dn_terms(qs, ks, vs, gb_ref[...])
        lane, is_last = f["lane"], f["is_last"]
        rowsum = lambda x: jnp.sum(x, axis=-1, keepdims=True)
        s = [st_ref[0, h] for h in hs]
        dsn = [dstate[h] for h in hs]
        sb = [x.astype(BF16) for x in s]
        dsb = [x.astype(BF16) for x in dsn]
        dob = [x.astype(BF16) for x in dos]
        v_new = _heads(lambda u, w, x: u - _bdot(w, x), f["u"], f["w"], sb)
        dv_new = _heads(lambda at, d, kd, x: _bdot(at, d, 0, 0) + _bdot(kd, x), f["attn"], dob, f["k_dec"], dsb)
        dattn = _heads(lambda d, vn: _bdot(d, vn, 1, 1), dob, v_new)
        dq_dec = _heads(lambda d, x: _bdot(d, x, 1, 1), dob, sb)
        dk_dec = _heads(lambda vn, x: _bdot(vn, x, 1, 1), v_new, dsb)
        dw = _heads(lambda dv_, x: -_bdot(dv_, x, 1, 1), dv_new, sb)
        new_ds = _heads(lambda x, c, qd, d, w, dv_: x * c + _bdot(qd, d, 0, 0) - _bdot(w, dv_, 0, 0),
                        dsn, f["c_dec"], f["q_dec"], dob, f["w"], dv_new)
        for h in hs:
            dstate[h] = new_ds[h]
        drhs_u = _heads(lambda tt, x: _dot3(tt, _split(x), 0, 0), f["t"], dv_new)
        drhs_w = _heads(lambda tt, x: _dot3(tt, _split(x), 0, 0), f["t"], dw)
        da = _heads(lambda du_, u, dw_, w: jnp.where(f["strict"], -(_bdot(du_, u, 1, 1) + _bdot(dw_, w, 1, 1)), 0.0),
                    drhs_u, f["u"], drhs_w, f["w"])
        dkk = _heads(lambda x, d: x * d, da, f["decay"])
        dqk = _heads(lambda x, d: x * d, dattn, f["decay"])
        dkb = _heads(lambda x, k_, dw_, e: _bdot(x, k_) + dw_ * e, dkk, ks, drhs_w, f["eg"])
        dq = _heads(lambda x, k_, dqd, e: _bdot(x, k_) + dqd * e, dqk, ks, dq_dec, f["eg"])
        dk = _heads(lambda x, kb_, y, q_, dkd, el, dkb_, b: _bdot(x, kb_, 0, 0) + _bdot(y, q_, 0, 0) + dkd * el + dkb_ * b,
                    dkk, f["kb"], dqk, qs, dk_dec, f["egl"], dkb, f["bcol"])
        m = _heads(lambda x, a_, y, at: x * a_ + y * at, da, f["a"], dattn, f["attn"])
        ones = jnp.ones((CHUNK, LANE), BF16)
        col_m = [(_dot(mh, ones, 0, 0) + _dot(ml, ones, 0, 0))[:, 0:1] for mh, ml in map(_split, m)]
        dgc_all = jnp.zeros((CHUNK, LANE), F32)
        dbeta_all = jnp.zeros((CHUNK, LANE), F32)
        for h in hs:
            dq_ref[:, h * DN_DIM:(h + 1) * DN_DIM] = dq[h]
            dk_ref[:, h * DN_DIM:(h + 1) * DN_DIM] = dk[h]
            dv_ref[:, h * DN_DIM:(h + 1) * DN_DIM] = drhs_u[h] * f["bcol"][h]
            kdec_term = rowsum(dk_dec[h] * f["k_dec"][h])
            dc_dec = _sum_all(dsn[h] * s[h])
            dgc = (rowsum(m[h]) - col_m[h] + rowsum(dq_dec[h] * f["q_dec"][h]) - kdec_term
                   + rowsum(drhs_w[h] * f["rhs_w"][h]))
            last_extra = jnp.sum(kdec_term, axis=0, keepdims=True) + dc_dec * f["c_dec"][h]
            dgc = dgc + jnp.where(is_last, last_extra, 0.0)
            dbeta = rowsum(drhs_u[h] * vs[h]) + rowsum(dkb[h] * ks[h])
            dgc_all = jnp.where(lane == h, dgc, dgc_all)
            dbeta_all = jnp.where(lane == DN_HEADS + h, dbeta, dbeta_all)
        dg_all = _dot(f["lower"].astype(F32), dgc_all, 0, 0, precision=HIGHEST)
        dgb_ref[...] = jnp.where(lane < DN_HEADS, dg_all, dbeta_all)

    rev = lambda n: (N - 1 - n, 0)
    blk = pl.BlockSpec((CHUNK, DN_WIDTH), rev)
    gblk = pl.BlockSpec((CHUNK, LANE), rev)
    st_blk = pl.BlockSpec((1, DN_HEADS, DN_DIM, DN_DIM), lambda n: (N - 1 - n, 0, 0, 0))
    return pl.pallas_call(
        body, name="dn_chunk_bwd", grid=(N,),
        in_specs=[blk, blk, blk, gblk, st_blk, blk],
        out_specs=[blk, blk, blk, gblk],
        out_shape=[jax.ShapeDtypeStruct((S, DN_WIDTH), F32)] * 3 + [jax.ShapeDtypeStruct((S, LANE), F32)],
        scratch_shapes=[pltpu.VMEM((DN_HEADS, DN_DIM, DN_DIM), F32)],
        compiler_params=_cparams(("arbitrary",)),
    )(q, k, v, gb, states, do)


def _dn_post(o, qkvz, gain_row):
    def fn(i, n, ot, z, g):
        cols = []
        for h in range(DN_HEADS):
            seg = ot[:, h * DN_DIM:(h + 1) * DN_DIM]
            cols.append(seg * lax.rsqrt(jnp.mean(seg * seg, axis=-1, keepdims=True) + EPS) * g)
        return (jnp.concatenate(cols, axis=1) * (z * _sigmoid(z)),)
    return _rows("dn_post", fn, [(o, "row"), (qkvz, "row", (3, DN_WIDTH)), (gain_row, "full")], [(DN_WIDTH, BF16)], tr=512)


def _dn_post_bwd(don, o, qkvz, gain_row):
    def fn(i, n, dy, ot, z, g):
        sg = _sigmoid(z)
        sz = z * sg
        dos, ohs = [], []
        dg = jnp.zeros((1, DN_DIM), F32)
        for h in range(DN_HEADS):
            sl = slice(h * DN_DIM, (h + 1) * DN_DIM)
            seg = ot[:, sl]
            r = lax.rsqrt(jnp.mean(seg * seg, axis=-1, keepdims=True) + EPS)
            oh = seg * r
            dno = dy[:, sl] * sz[:, sl]
            dg = dg + _colsum(dno * oh)
            dn = dno * g
            dos.append(r * (dn - oh * jnp.mean(dn * oh, axis=-1, keepdims=True)))
            ohs.append(oh * g)
        dz = dy * jnp.concatenate(ohs, axis=1) * (sg * (1.0 + z * (1.0 - sg)))
        return jnp.concatenate(dos, axis=1), dz, dg
    ins = [(don, "row"), (o, "row"), (qkvz, "row", (3, DN_WIDTH)), (gain_row, "full")]
    return _rows("dn_post_bwd", fn, ins, [(DN_WIDTH, F32), (DN_WIDTH, F32)], tr=256, accs=[((1, DN_DIM), F32)])


def _dn_prep_bwd(dq, dk, dv, dgb, u, ab, alog_row, dt_row):
    def fn(i, n, dqt, dkt, dvt, dgbt, ut, abt, al, dt):
        sg = _sigmoid(ut)
        y = ut * sg
        dys = []
        for grad, base, sc in ((dqt, 0, DN_DIM ** -0.5), (dkt, DN_WIDTH, 1.0)):
            for h in range(DN_HEADS):
                seg = y[:, base + h * DN_DIM:base + (h + 1) * DN_DIM]
                gr = grad[:, h * DN_DIM:(h + 1) * DN_DIM]
                r = lax.rsqrt(jnp.sum(seg * seg, axis=-1, keepdims=True) + EPS)
                xh = seg * r
                dys.append((r * sc) * (gr - xh * jnp.sum(gr * xh, axis=-1, keepdims=True)))
        dy = jnp.concatenate(dys + [dvt], axis=1)
        du = dy * (sg * (1.0 + ut * (1.0 - sg)))
        lane = lax.broadcasted_iota(jnp.int32, abt.shape, 1)
        is_g = lane < DN_HEADS
        ea = jnp.exp(al)
        x = abt + dt
        slope = -ea * _sigmoid(x)
        gval = -ea * _softplus(x)
        dg = jnp.where(is_g, dgbt, 0.0)
        beta = _sigmoid(abt)
        dab = jnp.where(is_g, dg * slope, jnp.where(lane < 2 * DN_HEADS, dgbt * beta * (1.0 - beta), 0.0))
        return du, dab, _colsum(dg * gval), _colsum(dg * slope)
    ins = [(dq, "row"), (dk, "row"), (dv, "row"), (dgb, "row"), (u, "row"), (ab, "row"), (alog_row, "full"), (dt_row, "full")]
    return _rows("dn_prep_bwd", fn, ins, [(DN_QKV, F32), (LANE, BF16)], tr=256, accs=[((1, LANE), F32)] * 2)


def _dn_conv_bwd(du, dz, qkvz, convw):
    tr = 256

    def fn(i, n, dut, dun, dzt, x, xp, w):
        dun = jnp.where(i < n - 1, dun, 0.0)
        dus = jnp.concatenate([dut, dun], axis=0)
        xs = jnp.concatenate([jnp.where(i > 0, xp, 0.0), x], axis=0)
        dx = None
        dws = []
        for j in range(CONV_W):
            sh = CONV_W - 1 - j
            term = (pltpu.roll(dus, tr + SUBLANE - sh, 0) if sh else dus)[:tr] * w[j:j + 1, :]
            dx = term if dx is None else dx + term
            dws.append(_colsum(dut * (pltpu.roll(xs, sh, 0) if sh else xs)[SUBLANE:]))
        return (jnp.concatenate([dx.astype(BF16), dzt.astype(BF16)], axis=1), *dws)

    ins = [(du, "row"), (du, "next8"), (dz, "row"), (qkvz, "row", (0, DN_QKV)), (qkvz, "prev8", (0, DN_QKV)), (convw, "full")]
    res = _rows("dn_conv_bwd", fn, ins, [(DN_QKVZ, BF16)], tr=tr, accs=[((1, DN_QKV), F32)] * CONV_W)
    return res[0], res[1:]


def _add(acc, r):
    return (r + acc,)


def _mlp_ple_fwd(i, x1, p_i, mlp_gain, ple_gain, w_up, w_down, w_ple, w_gate):
    hm = _rmsnorm_fwd(f"mlp_norm{i}", x1, mlp_gain)
    u, a = _mm(f"mlp_up{i}", hm, w_up, epilogue=lambda acc: (acc, jnp.square(jnp.maximum(acc, 0.0))),
               out_dtypes=(F32, BF16))
    x2 = _mm(f"mlp_down{i}", a, w_down, epilogue=_add, extras=(x1,))
    hp = _rmsnorm_fwd(f"ple_norm{i}", x2, ple_gain)
    pp = _mm(f"ple_proj{i}", p_i, w_ple)

    def gate_epilogue(acc, x2t, ppt):
        gate = _sigmoid(acc)
        return x2t + ppt * gate, gate

    x3, gate = _mm(f"ple_gate{i}", hp, w_gate, epilogue=gate_epilogue, extras=(x2, pp), out_dtypes=(F32, F32))
    return x3, dict(x1=x1, hm=hm, u=u, a=a, x2=x2, hp=hp, pp=pp, gate=gate, p=p_i)


def _mlp_ple_bwd(i, dx3, sv, mlp_gain, ple_gain, w_up, w_down, w_gate):
    def fn(_i, _n, d, g, pp):
        return d * g, d * pp * g * (1.0 - g)
    dpp, dzg = _rows(f"ple_gate_bwd{i}", fn, [(dx3, "row"), (sv["gate"], "row"), (sv["pp"], "row")],
                     [(D_MODEL, BF16), (D_MODEL, BF16)], tr=512)
    d_w_ple = _mm(f"ple_proj_dw{i}", sv["p"], dpp, ta=True, out_dtypes=(BF16,))
    d_w_gate = _mm(f"ple_gate_dw{i}", sv["hp"], dzg, ta=True, out_dtypes=(BF16,))
    dhp = _mm(f"ple_gate_dx{i}", dzg, w_gate, tb=True)
    dx2, d_ple_gain = _rmsnorm_bwd(f"ple_norm_bwd{i}", sv["x2"], ple_gain, dhp, dx3)
    d_w_down = _mm(f"mlp_down_dw{i}", sv["a"], dx2, ta=True, out_dtypes=(BF16,))
    du = _mm(f"mlp_down_dx{i}", dx2, w_down, tb=True, epilogue=lambda acc, ut: (acc * (2.0 * jnp.maximum(ut, 0.0)),),
             extras=(sv["u"],), out_dtypes=(BF16,))
    d_w_up = _mm(f"mlp_up_dw{i}", sv["hm"], du, ta=True, out_dtypes=(BF16,))
    dhm = _mm(f"mlp_up_dx{i}", du, w_up, tb=True)
    dx1, d_mlp_gain = _rmsnorm_bwd(f"mlp_norm_bwd{i}", sv["x1"], mlp_gain, dhm, dx2)
    return dx1, dict(w_ple=d_w_ple, w_ple_gate=d_w_gate, w_down=d_w_down, w_up=d_w_up,
                     ple_norm=d_ple_gain, mlp_norm=d_mlp_gain)


def _loss_fwd_bwd(y, target):
    D = y.shape[1]

    def fn(i, n, yt, tt):
        e = yt - tt
        return e * (1.0 / D), _colsum(e * e)
    dy, sq = _rows("loss", fn, [(y, "row"), (target, "row")], [(D, F32)], tr=512, accs=[((1, D), F32)])
    return sq, dy


def _after(value, token):
    return lax.optimization_barrier((value, token))[0]


def _local_step(x, p, positions, target, W, P, rest_of_weights, send_layer1, send_mlp0, send_attn):
    consts = _head_consts()
    bd = _block_diag(1.0 / A_HEAD_DIM)
    bd1 = _block_diag(1.0)
    ct, st = _rope_tables(positions, consts)
    gains = jnp.stack([jnp.tile(v, A_HEADS) for g in range(3) for v in (P["attn_q_gain"][g], P["attn_k_gain"][g])])
    pad = LANE - DN_HEADS
    alog_row = jnp.pad(P["dn_a_log"].reshape(1, DN_HEADS), ((0, 0), (0, pad)))
    dt_row = jnp.pad(P["dn_dt_bias"].reshape(1, DN_HEADS), ((0, 0), (0, pad)))
    ogain_row = P["dn_o_gain"].reshape(1, DN_DIM)
    row = lambda name, i: P[name][i:i + 1]

    h0 = _rmsnorm_fwd("mix_norm0", x, row("mix_norm", 0))
    qkv = _mm("attn_qkv", h0, W["attn_w_qkv"])
    qkvn = _attn_prep(qkv, gains, ct, st, consts, bd)
    os_, lses = zip(*[_attn_fwd(qkvn[g], g) for g in range(3)])
    o_attn = _attn_merge(os_, lses)
    x1 = _mm("attn_out", o_attn, W["attn_w_o"], epilogue=_add, extras=(x,))
    W = {**W, **rest_of_weights(x1)}
    x3, sv0 = _mlp_ple_fwd(0, x1, p[0], row("mlp_norm", 0), row("ple_norm", 0),
                           W["w_up"][0], W["w_down"][0], W["w_ple"][0], W["w_ple_gate"][0])
    h1 = _rmsnorm_fwd("mix_norm1", x3, row("mix_norm", 1))
    qkvz = _mm("dn_in_qkvz", h1, W["dn_w_qkvz"])
    ab = _mm("dn_in_ab", h1, W["dn_w_ab"])
    u, q, k, v, gb = _dn_prep(qkvz, ab, W["dn_conv"], alog_row, dt_row)
    o_dn, states = _dn_chunk_fwd(q, k, v, gb)
    on = _dn_post(o_dn, qkvz, ogain_row)
    x4 = _mm("dn_out", on, W["dn_w_o"], epilogue=_add, extras=(x3,))
    x6, sv1 = _mlp_ple_fwd(1, x4, p[1], row("mlp_norm", 1), row("ple_norm", 1),
                           W["w_up"][1], W["w_down"][1], W["w_ple"][1], W["w_ple_gate"][1])
    sq, dy = _loss_fwd_bwd(x6, target)

    dx4, g1 = _mlp_ple_bwd(1, dy, sv1, row("mlp_norm", 1), row("ple_norm", 1),
                           W["w_up"][1], W["w_down"][1], W["w_ple_gate"][1])
    don = _mm("dn_out_dx", dx4, W["dn_w_o"], tb=True)
    d_dn_w_o = _mm("dn_out_dw", on, dx4, ta=True, out_dtypes=(BF16,))
    do_dn, dz, d_ogain = _dn_post_bwd(don, o_dn, qkvz, ogain_row)
    dq, dk, dv, dgb = _dn_chunk_bwd(q, k, v, gb, states, do_dn)
    du, dab, d_alog, d_dt = _dn_prep_bwd(dq, dk, dv, dgb, u, ab, alog_row, dt_row)
    dqkvz, d_conv = _dn_conv_bwd(du, dz, qkvz, W["dn_conv"])
    dh1 = _mm("dn_in_qkvz_dx", dqkvz, W["dn_w_qkvz"], tb=True)
    dh1 = _mm("dn_in_ab_dx", dab, W["dn_w_ab"], tb=True, epilogue=_add, extras=(dh1,))
    d_w_qkvz = _mm("dn_in_qkvz_dw", h1, dqkvz, ta=True, out_dtypes=(BF16,))
    d_w_ab = _mm("dn_in_ab_dw", h1, dab, ta=True, out_dtypes=(BF16,))
    dx3, d_mix1 = _rmsnorm_bwd("mix_norm_bwd1", x3, row("mix_norm", 1), dh1, dx4)
    dx3 = _after(dx3, send_layer1(dict(
        dn_w_qkvz=d_w_qkvz, dn_w_ab=d_w_ab, dn_conv=jnp.concatenate(d_conv, 0), dn_w_o=d_dn_w_o,
        w_up=g1["w_up"], w_down=g1["w_down"], w_ple=g1["w_ple"], w_ple_gate=g1["w_ple_gate"])))
    dx1, g0 = _mlp_ple_bwd(0, dx3, sv0, row("mlp_norm", 0), row("ple_norm", 0),
                           W["w_up"][0], W["w_down"][0], W["w_ple_gate"][0])
    dx1 = _after(dx1, send_mlp0(dict(w_up=g0["w_up"], w_down=g0["w_down"], w_ple=g0["w_ple"], w_ple_gate=g0["w_ple_gate"])))
    do_attn = _mm("attn_out_dx", dx1, W["attn_w_o"], tb=True)
    d_attn_w_o = _mm("attn_out_dw", o_attn, dx1, ta=True, out_dtypes=(BF16,))
    dos, cs = _attn_merge_bwd(do_attn, os_, lses, bd1)
    grads9 = []
    for g in range(3):
        grads9 += list(_attn_bwd(qkvn[g], g, dos[g], lses[g], cs[g]))
    dqkv, dgains = _attn_prep_bwd(qkv, grads9, gains, ct, st, consts, bd)
    d_attn_w_qkv = _mm("attn_qkv_dw", h0, dqkv, ta=True, out_dtypes=(BF16,))
    dqkv = _after(dqkv, send_attn(dict(attn_w_qkv=d_attn_w_qkv, attn_w_o=d_attn_w_o)))
    dh0 = _mm("attn_qkv_dx", dqkv, W["attn_w_qkv"], tb=True)
    dx0, d_mix0 = _rmsnorm_bwd("mix_norm_bwd0", x, row("mix_norm", 0), dh0, dx1)

    dg = jnp.stack([t.reshape(A_HEADS, A_HEAD_DIM).sum(0) for t in dgains])
    small = dict(
        mix_norm=jnp.concatenate([d_mix0, d_mix1], 0),
        attn_q_gain=dg[0::2][None], attn_k_gain=dg[1::2][None],
        dn_a_log=d_alog[:, :DN_HEADS], dn_dt_bias=d_dt[:, :DN_HEADS], dn_o_gain=d_ogain,
        mlp_norm=jnp.concatenate([g0["mlp_norm"], g1["mlp_norm"]], 0),
        ple_norm=jnp.concatenate([g0["ple_norm"], g1["ple_norm"]], 0),
    )
    return sq, dx0, small


MESH_IDS = pl.DeviceIdType.MESH
ANY = pl.BlockSpec(memory_space=pl.ANY)


def _place():
    return lax.axis_index("x"), lax.axis_index("y"), lax.axis_index("c")


def _sem_scratch(n_streams):
    return [pltpu.SemaphoreType.DMA((n_streams, N_DEV - 1)), pltpu.SemaphoreType.DMA((n_streams, N_DEV - 1)),
            pltpu.SemaphoreType.DMA((n_streams,))]


def _all_gather(name, arrays, streams):
    n_in, n_st = len(arrays), len(streams)
    shapes = [arrays[a].shape if li is None else arrays[a].shape[1:] for a, li in streams]

    def body(*refs):
        in_refs, out_refs = refs[:n_in], refs[n_in:n_in + n_st]
        send_sems, recv_sems, local_sems = refs[n_in + n_st:]
        x, y, c = _place()
        me, sibling = (x, y, c), (x, y, 1 - c)
        chips = [(1 - x, y), (x, 1 - y), (1 - x, 1 - y)]

        def copy(s, k, block, to, own=False):
            a, li = streams[s]
            dst = out_refs[s].at[4 * block[0] + 2 * block[1] + block[2]]
            src = (in_refs[a] if li is None else in_refs[a].at[li]) if own else dst
            return pltpu.make_async_remote_copy(src_ref=src, dst_ref=dst, send_sem=send_sems.at[s, k],
                                                recv_sem=recv_sems.at[s, k], device_id=to, device_id_type=MESH_IDS)

        started = []
        for s, (a, li) in enumerate(streams):
            src = in_refs[a] if li is None else in_refs[a].at[li]
            mine = pltpu.make_async_copy(src, out_refs[s].at[4 * x + 2 * y + c], local_sems.at[s])
            mine.start()
            started.append(mine)
        sends = []
        for s in range(n_st):
            first = [copy(s, 0, me, sibling, own=True)]
            first += [copy(s, 1 + j, me, (*chip, c), own=True) for j, chip in enumerate(chips)]
            for cp in first:
                cp.start()
            sends += first
        for j, chip in enumerate(chips):
            for s in range(n_st):
                copy(s, 1 + j, (*chip, c), me).wait_recv()
                fwd = copy(s, 4 + j, (*chip, c), sibling)
                fwd.start()
                sends.append(fwd)
        for s in range(n_st):
            copy(s, 0, sibling, me).wait_recv()
            for j, chip in enumerate(chips):
                copy(s, 4 + j, (*chip, 1 - c), me).wait_recv()
        for cp in sends:
            cp.wait_send()
        for cp in started:
            cp.wait()

    return pl.pallas_call(
        body, name=name,
        out_shape=[jax.ShapeDtypeStruct((N_DEV,) + tuple(sh), arrays[a].dtype) for sh, (a, _) in zip(shapes, streams)],
        in_specs=[ANY] * n_in, out_specs=[ANY] * n_st, scratch_shapes=_sem_scratch(n_st),
    )(*arrays)


def _exchange(name, sends, recv_shapes, placement):
    n_st, n_out = len(sends), len(recv_shapes)

    def body(*refs):
        send_refs, recv_refs = refs[:n_st], refs[n_st:n_st + n_out]
        send_sems, recv_sems, local_sems = refs[n_st + n_out:]
        x, y, c = _place()
        me = 4 * x + 2 * y + c

        def landing(s, slot):
            r, off = placement[s]
            rows = sends[s].shape[1]
            if rows == recv_shapes[r][1]:
                return recv_refs[r].at[slot]
            return recv_refs[r].at[slot, pl.ds(off, rows)]

        local, copies, arrivals = [], [], []
        for s in range(n_st):
            cp = pltpu.make_async_copy(send_refs[s].at[me], landing(s, me), local_sems.at[s])
            cp.start()
            local.append(cp)
        for k in range(1, N_DEV):
            px = 1 - x if k & 4 else x
            py = 1 - y if k & 2 else y
            pc = 1 - c if k & 1 else c
            peer = 4 * px + 2 * py + pc
            for s in range(n_st):
                copies.append(pltpu.make_async_remote_copy(
                    src_ref=send_refs[s].at[peer], dst_ref=landing(s, me), send_sem=send_sems.at[s, k - 1],
                    recv_sem=recv_sems.at[s, k - 1], device_id=(px, py, pc), device_id_type=MESH_IDS))
                arrivals.append(pltpu.make_async_remote_copy(
                    src_ref=send_refs[s].at[peer], dst_ref=landing(s, peer), send_sem=send_sems.at[s, k - 1],
                    recv_sem=recv_sems.at[s, k - 1], device_id=(px, py, pc), device_id_type=MESH_IDS))
        for cp in copies:
            cp.start()
        for cp in arrivals:
            cp.wait_recv()
        for cp in copies:
            cp.wait_send()
        for cp in local:
            cp.wait()

    dtypes = {r: sends[s].dtype for s, (r, _) in enumerate(placement)}
    return pl.pallas_call(
        body, name=name, out_shape=[jax.ShapeDtypeStruct(sh, dtypes[r]) for r, sh in enumerate(recv_shapes)],
        in_specs=[ANY] * n_st, out_specs=[ANY] * n_out, scratch_shapes=_sem_scratch(n_st),
    )(*sends)


HBM = pl.BlockSpec(memory_space=pltpu.HBM)
SEM = pl.BlockSpec(memory_space=pltpu.SEMAPHORE)
FLOWS = pltpu.CompilerParams(has_side_effects=pltpu.SideEffectType.DATAFLOW_SIDE_EFFECTING)


def _in_hbm(a):
    return pltpu.with_memory_space_constraint(a, pltpu.HBM)


def _hbm_like(a):
    return pltpu.HBM(a.shape, a.dtype)


def _peers(x, y, c):
    return [(1 - x if k & 4 else x, 1 - y if k & 2 else y, 1 - c if k & 1 else c) for k in range(1, N_DEV)]


def _start_copies(name, n_remote, n_own, make_copies, operands):
    n = len(operands)

    def body(*refs):
        for cp in make_copies(refs[:n], refs[n], refs[n + 1], refs[n + 2]):
            cp.start()
        refs[-1][...] = jnp.zeros_like(refs[-1])

    res = pl.pallas_call(
        body, name=name,
        out_shape=(pltpu.SemaphoreType.DMA((n_remote,)), pltpu.SemaphoreType.DMA((n_remote,)), pltpu.SemaphoreType.DMA((n_own,)),
                   *[_hbm_like(t) for t in operands], jax.ShapeDtypeStruct((SUBLANE, LANE), F32)),
        in_specs=[HBM] * n, out_specs=(SEM, SEM, SEM, *[HBM] * n, pl.BlockSpec(memory_space=pltpu.VMEM)),
        input_output_aliases={i: 3 + i for i in range(n)}, compiler_params=FLOWS,
    )(*[_in_hbm(t) for t in operands])
    return res[:3], list(res[3:3 + n]), res[-1]


def _wait_copies(name, make_waits, sems, operands, after):
    n = len(operands)

    def body(*refs):
        for wait in make_waits(refs[:n], refs[n], refs[n + 1], refs[n + 2]):
            wait()

    res = pl.pallas_call(
        body, name=name, out_shape=tuple(_hbm_like(t) for t in operands),
        in_specs=[HBM] * n + [SEM, SEM, SEM, ANY], out_specs=tuple([HBM] * n),
        input_output_aliases={i: i for i in range(n)}, compiler_params=FLOWS,
    )(*operands, *sems, after)
    return list(res)


def _gather_plan(n_in, streams):
    def block(arr, s):
        a, li = streams[s]
        return arr[a] if li is None else arr[a].at[li]

    def copies(refs, send_sems, recv_sems, own_sems, arrivals=False):
        arr, land = refs[:n_in], refs[n_in:]
        x, y, c = _place()
        me = 4 * x + 2 * y + c
        out = []
        for s in range(len(streams)):
            out.append(("own", pltpu.make_async_copy(block(arr, s), land[s].at[me], own_sems.at[s])))
            for k, (px, py, pc) in enumerate(_peers(x, y, c)):
                out.append(("remote", pltpu.make_async_remote_copy(
                    src_ref=block(arr, s), dst_ref=land[s].at[4 * px + 2 * py + pc if arrivals else me],
                    send_sem=send_sems.at[s * (N_DEV - 1) + k], recv_sem=recv_sems.at[s * (N_DEV - 1) + k],
                    device_id=(px, py, pc), device_id_type=MESH_IDS)))
        return out
    return copies


def _exchange_plan(n_st):
    def copies(refs, send_sems, recv_sems, own_sems, arrivals=False):
        snd, rcv = refs[:n_st], refs[n_st:]
        x, y, c = _place()
        me = 4 * x + 2 * y + c
        out = []
        for s in range(n_st):
            out.append(("own", pltpu.make_async_copy(snd[s].at[me], rcv[s].at[me], own_sems.at[s])))
            for k, (px, py, pc) in enumerate(_peers(x, y, c)):
                peer = 4 * px + 2 * py + pc
                out.append(("remote", pltpu.make_async_remote_copy(
                    src_ref=snd[s].at[peer], dst_ref=rcv[s].at[peer if arrivals else me],
                    send_sem=send_sems.at[s * (N_DEV - 1) + k], recv_sem=recv_sems.at[s * (N_DEV - 1) + k],
                    device_id=(px, py, pc), device_id_type=MESH_IDS)))
        return out
    return copies


def _split_transfer(tag, plan, n_streams, operands):
    sems, operands, token = _start_copies(f"{tag}_start", n_streams * (N_DEV - 1), n_streams,
                                          lambda refs, a, b, o: [cp for _, cp in plan(refs, a, b, o)], operands)

    def waits(refs, a, b, o):
        out = []
        for kind, cp in plan(refs, a, b, o, arrivals=True):
            out += [cp.wait] if kind == "own" else [cp.wait_send, cp.wait_recv]
        return out

    return (lambda after: _wait_copies(f"{tag}_wait", waits, sems, operands, after)), token


def _gather_async(tag, arrays, streams):
    lands = [lax.empty((N_DEV,) + tuple(arrays[a].shape if li is None else arrays[a].shape[1:]), arrays[a].dtype)
             for a, li in streams]
    finish, token = _split_transfer(tag, _gather_plan(len(arrays), streams), len(streams), list(arrays) + lands)
    return (lambda after: finish(after)[len(arrays):]), token


def _exchange_async(tag, sends):
    recvs = [lax.empty(t.shape, t.dtype) for t in sends]
    finish, token = _split_transfer(tag, _exchange_plan(len(sends)), len(sends), list(sends) + recvs)
    return (lambda after: finish(after)[len(sends):]), token


def _dn_in_pieces():
    n = (DN_QKVZ + 2 * DN_HEADS) // N_DEV
    segs = ((0, DN_QKV, 0, 0), (DN_QKV, DN_QKV + 2 * DN_HEADS, 1, 0), (DN_QKV + 2 * DN_HEADS, DN_QKVZ + 2 * DN_HEADS, 0, DN_QKV))
    out = []
    for d in range(N_DEV):
        lo, hi = d * n, (d + 1) * n
        for s0, s1, tgt, t0 in segs:
            a, b = max(lo, s0), min(hi, s1)
            if a < b:
                out.append((d, a - lo, b - lo, tgt, t0 + a - s0))
    return out


def _unpack_cols(name, g):
    _, K, n = g.shape
    tr = 256

    def body(g_ref, o_ref):
        for d in range(N_DEV):
            o_ref[:, d * n:(d + 1) * n] = g_ref[d]

    return pl.pallas_call(
        body, name=name, grid=(K // tr,), in_specs=[pl.BlockSpec((N_DEV, tr, n), lambda i: (0, i, 0))],
        out_specs=pl.BlockSpec((tr, N_DEV * n), lambda i: (i, 0)),
        out_shape=jax.ShapeDtypeStruct((K, N_DEV * n), g.dtype), compiler_params=_cparams(("parallel",)),
    )(g)


def _pack_cols(name, w):
    K, n = w.shape[0], w.shape[1] // N_DEV
    tr = 256

    def body(w_ref, o_ref):
        for d in range(N_DEV):
            o_ref[d] = w_ref[:, d * n:(d + 1) * n]

    return pl.pallas_call(
        body, name=name, grid=(K // tr,), in_specs=[pl.BlockSpec((tr, N_DEV * n), lambda i: (i, 0))],
        out_specs=pl.BlockSpec((N_DEV, tr, n), lambda i: (0, i, 0)),
        out_shape=jax.ShapeDtypeStruct((N_DEV, K, n), w.dtype), compiler_params=_cparams(("parallel",)),
    )(w)


def _unpack_dn_in(g):
    _, K, n = g.shape
    tr = 256

    def body(g_ref, qkvz_ref, ab_ref):
        ab_ref[...] = jnp.zeros_like(ab_ref)
        for d, c0, c1, tgt, t0 in _dn_in_pieces():
            (qkvz_ref, ab_ref)[tgt][:, t0:t0 + c1 - c0] = g_ref[d, :, c0:c1]

    return pl.pallas_call(
        body, name="unpack_dn_in", grid=(K // tr,), in_specs=[pl.BlockSpec((N_DEV, tr, n), lambda i: (0, i, 0))],
        out_specs=[pl.BlockSpec((tr, DN_QKVZ), lambda i: (i, 0)), pl.BlockSpec((tr, LANE), lambda i: (i, 0))],
        out_shape=[jax.ShapeDtypeStruct((K, DN_QKVZ), g.dtype), jax.ShapeDtypeStruct((K, LANE), g.dtype)],
        compiler_params=_cparams(("parallel",)),
    )(g)


def _pack_dn_in(d_qkvz, d_ab):
    K = d_qkvz.shape[0]
    n = (DN_QKVZ + 2 * DN_HEADS) // N_DEV
    tr = 256

    def body(qkvz_ref, ab_ref, o_ref):
        for d, c0, c1, tgt, t0 in _dn_in_pieces():
            o_ref[d, :, c0:c1] = (qkvz_ref, ab_ref)[tgt][:, t0:t0 + c1 - c0]

    return pl.pallas_call(
        body, name="pack_dn_in", grid=(K // tr,),
        in_specs=[pl.BlockSpec((tr, DN_QKVZ), lambda i: (i, 0)), pl.BlockSpec((tr, LANE), lambda i: (i, 0))],
        out_specs=pl.BlockSpec((N_DEV, tr, n), lambda i: (0, i, 0)),
        out_shape=jax.ShapeDtypeStruct((N_DEV, K, n), d_qkvz.dtype), compiler_params=_cparams(("parallel",)),
    )(d_qkvz, d_ab)


ADAMW_ROWS = 256


def _adamw(name, parts, w, m, v):
    R, C = w.shape
    tr = min(R, ADAMW_ROWS)
    assert R % tr == 0 and parts.shape == (N_DEV, R, C)
    c1 = 1.0 - B1 ** STEP
    c2 = 1.0 - B2 ** STEP

    def body(p_ref, w_ref, m_ref, v_ref, g_ref, d_ref, nm_ref, nv_ref):
        g = p_ref[0].astype(F32)
        for dev in range(1, N_DEV):
            g = g + p_ref[dev].astype(F32)
        nm = B1 * m_ref[...] + (1.0 - B1) * g
        nv = B2 * v_ref[...] + (1.0 - B2) * jnp.square(g)
        g_ref[...] = g
        nm_ref[...] = nm
        nv_ref[...] = nv
        d_ref[...] = -LR * ((nm / c1) / (jnp.sqrt(nv / c2) + ADAM_EPS) + WD * w_ref[...])

    blk = pl.BlockSpec((tr, C), lambda i: (i, 0))
    return pl.pallas_call(
        body, name=name, grid=(R // tr,),
        in_specs=[pl.BlockSpec((N_DEV, tr, C), lambda i: (0, i, 0)), blk, blk, blk],
        out_specs=[blk] * 4, out_shape=[jax.ShapeDtypeStruct((R, C), F32)] * 4,
        compiler_params=_cparams(("parallel",)),
    )(parts, w, m, v)


SMALL = ("mix_norm", "attn_q_gain", "attn_k_gain", "dn_a_log", "dn_dt_bias", "dn_o_gain", "mlp_norm", "ple_norm")
WEIGHTS = ("mix_norm", "attn_w_qkv", "attn_q_gain", "attn_k_gain", "attn_w_o", "dn_w_in", "dn_conv", "dn_a_log",
           "dn_dt_bias", "dn_o_gain", "dn_w_o", "mlp_norm", "w_up", "w_down", "ple_norm", "w_ple", "w_ple_gate")


def _to_rows(flat, multiple):
    n = flat.shape[-1]
    rows = -(-n // (LANE * multiple)) * multiple
    return jnp.pad(flat, [(0, rows * LANE - n)]).reshape(rows, LANE)


def _cols_to_devices(w):
    K, N = w.shape
    return jnp.transpose(w.reshape(K, N_DEV, N // N_DEV), (1, 0, 2))


def _cols_from_devices(g):
    _, K, n = g.shape
    return jnp.transpose(g, (1, 0, 2)).reshape(K, N_DEV * n)


SMALL_ROWS = 96


def _pack_small(vals, loss_rows):
    rows = [_to_rows(vals[n].reshape(-1), SUBLANE) for n in SMALL] + [loss_rows]
    buf = jnp.concatenate(rows, 0)
    assert buf.shape == (SMALL_ROWS, LANE)
    return buf


def _unpack_small(buf, like):
    out, r = {}, 0
    for n in SMALL:
        sz = math.prod(like[n].shape)
        out[n] = buf[r:r + -(-sz // LANE)].reshape(-1)[:sz].reshape(like[n].shape)
        r += -(-sz // (LANE * SUBLANE)) * SUBLANE
    return out


def kernel(x, p, positions, mix_norm, attn_w_qkv, attn_q_gain, attn_k_gain, attn_w_o, dn_w_in, dn_conv, dn_a_log, dn_dt_bias, dn_o_gain, dn_w_o, mlp_norm, w_up, w_down, ple_norm, w_ple, w_ple_gate, loss_target, m_mix_norm, m_attn_w_qkv, m_attn_q_gain, m_attn_k_gain, m_attn_w_o, m_dn_w_in, m_dn_conv, m_dn_a_log, m_dn_dt_bias, m_dn_o_gain, m_dn_w_o, m_mlp_norm, m_w_up, m_w_down, m_ple_norm, m_w_ple, m_w_ple_gate, v_mix_norm, v_attn_w_qkv, v_attn_q_gain, v_attn_k_gain, v_attn_w_o, v_dn_w_in, v_dn_conv, v_dn_a_log, v_dn_dt_bias, v_dn_o_gain, v_dn_w_o, v_mlp_norm, v_w_up, v_w_down, v_ple_norm, v_w_ple, v_w_ple_gate):
    w = dict(mix_norm=mix_norm, attn_w_qkv=attn_w_qkv, attn_q_gain=attn_q_gain, attn_k_gain=attn_k_gain, attn_w_o=attn_w_o,
             dn_w_in=dn_w_in, dn_conv=dn_conv, dn_a_log=dn_a_log, dn_dt_bias=dn_dt_bias, dn_o_gain=dn_o_gain, dn_w_o=dn_w_o,
             mlp_norm=mlp_norm, w_up=w_up, w_down=w_down, ple_norm=ple_norm, w_ple=w_ple, w_ple_gate=w_ple_gate)
    m = dict(mix_norm=m_mix_norm, attn_w_qkv=m_attn_w_qkv, attn_q_gain=m_attn_q_gain, attn_k_gain=m_attn_k_gain,
             attn_w_o=m_attn_w_o, dn_w_in=m_dn_w_in, dn_conv=m_dn_conv, dn_a_log=m_dn_a_log, dn_dt_bias=m_dn_dt_bias,
             dn_o_gain=m_dn_o_gain, dn_w_o=m_dn_w_o, mlp_norm=m_mlp_norm, w_up=m_w_up, w_down=m_w_down,
             ple_norm=m_ple_norm, w_ple=m_w_ple, w_ple_gate=m_w_ple_gate)
    v = dict(mix_norm=v_mix_norm, attn_w_qkv=v_attn_w_qkv, attn_q_gain=v_attn_q_gain, attn_k_gain=v_attn_k_gain,
             attn_w_o=v_attn_w_o, dn_w_in=v_dn_w_in, dn_conv=v_dn_conv, dn_a_log=v_dn_a_log, dn_dt_bias=v_dn_dt_bias,
             dn_o_gain=v_dn_o_gain, dn_w_o=v_dn_w_o, mlp_norm=v_mlp_norm, w_up=v_w_up, w_down=v_w_down,
             ple_norm=v_ple_norm, w_ple=v_w_ple, w_ple_gate=v_w_ple_gate)
    S = x.shape[1]

    bf = lambda a: a.astype(BF16)
    rows_to_devices = lambda t: t.reshape(N_DEV, t.shape[0] // N_DEV, t.shape[1])

    g_qkv, g_ao = _all_gather("gather_attn", [bf(attn_w_qkv[0]), bf(attn_w_o[0])], [(0, None), (1, None)])
    rest_shards = [bf(dn_w_in[0]), bf(dn_w_o[0]), bf(w_up), bf(w_down), bf(w_ple), bf(w_ple_gate), dn_conv[0]]
    rest_streams = [(0, None), (1, None), (2, 0), (2, 1), (3, 0), (3, 1), (4, 0), (4, 1), (5, 0), (5, 1), (6, None)]
    rest_arrived, token = _gather_async("gather_rest", _after(rest_shards, g_ao), rest_streams)
    W = dict(attn_w_qkv=_unpack_cols("unpack_attn_qkv", _after(g_qkv, token)), attn_w_o=_cols_from_devices(g_ao))

    def rest_of_weights(after):
        g_in, g_do, g_up0, g_up1, g_dn0, g_dn1, g_pl0, g_pl1, g_gt0, g_gt1, g_conv = rest_arrived(after)
        rest = dict(
            dn_conv=jnp.transpose(g_conv, (1, 0, 2)).reshape(CONV_W, DN_QKV), dn_w_o=g_do.reshape(DN_WIDTH, D_MODEL),
            w_up=[_cols_from_devices(g_up0), _cols_from_devices(g_up1)],
            w_down=[g_dn0.reshape(D_FF, D_MODEL), g_dn1.reshape(D_FF, D_MODEL)],
            w_ple=[_cols_from_devices(g_pl0), _cols_from_devices(g_pl1)],
            w_ple_gate=[g_gt0.reshape(D_MODEL, D_MODEL), g_gt1.reshape(D_MODEL, D_MODEL)])
        rest["dn_w_qkvz"], rest["dn_w_ab"] = _unpack_dn_in(g_in)
        return rest

    pending = {}

    def mlp_sends(g):
        return [_cols_to_devices(g["w_up"]), rows_to_devices(g["w_down"]), _cols_to_devices(g["w_ple"]),
                rows_to_devices(g["w_ple_gate"])]

    def start(tag, sends):
        pending[tag], token = _exchange_async(f"exchange_{tag}", sends)
        return token

    def send_layer1(g):
        conv_send = jnp.transpose(g["dn_conv"].reshape(CONV_W, N_DEV, DN_QKV // N_DEV), (1, 0, 2))
        return start("layer1", [_pack_dn_in(g["dn_w_qkvz"], g["dn_w_ab"]), conv_send, rows_to_devices(g["dn_w_o"])] + mlp_sends(g))

    def send_mlp0(g):
        return start("mlp0", mlp_sends(g))

    def send_attn(g):
        return start("attn", [_pack_cols("pack_attn_qkv", g["attn_w_qkv"]), _cols_to_devices(g["attn_w_o"])])

    P = dict(mix_norm=mix_norm, attn_q_gain=attn_q_gain[0], attn_k_gain=attn_k_gain[0], dn_a_log=dn_a_log[0],
             dn_dt_bias=dn_dt_bias[0], dn_o_gain=dn_o_gain[0], mlp_norm=mlp_norm, ple_norm=ple_norm)

    sq, dx0, small_g = _local_step(x[0], p[:, 0], positions.reshape(S, 1), loss_target[0], W, P,
                                   rest_of_weights, send_layer1, send_mlp0, send_attn)

    r_in, r_conv, r_do, r_up1, r_dn1, r_pl1, r_gt1 = pending["layer1"](dx0)
    r_up0, r_dn0, r_pl0, r_gt0 = pending["mlp0"](dx0)
    r_qkv, r_ao = pending["attn"](dx0)
    big = {}
    for n, parts in (("attn_w_qkv", [r_qkv]), ("attn_w_o", [r_ao]), ("dn_w_in", [r_in]), ("dn_conv", [r_conv]),
                     ("dn_w_o", [r_do]), ("w_up", [r_up0, r_up1]), ("w_down", [r_dn0, r_dn1]),
                     ("w_ple", [r_pl0, r_pl1]), ("w_ple_gate", [r_gt0, r_gt1])):
        layers = [_adamw(f"adamw_{n}{l}", pt, w[n][l], m[n][l], v[n][l]) for l, pt in enumerate(parts)]
        big[n] = [jnp.stack([res[k] for res in layers]) for k in range(4)]

    loss_rows = jnp.pad((0.5 / D_MODEL) * jnp.sum(sq, axis=1, keepdims=True), ((0, SUBLANE - 1), (0, LANE - 1)))
    small_like = {n: w[n] for n in SMALL}
    parts_s = _all_gather("gather_small", [_pack_small(small_g, loss_rows)], [(0, None)])[0]
    zero_rows = jnp.zeros((SUBLANE, LANE), F32)
    small = _adamw("adamw_small", parts_s, _pack_small(w, zero_rows), _pack_small(m, zero_rows), _pack_small(v, zero_rows))
    loss = small[0][SMALL_ROWS - SUBLANE, 0]
    small = [_unpack_small(b, small_like) for b in small]

    outs = [loss, dx0[None]]
    for k in range(4):
        for n in WEIGHTS:
            outs.append(small[k][n] if n in SMALL else big[n][k])
    return tuple(outs)
```

```python
import functools
import math

import jax
import jax.numpy as jnp
from jax import lax
from jax.experimental import pallas as pl
from jax.experimental.pallas import tpu as pltpu

F32 = jnp.float32
BF16 = jnp.bfloat16
HIGHEST = lax.Precision.HIGHEST

N_DEV = 8
D_MODEL = 1024
EPS = 1e-6
SWA_GROUPS = ((128, 1), (512, 4), (2048, 16))
A_HEADS = 8
A_HEAD_DIM = 64
A_WIDTH = A_HEADS * A_HEAD_DIM
A_QKV = 3 * 3 * A_WIDTH
ROPE_DIM = 16
ROPE_HALF = 8
ROPE_THETA = 500000.0
BAND = 128
DN_HEADS = 8
DN_DIM = 128
DN_WIDTH = DN_HEADS * DN_DIM
CONV_W = 4
CHUNK = 64
D_FF = 4 * D_MODEL
PLE_DIM = 256
LR, B1, B2, ADAM_EPS, WD, STEP = 0.001, 0.9, 0.999, 1e-08, 0.01, 10

VMEM_LIMIT = 56 * 1024 * 1024
MXU_TILE = 1024
LANE = 128
SUBLANE = 8


def _cparams(sem):
    return pltpu.CompilerParams(dimension_semantics=sem, vmem_limit_bytes=VMEM_LIMIT)


def _tile(n, pref):
    if n <= pref:
        return n
    t = (pref // LANE) * LANE
    while t >= LANE:
        if n % t == 0:
            return t
        t -= LANE
    raise ValueError(f"no tile for {n}")


def _dot(a, b, ca=1, cb=0, precision=None):
    return lax.dot_general(a, b, (((ca,), (cb,)), ((), ())), precision=precision,
                           preferred_element_type=F32)


def _bdot(a, b, ca=1, cb=0):
    return _dot(a.astype(BF16), b.astype(BF16), ca, cb)


def _mm(name, a, b, *, ta=False, tb=False, epilogue=None, extras=(), out_dtypes=(F32,),
        tm_pref=MXU_TILE, tn_pref=1536, tk_pref=MXU_TILE):
    M, K = (a.shape[1], a.shape[0]) if ta else a.shape
    N = b.shape[0] if tb else b.shape[1]
    assert (b.shape[1] if tb else b.shape[0]) == K
    tm, tn, tk = _tile(M, tm_pref), _tile(N, tn_pref), _tile(K, tk_pref)
    nk = K // tk
    n_out = len(out_dtypes)
    n_ext = len(extras)

    def body(*refs):
        a_ref, b_ref = refs[0], refs[1]
        ext = refs[2:2 + n_ext]
        outs = refs[2 + n_ext:2 + n_ext + n_out]
        k = pl.program_id(2)
        prod = _bdot(a_ref[...], b_ref[...], 0 if ta else 1, 1 if tb else 0)

        def finish(r):
            res = (r,) if epilogue is None else epilogue(r, *[e[...] for e in ext])
            for o, v in zip(outs, res):
                o[...] = v.astype(o.dtype)

        if nk == 1:
            finish(prod)
            return
        acc = refs[-1]

        @pl.when(k == 0)
        def _():
            acc[...] = jnp.zeros_like(acc)

        acc[...] += prod

        @pl.when(k == nk - 1)
        def _():
            finish(acc[...])

    a_spec = pl.BlockSpec((tk, tm), lambda i, j, k: (k, i)) if ta else pl.BlockSpec((tm, tk), lambda i, j, k: (i, k))
    b_spec = pl.BlockSpec((tn, tk), lambda i, j, k: (j, k)) if tb else pl.BlockSpec((tk, tn), lambda i, j, k: (k, j))
    ext_specs = []
    for e in extras:
        if e.shape[0] == 1 and M != 1:
            ext_specs.append(pl.BlockSpec((1, tn), lambda i, j, k: (0, j)))
        else:
            ext_specs.append(pl.BlockSpec((tm, tn), lambda i, j, k: (i, j)))
    out = pl.pallas_call(
        body, name=name,
        grid=(M // tm, N // tn, nk),
        in_specs=[a_spec, b_spec] + ext_specs,
        out_specs=[pl.BlockSpec((tm, tn), lambda i, j, k: (i, j)) for _ in range(n_out)],
        out_shape=[jax.ShapeDtypeStruct((M, N), dt) for dt in out_dtypes],
        scratch_shapes=[pltpu.VMEM((tm, tn), F32)] if nk > 1 else [],
        compiler_params=_cparams(("parallel", "parallel", "arbitrary")),
    )(a, b, *extras)
    return out[0] if n_out == 1 else tuple(out)


def _rows(name, fn, ins, outs, *, tr, accs=()):
    ins = [(e[0], e[1]) + (e[2] if len(e) > 2 else (0, e[0].shape[-1])) for e in ins]
    n_rows = next(e[0].shape[0] for e in ins if e[1] == "row")
    assert n_rows % tr == 0 and tr % SUBLANE == 0
    steps = n_rows // tr
    t8 = tr // SUBLANE
    n8 = n_rows // SUBLANE
    n_in, n_out, n_acc = len(ins), len(outs), len(accs)

    def body(*refs):
        i = pl.program_id(0)
        vals = fn(i, steps, *[r[...] for r in refs[:n_in]])
        if not isinstance(vals, (tuple, list)):
            vals = (vals,)
        assert len(vals) == n_out + n_acc
        for o, v in zip(refs[n_in:n_in + n_out], vals[:n_out]):
            o[...] = v.astype(o.dtype)
        if n_acc:
            acc_refs = refs[n_in + n_out:]

            @pl.when(i == 0)
            def _():
                for r in acc_refs:
                    r[...] = jnp.zeros_like(r)

            for r, v in zip(acc_refs, vals[n_out:]):
                r[...] += v.astype(r.dtype)

    in_specs = []
    for a, kind, cb, c in ins:
        if kind == "row":
            in_specs.append(pl.BlockSpec((tr, c), lambda i, cb=cb: (i, cb)))
        elif kind == "full":
            in_specs.append(pl.BlockSpec(a.shape, lambda i, z=(0,) * a.ndim: z))
        elif kind == "prev8":
            in_specs.append(pl.BlockSpec((SUBLANE, c), lambda i, cb=cb: (jnp.maximum(i * t8 - 1, 0), cb)))
        elif kind == "next8":
            in_specs.append(pl.BlockSpec((SUBLANE, c), lambda i, cb=cb: (jnp.minimum((i + 1) * t8, n8 - 1), cb)))
        else:
            raise ValueError(kind)
    out_specs = [pl.BlockSpec((tr, c), lambda i: (i, 0)) for c, _ in outs]
    out_specs += [pl.BlockSpec(s, lambda i, z=(0,) * len(s): z) for s, _ in accs]
    out_shape = [jax.ShapeDtypeStruct((n_rows, c), dt) for c, dt in outs]
    out_shape += [jax.ShapeDtypeStruct(s, dt) for s, dt in accs]
    res = pl.pallas_call(
        body, name=name, grid=(steps,), in_specs=in_specs, out_specs=out_specs, out_shape=out_shape,
        compiler_params=_cparams(("arbitrary",) if n_acc else ("parallel",)),
    )(*[e[0] for e in ins])
    return res[0] if len(res) == 1 else tuple(res)


def _colsum(x):
    return jnp.sum(x, axis=0, keepdims=True)


def _sum_all(x):
    return jnp.sum(jnp.sum(x, axis=1, keepdims=True), axis=0, keepdims=True)


def _rmsnorm_fwd(name, x, gain):
    def fn(i, n, xt, g):
        r = lax.rsqrt(jnp.mean(xt * xt, axis=-1, keepdims=True) + EPS)
        return (xt * r * g,)
    return _rows(name, fn, [(x, "row"), (gain, "full")], [(x.shape[1], BF16)], tr=512)


def _rmsnorm_bwd(name, x, gain, dh, dres):
    def fn(i, n, xt, g, dht, drt):
        r = lax.rsqrt(jnp.mean(xt * xt, axis=-1, keepdims=True) + EPS)
        xh = xt * r
        dxn = dht * g
        dx = r * (dxn - xh * jnp.mean(dxn * xh, axis=-1, keepdims=True))
        return drt + dx, _colsum(dht * xh)
    D = x.shape[1]
    return _rows(name, fn, [(x, "row"), (gain, "full"), (dh, "row"), (dres, "row")], [(D, F32)],
                 tr=256, accs=[((1, D), F32)])


def _head_consts():
    import numpy as np
    e = np.arange(A_WIDTH) % A_HEAD_DIM
    inv = (np.float32(ROPE_THETA) ** (-np.arange(0, ROPE_DIM, 2, dtype=np.float32) / np.float32(ROPE_DIM))).astype(np.float32)
    c = np.zeros((8, A_WIDTH), np.float32)
    c[0] = np.where(e < ROPE_DIM, inv[e % ROPE_HALF], 0.0)
    c[1] = np.where(e < ROPE_HALF, -1.0, np.where(e < ROPE_DIM, 1.0, 0.0))
    c[2] = (e < ROPE_HALF).astype(np.float32)
    c[3] = (e < ROPE_DIM).astype(np.float32)
    return jnp.asarray(c)


def _block_diag(scale):
    import numpy as np
    h = np.arange(A_WIDTH) // A_HEAD_DIM
    return jnp.asarray((h[:, None] == h[None, :]).astype(np.float32) * scale, dtype=BF16)


def _seg_sum(x, bd):
    hi = x.astype(BF16)
    lo = (x - hi.astype(F32)).astype(BF16)
    return _dot(hi, bd) + _dot(lo, bd)


def _rope_tables(positions, consts):
    def fn(i, n, pos, c):
        ang = pos.astype(F32) * c[0:1, :]
        return jnp.cos(ang), jnp.sin(ang) * c[1:2, :]
    return _rows("rope_tables", fn, [(positions, "row"), (consts, "full")],
                 [(A_WIDTH, F32), (A_WIDTH, F32)], tr=512)


def _rope_apply(y, ct, st, low):
    rolled = jnp.where(low, pltpu.roll(y, A_WIDTH - ROPE_HALF, 1), pltpu.roll(y, ROPE_HALF, 1))
    return y * ct + rolled * st


def _rope_apply_bwd(dout, ct, st, low, in16):
    t = dout * st
    back = jnp.where(low, pltpu.roll(t, A_WIDTH - ROPE_HALF, 1), jnp.where(in16, pltpu.roll(t, ROPE_HALF, 1), 0.0))
    return dout * ct + back


def _attn_prep(qkv, gains, ct, st, consts, bd):
    def fn(i, n, t, g, c_t, s_t, c, b):
        low = c[2:3, :] > 0.5
        groups = []
        for grp in range(3):
            cols = []
            for which in range(3):
                off = (grp * 3 + which) * A_WIDTH
                x = t[:, off:off + A_WIDTH]
                if which == 2:
                    cols.append(x.astype(BF16))
                    continue
                r = lax.rsqrt(_seg_sum(x * x, b) + EPS)
                y = x * r * g[grp * 2 + which:grp * 2 + which + 1, :]
                cols.append(_rope_apply(y, c_t, s_t, low).astype(BF16))
            groups.append(jnp.concatenate(cols, axis=1))
        return tuple(groups)
    return _rows("attn_prep", fn, [(qkv, "row"), (gains, "full"), (ct, "row"), (st, "row"), (consts, "full"), (bd, "full")],
                 [(3 * A_WIDTH, BF16)] * 3, tr=256)


def _band_mask(n):
    row = lax.broadcasted_iota(jnp.int32, (BAND, 2 * BAND), 0)
    col = lax.broadcasted_iota(jnp.int32, (BAND, 2 * BAND), 1)
    dist = row + BAND - col
    return (dist >= 0) & (dist <= BAND) & ((col >= BAND) | (n > 0))


def _attn_fwd(qkvn, grp):
    S = qkvn.shape[0]
    d = SWA_GROUPS[grp][1]
    L = S // d
    nblk = L // BAND
    assert L % BAND == 0
    view = qkvn.reshape(L, d * 3 * A_WIDTH)

    def body(q_ref, kc_ref, kp_ref, vc_ref, vp_ref, o_ref, lse_ref):
        n = pl.program_id(1)
        valid = _band_mask(n)
        first = lax.broadcasted_iota(jnp.int32, (BAND, LANE), 1) < A_HEAD_DIM
        pairs = [slice(pr * LANE, (pr + 1) * LANE) for pr in range(A_WIDTH // LANE)]
        halves = (first, jnp.logical_not(first))
        qps = [q_ref[:, sl] for sl in pairs]
        kcats = [jnp.concatenate([kp_ref[:, sl], kc_ref[:, sl]], axis=0) for sl in pairs]
        vcats = [jnp.concatenate([vp_ref[:, sl], vc_ref[:, sl]], axis=0) for sl in pairs]
        heads = [(pr, m) for pr in range(len(pairs)) for m in halves]
        ss = [_dot(jnp.where(m, qps[pr], jnp.zeros_like(qps[pr])), kcats[pr], 1, 1) for pr, m in heads]
        ps, lses = [], []
        for s in ss:
            s = jnp.where(valid, s * (A_HEAD_DIM ** -0.5), -1e30)
            mx = jnp.max(s, axis=-1, keepdims=True)
            e = jnp.exp(s - mx)
            l = jnp.sum(e, axis=-1, keepdims=True)
            ps.append((e / l).astype(BF16))
            lses.append(mx + jnp.log(l))
        os_ = [_dot(p, vcats[pr]) for p, (pr, _) in zip(ps, heads)]
        o_ref[...] = jnp.concatenate([jnp.where(first, os_[2 * pr], os_[2 * pr + 1]) for pr in range(len(pairs))], axis=1)
        lse_ref[...] = jnp.concatenate([jnp.where(first, lses[2 * pr], lses[2 * pr + 1]) for pr in range(len(pairs))], axis=1)

    blk = (BAND, A_WIDTH)
    o, lse = pl.pallas_call(
        body, name=f"attn_fwd_g{grp}", grid=(d, nblk),
        in_specs=[pl.BlockSpec(blk, lambda r, n: (n, r * 3)),
                  pl.BlockSpec(blk, lambda r, n: (n, r * 3 + 1)),
                  pl.BlockSpec(blk, lambda r, n: (jnp.maximum(n - 1, 0), r * 3 + 1)),
                  pl.BlockSpec(blk, lambda r, n: (n, r * 3 + 2)),
                  pl.BlockSpec(blk, lambda r, n: (jnp.maximum(n - 1, 0), r * 3 + 2))],
        out_specs=[pl.BlockSpec(blk, lambda r, n: (n, r)), pl.BlockSpec(blk, lambda r, n: (n, r))],
        out_shape=[jax.ShapeDtypeStruct((L, d * A_WIDTH), F32)] * 2,
        compiler_params=_cparams(("parallel", "parallel")),
    )(view, view, view, view, view)
    return o.reshape(S, A_WIDTH), lse.reshape(S, A_WIDTH)


def _merge_weights(l0, l1, l2):
    mx = jnp.maximum(jnp.maximum(l0, l1), l2)
    e0, e1, e2 = jnp.exp(l0 - mx), jnp.exp(l1 - mx), jnp.exp(l2 - mx)
    inv = 1.0 / (e0 + e1 + e2)
    return e0 * inv, e1 * inv, e2 * inv


def _attn_merge(os_, lses):
    def fn(i, n, o0, o1, o2, l0, l1, l2):
        w0, w1, w2 = _merge_weights(l0, l1, l2)
        return (w0 * o0 + w1 * o1 + w2 * o2,)
    ins = [(a, "row") for a in (*os_, *lses)]
    return _rows("attn_merge", fn, ins, [(A_WIDTH, BF16)], tr=512)


def _attn_merge_bwd(do, os_, lses, bd1):
    def fn(i, n, dot_, o0, o1, o2, l0, l1, l2, b):
        w0, w1, w2 = _merge_weights(l0, l1, l2)
        o = w0 * o0 + w1 * o1 + w2 * o2
        dsum = _seg_sum(dot_ * o, b)
        return (w0 * dot_, w1 * dot_, w2 * dot_, -w0 * dsum, -w1 * dsum, -w2 * dsum)
    ins = [(do, "row")] + [(a, "row") for a in (*os_, *lses)] + [(bd1, "full")]
    res = _rows("attn_merge_bwd", fn, ins, [(A_WIDTH, BF16)] * 3 + [(A_WIDTH, F32)] * 3, tr=256)
    return res[:3], res[3:]


def _lane_pick(x, lane_idx, lane):
    return jnp.sum(jnp.where(lane_idx == lane, x, 0.0), axis=-1, keepdims=True)


def _attn_bwd(qkvn, grp, do_g, lse, c_g):
    S = qkvn.shape[0]
    d = SWA_GROUPS[grp][1]
    L = S // d
    nblk = L // BAND
    view = qkvn.reshape(L, d * 3 * A_WIDTH)
    dov, lsev, cv = (t.reshape(L, d * A_WIDTH) for t in (do_g, lse, c_g))

    def body(q_ref, kc_ref, kp_ref, vc_ref, vp_ref, do_ref, lse_ref, c_ref, dq_ref, dk_ref, dv_ref, ck, cv_):
        n = pl.program_id(1)

        @pl.when(n == 0)
        def _():
            ck[...] = jnp.zeros_like(ck)
            cv_[...] = jnp.zeros_like(cv_)

        @pl.when(n < nblk)
        def _():
            valid = _band_mask(n)
            lane = lax.broadcasted_iota(jnp.int32, (BAND, LANE), 1)
            first = lane < A_HEAD_DIM
            lane2 = lax.broadcasted_iota(jnp.int32, (2 * BAND, LANE), 1) < A_HEAD_DIM
            pairs = [slice(pr * LANE, (pr + 1) * LANE) for pr in range(A_WIDTH // LANE)]
            halves = (first, jnp.logical_not(first))
            qps = [q_ref[:, sl] for sl in pairs]
            dops = [do_ref[:, sl] for sl in pairs]
            kcats = [jnp.concatenate([kp_ref[:, sl], kc_ref[:, sl]], axis=0) for sl in pairs]
            vcats = [jnp.concatenate([vp_ref[:, sl], vc_ref[:, sl]], axis=0) for sl in pairs]
            heads = [(pr, hh) for pr in range(len(pairs)) for hh in range(2)]
            zero = jnp.zeros_like(qps[0])
            ss = [_dot(jnp.where(halves[hh], qps[pr], zero), kcats[pr], 1, 1) for pr, hh in heads]
            dps = [_dot(jnp.where(halves[hh], dops[pr], zero), vcats[pr], 1, 1) for pr, hh in heads]
            dss, pbs = [], []
            for (pr, hh), s, dp in zip(heads, ss, dps):
                lse_h = _lane_pick(lse_ref[:, pairs[pr]], lane, hh * A_HEAD_DIM)
                c_h = _lane_pick(c_ref[:, pairs[pr]], lane, hh * A_HEAD_DIM)
                p = jnp.where(valid, jnp.exp(s * (A_HEAD_DIM ** -0.5) - lse_h), 0.0)
                dss.append((p * (dp + c_h) * (A_HEAD_DIM ** -0.5)).astype(BF16))
                pbs.append(p.astype(BF16))
            dqs = [_dot(ds, kcats[pr]) for ds, (pr, _) in zip(dss, heads)]
            dks = [_dot(ds, qps[pr], 0, 0) for ds, (pr, _) in zip(dss, heads)]
            dvs = [_dot(pb, dops[pr], 0, 0) for pb, (pr, _) in zip(pbs, heads)]
            for pr, sl in enumerate(pairs):
                dq_ref[:, sl] = jnp.where(first, dqs[2 * pr], dqs[2 * pr + 1])
                dkc = jnp.where(lane2, dks[2 * pr], dks[2 * pr + 1])
                dvc = jnp.where(lane2, dvs[2 * pr], dvs[2 * pr + 1])
                dk_ref[:, sl] = ck[:, sl] + dkc[:BAND]
                dv_ref[:, sl] = cv_[:, sl] + dvc[:BAND]
                ck[:, sl] = dkc[BAND:]
                cv_[:, sl] = dvc[BAND:]

        @pl.when(n == nblk)
        def _():
            dk_ref[...] = ck[...]
            dv_ref[...] = cv_[...]

    blk = (BAND, A_WIDTH)
    last = nblk - 1
    qn = lambda n: jnp.minimum(n, last)
    pn = lambda n: jnp.clip(n - 1, 0, last)
    dq, dk, dv = pl.pallas_call(
        body, name=f"attn_bwd_g{grp}", grid=(d, nblk + 1),
        in_specs=[pl.BlockSpec(blk, lambda r, n: (qn(n), r * 3)),
                  pl.BlockSpec(blk, lambda r, n: (qn(n), r * 3 + 1)),
                  pl.BlockSpec(blk, lambda r, n: (pn(n), r * 3 + 1)),
                  pl.BlockSpec(blk, lambda r, n: (qn(n), r * 3 + 2)),
                  pl.BlockSpec(blk, lambda r, n: (pn(n), r * 3 + 2)),
                  pl.BlockSpec(blk, lambda r, n: (qn(n), r)),
                  pl.BlockSpec(blk, lambda r, n: (qn(n), r)),
                  pl.BlockSpec(blk, lambda r, n: (qn(n), r))],
        out_specs=[pl.BlockSpec(blk, lambda r, n: (qn(n), r)),
                   pl.BlockSpec(blk, lambda r, n: (pn(n), r)),
                   pl.BlockSpec(blk, lambda r, n: (pn(n), r))],
        out_shape=[jax.ShapeDtypeStruct((L, d * A_WIDTH), F32)] * 3,
        scratch_shapes=[pltpu.VMEM(blk, F32), pltpu.VMEM(blk, F32)],
        compiler_params=_cparams(("parallel", "arbitrary")),
    )(view, view, view, view, view, dov, lsev, cv)
    return tuple(t.reshape(S, A_WIDTH) for t in (dq, dk, dv))


def _attn_prep_bwd(qkv, grads, gains, ct, st, consts, bd):
    def fn(i, n, t, g, c_t, s_t, c, b, *gr):
        low = c[2:3, :] > 0.5
        in16 = c[3:4, :] > 0.5
        cols, dgs = [], []
        for grp in range(3):
            for which in range(3):
                dout = gr[grp * 3 + which]
                if which == 2:
                    cols.append(dout.astype(BF16))
                    continue
                off = (grp * 3 + which) * A_WIDTH
                x = t[:, off:off + A_WIDTH]
                gain = g[grp * 2 + which:grp * 2 + which + 1, :]
                r = lax.rsqrt(_seg_sum(x * x, b) + EPS)
                xh = x * r
                dy = _rope_apply_bwd(dout, c_t, s_t, low, in16)
                dyn = dy * gain
                dx = r * (dyn - xh * _seg_sum(dyn * xh, b))
                cols.append(dx.astype(BF16))
                dgs.append(_colsum(dy * xh))
        return (jnp.concatenate(cols, axis=1), *dgs)
    ins = [(qkv, "row"), (gains, "full"), (ct, "row"), (st, "row"), (consts, "full"), (bd, "full")] + [(a, "row") for a in grads]
    res = _rows("attn_prep_bwd", fn, ins, [(A_QKV, BF16)], tr=128, accs=[((1, A_WIDTH), F32)] * 6)
    return res[0], res[1:]


DN_QKV = 3 * DN_WIDTH
DN_QKVZ = DN_QKV + DN_WIDTH


def _sigmoid(x):
    return 1.0 / (1.0 + jnp.exp(-x))


def _softplus(x):
    return jnp.maximum(x, 0.0) + jnp.log(1.0 + jnp.exp(-jnp.abs(x)))


def _conv_taps(xs, w, tr):
    acc = None
    for j in range(CONV_W):
        sh = CONV_W - 1 - j
        term = (pltpu.roll(xs, sh, 0) if sh else xs)[SUBLANE:] * w[j:j + 1, :]
        acc = term if acc is None else acc + term
    return acc


def _dn_prep(qkvz, ab, convw, alog_row, dt_row):
    tr = 256

    def fn(i, n, x, xp, abt, w, al, dt):
        xp = jnp.where(i > 0, xp, 0.0)
        u = _conv_taps(jnp.concatenate([xp, x], axis=0), w, tr)
        y = u * _sigmoid(u)
        qs, ks = [], []
        for h in range(DN_HEADS):
            for dst, base, sc in ((qs, 0, DN_DIM ** -0.5), (ks, DN_WIDTH, 1.0)):
                seg = y[:, base + h * DN_DIM:base + (h + 1) * DN_DIM]
                dst.append(seg * (lax.rsqrt(jnp.sum(seg * seg, axis=-1, keepdims=True) + EPS) * sc))
        lane = lax.broadcasted_iota(jnp.int32, abt.shape, 1)
        g = -jnp.exp(al) * _softplus(abt + dt)
        gb = jnp.where(lane < DN_HEADS, g, jnp.where(lane < 2 * DN_HEADS, _sigmoid(abt), 0.0))
        return u, jnp.concatenate(qs, axis=1), jnp.concatenate(ks, axis=1), y[:, 2 * DN_WIDTH:], gb

    ins = [(qkvz, "row", (0, DN_QKV)), (qkvz, "prev8", (0, DN_QKV)), (ab, "row"), (convw, "full"),
           (alog_row, "full"), (dt_row, "full")]
    return _rows("dn_prep", fn, ins, [(DN_QKV, F32), (DN_WIDTH, F32), (DN_WIDTH, F32), (DN_WIDTH, F32), (LANE, F32)], tr=tr)


def _tri_masks():
    row = lax.broadcasted_iota(jnp.int32, (CHUNK, CHUNK), 0)
    col = lax.broadcasted_iota(jnp.int32, (CHUNK, CHUNK), 1)
    return row >= col, row > col, row == col


def _heads(fn, *lists):
    return [fn(*xs) for xs in zip(*lists)]


def _split(x):
    hi = x.astype(BF16)
    return hi, (x - hi.astype(F32)).astype(BF16)


def _dot3(a, b, ca=1, cb=0):
    (ah, al), (bh, bl) = a, b
    return _dot(ah, bh, ca, cb) + (_dot(ah, bl, ca, cb) + _dot(al, bh, ca, cb))


def _unit_lower_inverse(a_list, eye):
    ts = [eye - a for a in a_list]
    parts = [_split(a) for a in a_list]
    for _ in range(5):
        parts = [_split(_dot3(p, p)) for p in parts]
        ts = [t + _dot3(_split(t), p) for t, p in zip(ts, parts)]
    return ts


def _dn_terms(qs, ks, vs, gb):
    lower, strict, diag = _tri_masks()
    lane = lax.broadcasted_iota(jnp.int32, (CHUNK, LANE), 1)
    is_last = lax.broadcasted_iota(jnp.int32, (CHUNK, 1), 0) == CHUNK - 1
    hs = range(DN_HEADS)
    gc = _dot(lower.astype(F32), gb, precision=HIGHEST)
    gct = jnp.transpose(gc)
    bcol = [_lane_pick(gb, lane, DN_HEADS + h) for h in hs]
    gcol = [_lane_pick(gc, lane, h) for h in hs]
    glast = [jnp.sum(jnp.where(is_last, g, 0.0), axis=0, keepdims=True) for g in gcol]
    decay = [jnp.exp(jnp.where(lower, gcol[h] - gct[h:h + 1, :], -1e30)) for h in hs]
    kb = _heads(lambda k, b: k * b, ks, bcol)
    kk = _heads(lambda x, k: _bdot(x, k, 1, 1), kb, ks)
    qk = _heads(lambda q, k: _bdot(q, k, 1, 1), qs, ks)
    a = _heads(lambda x, d: jnp.where(strict, x * d, 0.0), kk, decay)
    t = [_split(x) for x in _unit_lower_inverse(a, diag.astype(F32))]
    eg = [jnp.exp(g) for g in gcol]
    egl = _heads(lambda gl, g: jnp.exp(gl - g), glast, gcol)
    rhs_w = _heads(lambda x, e: x * e, kb, eg)
    u = _heads(lambda tt, v, b: _dot3(tt, _split(v * b)), t, vs, bcol)
    w = _heads(lambda tt, r: _dot3(tt, _split(r)), t, rhs_w)
    return dict(bcol=bcol, decay=decay, kb=kb, a=a, t=t, eg=eg, egl=egl, rhs_w=rhs_w, u=u, w=w,
                attn=_heads(lambda x, d: x * d, qk, decay), q_dec=_heads(lambda q, e: q * e, qs, eg),
                k_dec=_heads(lambda k, e: k * e, ks, egl), c_dec=[jnp.exp(g) for g in glast],
                lower=lower, strict=strict, lane=lane, is_last=is_last)


def _head_slices(ref):
    return [ref[:, h * DN_DIM:(h + 1) * DN_DIM] for h in range(DN_HEADS)]


def _dn_chunk_fwd(q, k, v, gb):
    S = q.shape[0]
    N = S // CHUNK

    def body(q_ref, k_ref, v_ref, gb_ref, o_ref, st_ref, state):
        @pl.when(pl.program_id(0) == 0)
        def _():
            state[...] = jnp.zeros_like(state)

        f = _dn_terms(_head_slices(q_ref), _head_slices(k_ref), _head_slices(v_ref), gb_ref[...])
        s = [state[h] for h in range(DN_HEADS)]
        for h in range(DN_HEADS):
            st_ref[0, h] = s[h]
        sb = [x.astype(BF16) for x in s]
        v_new = _heads(lambda u, w, x: u - _bdot(w, x), f["u"], f["w"], sb)
        o = _heads(lambda qd, x, at, vn: _bdot(qd, x) + _bdot(at, vn), f["q_dec"], sb, f["attn"], v_new)
        new_s = _heads(lambda x, c, kd, vn: x * c + _bdot(kd, vn, 0, 0), s, f["c_dec"], f["k_dec"], v_new)
        for h in range(DN_HEADS):
            o_ref[:, h * DN_DIM:(h + 1) * DN_DIM] = o[h]
            state[h] = new_s[h]

    blk = pl.BlockSpec((CHUNK, DN_WIDTH), lambda n: (n, 0))
    st_blk = pl.BlockSpec((1, DN_HEADS, DN_DIM, DN_DIM), lambda n: (n, 0, 0, 0))
    return pl.pallas_call(
        body, name="dn_chunk_fwd", grid=(N,),
        in_specs=[blk, blk, blk, pl.BlockSpec((CHUNK, LANE), lambda n: (n, 0))],
        out_specs=[blk, st_blk],
        out_shape=[jax.ShapeDtypeStruct((S, DN_WIDTH), F32), jax.ShapeDtypeStruct((N, DN_HEADS, DN_DIM, DN_DIM), F32)],
        scratch_shapes=[pltpu.VMEM((DN_HEADS, DN_DIM, DN_DIM), F32)],
        compiler_params=_cparams(("arbitrary",)),
    )(q, k, v, gb)


def _dn_chunk_bwd(q, k, v, gb, states, do):
    S = q.shape[0]
    N = S // CHUNK

    def body(q_ref, k_ref, v_ref, gb_ref, st_ref, do_ref, dq_ref, dk_ref, dv_ref, dgb_ref, dstate):
        @pl.when(pl.program_id(0) == 0)
        def _():
            dstate[...] = jnp.zeros_like(dstate)

        hs = range(DN_HEADS)
        qs, ks, vs, dos = (_head_slices(r) for r in (q_ref, k_ref, v_ref, do_ref))
        f = _dn_terms(qs, ks, vs, gb_ref[...])
        lane, is_last = f["lane"], f["is_last"]
        rowsum = lambda x: jnp.sum(x, axis=-1, keepdims=True)
        s = [st_ref[0, h] for h in hs]
        dsn = [dstate[h] for h in hs]
        sb = [x.astype(BF16) for x in s]
        dsb = [x.astype(BF16) for x in dsn]
        dob = [x.astype(BF16) for x in dos]
        v_new = _heads(lambda u, w, x: u - _bdot(w, x), f["u"], f["w"], sb)
        dv_new = _heads(lambda at, d, kd, x: _bdot(at, d, 0, 0) + _bdot(kd, x), f["attn"], dob, f["k_dec"], dsb)
        dattn = _heads(lambda d, vn: _bdot(d, vn, 1, 1), dob, v_new)
        dq_dec = _heads(lambda d, x: _bdot(d, x, 1, 1), dob, sb)
        dk_dec = _heads(lambda vn, x: _bdot(vn, x, 1, 1), v_new, dsb)
        dw = _heads(lambda dv_, x: -_bdot(dv_, x, 1, 1), dv_new, sb)
        new_ds = _heads(lambda x, c, qd, d, w, dv_: x * c + _bdot(qd, d, 0, 0) - _bdot(w, dv_, 0, 0),
                        dsn, f["c_dec"], f["q_dec"], dob, f["w"], dv_new)
        for h in hs:
            dstate[h] = new_ds[h]
        drhs_u = _heads(lambda tt, x: _dot3(tt, _split(x), 0, 0), f["t"], dv_new)
        drhs_w = _heads(lambda tt, x: _dot3(tt, _split(x), 0, 0), f["t"], dw)
        da = _heads(lambda du_, u, dw_, w: jnp.where(f["strict"], -(_bdot(du_, u, 1, 1) + _bdot(dw_, w, 1, 1)), 0.0),
                    drhs_u, f["u"], drhs_w, f["w"])
        dkk = _heads(lambda x, d: x * d, da, f["decay"])
        dqk = _heads(lambda x, d: x * d, dattn, f["decay"])
        dkb = _heads(lambda x, k_, dw_, e: _bdot(x, k_) + dw_ * e, dkk, ks, drhs_w, f["eg"])
        dq = _heads(lambda x, k_, dqd, e: _bdot(x, k_) + dqd * e, dqk, ks, dq_dec, f["eg"])
        dk = _heads(lambda x, kb_, y, q_, dkd, el, dkb_, b: _bdot(x, kb_, 0, 0) + _bdot(y, q_, 0, 0) + dkd * el + dkb_ * b,
                    dkk, f["kb"], dqk, qs, dk_dec, f["egl"], dkb, f["bcol"])
        m = _heads(lambda x, a_, y, at: x * a_ + y * at, da, f["a"], dattn, f["attn"])
        ones = jnp.ones((CHUNK, LANE), BF16)
        col_m = [(_dot(mh, ones, 0, 0) + _dot(ml, ones, 0, 0))[:, 0:1] for mh, ml in map(_split, m)]
        dgc_all = jnp.zeros((CHUNK, LANE), F32)
        dbeta_all = jnp.zeros((CHUNK, LANE), F32)
        for h in hs:
            dq_ref[:, h * DN_DIM:(h + 1) * DN_DIM] = dq[h]
            dk_ref[:, h * DN_DIM:(h + 1) * DN_DIM] = dk[h]
            dv_ref[:, h * DN_DIM:(h + 1) * DN_DIM] = drhs_u[h] * f["bcol"][h]
            kdec_term = rowsum(dk_dec[h] * f["k_dec"][h])
            dc_dec = _sum_all(dsn[h] * s[h])
            dgc = (rowsum(m[h]) - col_m[h] + rowsum(dq_dec[h] * f["q_dec"][h]) - kdec_term
                   + rowsum(drhs_w[h] * f["rhs_w"][h]))
            last_extra = jnp.sum(kdec_term, axis=0, keepdims=True) + dc_dec * f["c_dec"][h]
            dgc = dgc + jnp.where(is_last, last_extra, 0.0)
            dbeta = rowsum(drhs_u[h] * vs[h]) + rowsum(dkb[h] * ks[h])
            dgc_all = jnp.where(lane == h, dgc, dgc_all)
            dbeta_all = jnp.where(lane == DN_HEADS + h, dbeta, dbeta_all)
        dg_all = _dot(f["lower"].astype(F32), dgc_all, 0, 0, precision=HIGHEST)
        dgb_ref[...] = jnp.where(lane < DN_HEADS, dg_all, dbeta_all)

    rev = lambda n: (N - 1 - n, 0)
    blk = pl.BlockSpec((CHUNK, DN_WIDTH), rev)
    gblk = pl.BlockSpec((CHUNK, LANE), rev)
    st_blk = pl.BlockSpec((1, DN_HEADS, DN_DIM, DN_DIM), lambda n: (N - 1 - n, 0, 0, 0))
    return pl.pallas_call(
        body, name="dn_chunk_bwd", grid=(N,),
        in_specs=[blk, blk, blk, gblk, st_blk, blk],
        out_specs=[blk, blk, blk, gblk],
        out_shape=[jax.ShapeDtypeStruct((S, DN_WIDTH), F32)] * 3 + [jax.ShapeDtypeStruct((S, LANE), F32)],
        scratch_shapes=[pltpu.VMEM((DN_HEADS, DN_DIM, DN_DIM), F32)],
        compiler_params=_cparams(("arbitrary",)),
    )(q, k, v, gb, states, do)


def _dn_post(o, qkvz, gain_row):
    def fn(i, n, ot, z, g):
        cols = []
        for h in range(DN_HEADS):
            seg = ot[:, h * DN_DIM:(h + 1) * DN_DIM]
            cols.append(seg * lax.rsqrt(jnp.mean(seg * seg, axis=-1, keepdims=True) + EPS) * g)
        return (jnp.concatenate(cols, axis=1) * (z * _sigmoid(z)),)
    return _rows("dn_post", fn, [(o, "row"), (qkvz, "row", (3, DN_WIDTH)), (gain_row, "full")], [(DN_WIDTH, BF16)], tr=512)


def _dn_post_bwd(don, o, qkvz, gain_row):
    def fn(i, n, dy, ot, z, g):
        sg = _sigmoid(z)
        sz = z * sg
        dos, ohs = [], []
        dg = jnp.zeros((1, DN_DIM), F32)
        for h in range(DN_HEADS):
            sl = slice(h * DN_DIM, (h + 1) * DN_DIM)
            seg = ot[:, sl]
            r = lax.rsqrt(jnp.mean(seg * seg, axis=-1, keepdims=True) + EPS)
            oh = seg * r
            dno = dy[:, sl] * sz[:, sl]
            dg = dg + _colsum(dno * oh)
            dn = dno * g
            dos.append(r * (dn - oh * jnp.mean(dn * oh, axis=-1, keepdims=True)))
            ohs.append(oh * g)
        dz = dy * jnp.concatenate(ohs, axis=1) * (sg * (1.0 + z * (1.0 - sg)))
        return jnp.concatenate(dos, axis=1), dz, dg
    ins = [(don, "row"), (o, "row"), (qkvz, "row", (3, DN_WIDTH)), (gain_row, "full")]
    return _rows("dn_post_bwd", fn, ins, [(DN_WIDTH, F32), (DN_WIDTH, F32)], tr=256, accs=[((1, DN_DIM), F32)])


def _dn_prep_bwd(dq, dk, dv, dgb, u, ab, alog_row, dt_row):
    def fn(i, n, dqt, dkt, dvt, dgbt, ut, abt, al, dt):
        sg = _sigmoid(ut)
        y = ut * sg
        dys = []
        for grad, base, sc in ((dqt, 0, DN_DIM ** -0.5), (dkt, DN_WIDTH, 1.0)):
            for h in range(DN_HEADS):
                seg = y[:, base + h * DN_DIM:base + (h + 1) * DN_DIM]
                gr = grad[:, h * DN_DIM:(h + 1) * DN_DIM]
                r = lax.rsqrt(jnp.sum(seg * seg, axis=-1, keepdims=True) + EPS)
                xh = seg * r
                dys.append((r * sc) * (gr - xh * jnp.sum(gr * xh, axis=-1, keepdims=True)))
        dy = jnp.concatenate(dys + [dvt], axis=1)
        du = dy * (sg * (1.0 + ut * (1.0 - sg)))
        lane = lax.broadcasted_iota(jnp.int32, abt.shape, 1)
        is_g = lane < DN_HEADS
        ea = jnp.exp(al)
        x = abt + dt
        slope = -ea * _sigmoid(x)
        gval = -ea * _softplus(x)
        dg = jnp.where(is_g, dgbt, 0.0)
        beta = _sigmoid(abt)
        dab = jnp.where(is_g, dg * slope, jnp.where(lane < 2 * DN_HEADS, dgbt * beta * (1.0 - beta), 0.0))
        return du, dab, _colsum(dg * gval), _colsum(dg * slope)
    ins = [(dq, "row"), (dk, "row"), (dv, "row"), (dgb, "row"), (u, "row"), (ab, "row"), (alog_row, "full"), (dt_row, "full")]
    return _rows("dn_prep_bwd", fn, ins, [(DN_QKV, F32), (LANE, BF16)], tr=256, accs=[((1, LANE), F32)] * 2)


def _dn_conv_bwd(du, dz, qkvz, convw):
    tr = 256

    def fn(i, n, dut, dun, dzt, x, xp, w):
        dun = jnp.where(i < n - 1, dun, 0.0)
        dus = jnp.concatenate([dut, dun], axis=0)
        xs = jnp.concatenate([jnp.where(i > 0, xp, 0.0), x], axis=0)
        dx = None
        dws = []
        for j in range(CONV_W):
            sh = CONV_W - 1 - j
            term = (pltpu.roll(dus, tr + SUBLANE - sh, 0) if sh else dus)[:tr] * w[j:j + 1, :]
            dx = term if dx is None else dx + term
            dws.append(_colsum(dut * (pltpu.roll(xs, sh, 0) if sh else xs)[SUBLANE:]))
        return (jnp.concatenate([dx.astype(BF16), dzt.astype(BF16)], axis=1), *dws)

    ins = [(du, "row"), (du, "next8"), (dz, "row"), (qkvz, "row", (0, DN_QKV)), (qkvz, "prev8", (0, DN_QKV)), (convw, "full")]
    res = _rows("dn_conv_bwd", fn, ins, [(DN_QKVZ, BF16)], tr=tr, accs=[((1, DN_QKV), F32)] * CONV_W)
    return res[0], res[1:]


def _add(acc, r):
    return (r + acc,)


def _mlp_ple_fwd(i, x1, p_i, mlp_gain, ple_gain, w_up, w_down, w_ple, w_gate):
    hm = _rmsnorm_fwd(f"mlp_norm{i}", x1, mlp_gain)
    u, a = _mm(f"mlp_up{i}", hm, w_up, epilogue=lambda acc: (acc, jnp.square(jnp.maximum(acc, 0.0))),
               out_dtypes=(F32, BF16))
    x2 = _mm(f"mlp_down{i}", a, w_down, epilogue=_add, extras=(x1,))
    hp = _rmsnorm_fwd(f"ple_norm{i}", x2, ple_gain)
    pp = _mm(f"ple_proj{i}", p_i, w_ple)

    def gate_epilogue(acc, x2t, ppt):
        gate = _sigmoid(acc)
        return x2t + ppt * gate, gate

    x3, gate = _mm(f"ple_gate{i}", hp, w_gate, epilogue=gate_epilogue, extras=(x2, pp), out_dtypes=(F32, F32))
    return x3, dict(x1=x1, hm=hm, u=u, a=a, x2=x2, hp=hp, pp=pp, gate=gate, p=p_i)


def _mlp_ple_bwd(i, dx3, sv, mlp_gain, ple_gain, w_up, w_down, w_gate):
    def fn(_i, _n, d, g, pp):
        return d * g, d * pp * g * (1.0 - g)
    dpp, dzg = _rows(f"ple_gate_bwd{i}", fn, [(dx3, "row"), (sv["gate"], "row"), (sv["pp"], "row")],
                     [(D_MODEL, BF16), (D_MODEL, BF16)], tr=512)
    d_w_ple = _mm(f"ple_proj_dw{i}", sv["p"], dpp, ta=True, out_dtypes=(BF16,))
    d_w_gate = _mm(f"ple_gate_dw{i}", sv["hp"], dzg, ta=True, out_dtypes=(BF16,))
    dhp = _mm(f"ple_gate_dx{i}", dzg, w_gate, tb=True)
    dx2, d_ple_gain = _rmsnorm_bwd(f"ple_norm_bwd{i}", sv["x2"], ple_gain, dhp, dx3)
    d_w_down = _mm(f"mlp_down_dw{i}", sv["a"], dx2, ta=True, out_dtypes=(BF16,))
    du = _mm(f"mlp_down_dx{i}", dx2, w_down, tb=True, epilogue=lambda acc, ut: (acc * (2.0 * jnp.maximum(ut, 0.0)),),
             extras=(sv["u"],), out_dtypes=(BF16,))
    d_w_up = _mm(f"mlp_up_dw{i}", sv["hm"], du, ta=True, out_dtypes=(BF16,))
    dhm = _mm(f"mlp_up_dx{i}", du, w_up, tb=True)
    dx1, d_mlp_gain = _rmsnorm_bwd(f"mlp_norm_bwd{i}", sv["x1"], mlp_gain, dhm, dx2)
    return dx1, dict(w_ple=d_w_ple, w_ple_gate=d_w_gate, w_down=d_w_down, w_up=d_w_up,
                     ple_norm=d_ple_gain, mlp_norm=d_mlp_gain)


def _loss_fwd_bwd(y, target):
    D = y.shape[1]

    def fn(i, n, yt, tt):
        e = yt - tt
        return e * (1.0 / D), _colsum(e * e)
    dy, sq = _rows("loss", fn, [(y, "row"), (target, "row")], [(D, F32)], tr=512, accs=[((1, D), F32)])
    return sq, dy


def _after(small, token):
    return small + token[0:1, 0:1]


def _local_step(x, p, positions, target, W, P, rest_of_weights, send_layer1, send_mlp0, send_attn):
    consts = _head_consts()
    bd = _block_diag(1.0 / A_HEAD_DIM)
    bd1 = _block_diag(1.0)
    ct, st = _rope_tables(positions, consts)
    gains = jnp.stack([jnp.tile(v, A_HEADS) for g in range(3) for v in (P["attn_q_gain"][g], P["attn_k_gain"][g])])
    pad = LANE - DN_HEADS
    alog_row = jnp.pad(P["dn_a_log"].reshape(1, DN_HEADS), ((0, 0), (0, pad)))
    dt_row = jnp.pad(P["dn_dt_bias"].reshape(1, DN_HEADS), ((0, 0), (0, pad)))
    ogain_row = P["dn_o_gain"].reshape(1, DN_DIM)
    row = lambda name, i: P[name][i:i + 1]

    h0 = _rmsnorm_fwd("mix_norm0", x, row("mix_norm", 0))
    qkv = _mm("attn_qkv", h0, W["attn_w_qkv"])
    qkvn = _attn_prep(qkv, gains, ct, st, consts, bd)
    os_, lses = zip(*[_attn_fwd(qkvn[g], g) for g in range(3)])
    o_attn = _attn_merge(os_, lses)
    x1 = _mm("attn_out", o_attn, W["attn_w_o"], epilogue=_add, extras=(x,))
    W = {**W, **rest_of_weights(x1)}
    x3, sv0 = _mlp_ple_fwd(0, x1, p[0], row("mlp_norm", 0), row("ple_norm", 0),
                           W["w_up"][0], W["w_down"][0], W["w_ple"][0], W["w_ple_gate"][0])
    h1 = _rmsnorm_fwd("mix_norm1", x3, row("mix_norm", 1))
    qkvz = _mm("dn_in_qkvz", h1, W["dn_w_qkvz"])
    ab = _mm("dn_in_ab", h1, W["dn_w_ab"])
    u, q, k, v, gb = _dn_prep(qkvz, ab, W["dn_conv"], alog_row, dt_row)
    o_dn, states = _dn_chunk_fwd(q, k, v, gb)
    on = _dn_post(o_dn, qkvz, ogain_row)
    x4 = _mm("dn_out", on, W["dn_w_o"], epilogue=_add, extras=(x3,))
    x6, sv1 = _mlp_ple_fwd(1, x4, p[1], row("mlp_norm", 1), row("ple_norm", 1),
                           W["w_up"][1], W["w_down"][1], W["w_ple"][1], W["w_ple_gate"][1])
    sq, dy = _loss_fwd_bwd(x6, target)

    dx4, g1 = _mlp_ple_bwd(1, dy, sv1, row("mlp_norm", 1), row("ple_norm", 1),
                           W["w_up"][1], W["w_down"][1], W["w_ple_gate"][1])
    don = _mm("dn_out_dx", dx4, W["dn_w_o"], tb=True)
    d_dn_w_o = _mm("dn_out_dw", on, dx4, ta=True, out_dtypes=(BF16,))
    do_dn, dz, d_ogain = _dn_post_bwd(don, o_dn, qkvz, ogain_row)
    dq, dk, dv, dgb = _dn_chunk_bwd(q, k, v, gb, states, do_dn)
    du, dab, d_alog, d_dt = _dn_prep_bwd(dq, dk, dv, dgb, u, ab, alog_row, dt_row)
    dqkvz, d_conv = _dn_conv_bwd(du, dz, qkvz, W["dn_conv"])
    dh1 = _mm("dn_in_qkvz_dx", dqkvz, W["dn_w_qkvz"], tb=True)
    dh1 = _mm("dn_in_ab_dx", dab, W["dn_w_ab"], tb=True, epilogue=_add, extras=(dh1,))
    d_w_qkvz = _mm("dn_in_qkvz_dw", h1, dqkvz, ta=True, out_dtypes=(BF16,))
    d_w_ab = _mm("dn_in_ab_dw", h1, dab, ta=True, out_dtypes=(BF16,))
    dx3, d_mix1 = _rmsnorm_bwd("mix_norm_bwd1", x3, row("mix_norm", 1), dh1, dx4)
    token = send_layer1(dict(
        dn_w_qkvz=d_w_qkvz, dn_w_ab=d_w_ab, dn_conv=jnp.concatenate(d_conv, 0), dn_w_o=d_dn_w_o,
        w_up=g1["w_up"], w_down=g1["w_down"], w_ple=g1["w_ple"], w_ple_gate=g1["w_ple_gate"]))
    dx1, g0 = _mlp_ple_bwd(0, dx3, sv0, row("mlp_norm", 0), _after(row("ple_norm", 0), token),
                           W["w_up"][0], W["w_down"][0], W["w_ple_gate"][0])
    token = send_mlp0(dict(w_up=g0["w_up"], w_down=g0["w_down"], w_ple=g0["w_ple"], w_ple_gate=g0["w_ple_gate"]))
    do_attn = _mm("attn_out_dx", dx1, W["attn_w_o"], tb=True, epilogue=_add, extras=(_after(jnp.zeros((1, A_WIDTH), F32), token),))
    d_attn_w_o = _mm("attn_out_dw", o_attn, dx1, ta=True, out_dtypes=(BF16,))
    dos, cs = _attn_merge_bwd(do_attn, os_, lses, bd1)
    grads9 = []
    for g in range(3):
        grads9 += list(_attn_bwd(qkvn[g], g, dos[g], lses[g], cs[g]))
    dqkv, dgains = _attn_prep_bwd(qkv, grads9, gains, ct, st, consts, bd)
    d_attn_w_qkv = _mm("attn_qkv_dw", h0, dqkv, ta=True, out_dtypes=(BF16,))
    token = send_attn(dict(attn_w_qkv=d_attn_w_qkv, attn_w_o=d_attn_w_o))
    dh0 = _mm("attn_qkv_dx", dqkv, W["attn_w_qkv"], tb=True, epilogue=_add, extras=(_after(jnp.zeros((1, D_MODEL), F32), token),))
    dx0, d_mix0 = _rmsnorm_bwd("mix_norm_bwd0", x, row("mix_norm", 0), dh0, dx1)

    dg = jnp.stack([t.reshape(A_HEADS, A_HEAD_DIM).sum(0) for t in dgains])
    small = dict(
        mix_norm=jnp.concatenate([d_mix0, d_mix1], 0),
        attn_q_gain=dg[0::2][None], attn_k_gain=dg[1::2][None],
        dn_a_log=d_alog[:, :DN_HEADS], dn_dt_bias=d_dt[:, :DN_HEADS], dn_o_gain=d_ogain,
        mlp_norm=jnp.concatenate([g0["mlp_norm"], g1["mlp_norm"]], 0),
        ple_norm=jnp.concatenate([g0["ple_norm"], g1["ple_norm"]], 0),
    )
    return sq, dx0, small


MESH_IDS = pl.DeviceIdType.MESH
ANY = pl.BlockSpec(memory_space=pl.ANY)


def _place():
    return lax.axis_index("x"), lax.axis_index("y"), lax.axis_index("c")


def _sem_scratch(n_streams):
    return [pltpu.SemaphoreType.DMA((n_streams, N_DEV - 1)), pltpu.SemaphoreType.DMA((n_streams, N_DEV - 1)),
            pltpu.SemaphoreType.DMA((n_streams,))]


def _all_gather(name, arrays, streams):
    n_in, n_st = len(arrays), len(streams)
    shapes = [arrays[a].shape if li is None else arrays[a].shape[1:] for a, li in streams]

    def body(*refs):
        in_refs, out_refs, token = refs[:n_in], refs[n_in:n_in + n_st], refs[n_in + n_st]
        send_sems, recv_sems, local_sems = refs[n_in + n_st + 1:]
        token[...] = jnp.zeros_like(token)
        x, y, c = _place()
        me, sibling = (x, y, c), (x, y, 1 - c)
        chips = [(1 - x, y), (x, 1 - y), (1 - x, 1 - y)]

        def copy(s, k, block, to, own=False):
            a, li = streams[s]
            dst = out_refs[s].at[4 * block[0] + 2 * block[1] + block[2]]
            src = (in_refs[a] if li is None else in_refs[a].at[li]) if own else dst
            return pltpu.make_async_remote_copy(src_ref=src, dst_ref=dst, send_sem=send_sems.at[s, k],
                                                recv_sem=recv_sems.at[s, k], device_id=to, device_id_type=MESH_IDS)

        started = []
        for s, (a, li) in enumerate(streams):
            src = in_refs[a] if li is None else in_refs[a].at[li]
            mine = pltpu.make_async_copy(src, out_refs[s].at[4 * x + 2 * y + c], local_sems.at[s])
            mine.start()
            started.append(mine)
        sends = []
        for s in range(n_st):
            first = [copy(s, 0, me, sibling, own=True)]
            first += [copy(s, 1 + j, me, (*chip, c), own=True) for j, chip in enumerate(chips)]
            for cp in first:
                cp.start()
            sends += first
        for j, chip in enumerate(chips):
            for s in range(n_st):
                copy(s, 1 + j, (*chip, c), me).wait_recv()
                fwd = copy(s, 4 + j, (*chip, c), sibling)
                fwd.start()
                sends.append(fwd)
        for s in range(n_st):
            copy(s, 0, sibling, me).wait_recv()
            for j, chip in enumerate(chips):
                copy(s, 4 + j, (*chip, 1 - c), me).wait_recv()
        for cp in sends:
            cp.wait_send()
        for cp in started:
            cp.wait()

    res = pl.pallas_call(
        body, name=name,
        out_shape=[jax.ShapeDtypeStruct((N_DEV,) + tuple(sh), arrays[a].dtype) for sh, (a, _) in zip(shapes, streams)]
        + [jax.ShapeDtypeStruct((SUBLANE, LANE), F32)],
        in_specs=[ANY] * n_in, out_specs=[ANY] * n_st + [pl.BlockSpec(memory_space=pltpu.VMEM)],
        scratch_shapes=_sem_scratch(n_st),
    )(*arrays)
    return list(res[:n_st]), res[n_st]


HBM = pl.BlockSpec(memory_space=pltpu.HBM)
SEM = pl.BlockSpec(memory_space=pltpu.SEMAPHORE)
FLOWS = pltpu.CompilerParams(has_side_effects=pltpu.SideEffectType.DATAFLOW_SIDE_EFFECTING)


def _in_hbm(a):
    return pltpu.with_memory_space_constraint(a, pltpu.HBM)


def _hbm_like(a):
    return pltpu.HBM(a.shape, a.dtype)


def _peers(x, y, c):
    return [(1 - x if k & 4 else x, 1 - y if k & 2 else y, 1 - c if k & 1 else c) for k in range(1, N_DEV)]


def _start_copies(name, n_remote, n_own, make_copies, operands):
    n = len(operands)

    def body(*refs):
        for cp in make_copies(refs[:n], refs[n], refs[n + 1], refs[n + 2]):
            cp.start()
        refs[-1][...] = jnp.zeros_like(refs[-1])

    res = pl.pallas_call(
        body, name=name,
        out_shape=(pltpu.SemaphoreType.DMA((n_remote,)), pltpu.SemaphoreType.DMA((n_remote,)), pltpu.SemaphoreType.DMA((n_own,)),
                   *[_hbm_like(t) for t in operands], jax.ShapeDtypeStruct((SUBLANE, LANE), F32)),
        in_specs=[HBM] * n, out_specs=(SEM, SEM, SEM, *[HBM] * n, pl.BlockSpec(memory_space=pltpu.VMEM)),
        input_output_aliases={i: 3 + i for i in range(n)}, compiler_params=FLOWS,
    )(*[_in_hbm(t) for t in operands])
    return res[:3], list(res[3:3 + n]), res[-1]


def _wait_copies(name, make_waits, sems, operands, after):
    n = len(operands)

    def body(*refs):
        for wait in make_waits(refs[:n], refs[n], refs[n + 1], refs[n + 2]):
            wait()

    res = pl.pallas_call(
        body, name=name, out_shape=tuple(_hbm_like(t) for t in operands),
        in_specs=[HBM] * n + [SEM, SEM, SEM, ANY], out_specs=tuple([HBM] * n),
        input_output_aliases={i: i for i in range(n)}, compiler_params=FLOWS,
    )(*operands, *sems, after)
    return list(res)


def _gather_plan(n_in, streams):
    def block(arr, s):
        a, li = streams[s]
        return arr[a] if li is None else arr[a].at[li]

    def copies(refs, send_sems, recv_sems, own_sems, arrivals=False):
        arr, land = refs[:n_in], refs[n_in:]
        x, y, c = _place()
        me = 4 * x + 2 * y + c
        out = []
        for s in range(len(streams)):
            out.append(("own", pltpu.make_async_copy(block(arr, s), land[s].at[me], own_sems.at[s])))
            for k, (px, py, pc) in enumerate(_peers(x, y, c)):
                out.append(("remote", pltpu.make_async_remote_copy(
                    src_ref=block(arr, s), dst_ref=land[s].at[4 * px + 2 * py + pc if arrivals else me],
                    send_sem=send_sems.at[s * (N_DEV - 1) + k], recv_sem=recv_sems.at[s * (N_DEV - 1) + k],
                    device_id=(px, py, pc), device_id_type=MESH_IDS)))
        return out
    return copies


def _exchange_plan(n_st):
    def copies(refs, send_sems, recv_sems, own_sems, arrivals=False):
        snd, rcv = refs[:n_st], refs[n_st:]
        x, y, c = _place()
        me = 4 * x + 2 * y + c
        out = []
        for s in range(n_st):
            out.append(("own", pltpu.make_async_copy(snd[s].at[me], rcv[s].at[me], own_sems.at[s])))
            for k, (px, py, pc) in enumerate(_peers(x, y, c)):
                peer = 4 * px + 2 * py + pc
                out.append(("remote", pltpu.make_async_remote_copy(
                    src_ref=snd[s].at[peer], dst_ref=rcv[s].at[peer if arrivals else me],
                    send_sem=send_sems.at[s * (N_DEV - 1) + k], recv_sem=recv_sems.at[s * (N_DEV - 1) + k],
                    device_id=(px, py, pc), device_id_type=MESH_IDS)))
        return out
    return copies


def _split_transfer(tag, plan, n_streams, operands):
    sems, operands, token = _start_copies(f"{tag}_start", n_streams * (N_DEV - 1), n_streams,
                                          lambda refs, a, b, o: [cp for _, cp in plan(refs, a, b, o)], operands)

    def waits(refs, a, b, o):
        out = []
        for kind, cp in plan(refs, a, b, o, arrivals=True):
            out += [cp.wait] if kind == "own" else [cp.wait_send, cp.wait_recv]
        return out

    return (lambda after: _wait_copies(f"{tag}_wait", waits, sems, operands, after)), token


def _gather_async(tag, arrays, streams):
    lands = [lax.empty((N_DEV,) + tuple(arrays[a].shape if li is None else arrays[a].shape[1:]), arrays[a].dtype)
             for a, li in streams]
    finish, token = _split_transfer(tag, _gather_plan(len(arrays), streams), len(streams), list(arrays) + lands)
    return (lambda after: finish(after)[len(arrays):]), token


def _exchange_async(tag, sends):
    recvs = [lax.empty(t.shape, t.dtype) for t in sends]
    finish, token = _split_transfer(tag, _exchange_plan(len(sends)), len(sends), list(sends) + recvs)
    return (lambda after: finish(after)[len(sends):]), token


def _dn_in_pieces():
    n = (DN_QKVZ + 2 * DN_HEADS) // N_DEV
    segs = ((0, DN_QKV, 0, 0), (DN_QKV, DN_QKV + 2 * DN_HEADS, 1, 0), (DN_QKV + 2 * DN_HEADS, DN_QKVZ + 2 * DN_HEADS, 0, DN_QKV))
    out = []
    for d in range(N_DEV):
        lo, hi = d * n, (d + 1) * n
        for s0, s1, tgt, t0 in segs:
            a, b = max(lo, s0), min(hi, s1)
            if a < b:
                out.append((d, a - lo, b - lo, tgt, t0 + a - s0))
    return out


def _unpack_cols(name, g):
    _, K, n = g.shape
    tr = 256

    def body(g_ref, o_ref):
        for d in range(N_DEV):
            o_ref[:, d * n:(d + 1) * n] = g_ref[d]

    return pl.pallas_call(
        body, name=name, grid=(K // tr,), in_specs=[pl.BlockSpec((N_DEV, tr, n), lambda i: (0, i, 0))],
        out_specs=pl.BlockSpec((tr, N_DEV * n), lambda i: (i, 0)),
        out_shape=jax.ShapeDtypeStruct((K, N_DEV * n), g.dtype), compiler_params=_cparams(("parallel",)),
    )(g)


def _pack_cols(name, w):
    K, n = w.shape[0], w.shape[1] // N_DEV
    tr = 256

    def body(w_ref, o_ref):
        for d in range(N_DEV):
            o_ref[d] = w_ref[:, d * n:(d + 1) * n]

    return pl.pallas_call(
        body, name=name, grid=(K // tr,), in_specs=[pl.BlockSpec((tr, N_DEV * n), lambda i: (i, 0))],
        out_specs=pl.BlockSpec((N_DEV, tr, n), lambda i: (0, i, 0)),
        out_shape=jax.ShapeDtypeStruct((N_DEV, K, n), w.dtype), compiler_params=_cparams(("parallel",)),
    )(w)


def _unpack_dn_in(g):
    _, K, n = g.shape
    tr = 256

    def body(g_ref, qkvz_ref, ab_ref):
        ab_ref[...] = jnp.zeros_like(ab_ref)
        for d, c0, c1, tgt, t0 in _dn_in_pieces():
            (qkvz_ref, ab_ref)[tgt][:, t0:t0 + c1 - c0] = g_ref[d, :, c0:c1]

    return pl.pallas_call(
        body, name="unpack_dn_in", grid=(K // tr,), in_specs=[pl.BlockSpec((N_DEV, tr, n), lambda i: (0, i, 0))],
        out_specs=[pl.BlockSpec((tr, DN_QKVZ), lambda i: (i, 0)), pl.BlockSpec((tr, LANE), lambda i: (i, 0))],
        out_shape=[jax.ShapeDtypeStruct((K, DN_QKVZ), g.dtype), jax.ShapeDtypeStruct((K, LANE), g.dtype)],
        compiler_params=_cparams(("parallel",)),
    )(g)


def _pack_dn_in(d_qkvz, d_ab):
    K = d_qkvz.shape[0]
    n = (DN_QKVZ + 2 * DN_HEADS) // N_DEV
    tr = 256

    def body(qkvz_ref, ab_ref, o_ref):
        for d, c0, c1, tgt, t0 in _dn_in_pieces():
            o_ref[d, :, c0:c1] = (qkvz_ref, ab_ref)[tgt][:, t0:t0 + c1 - c0]

    return pl.pallas_call(
        body, name="pack_dn_in", grid=(K // tr,),
        in_specs=[pl.BlockSpec((tr, DN_QKVZ), lambda i: (i, 0)), pl.BlockSpec((tr, LANE), lambda i: (i, 0))],
        out_specs=pl.BlockSpec((N_DEV, tr, n), lambda i: (0, i, 0)),
        out_shape=jax.ShapeDtypeStruct((N_DEV, K, n), d_qkvz.dtype), compiler_params=_cparams(("parallel",)),
    )(d_qkvz, d_ab)


ADAMW_ROWS = 256


def _adamw(name, parts, w, m, v):
    R, C = w.shape
    tr = min(R, ADAMW_ROWS)
    assert R % tr == 0 and parts.shape == (N_DEV, R, C)
    c1 = 1.0 - B1 ** STEP
    c2 = 1.0 - B2 ** STEP

    def body(p_ref, w_ref, m_ref, v_ref, g_ref, d_ref, nm_ref, nv_ref):
        g = p_ref[0].astype(F32)
        for dev in range(1, N_DEV):
            g = g + p_ref[dev].astype(F32)
        nm = B1 * m_ref[...] + (1.0 - B1) * g
        nv = B2 * v_ref[...] + (1.0 - B2) * jnp.square(g)
        g_ref[...] = g
        nm_ref[...] = nm
        nv_ref[...] = nv
        d_ref[...] = -LR * ((nm / c1) / (jnp.sqrt(nv / c2) + ADAM_EPS) + WD * w_ref[...])

    blk = pl.BlockSpec((tr, C), lambda i: (i, 0))
    return pl.pallas_call(
        body, name=name, grid=(R // tr,),
        in_specs=[pl.BlockSpec((N_DEV, tr, C), lambda i: (0, i, 0)), blk, blk, blk],
        out_specs=[blk] * 4, out_shape=[jax.ShapeDtypeStruct((R, C), F32)] * 4,
        compiler_params=_cparams(("parallel",)),
    )(parts, w, m, v)


SMALL = ("mix_norm", "attn_q_gain", "attn_k_gain", "dn_a_log", "dn_dt_bias", "dn_o_gain", "mlp_norm", "ple_norm")
WEIGHTS = ("mix_norm", "attn_w_qkv", "attn_q_gain", "attn_k_gain", "attn_w_o", "dn_w_in", "dn_conv", "dn_a_log",
           "dn_dt_bias", "dn_o_gain", "dn_w_o", "mlp_norm", "w_up", "w_down", "ple_norm", "w_ple", "w_ple_gate")


def _to_rows(flat, multiple):
    n = flat.shape[-1]
    rows = -(-n // (LANE * multiple)) * multiple
    return jnp.pad(flat, [(0, rows * LANE - n)]).reshape(rows, LANE)


def _cols_to_devices(w):
    K, N = w.shape
    return jnp.transpose(w.reshape(K, N_DEV, N // N_DEV), (1, 0, 2))


def _cols_from_devices(g):
    _, K, n = g.shape
    return jnp.transpose(g, (1, 0, 2)).reshape(K, N_DEV * n)


SMALL_ROWS = 96


def _pack_small(vals, loss_rows):
    rows = [_to_rows(vals[n].reshape(-1), SUBLANE) for n in SMALL] + [loss_rows]
    buf = jnp.concatenate(rows, 0)
    assert buf.shape == (SMALL_ROWS, LANE)
    return buf


def _unpack_small(buf, like):
    out, r = {}, 0
    for n in SMALL:
        sz = math.prod(like[n].shape)
        out[n] = buf[r:r + -(-sz // LANE)].reshape(-1)[:sz].reshape(like[n].shape)
        r += -(-sz // (LANE * SUBLANE)) * SUBLANE
    return out


def kernel(x, p, positions, mix_norm, attn_w_qkv, attn_q_gain, attn_k_gain, attn_w_o, dn_w_in, dn_conv, dn_a_log, dn_dt_bias, dn_o_gain, dn_w_o, mlp_norm, w_up, w_down, ple_norm, w_ple, w_ple_gate, loss_target, m_mix_norm, m_attn_w_qkv, m_attn_q_gain, m_attn_k_gain, m_attn_w_o, m_dn_w_in, m_dn_conv, m_dn_a_log, m_dn_dt_bias, m_dn_o_gain, m_dn_w_o, m_mlp_norm, m_w_up, m_w_down, m_ple_norm, m_w_ple, m_w_ple_gate, v_mix_norm, v_attn_w_qkv, v_attn_q_gain, v_attn_k_gain, v_attn_w_o, v_dn_w_in, v_dn_conv, v_dn_a_log, v_dn_dt_bias, v_dn_o_gain, v_dn_w_o, v_mlp_norm, v_w_up, v_w_down, v_ple_norm, v_w_ple, v_w_ple_gate):
    w = dict(mix_norm=mix_norm, attn_w_qkv=attn_w_qkv, attn_q_gain=attn_q_gain, attn_k_gain=attn_k_gain, attn_w_o=attn_w_o,
             dn_w_in=dn_w_in, dn_conv=dn_conv, dn_a_log=dn_a_log, dn_dt_bias=dn_dt_bias, dn_o_gain=dn_o_gain, dn_w_o=dn_w_o,
             mlp_norm=mlp_norm, w_up=w_up, w_down=w_down, ple_norm=ple_norm, w_ple=w_ple, w_ple_gate=w_ple_gate)
    m = dict(mix_norm=m_mix_norm, attn_w_qkv=m_attn_w_qkv, attn_q_gain=m_attn_q_gain, attn_k_gain=m_attn_k_gain,
             attn_w_o=m_attn_w_o, dn_w_in=m_dn_w_in, dn_conv=m_dn_conv, dn_a_log=m_dn_a_log, dn_dt_bias=m_dn_dt_bias,
             dn_o_gain=m_dn_o_gain, dn_w_o=m_dn_w_o, mlp_norm=m_mlp_norm, w_up=m_w_up, w_down=m_w_down,
             ple_norm=m_ple_norm, w_ple=m_w_ple, w_ple_gate=m_w_ple_gate)
    v = dict(mix_norm=v_mix_norm, attn_w_qkv=v_attn_w_qkv, attn_q_gain=v_attn_q_gain, attn_k_gain=v_attn_k_gain,
             attn_w_o=v_attn_w_o, dn_w_in=v_dn_w_in, dn_conv=v_dn_conv, dn_a_log=v_dn_a_log, dn_dt_bias=v_dn_dt_bias,
             dn_o_gain=v_dn_o_gain, dn_w_o=v_dn_w_o, mlp_norm=v_mlp_norm, w_up=v_w_up, w_down=v_w_down,
             ple_norm=v_ple_norm, w_ple=v_w_ple, w_ple_gate=v_w_ple_gate)
    S = x.shape[1]

    bf = lambda a: a.astype(BF16)
    rows_to_devices = lambda t: t.reshape(N_DEV, t.shape[0] // N_DEV, t.shape[1])

    (g_qkv, g_ao), token = _all_gather("gather_attn", [bf(attn_w_qkv[0]), bf(attn_w_o[0])], [(0, None), (1, None)])
    rest_shards = [bf(dn_w_in[0]), bf(dn_w_o[0]), bf(w_up), bf(w_down), bf(w_ple), bf(w_ple_gate), _after(dn_conv[0], token)]
    rest_streams = [(0, None), (1, None), (2, 0), (2, 1), (3, 0), (3, 1), (4, 0), (4, 1), (5, 0), (5, 1), (6, None)]
    rest_arrived, token = _gather_async("gather_rest", rest_shards, rest_streams)
    W = dict(attn_w_qkv=_unpack_cols("unpack_attn_qkv", g_qkv), attn_w_o=_cols_from_devices(g_ao))

    def rest_of_weights(after):
        g_in, g_do, g_up0, g_up1, g_dn0, g_dn1, g_pl0, g_pl1, g_gt0, g_gt1, g_conv = rest_arrived(after)
        rest = dict(
            dn_conv=jnp.transpose(g_conv, (1, 0, 2)).reshape(CONV_W, DN_QKV), dn_w_o=g_do.reshape(DN_WIDTH, D_MODEL),
            w_up=[_cols_from_devices(g_up0), _cols_from_devices(g_up1)],
            w_down=[g_dn0.reshape(D_FF, D_MODEL), g_dn1.reshape(D_FF, D_MODEL)],
            w_ple=[_cols_from_devices(g_pl0), _cols_from_devices(g_pl1)],
            w_ple_gate=[g_gt0.reshape(D_MODEL, D_MODEL), g_gt1.reshape(D_MODEL, D_MODEL)])
        rest["dn_w_qkvz"], rest["dn_w_ab"] = _unpack_dn_in(g_in)
        return rest

    pending = {}

    def mlp_sends(g):
        return [_cols_to_devices(g["w_up"]), rows_to_devices(g["w_down"]), _cols_to_devices(g["w_ple"]),
                rows_to_devices(g["w_ple_gate"])]

    def start(tag, sends):
        pending[tag], token = _exchange_async(f"exchange_{tag}", sends)
        return token

    def send_layer1(g):
        conv_send = jnp.transpose(g["dn_conv"].reshape(CONV_W, N_DEV, DN_QKV // N_DEV), (1, 0, 2))
        return start("layer1", [_pack_dn_in(g["dn_w_qkvz"], g["dn_w_ab"]), conv_send, rows_to_devices(g["dn_w_o"])] + mlp_sends(g))

    def send_mlp0(g):
        return start("mlp0", mlp_sends(g))

    def send_attn(g):
        return start("attn", [_pack_cols("pack_attn_qkv", g["attn_w_qkv"]), _cols_to_devices(g["attn_w_o"])])

    P = dict(mix_norm=_after(mix_norm, token), attn_q_gain=attn_q_gain[0], attn_k_gain=attn_k_gain[0], dn_a_log=dn_a_log[0],
             dn_dt_bias=dn_dt_bias[0], dn_o_gain=dn_o_gain[0], mlp_norm=mlp_norm, ple_norm=ple_norm)

    sq, dx0, small_g = _local_step(x[0], p[:, 0], positions.reshape(S, 1), loss_target[0], W, P,
                                   rest_of_weights, send_layer1, send_mlp0, send_attn)

    r_in, r_conv, r_do, r_up1, r_dn1, r_pl1, r_gt1 = pending["layer1"](dx0)
    r_up0, r_dn0, r_pl0, r_gt0 = pending["mlp0"](dx0)
    r_qkv, r_ao = pending["attn"](dx0)
    big = {}
    for n, parts in (("attn_w_qkv", [r_qkv]), ("attn_w_o", [r_ao]), ("dn_w_in", [r_in]), ("dn_conv", [r_conv]),
                     ("dn_w_o", [r_do]), ("w_up", [r_up0, r_up1]), ("w_down", [r_dn0, r_dn1]),
                     ("w_ple", [r_pl0, r_pl1]), ("w_ple_gate", [r_gt0, r_gt1])):
        layers = [_adamw(f"adamw_{n}{l}", pt, w[n][l], m[n][l], v[n][l]) for l, pt in enumerate(parts)]
        big[n] = [jnp.stack([res[k] for res in layers]) for k in range(4)]

    loss_rows = jnp.pad((0.5 / D_MODEL) * jnp.sum(sq, axis=1, keepdims=True), ((0, SUBLANE - 1), (0, LANE - 1)))
    small_like = {n: w[n] for n in SMALL}
    parts_s = _all_gather("gather_small", [_pack_small(small_g, loss_rows)], [(0, None)])[0][0]
    zero_rows = jnp.zeros((SUBLANE, LANE), F32)
    small = _adamw("adamw_small", parts_s, _pack_small(w, zero_rows), _pack_small(m, zero_rows), _pack_small(v, zero_rows))
    loss = small[0][SMALL_ROWS - SUBLANE, 0]
    small = [_unpack_small(b, small_like) for b in small]

    outs = [loss, dx0[None]]
    for k in range(4):
        for n in WEIGHTS:
            outs.append(small[k][n] if n in SMALL else big[n][k])
    return tuple(outs)
```

```python
import functools
import math

import jax
import jax.numpy as jnp
from jax import lax
from jax.experimental import pallas as pl
from jax.experimental.pallas import tpu as pltpu

F32 = jnp.float32
BF16 = jnp.bfloat16
HIGHEST = lax.Precision.HIGHEST

N_DEV = 8
D_MODEL = 1024
EPS = 1e-6
SWA_GROUPS = ((128, 1), (512, 4), (2048, 16))
A_HEADS = 8
A_HEAD_DIM = 64
A_WIDTH = A_HEADS * A_HEAD_DIM
A_QKV = 3 * 3 * A_WIDTH
ROPE_DIM = 16
ROPE_HALF = 8
ROPE_THETA = 500000.0
BAND = 128
DN_HEADS = 8
DN_DIM = 128
DN_WIDTH = DN_HEADS * DN_DIM
CONV_W = 4
CHUNK = 64
D_FF = 4 * D_MODEL
PLE_DIM = 256
LR, B1, B2, ADAM_EPS, WD, STEP = 0.001, 0.9, 0.999, 1e-08, 0.01, 10

VMEM_LIMIT = 56 * 1024 * 1024
MXU_TILE = 1024
LANE = 128
SUBLANE = 8


def _cparams(sem):
    return pltpu.CompilerParams(dimension_semantics=sem, vmem_limit_bytes=VMEM_LIMIT)


def _tile(n, pref):
    if n <= pref:
        return n
    t = (pref // LANE) * LANE
    while t >= LANE:
        if n % t == 0:
            return t
        t -= LANE
    raise ValueError(f"no tile for {n}")


def _dot(a, b, ca=1, cb=0, precision=None):
    return lax.dot_general(a, b, (((ca,), (cb,)), ((), ())), precision=precision,
                           preferred_element_type=F32)


def _bdot(a, b, ca=1, cb=0):
    return _dot(a.astype(BF16), b.astype(BF16), ca, cb)


def _mm(name, a, b, *, ta=False, tb=False, epilogue=None, extras=(), out_dtypes=(F32,), n_colsums=0,
        tm_pref=MXU_TILE, tn_pref=1536, tk_pref=MXU_TILE):
    M, K = (a.shape[1], a.shape[0]) if ta else a.shape
    N = b.shape[0] if tb else b.shape[1]
    assert (b.shape[1] if tb else b.shape[0]) == K
    tm, tn, tk = _tile(M, tm_pref), _tile(N, tn_pref), _tile(K, tk_pref)
    nk = K // tk
    n_out = len(out_dtypes)
    n_ext = len(extras)
    assert n_colsums == 0 or tn == N

    def body(*refs):
        a_ref, b_ref = refs[0], refs[1]
        ext = refs[2:2 + n_ext]
        outs = refs[2 + n_ext:2 + n_ext + n_out]
        sums = refs[2 + n_ext + n_out:2 + n_ext + n_out + n_colsums]
        row_tile, k = pl.program_id(0), pl.program_id(2)
        prod = _bdot(a_ref[...], b_ref[...], 0 if ta else 1, 1 if tb else 0)

        def finish(r):
            res = (r,) if epilogue is None else epilogue(r, *[e[...] for e in ext])
            for o, v in zip(outs, res):
                o[...] = v.astype(o.dtype)
            for o, v in zip(sums, res[n_out:]):
                @pl.when(row_tile == 0)
                def _(o=o, v=v):
                    o[...] = v

                @pl.when(row_tile > 0)
                def _(o=o, v=v):
                    o[...] += v

        if nk == 1:
            finish(prod)
            return
        acc = refs[-1]

        @pl.when(k == 0)
        def _():
            acc[...] = jnp.zeros_like(acc)

        acc[...] += prod

        @pl.when(k == nk - 1)
        def _():
            finish(acc[...])

    a_spec = pl.BlockSpec((tk, tm), lambda i, j, k: (k, i)) if ta else pl.BlockSpec((tm, tk), lambda i, j, k: (i, k))
    b_spec = pl.BlockSpec((tn, tk), lambda i, j, k: (j, k)) if tb else pl.BlockSpec((tk, tn), lambda i, j, k: (k, j))
    ext_specs = []
    for e in extras:
        if e.shape[0] == 1 and M != 1:
            ext_specs.append(pl.BlockSpec((1, tn), lambda i, j, k: (0, j)))
        else:
            ext_specs.append(pl.BlockSpec((tm, tn), lambda i, j, k: (i, j)))
    out = pl.pallas_call(
        body, name=name,
        grid=(M // tm, N // tn, nk),
        in_specs=[a_spec, b_spec] + ext_specs,
        out_specs=[pl.BlockSpec((tm, tn), lambda i, j, k: (i, j)) for _ in range(n_out)]
        + [pl.BlockSpec((1, tn), lambda i, j, k: (0, 0)) for _ in range(n_colsums)],
        out_shape=[jax.ShapeDtypeStruct((M, N), dt) for dt in out_dtypes]
        + [jax.ShapeDtypeStruct((1, N), F32) for _ in range(n_colsums)],
        scratch_shapes=[pltpu.VMEM((tm, tn), F32)] if nk > 1 else [],
        compiler_params=_cparams(("arbitrary" if n_colsums else "parallel", "parallel", "arbitrary")),
    )(a, b, *extras)
    return out[0] if len(out) == 1 else tuple(out)


def _rows(name, fn, ins, outs, *, tr, accs=()):
    ins = [(e[0], e[1]) + (e[2] if len(e) > 2 else (0, e[0].shape[-1])) for e in ins]
    n_rows = next(e[0].shape[0] for e in ins if e[1] == "row")
    assert n_rows % tr == 0 and tr % SUBLANE == 0
    steps = n_rows // tr
    t8 = tr // SUBLANE
    n8 = n_rows // SUBLANE
    n_in, n_out, n_acc = len(ins), len(outs), len(accs)

    def body(*refs):
        i = pl.program_id(0)
        vals = fn(i, steps, *[r[...] for r in refs[:n_in]])
        if not isinstance(vals, (tuple, list)):
            vals = (vals,)
        assert len(vals) == n_out + n_acc
        for o, v in zip(refs[n_in:n_in + n_out], vals[:n_out]):
            o[...] = v.astype(o.dtype)
        if n_acc:
            acc_refs = refs[n_in + n_out:]

            @pl.when(i == 0)
            def _():
                for r in acc_refs:
                    r[...] = jnp.zeros_like(r)

            for r, v in zip(acc_refs, vals[n_out:]):
                r[...] += v.astype(r.dtype)

    in_specs = []
    for a, kind, cb, c in ins:
        if kind == "row":
            in_specs.append(pl.BlockSpec((tr, c), lambda i, cb=cb: (i, cb)))
        elif kind == "full":
            in_specs.append(pl.BlockSpec(a.shape, lambda i, z=(0,) * a.ndim: z))
        elif kind == "prev8":
            in_specs.append(pl.BlockSpec((SUBLANE, c), lambda i, cb=cb: (jnp.maximum(i * t8 - 1, 0), cb)))
        elif kind == "next8":
            in_specs.append(pl.BlockSpec((SUBLANE, c), lambda i, cb=cb: (jnp.minimum((i + 1) * t8, n8 - 1), cb)))
        else:
            raise ValueError(kind)
    out_specs = [pl.BlockSpec((tr, c), lambda i: (i, 0)) for c, _ in outs]
    out_specs += [pl.BlockSpec(s, lambda i, z=(0,) * len(s): z) for s, _ in accs]
    out_shape = [jax.ShapeDtypeStruct((n_rows, c), dt) for c, dt in outs]
    out_shape += [jax.ShapeDtypeStruct(s, dt) for s, dt in accs]
    res = pl.pallas_call(
        body, name=name, grid=(steps,), in_specs=in_specs, out_specs=out_specs, out_shape=out_shape,
        compiler_params=_cparams(("arbitrary",) if n_acc else ("parallel",)),
    )(*[e[0] for e in ins])
    return res[0] if len(res) == 1 else tuple(res)


def _colsum(x):
    return jnp.sum(x, axis=0, keepdims=True)


def _sum_all(x):
    return jnp.sum(jnp.sum(x, axis=1, keepdims=True), axis=0, keepdims=True)


def _rmsnorm_fwd(name, x, gain):
    def fn(i, n, xt, g):
        r = lax.rsqrt(jnp.mean(xt * xt, axis=-1, keepdims=True) + EPS)
        return (xt * r * g,)
    return _rows(name, fn, [(x, "row"), (gain, "full")], [(x.shape[1], BF16)], tr=512)


FUSED_ROWS = 512


def _res_norm(acc, res, g):
    x = res + acc
    return x, x * lax.rsqrt(jnp.mean(x * x, axis=-1, keepdims=True) + EPS) * g


def _norm_bwd(dh, x, g, dres):
    r = lax.rsqrt(jnp.mean(x * x, axis=-1, keepdims=True) + EPS)
    xh = x * r
    dxn = dh * g
    dx = dres + r * (dxn - xh * jnp.mean(dxn * xh, axis=-1, keepdims=True))
    return dx, _colsum(dh * xh)


def _norm_bwd_2(dh, x, g, dres):
    dx, dg = _norm_bwd(dh, x, g, dres)
    return dx, dx, dg


def _head_consts():
    import numpy as np
    e = np.arange(A_WIDTH) % A_HEAD_DIM
    inv = (np.float32(ROPE_THETA) ** (-np.arange(0, ROPE_DIM, 2, dtype=np.float32) / np.float32(ROPE_DIM))).astype(np.float32)
    c = np.zeros((8, A_WIDTH), np.float32)
    c[0] = np.where(e < ROPE_DIM, inv[e % ROPE_HALF], 0.0)
    c[1] = np.where(e < ROPE_HALF, -1.0, np.where(e < ROPE_DIM, 1.0, 0.0))
    c[2] = (e < ROPE_HALF).astype(np.float32)
    c[3] = (e < ROPE_DIM).astype(np.float32)
    return jnp.asarray(c)


def _block_diag(scale):
    import numpy as np
    h = np.arange(A_WIDTH) // A_HEAD_DIM
    return jnp.asarray((h[:, None] == h[None, :]).astype(np.float32) * scale, dtype=BF16)


def _seg_sum(x, bd):
    hi = x.astype(BF16)
    lo = (x - hi.astype(F32)).astype(BF16)
    return _dot(hi, bd) + _dot(lo, bd)


def _rope_tables(positions, consts):
    def fn(i, n, pos, c):
        ang = pos.astype(F32) * c[0:1, :LANE]
        return jnp.cos(ang), jnp.sin(ang) * c[1:2, :LANE]
    return _rows("rope_tables", fn, [(positions, "row"), (consts, "full")], [(LANE, F32), (LANE, F32)], tr=512)


def _rope_wide(t):
    return jnp.concatenate([t] * (A_WIDTH // LANE), axis=1)


def _rope_apply(y, ct, st, low):
    rolled = jnp.where(low, pltpu.roll(y, A_WIDTH - ROPE_HALF, 1), pltpu.roll(y, ROPE_HALF, 1))
    return y * ct + rolled * st


def _rope_apply_bwd(dout, ct, st, low, in16):
    t = dout * st
    back = jnp.where(low, pltpu.roll(t, A_WIDTH - ROPE_HALF, 1), jnp.where(in16, pltpu.roll(t, ROPE_HALF, 1), 0.0))
    return dout * ct + back


def _attn_prep(qkv, gains, ct, st, consts, bd):
    def fn(i, n, t, g, c_t, s_t, c, b):
        low = c[2:3, :] > 0.5
        c_t, s_t = _rope_wide(c_t), _rope_wide(s_t)
        groups = []
        for grp in range(3):
            cols = []
            for which in range(3):
                off = (grp * 3 + which) * A_WIDTH
                x = t[:, off:off + A_WIDTH]
                if which == 2:
                    cols.append(x.astype(BF16))
                    continue
                r = lax.rsqrt(_seg_sum(x * x, b) + EPS)
                y = x * r * g[grp * 2 + which:grp * 2 + which + 1, :]
                cols.append(_rope_apply(y, c_t, s_t, low).astype(BF16))
            groups.append(jnp.concatenate(cols, axis=1))
        return tuple(groups)
    return _rows("attn_prep", fn, [(qkv, "row"), (gains, "full"), (ct, "row"), (st, "row"), (consts, "full"), (bd, "full")],
                 [(3 * A_WIDTH, BF16)] * 3, tr=256)


def _band_mask(n):
    row = lax.broadcasted_iota(jnp.int32, (BAND, 2 * BAND), 0)
    col = lax.broadcasted_iota(jnp.int32, (BAND, 2 * BAND), 1)
    dist = row + BAND - col
    return (dist >= 0) & (dist <= BAND) & ((col >= BAND) | (n > 0))


def _attn_fwd(qkvn, grp):
    S = qkvn.shape[0]
    d = SWA_GROUPS[grp][1]
    L = S // d
    nblk = L // BAND
    assert L % BAND == 0
    view = qkvn.reshape(L, d * 3 * A_WIDTH)

    def body(q_ref, kc_ref, kp_ref, vc_ref, vp_ref, o_ref, lse_ref):
        n = pl.program_id(1)
        valid = _band_mask(n)
        first = lax.broadcasted_iota(jnp.int32, (BAND, LANE), 1) < A_HEAD_DIM
        pairs = [slice(pr * LANE, (pr + 1) * LANE) for pr in range(A_WIDTH // LANE)]
        halves = (first, jnp.logical_not(first))
        qps = [q_ref[:, sl] for sl in pairs]
        kcats = [jnp.concatenate([kp_ref[:, sl], kc_ref[:, sl]], axis=0) for sl in pairs]
        vcats = [jnp.concatenate([vp_ref[:, sl], vc_ref[:, sl]], axis=0) for sl in pairs]
        heads = [(pr, m) for pr in range(len(pairs)) for m in halves]
        ss = [_dot(jnp.where(m, qps[pr], jnp.zeros_like(qps[pr])), kcats[pr], 1, 1) for pr, m in heads]
        ps, lses = [], []
        for s in ss:
            s = jnp.where(valid, s * (A_HEAD_DIM ** -0.5), -1e30)
            mx = jnp.max(s, axis=-1, keepdims=True)
            e = jnp.exp(s - mx)
            l = jnp.sum(e, axis=-1, keepdims=True)
            ps.append((e / l).astype(BF16))
            lses.append(mx + jnp.log(l))
        os_ = [_dot(p, vcats[pr]) for p, (pr, _) in zip(ps, heads)]
        o_ref[...] = jnp.concatenate([jnp.where(first, os_[2 * pr], os_[2 * pr + 1]) for pr in range(len(pairs))], axis=1)
        lse_ref[...] = jnp.concatenate([jnp.where(first, lses[2 * pr], lses[2 * pr + 1]) for pr in range(len(pairs))], axis=1)

    blk = (BAND, A_WIDTH)
    o, lse = pl.pallas_call(
        body, name=f"attn_fwd_g{grp}", grid=(d, nblk),
        in_specs=[pl.BlockSpec(blk, lambda r, n: (n, r * 3)),
                  pl.BlockSpec(blk, lambda r, n: (n, r * 3 + 1)),
                  pl.BlockSpec(blk, lambda r, n: (jnp.maximum(n - 1, 0), r * 3 + 1)),
                  pl.BlockSpec(blk, lambda r, n: (n, r * 3 + 2)),
                  pl.BlockSpec(blk, lambda r, n: (jnp.maximum(n - 1, 0), r * 3 + 2))],
        out_specs=[pl.BlockSpec(blk, lambda r, n: (n, r)), pl.BlockSpec(blk, lambda r, n: (n, r))],
        out_shape=[jax.ShapeDtypeStruct((L, d * A_WIDTH), F32)] * 2,
        compiler_params=_cparams(("parallel", "parallel")),
    )(view, view, view, view, view)
    return o.reshape(S, A_WIDTH), lse.reshape(S, A_WIDTH)


def _merge_weights(l0, l1, l2):
    mx = jnp.maximum(jnp.maximum(l0, l1), l2)
    e0, e1, e2 = jnp.exp(l0 - mx), jnp.exp(l1 - mx), jnp.exp(l2 - mx)
    inv = 1.0 / (e0 + e1 + e2)
    return e0 * inv, e1 * inv, e2 * inv


def _attn_merge(os_, lses):
    def fn(i, n, o0, o1, o2, l0, l1, l2):
        w0, w1, w2 = _merge_weights(l0, l1, l2)
        return (w0 * o0 + w1 * o1 + w2 * o2,)
    ins = [(a, "row") for a in (*os_, *lses)]
    return _rows("attn_merge", fn, ins, [(A_WIDTH, BF16)], tr=512)


def _attn_merge_bwd(do, os_, lses, bd1):
    def fn(i, n, dot_, o0, o1, o2, l0, l1, l2, b):
        w0, w1, w2 = _merge_weights(l0, l1, l2)
        o = w0 * o0 + w1 * o1 + w2 * o2
        dsum = _seg_sum(dot_ * o, b)
        return (w0 * dot_, w1 * dot_, w2 * dot_, -w0 * dsum, -w1 * dsum, -w2 * dsum)
    ins = [(do, "row")] + [(a, "row") for a in (*os_, *lses)] + [(bd1, "full")]
    res = _rows("attn_merge_bwd", fn, ins, [(A_WIDTH, BF16)] * 3 + [(A_WIDTH, F32)] * 3, tr=256)
    return res[:3], res[3:]


def _lane_pick(x, lane_idx, lane):
    return jnp.sum(jnp.where(lane_idx == lane, x, 0.0), axis=-1, keepdims=True)


def _attn_bwd(qkvn, grp, do_g, lse, c_g):
    S = qkvn.shape[0]
    d = SWA_GROUPS[grp][1]
    L = S // d
    nblk = L // BAND
    view = qkvn.reshape(L, d * 3 * A_WIDTH)
    dov, lsev, cv = (t.reshape(L, d * A_WIDTH) for t in (do_g, lse, c_g))

    def body(q_ref, kc_ref, kp_ref, vc_ref, vp_ref, do_ref, lse_ref, c_ref, dq_ref, dk_ref, dv_ref, ck, cv_):
        n = pl.program_id(1)

        @pl.when(n == 0)
        def _():
            ck[...] = jnp.zeros_like(ck)
            cv_[...] = jnp.zeros_like(cv_)

        @pl.when(n < nblk)
        def _():
            valid = _band_mask(n)
            lane = lax.broadcasted_iota(jnp.int32, (BAND, LANE), 1)
            first = lane < A_HEAD_DIM
            lane2 = lax.broadcasted_iota(jnp.int32, (2 * BAND, LANE), 1) < A_HEAD_DIM
            pairs = [slice(pr * LANE, (pr + 1) * LANE) for pr in range(A_WIDTH // LANE)]
            halves = (first, jnp.logical_not(first))
            qps = [q_ref[:, sl] for sl in pairs]
            dops = [do_ref[:, sl] for sl in pairs]
            kcats = [jnp.concatenate([kp_ref[:, sl], kc_ref[:, sl]], axis=0) for sl in pairs]
            vcats = [jnp.concatenate([vp_ref[:, sl], vc_ref[:, sl]], axis=0) for sl in pairs]
            heads = [(pr, hh) for pr in range(len(pairs)) for hh in range(2)]
            zero = jnp.zeros_like(qps[0])
            ss = [_dot(jnp.where(halves[hh], qps[pr], zero), kcats[pr], 1, 1) for pr, hh in heads]
            dps = [_dot(jnp.where(halves[hh], dops[pr], zero), vcats[pr], 1, 1) for pr, hh in heads]
            dss, pbs = [], []
            for (pr, hh), s, dp in zip(heads, ss, dps):
                lse_h = _lane_pick(lse_ref[:, pairs[pr]], lane, hh * A_HEAD_DIM)
                c_h = _lane_pick(c_ref[:, pairs[pr]], lane, hh * A_HEAD_DIM)
                p = jnp.where(valid, jnp.exp(s * (A_HEAD_DIM ** -0.5) - lse_h), 0.0)
                dss.append((p * (dp + c_h) * (A_HEAD_DIM ** -0.5)).astype(BF16))
                pbs.append(p.astype(BF16))
            dqs = [_dot(ds, kcats[pr]) for ds, (pr, _) in zip(dss, heads)]
            dks = [_dot(ds, qps[pr], 0, 0) for ds, (pr, _) in zip(dss, heads)]
            dvs = [_dot(pb, dops[pr], 0, 0) for pb, (pr, _) in zip(pbs, heads)]
            for pr, sl in enumerate(pairs):
                dq_ref[:, sl] = jnp.where(first, dqs[2 * pr], dqs[2 * pr + 1])
                dkc = jnp.where(lane2, dks[2 * pr], dks[2 * pr + 1])
                dvc = jnp.where(lane2, dvs[2 * pr], dvs[2 * pr + 1])
                dk_ref[:, sl] = ck[:, sl] + dkc[:BAND]
                dv_ref[:, sl] = cv_[:, sl] + dvc[:BAND]
                ck[:, sl] = dkc[BAND:]
                cv_[:, sl] = dvc[BAND:]

        @pl.when(n == nblk)
        def _():
            dk_ref[...] = ck[...]
            dv_ref[...] = cv_[...]

    blk = (BAND, A_WIDTH)
    last = nblk - 1
    qn = lambda n: jnp.minimum(n, last)
    pn = lambda n: jnp.clip(n - 1, 0, last)
    dq, dk, dv = pl.pallas_call(
        body, name=f"attn_bwd_g{grp}", grid=(d, nblk + 1),
        in_specs=[pl.BlockSpec(blk, lambda r, n: (qn(n), r * 3)),
                  pl.BlockSpec(blk, lambda r, n: (qn(n), r * 3 + 1)),
                  pl.BlockSpec(blk, lambda r, n: (pn(n), r * 3 + 1)),
                  pl.BlockSpec(blk, lambda r, n: (qn(n), r * 3 + 2)),
                  pl.BlockSpec(blk, lambda r, n: (pn(n), r * 3 + 2)),
                  pl.BlockSpec(blk, lambda r, n: (qn(n), r)),
                  pl.BlockSpec(blk, lambda r, n: (qn(n), r)),
                  pl.BlockSpec(blk, lambda r, n: (qn(n), r))],
        out_specs=[pl.BlockSpec(blk, lambda r, n: (qn(n), r)),
                   pl.BlockSpec(blk, lambda r, n: (pn(n), r)),
                   pl.BlockSpec(blk, lambda r, n: (pn(n), r))],
        out_shape=[jax.ShapeDtypeStruct((L, d * A_WIDTH), F32)] * 3,
        scratch_shapes=[pltpu.VMEM(blk, F32), pltpu.VMEM(blk, F32)],
        compiler_params=_cparams(("parallel", "arbitrary")),
    )(view, view, view, view, view, dov, lsev, cv)
    return tuple(t.reshape(S, A_WIDTH) for t in (dq, dk, dv))


def _attn_prep_bwd(qkv, grads, gains, ct, st, consts, bd):
    def fn(i, n, t, g, c_t, s_t, c, b, *gr):
        low = c[2:3, :] > 0.5
        in16 = c[3:4, :] > 0.5
        c_t, s_t = _rope_wide(c_t), _rope_wide(s_t)
        cols, dgs = [], []
        for grp in range(3):
            for which in range(3):
                dout = gr[grp * 3 + which]
                if which == 2:
                    cols.append(dout.astype(BF16))
                    continue
                off = (grp * 3 + which) * A_WIDTH
                x = t[:, off:off + A_WIDTH]
                gain = g[grp * 2 + which:grp * 2 + which + 1, :]
                r = lax.rsqrt(_seg_sum(x * x, b) + EPS)
                xh = x * r
                dy = _rope_apply_bwd(dout, c_t, s_t, low, in16)
                dyn = dy * gain
                dx = r * (dyn - xh * _seg_sum(dyn * xh, b))
                cols.append(dx.astype(BF16))
                dgs.append(_colsum(dy * xh))
        return (jnp.concatenate(cols, axis=1), *dgs)
    ins = [(qkv, "row"), (gains, "full"), (ct, "row"), (st, "row"), (consts, "full"), (bd, "full")] + [(a, "row") for a in grads]
    res = _rows("attn_prep_bwd", fn, ins, [(A_QKV, BF16)], tr=128, accs=[((1, A_WIDTH), F32)] * 6)
    return res[0], res[1:]


DN_QKV = 3 * DN_WIDTH
DN_QKVZ = DN_QKV + DN_WIDTH


def _sigmoid(x):
    return 1.0 / (1.0 + jnp.exp(-x))


def _softplus(x):
    return jnp.maximum(x, 0.0) + jnp.log(1.0 + jnp.exp(-jnp.abs(x)))


def _conv_taps(xs, w, tr):
    acc = None
    for j in range(CONV_W):
        sh = CONV_W - 1 - j
        term = (pltpu.roll(xs, sh, 0) if sh else xs)[SUBLANE:] * w[j:j + 1, :]
        acc = term if acc is None else acc + term
    return acc


def _dn_prep(qkvz, ab, convw, alog_row, dt_row):
    tr = 256

    def fn(i, n, x, xp, abt, w, al, dt):
        xp = jnp.where(i > 0, xp, 0.0)
        u = _conv_taps(jnp.concatenate([xp, x], axis=0), w, tr)
        y = u * _sigmoid(u)
        qs, ks = [], []
        for h in range(DN_HEADS):
            for dst, base, sc in ((qs, 0, DN_DIM ** -0.5), (ks, DN_WIDTH, 1.0)):
                seg = y[:, base + h * DN_DIM:base + (h + 1) * DN_DIM]
                dst.append(seg * (lax.rsqrt(jnp.sum(seg * seg, axis=-1, keepdims=True) + EPS) * sc))
        lane = lax.broadcasted_iota(jnp.int32, abt.shape, 1)
        g = -jnp.exp(al) * _softplus(abt + dt)
        gb = jnp.where(lane < DN_HEADS, g, jnp.where(lane < 2 * DN_HEADS, _sigmoid(abt), 0.0))
        return u, jnp.concatenate(qs, axis=1), jnp.concatenate(ks, axis=1), y[:, 2 * DN_WIDTH:], gb

    ins = [(qkvz, "row", (0, DN_QKV)), (qkvz, "prev8", (0, DN_QKV)), (ab, "row"), (convw, "full"),
           (alog_row, "full"), (dt_row, "full")]
    return _rows("dn_prep", fn, ins, [(DN_QKV, F32), (DN_WIDTH, F32), (DN_WIDTH, F32), (DN_WIDTH, F32), (LANE, F32)], tr=tr)


def _tri_masks():
    row = lax.broadcasted_iota(jnp.int32, (CHUNK, CHUNK), 0)
    col = lax.broadcasted_iota(jnp.int32, (CHUNK, CHUNK), 1)
    return row >= col, row > col, row == col


def _heads(fn, *lists):
    return [fn(*xs) for xs in zip(*lists)]


def _split(x):
    hi = x.astype(BF16)
    return hi, (x - hi.astype(F32)).astype(BF16)


def _dot3(a, b, ca=1, cb=0):
    (ah, al), (bh, bl) = a, b
    return _dot(ah, bh, ca, cb) + (_dot(ah, bl, ca, cb) + _dot(al, bh, ca, cb))


def _unit_lower_inverse(a_list, eye):
    ts = [eye - a for a in a_list]
    parts = [_split(a) for a in a_list]
    for _ in range(5):
        parts = [_split(_dot3(p, p)) for p in parts]
        ts = [t + _dot3(_split(t), p) for t, p in zip(ts, parts)]
    return ts


def _dn_terms(qs, ks, vs, gb):
    lower, strict, diag = _tri_masks()
    lane = lax.broadcasted_iota(jnp.int32, (CHUNK, LANE), 1)
    is_last = lax.broadcasted_iota(jnp.int32, (CHUNK, 1), 0) == CHUNK - 1
    hs = range(DN_HEADS)
    gc = _dot(lower.astype(F32), gb, precision=HIGHEST)
    gct = jnp.transpose(gc)
    bcol = [_lane_pick(gb, lane, DN_HEADS + h) for h in hs]
    gcol = [_lane_pick(gc, lane, h) for h in hs]
    glast = [jnp.sum(jnp.where(is_last, g, 0.0), axis=0, keepdims=True) for g in gcol]
    decay = [jnp.exp(jnp.where(lower, gcol[h] - gct[h:h + 1, :], -1e30)) for h in hs]
    kb = _heads(lambda k, b: k * b, ks, bcol)
    kk = _heads(lambda x, k: _bdot(x, k, 1, 1), kb, ks)
    qk = _heads(lambda q, k: _bdot(q, k, 1, 1), qs, ks)
    a = _heads(lambda x, d: jnp.where(strict, x * d, 0.0), kk, decay)
    t = [_split(x) for x in _unit_lower_inverse(a, diag.astype(F32))]
    eg = [jnp.exp(g) for g in gcol]
    egl = _heads(lambda gl, g: jnp.exp(gl - g), glast, gcol)
    rhs_w = _heads(lambda x, e: x * e, kb, eg)
    u = _heads(lambda tt, v, b: _dot3(tt, _split(v * b)), t, vs, bcol)
    w = _heads(lambda tt, r: _dot3(tt, _split(r)), t, rhs_w)
    return dict(bcol=bcol, decay=decay, kb=kb, a=a, t=t, eg=eg, egl=egl, rhs_w=rhs_w, u=u, w=w,
                attn=_heads(lambda x, d: x * d, qk, decay), q_dec=_heads(lambda q, e: q * e, qs, eg),
                k_dec=_heads(lambda k, e: k * e, ks, egl), c_dec=[jnp.exp(g) for g in glast],
                lower=lower, strict=strict, lane=lane, is_last=is_last)


def _head_slices(ref):
    return [ref[:, h * DN_DIM:(h + 1) * DN_DIM] for h in range(DN_HEADS)]


def _dn_chunk_fwd(q, k, v, gb):
    S = q.shape[0]
    N = S // CHUNK

    def body(q_ref, k_ref, v_ref, gb_ref, o_ref, st_ref, state):
        @pl.when(pl.program_id(0) == 0)
        def _():
            state[...] = jnp.zeros_like(state)

        f = _dn_terms(_head_slices(q_ref), _head_slices(k_ref), _head_slices(v_ref), gb_ref[...])
        s = [state[h] for h in range(DN_HEADS)]
        for h in range(DN_HEADS):
            st_ref[0, h] = s[h]
        sb = [x.astype(BF16) for x in s]
        v_new = _heads(lambda u, w, x: u - _bdot(w, x), f["u"], f["w"], sb)
        o = _heads(lambda qd, x, at, vn: _bdot(qd, x) + _bdot(at, vn), f["q_dec"], sb, f["attn"], v_new)
        new_s = _heads(lambda x, c, kd, vn: x * c + _bdot(kd, vn, 0, 0), s, f["c_dec"], f["k_dec"], v_new)
        for h in range(DN_HEADS):
            o_ref[:, h * DN_DIM:(h + 1) * DN_DIM] = o[h]
            state[h] = new_s[h]

    blk = pl.BlockSpec((CHUNK, DN_WIDTH), lambda n: (n, 0))
    st_blk = pl.BlockSpec((1, DN_HEADS, DN_DIM, DN_DIM), lambda n: (n, 0, 0, 0))
    return pl.pallas_call(
        body, name="dn_chunk_fwd", grid=(N,),
        in_specs=[blk, blk, blk, pl.BlockSpec((CHUNK, LANE), lambda n: (n, 0))],
        out_specs=[blk, st_blk],
        out_shape=[jax.ShapeDtypeStruct((S, DN_WIDTH), F32), jax.ShapeDtypeStruct((N, DN_HEADS, DN_DIM, DN_DIM), F32)],
        scratch_shapes=[pltpu.VMEM((DN_HEADS, DN_DIM, DN_DIM), F32)],
        compiler_params=_cparams(("arbitrary",)),
    )(q, k, v, gb)


def _dn_chunk_bwd(q, k, v, gb, states, do):
    S = q.shape[0]
    N = S // CHUNK

    def body(q_ref, k_ref, v_ref, gb_ref, st_ref, do_ref, dq_ref, dk_ref, dv_ref, dgb_ref, dstate):
        @pl.when(pl.program_id(0) == 0)
        def _():
            dstate[...] = jnp.zeros_like(dstate)

        hs = range(DN_HEADS)
        qs, ks, vs, dos = (_head_slices(r) for r in (q_ref, k_ref, v_ref, do_ref))
        f = _dn_terms(qs, ks, vs, gb_ref[...])
        lane, is_last = f["lane"], f["is_last"]
        rowsum = lambda x: jnp.sum(x, axis=-1, keepdims=True)
        s = [st_ref[0, h] for h in hs]
        dsn = [dstate[h] for h in hs]
        sb = [x.astype(BF16) for x in s]
        dsb = [x.astype(BF16) for x in dsn]
        dob = [x.astype(BF16) for x in dos]
        v_new = _heads(lambda u, w, x: u - _bdot(w, x), f["u"], f["w"], sb)
        dv_new = _heads(lambda at, d, kd, x: _bdot(at, d, 0, 0) + _bdot(kd, x), f["attn"], dob, f["k_dec"], dsb)
        dattn = _heads(lambda d, vn: _bdot(d, vn, 1, 1), dob, v_new)
        dq_dec = _heads(lambda d, x: _bdot(d, x, 1, 1), dob, sb)
        dk_dec = _heads(lambda vn, x: _bdot(vn, x, 1, 1), v_new, dsb)
        dw = _heads(lambda dv_, x: -_bdot(dv_, x, 1, 1), dv_new, sb)
        new_ds = _heads(lambda x, c, qd, d, w, dv_: x * c + _bdot(qd, d, 0, 0) - _bdot(w, dv_, 0, 0),
                        dsn, f["c_dec"], f["q_dec"], dob, f["w"], dv_new)
        for h in hs:
            dstate[h] = new_ds[h]
        drhs_u = _heads(lambda tt, x: _dot3(tt, _split(x), 0, 0), f["t"], dv_new)
        drhs_w = _heads(lambda tt, x: _dot3(tt, _split(x), 0, 0), f["t"], dw)
        da = _heads(lambda du_, u, dw_, w: jnp.where(f["strict"], -(_bdot(du_, u, 1, 1) + _bdot(dw_, w, 1, 1)), 0.0),
                    drhs_u, f["u"], drhs_w, f["w"])
        dkk = _heads(lambda x, d: x * d, da, f["decay"])
        dqk = _heads(lambda x, d: x * d, dattn, f["decay"])
        dkb = _heads(lambda x, k_, dw_, e: _bdot(x, k_) + dw_ * e, dkk, ks, drhs_w, f["eg"])
        dq = _heads(lambda x, k_, dqd, e: _bdot(x, k_) + dqd * e, dqk, ks, dq_dec, f["eg"])
        dk = _heads(lambda x, kb_, y, q_, dkd, el, dkb_, b: _bdot(x, kb_, 0, 0) + _bdot(y, q_, 0, 0) + dkd * el + dkb_ * b,
                    dkk, f["kb"], dqk, qs, dk_dec, f["egl"], dkb, f["bcol"])
        m = _heads(lambda x, a_, y, at: x * a_ + y * at, da, f["a"], dattn, f["attn"])
        ones = jnp.ones((CHUNK, LANE), BF16)
        col_m = [(_dot(mh, ones, 0, 0) + _dot(ml, ones, 0, 0))[:, 0:1] for mh, ml in map(_split, m)]
        dgc_all = jnp.zeros((CHUNK, LANE), F32)
        dbeta_all = jnp.zeros((CHUNK, LANE), F32)
        for h in hs:
            dq_ref[:, h * DN_DIM:(h + 1) * DN_DIM] = dq[h]
            dk_ref[:, h * DN_DIM:(h + 1) * DN_DIM] = dk[h]
            dv_ref[:, h * DN_DIM:(h + 1) * DN_DIM] = drhs_u[h] * f["bcol"][h]
            kdec_term = rowsum(dk_dec[h] * f["k_dec"][h])
            dc_dec = _sum_all(dsn[h] * s[h])
            dgc = (rowsum(m[h]) - col_m[h] + rowsum(dq_dec[h] * f["q_dec"][h]) - kdec_term
                   + rowsum(drhs_w[h] * f["rhs_w"][h]))
            last_extra = jnp.sum(kdec_term, axis=0, keepdims=True) + dc_dec * f["c_dec"][h]
            dgc = dgc + jnp.where(is_last, last_extra, 0.0)
            dbeta = rowsum(drhs_u[h] * vs[h]) + rowsum(dkb[h] * ks[h])
            dgc_all = jnp.where(lane == h, dgc, dgc_all)
            dbeta_all = jnp.where(lane == DN_HEADS + h, dbeta, dbeta_all)
        dg_all = _dot(f["lower"].astype(F32), dgc_all, 0, 0, precision=HIGHEST)
        dgb_ref[...] = jnp.where(lane < DN_HEADS, dg_all, dbeta_all)

    rev = lambda n: (N - 1 - n, 0)
    blk = pl.BlockSpec((CHUNK, DN_WIDTH), rev)
    gblk = pl.BlockSpec((CHUNK, LANE), rev)
    st_blk = pl.BlockSpec((1, DN_HEADS, DN_DIM, DN_DIM), lambda n: (N - 1 - n, 0, 0, 0))
    return pl.pallas_call(
        body, name="dn_chunk_bwd", grid=(N,),
        in_specs=[blk, blk, blk, gblk, st_blk, blk],
        out_specs=[blk, blk, blk, gblk],
        out_shape=[jax.ShapeDtypeStruct((S, DN_WIDTH), F32)] * 3 + [jax.ShapeDtypeStruct((S, LANE), F32)],
        scratch_shapes=[pltpu.VMEM((DN_HEADS, DN_DIM, DN_DIM), F32)],
        compiler_params=_cparams(("arbitrary",)),
    )(q, k, v, gb, states, do)


def _dn_post(o, qkvz, gain_row):
    def fn(i, n, ot, z, g):
        cols = []
        for h in range(DN_HEADS):
            seg = ot[:, h * DN_DIM:(h + 1) * DN_DIM]
            cols.append(seg * lax.rsqrt(jnp.mean(seg * seg, axis=-1, keepdims=True) + EPS) * g)
        return (jnp.concatenate(cols, axis=1) * (z * _sigmoid(z)),)
    return _rows("dn_post", fn, [(o, "row"), (qkvz, "row", (3, DN_WIDTH)), (gain_row, "full")], [(DN_WIDTH, BF16)], tr=512)


def _dn_post_bwd(don, o, qkvz, gain_row):
    def fn(i, n, dy, ot, z, g):
        sg = _sigmoid(z)
        sz = z * sg
        dos, ohs = [], []
        dg = jnp.zeros((1, DN_DIM), F32)
        for h in range(DN_HEADS):
            sl = slice(h * DN_DIM, (h + 1) * DN_DIM)
            seg = ot[:, sl]
            r = lax.rsqrt(jnp.mean(seg * seg, axis=-1, keepdims=True) + EPS)
            oh = seg * r
            dno = dy[:, sl] * sz[:, sl]
            dg = dg + _colsum(dno * oh)
            dn = dno * g
            dos.append(r * (dn - oh * jnp.mean(dn * oh, axis=-1, keepdims=True)))
            ohs.append(oh * g)
        dz = dy * jnp.concatenate(ohs, axis=1) * (sg * (1.0 + z * (1.0 - sg)))
        return jnp.concatenate(dos, axis=1), dz, dg
    ins = [(don, "row"), (o, "row"), (qkvz, "row", (3, DN_WIDTH)), (gain_row, "full")]
    return _rows("dn_post_bwd", fn, ins, [(DN_WIDTH, F32), (DN_WIDTH, F32)], tr=256, accs=[((1, DN_DIM), F32)])


def _dn_prep_bwd(dq, dk, dv, dgb, u, ab, alog_row, dt_row):
    def fn(i, n, dqt, dkt, dvt, dgbt, ut, abt, al, dt):
        sg = _sigmoid(ut)
        y = ut * sg
        dys = []
        for grad, base, sc in ((dqt, 0, DN_DIM ** -0.5), (dkt, DN_WIDTH, 1.0)):
            for h in range(DN_HEADS):
                seg = y[:, base + h * DN_DIM:base + (h + 1) * DN_DIM]
                gr = grad[:, h * DN_DIM:(h + 1) * DN_DIM]
                r = lax.rsqrt(jnp.sum(seg * seg, axis=-1, keepdims=True) + EPS)
                xh = seg * r
                dys.append((r * sc) * (gr - xh * jnp.sum(gr * xh, axis=-1, keepdims=True)))
        dy = jnp.concatenate(dys + [dvt], axis=1)
        du = dy * (sg * (1.0 + ut * (1.0 - sg)))
        lane = lax.broadcasted_iota(jnp.int32, abt.shape, 1)
        is_g = lane < DN_HEADS
        ea = jnp.exp(al)
        x = abt + dt
        slope = -ea * _sigmoid(x)
        gval = -ea * _softplus(x)
        dg = jnp.where(is_g, dgbt, 0.0)
        beta = _sigmoid(abt)
        dab = jnp.where(is_g, dg * slope, jnp.where(lane < 2 * DN_HEADS, dgbt * beta * (1.0 - beta), 0.0))
        return du, dab, _colsum(dg * gval), _colsum(dg * slope)
    ins = [(dq, "row"), (dk, "row"), (dv, "row"), (dgb, "row"), (u, "row"), (ab, "row"), (alog_row, "full"), (dt_row, "full")]
    return _rows("dn_prep_bwd", fn, ins, [(DN_QKV, F32), (LANE, BF16)], tr=256, accs=[((1, LANE), F32)] * 2)


def _dn_conv_bwd(du, dz, qkvz, convw):
    tr = 256

    def fn(i, n, dut, dun, dzt, x, xp, w):
        dun = jnp.where(i < n - 1, dun, 0.0)
        dus = jnp.concatenate([dut, dun], axis=0)
        xs = jnp.concatenate([jnp.where(i > 0, xp, 0.0), x], axis=0)
        dx = None
        dws = []
        for j in range(CONV_W):
            sh = CONV_W - 1 - j
            term = (pltpu.roll(dus, tr + SUBLANE - sh, 0) if sh else dus)[:tr] * w[j:j + 1, :]
            dx = term if dx is None else dx + term
            dws.append(_colsum(dut * (pltpu.roll(xs, sh, 0) if sh else xs)[SUBLANE:]))
        return (jnp.concatenate([dx.astype(BF16), dzt.astype(BF16)], axis=1), *dws)

    ins = [(du, "row"), (du, "next8"), (dz, "row"), (qkvz, "row", (0, DN_QKV)), (qkvz, "prev8", (0, DN_QKV)), (convw, "full")]
    res = _rows("dn_conv_bwd", fn, ins, [(DN_QKVZ, BF16)], tr=tr, accs=[((1, DN_QKV), F32)] * CONV_W)
    return res[0], res[1:]


def _add(acc, r):
    return (r + acc,)


def _mlp_ple_fwd(i, x1, hm, p_i, ple_gain, next_gain, w_up, w_down, w_ple, w_gate):
    u, a = _mm(f"mlp_up{i}", hm, w_up, epilogue=lambda acc: (acc, jnp.square(jnp.maximum(acc, 0.0))),
               out_dtypes=(BF16, BF16))
    x2, hp = _mm(f"mlp_down{i}", a, w_down, epilogue=_res_norm, extras=(x1, ple_gain), out_dtypes=(F32, BF16),
                 tm_pref=FUSED_ROWS)
    pp = _mm(f"ple_proj{i}", p_i, w_ple)

    def gate_epilogue(acc, x2t, ppt, *g):
        gate = _sigmoid(acc)
        x3 = x2t + ppt * gate
        if not g:
            return x3, gate
        return x3, gate, x3 * lax.rsqrt(jnp.mean(x3 * x3, axis=-1, keepdims=True) + EPS) * g[0]

    more = () if next_gain is None else (next_gain,)
    x3, gate, *h_next = _mm(f"ple_gate{i}", hp, w_gate, epilogue=gate_epilogue, extras=(x2, pp) + more,
                            out_dtypes=(F32, F32) + (BF16,) * len(more), tm_pref=FUSED_ROWS)
    return x3, (h_next[0] if more else None), dict(x1=x1, hm=hm, u=u, a=a, x2=x2, hp=hp, pp=pp, gate=gate, p=p_i)


def _mlp_ple_bwd(i, dx3, sv, mlp_gain, ple_gain, w_up, w_down, w_gate):
    def fn(_i, _n, d, g, pp):
        return d * g, d * pp * g * (1.0 - g)
    dpp, dzg = _rows(f"ple_gate_bwd{i}", fn, [(dx3, "row"), (sv["gate"], "row"), (sv["pp"], "row")],
                     [(D_MODEL, BF16), (D_MODEL, BF16)], tr=512)
    d_w_ple = _mm(f"ple_proj_dw{i}", sv["p"], dpp, ta=True, out_dtypes=(BF16,))
    d_w_gate = _mm(f"ple_gate_dw{i}", sv["hp"], dzg, ta=True, out_dtypes=(BF16,))
    dx2, dx2b, d_ple_gain = _mm(f"ple_gate_dx{i}", dzg, w_gate, tb=True, epilogue=_norm_bwd_2,
                                extras=(sv["x2"], ple_gain, dx3), out_dtypes=(F32, BF16), n_colsums=1, tm_pref=FUSED_ROWS)
    d_w_down = _mm(f"mlp_down_dw{i}", sv["a"], dx2b, ta=True, out_dtypes=(BF16,))
    du = _mm(f"mlp_down_dx{i}", dx2b, w_down, tb=True,
             epilogue=lambda acc, ut: (acc * (2.0 * jnp.maximum(ut.astype(F32), 0.0)),), extras=(sv["u"],), out_dtypes=(BF16,))
    d_w_up = _mm(f"mlp_up_dw{i}", sv["hm"], du, ta=True, out_dtypes=(BF16,))
    dx1, dx1b, d_mlp_gain = _mm(f"mlp_up_dx{i}", du, w_up, tb=True, epilogue=_norm_bwd_2,
                                extras=(sv["x1"], mlp_gain, dx2), out_dtypes=(F32, BF16), n_colsums=1, tm_pref=FUSED_ROWS)
    return dx1, dx1b, dict(w_ple=d_w_ple, w_ple_gate=d_w_gate, w_down=d_w_down, w_up=d_w_up,
                           ple_norm=d_ple_gain, mlp_norm=d_mlp_gain)


def _loss_fwd_bwd(y, target):
    D = y.shape[1]

    def fn(i, n, yt, tt):
        e = yt - tt
        return e * (1.0 / D), _colsum(e * e)
    dy, sq = _rows("loss", fn, [(y, "row"), (target, "row")], [(D, F32)], tr=512, accs=[((1, D), F32)])
    return sq, dy


def _after(small, token):
    return small + token[0:1, 0:1]


def _local_step(x, p, positions, target, W, P, rest_of_weights, send_layer1, send_mlp0, send_attn):
    consts = _head_consts()
    bd = _block_diag(1.0 / A_HEAD_DIM)
    bd1 = _block_diag(1.0)
    ct, st = _rope_tables(positions, consts)
    gains = jnp.stack([jnp.tile(v, A_HEADS) for g in range(3) for v in (P["attn_q_gain"][g], P["attn_k_gain"][g])])
    pad = LANE - DN_HEADS
    alog_row = jnp.pad(P["dn_a_log"].reshape(1, DN_HEADS), ((0, 0), (0, pad)))
    dt_row = jnp.pad(P["dn_dt_bias"].reshape(1, DN_HEADS), ((0, 0), (0, pad)))
    ogain_row = P["dn_o_gain"].reshape(1, DN_DIM)
    row = lambda name, i: P[name][i:i + 1]

    h0 = _rmsnorm_fwd("mix_norm0", x, row("mix_norm", 0))
    qkv = _mm("attn_qkv", h0, W["attn_w_qkv"])
    qkvn = _attn_prep(qkv, gains, ct, st, consts, bd)
    os_, lses = zip(*[_attn_fwd(qkvn[g], g) for g in range(3)])
    o_attn = _attn_merge(os_, lses)
    x1, hm0 = _mm("attn_out", o_attn, W["attn_w_o"], epilogue=_res_norm, extras=(x, row("mlp_norm", 0)),
                  out_dtypes=(F32, BF16), tm_pref=FUSED_ROWS)
    W = {**W, **rest_of_weights(x1)}
    x3, h1, sv0 = _mlp_ple_fwd(0, x1, hm0, p[0], row("ple_norm", 0), row("mix_norm", 1),
                               W["w_up"][0], W["w_down"][0], W["w_ple"][0], W["w_ple_gate"][0])
    qkvz = _mm("dn_in_qkvz", h1, W["dn_w_qkvz"])
    ab = _mm("dn_in_ab", h1, W["dn_w_ab"])
    u, q, k, v, gb = _dn_prep(qkvz, ab, W["dn_conv"], alog_row, dt_row)
    o_dn, states = _dn_chunk_fwd(q, k, v, gb)
    on = _dn_post(o_dn, qkvz, ogain_row)
    x4, hm1 = _mm("dn_out", on, W["dn_w_o"], epilogue=_res_norm, extras=(x3, row("mlp_norm", 1)),
                  out_dtypes=(F32, BF16), tm_pref=FUSED_ROWS)
    x6, _, sv1 = _mlp_ple_fwd(1, x4, hm1, p[1], row("ple_norm", 1), None,
                              W["w_up"][1], W["w_down"][1], W["w_ple"][1], W["w_ple_gate"][1])
    sq, dy = _loss_fwd_bwd(x6, target)

    dx4, dx4b, g1 = _mlp_ple_bwd(1, dy, sv1, row("mlp_norm", 1), row("ple_norm", 1),
                                 W["w_up"][1], W["w_down"][1], W["w_ple_gate"][1])
    don = _mm("dn_out_dx", dx4b, W["dn_w_o"], tb=True)
    d_dn_w_o = _mm("dn_out_dw", on, dx4b, ta=True, out_dtypes=(BF16,))
    do_dn, dz, d_ogain = _dn_post_bwd(don, o_dn, qkvz, ogain_row)
    dq, dk, dv, dgb = _dn_chunk_bwd(q, k, v, gb, states, do_dn)
    du, dab, d_alog, d_dt = _dn_prep_bwd(dq, dk, dv, dgb, u, ab, alog_row, dt_row)
    dqkvz, d_conv = _dn_conv_bwd(du, dz, qkvz, W["dn_conv"])
    dh1 = _mm("dn_in_ab_dx", dab, W["dn_w_ab"], tb=True)
    dx3, d_mix1 = _mm("dn_in_qkvz_dx", dqkvz, W["dn_w_qkvz"], tb=True,
                      epilogue=lambda acc, part, xt, g, dres: _norm_bwd(acc + part, xt, g, dres),
                      extras=(dh1, x3, row("mix_norm", 1), dx4), n_colsums=1, tm_pref=FUSED_ROWS)
    d_w_qkvz = _mm("dn_in_qkvz_dw", h1, dqkvz, ta=True, out_dtypes=(BF16,))
    d_w_ab = _mm("dn_in_ab_dw", h1, dab, ta=True, out_dtypes=(BF16,))
    token = send_layer1(dict(
        dn_w_qkvz=d_w_qkvz, dn_w_ab=d_w_ab, dn_conv=jnp.concatenate(d_conv, 0), dn_w_o=d_dn_w_o,
        w_up=g1["w_up"], w_down=g1["w_down"], w_ple=g1["w_ple"], w_ple_gate=g1["w_ple_gate"]))
    dx1, dx1b, g0 = _mlp_ple_bwd(0, dx3, sv0, row("mlp_norm", 0), _after(row("ple_norm", 0), token),
                                 W["w_up"][0], W["w_down"][0], W["w_ple_gate"][0])
    token = send_mlp0(dict(w_up=g0["w_up"], w_down=g0["w_down"], w_ple=g0["w_ple"], w_ple_gate=g0["w_ple_gate"]))
    do_attn = _mm("attn_out_dx", dx1b, W["attn_w_o"], tb=True, epilogue=_add, extras=(_after(jnp.zeros((1, A_WIDTH), F32), token),))
    d_attn_w_o = _mm("attn_out_dw", o_attn, dx1b, ta=True, out_dtypes=(BF16,))
    dos, cs = _attn_merge_bwd(do_attn, os_, lses, bd1)
    grads9 = []
    for g in range(3):
        grads9 += list(_attn_bwd(qkvn[g], g, dos[g], lses[g], cs[g]))
    dqkv, dgains = _attn_prep_bwd(qkv, grads9, gains, ct, st, consts, bd)
    d_attn_w_qkv = _mm("attn_qkv_dw", h0, dqkv, ta=True, out_dtypes=(BF16,))
    token = send_attn(dict(attn_w_qkv=d_attn_w_qkv, attn_w_o=d_attn_w_o))
    dx0, d_mix0 = _mm("attn_qkv_dx", dqkv, W["attn_w_qkv"], tb=True, epilogue=_norm_bwd,
                      extras=(x, _after(row("mix_norm", 0), token), dx1), n_colsums=1, tm_pref=FUSED_ROWS)

    dg = jnp.stack([t.reshape(A_HEADS, A_HEAD_DIM).sum(0) for t in dgains])
    small = dict(
        mix_norm=jnp.concatenate([d_mix0, d_mix1], 0),
        attn_q_gain=dg[0::2][None], attn_k_gain=dg[1::2][None],
        dn_a_log=d_alog[:, :DN_HEADS], dn_dt_bias=d_dt[:, :DN_HEADS], dn_o_gain=d_ogain,
        mlp_norm=jnp.concatenate([g0["mlp_norm"], g1["mlp_norm"]], 0),
        ple_norm=jnp.concatenate([g0["ple_norm"], g1["ple_norm"]], 0),
    )
    return sq, dx0, small


MESH_IDS = pl.DeviceIdType.MESH
ANY = pl.BlockSpec(memory_space=pl.ANY)


def _place():
    return lax.axis_index("x"), lax.axis_index("y"), lax.axis_index("c")


def _sem_scratch(n_streams):
    return [pltpu.SemaphoreType.DMA((n_streams, N_DEV - 1)), pltpu.SemaphoreType.DMA((n_streams, N_DEV - 1)),
            pltpu.SemaphoreType.DMA((n_streams,))]


def _all_gather(name, arrays, streams):
    n_in, n_st = len(arrays), len(streams)
    shapes = [arrays[a].shape if li is None else arrays[a].shape[1:] for a, li in streams]

    def body(*refs):
        in_refs, out_refs, token = refs[:n_in], refs[n_in:n_in + n_st], refs[n_in + n_st]
        send_sems, recv_sems, local_sems = refs[n_in + n_st + 1:]
        token[...] = jnp.zeros_like(token)
        x, y, c = _place()
        me, sibling = (x, y, c), (x, y, 1 - c)
        chips = [(1 - x, y), (x, 1 - y), (1 - x, 1 - y)]

        def copy(s, k, block, to, own=False):
            a, li = streams[s]
            dst = out_refs[s].at[4 * block[0] + 2 * block[1] + block[2]]
            src = (in_refs[a] if li is None else in_refs[a].at[li]) if own else dst
            return pltpu.make_async_remote_copy(src_ref=src, dst_ref=dst, send_sem=send_sems.at[s, k],
                                                recv_sem=recv_sems.at[s, k], device_id=to, device_id_type=MESH_IDS)

        started = []
        for s, (a, li) in enumerate(streams):
            src = in_refs[a] if li is None else in_refs[a].at[li]
            mine = pltpu.make_async_copy(src, out_refs[s].at[4 * x + 2 * y + c], local_sems.at[s])
            mine.start()
            started.append(mine)
        sends = []
        for s in range(n_st):
            first = [copy(s, 0, me, sibling, own=True)]
            first += [copy(s, 1 + j, me, (*chip, c), own=True) for j, chip in enumerate(chips)]
            for cp in first:
                cp.start()
            sends += first
        for j, chip in enumerate(chips):
            for s in range(n_st):
                copy(s, 1 + j, (*chip, c), me).wait_recv()
                fwd = copy(s, 4 + j, (*chip, c), sibling)
                fwd.start()
                sends.append(fwd)
        for s in range(n_st):
            copy(s, 0, sibling, me).wait_recv()
            for j, chip in enumerate(chips):
                copy(s, 4 + j, (*chip, 1 - c), me).wait_recv()
        for cp in sends:
            cp.wait_send()
        for cp in started:
            cp.wait()

    res = pl.pallas_call(
        body, name=name,
        out_shape=[jax.ShapeDtypeStruct((N_DEV,) + tuple(sh), arrays[a].dtype) for sh, (a, _) in zip(shapes, streams)]
        + [jax.ShapeDtypeStruct((SUBLANE, LANE), F32)],
        in_specs=[ANY] * n_in, out_specs=[ANY] * n_st + [pl.BlockSpec(memory_space=pltpu.VMEM)],
        scratch_shapes=_sem_scratch(n_st),
    )(*arrays)
    return list(res[:n_st]), res[n_st]


HBM = pl.BlockSpec(memory_space=pltpu.HBM)
SEM = pl.BlockSpec(memory_space=pltpu.SEMAPHORE)
FLOWS = pltpu.CompilerParams(has_side_effects=pltpu.SideEffectType.DATAFLOW_SIDE_EFFECTING)


def _in_hbm(a):
    return pltpu.with_memory_space_constraint(a, pltpu.HBM)


def _hbm_like(a):
    return pltpu.HBM(a.shape, a.dtype)


def _peers(x, y, c):
    return [(1 - x if k & 4 else x, 1 - y if k & 2 else y, 1 - c if k & 1 else c) for k in range(1, N_DEV)]


def _start_copies(name, n_remote, n_own, make_copies, operands):
    n = len(operands)

    def body(*refs):
        for cp in make_copies(refs[:n], refs[n], refs[n + 1], refs[n + 2]):
            cp.start()
        refs[-1][...] = jnp.zeros_like(refs[-1])

    res = pl.pallas_call(
        body, name=name,
        out_shape=(pltpu.SemaphoreType.DMA((n_remote,)), pltpu.SemaphoreType.DMA((n_remote,)), pltpu.SemaphoreType.DMA((n_own,)),
                   *[_hbm_like(t) for t in operands], jax.ShapeDtypeStruct((SUBLANE, LANE), F32)),
        in_specs=[HBM] * n, out_specs=(SEM, SEM, SEM, *[HBM] * n, pl.BlockSpec(memory_space=pltpu.VMEM)),
        input_output_aliases={i: 3 + i for i in range(n)}, compiler_params=FLOWS,
    )(*[_in_hbm(t) for t in operands])
    return res[:3], list(res[3:3 + n]), res[-1]


def _wait_copies(name, make_waits, sems, operands, after):
    n = len(operands)

    def body(*refs):
        for wait in make_waits(refs[:n], refs[n], refs[n + 1], refs[n + 2]):
            wait()

    res = pl.pallas_call(
        body, name=name, out_shape=tuple(_hbm_like(t) for t in operands),
        in_specs=[HBM] * n + [SEM, SEM, SEM, ANY], out_specs=tuple([HBM] * n),
        input_output_aliases={i: i for i in range(n)}, compiler_params=FLOWS,
    )(*operands, *sems, after)
    return list(res)


def _gather_plan(n_in, streams):
    def block(arr, s):
        a, li = streams[s]
        return arr[a] if li is None else arr[a].at[li]

    def copies(refs, send_sems, recv_sems, own_sems, arrivals=False):
        arr, land = refs[:n_in], refs[n_in:]
        x, y, c = _place()
        me = 4 * x + 2 * y + c
        out = []
        for s in range(len(streams)):
            out.append(("own", pltpu.make_async_copy(block(arr, s), land[s].at[me], own_sems.at[s])))
            for k, (px, py, pc) in enumerate(_peers(x, y, c)):
                out.append(("remote", pltpu.make_async_remote_copy(
                    src_ref=block(arr, s), dst_ref=land[s].at[4 * px + 2 * py + pc if arrivals else me],
                    send_sem=send_sems.at[s * (N_DEV - 1) + k], recv_sem=recv_sems.at[s * (N_DEV - 1) + k],
                    device_id=(px, py, pc), device_id_type=MESH_IDS)))
        return out
    return copies


def _exchange_plan(n_st):
    def copies(refs, send_sems, recv_sems, own_sems, arrivals=False):
        snd, rcv = refs[:n_st], refs[n_st:]
        x, y, c = _place()
        me = 4 * x + 2 * y + c
        out = []
        for s in range(n_st):
            out.append(("own", pltpu.make_async_copy(snd[s].at[me], rcv[s].at[me], own_sems.at[s])))
            for k, (px, py, pc) in enumerate(_peers(x, y, c)):
                peer = 4 * px + 2 * py + pc
                out.append(("remote", pltpu.make_async_remote_copy(
                    src_ref=snd[s].at[peer], dst_ref=rcv[s].at[peer if arrivals else me],
                    send_sem=send_sems.at[s * (N_DEV - 1) + k], recv_sem=recv_sems.at[s * (N_DEV - 1) + k],
                    device_id=(px, py, pc), device_id_type=MESH_IDS)))
        return out
    return copies


def _split_transfer(tag, plan, n_streams, operands):
    sems, operands, token = _start_copies(f"{tag}_start", n_streams * (N_DEV - 1), n_streams,
                                          lambda refs, a, b, o: [cp for _, cp in plan(refs, a, b, o)], operands)

    def waits(refs, a, b, o):
        out = []
        for kind, cp in plan(refs, a, b, o, arrivals=True):
            out += [cp.wait] if kind == "own" else [cp.wait_send, cp.wait_recv]
        return out

    return (lambda after: _wait_copies(f"{tag}_wait", waits, sems, operands, after)), token


def _gather_async(tag, arrays, streams):
    lands = [lax.empty((N_DEV,) + tuple(arrays[a].shape if li is None else arrays[a].shape[1:]), arrays[a].dtype)
             for a, li in streams]
    finish, token = _split_transfer(tag, _gather_plan(len(arrays), streams), len(streams), list(arrays) + lands)
    return (lambda after: finish(after)[len(arrays):]), token


def _exchange_async(tag, sends):
    recvs = [lax.empty(t.shape, t.dtype) for t in sends]
    finish, token = _split_transfer(tag, _exchange_plan(len(sends)), len(sends), list(sends) + recvs)
    return (lambda after: finish(after)[len(sends):]), token


def _dn_in_pieces():
    n = (DN_QKVZ + 2 * DN_HEADS) // N_DEV
    segs = ((0, DN_QKV, 0, 0), (DN_QKV, DN_QKV + 2 * DN_HEADS, 1, 0), (DN_QKV + 2 * DN_HEADS, DN_QKVZ + 2 * DN_HEADS, 0, DN_QKV))
    out = []
    for d in range(N_DEV):
        lo, hi = d * n, (d + 1) * n
        for s0, s1, tgt, t0 in segs:
            a, b = max(lo, s0), min(hi, s1)
            if a < b:
                out.append((d, a - lo, b - lo, tgt, t0 + a - s0))
    return out


def _unpack_cols(name, g):
    _, K, n = g.shape
    tr = 256

    def body(g_ref, o_ref):
        for d in range(N_DEV):
            o_ref[:, d * n:(d + 1) * n] = g_ref[d]

    return pl.pallas_call(
        body, name=name, grid=(K // tr,), in_specs=[pl.BlockSpec((N_DEV, tr, n), lambda i: (0, i, 0))],
        out_specs=pl.BlockSpec((tr, N_DEV * n), lambda i: (i, 0)),
        out_shape=jax.ShapeDtypeStruct((K, N_DEV * n), g.dtype), compiler_params=_cparams(("parallel",)),
    )(g)


def _pack_cols(name, w):
    K, n = w.shape[0], w.shape[1] // N_DEV
    tr = 256

    def body(w_ref, o_ref):
        for d in range(N_DEV):
            o_ref[d] = w_ref[:, d * n:(d + 1) * n]

    return pl.pallas_call(
        body, name=name, grid=(K // tr,), in_specs=[pl.BlockSpec((tr, N_DEV * n), lambda i: (i, 0))],
        out_specs=pl.BlockSpec((N_DEV, tr, n), lambda i: (0, i, 0)),
        out_shape=jax.ShapeDtypeStruct((N_DEV, K, n), w.dtype), compiler_params=_cparams(("parallel",)),
    )(w)


def _unpack_dn_in(g):
    _, K, n = g.shape
    tr = 256

    def body(g_ref, qkvz_ref, ab_ref):
        ab_ref[...] = jnp.zeros_like(ab_ref)
        for d, c0, c1, tgt, t0 in _dn_in_pieces():
            (qkvz_ref, ab_ref)[tgt][:, t0:t0 + c1 - c0] = g_ref[d, :, c0:c1]

    return pl.pallas_call(
        body, name="unpack_dn_in", grid=(K // tr,), in_specs=[pl.BlockSpec((N_DEV, tr, n), lambda i: (0, i, 0))],
        out_specs=[pl.BlockSpec((tr, DN_QKVZ), lambda i: (i, 0)), pl.BlockSpec((tr, LANE), lambda i: (i, 0))],
        out_shape=[jax.ShapeDtypeStruct((K, DN_QKVZ), g.dtype), jax.ShapeDtypeStruct((K, LANE), g.dtype)],
        compiler_params=_cparams(("parallel",)),
    )(g)


def _pack_dn_in(d_qkvz, d_ab):
    K = d_qkvz.shape[0]
    n = (DN_QKVZ + 2 * DN_HEADS) // N_DEV
    tr = 256

    def body(qkvz_ref, ab_ref, o_ref):
        for d, c0, c1, tgt, t0 in _dn_in_pieces():
            o_ref[d, :, c0:c1] = (qkvz_ref, ab_ref)[tgt][:, t0:t0 + c1 - c0]

    return pl.pallas_call(
        body, name="pack_dn_in", grid=(K // tr,),
        in_specs=[pl.BlockSpec((tr, DN_QKVZ), lambda i: (i, 0)), pl.BlockSpec((tr, LANE), lambda i: (i, 0))],
        out_specs=pl.BlockSpec((N_DEV, tr, n), lambda i: (0, i, 0)),
        out_shape=jax.ShapeDtypeStruct((N_DEV, K, n), d_qkvz.dtype), compiler_params=_cparams(("parallel",)),
    )(d_qkvz, d_ab)


ADAMW_ROWS = 256


def _adamw(name, parts, w, m, v):
    R, C = w.shape
    tr = min(R, ADAMW_ROWS)
    assert R % tr == 0 and parts.shape == (N_DEV, R, C)
    c1 = 1.0 - B1 ** STEP
    c2 = 1.0 - B2 ** STEP

    def body(p_ref, w_ref, m_ref, v_ref, g_ref, d_ref, nm_ref, nv_ref):
        g = p_ref[0].astype(F32)
        for dev in range(1, N_DEV):
            g = g + p_ref[dev].astype(F32)
        nm = B1 * m_ref[...] + (1.0 - B1) * g
        nv = B2 * v_ref[...] + (1.0 - B2) * jnp.square(g)
        g_ref[...] = g
        nm_ref[...] = nm
        nv_ref[...] = nv
        d_ref[...] = -LR * ((nm / c1) / (jnp.sqrt(nv / c2) + ADAM_EPS) + WD * w_ref[...])

    blk = pl.BlockSpec((tr, C), lambda i: (i, 0))
    return pl.pallas_call(
        body, name=name, grid=(R // tr,),
        in_specs=[pl.BlockSpec((N_DEV, tr, C), lambda i: (0, i, 0)), blk, blk, blk],
        out_specs=[blk] * 4, out_shape=[jax.ShapeDtypeStruct((R, C), F32)] * 4,
        compiler_params=_cparams(("parallel",)),
    )(parts, w, m, v)


SMALL = ("mix_norm", "attn_q_gain", "attn_k_gain", "dn_a_log", "dn_dt_bias", "dn_o_gain", "mlp_norm", "ple_norm")
WEIGHTS = ("mix_norm", "attn_w_qkv", "attn_q_gain", "attn_k_gain", "attn_w_o", "dn_w_in", "dn_conv", "dn_a_log",
           "dn_dt_bias", "dn_o_gain", "dn_w_o", "mlp_norm", "w_up", "w_down", "ple_norm", "w_ple", "w_ple_gate")


def _to_rows(flat, multiple):
    n = flat.shape[-1]
    rows = -(-n // (LANE * multiple)) * multiple
    return jnp.pad(flat, [(0, rows * LANE - n)]).reshape(rows, LANE)


def _cols_to_devices(w):
    K, N = w.shape
    return jnp.transpose(w.reshape(K, N_DEV, N // N_DEV), (1, 0, 2))


def _cols_from_devices(g):
    _, K, n = g.shape
    return jnp.transpose(g, (1, 0, 2)).reshape(K, N_DEV * n)


SMALL_ROWS = 96


def _pack_small(vals, loss_rows):
    rows = [_to_rows(vals[n].reshape(-1), SUBLANE) for n in SMALL] + [loss_rows]
    buf = jnp.concatenate(rows, 0)
    assert buf.shape == (SMALL_ROWS, LANE)
    return buf


def _unpack_small(buf, like):
    out, r = {}, 0
    for n in SMALL:
        sz = math.prod(like[n].shape)
        out[n] = buf[r:r + -(-sz // LANE)].reshape(-1)[:sz].reshape(like[n].shape)
        r += -(-sz // (LANE * SUBLANE)) * SUBLANE
    return out


def kernel(x, p, positions, mix_norm, attn_w_qkv, attn_q_gain, attn_k_gain, attn_w_o, dn_w_in, dn_conv, dn_a_log, dn_dt_bias, dn_o_gain, dn_w_o, mlp_norm, w_up, w_down, ple_norm, w_ple, w_ple_gate, loss_target, m_mix_norm, m_attn_w_qkv, m_attn_q_gain, m_attn_k_gain, m_attn_w_o, m_dn_w_in, m_dn_conv, m_dn_a_log, m_dn_dt_bias, m_dn_o_gain, m_dn_w_o, m_mlp_norm, m_w_up, m_w_down, m_ple_norm, m_w_ple, m_w_ple_gate, v_mix_norm, v_attn_w_qkv, v_attn_q_gain, v_attn_k_gain, v_attn_w_o, v_dn_w_in, v_dn_conv, v_dn_a_log, v_dn_dt_bias, v_dn_o_gain, v_dn_w_o, v_mlp_norm, v_w_up, v_w_down, v_ple_norm, v_w_ple, v_w_ple_gate):
    w = dict(mix_norm=mix_norm, attn_w_qkv=attn_w_qkv, attn_q_gain=attn_q_gain, attn_k_gain=attn_k_gain, attn_w_o=attn_w_o,
             dn_w_in=dn_w_in, dn_conv=dn_conv, dn_a_log=dn_a_log, dn_dt_bias=dn_dt_bias, dn_o_gain=dn_o_gain, dn_w_o=dn_w_o,
             mlp_norm=mlp_norm, w_up=w_up, w_down=w_down, ple_norm=ple_norm, w_ple=w_ple, w_ple_gate=w_ple_gate)
    m = dict(mix_norm=m_mix_norm, attn_w_qkv=m_attn_w_qkv, attn_q_gain=m_attn_q_gain, attn_k_gain=m_attn_k_gain,
             attn_w_o=m_attn_w_o, dn_w_in=m_dn_w_in, dn_conv=m_dn_conv, dn_a_log=m_dn_a_log, dn_dt_bias=m_dn_dt_bias,
             dn_o_gain=m_dn_o_gain, dn_w_o=m_dn_w_o, mlp_norm=m_mlp_norm, w_up=m_w_up, w_down=m_w_down,
             ple_norm=m_ple_norm, w_ple=m_w_ple, w_ple_gate=m_w_ple_gate)
    v = dict(mix_norm=v_mix_norm, attn_w_qkv=v_attn_w_qkv, attn_q_gain=v_attn_q_gain, attn_k_gain=v_attn_k_gain,
             attn_w_o=v_attn_w_o, dn_w_in=v_dn_w_in, dn_conv=v_dn_conv, dn_a_log=v_dn_a_log, dn_dt_bias=v_dn_dt_bias,
             dn_o_gain=v_dn_o_gain, dn_w_o=v_dn_w_o, mlp_norm=v_mlp_norm, w_up=v_w_up, w_down=v_w_down,
             ple_norm=v_ple_norm, w_ple=v_w_ple, w_ple_gate=v_w_ple_gate)
    S = x.shape[1]

    bf = lambda a: a.astype(BF16)
    rows_to_devices = lambda t: t.reshape(N_DEV, t.shape[0] // N_DEV, t.shape[1])

    (g_qkv, g_ao), token = _all_gather("gather_attn", [bf(attn_w_qkv[0]), bf(attn_w_o[0])], [(0, None), (1, None)])
    rest_shards = [bf(dn_w_in[0]), bf(dn_w_o[0]), bf(w_up), bf(w_down), bf(w_ple), bf(w_ple_gate), _after(dn_conv[0], token)]
    rest_streams = [(0, None), (1, None), (2, 0), (2, 1), (3, 0), (3, 1), (4, 0), (4, 1), (5, 0), (5, 1), (6, None)]
    rest_arrived, token = _gather_async("gather_rest", rest_shards, rest_streams)
    W = dict(attn_w_qkv=_unpack_cols("unpack_attn_qkv", g_qkv), attn_w_o=_cols_from_devices(g_ao))

    def rest_of_weights(after):
        g_in, g_do, g_up0, g_up1, g_dn0, g_dn1, g_pl0, g_pl1, g_gt0, g_gt1, g_conv = rest_arrived(after)
        rest = dict(
            dn_conv=jnp.transpose(g_conv, (1, 0, 2)).reshape(CONV_W, DN_QKV), dn_w_o=g_do.reshape(DN_WIDTH, D_MODEL),
            w_up=[_cols_from_devices(g_up0), _cols_from_devices(g_up1)],
            w_down=[g_dn0.reshape(D_FF, D_MODEL), g_dn1.reshape(D_FF, D_MODEL)],
            w_ple=[_cols_from_devices(g_pl0), _cols_from_devices(g_pl1)],
            w_ple_gate=[g_gt0.reshape(D_MODEL, D_MODEL), g_gt1.reshape(D_MODEL, D_MODEL)])
        rest["dn_w_qkvz"], rest["dn_w_ab"] = _unpack_dn_in(g_in)
        return rest

    pending = {}

    def mlp_sends(g):
        return [_cols_to_devices(g["w_up"]), rows_to_devices(g["w_down"]), _cols_to_devices(g["w_ple"]),
                rows_to_devices(g["w_ple_gate"])]

    def start(tag, sends):
        pending[tag], token = _exchange_async(f"exchange_{tag}", sends)
        return token

    def send_layer1(g):
        conv_send = jnp.transpose(g["dn_conv"].reshape(CONV_W, N_DEV, DN_QKV // N_DEV), (1, 0, 2))
        return start("layer1", [_pack_dn_in(g["dn_w_qkvz"], g["dn_w_ab"]), conv_send, rows_to_devices(g["dn_w_o"])] + mlp_sends(g))

    def send_mlp0(g):
        return start("mlp0", mlp_sends(g))

    def send_attn(g):
        return start("attn", [_pack_cols("pack_attn_qkv", g["attn_w_qkv"]), _cols_to_devices(g["attn_w_o"])])

    P = dict(mix_norm=_after(mix_norm, token), attn_q_gain=attn_q_gain[0], attn_k_gain=attn_k_gain[0], dn_a_log=dn_a_log[0],
             dn_dt_bias=dn_dt_bias[0], dn_o_gain=dn_o_gain[0], mlp_norm=mlp_norm, ple_norm=ple_norm)

    sq, dx0, small_g = _local_step(x[0], p[:, 0], positions.reshape(S, 1), loss_target[0], W, P,
                                   rest_of_weights, send_layer1, send_mlp0, send_attn)

    r_in, r_conv, r_do, r_up1, r_dn1, r_pl1, r_gt1 = pending["layer1"](dx0)
    r_up0, r_dn0, r_pl0, r_gt0 = pending["mlp0"](dx0)
    r_qkv, r_ao = pending["attn"](dx0)
    big = {}
    for n, parts in (("attn_w_qkv", [r_qkv]), ("attn_w_o", [r_ao]), ("dn_w_in", [r_in]), ("dn_conv", [r_conv]),
                     ("dn_w_o", [r_do]), ("w_up", [r_up0, r_up1]), ("w_down", [r_dn0, r_dn1]),
                     ("w_ple", [r_pl0, r_pl1]), ("w_ple_gate", [r_gt0, r_gt1])):
        layers = [_adamw(f"adamw_{n}{l}", pt, w[n][l], m[n][l], v[n][l]) for l, pt in enumerate(parts)]
        big[n] = [jnp.stack([res[k] for res in layers]) for k in range(4)]

    loss_rows = jnp.pad((0.5 / D_MODEL) * jnp.sum(sq, axis=1, keepdims=True), ((0, SUBLANE - 1), (0, LANE - 1)))
    small_like = {n: w[n] for n in SMALL}
    parts_s = _all_gather("gather_small", [_pack_small(small_g, loss_rows)], [(0, None)])[0][0]
    zero_rows = jnp.zeros((SUBLANE, LANE), F32)
    small = _adamw("adamw_small", parts_s, _pack_small(w, zero_rows), _pack_small(m, zero_rows), _pack_small(v, zero_rows))
    loss = small[0][SMALL_ROWS - SUBLANE, 0]
    small = [_unpack_small(b, small_like) for b in small]

    outs = [loss, dx0[None]]
    for k in range(4):
        for n in WEIGHTS:
            outs.append(small[k][n] if n in SMALL else big[n][k])
    return tuple(outs)
```

```python
import functools
import math

import jax
import jax.numpy as jnp
from jax import lax
from jax.experimental import pallas as pl
from jax.experimental.pallas import tpu as pltpu

F32 = jnp.float32
BF16 = jnp.bfloat16
HIGHEST = lax.Precision.HIGHEST

N_DEV = 8
D_MODEL = 1024
EPS = 1e-6
SWA_GROUPS = ((128, 1), (512, 4), (2048, 16))
A_HEADS = 8
A_HEAD_DIM = 64
A_WIDTH = A_HEADS * A_HEAD_DIM
A_QKV = 3 * 3 * A_WIDTH
ROPE_DIM = 16
ROPE_HALF = 8
ROPE_THETA = 500000.0
BAND = 128
DN_HEADS = 8
DN_DIM = 128
DN_WIDTH = DN_HEADS * DN_DIM
CONV_W = 4
CHUNK = 64
D_FF = 4 * D_MODEL
PLE_DIM = 256
LR, B1, B2, ADAM_EPS, WD, STEP = 0.001, 0.9, 0.999, 1e-08, 0.01, 10

VMEM_LIMIT = 56 * 1024 * 1024
MXU_TILE = 1024
LANE = 128
SUBLANE = 8


def _cparams(sem):
    return pltpu.CompilerParams(dimension_semantics=sem, vmem_limit_bytes=VMEM_LIMIT)


def _tile(n, pref):
    if n <= pref:
        return n
    t = (pref // LANE) * LANE
    while t >= LANE:
        if n % t == 0:
            return t
        t -= LANE
    raise ValueError(f"no tile for {n}")


def _dot(a, b, ca=1, cb=0, precision=None):
    return lax.dot_general(a, b, (((ca,), (cb,)), ((), ())), precision=precision,
                           preferred_element_type=F32)


def _bdot(a, b, ca=1, cb=0):
    return _dot(a.astype(BF16), b.astype(BF16), ca, cb)


def _mm(name, a, b, *, ta=False, tb=False, epilogue=None, extras=(), out_dtypes=(F32,), n_colsums=0,
        tm_pref=MXU_TILE, tn_pref=1536, tk_pref=MXU_TILE):
    M, K = (a.shape[1], a.shape[0]) if ta else a.shape
    N = b.shape[0] if tb else b.shape[1]
    assert (b.shape[1] if tb else b.shape[0]) == K
    tm, tn, tk = _tile(M, tm_pref), _tile(N, tn_pref), _tile(K, tk_pref)
    nk = K // tk
    n_out = len(out_dtypes)
    n_ext = len(extras)
    assert n_colsums == 0 or tn == N

    def body(*refs):
        a_ref, b_ref = refs[0], refs[1]
        ext = refs[2:2 + n_ext]
        outs = refs[2 + n_ext:2 + n_ext + n_out]
        sums = refs[2 + n_ext + n_out:2 + n_ext + n_out + n_colsums]
        row_tile, k = pl.program_id(0), pl.program_id(2)
        prod = _bdot(a_ref[...], b_ref[...], 0 if ta else 1, 1 if tb else 0)

        def finish(r):
            res = (r,) if epilogue is None else epilogue(r, *[e[...] for e in ext])
            for o, v in zip(outs, res):
                o[...] = v.astype(o.dtype)
            for o, v in zip(sums, res[n_out:]):
                @pl.when(row_tile == 0)
                def _(o=o, v=v):
                    o[...] = v

                @pl.when(row_tile > 0)
                def _(o=o, v=v):
                    o[...] += v

        if nk == 1:
            finish(prod)
            return
        acc = refs[-1]

        @pl.when(k == 0)
        def _():
            acc[...] = jnp.zeros_like(acc)

        acc[...] += prod

        @pl.when(k == nk - 1)
        def _():
            finish(acc[...])

    a_spec = pl.BlockSpec((tk, tm), lambda i, j, k: (k, i)) if ta else pl.BlockSpec((tm, tk), lambda i, j, k: (i, k))
    b_spec = pl.BlockSpec((tn, tk), lambda i, j, k: (j, k)) if tb else pl.BlockSpec((tk, tn), lambda i, j, k: (k, j))
    ext_specs = []
    for e in extras:
        if e.shape[0] == 1 and M != 1:
            ext_specs.append(pl.BlockSpec((1, tn), lambda i, j, k: (0, j)))
        else:
            ext_specs.append(pl.BlockSpec((tm, tn), lambda i, j, k: (i, j)))
    out = pl.pallas_call(
        body, name=name,
        grid=(M // tm, N // tn, nk),
        in_specs=[a_spec, b_spec] + ext_specs,
        out_specs=[pl.BlockSpec((tm, tn), lambda i, j, k: (i, j)) for _ in range(n_out)]
        + [pl.BlockSpec((1, tn), lambda i, j, k: (0, 0)) for _ in range(n_colsums)],
        out_shape=[jax.ShapeDtypeStruct((M, N), dt) for dt in out_dtypes]
        + [jax.ShapeDtypeStruct((1, N), F32) for _ in range(n_colsums)],
        scratch_shapes=[pltpu.VMEM((tm, tn), F32)] if nk > 1 else [],
        compiler_params=_cparams(("arbitrary" if n_colsums else "parallel", "parallel", "arbitrary")),
    )(a, b, *extras)
    return out[0] if len(out) == 1 else tuple(out)


def _rows(name, fn, ins, outs, *, tr, accs=()):
    ins = [(e[0], e[1]) + (e[2] if len(e) > 2 else (0, e[0].shape[-1])) for e in ins]
    n_rows = next(e[0].shape[0] for e in ins if e[1] == "row")
    assert n_rows % tr == 0 and tr % SUBLANE == 0
    steps = n_rows // tr
    t8 = tr // SUBLANE
    n8 = n_rows // SUBLANE
    n_in, n_out, n_acc = len(ins), len(outs), len(accs)

    def body(*refs):
        i = pl.program_id(0)
        vals = fn(i, steps, *[r[...] for r in refs[:n_in]])
        if not isinstance(vals, (tuple, list)):
            vals = (vals,)
        assert len(vals) == n_out + n_acc
        for o, v in zip(refs[n_in:n_in + n_out], vals[:n_out]):
            o[...] = v.astype(o.dtype)
        if n_acc:
            acc_refs = refs[n_in + n_out:]

            @pl.when(i == 0)
            def _():
                for r in acc_refs:
                    r[...] = jnp.zeros_like(r)

            for r, v in zip(acc_refs, vals[n_out:]):
                r[...] += v.astype(r.dtype)

    in_specs = []
    for a, kind, cb, c in ins:
        if kind == "row":
            in_specs.append(pl.BlockSpec((tr, c), lambda i, cb=cb: (i, cb)))
        elif kind == "full":
            in_specs.append(pl.BlockSpec(a.shape, lambda i, z=(0,) * a.ndim: z))
        elif kind == "prev8":
            in_specs.append(pl.BlockSpec((SUBLANE, c), lambda i, cb=cb: (jnp.maximum(i * t8 - 1, 0), cb)))
        elif kind == "next8":
            in_specs.append(pl.BlockSpec((SUBLANE, c), lambda i, cb=cb: (jnp.minimum((i + 1) * t8, n8 - 1), cb)))
        else:
            raise ValueError(kind)
    out_specs = [pl.BlockSpec((tr, c), lambda i: (i, 0)) for c, _ in outs]
    out_specs += [pl.BlockSpec(s, lambda i, z=(0,) * len(s): z) for s, _ in accs]
    out_shape = [jax.ShapeDtypeStruct((n_rows, c), dt) for c, dt in outs]
    out_shape += [jax.ShapeDtypeStruct(s, dt) for s, dt in accs]
    res = pl.pallas_call(
        body, name=name, grid=(steps,), in_specs=in_specs, out_specs=out_specs, out_shape=out_shape,
        compiler_params=_cparams(("arbitrary",) if n_acc else ("parallel",)),
    )(*[e[0] for e in ins])
    return res[0] if len(res) == 1 else tuple(res)


def _colsum(x):
    return jnp.sum(x, axis=0, keepdims=True)


def _sum_all(x):
    return jnp.sum(jnp.sum(x, axis=1, keepdims=True), axis=0, keepdims=True)


def _rmsnorm_fwd(name, x, gain):
    def fn(i, n, xt, g):
        r = lax.rsqrt(jnp.mean(xt * xt, axis=-1, keepdims=True) + EPS)
        return (xt * r * g,)
    return _rows(name, fn, [(x, "row"), (gain, "full")], [(x.shape[1], BF16)], tr=512)


FUSED_ROWS = 1024


def _res_norm(acc, res, g):
    x = res + acc
    return x, x * lax.rsqrt(jnp.mean(x * x, axis=-1, keepdims=True) + EPS) * g


def _norm_bwd(dh, x, g, dres):
    r = lax.rsqrt(jnp.mean(x * x, axis=-1, keepdims=True) + EPS)
    xh = x * r
    dxn = dh * g
    dx = dres + r * (dxn - xh * jnp.mean(dxn * xh, axis=-1, keepdims=True))
    return dx, _colsum(dh * xh)


def _norm_bwd_2(dh, x, g, dres):
    dx, dg = _norm_bwd(dh, x, g, dres)
    return dx, dx, dg


def _head_consts():
    import numpy as np
    e = np.arange(A_WIDTH) % A_HEAD_DIM
    inv = (np.float32(ROPE_THETA) ** (-np.arange(0, ROPE_DIM, 2, dtype=np.float32) / np.float32(ROPE_DIM))).astype(np.float32)
    c = np.zeros((8, A_WIDTH), np.float32)
    c[0] = np.where(e < ROPE_DIM, inv[e % ROPE_HALF], 0.0)
    c[1] = np.where(e < ROPE_HALF, -1.0, np.where(e < ROPE_DIM, 1.0, 0.0))
    c[2] = (e < ROPE_HALF).astype(np.float32)
    c[3] = (e < ROPE_DIM).astype(np.float32)
    return jnp.asarray(c)


def _block_diag(scale):
    import numpy as np
    h = np.arange(A_WIDTH) // A_HEAD_DIM
    return jnp.asarray((h[:, None] == h[None, :]).astype(np.float32) * scale, dtype=BF16)


def _seg_sum(x, bd):
    hi = x.astype(BF16)
    lo = (x - hi.astype(F32)).astype(BF16)
    return _dot(hi, bd) + _dot(lo, bd)


def _rope_tables(positions, consts):
    def fn(i, n, pos, c):
        ang = pos.astype(F32) * c[0:1, :LANE]
        return jnp.cos(ang), jnp.sin(ang) * c[1:2, :LANE]
    return _rows("rope_tables", fn, [(positions, "row"), (consts, "full")], [(LANE, F32), (LANE, F32)], tr=512)


def _rope_wide(t):
    return jnp.concatenate([t] * (A_WIDTH // LANE), axis=1)


def _rope_apply(y, ct, st, low):
    rolled = jnp.where(low, pltpu.roll(y, A_WIDTH - ROPE_HALF, 1), pltpu.roll(y, ROPE_HALF, 1))
    return y * ct + rolled * st


def _rope_apply_bwd(dout, ct, st, low, in16):
    t = dout * st
    back = jnp.where(low, pltpu.roll(t, A_WIDTH - ROPE_HALF, 1), jnp.where(in16, pltpu.roll(t, ROPE_HALF, 1), 0.0))
    return dout * ct + back


def _attn_prep(qkv, gains, ct, st, consts, bd):
    def fn(i, n, t, g, c_t, s_t, c, b):
        low = c[2:3, :] > 0.5
        c_t, s_t = _rope_wide(c_t), _rope_wide(s_t)
        groups = []
        for grp in range(3):
            cols = []
            for which in range(3):
                off = (grp * 3 + which) * A_WIDTH
                x = t[:, off:off + A_WIDTH].astype(F32)
                if which == 2:
                    cols.append(x.astype(BF16))
                    continue
                r = lax.rsqrt(_seg_sum(x * x, b) + EPS)
                y = x * r * g[grp * 2 + which:grp * 2 + which + 1, :]
                cols.append(_rope_apply(y, c_t, s_t, low).astype(BF16))
            groups.append(jnp.concatenate(cols, axis=1))
        return tuple(groups)
    return _rows("attn_prep", fn, [(qkv, "row"), (gains, "full"), (ct, "row"), (st, "row"), (consts, "full"), (bd, "full")],
                 [(3 * A_WIDTH, BF16)] * 3, tr=256)


def _band_mask(n):
    row = lax.broadcasted_iota(jnp.int32, (BAND, 2 * BAND), 0)
    col = lax.broadcasted_iota(jnp.int32, (BAND, 2 * BAND), 1)
    dist = row + BAND - col
    return (dist >= 0) & (dist <= BAND) & ((col >= BAND) | (n > 0))


def _attn_fwd(qkvn, grp):
    S = qkvn.shape[0]
    d = SWA_GROUPS[grp][1]
    L = S // d
    nblk = L // BAND
    assert L % BAND == 0
    view = qkvn.reshape(L, d * 3 * A_WIDTH)

    def body(q_ref, kc_ref, kp_ref, vc_ref, vp_ref, o_ref, lse_ref):
        n = pl.program_id(1)
        valid = _band_mask(n)
        first = lax.broadcasted_iota(jnp.int32, (BAND, LANE), 1) < A_HEAD_DIM
        pairs = [slice(pr * LANE, (pr + 1) * LANE) for pr in range(A_WIDTH // LANE)]
        halves = (first, jnp.logical_not(first))
        qps = [q_ref[:, sl] for sl in pairs]
        kcats = [jnp.concatenate([kp_ref[:, sl], kc_ref[:, sl]], axis=0) for sl in pairs]
        vcats = [jnp.concatenate([vp_ref[:, sl], vc_ref[:, sl]], axis=0) for sl in pairs]
        heads = [(pr, m) for pr in range(len(pairs)) for m in halves]
        ss = [_dot(jnp.where(m, qps[pr], jnp.zeros_like(qps[pr])), kcats[pr], 1, 1) for pr, m in heads]
        ps, lses = [], []
        for s in ss:
            s = jnp.where(valid, s * (A_HEAD_DIM ** -0.5), -1e30)
            mx = jnp.max(s, axis=-1, keepdims=True)
            e = jnp.exp(s - mx)
            l = jnp.sum(e, axis=-1, keepdims=True)
            ps.append((e / l).astype(BF16))
            lses.append(mx + jnp.log(l))
        os_ = [_dot(p, vcats[pr]) for p, (pr, _) in zip(ps, heads)]
        o_ref[...] = jnp.concatenate([jnp.where(first, os_[2 * pr], os_[2 * pr + 1]) for pr in range(len(pairs))], axis=1)
        lse_ref[...] = jnp.concatenate([jnp.where(first, lses[2 * pr], lses[2 * pr + 1]) for pr in range(len(pairs))], axis=1)

    blk = (BAND, A_WIDTH)
    o, lse = pl.pallas_call(
        body, name=f"attn_fwd_g{grp}", grid=(d, nblk),
        in_specs=[pl.BlockSpec(blk, lambda r, n: (n, r * 3)),
                  pl.BlockSpec(blk, lambda r, n: (n, r * 3 + 1)),
                  pl.BlockSpec(blk, lambda r, n: (jnp.maximum(n - 1, 0), r * 3 + 1)),
                  pl.BlockSpec(blk, lambda r, n: (n, r * 3 + 2)),
                  pl.BlockSpec(blk, lambda r, n: (jnp.maximum(n - 1, 0), r * 3 + 2))],
        out_specs=[pl.BlockSpec(blk, lambda r, n: (n, r)), pl.BlockSpec(blk, lambda r, n: (n, r))],
        out_shape=[jax.ShapeDtypeStruct((L, d * A_WIDTH), F32)] * 2,
        compiler_params=_cparams(("parallel", "parallel")),
    )(view, view, view, view, view)
    return o.reshape(S, A_WIDTH), lse.reshape(S, A_WIDTH)


def _merge_weights(l0, l1, l2):
    mx = jnp.maximum(jnp.maximum(l0, l1), l2)
    e0, e1, e2 = jnp.exp(l0 - mx), jnp.exp(l1 - mx), jnp.exp(l2 - mx)
    inv = 1.0 / (e0 + e1 + e2)
    return e0 * inv, e1 * inv, e2 * inv


def _attn_merge(os_, lses):
    def fn(i, n, o0, o1, o2, l0, l1, l2):
        w0, w1, w2 = _merge_weights(l0, l1, l2)
        return (w0 * o0 + w1 * o1 + w2 * o2,)
    ins = [(a, "row") for a in (*os_, *lses)]
    return _rows("attn_merge", fn, ins, [(A_WIDTH, BF16)], tr=512)


def _attn_merge_bwd(do, os_, lses, bd1):
    def fn(i, n, dot_, o0, o1, o2, l0, l1, l2, b):
        w0, w1, w2 = _merge_weights(l0, l1, l2)
        o = w0 * o0 + w1 * o1 + w2 * o2
        dsum = _seg_sum(dot_ * o, b)
        return (w0 * dot_, w1 * dot_, w2 * dot_, -w0 * dsum, -w1 * dsum, -w2 * dsum)
    ins = [(do, "row")] + [(a, "row") for a in (*os_, *lses)] + [(bd1, "full")]
    res = _rows("attn_merge_bwd", fn, ins, [(A_WIDTH, BF16)] * 3 + [(A_WIDTH, F32)] * 3, tr=256)
    return res[:3], res[3:]


def _lane_pick(x, lane_idx, lane):
    return jnp.sum(jnp.where(lane_idx == lane, x, 0.0), axis=-1, keepdims=True)


def _attn_bwd(qkvn, grp, do_g, lse, c_g):
    S = qkvn.shape[0]
    d = SWA_GROUPS[grp][1]
    L = S // d
    nblk = L // BAND
    view = qkvn.reshape(L, d * 3 * A_WIDTH)
    dov, lsev, cv = (t.reshape(L, d * A_WIDTH) for t in (do_g, lse, c_g))

    def body(q_ref, kc_ref, kp_ref, vc_ref, vp_ref, do_ref, lse_ref, c_ref, dq_ref, dk_ref, dv_ref, ck, cv_):
        n = pl.program_id(1)

        @pl.when(n == 0)
        def _():
            ck[...] = jnp.zeros_like(ck)
            cv_[...] = jnp.zeros_like(cv_)

        @pl.when(n < nblk)
        def _():
            valid = _band_mask(n)
            lane = lax.broadcasted_iota(jnp.int32, (BAND, LANE), 1)
            first = lane < A_HEAD_DIM
            lane2 = lax.broadcasted_iota(jnp.int32, (2 * BAND, LANE), 1) < A_HEAD_DIM
            pairs = [slice(pr * LANE, (pr + 1) * LANE) for pr in range(A_WIDTH // LANE)]
            halves = (first, jnp.logical_not(first))
            qps = [q_ref[:, sl] for sl in pairs]
            dops = [do_ref[:, sl] for sl in pairs]
            kcats = [jnp.concatenate([kp_ref[:, sl], kc_ref[:, sl]], axis=0) for sl in pairs]
            vcats = [jnp.concatenate([vp_ref[:, sl], vc_ref[:, sl]], axis=0) for sl in pairs]
            heads = [(pr, hh) for pr in range(len(pairs)) for hh in range(2)]
            zero = jnp.zeros_like(qps[0])
            ss = [_dot(jnp.where(halves[hh], qps[pr], zero), kcats[pr], 1, 1) for pr, hh in heads]
            dps = [_dot(jnp.where(halves[hh], dops[pr], zero), vcats[pr], 1, 1) for pr, hh in heads]
            dss, pbs = [], []
            for (pr, hh), s, dp in zip(heads, ss, dps):
                lse_h = _lane_pick(lse_ref[:, pairs[pr]], lane, hh * A_HEAD_DIM)
                c_h = _lane_pick(c_ref[:, pairs[pr]], lane, hh * A_HEAD_DIM)
                p = jnp.where(valid, jnp.exp(s * (A_HEAD_DIM ** -0.5) - lse_h), 0.0)
                dss.append((p * (dp + c_h) * (A_HEAD_DIM ** -0.5)).astype(BF16))
                pbs.append(p.astype(BF16))
            dqs = [_dot(ds, kcats[pr]) for ds, (pr, _) in zip(dss, heads)]
            dks = [_dot(ds, qps[pr], 0, 0) for ds, (pr, _) in zip(dss, heads)]
            dvs = [_dot(pb, dops[pr], 0, 0) for pb, (pr, _) in zip(pbs, heads)]
            for pr, sl in enumerate(pairs):
                dq_ref[:, sl] = jnp.where(first, dqs[2 * pr], dqs[2 * pr + 1])
                dkc = jnp.where(lane2, dks[2 * pr], dks[2 * pr + 1])
                dvc = jnp.where(lane2, dvs[2 * pr], dvs[2 * pr + 1])
                dk_ref[:, sl] = ck[:, sl] + dkc[:BAND]
                dv_ref[:, sl] = cv_[:, sl] + dvc[:BAND]
                ck[:, sl] = dkc[BAND:]
                cv_[:, sl] = dvc[BAND:]

        @pl.when(n == nblk)
        def _():
            dk_ref[...] = ck[...]
            dv_ref[...] = cv_[...]

    blk = (BAND, A_WIDTH)
    last = nblk - 1
    qn = lambda n: jnp.minimum(n, last)
    pn = lambda n: jnp.clip(n - 1, 0, last)
    dq, dk, dv = pl.pallas_call(
        body, name=f"attn_bwd_g{grp}", grid=(d, nblk + 1),
        in_specs=[pl.BlockSpec(blk, lambda r, n: (qn(n), r * 3)),
                  pl.BlockSpec(blk, lambda r, n: (qn(n), r * 3 + 1)),
                  pl.BlockSpec(blk, lambda r, n: (pn(n), r * 3 + 1)),
                  pl.BlockSpec(blk, lambda r, n: (qn(n), r * 3 + 2)),
                  pl.BlockSpec(blk, lambda r, n: (pn(n), r * 3 + 2)),
                  pl.BlockSpec(blk, lambda r, n: (qn(n), r)),
                  pl.BlockSpec(blk, lambda r, n: (qn(n), r)),
                  pl.BlockSpec(blk, lambda r, n: (qn(n), r))],
        out_specs=[pl.BlockSpec(blk, lambda r, n: (qn(n), r)),
                   pl.BlockSpec(blk, lambda r, n: (pn(n), r)),
                   pl.BlockSpec(blk, lambda r, n: (pn(n), r))],
        out_shape=[jax.ShapeDtypeStruct((L, d * A_WIDTH), F32)] * 3,
        scratch_shapes=[pltpu.VMEM(blk, F32), pltpu.VMEM(blk, F32)],
        compiler_params=_cparams(("parallel", "arbitrary")),
    )(view, view, view, view, view, dov, lsev, cv)
    return tuple(t.reshape(S, A_WIDTH) for t in (dq, dk, dv))


def _attn_prep_bwd(qkv, grads, gains, ct, st, consts, bd):
    def fn(i, n, t, g, c_t, s_t, c, b, *gr):
        low = c[2:3, :] > 0.5
        in16 = c[3:4, :] > 0.5
        c_t, s_t = _rope_wide(c_t), _rope_wide(s_t)
        cols, dgs = [], []
        for grp in range(3):
            for which in range(3):
                dout = gr[grp * 3 + which]
                if which == 2:
                    cols.append(dout.astype(BF16))
                    continue
                off = (grp * 3 + which) * A_WIDTH
                x = t[:, off:off + A_WIDTH].astype(F32)
                gain = g[grp * 2 + which:grp * 2 + which + 1, :]
                r = lax.rsqrt(_seg_sum(x * x, b) + EPS)
                xh = x * r
                dy = _rope_apply_bwd(dout, c_t, s_t, low, in16)
                dyn = dy * gain
                dx = r * (dyn - xh * _seg_sum(dyn * xh, b))
                cols.append(dx.astype(BF16))
                dgs.append(_colsum(dy * xh))
        return (jnp.concatenate(cols, axis=1), *dgs)
    ins = [(qkv, "row"), (gains, "full"), (ct, "row"), (st, "row"), (consts, "full"), (bd, "full")] + [(a, "row") for a in grads]
    res = _rows("attn_prep_bwd", fn, ins, [(A_QKV, BF16)], tr=128, accs=[((1, A_WIDTH), F32)] * 6)
    return res[0], res[1:]


DN_QKV = 3 * DN_WIDTH
DN_QKVZ = DN_QKV + DN_WIDTH


def _sigmoid(x):
    return 1.0 / (1.0 + jnp.exp(-x))


def _softplus(x):
    return jnp.maximum(x, 0.0) + jnp.log(1.0 + jnp.exp(-jnp.abs(x)))


def _conv_taps(xs, w, tr):
    acc = None
    for j in range(CONV_W):
        sh = CONV_W - 1 - j
        term = (pltpu.roll(xs, sh, 0) if sh else xs)[SUBLANE:] * w[j:j + 1, :]
        acc = term if acc is None else acc + term
    return acc


def _dn_prep(qkvz, ab, convw, alog_row, dt_row):
    tr = 256

    def fn(i, n, x, xp, abt, w, al, dt):
        xp = jnp.where(i > 0, xp, 0.0)
        u = _conv_taps(jnp.concatenate([xp, x], axis=0), w, tr)
        y = u * _sigmoid(u)
        qs, ks = [], []
        for h in range(DN_HEADS):
            for dst, base, sc in ((qs, 0, DN_DIM ** -0.5), (ks, DN_WIDTH, 1.0)):
                seg = y[:, base + h * DN_DIM:base + (h + 1) * DN_DIM]
                dst.append(seg * (lax.rsqrt(jnp.sum(seg * seg, axis=-1, keepdims=True) + EPS) * sc))
        lane = lax.broadcasted_iota(jnp.int32, abt.shape, 1)
        g = -jnp.exp(al) * _softplus(abt + dt)
        gb = jnp.where(lane < DN_HEADS, g, jnp.where(lane < 2 * DN_HEADS, _sigmoid(abt), 0.0))
        return u, jnp.concatenate(qs, axis=1), jnp.concatenate(ks, axis=1), y[:, 2 * DN_WIDTH:], gb

    ins = [(qkvz, "row", (0, DN_QKV)), (qkvz, "prev8", (0, DN_QKV)), (ab, "row"), (convw, "full"),
           (alog_row, "full"), (dt_row, "full")]
    return _rows("dn_prep", fn, ins, [(DN_QKV, F32), (DN_WIDTH, F32), (DN_WIDTH, F32), (DN_WIDTH, F32), (LANE, F32)], tr=tr)


def _tri_masks():
    row = lax.broadcasted_iota(jnp.int32, (CHUNK, CHUNK), 0)
    col = lax.broadcasted_iota(jnp.int32, (CHUNK, CHUNK), 1)
    return row >= col, row > col, row == col


def _heads(fn, *lists):
    return [fn(*xs) for xs in zip(*lists)]


def _split(x):
    hi = x.astype(BF16)
    return hi, (x - hi.astype(F32)).astype(BF16)


def _dot3(a, b, ca=1, cb=0):
    (ah, al), (bh, bl) = a, b
    return _dot(ah, bh, ca, cb) + (_dot(ah, bl, ca, cb) + _dot(al, bh, ca, cb))


SPLIT_STEPS = 3


def _unit_lower_inverse(a_list, eye):
    ts = [eye - a for a in a_list]
    parts = [_split(a) for a in a_list]
    for step in range(5):
        if step < SPLIT_STEPS:
            parts = [_split(_dot3(p, p)) for p in parts]
            ts = [t + _dot3(_split(t), p) for t, p in zip(ts, parts)]
        else:
            parts = [(_dot(p[0], p[0]).astype(BF16), None) for p in parts]
            ts = [t + _dot(t.astype(BF16), p[0]) for t, p in zip(ts, parts)]
    return ts


def _dn_terms(qs, ks, vs, gb):
    lower, strict, diag = _tri_masks()
    lane = lax.broadcasted_iota(jnp.int32, (CHUNK, LANE), 1)
    is_last = lax.broadcasted_iota(jnp.int32, (CHUNK, 1), 0) == CHUNK - 1
    hs = range(DN_HEADS)
    gc = _dot(lower.astype(F32), gb, precision=HIGHEST)
    gct = jnp.transpose(gc)
    bcol = [_lane_pick(gb, lane, DN_HEADS + h) for h in hs]
    gcol = [_lane_pick(gc, lane, h) for h in hs]
    glast = [jnp.sum(jnp.where(is_last, g, 0.0), axis=0, keepdims=True) for g in gcol]
    decay = [jnp.exp(jnp.where(lower, gcol[h] - gct[h:h + 1, :], -1e30)) for h in hs]
    kb = _heads(lambda k, b: k * b, ks, bcol)
    kk = _heads(lambda x, k: _bdot(x, k, 1, 1), kb, ks)
    qk = _heads(lambda q, k: _bdot(q, k, 1, 1), qs, ks)
    a = _heads(lambda x, d: jnp.where(strict, x * d, 0.0), kk, decay)
    t = [_split(x) for x in _unit_lower_inverse(a, diag.astype(F32))]
    eg = [jnp.exp(g) for g in gcol]
    egl = _heads(lambda gl, g: jnp.exp(gl - g), glast, gcol)
    rhs_w = _heads(lambda x, e: x * e, kb, eg)
    u = _heads(lambda tt, v, b: _dot3(tt, _split(v * b)), t, vs, bcol)
    w = _heads(lambda tt, r: _dot3(tt, _split(r)), t, rhs_w)
    return dict(bcol=bcol, decay=decay, kb=kb, a=a, t=t, eg=eg, egl=egl, rhs_w=rhs_w, u=u, w=w,
                attn=_heads(lambda x, d: x * d, qk, decay), q_dec=_heads(lambda q, e: q * e, qs, eg),
                k_dec=_heads(lambda k, e: k * e, ks, egl), c_dec=[jnp.exp(g) for g in glast],
                lower=lower, strict=strict, lane=lane, is_last=is_last)


def _head_slices(ref):
    return [ref[:, h * DN_DIM:(h + 1) * DN_DIM] for h in range(DN_HEADS)]


def _dn_chunk_fwd(q, k, v, gb):
    S = q.shape[0]
    N = S // CHUNK

    def body(q_ref, k_ref, v_ref, gb_ref, o_ref, st_ref, state):
        @pl.when(pl.program_id(0) == 0)
        def _():
            state[...] = jnp.zeros_like(state)

        f = _dn_terms(_head_slices(q_ref), _head_slices(k_ref), _head_slices(v_ref), gb_ref[...])
        s = [state[h] for h in range(DN_HEADS)]
        for h in range(DN_HEADS):
            st_ref[0, h] = s[h]
        sb = [x.astype(BF16) for x in s]
        v_new = _heads(lambda u, w, x: u - _bdot(w, x), f["u"], f["w"], sb)
        o = _heads(lambda qd, x, at, vn: _bdot(qd, x) + _bdot(at, vn), f["q_dec"], sb, f["attn"], v_new)
        new_s = _heads(lambda x, c, kd, vn: x * c + _bdot(kd, vn, 0, 0), s, f["c_dec"], f["k_dec"], v_new)
        for h in range(DN_HEADS):
            o_ref[:, h * DN_DIM:(h + 1) * DN_DIM] = o[h]
            state[h] = new_s[h]

    blk = pl.BlockSpec((CHUNK, DN_WIDTH), lambda n: (n, 0))
    st_blk = pl.BlockSpec((1, DN_HEADS, DN_DIM, DN_DIM), lambda n: (n, 0, 0, 0))
    return pl.pallas_call(
        body, name="dn_chunk_fwd", grid=(N,),
        in_specs=[blk, blk, blk, pl.BlockSpec((CHUNK, LANE), lambda n: (n, 0))],
        out_specs=[blk, st_blk],
        out_shape=[jax.ShapeDtypeStruct((S, DN_WIDTH), F32), jax.ShapeDtypeStruct((N, DN_HEADS, DN_DIM, DN_DIM), F32)],
        scratch_shapes=[pltpu.VMEM((DN_HEADS, DN_DIM, DN_DIM), F32)],
        compiler_params=_cparams(("arbitrary",)),
    )(q, k, v, gb)


def _dn_chunk_bwd(q, k, v, gb, states, do):
    S = q.shape[0]
    N = S // CHUNK

    def body(q_ref, k_ref, v_ref, gb_ref, st_ref, do_ref, dq_ref, dk_ref, dv_ref, dgb_ref, dstate):
        @pl.when(pl.program_id(0) == 0)
        def _():
            dstate[...] = jnp.zeros_like(dstate)

        hs = range(DN_HEADS)
        qs, ks, vs, dos = (_head_slices(r) for r in (q_ref, k_ref, v_ref, do_ref))
        f = _dn_terms(qs, ks, vs, gb_ref[...])
        lane, is_last = f["lane"], f["is_last"]
        rowsum = lambda x: jnp.sum(x, axis=-1, keepdims=True)
        s = [st_ref[0, h] for h in hs]
        dsn = [dstate[h] for h in hs]
        sb = [x.astype(BF16) for x in s]
        dsb = [x.astype(BF16) for x in dsn]
        dob = [x.astype(BF16) for x in dos]
        v_new = _heads(lambda u, w, x: u - _bdot(w, x), f["u"], f["w"], sb)
        dv_new = _heads(lambda at, d, kd, x: _bdot(at, d, 0, 0) + _bdot(kd, x), f["attn"], dob, f["k_dec"], dsb)
        dattn = _heads(lambda d, vn: _bdot(d, vn, 1, 1), dob, v_new)
        dq_dec = _heads(lambda d, x: _bdot(d, x, 1, 1), dob, sb)
        dk_dec = _heads(lambda vn, x: _bdot(vn, x, 1, 1), v_new, dsb)
        dw = _heads(lambda dv_, x: -_bdot(dv_, x, 1, 1), dv_new, sb)
        new_ds = _heads(lambda x, c, qd, d, w, dv_: x * c + _bdot(qd, d, 0, 0) - _bdot(w, dv_, 0, 0),
                        dsn, f["c_dec"], f["q_dec"], dob, f["w"], dv_new)
        for h in hs:
            dstate[h] = new_ds[h]
        drhs_u = _heads(lambda tt, x: _dot3(tt, _split(x), 0, 0), f["t"], dv_new)
        drhs_w = _heads(lambda tt, x: _dot3(tt, _split(x), 0, 0), f["t"], dw)
        da = _heads(lambda du_, u, dw_, w: jnp.where(f["strict"], -(_bdot(du_, u, 1, 1) + _bdot(dw_, w, 1, 1)), 0.0),
                    drhs_u, f["u"], drhs_w, f["w"])
        dkk = _heads(lambda x, d: x * d, da, f["decay"])
        dqk = _heads(lambda x, d: x * d, dattn, f["decay"])
        dkb = _heads(lambda x, k_, dw_, e: _bdot(x, k_) + dw_ * e, dkk, ks, drhs_w, f["eg"])
        dq = _heads(lambda x, k_, dqd, e: _bdot(x, k_) + dqd * e, dqk, ks, dq_dec, f["eg"])
        dk = _heads(lambda x, kb_, y, q_, dkd, el, dkb_, b: _bdot(x, kb_, 0, 0) + _bdot(y, q_, 0, 0) + dkd * el + dkb_ * b,
                    dkk, f["kb"], dqk, qs, dk_dec, f["egl"], dkb, f["bcol"])
        m = _heads(lambda x, a_, y, at: x * a_ + y * at, da, f["a"], dattn, f["attn"])
        ones = jnp.ones((CHUNK, LANE), BF16)
        col_m = [(_dot(mh, ones, 0, 0) + _dot(ml, ones, 0, 0))[:, 0:1] for mh, ml in map(_split, m)]
        dgc_all = jnp.zeros((CHUNK, LANE), F32)
        dbeta_all = jnp.zeros((CHUNK, LANE), F32)
        for h in hs:
            dq_ref[:, h * DN_DIM:(h + 1) * DN_DIM] = dq[h]
            dk_ref[:, h * DN_DIM:(h + 1) * DN_DIM] = dk[h]
            dv_ref[:, h * DN_DIM:(h + 1) * DN_DIM] = drhs_u[h] * f["bcol"][h]
            kdec_term = rowsum(dk_dec[h] * f["k_dec"][h])
            dc_dec = _sum_all(dsn[h] * s[h])
            dgc = (rowsum(m[h]) - col_m[h] + rowsum(dq_dec[h] * f["q_dec"][h]) - kdec_term
                   + rowsum(drhs_w[h] * f["rhs_w"][h]))
            last_extra = jnp.sum(kdec_term, axis=0, keepdims=True) + dc_dec * f["c_dec"][h]
            dgc = dgc + jnp.where(is_last, last_extra, 0.0)
            dbeta = rowsum(drhs_u[h] * vs[h]) + rowsum(dkb[h] * ks[h])
            dgc_all = jnp.where(lane == h, dgc, dgc_all)
            dbeta_all = jnp.where(lane == DN_HEADS + h, dbeta, dbeta_all)
        dg_all = _dot(f["lower"].astype(F32), dgc_all, 0, 0, precision=HIGHEST)
        dgb_ref[...] = jnp.where(lane < DN_HEADS, dg_all, dbeta_all)

    rev = lambda n: (N - 1 - n, 0)
    blk = pl.BlockSpec((CHUNK, DN_WIDTH), rev)
    gblk = pl.BlockSpec((CHUNK, LANE), rev)
    st_blk = pl.BlockSpec((1, DN_HEADS, DN_DIM, DN_DIM), lambda n: (N - 1 - n, 0, 0, 0))
    return pl.pallas_call(
        body, name="dn_chunk_bwd", grid=(N,),
        in_specs=[blk, blk, blk, gblk, st_blk, blk],
        out_specs=[blk, blk, blk, gblk],
        out_shape=[jax.ShapeDtypeStruct((S, DN_WIDTH), F32)] * 3 + [jax.ShapeDtypeStruct((S, LANE), F32)],
        scratch_shapes=[pltpu.VMEM((DN_HEADS, DN_DIM, DN_DIM), F32)],
        compiler_params=_cparams(("arbitrary",)),
    )(q, k, v, gb, states, do)


def _dn_post(o, qkvz, gain_row):
    def fn(i, n, ot, z, g):
        cols = []
        for h in range(DN_HEADS):
            seg = ot[:, h * DN_DIM:(h + 1) * DN_DIM]
            cols.append(seg * lax.rsqrt(jnp.mean(seg * seg, axis=-1, keepdims=True) + EPS) * g)
        return (jnp.concatenate(cols, axis=1) * (z * _sigmoid(z)),)
    return _rows("dn_post", fn, [(o, "row"), (qkvz, "row", (3, DN_WIDTH)), (gain_row, "full")], [(DN_WIDTH, BF16)], tr=512)


def _dn_post_bwd(don, o, qkvz, gain_row):
    def fn(i, n, dy, ot, z, g):
        sg = _sigmoid(z)
        sz = z * sg
        dos, ohs = [], []
        dg = jnp.zeros((1, DN_DIM), F32)
        for h in range(DN_HEADS):
            sl = slice(h * DN_DIM, (h + 1) * DN_DIM)
            seg = ot[:, sl]
            r = lax.rsqrt(jnp.mean(seg * seg, axis=-1, keepdims=True) + EPS)
            oh = seg * r
            dno = dy[:, sl] * sz[:, sl]
            dg = dg + _colsum(dno * oh)
            dn = dno * g
            dos.append(r * (dn - oh * jnp.mean(dn * oh, axis=-1, keepdims=True)))
            ohs.append(oh * g)
        dz = dy * jnp.concatenate(ohs, axis=1) * (sg * (1.0 + z * (1.0 - sg)))
        return jnp.concatenate(dos, axis=1), dz, dg
    ins = [(don, "row"), (o, "row"), (qkvz, "row", (3, DN_WIDTH)), (gain_row, "full")]
    return _rows("dn_post_bwd", fn, ins, [(DN_WIDTH, F32), (DN_WIDTH, F32)], tr=256, accs=[((1, DN_DIM), F32)])


def _dn_prep_bwd(dq, dk, dv, dgb, u, ab, alog_row, dt_row):
    def fn(i, n, dqt, dkt, dvt, dgbt, ut, abt, al, dt):
        sg = _sigmoid(ut)
        y = ut * sg
        dys = []
        for grad, base, sc in ((dqt, 0, DN_DIM ** -0.5), (dkt, DN_WIDTH, 1.0)):
            for h in range(DN_HEADS):
                seg = y[:, base + h * DN_DIM:base + (h + 1) * DN_DIM]
                gr = grad[:, h * DN_DIM:(h + 1) * DN_DIM]
                r = lax.rsqrt(jnp.sum(seg * seg, axis=-1, keepdims=True) + EPS)
                xh = seg * r
                dys.append((r * sc) * (gr - xh * jnp.sum(gr * xh, axis=-1, keepdims=True)))
        dy = jnp.concatenate(dys + [dvt], axis=1)
        du = dy * (sg * (1.0 + ut * (1.0 - sg)))
        lane = lax.broadcasted_iota(jnp.int32, abt.shape, 1)
        is_g = lane < DN_HEADS
        ea = jnp.exp(al)
        x = abt + dt
        slope = -ea * _sigmoid(x)
        gval = -ea * _softplus(x)
        dg = jnp.where(is_g, dgbt, 0.0)
        beta = _sigmoid(abt)
        dab = jnp.where(is_g, dg * slope, jnp.where(lane < 2 * DN_HEADS, dgbt * beta * (1.0 - beta), 0.0))
        return du, dab, _colsum(dg * gval), _colsum(dg * slope)
    ins = [(dq, "row"), (dk, "row"), (dv, "row"), (dgb, "row"), (u, "row"), (ab, "row"), (alog_row, "full"), (dt_row, "full")]
    return _rows("dn_prep_bwd", fn, ins, [(DN_QKV, F32), (LANE, BF16)], tr=256, accs=[((1, LANE), F32)] * 2)


def _dn_conv_bwd(du, dz, qkvz, convw):
    tr = 256

    def fn(i, n, dut, dun, dzt, x, xp, w):
        dun = jnp.where(i < n - 1, dun, 0.0)
        dus = jnp.concatenate([dut, dun], axis=0)
        xs = jnp.concatenate([jnp.where(i > 0, xp, 0.0), x], axis=0)
        dx = None
        dws = []
        for j in range(CONV_W):
            sh = CONV_W - 1 - j
            term = (pltpu.roll(dus, tr + SUBLANE - sh, 0) if sh else dus)[:tr] * w[j:j + 1, :]
            dx = term if dx is None else dx + term
            dws.append(_colsum(dut * (pltpu.roll(xs, sh, 0) if sh else xs)[SUBLANE:]))
        return (jnp.concatenate([dx.astype(BF16), dzt.astype(BF16)], axis=1), *dws)

    ins = [(du, "row"), (du, "next8"), (dz, "row"), (qkvz, "row", (0, DN_QKV)), (qkvz, "prev8", (0, DN_QKV)), (convw, "full")]
    res = _rows("dn_conv_bwd", fn, ins, [(DN_QKVZ, BF16)], tr=tr, accs=[((1, DN_QKV), F32)] * CONV_W)
    return res[0], res[1:]


def _add(acc, r):
    return (r + acc,)


def _mlp_ple_fwd(i, x1, hm, p_i, ple_gain, next_gain, w_up, w_down, w_ple, w_gate):
    u, a = _mm(f"mlp_up{i}", hm, w_up, epilogue=lambda acc: (acc, jnp.square(jnp.maximum(acc, 0.0))),
               out_dtypes=(BF16, BF16))
    x2, hp = _mm(f"mlp_down{i}", a, w_down, epilogue=_res_norm, extras=(x1, ple_gain), out_dtypes=(F32, BF16),
                 tm_pref=FUSED_ROWS)
    pp = _mm(f"ple_proj{i}", p_i, w_ple)

    def gate_epilogue(acc, x2t, ppt, *g):
        gate = _sigmoid(acc)
        x3 = x2t + ppt * gate
        if not g:
            return x3, gate
        return x3, gate, x3 * lax.rsqrt(jnp.mean(x3 * x3, axis=-1, keepdims=True) + EPS) * g[0]

    more = () if next_gain is None else (next_gain,)
    x3, gate, *h_next = _mm(f"ple_gate{i}", hp, w_gate, epilogue=gate_epilogue, extras=(x2, pp) + more,
                            out_dtypes=(F32, F32) + (BF16,) * len(more), tm_pref=FUSED_ROWS)
    return x3, (h_next[0] if more else None), dict(x1=x1, hm=hm, u=u, a=a, x2=x2, hp=hp, pp=pp, gate=gate, p=p_i)


def _mlp_ple_bwd(i, dx3, sv, mlp_gain, ple_gain, w_up, w_down, w_gate):
    def fn(_i, _n, d, g, pp):
        return d * g, d * pp * g * (1.0 - g)
    dpp, dzg = _rows(f"ple_gate_bwd{i}", fn, [(dx3, "row"), (sv["gate"], "row"), (sv["pp"], "row")],
                     [(D_MODEL, BF16), (D_MODEL, BF16)], tr=512)
    d_w_ple = _mm(f"ple_proj_dw{i}", sv["p"], dpp, ta=True, out_dtypes=(BF16,))
    d_w_gate = _mm(f"ple_gate_dw{i}", sv["hp"], dzg, ta=True, out_dtypes=(BF16,))
    dx2, dx2b, d_ple_gain = _mm(f"ple_gate_dx{i}", dzg, w_gate, tb=True, epilogue=_norm_bwd_2,
                                extras=(sv["x2"], ple_gain, dx3), out_dtypes=(F32, BF16), n_colsums=1, tm_pref=FUSED_ROWS)
    d_w_down = _mm(f"mlp_down_dw{i}", sv["a"], dx2b, ta=True, out_dtypes=(BF16,))
    du = _mm(f"mlp_down_dx{i}", dx2b, w_down, tb=True,
             epilogue=lambda acc, ut: (acc * (2.0 * jnp.maximum(ut.astype(F32), 0.0)),), extras=(sv["u"],), out_dtypes=(BF16,))
    d_w_up = _mm(f"mlp_up_dw{i}", sv["hm"], du, ta=True, out_dtypes=(BF16,))
    dx1, dx1b, d_mlp_gain = _mm(f"mlp_up_dx{i}", du, w_up, tb=True, epilogue=_norm_bwd_2,
                                extras=(sv["x1"], mlp_gain, dx2), out_dtypes=(F32, BF16), n_colsums=1, tm_pref=FUSED_ROWS)
    return dx1, dx1b, dict(w_ple=d_w_ple, w_ple_gate=d_w_gate, w_down=d_w_down, w_up=d_w_up,
                           ple_norm=d_ple_gain, mlp_norm=d_mlp_gain)


def _loss_fwd_bwd(y, target):
    D = y.shape[1]

    def fn(i, n, yt, tt):
        e = yt - tt
        return e * (1.0 / D), _colsum(e * e)
    dy, sq = _rows("loss", fn, [(y, "row"), (target, "row")], [(D, F32)], tr=512, accs=[((1, D), F32)])
    return sq, dy


def _after(small, token):
    return small + token[0:1, 0:1]


def _local_step(x, p, positions, target, W, P, rest_of_weights, send_layer1, send_mlp0, send_attn):
    consts = _head_consts()
    bd = _block_diag(1.0 / A_HEAD_DIM)
    bd1 = _block_diag(1.0)
    ct, st = _rope_tables(positions, consts)
    gains = jnp.stack([jnp.tile(v, A_HEADS) for g in range(3) for v in (P["attn_q_gain"][g], P["attn_k_gain"][g])])
    pad = LANE - DN_HEADS
    alog_row = jnp.pad(P["dn_a_log"].reshape(1, DN_HEADS), ((0, 0), (0, pad)))
    dt_row = jnp.pad(P["dn_dt_bias"].reshape(1, DN_HEADS), ((0, 0), (0, pad)))
    ogain_row = P["dn_o_gain"].reshape(1, DN_DIM)
    row = lambda name, i: P[name][i:i + 1]

    h0 = _rmsnorm_fwd("mix_norm0", x, row("mix_norm", 0))
    qkv = _mm("attn_qkv", h0, W["attn_w_qkv"], out_dtypes=(BF16,))
    qkvn = _attn_prep(qkv, gains, ct, st, consts, bd)
    os_, lses = zip(*[_attn_fwd(qkvn[g], g) for g in range(3)])
    o_attn = _attn_merge(os_, lses)
    x1, hm0 = _mm("attn_out", o_attn, W["attn_w_o"], epilogue=_res_norm, extras=(x, row("mlp_norm", 0)),
                  out_dtypes=(F32, BF16), tm_pref=FUSED_ROWS)
    W = {**W, **rest_of_weights(x1)}
    x3, h1, sv0 = _mlp_ple_fwd(0, x1, hm0, p[0], row("ple_norm", 0), row("mix_norm", 1),
                               W["w_up"][0], W["w_down"][0], W["w_ple"][0], W["w_ple_gate"][0])
    qkvz = _mm("dn_in_qkvz", h1, W["dn_w_qkvz"])
    ab = _mm("dn_in_ab", h1, W["dn_w_ab"])
    u, q, k, v, gb = _dn_prep(qkvz, ab, W["dn_conv"], alog_row, dt_row)
    o_dn, states = _dn_chunk_fwd(q, k, v, gb)
    on = _dn_post(o_dn, qkvz, ogain_row)
    x4, hm1 = _mm("dn_out", on, W["dn_w_o"], epilogue=_res_norm, extras=(x3, row("mlp_norm", 1)),
                  out_dtypes=(F32, BF16), tm_pref=FUSED_ROWS)
    x6, _, sv1 = _mlp_ple_fwd(1, x4, hm1, p[1], row("ple_norm", 1), None,
                              W["w_up"][1], W["w_down"][1], W["w_ple"][1], W["w_ple_gate"][1])
    sq, dy = _loss_fwd_bwd(x6, target)

    dx4, dx4b, g1 = _mlp_ple_bwd(1, dy, sv1, row("mlp_norm", 1), row("ple_norm", 1),
                                 W["w_up"][1], W["w_down"][1], W["w_ple_gate"][1])
    don = _mm("dn_out_dx", dx4b, W["dn_w_o"], tb=True)
    d_dn_w_o = _mm("dn_out_dw", on, dx4b, ta=True, out_dtypes=(BF16,))
    do_dn, dz, d_ogain = _dn_post_bwd(don, o_dn, qkvz, ogain_row)
    dq, dk, dv, dgb = _dn_chunk_bwd(q, k, v, gb, states, do_dn)
    du, dab, d_alog, d_dt = _dn_prep_bwd(dq, dk, dv, dgb, u, ab, alog_row, dt_row)
    dqkvz, d_conv = _dn_conv_bwd(du, dz, qkvz, W["dn_conv"])
    dh1 = _mm("dn_in_ab_dx", dab, W["dn_w_ab"], tb=True)
    dx3, d_mix1 = _mm("dn_in_qkvz_dx", dqkvz, W["dn_w_qkvz"], tb=True,
                      epilogue=lambda acc, part, xt, g, dres: _norm_bwd(acc + part, xt, g, dres),
                      extras=(dh1, x3, row("mix_norm", 1), dx4), n_colsums=1, tm_pref=FUSED_ROWS)
    d_w_qkvz = _mm("dn_in_qkvz_dw", h1, dqkvz, ta=True, out_dtypes=(BF16,))
    d_w_ab = _mm("dn_in_ab_dw", h1, dab, ta=True, out_dtypes=(BF16,))
    token = send_layer1(dict(
        dn_w_qkvz=d_w_qkvz, dn_w_ab=d_w_ab, dn_conv=jnp.concatenate(d_conv, 0), dn_w_o=d_dn_w_o,
        w_up=g1["w_up"], w_down=g1["w_down"], w_ple=g1["w_ple"], w_ple_gate=g1["w_ple_gate"]))
    dx1, dx1b, g0 = _mlp_ple_bwd(0, dx3, sv0, row("mlp_norm", 0), _after(row("ple_norm", 0), token),
                                 W["w_up"][0], W["w_down"][0], W["w_ple_gate"][0])
    token = send_mlp0(dict(w_up=g0["w_up"], w_down=g0["w_down"], w_ple=g0["w_ple"], w_ple_gate=g0["w_ple_gate"]))
    do_attn = _mm("attn_out_dx", dx1b, W["attn_w_o"], tb=True, epilogue=_add, extras=(_after(jnp.zeros((1, A_WIDTH), F32), token),))
    d_attn_w_o = _mm("attn_out_dw", o_attn, dx1b, ta=True, out_dtypes=(BF16,))
    dos, cs = _attn_merge_bwd(do_attn, os_, lses, bd1)
    grads9 = []
    for g in range(3):
        grads9 += list(_attn_bwd(qkvn[g], g, dos[g], lses[g], cs[g]))
    dqkv, dgains = _attn_prep_bwd(qkv, grads9, gains, ct, st, consts, bd)
    d_attn_w_qkv = _mm("attn_qkv_dw", h0, dqkv, ta=True, out_dtypes=(BF16,))
    token = send_attn(dict(attn_w_qkv=d_attn_w_qkv, attn_w_o=d_attn_w_o))
    dx0, d_mix0 = _mm("attn_qkv_dx", dqkv, W["attn_w_qkv"], tb=True, epilogue=_norm_bwd,
                      extras=(x, _after(row("mix_norm", 0), token), dx1), n_colsums=1, tm_pref=FUSED_ROWS)

    dg = jnp.stack([t.reshape(A_HEADS, A_HEAD_DIM).sum(0) for t in dgains])
    small = dict(
        mix_norm=jnp.concatenate([d_mix0, d_mix1], 0),
        attn_q_gain=dg[0::2][None], attn_k_gain=dg[1::2][None],
        dn_a_log=d_alog[:, :DN_HEADS], dn_dt_bias=d_dt[:, :DN_HEADS], dn_o_gain=d_ogain,
        mlp_norm=jnp.concatenate([g0["mlp_norm"], g1["mlp_norm"]], 0),
        ple_norm=jnp.concatenate([g0["ple_norm"], g1["ple_norm"]], 0),
    )
    return sq, dx0, small


MESH_IDS = pl.DeviceIdType.MESH
ANY = pl.BlockSpec(memory_space=pl.ANY)


def _place():
    return lax.axis_index("x"), lax.axis_index("y"), lax.axis_index("c")


def _sem_scratch(n_streams):
    return [pltpu.SemaphoreType.DMA((n_streams, N_DEV - 1)), pltpu.SemaphoreType.DMA((n_streams, N_DEV - 1)),
            pltpu.SemaphoreType.DMA((n_streams,))]


def _all_gather(name, arrays, streams):
    n_in, n_st = len(arrays), len(streams)
    shapes = [arrays[a].shape if li is None else arrays[a].shape[1:] for a, li in streams]

    def body(*refs):
        in_refs, out_refs, token = refs[:n_in], refs[n_in:n_in + n_st], refs[n_in + n_st]
        send_sems, recv_sems, local_sems = refs[n_in + n_st + 1:]
        token[...] = jnp.zeros_like(token)
        x, y, c = _place()
        me, sibling = (x, y, c), (x, y, 1 - c)
        chips = [(1 - x, y), (x, 1 - y), (1 - x, 1 - y)]

        def copy(s, k, block, to, own=False):
            a, li = streams[s]
            dst = out_refs[s].at[4 * block[0] + 2 * block[1] + block[2]]
            src = (in_refs[a] if li is None else in_refs[a].at[li]) if own else dst
            return pltpu.make_async_remote_copy(src_ref=src, dst_ref=dst, send_sem=send_sems.at[s, k],
                                                recv_sem=recv_sems.at[s, k], device_id=to, device_id_type=MESH_IDS)

        started = []
        for s, (a, li) in enumerate(streams):
            src = in_refs[a] if li is None else in_refs[a].at[li]
            mine = pltpu.make_async_copy(src, out_refs[s].at[4 * x + 2 * y + c], local_sems.at[s])
            mine.start()
            started.append(mine)
        sends = []
        for s in range(n_st):
            first = [copy(s, 0, me, sibling, own=True)]
            first += [copy(s, 1 + j, me, (*chip, c), own=True) for j, chip in enumerate(chips)]
            for cp in first:
                cp.start()
            sends += first
        for j, chip in enumerate(chips):
            for s in range(n_st):
                copy(s, 1 + j, (*chip, c), me).wait_recv()
                fwd = copy(s, 4 + j, (*chip, c), sibling)
                fwd.start()
                sends.append(fwd)
        for s in range(n_st):
            copy(s, 0, sibling, me).wait_recv()
            for j, chip in enumerate(chips):
                copy(s, 4 + j, (*chip, 1 - c), me).wait_recv()
        for cp in sends:
            cp.wait_send()
        for cp in started:
            cp.wait()

    res = pl.pallas_call(
        body, name=name,
        out_shape=[jax.ShapeDtypeStruct((N_DEV,) + tuple(sh), arrays[a].dtype) for sh, (a, _) in zip(shapes, streams)]
        + [jax.ShapeDtypeStruct((SUBLANE, LANE), F32)],
        in_specs=[ANY] * n_in, out_specs=[ANY] * n_st + [pl.BlockSpec(memory_space=pltpu.VMEM)],
        scratch_shapes=_sem_scratch(n_st),
    )(*arrays)
    return list(res[:n_st]), res[n_st]


HBM = pl.BlockSpec(memory_space=pltpu.HBM)
SEM = pl.BlockSpec(memory_space=pltpu.SEMAPHORE)
FLOWS = pltpu.CompilerParams(has_side_effects=pltpu.SideEffectType.DATAFLOW_SIDE_EFFECTING)


def _in_hbm(a):
    return pltpu.with_memory_space_constraint(a, pltpu.HBM)


def _hbm_like(a):
    return pltpu.HBM(a.shape, a.dtype)


def _peers(x, y, c):
    return [(1 - x if k & 4 else x, 1 - y if k & 2 else y, 1 - c if k & 1 else c) for k in range(1, N_DEV)]


def _start_copies(name, n_remote, n_own, make_copies, operands):
    n = len(operands)

    def body(*refs):
        for cp in make_copies(refs[:n], refs[n], refs[n + 1], refs[n + 2]):
            cp.start()
        refs[-1][...] = jnp.zeros_like(refs[-1])

    res = pl.pallas_call(
        body, name=name,
        out_shape=(pltpu.SemaphoreType.DMA((n_remote,)), pltpu.SemaphoreType.DMA((n_remote,)), pltpu.SemaphoreType.DMA((n_own,)),
                   *[_hbm_like(t) for t in operands], jax.ShapeDtypeStruct((SUBLANE, LANE), F32)),
        in_specs=[HBM] * n, out_specs=(SEM, SEM, SEM, *[HBM] * n, pl.BlockSpec(memory_space=pltpu.VMEM)),
        input_output_aliases={i: 3 + i for i in range(n)}, compiler_params=FLOWS,
    )(*[_in_hbm(t) for t in operands])
    return res[:3], list(res[3:3 + n]), res[-1]


def _wait_copies(name, make_waits, sems, operands, after):
    n = len(operands)

    def body(*refs):
        for wait in make_waits(refs[:n], refs[n], refs[n + 1], refs[n + 2]):
            wait()

    res = pl.pallas_call(
        body, name=name, out_shape=tuple(_hbm_like(t) for t in operands),
        in_specs=[HBM] * n + [SEM, SEM, SEM, ANY], out_specs=tuple([HBM] * n),
        input_output_aliases={i: i for i in range(n)}, compiler_params=FLOWS,
    )(*operands, *sems, after)
    return list(res)


def _gather_plan(n_in, streams):
    def block(arr, s):
        a, li = streams[s]
        return arr[a] if li is None else arr[a].at[li]

    def copies(refs, send_sems, recv_sems, own_sems, arrivals=False):
        arr, land = refs[:n_in], refs[n_in:]
        x, y, c = _place()
        me = 4 * x + 2 * y + c
        out = []
        for s in range(len(streams)):
            out.append(("own", pltpu.make_async_copy(block(arr, s), land[s].at[me], own_sems.at[s])))
            for k, (px, py, pc) in enumerate(_peers(x, y, c)):
                out.append(("remote", pltpu.make_async_remote_copy(
                    src_ref=block(arr, s), dst_ref=land[s].at[4 * px + 2 * py + pc if arrivals else me],
                    send_sem=send_sems.at[s * (N_DEV - 1) + k], recv_sem=recv_sems.at[s * (N_DEV - 1) + k],
                    device_id=(px, py, pc), device_id_type=MESH_IDS)))
        return out
    return copies


def _exchange_plan(n_st):
    def copies(refs, send_sems, recv_sems, own_sems, arrivals=False):
        snd, rcv = refs[:n_st], refs[n_st:]
        x, y, c = _place()
        me = 4 * x + 2 * y + c
        out = []
        for s in range(n_st):
            out.append(("own", pltpu.make_async_copy(snd[s].at[me], rcv[s].at[me], own_sems.at[s])))
            for k, (px, py, pc) in enumerate(_peers(x, y, c)):
                peer = 4 * px + 2 * py + pc
                out.append(("remote", pltpu.make_async_remote_copy(
                    src_ref=snd[s].at[peer], dst_ref=rcv[s].at[peer if arrivals else me],
                    send_sem=send_sems.at[s * (N_DEV - 1) + k], recv_sem=recv_sems.at[s * (N_DEV - 1) + k],
                    device_id=(px, py, pc), device_id_type=MESH_IDS)))
        return out
    return copies


def _split_transfer(tag, plan, n_streams, operands):
    sems, operands, token = _start_copies(f"{tag}_start", n_streams * (N_DEV - 1), n_streams,
                                          lambda refs, a, b, o: [cp for _, cp in plan(refs, a, b, o)], operands)

    def waits(refs, a, b, o):
        out = []
        for kind, cp in plan(refs, a, b, o, arrivals=True):
            out += [cp.wait] if kind == "own" else [cp.wait_send, cp.wait_recv]
        return out

    return (lambda after: _wait_copies(f"{tag}_wait", waits, sems, operands, after)), token


def _gather_async(tag, arrays, streams):
    lands = [lax.empty((N_DEV,) + tuple(arrays[a].shape if li is None else arrays[a].shape[1:]), arrays[a].dtype)
             for a, li in streams]
    finish, token = _split_transfer(tag, _gather_plan(len(arrays), streams), len(streams), list(arrays) + lands)
    return (lambda after: finish(after)[len(arrays):]), token


def _exchange_async(tag, sends):
    recvs = [lax.empty(t.shape, t.dtype) for t in sends]
    finish, token = _split_transfer(tag, _exchange_plan(len(sends)), len(sends), list(sends) + recvs)
    return (lambda after: finish(after)[len(sends):]), token


def _dn_in_pieces():
    n = (DN_QKVZ + 2 * DN_HEADS) // N_DEV
    segs = ((0, DN_QKV, 0, 0), (DN_QKV, DN_QKV + 2 * DN_HEADS, 1, 0), (DN_QKV + 2 * DN_HEADS, DN_QKVZ + 2 * DN_HEADS, 0, DN_QKV))
    out = []
    for d in range(N_DEV):
        lo, hi = d * n, (d + 1) * n
        for s0, s1, tgt, t0 in segs:
            a, b = max(lo, s0), min(hi, s1)
            if a < b:
                out.append((d, a - lo, b - lo, tgt, t0 + a - s0))
    return out


def _unpack_cols(name, g):
    _, K, n = g.shape
    tr = 256

    def body(g_ref, o_ref):
        for d in range(N_DEV):
            o_ref[:, d * n:(d + 1) * n] = g_ref[d]

    return pl.pallas_call(
        body, name=name, grid=(K // tr,), in_specs=[pl.BlockSpec((N_DEV, tr, n), lambda i: (0, i, 0))],
        out_specs=pl.BlockSpec((tr, N_DEV * n), lambda i: (i, 0)),
        out_shape=jax.ShapeDtypeStruct((K, N_DEV * n), g.dtype), compiler_params=_cparams(("parallel",)),
    )(g)


def _pack_cols(name, w):
    K, n = w.shape[0], w.shape[1] // N_DEV
    tr = 256

    def body(w_ref, o_ref):
        for d in range(N_DEV):
            o_ref[d] = w_ref[:, d * n:(d + 1) * n]

    return pl.pallas_call(
        body, name=name, grid=(K // tr,), in_specs=[pl.BlockSpec((tr, N_DEV * n), lambda i: (i, 0))],
        out_specs=pl.BlockSpec((N_DEV, tr, n), lambda i: (0, i, 0)),
        out_shape=jax.ShapeDtypeStruct((N_DEV, K, n), w.dtype), compiler_params=_cparams(("parallel",)),
    )(w)


def _unpack_dn_in(g):
    _, K, n = g.shape
    tr = 256

    def body(g_ref, qkvz_ref, ab_ref):
        ab_ref[...] = jnp.zeros_like(ab_ref)
        for d, c0, c1, tgt, t0 in _dn_in_pieces():
            (qkvz_ref, ab_ref)[tgt][:, t0:t0 + c1 - c0] = g_ref[d, :, c0:c1]

    return pl.pallas_call(
        body, name="unpack_dn_in", grid=(K // tr,), in_specs=[pl.BlockSpec((N_DEV, tr, n), lambda i: (0, i, 0))],
        out_specs=[pl.BlockSpec((tr, DN_QKVZ), lambda i: (i, 0)), pl.BlockSpec((tr, LANE), lambda i: (i, 0))],
        out_shape=[jax.ShapeDtypeStruct((K, DN_QKVZ), g.dtype), jax.ShapeDtypeStruct((K, LANE), g.dtype)],
        compiler_params=_cparams(("parallel",)),
    )(g)


def _pack_dn_in(d_qkvz, d_ab):
    K = d_qkvz.shape[0]
    n = (DN_QKVZ + 2 * DN_HEADS) // N_DEV
    tr = 256

    def body(qkvz_ref, ab_ref, o_ref):
        for d, c0, c1, tgt, t0 in _dn_in_pieces():
            o_ref[d, :, c0:c1] = (qkvz_ref, ab_ref)[tgt][:, t0:t0 + c1 - c0]

    return pl.pallas_call(
        body, name="pack_dn_in", grid=(K // tr,),
        in_specs=[pl.BlockSpec((tr, DN_QKVZ), lambda i: (i, 0)), pl.BlockSpec((tr, LANE), lambda i: (i, 0))],
        out_specs=pl.BlockSpec((N_DEV, tr, n), lambda i: (0, i, 0)),
        out_shape=jax.ShapeDtypeStruct((N_DEV, K, n), d_qkvz.dtype), compiler_params=_cparams(("parallel",)),
    )(d_qkvz, d_ab)


ADAMW_ROWS = 256


def _adamw(name, parts, w, m, v):
    R, C = w.shape
    tr = min(R, ADAMW_ROWS)
    assert R % tr == 0 and parts.shape == (N_DEV, R, C)
    c1 = 1.0 - B1 ** STEP
    c2 = 1.0 - B2 ** STEP

    def body(p_ref, w_ref, m_ref, v_ref, g_ref, d_ref, nm_ref, nv_ref):
        g = p_ref[0].astype(F32)
        for dev in range(1, N_DEV):
            g = g + p_ref[dev].astype(F32)
        nm = B1 * m_ref[...] + (1.0 - B1) * g
        nv = B2 * v_ref[...] + (1.0 - B2) * jnp.square(g)
        g_ref[...] = g
        nm_ref[...] = nm
        nv_ref[...] = nv
        d_ref[...] = -LR * ((nm / c1) / (jnp.sqrt(nv / c2) + ADAM_EPS) + WD * w_ref[...])

    blk = pl.BlockSpec((tr, C), lambda i: (i, 0))
    return pl.pallas_call(
        body, name=name, grid=(R // tr,),
        in_specs=[pl.BlockSpec((N_DEV, tr, C), lambda i: (0, i, 0)), blk, blk, blk],
        out_specs=[blk] * 4, out_shape=[jax.ShapeDtypeStruct((R, C), F32)] * 4,
        compiler_params=_cparams(("parallel",)),
    )(parts, w, m, v)


SMALL = ("mix_norm", "attn_q_gain", "attn_k_gain", "dn_a_log", "dn_dt_bias", "dn_o_gain", "mlp_norm", "ple_norm")
WEIGHTS = ("mix_norm", "attn_w_qkv", "attn_q_gain", "attn_k_gain", "attn_w_o", "dn_w_in", "dn_conv", "dn_a_log",
           "dn_dt_bias", "dn_o_gain", "dn_w_o", "mlp_norm", "w_up", "w_down", "ple_norm", "w_ple", "w_ple_gate")


def _to_rows(flat, multiple):
    n = flat.shape[-1]
    rows = -(-n // (LANE * multiple)) * multiple
    return jnp.pad(flat, [(0, rows * LANE - n)]).reshape(rows, LANE)


def _cols_to_devices(w):
    K, N = w.shape
    return jnp.transpose(w.reshape(K, N_DEV, N // N_DEV), (1, 0, 2))


def _cols_from_devices(g):
    _, K, n = g.shape
    return jnp.transpose(g, (1, 0, 2)).reshape(K, N_DEV * n)


SMALL_ROWS = 96


def _pack_small(vals, loss_rows):
    rows = [_to_rows(vals[n].reshape(-1), SUBLANE) for n in SMALL] + [loss_rows]
    buf = jnp.concatenate(rows, 0)
    assert buf.shape == (SMALL_ROWS, LANE)
    return buf


def _unpack_small(buf, like):
    out, r = {}, 0
    for n in SMALL:
        sz = math.prod(like[n].shape)
        out[n] = buf[r:r + -(-sz // LANE)].reshape(-1)[:sz].reshape(like[n].shape)
        r += -(-sz // (LANE * SUBLANE)) * SUBLANE
    return out


def kernel(x, p, positions, mix_norm, attn_w_qkv, attn_q_gain, attn_k_gain, attn_w_o, dn_w_in, dn_conv, dn_a_log, dn_dt_bias, dn_o_gain, dn_w_o, mlp_norm, w_up, w_down, ple_norm, w_ple, w_ple_gate, loss_target, m_mix_norm, m_attn_w_qkv, m_attn_q_gain, m_attn_k_gain, m_attn_w_o, m_dn_w_in, m_dn_conv, m_dn_a_log, m_dn_dt_bias, m_dn_o_gain, m_dn_w_o, m_mlp_norm, m_w_up, m_w_down, m_ple_norm, m_w_ple, m_w_ple_gate, v_mix_norm, v_attn_w_qkv, v_attn_q_gain, v_attn_k_gain, v_attn_w_o, v_dn_w_in, v_dn_conv, v_dn_a_log, v_dn_dt_bias, v_dn_o_gain, v_dn_w_o, v_mlp_norm, v_w_up, v_w_down, v_ple_norm, v_w_ple, v_w_ple_gate):
    w = dict(mix_norm=mix_norm, attn_w_qkv=attn_w_qkv, attn_q_gain=attn_q_gain, attn_k_gain=attn_k_gain, attn_w_o=attn_w_o,
             dn_w_in=dn_w_in, dn_conv=dn_conv, dn_a_log=dn_a_log, dn_dt_bias=dn_dt_bias, dn_o_gain=dn_o_gain, dn_w_o=dn_w_o,
             mlp_norm=mlp_norm, w_up=w_up, w_down=w_down, ple_norm=ple_norm, w_ple=w_ple, w_ple_gate=w_ple_gate)
    m = dict(mix_norm=m_mix_norm, attn_w_qkv=m_attn_w_qkv, attn_q_gain=m_attn_q_gain, attn_k_gain=m_attn_k_gain,
             attn_w_o=m_attn_w_o, dn_w_in=m_dn_w_in, dn_conv=m_dn_conv, dn_a_log=m_dn_a_log, dn_dt_bias=m_dn_dt_bias,
             dn_o_gain=m_dn_o_gain, dn_w_o=m_dn_w_o, mlp_norm=m_mlp_norm, w_up=m_w_up, w_down=m_w_down,
             ple_norm=m_ple_norm, w_ple=m_w_ple, w_ple_gate=m_w_ple_gate)
    v = dict(mix_norm=v_mix_norm, attn_w_qkv=v_attn_w_qkv, attn_q_gain=v_attn_q_gain, attn_k_gain=v_attn_k_gain,
             attn_w_o=v_attn_w_o, dn_w_in=v_dn_w_in, dn_conv=v_dn_conv, dn_a_log=v_dn_a_log, dn_dt_bias=v_dn_dt_bias,
             dn_o_gain=v_dn_o_gain, dn_w_o=v_dn_w_o, mlp_norm=v_mlp_norm, w_up=v_w_up, w_down=v_w_down,
             ple_norm=v_ple_norm, w_ple=v_w_ple, w_ple_gate=v_w_ple_gate)
    S = x.shape[1]

    bf = lambda a: a.astype(BF16)
    rows_to_devices = lambda t: t.reshape(N_DEV, t.shape[0] // N_DEV, t.shape[1])

    (g_qkv, g_ao), token = _all_gather("gather_attn", [bf(attn_w_qkv[0]), bf(attn_w_o[0])], [(0, None), (1, None)])
    rest_shards = [bf(dn_w_in[0]), bf(dn_w_o[0]), bf(w_up), bf(w_down), bf(w_ple), bf(w_ple_gate), _after(dn_conv[0], token)]
    rest_streams = [(0, None), (1, None), (2, 0), (2, 1), (3, 0), (3, 1), (4, 0), (4, 1), (5, 0), (5, 1), (6, None)]
    rest_arrived, token = _gather_async("gather_rest", rest_shards, rest_streams)
    W = dict(attn_w_qkv=_unpack_cols("unpack_attn_qkv", g_qkv), attn_w_o=_cols_from_devices(g_ao))

    def rest_of_weights(after):
        g_in, g_do, g_up0, g_up1, g_dn0, g_dn1, g_pl0, g_pl1, g_gt0, g_gt1, g_conv = rest_arrived(after)
        rest = dict(
            dn_conv=jnp.transpose(g_conv, (1, 0, 2)).reshape(CONV_W, DN_QKV), dn_w_o=g_do.reshape(DN_WIDTH, D_MODEL),
            w_up=[_cols_from_devices(g_up0), _cols_from_devices(g_up1)],
            w_down=[g_dn0.reshape(D_FF, D_MODEL), g_dn1.reshape(D_FF, D_MODEL)],
            w_ple=[_cols_from_devices(g_pl0), _cols_from_devices(g_pl1)],
            w_ple_gate=[g_gt0.reshape(D_MODEL, D_MODEL), g_gt1.reshape(D_MODEL, D_MODEL)])
        rest["dn_w_qkvz"], rest["dn_w_ab"] = _unpack_dn_in(g_in)
        return rest

    pending = {}

    def mlp_sends(g):
        return [_cols_to_devices(g["w_up"]), rows_to_devices(g["w_down"]), _cols_to_devices(g["w_ple"]),
                rows_to_devices(g["w_ple_gate"])]

    def start(tag, sends):
        pending[tag], token = _exchange_async(f"exchange_{tag}", sends)
        return token

    def send_layer1(g):
        conv_send = jnp.transpose(g["dn_conv"].reshape(CONV_W, N_DEV, DN_QKV // N_DEV), (1, 0, 2))
        return start("layer1", [_pack_dn_in(g["dn_w_qkvz"], g["dn_w_ab"]), conv_send, rows_to_devices(g["dn_w_o"])] + mlp_sends(g))

    def send_mlp0(g):
        return start("mlp0", mlp_sends(g))

    def send_attn(g):
        return start("attn", [_pack_cols("pack_attn_qkv", g["attn_w_qkv"]), _cols_to_devices(g["attn_w_o"])])

    P = dict(mix_norm=_after(mix_norm, token), attn_q_gain=attn_q_gain[0], attn_k_gain=attn_k_gain[0], dn_a_log=dn_a_log[0],
             dn_dt_bias=dn_dt_bias[0], dn_o_gain=dn_o_gain[0], mlp_norm=mlp_norm, ple_norm=ple_norm)

    sq, dx0, small_g = _local_step(x[0], p[:, 0], positions.reshape(S, 1), loss_target[0], W, P,
                                   rest_of_weights, send_layer1, send_mlp0, send_attn)

    r_in, r_conv, r_do, r_up1, r_dn1, r_pl1, r_gt1 = pending["layer1"](dx0)
    r_up0, r_dn0, r_pl0, r_gt0 = pending["mlp0"](dx0)
    r_qkv, r_ao = pending["attn"](dx0)
    big = {}
    for n, parts in (("attn_w_qkv", [r_qkv]), ("attn_w_o", [r_ao]), ("dn_w_in", [r_in]), ("dn_conv", [r_conv]),
                     ("dn_w_o", [r_do]), ("w_up", [r_up0, r_up1]), ("w_down", [r_dn0, r_dn1]),
                     ("w_ple", [r_pl0, r_pl1]), ("w_ple_gate", [r_gt0, r_gt1])):
        layers = [_adamw(f"adamw_{n}{l}", pt, w[n][l], m[n][l], v[n][l]) for l, pt in enumerate(parts)]
        big[n] = [jnp.stack([res[k] for res in layers]) for k in range(4)]

    loss_rows = jnp.pad((0.5 / D_MODEL) * jnp.sum(sq, axis=1, keepdims=True), ((0, SUBLANE - 1), (0, LANE - 1)))
    small_like = {n: w[n] for n in SMALL}
    parts_s = _all_gather("gather_small", [_pack_small(small_g, loss_rows)], [(0, None)])[0][0]
    zero_rows = jnp.zeros((SUBLANE, LANE), F32)
    small = _adamw("adamw_small", parts_s, _pack_small(w, zero_rows), _pack_small(m, zero_rows), _pack_small(v, zero_rows))
    loss = small[0][SMALL_ROWS - SUBLANE, 0]
    small = [_unpack_small(b, small_like) for b in small]

    outs = [loss, dx0[None]]
    for k in range(4):
        for n in WEIGHTS:
            outs.append(small[k][n] if n in SMALL else big[n][k])
    return tuple(outs)
```

```python
import functools
import math

import jax
import jax.numpy as jnp
from jax import lax
from jax.experimental import pallas as pl
from jax.experimental.pallas import tpu as pltpu

F32 = jnp.float32
BF16 = jnp.bfloat16
HIGHEST = lax.Precision.HIGHEST

N_DEV = 8
D_MODEL = 1024
EPS = 1e-6
SWA_GROUPS = ((128, 1), (512, 4), (2048, 16))
A_HEADS = 8
A_HEAD_DIM = 64
A_WIDTH = A_HEADS * A_HEAD_DIM
A_QKV = 3 * 3 * A_WIDTH
ROPE_DIM = 16
ROPE_HALF = 8
ROPE_THETA = 500000.0
BAND = 128
DN_HEADS = 8
DN_DIM = 128
DN_WIDTH = DN_HEADS * DN_DIM
CONV_W = 4
CHUNK = 64
D_FF = 4 * D_MODEL
PLE_DIM = 256
LR, B1, B2, ADAM_EPS, WD, STEP = 0.001, 0.9, 0.999, 1e-08, 0.01, 10

VMEM_LIMIT = 56 * 1024 * 1024
MXU_TILE = 1024
MM_SLAB = 256
LANE = 128
SUBLANE = 8


def _cparams(sem):
    return pltpu.CompilerParams(dimension_semantics=sem, vmem_limit_bytes=VMEM_LIMIT)


def _tile(n, pref):
    if n <= pref:
        return n
    t = (pref // LANE) * LANE
    while t >= LANE:
        if n % t == 0:
            return t
        t -= LANE
    raise ValueError(f"no tile for {n}")


def _dot(a, b, ca=1, cb=0, precision=None):
    return lax.dot_general(a, b, (((ca,), (cb,)), ((), ())), precision=precision,
                           preferred_element_type=F32)


def _bdot(a, b, ca=1, cb=0):
    return _dot(a.astype(BF16), b.astype(BF16), ca, cb)


def _mm(name, a, b, *, ta=False, tb=False, epilogue=None, extras=(), out_dtypes=(F32,), n_colsums=0,
        tm_pref=MXU_TILE, tn_pref=1536, tk_pref=MXU_TILE):
    M, K = (a.shape[1], a.shape[0]) if ta else a.shape
    N = b.shape[0] if tb else b.shape[1]
    assert (b.shape[1] if tb else b.shape[0]) == K
    tm, tn, tk = _tile(M, tm_pref), _tile(N, tn_pref), _tile(K, tk_pref)
    nk = K // tk
    n_out = len(out_dtypes)
    n_ext = len(extras)
    assert n_colsums == 0 or tn == N
    sub = min(tm, MM_SLAB)

    def body(*refs):
        a_ref, b_ref = refs[0], refs[1]
        ext = refs[2:2 + n_ext]
        outs = refs[2 + n_ext:2 + n_ext + n_out]
        sums = refs[2 + n_ext + n_out:2 + n_ext + n_out + n_colsums]
        row_tile, k = pl.program_id(0), pl.program_id(2)
        slabs = [slice(s * sub, (s + 1) * sub) for s in range(tm // sub)]

        def product(rows):
            return _bdot(a_ref[:, rows] if ta else a_ref[rows, :], b_ref[...], 0 if ta else 1, 1 if tb else 0)

        def finish(results):
            col_rows = []
            for rows, r in zip(slabs, results):
                res = (r,) if epilogue is None else epilogue(r, *[e[...] if e.shape[0] == 1 else e[rows, :] for e in ext])
                for o, v in zip(outs, res):
                    o[rows, :] = v.astype(o.dtype)
                col_rows.append(res[n_out:])
            for n, o in enumerate(sums):
                v = functools.reduce(lambda x, y: x + y, [c[n] for c in col_rows])

                @pl.when(row_tile == 0)
                def _(o=o, v=v):
                    o[...] = v

                @pl.when(row_tile > 0)
                def _(o=o, v=v):
                    o[...] += v

        if nk == 1:
            finish([product(rows) for rows in slabs])
            return
        acc = refs[-1]

        @pl.when(k == 0)
        def _():
            acc[...] = jnp.zeros_like(acc)

        for rows in slabs:
            acc[rows, :] += product(rows)

        @pl.when(k == nk - 1)
        def _():
            finish([acc[rows, :] for rows in slabs])

    a_spec = pl.BlockSpec((tk, tm), lambda i, j, k: (k, i)) if ta else pl.BlockSpec((tm, tk), lambda i, j, k: (i, k))
    b_spec = pl.BlockSpec((tn, tk), lambda i, j, k: (j, k)) if tb else pl.BlockSpec((tk, tn), lambda i, j, k: (k, j))
    ext_specs = []
    for e in extras:
        if e.shape[0] == 1 and M != 1:
            ext_specs.append(pl.BlockSpec((1, tn), lambda i, j, k: (0, j)))
        else:
            ext_specs.append(pl.BlockSpec((tm, tn), lambda i, j, k: (i, j)))
    out = pl.pallas_call(
        body, name=name,
        grid=(M // tm, N // tn, nk),
        in_specs=[a_spec, b_spec] + ext_specs,
        out_specs=[pl.BlockSpec((tm, tn), lambda i, j, k: (i, j)) for _ in range(n_out)]
        + [pl.BlockSpec((1, tn), lambda i, j, k: (0, 0)) for _ in range(n_colsums)],
        out_shape=[jax.ShapeDtypeStruct((M, N), dt) for dt in out_dtypes]
        + [jax.ShapeDtypeStruct((1, N), F32) for _ in range(n_colsums)],
        scratch_shapes=[pltpu.VMEM((tm, tn), F32)] if nk > 1 else [],
        compiler_params=_cparams(("arbitrary" if n_colsums else "parallel", "parallel", "arbitrary")),
    )(a, b, *extras)
    return out[0] if len(out) == 1 else tuple(out)


def _rows(name, fn, ins, outs, *, tr, accs=()):
    ins = [(e[0], e[1]) + (e[2] if len(e) > 2 else (0, e[0].shape[-1])) for e in ins]
    n_rows = next(e[0].shape[0] for e in ins if e[1] == "row")
    assert n_rows % tr == 0 and tr % SUBLANE == 0
    steps = n_rows // tr
    t8 = tr // SUBLANE
    n8 = n_rows // SUBLANE
    n_in, n_out, n_acc = len(ins), len(outs), len(accs)

    def body(*refs):
        i = pl.program_id(0)
        vals = fn(i, steps, *[r[...] for r in refs[:n_in]])
        if not isinstance(vals, (tuple, list)):
            vals = (vals,)
        assert len(vals) == n_out + n_acc
        for o, v in zip(refs[n_in:n_in + n_out], vals[:n_out]):
            o[...] = v.astype(o.dtype)
        if n_acc:
            acc_refs = refs[n_in + n_out:]

            @pl.when(i == 0)
            def _():
                for r in acc_refs:
                    r[...] = jnp.zeros_like(r)

            for r, v in zip(acc_refs, vals[n_out:]):
                r[...] += v.astype(r.dtype)

    in_specs = []
    for a, kind, cb, c in ins:
        if kind == "row":
            in_specs.append(pl.BlockSpec((tr, c), lambda i, cb=cb: (i, cb)))
        elif kind == "full":
            in_specs.append(pl.BlockSpec(a.shape, lambda i, z=(0,) * a.ndim: z))
        elif kind == "prev8":
            in_specs.append(pl.BlockSpec((SUBLANE, c), lambda i, cb=cb: (jnp.maximum(i * t8 - 1, 0), cb)))
        elif kind == "next8":
            in_specs.append(pl.BlockSpec((SUBLANE, c), lambda i, cb=cb: (jnp.minimum((i + 1) * t8, n8 - 1), cb)))
        else:
            raise ValueError(kind)
    out_specs = [pl.BlockSpec((tr, c), lambda i: (i, 0)) for c, _ in outs]
    out_specs += [pl.BlockSpec(s, lambda i, z=(0,) * len(s): z) for s, _ in accs]
    out_shape = [jax.ShapeDtypeStruct((n_rows, c), dt) for c, dt in outs]
    out_shape += [jax.ShapeDtypeStruct(s, dt) for s, dt in accs]
    res = pl.pallas_call(
        body, name=name, grid=(steps,), in_specs=in_specs, out_specs=out_specs, out_shape=out_shape,
        compiler_params=_cparams(("arbitrary",) if n_acc else ("parallel",)),
    )(*[e[0] for e in ins])
    return res[0] if len(res) == 1 else tuple(res)


def _colsum(x):
    return jnp.sum(x, axis=0, keepdims=True)


def _sum_all(x):
    return jnp.sum(jnp.sum(x, axis=1, keepdims=True), axis=0, keepdims=True)


def _rmsnorm_fwd(name, x, gain):
    def fn(i, n, xt, g):
        r = lax.rsqrt(jnp.mean(xt * xt, axis=-1, keepdims=True) + EPS)
        return (xt * r * g,)
    return _rows(name, fn, [(x, "row"), (gain, "full")], [(x.shape[1], BF16)], tr=512)


FUSED_ROWS = 1024


def _res_norm(acc, res, g):
    x = res + acc
    return x, x * lax.rsqrt(jnp.mean(x * x, axis=-1, keepdims=True) + EPS) * g


def _norm_bwd(dh, x, g, dres):
    r = lax.rsqrt(jnp.mean(x * x, axis=-1, keepdims=True) + EPS)
    xh = x * r
    dxn = dh * g
    dx = dres + r * (dxn - xh * jnp.mean(dxn * xh, axis=-1, keepdims=True))
    return dx, _colsum(dh * xh)


def _norm_bwd_2(dh, x, g, dres):
    dx, dg = _norm_bwd(dh, x, g, dres)
    return dx, dx, dg


def _head_consts():
    import numpy as np
    e = np.arange(A_WIDTH) % A_HEAD_DIM
    inv = (np.float32(ROPE_THETA) ** (-np.arange(0, ROPE_DIM, 2, dtype=np.float32) / np.float32(ROPE_DIM))).astype(np.float32)
    c = np.zeros((8, A_WIDTH), np.float32)
    c[0] = np.where(e < ROPE_DIM, inv[e % ROPE_HALF], 0.0)
    c[1] = np.where(e < ROPE_HALF, -1.0, np.where(e < ROPE_DIM, 1.0, 0.0))
    c[2] = (e < ROPE_HALF).astype(np.float32)
    c[3] = (e < ROPE_DIM).astype(np.float32)
    return jnp.asarray(c)


def _block_diag(scale):
    import numpy as np
    h = np.arange(A_WIDTH) // A_HEAD_DIM
    return jnp.asarray((h[:, None] == h[None, :]).astype(np.float32) * scale, dtype=BF16)


def _seg_sum(x, bd):
    hi = x.astype(BF16)
    lo = (x - hi.astype(F32)).astype(BF16)
    return _dot(hi, bd) + _dot(lo, bd)


def _rope_tables(positions, consts):
    def fn(i, n, pos, c):
        ang = pos.astype(F32) * c[0:1, :LANE]
        return jnp.cos(ang), jnp.sin(ang) * c[1:2, :LANE]
    return _rows("rope_tables", fn, [(positions, "row"), (consts, "full")], [(LANE, F32), (LANE, F32)], tr=512)


def _rope_wide(t):
    return jnp.concatenate([t] * (A_WIDTH // LANE), axis=1)


def _rope_apply(y, ct, st, low):
    rolled = jnp.where(low, pltpu.roll(y, A_WIDTH - ROPE_HALF, 1), pltpu.roll(y, ROPE_HALF, 1))
    return y * ct + rolled * st


def _rope_apply_bwd(dout, ct, st, low, in16):
    t = dout * st
    back = jnp.where(low, pltpu.roll(t, A_WIDTH - ROPE_HALF, 1), jnp.where(in16, pltpu.roll(t, ROPE_HALF, 1), 0.0))
    return dout * ct + back


def _attn_prep(qkv, gains, ct, st, consts, bd):
    def fn(i, n, t, g, c_t, s_t, c, b):
        low = c[2:3, :] > 0.5
        c_t, s_t = _rope_wide(c_t), _rope_wide(s_t)
        groups = []
        for grp in range(3):
            cols = []
            for which in range(3):
                off = (grp * 3 + which) * A_WIDTH
                x = t[:, off:off + A_WIDTH].astype(F32)
                if which == 2:
                    cols.append(x.astype(BF16))
                    continue
                r = lax.rsqrt(_seg_sum(x * x, b) + EPS)
                y = x * r * g[grp * 2 + which:grp * 2 + which + 1, :]
                cols.append(_rope_apply(y, c_t, s_t, low).astype(BF16))
            groups.append(jnp.concatenate(cols, axis=1))
        return tuple(groups)
    return _rows("attn_prep", fn, [(qkv, "row"), (gains, "full"), (ct, "row"), (st, "row"), (consts, "full"), (bd, "full")],
                 [(3 * A_WIDTH, BF16)] * 3, tr=256)


def _band_mask(n):
    row = lax.broadcasted_iota(jnp.int32, (BAND, 2 * BAND), 0)
    col = lax.broadcasted_iota(jnp.int32, (BAND, 2 * BAND), 1)
    dist = row + BAND - col
    return (dist >= 0) & (dist <= BAND) & ((col >= BAND) | (n > 0))


def _attn_fwd(qkvn, grp):
    S = qkvn.shape[0]
    d = SWA_GROUPS[grp][1]
    L = S // d
    nblk = L // BAND
    assert L % BAND == 0
    view = qkvn.reshape(L, d * 3 * A_WIDTH)

    def body(q_ref, kc_ref, kp_ref, vc_ref, vp_ref, o_ref, lse_ref):
        n = pl.program_id(1)
        valid = _band_mask(n)
        first = lax.broadcasted_iota(jnp.int32, (BAND, LANE), 1) < A_HEAD_DIM
        pairs = [slice(pr * LANE, (pr + 1) * LANE) for pr in range(A_WIDTH // LANE)]
        halves = (first, jnp.logical_not(first))
        qps = [q_ref[:, sl] for sl in pairs]
        kcats = [jnp.concatenate([kp_ref[:, sl], kc_ref[:, sl]], axis=0) for sl in pairs]
        vcats = [jnp.concatenate([vp_ref[:, sl], vc_ref[:, sl]], axis=0) for sl in pairs]
        heads = [(pr, m) for pr in range(len(pairs)) for m in halves]
        ss = [_dot(jnp.where(m, qps[pr], jnp.zeros_like(qps[pr])), kcats[pr], 1, 1) for pr, m in heads]
        ps, lses = [], []
        for s in ss:
            s = jnp.where(valid, s * (A_HEAD_DIM ** -0.5), -1e30)
            mx = jnp.max(s, axis=-1, keepdims=True)
            e = jnp.exp(s - mx)
            l = jnp.sum(e, axis=-1, keepdims=True)
            ps.append((e / l).astype(BF16))
            lses.append(mx + jnp.log(l))
        os_ = [_dot(p, vcats[pr]) for p, (pr, _) in zip(ps, heads)]
        o_ref[...] = jnp.concatenate([jnp.where(first, os_[2 * pr], os_[2 * pr + 1]) for pr in range(len(pairs))], axis=1)
        lse_ref[...] = jnp.concatenate([jnp.where(first, lses[2 * pr], lses[2 * pr + 1]) for pr in range(len(pairs))], axis=1)

    blk = (BAND, A_WIDTH)
    o, lse = pl.pallas_call(
        body, name=f"attn_fwd_g{grp}", grid=(d, nblk),
        in_specs=[pl.BlockSpec(blk, lambda r, n: (n, r * 3)),
                  pl.BlockSpec(blk, lambda r, n: (n, r * 3 + 1)),
                  pl.BlockSpec(blk, lambda r, n: (jnp.maximum(n - 1, 0), r * 3 + 1)),
                  pl.BlockSpec(blk, lambda r, n: (n, r * 3 + 2)),
                  pl.BlockSpec(blk, lambda r, n: (jnp.maximum(n - 1, 0), r * 3 + 2))],
        out_specs=[pl.BlockSpec(blk, lambda r, n: (n, r)), pl.BlockSpec(blk, lambda r, n: (n, r))],
        out_shape=[jax.ShapeDtypeStruct((L, d * A_WIDTH), F32)] * 2,
        compiler_params=_cparams(("parallel", "parallel")),
    )(view, view, view, view, view)
    return o.reshape(S, A_WIDTH), lse.reshape(S, A_WIDTH)


def _merge_weights(l0, l1, l2):
    mx = jnp.maximum(jnp.maximum(l0, l1), l2)
    e0, e1, e2 = jnp.exp(l0 - mx), jnp.exp(l1 - mx), jnp.exp(l2 - mx)
    inv = 1.0 / (e0 + e1 + e2)
    return e0 * inv, e1 * inv, e2 * inv


def _attn_merge(os_, lses):
    def fn(i, n, o0, o1, o2, l0, l1, l2):
        w0, w1, w2 = _merge_weights(l0, l1, l2)
        return (w0 * o0 + w1 * o1 + w2 * o2,)
    ins = [(a, "row") for a in (*os_, *lses)]
    return _rows("attn_merge", fn, ins, [(A_WIDTH, BF16)], tr=512)


def _attn_merge_bwd(do, os_, lses, bd1):
    def fn(i, n, dot_, o0, o1, o2, l0, l1, l2, b):
        w0, w1, w2 = _merge_weights(l0, l1, l2)
        o = w0 * o0 + w1 * o1 + w2 * o2
        dsum = _seg_sum(dot_ * o, b)
        return (w0 * dot_, w1 * dot_, w2 * dot_, -w0 * dsum, -w1 * dsum, -w2 * dsum)
    ins = [(do, "row")] + [(a, "row") for a in (*os_, *lses)] + [(bd1, "full")]
    res = _rows("attn_merge_bwd", fn, ins, [(A_WIDTH, BF16)] * 3 + [(A_WIDTH, F32)] * 3, tr=256)
    return res[:3], res[3:]


def _lane_pick(x, lane_idx, lane):
    return jnp.sum(jnp.where(lane_idx == lane, x, 0.0), axis=-1, keepdims=True)


def _attn_bwd(qkvn, grp, do_g, lse, c_g):
    S = qkvn.shape[0]
    d = SWA_GROUPS[grp][1]
    L = S // d
    nblk = L // BAND
    view = qkvn.reshape(L, d * 3 * A_WIDTH)
    dov, lsev, cv = (t.reshape(L, d * A_WIDTH) for t in (do_g, lse, c_g))

    def body(q_ref, kc_ref, kp_ref, vc_ref, vp_ref, do_ref, lse_ref, c_ref, dq_ref, dk_ref, dv_ref, ck, cv_):
        n = pl.program_id(1)

        @pl.when(n == 0)
        def _():
            ck[...] = jnp.zeros_like(ck)
            cv_[...] = jnp.zeros_like(cv_)

        @pl.when(n < nblk)
        def _():
            valid = _band_mask(n)
            lane = lax.broadcasted_iota(jnp.int32, (BAND, LANE), 1)
            first = lane < A_HEAD_DIM
            lane2 = lax.broadcasted_iota(jnp.int32, (2 * BAND, LANE), 1) < A_HEAD_DIM
            pairs = [slice(pr * LANE, (pr + 1) * LANE) for pr in range(A_WIDTH // LANE)]
            halves = (first, jnp.logical_not(first))
            qps = [q_ref[:, sl] for sl in pairs]
            dops = [do_ref[:, sl] for sl in pairs]
            kcats = [jnp.concatenate([kp_ref[:, sl], kc_ref[:, sl]], axis=0) for sl in pairs]
            vcats = [jnp.concatenate([vp_ref[:, sl], vc_ref[:, sl]], axis=0) for sl in pairs]
            heads = [(pr, hh) for pr in range(len(pairs)) for hh in range(2)]
            zero = jnp.zeros_like(qps[0])
            ss = [_dot(jnp.where(halves[hh], qps[pr], zero), kcats[pr], 1, 1) for pr, hh in heads]
            dps = [_dot(jnp.where(halves[hh], dops[pr], zero), vcats[pr], 1, 1) for pr, hh in heads]
            dss, pbs = [], []
            for (pr, hh), s, dp in zip(heads, ss, dps):
                lse_h = _lane_pick(lse_ref[:, pairs[pr]], lane, hh * A_HEAD_DIM)
                c_h = _lane_pick(c_ref[:, pairs[pr]], lane, hh * A_HEAD_DIM)
                p = jnp.where(valid, jnp.exp(s * (A_HEAD_DIM ** -0.5) - lse_h), 0.0)
                dss.append((p * (dp + c_h) * (A_HEAD_DIM ** -0.5)).astype(BF16))
                pbs.append(p.astype(BF16))
            dqs = [_dot(ds, kcats[pr]) for ds, (pr, _) in zip(dss, heads)]
            dks = [_dot(ds, qps[pr], 0, 0) for ds, (pr, _) in zip(dss, heads)]
            dvs = [_dot(pb, dops[pr], 0, 0) for pb, (pr, _) in zip(pbs, heads)]
            for pr, sl in enumerate(pairs):
                dq_ref[:, sl] = jnp.where(first, dqs[2 * pr], dqs[2 * pr + 1])
                dkc = jnp.where(lane2, dks[2 * pr], dks[2 * pr + 1])
                dvc = jnp.where(lane2, dvs[2 * pr], dvs[2 * pr + 1])
                dk_ref[:, sl] = ck[:, sl] + dkc[:BAND]
                dv_ref[:, sl] = cv_[:, sl] + dvc[:BAND]
                ck[:, sl] = dkc[BAND:]
                cv_[:, sl] = dvc[BAND:]

        @pl.when(n == nblk)
        def _():
            dk_ref[...] = ck[...]
            dv_ref[...] = cv_[...]

    blk = (BAND, A_WIDTH)
    last = nblk - 1
    qn = lambda n: jnp.minimum(n, last)
    pn = lambda n: jnp.clip(n - 1, 0, last)
    dq, dk, dv = pl.pallas_call(
        body, name=f"attn_bwd_g{grp}", grid=(d, nblk + 1),
        in_specs=[pl.BlockSpec(blk, lambda r, n: (qn(n), r * 3)),
                  pl.BlockSpec(blk, lambda r, n: (qn(n), r * 3 + 1)),
                  pl.BlockSpec(blk, lambda r, n: (pn(n), r * 3 + 1)),
                  pl.BlockSpec(blk, lambda r, n: (qn(n), r * 3 + 2)),
                  pl.BlockSpec(blk, lambda r, n: (pn(n), r * 3 + 2)),
                  pl.BlockSpec(blk, lambda r, n: (qn(n), r)),
                  pl.BlockSpec(blk, lambda r, n: (qn(n), r)),
                  pl.BlockSpec(blk, lambda r, n: (qn(n), r))],
        out_specs=[pl.BlockSpec(blk, lambda r, n: (qn(n), r)),
                   pl.BlockSpec(blk, lambda r, n: (pn(n), r)),
                   pl.BlockSpec(blk, lambda r, n: (pn(n), r))],
        out_shape=[jax.ShapeDtypeStruct((L, d * A_WIDTH), F32)] * 3,
        scratch_shapes=[pltpu.VMEM(blk, F32), pltpu.VMEM(blk, F32)],
        compiler_params=_cparams(("parallel", "arbitrary")),
    )(view, view, view, view, view, dov, lsev, cv)
    return tuple(t.reshape(S, A_WIDTH) for t in (dq, dk, dv))


def _attn_prep_bwd(qkv, grads, gains, ct, st, consts, bd):
    def fn(i, n, t, g, c_t, s_t, c, b, *gr):
        low = c[2:3, :] > 0.5
        in16 = c[3:4, :] > 0.5
        c_t, s_t = _rope_wide(c_t), _rope_wide(s_t)
        cols, dgs = [], []
        for grp in range(3):
            for which in range(3):
                dout = gr[grp * 3 + which]
                if which == 2:
                    cols.append(dout.astype(BF16))
                    continue
                off = (grp * 3 + which) * A_WIDTH
                x = t[:, off:off + A_WIDTH].astype(F32)
                gain = g[grp * 2 + which:grp * 2 + which + 1, :]
                r = lax.rsqrt(_seg_sum(x * x, b) + EPS)
                xh = x * r
                dy = _rope_apply_bwd(dout, c_t, s_t, low, in16)
                dyn = dy * gain
                dx = r * (dyn - xh * _seg_sum(dyn * xh, b))
                cols.append(dx.astype(BF16))
                dgs.append(_colsum(dy * xh))
        return (jnp.concatenate(cols, axis=1), *dgs)
    ins = [(qkv, "row"), (gains, "full"), (ct, "row"), (st, "row"), (consts, "full"), (bd, "full")] + [(a, "row") for a in grads]
    res = _rows("attn_prep_bwd", fn, ins, [(A_QKV, BF16)], tr=128, accs=[((1, A_WIDTH), F32)] * 6)
    return res[0], res[1:]


DN_QKV = 3 * DN_WIDTH
DN_QKVZ = DN_QKV + DN_WIDTH


def _sigmoid(x):
    return 1.0 / (1.0 + jnp.exp(-x))


def _softplus(x):
    return jnp.maximum(x, 0.0) + jnp.log(1.0 + jnp.exp(-jnp.abs(x)))


def _conv_taps(xs, w, tr):
    acc = None
    for j in range(CONV_W):
        sh = CONV_W - 1 - j
        term = (pltpu.roll(xs, sh, 0) if sh else xs)[SUBLANE:] * w[j:j + 1, :]
        acc = term if acc is None else acc + term
    return acc


def _dn_prep(qkvz, ab, convw, alog_row, dt_row):
    tr = 256

    def fn(i, n, x, xp, abt, w, al, dt):
        xp = jnp.where(i > 0, xp, 0.0)
        u = _conv_taps(jnp.concatenate([xp, x], axis=0), w, tr)
        y = u * _sigmoid(u)
        qs, ks = [], []
        for h in range(DN_HEADS):
            for dst, base, sc in ((qs, 0, DN_DIM ** -0.5), (ks, DN_WIDTH, 1.0)):
                seg = y[:, base + h * DN_DIM:base + (h + 1) * DN_DIM]
                dst.append(seg * (lax.rsqrt(jnp.sum(seg * seg, axis=-1, keepdims=True) + EPS) * sc))
        lane = lax.broadcasted_iota(jnp.int32, abt.shape, 1)
        g = -jnp.exp(al) * _softplus(abt + dt)
        gb = jnp.where(lane < DN_HEADS, g, jnp.where(lane < 2 * DN_HEADS, _sigmoid(abt), 0.0))
        return u, jnp.concatenate(qs, axis=1), jnp.concatenate(ks, axis=1), y[:, 2 * DN_WIDTH:], gb

    ins = [(qkvz, "row", (0, DN_QKV)), (qkvz, "prev8", (0, DN_QKV)), (ab, "row"), (convw, "full"),
           (alog_row, "full"), (dt_row, "full")]
    return _rows("dn_prep", fn, ins, [(DN_QKV, F32), (DN_WIDTH, F32), (DN_WIDTH, F32), (DN_WIDTH, F32), (LANE, F32)], tr=tr)


def _tri_masks():
    row = lax.broadcasted_iota(jnp.int32, (CHUNK, CHUNK), 0)
    col = lax.broadcasted_iota(jnp.int32, (CHUNK, CHUNK), 1)
    return row >= col, row > col, row == col


def _heads(fn, *lists):
    return [fn(*xs) for xs in zip(*lists)]


def _split(x):
    hi = x.astype(BF16)
    return hi, (x - hi.astype(F32)).astype(BF16)


def _dot3(a, b, ca=1, cb=0):
    (ah, al), (bh, bl) = a, b
    return _dot(ah, bh, ca, cb) + (_dot(ah, bl, ca, cb) + _dot(al, bh, ca, cb))


SPLIT_STEPS = 3


def _unit_lower_inverse(a_list, eye):
    ts = [eye - a for a in a_list]
    parts = [_split(a) for a in a_list]
    for step in range(5):
        if step < SPLIT_STEPS:
            parts = [_split(_dot3(p, p)) for p in parts]
            ts = [t + _dot3(_split(t), p) for t, p in zip(ts, parts)]
        else:
            parts = [(_dot(p[0], p[0]).astype(BF16), None) for p in parts]
            ts = [t + _dot(t.astype(BF16), p[0]) for t, p in zip(ts, parts)]
    return ts


def _dn_terms(qs, ks, vs, gb):
    lower, strict, diag = _tri_masks()
    lane = lax.broadcasted_iota(jnp.int32, (CHUNK, LANE), 1)
    is_last = lax.broadcasted_iota(jnp.int32, (CHUNK, 1), 0) == CHUNK - 1
    hs = range(DN_HEADS)
    gc = _dot(lower.astype(F32), gb, precision=HIGHEST)
    gct = jnp.transpose(gc)
    bcol = [_lane_pick(gb, lane, DN_HEADS + h) for h in hs]
    gcol = [_lane_pick(gc, lane, h) for h in hs]
    glast = [jnp.sum(jnp.where(is_last, g, 0.0), axis=0, keepdims=True) for g in gcol]
    decay = [jnp.exp(jnp.where(lower, gcol[h] - gct[h:h + 1, :], -1e30)) for h in hs]
    kb = _heads(lambda k, b: k * b, ks, bcol)
    kk = _heads(lambda x, k: _bdot(x, k, 1, 1), kb, ks)
    qk = _heads(lambda q, k: _bdot(q, k, 1, 1), qs, ks)
    a = _heads(lambda x, d: jnp.where(strict, x * d, 0.0), kk, decay)
    t = [_split(x) for x in _unit_lower_inverse(a, diag.astype(F32))]
    eg = [jnp.exp(g) for g in gcol]
    egl = _heads(lambda gl, g: jnp.exp(gl - g), glast, gcol)
    rhs_w = _heads(lambda x, e: x * e, kb, eg)
    u = _heads(lambda tt, v, b: _dot3(tt, _split(v * b)), t, vs, bcol)
    w = _heads(lambda tt, r: _dot3(tt, _split(r)), t, rhs_w)
    return dict(bcol=bcol, decay=decay, kb=kb, a=a, t=t, eg=eg, egl=egl, rhs_w=rhs_w, u=u, w=w,
                attn=_heads(lambda x, d: x * d, qk, decay), q_dec=_heads(lambda q, e: q * e, qs, eg),
                k_dec=_heads(lambda k, e: k * e, ks, egl), c_dec=[jnp.exp(g) for g in glast],
                lower=lower, strict=strict, lane=lane, is_last=is_last)


def _head_slices(ref):
    return [ref[:, h * DN_DIM:(h + 1) * DN_DIM] for h in range(DN_HEADS)]


def _dn_chunk_fwd(q, k, v, gb):
    S = q.shape[0]
    N = S // CHUNK

    def body(q_ref, k_ref, v_ref, gb_ref, o_ref, st_ref, state):
        @pl.when(pl.program_id(0) == 0)
        def _():
            state[...] = jnp.zeros_like(state)

        f = _dn_terms(_head_slices(q_ref), _head_slices(k_ref), _head_slices(v_ref), gb_ref[...])
        s = [state[h] for h in range(DN_HEADS)]
        for h in range(DN_HEADS):
            st_ref[0, h] = s[h]
        sb = [x.astype(BF16) for x in s]
        v_new = _heads(lambda u, w, x: u - _bdot(w, x), f["u"], f["w"], sb)
        o = _heads(lambda qd, x, at, vn: _bdot(qd, x) + _bdot(at, vn), f["q_dec"], sb, f["attn"], v_new)
        new_s = _heads(lambda x, c, kd, vn: x * c + _bdot(kd, vn, 0, 0), s, f["c_dec"], f["k_dec"], v_new)
        for h in range(DN_HEADS):
            o_ref[:, h * DN_DIM:(h + 1) * DN_DIM] = o[h]
            state[h] = new_s[h]

    blk = pl.BlockSpec((CHUNK, DN_WIDTH), lambda n: (n, 0))
    st_blk = pl.BlockSpec((1, DN_HEADS, DN_DIM, DN_DIM), lambda n: (n, 0, 0, 0))
    return pl.pallas_call(
        body, name="dn_chunk_fwd", grid=(N,),
        in_specs=[blk, blk, blk, pl.BlockSpec((CHUNK, LANE), lambda n: (n, 0))],
        out_specs=[blk, st_blk],
        out_shape=[jax.ShapeDtypeStruct((S, DN_WIDTH), F32), jax.ShapeDtypeStruct((N, DN_HEADS, DN_DIM, DN_DIM), F32)],
        scratch_shapes=[pltpu.VMEM((DN_HEADS, DN_DIM, DN_DIM), F32)],
        compiler_params=_cparams(("arbitrary",)),
    )(q, k, v, gb)


def _dn_chunk_bwd(q, k, v, gb, states, do):
    S = q.shape[0]
    N = S // CHUNK

    def body(q_ref, k_ref, v_ref, gb_ref, st_ref, do_ref, dq_ref, dk_ref, dv_ref, dgb_ref, dstate):
        @pl.when(pl.program_id(0) == 0)
        def _():
            dstate[...] = jnp.zeros_like(dstate)

        hs = range(DN_HEADS)
        qs, ks, vs, dos = (_head_slices(r) for r in (q_ref, k_ref, v_ref, do_ref))
        f = _dn_terms(qs, ks, vs, gb_ref[...])
        lane, is_last = f["lane"], f["is_last"]
        rowsum = lambda x: jnp.sum(x, axis=-1, keepdims=True)
        s = [st_ref[0, h] for h in hs]
        dsn = [dstate[h] for h in hs]
        sb = [x.astype(BF16) for x in s]
        dsb = [x.astype(BF16) for x in dsn]
        dob = [x.astype(BF16) for x in dos]
        v_new = _heads(lambda u, w, x: u - _bdot(w, x), f["u"], f["w"], sb)
        dv_new = _heads(lambda at, d, kd, x: _bdot(at, d, 0, 0) + _bdot(kd, x), f["attn"], dob, f["k_dec"], dsb)
        dattn = _heads(lambda d, vn: _bdot(d, vn, 1, 1), dob, v_new)
        dq_dec = _heads(lambda d, x: _bdot(d, x, 1, 1), dob, sb)
        dk_dec = _heads(lambda vn, x: _bdot(vn, x, 1, 1), v_new, dsb)
        dw = _heads(lambda dv_, x: -_bdot(dv_, x, 1, 1), dv_new, sb)
        new_ds = _heads(lambda x, c, qd, d, w, dv_: x * c + _bdot(qd, d, 0, 0) - _bdot(w, dv_, 0, 0),
                        dsn, f["c_dec"], f["q_dec"], dob, f["w"], dv_new)
        for h in hs:
            dstate[h] = new_ds[h]
        drhs_u = _heads(lambda tt, x: _dot3(tt, _split(x), 0, 0), f["t"], dv_new)
        drhs_w = _heads(lambda tt, x: _dot3(tt, _split(x), 0, 0), f["t"], dw)
        da = _heads(lambda du_, u, dw_, w: jnp.where(f["strict"], -(_bdot(du_, u, 1, 1) + _bdot(dw_, w, 1, 1)), 0.0),
                    drhs_u, f["u"], drhs_w, f["w"])
        dkk = _heads(lambda x, d: x * d, da, f["decay"])
        dqk = _heads(lambda x, d: x * d, dattn, f["decay"])
        dkb = _heads(lambda x, k_, dw_, e: _bdot(x, k_) + dw_ * e, dkk, ks, drhs_w, f["eg"])
        dq = _heads(lambda x, k_, dqd, e: _bdot(x, k_) + dqd * e, dqk, ks, dq_dec, f["eg"])
        dk = _heads(lambda x, kb_, y, q_, dkd, el, dkb_, b: _bdot(x, kb_, 0, 0) + _bdot(y, q_, 0, 0) + dkd * el + dkb_ * b,
                    dkk, f["kb"], dqk, qs, dk_dec, f["egl"], dkb, f["bcol"])
        m = _heads(lambda x, a_, y, at: x * a_ + y * at, da, f["a"], dattn, f["attn"])
        ones = jnp.ones((CHUNK, LANE), BF16)
        col_m = [(_dot(mh, ones, 0, 0) + _dot(ml, ones, 0, 0))[:, 0:1] for mh, ml in map(_split, m)]
        dgc_all = jnp.zeros((CHUNK, LANE), F32)
        dbeta_all = jnp.zeros((CHUNK, LANE), F32)
        for h in hs:
            dq_ref[:, h * DN_DIM:(h + 1) * DN_DIM] = dq[h]
            dk_ref[:, h * DN_DIM:(h + 1) * DN_DIM] = dk[h]
            dv_ref[:, h * DN_DIM:(h + 1) * DN_DIM] = drhs_u[h] * f["bcol"][h]
            kdec_term = rowsum(dk_dec[h] * f["k_dec"][h])
            dc_dec = _sum_all(dsn[h] * s[h])
            dgc = (rowsum(m[h]) - col_m[h] + rowsum(dq_dec[h] * f["q_dec"][h]) - kdec_term
                   + rowsum(drhs_w[h] * f["rhs_w"][h]))
            last_extra = jnp.sum(kdec_term, axis=0, keepdims=True) + dc_dec * f["c_dec"][h]
            dgc = dgc + jnp.where(is_last, last_extra, 0.0)
            dbeta = rowsum(drhs_u[h] * vs[h]) + rowsum(dkb[h] * ks[h])
            dgc_all = jnp.where(lane == h, dgc, dgc_all)
            dbeta_all = jnp.where(lane == DN_HEADS + h, dbeta, dbeta_all)
        dg_all = _dot(f["lower"].astype(F32), dgc_all, 0, 0, precision=HIGHEST)
        dgb_ref[...] = jnp.where(lane < DN_HEADS, dg_all, dbeta_all)

    rev = lambda n: (N - 1 - n, 0)
    blk = pl.BlockSpec((CHUNK, DN_WIDTH), rev)
    gblk = pl.BlockSpec((CHUNK, LANE), rev)
    st_blk = pl.BlockSpec((1, DN_HEADS, DN_DIM, DN_DIM), lambda n: (N - 1 - n, 0, 0, 0))
    return pl.pallas_call(
        body, name="dn_chunk_bwd", grid=(N,),
        in_specs=[blk, blk, blk, gblk, st_blk, blk],
        out_specs=[blk, blk, blk, gblk],
        out_shape=[jax.ShapeDtypeStruct((S, DN_WIDTH), F32)] * 3 + [jax.ShapeDtypeStruct((S, LANE), F32)],
        scratch_shapes=[pltpu.VMEM((DN_HEADS, DN_DIM, DN_DIM), F32)],
        compiler_params=_cparams(("arbitrary",)),
    )(q, k, v, gb, states, do)


def _dn_post(o, qkvz, gain_row):
    def fn(i, n, ot, z, g):
        cols = []
        for h in range(DN_HEADS):
            seg = ot[:, h * DN_DIM:(h + 1) * DN_DIM]
            cols.append(seg * lax.rsqrt(jnp.mean(seg * seg, axis=-1, keepdims=True) + EPS) * g)
        return (jnp.concatenate(cols, axis=1) * (z * _sigmoid(z)),)
    return _rows("dn_post", fn, [(o, "row"), (qkvz, "row", (3, DN_WIDTH)), (gain_row, "full")], [(DN_WIDTH, BF16)], tr=512)


def _dn_post_bwd(don, o, qkvz, gain_row):
    def fn(i, n, dy, ot, z, g):
        sg = _sigmoid(z)
        sz = z * sg
        dos, ohs = [], []
        dg = jnp.zeros((1, DN_DIM), F32)
        for h in range(DN_HEADS):
            sl = slice(h * DN_DIM, (h + 1) * DN_DIM)
            seg = ot[:, sl]
            r = lax.rsqrt(jnp.mean(seg * seg, axis=-1, keepdims=True) + EPS)
            oh = seg * r
            dno = dy[:, sl] * sz[:, sl]
            dg = dg + _colsum(dno * oh)
            dn = dno * g
            dos.append(r * (dn - oh * jnp.mean(dn * oh, axis=-1, keepdims=True)))
            ohs.append(oh * g)
        dz = dy * jnp.concatenate(ohs, axis=1) * (sg * (1.0 + z * (1.0 - sg)))
        return jnp.concatenate(dos, axis=1), dz, dg
    ins = [(don, "row"), (o, "row"), (qkvz, "row", (3, DN_WIDTH)), (gain_row, "full")]
    return _rows("dn_post_bwd", fn, ins, [(DN_WIDTH, F32), (DN_WIDTH, F32)], tr=256, accs=[((1, DN_DIM), F32)])


def _dn_prep_bwd(dq, dk, dv, dgb, u, ab, alog_row, dt_row):
    def fn(i, n, dqt, dkt, dvt, dgbt, ut, abt, al, dt):
        sg = _sigmoid(ut)
        y = ut * sg
        dys = []
        for grad, base, sc in ((dqt, 0, DN_DIM ** -0.5), (dkt, DN_WIDTH, 1.0)):
            for h in range(DN_HEADS):
                seg = y[:, base + h * DN_DIM:base + (h + 1) * DN_DIM]
                gr = grad[:, h * DN_DIM:(h + 1) * DN_DIM]
                r = lax.rsqrt(jnp.sum(seg * seg, axis=-1, keepdims=True) + EPS)
                xh = seg * r
                dys.append((r * sc) * (gr - xh * jnp.sum(gr * xh, axis=-1, keepdims=True)))
        dy = jnp.concatenate(dys + [dvt], axis=1)
        du = dy * (sg * (1.0 + ut * (1.0 - sg)))
        lane = lax.broadcasted_iota(jnp.int32, abt.shape, 1)
        is_g = lane < DN_HEADS
        ea = jnp.exp(al)
        x = abt + dt
        slope = -ea * _sigmoid(x)
        gval = -ea * _softplus(x)
        dg = jnp.where(is_g, dgbt, 0.0)
        beta = _sigmoid(abt)
        dab = jnp.where(is_g, dg * slope, jnp.where(lane < 2 * DN_HEADS, dgbt * beta * (1.0 - beta), 0.0))
        return du, dab, _colsum(dg * gval), _colsum(dg * slope)
    ins = [(dq, "row"), (dk, "row"), (dv, "row"), (dgb, "row"), (u, "row"), (ab, "row"), (alog_row, "full"), (dt_row, "full")]
    return _rows("dn_prep_bwd", fn, ins, [(DN_QKV, F32), (LANE, BF16)], tr=256, accs=[((1, LANE), F32)] * 2)


def _dn_conv_bwd(du, dz, qkvz, convw):
    tr = 256

    def fn(i, n, dut, dun, dzt, x, xp, w):
        dun = jnp.where(i < n - 1, dun, 0.0)
        dus = jnp.concatenate([dut, dun], axis=0)
        xs = jnp.concatenate([jnp.where(i > 0, xp, 0.0), x], axis=0)
        dx = None
        dws = []
        for j in range(CONV_W):
            sh = CONV_W - 1 - j
            term = (pltpu.roll(dus, tr + SUBLANE - sh, 0) if sh else dus)[:tr] * w[j:j + 1, :]
            dx = term if dx is None else dx + term
            dws.append(_colsum(dut * (pltpu.roll(xs, sh, 0) if sh else xs)[SUBLANE:]))
        return (jnp.concatenate([dx.astype(BF16), dzt.astype(BF16)], axis=1), *dws)

    ins = [(du, "row"), (du, "next8"), (dz, "row"), (qkvz, "row", (0, DN_QKV)), (qkvz, "prev8", (0, DN_QKV)), (convw, "full")]
    res = _rows("dn_conv_bwd", fn, ins, [(DN_QKVZ, BF16)], tr=tr, accs=[((1, DN_QKV), F32)] * CONV_W)
    return res[0], res[1:]


def _add(acc, r):
    return (r + acc,)


def _mlp_ple_fwd(i, x1, hm, p_i, ple_gain, next_gain, w_up, w_down, w_ple, w_gate):
    u, a = _mm(f"mlp_up{i}", hm, w_up, epilogue=lambda acc: (acc, jnp.square(jnp.maximum(acc, 0.0))),
               out_dtypes=(BF16, BF16))
    x2, hp = _mm(f"mlp_down{i}", a, w_down, epilogue=_res_norm, extras=(x1, ple_gain), out_dtypes=(F32, BF16),
                 tm_pref=FUSED_ROWS)
    pp = _mm(f"ple_proj{i}", p_i, w_ple)

    def gate_epilogue(acc, x2t, ppt, *g):
        gate = _sigmoid(acc)
        x3 = x2t + ppt * gate
        if not g:
            return x3, gate
        return x3, gate, x3 * lax.rsqrt(jnp.mean(x3 * x3, axis=-1, keepdims=True) + EPS) * g[0]

    more = () if next_gain is None else (next_gain,)
    x3, gate, *h_next = _mm(f"ple_gate{i}", hp, w_gate, epilogue=gate_epilogue, extras=(x2, pp) + more,
                            out_dtypes=(F32, F32) + (BF16,) * len(more), tm_pref=FUSED_ROWS)
    return x3, (h_next[0] if more else None), dict(x1=x1, hm=hm, u=u, a=a, x2=x2, hp=hp, pp=pp, gate=gate, p=p_i)


def _mlp_ple_bwd(i, dx3, sv, mlp_gain, ple_gain, w_up, w_down, w_gate):
    def fn(_i, _n, d, g, pp):
        return d * g, d * pp * g * (1.0 - g)
    dpp, dzg = _rows(f"ple_gate_bwd{i}", fn, [(dx3, "row"), (sv["gate"], "row"), (sv["pp"], "row")],
                     [(D_MODEL, BF16), (D_MODEL, BF16)], tr=512)
    d_w_ple = _mm(f"ple_proj_dw{i}", sv["p"], dpp, ta=True, out_dtypes=(BF16,))
    d_w_gate = _mm(f"ple_gate_dw{i}", sv["hp"], dzg, ta=True, out_dtypes=(BF16,))
    dx2, dx2b, d_ple_gain = _mm(f"ple_gate_dx{i}", dzg, w_gate, tb=True, epilogue=_norm_bwd_2,
                                extras=(sv["x2"], ple_gain, dx3), out_dtypes=(F32, BF16), n_colsums=1, tm_pref=FUSED_ROWS)
    d_w_down = _mm(f"mlp_down_dw{i}", sv["a"], dx2b, ta=True, out_dtypes=(BF16,))
    du = _mm(f"mlp_down_dx{i}", dx2b, w_down, tb=True,
             epilogue=lambda acc, ut: (acc * (2.0 * jnp.maximum(ut.astype(F32), 0.0)),), extras=(sv["u"],), out_dtypes=(BF16,))
    d_w_up = _mm(f"mlp_up_dw{i}", sv["hm"], du, ta=True, out_dtypes=(BF16,))
    dx1, dx1b, d_mlp_gain = _mm(f"mlp_up_dx{i}", du, w_up, tb=True, epilogue=_norm_bwd_2,
                                extras=(sv["x1"], mlp_gain, dx2), out_dtypes=(F32, BF16), n_colsums=1, tm_pref=FUSED_ROWS)
    return dx1, dx1b, dict(w_ple=d_w_ple, w_ple_gate=d_w_gate, w_down=d_w_down, w_up=d_w_up,
                           ple_norm=d_ple_gain, mlp_norm=d_mlp_gain)


def _loss_fwd_bwd(y, target):
    D = y.shape[1]

    def fn(i, n, yt, tt):
        e = yt - tt
        return e * (1.0 / D), _colsum(e * e)
    dy, sq = _rows("loss", fn, [(y, "row"), (target, "row")], [(D, F32)], tr=512, accs=[((1, D), F32)])
    return sq, dy


def _after(small, token):
    return small + token[0:1, 0:1]


def _local_step(x, p, positions, target, W, P, rest_of_weights, send_layer1, send_mlp0, send_attn):
    consts = _head_consts()
    bd = _block_diag(1.0 / A_HEAD_DIM)
    bd1 = _block_diag(1.0)
    ct, st = _rope_tables(positions, consts)
    gains = jnp.stack([jnp.tile(v, A_HEADS) for g in range(3) for v in (P["attn_q_gain"][g], P["attn_k_gain"][g])])
    pad = LANE - DN_HEADS
    alog_row = jnp.pad(P["dn_a_log"].reshape(1, DN_HEADS), ((0, 0), (0, pad)))
    dt_row = jnp.pad(P["dn_dt_bias"].reshape(1, DN_HEADS), ((0, 0), (0, pad)))
    ogain_row = P["dn_o_gain"].reshape(1, DN_DIM)
    row = lambda name, i: P[name][i:i + 1]

    h0 = _rmsnorm_fwd("mix_norm0", x, row("mix_norm", 0))
    qkv = _mm("attn_qkv", h0, W["attn_w_qkv"], out_dtypes=(BF16,))
    qkvn = _attn_prep(qkv, gains, ct, st, consts, bd)
    os_, lses = zip(*[_attn_fwd(qkvn[g], g) for g in range(3)])
    o_attn = _attn_merge(os_, lses)
    x1, hm0 = _mm("attn_out", o_attn, W["attn_w_o"], epilogue=_res_norm, extras=(x, row("mlp_norm", 0)),
                  out_dtypes=(F32, BF16), tm_pref=FUSED_ROWS)
    W = {**W, **rest_of_weights(x1)}
    x3, h1, sv0 = _mlp_ple_fwd(0, x1, hm0, p[0], row("ple_norm", 0), row("mix_norm", 1),
                               W["w_up"][0], W["w_down"][0], W["w_ple"][0], W["w_ple_gate"][0])
    qkvz = _mm("dn_in_qkvz", h1, W["dn_w_qkvz"])
    ab = _mm("dn_in_ab", h1, W["dn_w_ab"])
    u, q, k, v, gb = _dn_prep(qkvz, ab, W["dn_conv"], alog_row, dt_row)
    o_dn, states = _dn_chunk_fwd(q, k, v, gb)
    on = _dn_post(o_dn, qkvz, ogain_row)
    x4, hm1 = _mm("dn_out", on, W["dn_w_o"], epilogue=_res_norm, extras=(x3, row("mlp_norm", 1)),
                  out_dtypes=(F32, BF16), tm_pref=FUSED_ROWS)
    x6, _, sv1 = _mlp_ple_fwd(1, x4, hm1, p[1], row("ple_norm", 1), None,
                              W["w_up"][1], W["w_down"][1], W["w_ple"][1], W["w_ple_gate"][1])
    sq, dy = _loss_fwd_bwd(x6, target)

    dx4, dx4b, g1 = _mlp_ple_bwd(1, dy, sv1, row("mlp_norm", 1), row("ple_norm", 1),
                                 W["w_up"][1], W["w_down"][1], W["w_ple_gate"][1])
    don = _mm("dn_out_dx", dx4b, W["dn_w_o"], tb=True)
    d_dn_w_o = _mm("dn_out_dw", on, dx4b, ta=True, out_dtypes=(BF16,))
    do_dn, dz, d_ogain = _dn_post_bwd(don, o_dn, qkvz, ogain_row)
    dq, dk, dv, dgb = _dn_chunk_bwd(q, k, v, gb, states, do_dn)
    du, dab, d_alog, d_dt = _dn_prep_bwd(dq, dk, dv, dgb, u, ab, alog_row, dt_row)
    dqkvz, d_conv = _dn_conv_bwd(du, dz, qkvz, W["dn_conv"])
    dh1 = _mm("dn_in_ab_dx", dab, W["dn_w_ab"], tb=True)
    dx3, d_mix1 = _mm("dn_in_qkvz_dx", dqkvz, W["dn_w_qkvz"], tb=True,
                      epilogue=lambda acc, part, xt, g, dres: _norm_bwd(acc + part, xt, g, dres),
                      extras=(dh1, x3, row("mix_norm", 1), dx4), n_colsums=1, tm_pref=FUSED_ROWS)
    d_w_qkvz = _mm("dn_in_qkvz_dw", h1, dqkvz, ta=True, out_dtypes=(BF16,))
    d_w_ab = _mm("dn_in_ab_dw", h1, dab, ta=True, out_dtypes=(BF16,))
    token = send_layer1(dict(
        dn_w_qkvz=d_w_qkvz, dn_w_ab=d_w_ab, dn_conv=jnp.concatenate(d_conv, 0), dn_w_o=d_dn_w_o,
        w_up=g1["w_up"], w_down=g1["w_down"], w_ple=g1["w_ple"], w_ple_gate=g1["w_ple_gate"]))
    dx1, dx1b, g0 = _mlp_ple_bwd(0, dx3, sv0, row("mlp_norm", 0), _after(row("ple_norm", 0), token),
                                 W["w_up"][0], W["w_down"][0], W["w_ple_gate"][0])
    token = send_mlp0(dict(w_up=g0["w_up"], w_down=g0["w_down"], w_ple=g0["w_ple"], w_ple_gate=g0["w_ple_gate"]))
    do_attn = _mm("attn_out_dx", dx1b, W["attn_w_o"], tb=True, epilogue=_add, extras=(_after(jnp.zeros((1, A_WIDTH), F32), token),))
    d_attn_w_o = _mm("attn_out_dw", o_attn, dx1b, ta=True, out_dtypes=(BF16,))
    dos, cs = _attn_merge_bwd(do_attn, os_, lses, bd1)
    grads9 = []
    for g in range(3):
        grads9 += list(_attn_bwd(qkvn[g], g, dos[g], lses[g], cs[g]))
    dqkv, dgains = _attn_prep_bwd(qkv, grads9, gains, ct, st, consts, bd)
    d_attn_w_qkv = _mm("attn_qkv_dw", h0, dqkv, ta=True, out_dtypes=(BF16,))
    token = send_attn(dict(attn_w_qkv=d_attn_w_qkv, attn_w_o=d_attn_w_o))
    dx0, d_mix0 = _mm("attn_qkv_dx", dqkv, W["attn_w_qkv"], tb=True, epilogue=_norm_bwd,
                      extras=(x, _after(row("mix_norm", 0), token), dx1), n_colsums=1, tm_pref=FUSED_ROWS)

    dg = jnp.stack([t.reshape(A_HEADS, A_HEAD_DIM).sum(0) for t in dgains])
    small = dict(
        mix_norm=jnp.concatenate([d_mix0, d_mix1], 0),
        attn_q_gain=dg[0::2][None], attn_k_gain=dg[1::2][None],
        dn_a_log=d_alog[:, :DN_HEADS], dn_dt_bias=d_dt[:, :DN_HEADS], dn_o_gain=d_ogain,
        mlp_norm=jnp.concatenate([g0["mlp_norm"], g1["mlp_norm"]], 0),
        ple_norm=jnp.concatenate([g0["ple_norm"], g1["ple_norm"]], 0),
    )
    return sq, dx0, small


MESH_IDS = pl.DeviceIdType.MESH
ANY = pl.BlockSpec(memory_space=pl.ANY)


def _place():
    return lax.axis_index("x"), lax.axis_index("y"), lax.axis_index("c")


def _sem_scratch(n_streams):
    return [pltpu.SemaphoreType.DMA((n_streams, N_DEV - 1)), pltpu.SemaphoreType.DMA((n_streams, N_DEV - 1)),
            pltpu.SemaphoreType.DMA((n_streams,))]


def _all_gather(name, arrays, streams):
    n_in, n_st = len(arrays), len(streams)
    shapes = [arrays[a].shape if li is None else arrays[a].shape[1:] for a, li in streams]

    def body(*refs):
        in_refs, out_refs, token = refs[:n_in], refs[n_in:n_in + n_st], refs[n_in + n_st]
        send_sems, recv_sems, local_sems = refs[n_in + n_st + 1:]
        token[...] = jnp.zeros_like(token)
        x, y, c = _place()
        me, sibling = (x, y, c), (x, y, 1 - c)
        chips = [(1 - x, y), (x, 1 - y), (1 - x, 1 - y)]

        def copy(s, k, block, to, own=False):
            a, li = streams[s]
            dst = out_refs[s].at[4 * block[0] + 2 * block[1] + block[2]]
            src = (in_refs[a] if li is None else in_refs[a].at[li]) if own else dst
            return pltpu.make_async_remote_copy(src_ref=src, dst_ref=dst, send_sem=send_sems.at[s, k],
                                                recv_sem=recv_sems.at[s, k], device_id=to, device_id_type=MESH_IDS)

        started = []
        for s, (a, li) in enumerate(streams):
            src = in_refs[a] if li is None else in_refs[a].at[li]
            mine = pltpu.make_async_copy(src, out_refs[s].at[4 * x + 2 * y + c], local_sems.at[s])
            mine.start()
            started.append(mine)
        sends = []
        for s in range(n_st):
            first = [copy(s, 0, me, sibling, own=True)]
            first += [copy(s, 1 + j, me, (*chip, c), own=True) for j, chip in enumerate(chips)]
            for cp in first:
                cp.start()
            sends += first
        for j, chip in enumerate(chips):
            for s in range(n_st):
                copy(s, 1 + j, (*chip, c), me).wait_recv()
                fwd = copy(s, 4 + j, (*chip, c), sibling)
                fwd.start()
                sends.append(fwd)
        for s in range(n_st):
            copy(s, 0, sibling, me).wait_recv()
            for j, chip in enumerate(chips):
                copy(s, 4 + j, (*chip, 1 - c), me).wait_recv()
        for cp in sends:
            cp.wait_send()
        for cp in started:
            cp.wait()

    res = pl.pallas_call(
        body, name=name,
        out_shape=[jax.ShapeDtypeStruct((N_DEV,) + tuple(sh), arrays[a].dtype) for sh, (a, _) in zip(shapes, streams)]
        + [jax.ShapeDtypeStruct((SUBLANE, LANE), F32)],
        in_specs=[ANY] * n_in, out_specs=[ANY] * n_st + [pl.BlockSpec(memory_space=pltpu.VMEM)],
        scratch_shapes=_sem_scratch(n_st),
    )(*arrays)
    return list(res[:n_st]), res[n_st]


HBM = pl.BlockSpec(memory_space=pltpu.HBM)
SEM = pl.BlockSpec(memory_space=pltpu.SEMAPHORE)
FLOWS = pltpu.CompilerParams(has_side_effects=pltpu.SideEffectType.DATAFLOW_SIDE_EFFECTING)


def _in_hbm(a):
    return pltpu.with_memory_space_constraint(a, pltpu.HBM)


def _hbm_like(a):
    return pltpu.HBM(a.shape, a.dtype)


def _peers(x, y, c):
    return [(1 - x if k & 4 else x, 1 - y if k & 2 else y, 1 - c if k & 1 else c) for k in range(1, N_DEV)]


def _start_copies(name, n_remote, n_own, make_copies, operands):
    n = len(operands)

    def body(*refs):
        for cp in make_copies(refs[:n], refs[n], refs[n + 1], refs[n + 2]):
            cp.start()
        refs[-1][...] = jnp.zeros_like(refs[-1])

    res = pl.pallas_call(
        body, name=name,
        out_shape=(pltpu.SemaphoreType.DMA((n_remote,)), pltpu.SemaphoreType.DMA((n_remote,)), pltpu.SemaphoreType.DMA((n_own,)),
                   *[_hbm_like(t) for t in operands], jax.ShapeDtypeStruct((SUBLANE, LANE), F32)),
        in_specs=[HBM] * n, out_specs=(SEM, SEM, SEM, *[HBM] * n, pl.BlockSpec(memory_space=pltpu.VMEM)),
        input_output_aliases={i: 3 + i for i in range(n)}, compiler_params=FLOWS,
    )(*[_in_hbm(t) for t in operands])
    return res[:3], list(res[3:3 + n]), res[-1]


def _wait_copies(name, make_waits, sems, operands, after):
    n = len(operands)

    def body(*refs):
        for wait in make_waits(refs[:n], refs[n], refs[n + 1], refs[n + 2]):
            wait()

    res = pl.pallas_call(
        body, name=name, out_shape=tuple(_hbm_like(t) for t in operands),
        in_specs=[HBM] * n + [SEM, SEM, SEM, ANY], out_specs=tuple([HBM] * n),
        input_output_aliases={i: i for i in range(n)}, compiler_params=FLOWS,
    )(*operands, *sems, after)
    return list(res)


def _gather_plan(n_in, streams):
    def block(arr, s):
        a, li = streams[s]
        return arr[a] if li is None else arr[a].at[li]

    def copies(refs, send_sems, recv_sems, own_sems, arrivals=False):
        arr, land = refs[:n_in], refs[n_in:]
        x, y, c = _place()
        me = 4 * x + 2 * y + c
        out = []
        for s in range(len(streams)):
            out.append(("own", pltpu.make_async_copy(block(arr, s), land[s].at[me], own_sems.at[s])))
            for k, (px, py, pc) in enumerate(_peers(x, y, c)):
                out.append(("remote", pltpu.make_async_remote_copy(
                    src_ref=block(arr, s), dst_ref=land[s].at[4 * px + 2 * py + pc if arrivals else me],
                    send_sem=send_sems.at[s * (N_DEV - 1) + k], recv_sem=recv_sems.at[s * (N_DEV - 1) + k],
                    device_id=(px, py, pc), device_id_type=MESH_IDS)))
        return out
    return copies


def _exchange_plan(n_st):
    def copies(refs, send_sems, recv_sems, own_sems, arrivals=False):
        snd, rcv = refs[:n_st], refs[n_st:]
        x, y, c = _place()
        me = 4 * x + 2 * y + c
        out = []
        for s in range(n_st):
            out.append(("own", pltpu.make_async_copy(snd[s].at[me], rcv[s].at[me], own_sems.at[s])))
            for k, (px, py, pc) in enumerate(_peers(x, y, c)):
                peer = 4 * px + 2 * py + pc
                out.append(("remote", pltpu.make_async_remote_copy(
                    src_ref=snd[s].at[peer], dst_ref=rcv[s].at[peer if arrivals else me],
                    send_sem=send_sems.at[s * (N_DEV - 1) + k], recv_sem=recv_sems.at[s * (N_DEV - 1) + k],
                    device_id=(px, py, pc), device_id_type=MESH_IDS)))
        return out
    return copies


def _split_transfer(tag, plan, n_streams, operands):
    sems, operands, token = _start_copies(f"{tag}_start", n_streams * (N_DEV - 1), n_streams,
                                          lambda refs, a, b, o: [cp for _, cp in plan(refs, a, b, o)], operands)

    def waits(refs, a, b, o):
        out = []
        for kind, cp in plan(refs, a, b, o, arrivals=True):
            out += [cp.wait] if kind == "own" else [cp.wait_send, cp.wait_recv]
        return out

    return (lambda after: _wait_copies(f"{tag}_wait", waits, sems, operands, after)), token


def _gather_async(tag, arrays, streams):
    lands = [lax.empty((N_DEV,) + tuple(arrays[a].shape if li is None else arrays[a].shape[1:]), arrays[a].dtype)
             for a, li in streams]
    finish, token = _split_transfer(tag, _gather_plan(len(arrays), streams), len(streams), list(arrays) + lands)
    return (lambda after: finish(after)[len(arrays):]), token


def _exchange_async(tag, sends):
    recvs = [lax.empty(t.shape, t.dtype) for t in sends]
    finish, token = _split_transfer(tag, _exchange_plan(len(sends)), len(sends), list(sends) + recvs)
    return (lambda after: finish(after)[len(sends):]), token


def _dn_in_pieces():
    n = (DN_QKVZ + 2 * DN_HEADS) // N_DEV
    segs = ((0, DN_QKV, 0, 0), (DN_QKV, DN_QKV + 2 * DN_HEADS, 1, 0), (DN_QKV + 2 * DN_HEADS, DN_QKVZ + 2 * DN_HEADS, 0, DN_QKV))
    out = []
    for d in range(N_DEV):
        lo, hi = d * n, (d + 1) * n
        for s0, s1, tgt, t0 in segs:
            a, b = max(lo, s0), min(hi, s1)
            if a < b:
                out.append((d, a - lo, b - lo, tgt, t0 + a - s0))
    return out


def _unpack_cols(name, g):
    _, K, n = g.shape
    tr = 256

    def body(g_ref, o_ref):
        for d in range(N_DEV):
            o_ref[:, d * n:(d + 1) * n] = g_ref[d]

    return pl.pallas_call(
        body, name=name, grid=(K // tr,), in_specs=[pl.BlockSpec((N_DEV, tr, n), lambda i: (0, i, 0))],
        out_specs=pl.BlockSpec((tr, N_DEV * n), lambda i: (i, 0)),
        out_shape=jax.ShapeDtypeStruct((K, N_DEV * n), g.dtype), compiler_params=_cparams(("parallel",)),
    )(g)


def _pack_cols(name, w):
    K, n = w.shape[0], w.shape[1] // N_DEV
    tr = 256

    def body(w_ref, o_ref):
        for d in range(N_DEV):
            o_ref[d] = w_ref[:, d * n:(d + 1) * n]

    return pl.pallas_call(
        body, name=name, grid=(K // tr,), in_specs=[pl.BlockSpec((tr, N_DEV * n), lambda i: (i, 0))],
        out_specs=pl.BlockSpec((N_DEV, tr, n), lambda i: (0, i, 0)),
        out_shape=jax.ShapeDtypeStruct((N_DEV, K, n), w.dtype), compiler_params=_cparams(("parallel",)),
    )(w)


def _unpack_dn_in(g):
    _, K, n = g.shape
    tr = 256

    def body(g_ref, qkvz_ref, ab_ref):
        ab_ref[...] = jnp.zeros_like(ab_ref)
        for d, c0, c1, tgt, t0 in _dn_in_pieces():
            (qkvz_ref, ab_ref)[tgt][:, t0:t0 + c1 - c0] = g_ref[d, :, c0:c1]

    return pl.pallas_call(
        body, name="unpack_dn_in", grid=(K // tr,), in_specs=[pl.BlockSpec((N_DEV, tr, n), lambda i: (0, i, 0))],
        out_specs=[pl.BlockSpec((tr, DN_QKVZ), lambda i: (i, 0)), pl.BlockSpec((tr, LANE), lambda i: (i, 0))],
        out_shape=[jax.ShapeDtypeStruct((K, DN_QKVZ), g.dtype), jax.ShapeDtypeStruct((K, LANE), g.dtype)],
        compiler_params=_cparams(("parallel",)),
    )(g)


def _pack_dn_in(d_qkvz, d_ab):
    K = d_qkvz.shape[0]
    n = (DN_QKVZ + 2 * DN_HEADS) // N_DEV
    tr = 256

    def body(qkvz_ref, ab_ref, o_ref):
        for d, c0, c1, tgt, t0 in _dn_in_pieces():
            o_ref[d, :, c0:c1] = (qkvz_ref, ab_ref)[tgt][:, t0:t0 + c1 - c0]

    return pl.pallas_call(
        body, name="pack_dn_in", grid=(K // tr,),
        in_specs=[pl.BlockSpec((tr, DN_QKVZ), lambda i: (i, 0)), pl.BlockSpec((tr, LANE), lambda i: (i, 0))],
        out_specs=pl.BlockSpec((N_DEV, tr, n), lambda i: (0, i, 0)),
        out_shape=jax.ShapeDtypeStruct((N_DEV, K, n), d_qkvz.dtype), compiler_params=_cparams(("parallel",)),
    )(d_qkvz, d_ab)


ADAMW_ROWS = 256


def _adamw(name, parts, w, m, v):
    R, C = w.shape
    tr = min(R, ADAMW_ROWS)
    assert R % tr == 0 and parts.shape == (N_DEV, R, C)
    c1 = 1.0 - B1 ** STEP
    c2 = 1.0 - B2 ** STEP

    def body(p_ref, w_ref, m_ref, v_ref, g_ref, d_ref, nm_ref, nv_ref):
        g = p_ref[0].astype(F32)
        for dev in range(1, N_DEV):
            g = g + p_ref[dev].astype(F32)
        nm = B1 * m_ref[...] + (1.0 - B1) * g
        nv = B2 * v_ref[...] + (1.0 - B2) * jnp.square(g)
        g_ref[...] = g
        nm_ref[...] = nm
        nv_ref[...] = nv
        d_ref[...] = -LR * ((nm / c1) / (jnp.sqrt(nv / c2) + ADAM_EPS) + WD * w_ref[...])

    blk = pl.BlockSpec((tr, C), lambda i: (i, 0))
    return pl.pallas_call(
        body, name=name, grid=(R // tr,),
        in_specs=[pl.BlockSpec((N_DEV, tr, C), lambda i: (0, i, 0)), blk, blk, blk],
        out_specs=[blk] * 4, out_shape=[jax.ShapeDtypeStruct((R, C), F32)] * 4,
        compiler_params=_cparams(("parallel",)),
    )(parts, w, m, v)


SMALL = ("mix_norm", "attn_q_gain", "attn_k_gain", "dn_a_log", "dn_dt_bias", "dn_o_gain", "mlp_norm", "ple_norm")
WEIGHTS = ("mix_norm", "attn_w_qkv", "attn_q_gain", "attn_k_gain", "attn_w_o", "dn_w_in", "dn_conv", "dn_a_log",
           "dn_dt_bias", "dn_o_gain", "dn_w_o", "mlp_norm", "w_up", "w_down", "ple_norm", "w_ple", "w_ple_gate")


def _to_rows(flat, multiple):
    n = flat.shape[-1]
    rows = -(-n // (LANE * multiple)) * multiple
    return jnp.pad(flat, [(0, rows * LANE - n)]).reshape(rows, LANE)


def _cols_to_devices(w):
    K, N = w.shape
    return jnp.transpose(w.reshape(K, N_DEV, N // N_DEV), (1, 0, 2))


def _cols_from_devices(g):
    _, K, n = g.shape
    return jnp.transpose(g, (1, 0, 2)).reshape(K, N_DEV * n)


SMALL_ROWS = 96


def _pack_small(vals, loss_rows):
    rows = [_to_rows(vals[n].reshape(-1), SUBLANE) for n in SMALL] + [loss_rows]
    buf = jnp.concatenate(rows, 0)
    assert buf.shape == (SMALL_ROWS, LANE)
    return buf


def _unpack_small(buf, like):
    out, r = {}, 0
    for n in SMALL:
        sz = math.prod(like[n].shape)
        out[n] = buf[r:r + -(-sz // LANE)].reshape(-1)[:sz].reshape(like[n].shape)
        r += -(-sz // (LANE * SUBLANE)) * SUBLANE
    return out


def kernel(x, p, positions, mix_norm, attn_w_qkv, attn_q_gain, attn_k_gain, attn_w_o, dn_w_in, dn_conv, dn_a_log, dn_dt_bias, dn_o_gain, dn_w_o, mlp_norm, w_up, w_down, ple_norm, w_ple, w_ple_gate, loss_target, m_mix_norm, m_attn_w_qkv, m_attn_q_gain, m_attn_k_gain, m_attn_w_o, m_dn_w_in, m_dn_conv, m_dn_a_log, m_dn_dt_bias, m_dn_o_gain, m_dn_w_o, m_mlp_norm, m_w_up, m_w_down, m_ple_norm, m_w_ple, m_w_ple_gate, v_mix_norm, v_attn_w_qkv, v_attn_q_gain, v_attn_k_gain, v_attn_w_o, v_dn_w_in, v_dn_conv, v_dn_a_log, v_dn_dt_bias, v_dn_o_gain, v_dn_w_o, v_mlp_norm, v_w_up, v_w_down, v_ple_norm, v_w_ple, v_w_ple_gate):
    w = dict(mix_norm=mix_norm, attn_w_qkv=attn_w_qkv, attn_q_gain=attn_q_gain, attn_k_gain=attn_k_gain, attn_w_o=attn_w_o,
             dn_w_in=dn_w_in, dn_conv=dn_conv, dn_a_log=dn_a_log, dn_dt_bias=dn_dt_bias, dn_o_gain=dn_o_gain, dn_w_o=dn_w_o,
             mlp_norm=mlp_norm, w_up=w_up, w_down=w_down, ple_norm=ple_norm, w_ple=w_ple, w_ple_gate=w_ple_gate)
    m = dict(mix_norm=m_mix_norm, attn_w_qkv=m_attn_w_qkv, attn_q_gain=m_attn_q_gain, attn_k_gain=m_attn_k_gain,
             attn_w_o=m_attn_w_o, dn_w_in=m_dn_w_in, dn_conv=m_dn_conv, dn_a_log=m_dn_a_log, dn_dt_bias=m_dn_dt_bias,
             dn_o_gain=m_dn_o_gain, dn_w_o=m_dn_w_o, mlp_norm=m_mlp_norm, w_up=m_w_up, w_down=m_w_down,
             ple_norm=m_ple_norm, w_ple=m_w_ple, w_ple_gate=m_w_ple_gate)
    v = dict(mix_norm=v_mix_norm, attn_w_qkv=v_attn_w_qkv, attn_q_gain=v_attn_q_gain, attn_k_gain=v_attn_k_gain,
             attn_w_o=v_attn_w_o, dn_w_in=v_dn_w_in, dn_conv=v_dn_conv, dn_a_log=v_dn_a_log, dn_dt_bias=v_dn_dt_bias,
             dn_o_gain=v_dn_o_gain, dn_w_o=v_dn_w_o, mlp_norm=v_mlp_norm, w_up=v_w_up, w_down=v_w_down,
             ple_norm=v_ple_norm, w_ple=v_w_ple, w_ple_gate=v_w_ple_gate)
    S = x.shape[1]

    bf = lambda a: a.astype(BF16)
    rows_to_devices = lambda t: t.reshape(N_DEV, t.shape[0] // N_DEV, t.shape[1])

    (g_qkv, g_ao), token = _all_gather("gather_attn", [bf(attn_w_qkv[0]), bf(attn_w_o[0])], [(0, None), (1, None)])
    rest_shards = [bf(dn_w_in[0]), bf(dn_w_o[0]), bf(w_up), bf(w_down), bf(w_ple), bf(w_ple_gate), _after(dn_conv[0], token)]
    rest_streams = [(0, None), (1, None), (2, 0), (2, 1), (3, 0), (3, 1), (4, 0), (4, 1), (5, 0), (5, 1), (6, None)]
    rest_arrived, token = _gather_async("gather_rest", rest_shards, rest_streams)
    W = dict(attn_w_qkv=_unpack_cols("unpack_attn_qkv", g_qkv), attn_w_o=_cols_from_devices(g_ao))

    def rest_of_weights(after):
        g_in, g_do, g_up0, g_up1, g_dn0, g_dn1, g_pl0, g_pl1, g_gt0, g_gt1, g_conv = rest_arrived(after)
        rest = dict(
            dn_conv=jnp.transpose(g_conv, (1, 0, 2)).reshape(CONV_W, DN_QKV), dn_w_o=g_do.reshape(DN_WIDTH, D_MODEL),
            w_up=[_cols_from_devices(g_up0), _cols_from_devices(g_up1)],
            w_down=[g_dn0.reshape(D_FF, D_MODEL), g_dn1.reshape(D_FF, D_MODEL)],
            w_ple=[_cols_from_devices(g_pl0), _cols_from_devices(g_pl1)],
            w_ple_gate=[g_gt0.reshape(D_MODEL, D_MODEL), g_gt1.reshape(D_MODEL, D_MODEL)])
        rest["dn_w_qkvz"], rest["dn_w_ab"] = _unpack_dn_in(g_in)
        return rest

    pending = {}

    def mlp_sends(g):
        return [_cols_to_devices(g["w_up"]), rows_to_devices(g["w_down"]), _cols_to_devices(g["w_ple"]),
                rows_to_devices(g["w_ple_gate"])]

    def start(tag, sends):
        pending[tag], token = _exchange_async(f"exchange_{tag}", sends)
        return token

    def send_layer1(g):
        conv_send = jnp.transpose(g["dn_conv"].reshape(CONV_W, N_DEV, DN_QKV // N_DEV), (1, 0, 2))
        return start("layer1", [_pack_dn_in(g["dn_w_qkvz"], g["dn_w_ab"]), conv_send, rows_to_devices(g["dn_w_o"])] + mlp_sends(g))

    def send_mlp0(g):
        return start("mlp0", mlp_sends(g))

    def send_attn(g):
        return start("attn", [_pack_cols("pack_attn_qkv", g["attn_w_qkv"]), _cols_to_devices(g["attn_w_o"])])

    P = dict(mix_norm=_after(mix_norm, token), attn_q_gain=attn_q_gain[0], attn_k_gain=attn_k_gain[0], dn_a_log=dn_a_log[0],
             dn_dt_bias=dn_dt_bias[0], dn_o_gain=dn_o_gain[0], mlp_norm=mlp_norm, ple_norm=ple_norm)

    sq, dx0, small_g = _local_step(x[0], p[:, 0], positions.reshape(S, 1), loss_target[0], W, P,
                                   rest_of_weights, send_layer1, send_mlp0, send_attn)

    r_in, r_conv, r_do, r_up1, r_dn1, r_pl1, r_gt1 = pending["layer1"](dx0)
    r_up0, r_dn0, r_pl0, r_gt0 = pending["mlp0"](dx0)
    r_qkv, r_ao = pending["attn"](dx0)
    big = {}
    for n, parts in (("attn_w_qkv", [r_qkv]), ("attn_w_o", [r_ao]), ("dn_w_in", [r_in]), ("dn_conv", [r_conv]),
                     ("dn_w_o", [r_do]), ("w_up", [r_up0, r_up1]), ("w_down", [r_dn0, r_dn1]),
                     ("w_ple", [r_pl0, r_pl1]), ("w_ple_gate", [r_gt0, r_gt1])):
        layers = [_adamw(f"adamw_{n}{l}", pt, w[n][l], m[n][l], v[n][l]) for l, pt in enumerate(parts)]
        big[n] = [jnp.stack([res[k] for res in layers]) for k in range(4)]

    loss_rows = jnp.pad((0.5 / D_MODEL) * jnp.sum(sq, axis=1, keepdims=True), ((0, SUBLANE - 1), (0, LANE - 1)))
    small_like = {n: w[n] for n in SMALL}
    parts_s = _all_gather("gather_small", [_pack_small(small_g, loss_rows)], [(0, None)])[0][0]
    zero_rows = jnp.zeros((SUBLANE, LANE), F32)
    small = _adamw("adamw_small", parts_s, _pack_small(w, zero_rows), _pack_small(m, zero_rows), _pack_small(v, zero_rows))
    loss = small[0][SMALL_ROWS - SUBLANE, 0]
    small = [_unpack_small(b, small_like) for b in small]

    outs = [loss, dx0[None]]
    for k in range(4):
        for n in WEIGHTS:
            outs.append(small[k][n] if n in SMALL else big[n][k])
    return tuple(outs)
```

```python
import functools
import math

import jax
import jax.numpy as jnp
from jax import lax
from jax.experimental import pallas as pl
from jax.experimental.pallas import tpu as pltpu

F32 = jnp.float32
BF16 = jnp.bfloat16
HIGHEST = lax.Precision.HIGHEST

N_DEV = 8
D_MODEL = 1024
EPS = 1e-6
SWA_GROUPS = ((128, 1), (512, 4), (2048, 16))
A_HEADS = 8
A_HEAD_DIM = 64
A_WIDTH = A_HEADS * A_HEAD_DIM
A_QKV = 3 * 3 * A_WIDTH
ROPE_DIM = 16
ROPE_HALF = 8
ROPE_THETA = 500000.0
BAND = 128
DN_HEADS = 8
DN_DIM = 128
DN_WIDTH = DN_HEADS * DN_DIM
CONV_W = 4
CHUNK = 64
D_FF = 4 * D_MODEL
PLE_DIM = 256
LR, B1, B2, ADAM_EPS, WD, STEP = 0.001, 0.9, 0.999, 1e-08, 0.01, 10

VMEM_LIMIT = 56 * 1024 * 1024
MXU_TILE = 1024
MM_SLAB = 256
LANE = 128
SUBLANE = 8


def _cparams(sem):
    return pltpu.CompilerParams(dimension_semantics=sem, vmem_limit_bytes=VMEM_LIMIT)


def _tile(n, pref):
    if n <= pref:
        return n
    t = (pref // LANE) * LANE
    while t >= LANE:
        if n % t == 0:
            return t
        t -= LANE
    raise ValueError(f"no tile for {n}")


def _dot(a, b, ca=1, cb=0, precision=None):
    return lax.dot_general(a, b, (((ca,), (cb,)), ((), ())), precision=precision,
                           preferred_element_type=F32)


def _bdot(a, b, ca=1, cb=0):
    return _dot(a.astype(BF16), b.astype(BF16), ca, cb)


def _mm(name, a, b, *, ta=False, tb=False, epilogue=None, extras=(), out_dtypes=(F32,), n_colsums=0,
        tm_pref=MXU_TILE, tn_pref=1536, tk_pref=MXU_TILE):
    M, K = (a.shape[1], a.shape[0]) if ta else a.shape
    N = b.shape[0] if tb else b.shape[1]
    assert (b.shape[1] if tb else b.shape[0]) == K
    tm, tn, tk = _tile(M, tm_pref), _tile(N, tn_pref), _tile(K, tk_pref)
    nk = K // tk
    n_out = len(out_dtypes)
    n_ext = len(extras)
    assert n_colsums == 0 or tn == N
    sub = min(tm, MM_SLAB)

    def body(*refs):
        a_ref, b_ref = refs[0], refs[1]
        ext = refs[2:2 + n_ext]
        outs = refs[2 + n_ext:2 + n_ext + n_out]
        sums = refs[2 + n_ext + n_out:2 + n_ext + n_out + n_colsums]
        row_tile, k = pl.program_id(0), pl.program_id(2)
        slabs = [slice(s * sub, (s + 1) * sub) for s in range(tm // sub)]

        def product(rows):
            return _bdot(a_ref[:, rows] if ta else a_ref[rows, :], b_ref[...], 0 if ta else 1, 1 if tb else 0)

        def finish(results):
            col_rows = []
            for rows, r in zip(slabs, results):
                res = (r,) if epilogue is None else epilogue(r, *[e[...] if e.shape[0] == 1 else e[rows, :] for e in ext])
                for o, v in zip(outs, res):
                    o[rows, :] = v.astype(o.dtype)
                col_rows.append(res[n_out:])
            for n, o in enumerate(sums):
                v = functools.reduce(lambda x, y: x + y, [c[n] for c in col_rows])

                @pl.when(row_tile == 0)
                def _(o=o, v=v):
                    o[...] = v

                @pl.when(row_tile > 0)
                def _(o=o, v=v):
                    o[...] += v

        if nk == 1:
            finish([product(rows) for rows in slabs])
            return
        acc = refs[-1]

        @pl.when(k == 0)
        def _():
            acc[...] = jnp.zeros_like(acc)

        for rows in slabs:
            acc[rows, :] += product(rows)

        @pl.when(k == nk - 1)
        def _():
            finish([acc[rows, :] for rows in slabs])

    a_spec = pl.BlockSpec((tk, tm), lambda i, j, k: (k, i)) if ta else pl.BlockSpec((tm, tk), lambda i, j, k: (i, k))
    b_spec = pl.BlockSpec((tn, tk), lambda i, j, k: (j, k)) if tb else pl.BlockSpec((tk, tn), lambda i, j, k: (k, j))
    ext_specs = []
    for e in extras:
        if e.shape[0] == 1 and M != 1:
            ext_specs.append(pl.BlockSpec((1, tn), lambda i, j, k: (0, j)))
        else:
            ext_specs.append(pl.BlockSpec((tm, tn), lambda i, j, k: (i, j)))
    out = pl.pallas_call(
        body, name=name,
        grid=(M // tm, N // tn, nk),
        in_specs=[a_spec, b_spec] + ext_specs,
        out_specs=[pl.BlockSpec((tm, tn), lambda i, j, k: (i, j)) for _ in range(n_out)]
        + [pl.BlockSpec((1, tn), lambda i, j, k: (0, 0)) for _ in range(n_colsums)],
        out_shape=[jax.ShapeDtypeStruct((M, N), dt) for dt in out_dtypes]
        + [jax.ShapeDtypeStruct((1, N), F32) for _ in range(n_colsums)],
        scratch_shapes=[pltpu.VMEM((tm, tn), F32)] if nk > 1 else [],
        compiler_params=_cparams(("arbitrary" if n_colsums else "parallel", "parallel", "arbitrary")),
    )(a, b, *extras)
    return out[0] if len(out) == 1 else tuple(out)


def _perm_matrices(tr, d):
    import numpy as np
    old = np.arange(tr)
    p = np.zeros((tr, tr), np.float32)
    p[(old % d) * (tr // d) + old // d, old] = 1.0
    return jnp.asarray(p, BF16), jnp.asarray(p.T, BF16)


def _permute(p, x):
    if x.dtype == BF16:
        return _dot(p, x)
    hi = x.astype(BF16)
    rest = x - hi.astype(F32)
    mid = rest.astype(BF16)
    lo = (rest - mid.astype(F32)).astype(BF16)
    return _dot(p, hi) + _dot(p, mid) + _dot(p, lo)


def _rows(name, fn, ins, outs, *, tr, accs=()):
    ins = [(e[0], e[1]) + (e[2] if len(e) > 2 else (0, e[0].shape[-1])) for e in ins]
    outs = [tuple(o) + (0,) * (3 - len(o)) for o in outs]
    n_rows = next(e[0].shape[0] if e[1] == "row" else e[0].shape[0] * e[0].shape[1] for e in ins if e[1] in ("row", "res"))
    assert n_rows % tr == 0 and tr % SUBLANE == 0
    steps = n_rows // tr
    t8 = tr // SUBLANE
    n8 = n_rows // SUBLANE
    dils = sorted({e[0].shape[0] for e in ins if e[1] == "res" and e[0].shape[0] > 1} | {o[2] for o in outs if o[2] > 1})
    perms = [m for d in dils for m in _perm_matrices(tr, d)]
    ins = ins + [(m, "full", 0, tr) for m in perms]
    n_in, n_out, n_acc = len(ins), len(outs), len(accs)

    def body(*refs):
        i = pl.program_id(0)
        to_res = {d: refs[n_in - len(perms) + 2 * j][...] for j, d in enumerate(dils)}
        to_tok = {d: refs[n_in - len(perms) + 2 * j + 1][...] for j, d in enumerate(dils)}
        tiles = []
        for r, e in zip(refs[:n_in - len(perms)], ins):
            d = e[0].shape[0] if e[1] == "res" else 0
            if d == 0:
                tiles.append(r[...])
            elif d == 1:
                tiles.append(r[0])
            else:
                tiles.append(_permute(to_tok[d], jnp.concatenate([r[j] for j in range(d)], axis=0)))
        vals = fn(i, steps, *tiles)
        if not isinstance(vals, (tuple, list)):
            vals = (vals,)
        assert len(vals) == n_out + n_acc
        for o, v, (_, dt, d) in zip(refs[n_in:n_in + n_out], vals[:n_out], outs):
            if d == 0:
                o[...] = v.astype(o.dtype)
            elif d == 1:
                o[0] = v.astype(o.dtype)
            else:
                y = _permute(to_res[d], v.astype(dt))
                for j in range(d):
                    o[j] = y[j * (tr // d):(j + 1) * (tr // d)].astype(o.dtype)
        if n_acc:
            acc_refs = refs[n_in + n_out:]

            @pl.when(i == 0)
            def _():
                for r in acc_refs:
                    r[...] = jnp.zeros_like(r)

            for r, v in zip(acc_refs, vals[n_out:]):
                r[...] += v.astype(r.dtype)

    in_specs = []
    for a, kind, cb, c in ins:
        if kind == "row":
            in_specs.append(pl.BlockSpec((tr, c), lambda i, cb=cb: (i, cb)))
        elif kind == "full":
            in_specs.append(pl.BlockSpec(a.shape, lambda i, z=(0,) * a.ndim: z))
        elif kind == "prev8":
            in_specs.append(pl.BlockSpec((SUBLANE, c), lambda i, cb=cb: (jnp.maximum(i * t8 - 1, 0), cb)))
        elif kind == "next8":
            in_specs.append(pl.BlockSpec((SUBLANE, c), lambda i, cb=cb: (jnp.minimum((i + 1) * t8, n8 - 1), cb)))
        elif kind == "res":
            d = a.shape[0]
            in_specs.append(pl.BlockSpec((d, tr // d, a.shape[2]), lambda i: (0, i, 0)))
        else:
            raise ValueError(kind)
    out_specs = [pl.BlockSpec((tr, c), lambda i: (i, 0)) if d == 0 else pl.BlockSpec((d, tr // d, c), lambda i: (0, i, 0))
                 for c, _, d in outs]
    out_specs += [pl.BlockSpec(s, lambda i, z=(0,) * len(s): z) for s, _ in accs]
    out_shape = [jax.ShapeDtypeStruct((n_rows, c) if d == 0 else (d, n_rows // d, c), dt) for c, dt, d in outs]
    out_shape += [jax.ShapeDtypeStruct(s, dt) for s, dt in accs]
    res = pl.pallas_call(
        body, name=name, grid=(steps,), in_specs=in_specs, out_specs=out_specs, out_shape=out_shape,
        compiler_params=_cparams(("arbitrary",) if n_acc else ("parallel",)),
    )(*[e[0] for e in ins])
    return res[0] if len(res) == 1 else tuple(res)


def _colsum(x):
    return jnp.sum(x, axis=0, keepdims=True)


def _sum_all(x):
    return jnp.sum(jnp.sum(x, axis=1, keepdims=True), axis=0, keepdims=True)


def _rmsnorm_fwd(name, x, gain):
    def fn(i, n, xt, g):
        r = lax.rsqrt(jnp.mean(xt * xt, axis=-1, keepdims=True) + EPS)
        return (xt * r * g,)
    return _rows(name, fn, [(x, "row"), (gain, "full")], [(x.shape[1], BF16)], tr=512)


FUSED_ROWS = 1024


def _res_norm(acc, res, g):
    x = res + acc
    return x, x * lax.rsqrt(jnp.mean(x * x, axis=-1, keepdims=True) + EPS) * g


def _norm_bwd(dh, x, g, dres):
    r = lax.rsqrt(jnp.mean(x * x, axis=-1, keepdims=True) + EPS)
    xh = x * r
    dxn = dh * g
    dx = dres + r * (dxn - xh * jnp.mean(dxn * xh, axis=-1, keepdims=True))
    return dx, _colsum(dh * xh)


def _norm_bwd_2(dh, x, g, dres):
    dx, dg = _norm_bwd(dh, x, g, dres)
    return dx, dx, dg


def _head_consts():
    import numpy as np
    e = np.arange(A_WIDTH) % A_HEAD_DIM
    inv = (np.float32(ROPE_THETA) ** (-np.arange(0, ROPE_DIM, 2, dtype=np.float32) / np.float32(ROPE_DIM))).astype(np.float32)
    c = np.zeros((8, A_WIDTH), np.float32)
    c[0] = np.where(e < ROPE_DIM, inv[e % ROPE_HALF], 0.0)
    c[1] = np.where(e < ROPE_HALF, -1.0, np.where(e < ROPE_DIM, 1.0, 0.0))
    c[2] = (e < ROPE_HALF).astype(np.float32)
    c[3] = (e < ROPE_DIM).astype(np.float32)
    return jnp.asarray(c)


def _block_diag(scale):
    import numpy as np
    h = np.arange(A_WIDTH) // A_HEAD_DIM
    return jnp.asarray((h[:, None] == h[None, :]).astype(np.float32) * scale, dtype=BF16)


def _seg_sum(x, bd):
    hi = x.astype(BF16)
    lo = (x - hi.astype(F32)).astype(BF16)
    return _dot(hi, bd) + _dot(lo, bd)


def _rope_tables(positions, consts):
    def fn(i, n, pos, c):
        ang = pos.astype(F32) * c[0:1, :LANE]
        return jnp.cos(ang), jnp.sin(ang) * c[1:2, :LANE]
    return _rows("rope_tables", fn, [(positions, "row"), (consts, "full")], [(LANE, F32), (LANE, F32)], tr=512)


def _rope_wide(t):
    return jnp.concatenate([t] * (A_WIDTH // LANE), axis=1)


def _rope_apply(y, ct, st, low):
    rolled = jnp.where(low, pltpu.roll(y, A_WIDTH - ROPE_HALF, 1), pltpu.roll(y, ROPE_HALF, 1))
    return y * ct + rolled * st


def _rope_apply_bwd(dout, ct, st, low, in16):
    t = dout * st
    back = jnp.where(low, pltpu.roll(t, A_WIDTH - ROPE_HALF, 1), jnp.where(in16, pltpu.roll(t, ROPE_HALF, 1), 0.0))
    return dout * ct + back


def _attn_prep(qkv, gains, ct, st, consts, bd):
    def fn(i, n, t, g, c_t, s_t, c, b):
        low = c[2:3, :] > 0.5
        c_t, s_t = _rope_wide(c_t), _rope_wide(s_t)
        groups = []
        for grp in range(3):
            cols = []
            for which in range(3):
                off = (grp * 3 + which) * A_WIDTH
                x = t[:, off:off + A_WIDTH].astype(F32)
                if which == 2:
                    cols.append(x.astype(BF16))
                    continue
                r = lax.rsqrt(_seg_sum(x * x, b) + EPS)
                y = x * r * g[grp * 2 + which:grp * 2 + which + 1, :]
                cols.append(_rope_apply(y, c_t, s_t, low).astype(BF16))
            groups.append(jnp.concatenate(cols, axis=1))
        return tuple(groups)
    return _rows("attn_prep", fn, [(qkv, "row"), (gains, "full"), (ct, "row"), (st, "row"), (consts, "full"), (bd, "full")],
                 [(3 * A_WIDTH, BF16, d) for _, d in SWA_GROUPS], tr=256)


def _band_mask(n):
    row = lax.broadcasted_iota(jnp.int32, (BAND, 2 * BAND), 0)
    col = lax.broadcasted_iota(jnp.int32, (BAND, 2 * BAND), 1)
    dist = row + BAND - col
    return (dist >= 0) & (dist <= BAND) & ((col >= BAND) | (n > 0))


def _attn_fwd(qkvn, grp):
    d, L, _ = qkvn.shape
    nblk = L // BAND
    assert L % BAND == 0 and d == SWA_GROUPS[grp][1]

    def body(q_ref, kc_ref, kp_ref, vc_ref, vp_ref, o_ref, lse_ref):
        n = pl.program_id(1)
        valid = _band_mask(n)
        first = lax.broadcasted_iota(jnp.int32, (BAND, LANE), 1) < A_HEAD_DIM
        pairs = [slice(pr * LANE, (pr + 1) * LANE) for pr in range(A_WIDTH // LANE)]
        halves = (first, jnp.logical_not(first))
        qps = [q_ref[:, sl] for sl in pairs]
        kcats = [jnp.concatenate([kp_ref[:, sl], kc_ref[:, sl]], axis=0) for sl in pairs]
        vcats = [jnp.concatenate([vp_ref[:, sl], vc_ref[:, sl]], axis=0) for sl in pairs]
        heads = [(pr, m) for pr in range(len(pairs)) for m in halves]
        ss = [_dot(jnp.where(m, qps[pr], jnp.zeros_like(qps[pr])), kcats[pr], 1, 1) for pr, m in heads]
        ps, lses = [], []
        for s in ss:
            s = jnp.where(valid, s * (A_HEAD_DIM ** -0.5), -1e30)
            mx = jnp.max(s, axis=-1, keepdims=True)
            e = jnp.exp(s - mx)
            l = jnp.sum(e, axis=-1, keepdims=True)
            ps.append((e / l).astype(BF16))
            lses.append(mx + jnp.log(l))
        os_ = [_dot(p, vcats[pr]) for p, (pr, _) in zip(ps, heads)]
        o_ref[...] = jnp.concatenate([jnp.where(first, os_[2 * pr], os_[2 * pr + 1]) for pr in range(len(pairs))], axis=1)
        lse_ref[...] = jnp.concatenate([jnp.where(first, lses[2 * pr], lses[2 * pr + 1]) for pr in range(len(pairs))], axis=1)

    blk = (None, BAND, A_WIDTH)
    return pl.pallas_call(
        body, name=f"attn_fwd_g{grp}", grid=(d, nblk),
        in_specs=[pl.BlockSpec(blk, lambda r, n: (r, n, 0)),
                  pl.BlockSpec(blk, lambda r, n: (r, n, 1)),
                  pl.BlockSpec(blk, lambda r, n: (r, jnp.maximum(n - 1, 0), 1)),
                  pl.BlockSpec(blk, lambda r, n: (r, n, 2)),
                  pl.BlockSpec(blk, lambda r, n: (r, jnp.maximum(n - 1, 0), 2))],
        out_specs=[pl.BlockSpec(blk, lambda r, n: (r, n, 0)), pl.BlockSpec(blk, lambda r, n: (r, n, 0))],
        out_shape=[jax.ShapeDtypeStruct((d, L, A_WIDTH), F32)] * 2,
        compiler_params=_cparams(("parallel", "parallel")),
    )(qkvn, qkvn, qkvn, qkvn, qkvn)


def _merge_weights(l0, l1, l2):
    mx = jnp.maximum(jnp.maximum(l0, l1), l2)
    e0, e1, e2 = jnp.exp(l0 - mx), jnp.exp(l1 - mx), jnp.exp(l2 - mx)
    inv = 1.0 / (e0 + e1 + e2)
    return e0 * inv, e1 * inv, e2 * inv


def _attn_merge(os_, lses):
    def fn(i, n, o0, o1, o2, l0, l1, l2):
        w0, w1, w2 = _merge_weights(l0, l1, l2)
        return (w0 * o0 + w1 * o1 + w2 * o2,)
    ins = [(a, "res") for a in (*os_, *lses)]
    return _rows("attn_merge", fn, ins, [(A_WIDTH, BF16)], tr=256)


def _attn_merge_bwd(do, os_, lses, bd1):
    def fn(i, n, dot_, o0, o1, o2, l0, l1, l2, b):
        w0, w1, w2 = _merge_weights(l0, l1, l2)
        o = w0 * o0 + w1 * o1 + w2 * o2
        dsum = _seg_sum(dot_ * o, b)
        return (w0 * dot_, w1 * dot_, w2 * dot_, -w0 * dsum, -w1 * dsum, -w2 * dsum)
    ins = [(do, "row")] + [(a, "res") for a in (*os_, *lses)] + [(bd1, "full")]
    res = _rows("attn_merge_bwd", fn, ins, [(A_WIDTH, dt, d) for dt in (BF16, F32) for _, d in SWA_GROUPS], tr=256)
    return res[:3], res[3:]


def _lane_pick(x, lane_idx, lane):
    return jnp.sum(jnp.where(lane_idx == lane, x, 0.0), axis=-1, keepdims=True)


def _attn_bwd(qkvn, grp, do_g, lse, c_g):
    d, L, _ = qkvn.shape
    nblk = L // BAND

    def body(q_ref, kc_ref, kp_ref, vc_ref, vp_ref, do_ref, lse_ref, c_ref, dq_ref, dk_ref, dv_ref, ck, cv_):
        n = pl.program_id(1)

        @pl.when(n == 0)
        def _():
            ck[...] = jnp.zeros_like(ck)
            cv_[...] = jnp.zeros_like(cv_)

        @pl.when(n < nblk)
        def _():
            valid = _band_mask(n)
            lane = lax.broadcasted_iota(jnp.int32, (BAND, LANE), 1)
            first = lane < A_HEAD_DIM
            lane2 = lax.broadcasted_iota(jnp.int32, (2 * BAND, LANE), 1) < A_HEAD_DIM
            pairs = [slice(pr * LANE, (pr + 1) * LANE) for pr in range(A_WIDTH // LANE)]
            halves = (first, jnp.logical_not(first))
            qps = [q_ref[:, sl] for sl in pairs]
            dops = [do_ref[:, sl] for sl in pairs]
            kcats = [jnp.concatenate([kp_ref[:, sl], kc_ref[:, sl]], axis=0) for sl in pairs]
            vcats = [jnp.concatenate([vp_ref[:, sl], vc_ref[:, sl]], axis=0) for sl in pairs]
            heads = [(pr, hh) for pr in range(len(pairs)) for hh in range(2)]
            zero = jnp.zeros_like(qps[0])
            ss = [_dot(jnp.where(halves[hh], qps[pr], zero), kcats[pr], 1, 1) for pr, hh in heads]
            dps = [_dot(jnp.where(halves[hh], dops[pr], zero), vcats[pr], 1, 1) for pr, hh in heads]
            dss, pbs = [], []
            for (pr, hh), s, dp in zip(heads, ss, dps):
                lse_h = _lane_pick(lse_ref[:, pairs[pr]], lane, hh * A_HEAD_DIM)
                c_h = _lane_pick(c_ref[:, pairs[pr]], lane, hh * A_HEAD_DIM)
                p = jnp.where(valid, jnp.exp(s * (A_HEAD_DIM ** -0.5) - lse_h), 0.0)
                dss.append((p * (dp + c_h) * (A_HEAD_DIM ** -0.5)).astype(BF16))
                pbs.append(p.astype(BF16))
            dqs = [_dot(ds, kcats[pr]) for ds, (pr, _) in zip(dss, heads)]
            dks = [_dot(ds, qps[pr], 0, 0) for ds, (pr, _) in zip(dss, heads)]
            dvs = [_dot(pb, dops[pr], 0, 0) for pb, (pr, _) in zip(pbs, heads)]
            for pr, sl in enumerate(pairs):
                dq_ref[:, sl] = jnp.where(first, dqs[2 * pr], dqs[2 * pr + 1])
                dkc = jnp.where(lane2, dks[2 * pr], dks[2 * pr + 1])
                dvc = jnp.where(lane2, dvs[2 * pr], dvs[2 * pr + 1])
                dk_ref[:, sl] = ck[:, sl] + dkc[:BAND]
                dv_ref[:, sl] = cv_[:, sl] + dvc[:BAND]
                ck[:, sl] = dkc[BAND:]
                cv_[:, sl] = dvc[BAND:]

        @pl.when(n == nblk)
        def _():
            dk_ref[...] = ck[...]
            dv_ref[...] = cv_[...]

    blk = (None, BAND, A_WIDTH)
    last = nblk - 1
    qn = lambda n: jnp.minimum(n, last)
    pn = lambda n: jnp.clip(n - 1, 0, last)
    return tuple(pl.pallas_call(
        body, name=f"attn_bwd_g{grp}", grid=(d, nblk + 1),
        in_specs=[pl.BlockSpec(blk, lambda r, n: (r, qn(n), 0)),
                  pl.BlockSpec(blk, lambda r, n: (r, qn(n), 1)),
                  pl.BlockSpec(blk, lambda r, n: (r, pn(n), 1)),
                  pl.BlockSpec(blk, lambda r, n: (r, qn(n), 2)),
                  pl.BlockSpec(blk, lambda r, n: (r, pn(n), 2)),
                  pl.BlockSpec(blk, lambda r, n: (r, qn(n), 0)),
                  pl.BlockSpec(blk, lambda r, n: (r, qn(n), 0)),
                  pl.BlockSpec(blk, lambda r, n: (r, qn(n), 0))],
        out_specs=[pl.BlockSpec(blk, lambda r, n: (r, qn(n), 0)),
                   pl.BlockSpec(blk, lambda r, n: (r, pn(n), 0)),
                   pl.BlockSpec(blk, lambda r, n: (r, pn(n), 0))],
        out_shape=[jax.ShapeDtypeStruct((d, L, A_WIDTH), F32)] * 3,
        scratch_shapes=[pltpu.VMEM((BAND, A_WIDTH), F32), pltpu.VMEM((BAND, A_WIDTH), F32)],
        compiler_params=_cparams(("parallel", "arbitrary")),
    )(qkvn, qkvn, qkvn, qkvn, qkvn, do_g, lse, c_g))


def _attn_prep_bwd(qkv, grads, gains, ct, st, consts, bd):
    def fn(i, n, t, g, c_t, s_t, c, b, *gr):
        low = c[2:3, :] > 0.5
        in16 = c[3:4, :] > 0.5
        c_t, s_t = _rope_wide(c_t), _rope_wide(s_t)
        cols, dgs = [], []
        for grp in range(3):
            for which in range(3):
                dout = gr[grp * 3 + which]
                if which == 2:
                    cols.append(dout.astype(BF16))
                    continue
                off = (grp * 3 + which) * A_WIDTH
                x = t[:, off:off + A_WIDTH].astype(F32)
                gain = g[grp * 2 + which:grp * 2 + which + 1, :]
                r = lax.rsqrt(_seg_sum(x * x, b) + EPS)
                xh = x * r
                dy = _rope_apply_bwd(dout, c_t, s_t, low, in16)
                dyn = dy * gain
                dx = r * (dyn - xh * _seg_sum(dyn * xh, b))
                cols.append(dx.astype(BF16))
                dgs.append(_colsum(dy * xh))
        return (jnp.concatenate(cols, axis=1), *dgs)
    ins = [(qkv, "row"), (gains, "full"), (ct, "row"), (st, "row"), (consts, "full"), (bd, "full")] + [(a, "res") for a in grads]
    res = _rows("attn_prep_bwd", fn, ins, [(A_QKV, BF16)], tr=128, accs=[((1, A_WIDTH), F32)] * 6)
    return res[0], res[1:]


DN_QKV = 3 * DN_WIDTH
DN_QKVZ = DN_QKV + DN_WIDTH


def _sigmoid(x):
    return 1.0 / (1.0 + jnp.exp(-x))


def _softplus(x):
    return jnp.maximum(x, 0.0) + jnp.log(1.0 + jnp.exp(-jnp.abs(x)))


def _conv_taps(xs, w, tr):
    acc = None
    for j in range(CONV_W):
        sh = CONV_W - 1 - j
        term = (pltpu.roll(xs, sh, 0) if sh else xs)[SUBLANE:] * w[j:j + 1, :]
        acc = term if acc is None else acc + term
    return acc


def _dn_prep(qkvz, ab, convw, alog_row, dt_row):
    tr = 256

    def fn(i, n, x, xp, abt, w, al, dt):
        xp = jnp.where(i > 0, xp, 0.0)
        u = _conv_taps(jnp.concatenate([xp, x], axis=0), w, tr)
        y = u * _sigmoid(u)
        qs, ks = [], []
        for h in range(DN_HEADS):
            for dst, base, sc in ((qs, 0, DN_DIM ** -0.5), (ks, DN_WIDTH, 1.0)):
                seg = y[:, base + h * DN_DIM:base + (h + 1) * DN_DIM]
                dst.append(seg * (lax.rsqrt(jnp.sum(seg * seg, axis=-1, keepdims=True) + EPS) * sc))
        lane = lax.broadcasted_iota(jnp.int32, abt.shape, 1)
        g = -jnp.exp(al) * _softplus(abt + dt)
        gb = jnp.where(lane < DN_HEADS, g, jnp.where(lane < 2 * DN_HEADS, _sigmoid(abt), 0.0))
        return u, jnp.concatenate(qs, axis=1), jnp.concatenate(ks, axis=1), y[:, 2 * DN_WIDTH:], gb

    ins = [(qkvz, "row", (0, DN_QKV)), (qkvz, "prev8", (0, DN_QKV)), (ab, "row"), (convw, "full"),
           (alog_row, "full"), (dt_row, "full")]
    return _rows("dn_prep", fn, ins, [(DN_QKV, F32), (DN_WIDTH, F32), (DN_WIDTH, F32), (DN_WIDTH, F32), (LANE, F32)], tr=tr)


def _tri_masks():
    row = lax.broadcasted_iota(jnp.int32, (CHUNK, CHUNK), 0)
    col = lax.broadcasted_iota(jnp.int32, (CHUNK, CHUNK), 1)
    return row >= col, row > col, row == col


def _heads(fn, *lists):
    return [fn(*xs) for xs in zip(*lists)]


def _split(x):
    hi = x.astype(BF16)
    return hi, (x - hi.astype(F32)).astype(BF16)


def _dot3(a, b, ca=1, cb=0):
    (ah, al), (bh, bl) = a, b
    return _dot(ah, bh, ca, cb) + (_dot(ah, bl, ca, cb) + _dot(al, bh, ca, cb))


SPLIT_STEPS = 3


def _unit_lower_inverse(a_list, eye):
    ts = [eye - a for a in a_list]
    parts = [_split(a) for a in a_list]
    for step in range(5):
        if step < SPLIT_STEPS:
            parts = [_split(_dot3(p, p)) for p in parts]
            ts = [t + _dot3(_split(t), p) for t, p in zip(ts, parts)]
        else:
            parts = [(_dot(p[0], p[0]).astype(BF16), None) for p in parts]
            ts = [t + _dot(t.astype(BF16), p[0]) for t, p in zip(ts, parts)]
    return ts


def _dn_terms(qs, ks, vs, gb):
    lower, strict, diag = _tri_masks()
    lane = lax.broadcasted_iota(jnp.int32, (CHUNK, LANE), 1)
    is_last = lax.broadcasted_iota(jnp.int32, (CHUNK, 1), 0) == CHUNK - 1
    hs = range(DN_HEADS)
    gc = _dot(lower.astype(F32), gb, precision=HIGHEST)
    gct = jnp.transpose(gc)
    bcol = [_lane_pick(gb, lane, DN_HEADS + h) for h in hs]
    gcol = [_lane_pick(gc, lane, h) for h in hs]
    glast = [jnp.sum(jnp.where(is_last, g, 0.0), axis=0, keepdims=True) for g in gcol]
    decay = [jnp.exp(jnp.where(lower, gcol[h] - gct[h:h + 1, :], -1e30)) for h in hs]
    kb = _heads(lambda k, b: k * b, ks, bcol)
    kk = _heads(lambda x, k: _bdot(x, k, 1, 1), kb, ks)
    qk = _heads(lambda q, k: _bdot(q, k, 1, 1), qs, ks)
    a = _heads(lambda x, d: jnp.where(strict, x * d, 0.0), kk, decay)
    t = [_split(x) for x in _unit_lower_inverse(a, diag.astype(F32))]
    eg = [jnp.exp(g) for g in gcol]
    egl = _heads(lambda gl, g: jnp.exp(gl - g), glast, gcol)
    rhs_w = _heads(lambda x, e: x * e, kb, eg)
    u = _heads(lambda tt, v, b: _dot3(tt, _split(v * b)), t, vs, bcol)
    w = _heads(lambda tt, r: _dot3(tt, _split(r)), t, rhs_w)
    return dict(bcol=bcol, decay=decay, kb=kb, a=a, t=t, eg=eg, egl=egl, rhs_w=rhs_w, u=u, w=w,
                attn=_heads(lambda x, d: x * d, qk, decay), q_dec=_heads(lambda q, e: q * e, qs, eg),
                k_dec=_heads(lambda k, e: k * e, ks, egl), c_dec=[jnp.exp(g) for g in glast],
                lower=lower, strict=strict, lane=lane, is_last=is_last)


def _head_slices(ref):
    return [ref[:, h * DN_DIM:(h + 1) * DN_DIM] for h in range(DN_HEADS)]


def _dn_chunk_fwd(q, k, v, gb):
    S = q.shape[0]
    N = S // CHUNK

    def body(q_ref, k_ref, v_ref, gb_ref, o_ref, st_ref, state):
        @pl.when(pl.program_id(0) == 0)
        def _():
            state[...] = jnp.zeros_like(state)

        f = _dn_terms(_head_slices(q_ref), _head_slices(k_ref), _head_slices(v_ref), gb_ref[...])
        s = [state[h] for h in range(DN_HEADS)]
        for h in range(DN_HEADS):
            st_ref[0, h] = s[h]
        sb = [x.astype(BF16) for x in s]
        v_new = _heads(lambda u, w, x: u - _bdot(w, x), f["u"], f["w"], sb)
        o = _heads(lambda qd, x, at, vn: _bdot(qd, x) + _bdot(at, vn), f["q_dec"], sb, f["attn"], v_new)
        new_s = _heads(lambda x, c, kd, vn: x * c + _bdot(kd, vn, 0, 0), s, f["c_dec"], f["k_dec"], v_new)
        for h in range(DN_HEADS):
            o_ref[:, h * DN_DIM:(h + 1) * DN_DIM] = o[h]
            state[h] = new_s[h]

    blk = pl.BlockSpec((CHUNK, DN_WIDTH), lambda n: (n, 0))
    st_blk = pl.BlockSpec((1, DN_HEADS, DN_DIM, DN_DIM), lambda n: (n, 0, 0, 0))
    return pl.pallas_call(
        body, name="dn_chunk_fwd", grid=(N,),
        in_specs=[blk, blk, blk, pl.BlockSpec((CHUNK, LANE), lambda n: (n, 0))],
        out_specs=[blk, st_blk],
        out_shape=[jax.ShapeDtypeStruct((S, DN_WIDTH), F32), jax.ShapeDtypeStruct((N, DN_HEADS, DN_DIM, DN_DIM), F32)],
        scratch_shapes=[pltpu.VMEM((DN_HEADS, DN_DIM, DN_DIM), F32)],
        compiler_params=_cparams(("arbitrary",)),
    )(q, k, v, gb)


def _dn_chunk_bwd(q, k, v, gb, states, do):
    S = q.shape[0]
    N = S // CHUNK

    def body(q_ref, k_ref, v_ref, gb_ref, st_ref, do_ref, dq_ref, dk_ref, dv_ref, dgb_ref, dstate):
        @pl.when(pl.program_id(0) == 0)
        def _():
            dstate[...] = jnp.zeros_like(dstate)

        hs = range(DN_HEADS)
        qs, ks, vs, dos = (_head_slices(r) for r in (q_ref, k_ref, v_ref, do_ref))
        f = _dn_terms(qs, ks, vs, gb_ref[...])
        lane, is_last = f["lane"], f["is_last"]
        rowsum = lambda x: jnp.sum(x, axis=-1, keepdims=True)
        s = [st_ref[0, h] for h in hs]
        dsn = [dstate[h] for h in hs]
        sb = [x.astype(BF16) for x in s]
        dsb = [x.astype(BF16) for x in dsn]
        dob = [x.astype(BF16) for x in dos]
        v_new = _heads(lambda u, w, x: u - _bdot(w, x), f["u"], f["w"], sb)
        dv_new = _heads(lambda at, d, kd, x: _bdot(at, d, 0, 0) + _bdot(kd, x), f["attn"], dob, f["k_dec"], dsb)
        dattn = _heads(lambda d, vn: _bdot(d, vn, 1, 1), dob, v_new)
        dq_dec = _heads(lambda d, x: _bdot(d, x, 1, 1), dob, sb)
        dk_dec = _heads(lambda vn, x: _bdot(vn, x, 1, 1), v_new, dsb)
        dw = _heads(lambda dv_, x: -_bdot(dv_, x, 1, 1), dv_new, sb)
        new_ds = _heads(lambda x, c, qd, d, w, dv_: x * c + _bdot(qd, d, 0, 0) - _bdot(w, dv_, 0, 0),
                        dsn, f["c_dec"], f["q_dec"], dob, f["w"], dv_new)
        for h in hs:
            dstate[h] = new_ds[h]
        drhs_u = _heads(lambda tt, x: _dot3(tt, _split(x), 0, 0), f["t"], dv_new)
        drhs_w = _heads(lambda tt, x: _dot3(tt, _split(x), 0, 0), f["t"], dw)
        da = _heads(lambda du_, u, dw_, w: jnp.where(f["strict"], -(_bdot(du_, u, 1, 1) + _bdot(dw_, w, 1, 1)), 0.0),
                    drhs_u, f["u"], drhs_w, f["w"])
        dkk = _heads(lambda x, d: x * d, da, f["decay"])
        dqk = _heads(lambda x, d: x * d, dattn, f["decay"])
        dkb = _heads(lambda x, k_, dw_, e: _bdot(x, k_) + dw_ * e, dkk, ks, drhs_w, f["eg"])
        dq = _heads(lambda x, k_, dqd, e: _bdot(x, k_) + dqd * e, dqk, ks, dq_dec, f["eg"])
        dk = _heads(lambda x, kb_, y, q_, dkd, el, dkb_, b: _bdot(x, kb_, 0, 0) + _bdot(y, q_, 0, 0) + dkd * el + dkb_ * b,
                    dkk, f["kb"], dqk, qs, dk_dec, f["egl"], dkb, f["bcol"])
        m = _heads(lambda x, a_, y, at: x * a_ + y * at, da, f["a"], dattn, f["attn"])
        ones = jnp.ones((CHUNK, LANE), BF16)
        col_m = [(_dot(mh, ones, 0, 0) + _dot(ml, ones, 0, 0))[:, 0:1] for mh, ml in map(_split, m)]
        dgc_all = jnp.zeros((CHUNK, LANE), F32)
        dbeta_all = jnp.zeros((CHUNK, LANE), F32)
        for h in hs:
            dq_ref[:, h * DN_DIM:(h + 1) * DN_DIM] = dq[h]
            dk_ref[:, h * DN_DIM:(h + 1) * DN_DIM] = dk[h]
            dv_ref[:, h * DN_DIM:(h + 1) * DN_DIM] = drhs_u[h] * f["bcol"][h]
            kdec_term = rowsum(dk_dec[h] * f["k_dec"][h])
            dc_dec = _sum_all(dsn[h] * s[h])
            dgc = (rowsum(m[h]) - col_m[h] + rowsum(dq_dec[h] * f["q_dec"][h]) - kdec_term
                   + rowsum(drhs_w[h] * f["rhs_w"][h]))
            last_extra = jnp.sum(kdec_term, axis=0, keepdims=True) + dc_dec * f["c_dec"][h]
            dgc = dgc + jnp.where(is_last, last_extra, 0.0)
            dbeta = rowsum(drhs_u[h] * vs[h]) + rowsum(dkb[h] * ks[h])
            dgc_all = jnp.where(lane == h, dgc, dgc_all)
            dbeta_all = jnp.where(lane == DN_HEADS + h, dbeta, dbeta_all)
        dg_all = _dot(f["lower"].astype(F32), dgc_all, 0, 0, precision=HIGHEST)
        dgb_ref[...] = jnp.where(lane < DN_HEADS, dg_all, dbeta_all)

    rev = lambda n: (N - 1 - n, 0)
    blk = pl.BlockSpec((CHUNK, DN_WIDTH), rev)
    gblk = pl.BlockSpec((CHUNK, LANE), rev)
    st_blk = pl.BlockSpec((1, DN_HEADS, DN_DIM, DN_DIM), lambda n: (N - 1 - n, 0, 0, 0))
    return pl.pallas_call(
        body, name="dn_chunk_bwd", grid=(N,),
        in_specs=[blk, blk, blk, gblk, st_blk, blk],
        out_specs=[blk, blk, blk, gblk],
        out_shape=[jax.ShapeDtypeStruct((S, DN_WIDTH), F32)] * 3 + [jax.ShapeDtypeStruct((S, LANE), F32)],
        scratch_shapes=[pltpu.VMEM((DN_HEADS, DN_DIM, DN_DIM), F32)],
        compiler_params=_cparams(("arbitrary",)),
    )(q, k, v, gb, states, do)


def _dn_post(o, qkvz, gain_row):
    def fn(i, n, ot, z, g):
        cols = []
        for h in range(DN_HEADS):
            seg = ot[:, h * DN_DIM:(h + 1) * DN_DIM]
            cols.append(seg * lax.rsqrt(jnp.mean(seg * seg, axis=-1, keepdims=True) + EPS) * g)
        return (jnp.concatenate(cols, axis=1) * (z * _sigmoid(z)),)
    return _rows("dn_post", fn, [(o, "row"), (qkvz, "row", (3, DN_WIDTH)), (gain_row, "full")], [(DN_WIDTH, BF16)], tr=512)


def _dn_post_bwd(don, o, qkvz, gain_row):
    def fn(i, n, dy, ot, z, g):
        sg = _sigmoid(z)
        sz = z * sg
        dos, ohs = [], []
        dg = jnp.zeros((1, DN_DIM), F32)
        for h in range(DN_HEADS):
            sl = slice(h * DN_DIM, (h + 1) * DN_DIM)
            seg = ot[:, sl]
            r = lax.rsqrt(jnp.mean(seg * seg, axis=-1, keepdims=True) + EPS)
            oh = seg * r
            dno = dy[:, sl] * sz[:, sl]
            dg = dg + _colsum(dno * oh)
            dn = dno * g
            dos.append(r * (dn - oh * jnp.mean(dn * oh, axis=-1, keepdims=True)))
            ohs.append(oh * g)
        dz = dy * jnp.concatenate(ohs, axis=1) * (sg * (1.0 + z * (1.0 - sg)))
        return jnp.concatenate(dos, axis=1), dz, dg
    ins = [(don, "row"), (o, "row"), (qkvz, "row", (3, DN_WIDTH)), (gain_row, "full")]
    return _rows("dn_post_bwd", fn, ins, [(DN_WIDTH, F32), (DN_WIDTH, F32)], tr=256, accs=[((1, DN_DIM), F32)])


def _dn_prep_bwd(dq, dk, dv, dgb, u, ab, alog_row, dt_row):
    def fn(i, n, dqt, dkt, dvt, dgbt, ut, abt, al, dt):
        sg = _sigmoid(ut)
        y = ut * sg
        dys = []
        for grad, base, sc in ((dqt, 0, DN_DIM ** -0.5), (dkt, DN_WIDTH, 1.0)):
            for h in range(DN_HEADS):
                seg = y[:, base + h * DN_DIM:base + (h + 1) * DN_DIM]
                gr = grad[:, h * DN_DIM:(h + 1) * DN_DIM]
                r = lax.rsqrt(jnp.sum(seg * seg, axis=-1, keepdims=True) + EPS)
                xh = seg * r
                dys.append((r * sc) * (gr - xh * jnp.sum(gr * xh, axis=-1, keepdims=True)))
        dy = jnp.concatenate(dys + [dvt], axis=1)
        du = dy * (sg * (1.0 + ut * (1.0 - sg)))
        lane = lax.broadcasted_iota(jnp.int32, abt.shape, 1)
        is_g = lane < DN_HEADS
        ea = jnp.exp(al)
        x = abt + dt
        slope = -ea * _sigmoid(x)
        gval = -ea * _softplus(x)
        dg = jnp.where(is_g, dgbt, 0.0)
        beta = _sigmoid(abt)
        dab = jnp.where(is_g, dg * slope, jnp.where(lane < 2 * DN_HEADS, dgbt * beta * (1.0 - beta), 0.0))
        return du, dab, _colsum(dg * gval), _colsum(dg * slope)
    ins = [(dq, "row"), (dk, "row"), (dv, "row"), (dgb, "row"), (u, "row"), (ab, "row"), (alog_row, "full"), (dt_row, "full")]
    return _rows("dn_prep_bwd", fn, ins, [(DN_QKV, F32), (LANE, BF16)], tr=256, accs=[((1, LANE), F32)] * 2)


def _dn_conv_bwd(du, dz, qkvz, convw):
    tr = 256

    def fn(i, n, dut, dun, dzt, x, xp, w):
        dun = jnp.where(i < n - 1, dun, 0.0)
        dus = jnp.concatenate([dut, dun], axis=0)
        xs = jnp.concatenate([jnp.where(i > 0, xp, 0.0), x], axis=0)
        dx = None
        dws = []
        for j in range(CONV_W):
            sh = CONV_W - 1 - j
            term = (pltpu.roll(dus, tr + SUBLANE - sh, 0) if sh else dus)[:tr] * w[j:j + 1, :]
            dx = term if dx is None else dx + term
            dws.append(_colsum(dut * (pltpu.roll(xs, sh, 0) if sh else xs)[SUBLANE:]))
        return (jnp.concatenate([dx.astype(BF16), dzt.astype(BF16)], axis=1), *dws)

    ins = [(du, "row"), (du, "next8"), (dz, "row"), (qkvz, "row", (0, DN_QKV)), (qkvz, "prev8", (0, DN_QKV)), (convw, "full")]
    res = _rows("dn_conv_bwd", fn, ins, [(DN_QKVZ, BF16)], tr=tr, accs=[((1, DN_QKV), F32)] * CONV_W)
    return res[0], res[1:]


def _add(acc, r):
    return (r + acc,)


def _mlp_ple_fwd(i, x1, hm, p_i, ple_gain, next_gain, w_up, w_down, w_ple, w_gate):
    u, a = _mm(f"mlp_up{i}", hm, w_up, epilogue=lambda acc: (acc, jnp.square(jnp.maximum(acc, 0.0))),
               out_dtypes=(BF16, BF16))
    x2, hp = _mm(f"mlp_down{i}", a, w_down, epilogue=_res_norm, extras=(x1, ple_gain), out_dtypes=(F32, BF16),
                 tm_pref=FUSED_ROWS)
    pp = _mm(f"ple_proj{i}", p_i, w_ple)

    def gate_epilogue(acc, x2t, ppt, *g):
        gate = _sigmoid(acc)
        x3 = x2t + ppt * gate
        if not g:
            return x3, gate
        return x3, gate, x3 * lax.rsqrt(jnp.mean(x3 * x3, axis=-1, keepdims=True) + EPS) * g[0]

    more = () if next_gain is None else (next_gain,)
    x3, gate, *h_next = _mm(f"ple_gate{i}", hp, w_gate, epilogue=gate_epilogue, extras=(x2, pp) + more,
                            out_dtypes=(F32, F32) + (BF16,) * len(more), tm_pref=FUSED_ROWS)
    return x3, (h_next[0] if more else None), dict(x1=x1, hm=hm, u=u, a=a, x2=x2, hp=hp, pp=pp, gate=gate, p=p_i)


def _mlp_ple_bwd(i, dx3, sv, mlp_gain, ple_gain, w_up, w_down, w_gate):
    def fn(_i, _n, d, g, pp):
        return d * g, d * pp * g * (1.0 - g)
    dpp, dzg = _rows(f"ple_gate_bwd{i}", fn, [(dx3, "row"), (sv["gate"], "row"), (sv["pp"], "row")],
                     [(D_MODEL, BF16), (D_MODEL, BF16)], tr=512)
    d_w_ple = _mm(f"ple_proj_dw{i}", sv["p"], dpp, ta=True, out_dtypes=(BF16,))
    d_w_gate = _mm(f"ple_gate_dw{i}", sv["hp"], dzg, ta=True, out_dtypes=(BF16,))
    dx2, dx2b, d_ple_gain = _mm(f"ple_gate_dx{i}", dzg, w_gate, tb=True, epilogue=_norm_bwd_2,
                                extras=(sv["x2"], ple_gain, dx3), out_dtypes=(F32, BF16), n_colsums=1, tm_pref=FUSED_ROWS)
    d_w_down = _mm(f"mlp_down_dw{i}", sv["a"], dx2b, ta=True, out_dtypes=(BF16,))
    du = _mm(f"mlp_down_dx{i}", dx2b, w_down, tb=True,
             epilogue=lambda acc, ut: (acc * (2.0 * jnp.maximum(ut.astype(F32), 0.0)),), extras=(sv["u"],), out_dtypes=(BF16,))
    d_w_up = _mm(f"mlp_up_dw{i}", sv["hm"], du, ta=True, out_dtypes=(BF16,))
    dx1, dx1b, d_mlp_gain = _mm(f"mlp_up_dx{i}", du, w_up, tb=True, epilogue=_norm_bwd_2,
                                extras=(sv["x1"], mlp_gain, dx2), out_dtypes=(F32, BF16), n_colsums=1, tm_pref=FUSED_ROWS)
    return dx1, dx1b, dict(w_ple=d_w_ple, w_ple_gate=d_w_gate, w_down=d_w_down, w_up=d_w_up,
                           ple_norm=d_ple_gain, mlp_norm=d_mlp_gain)


def _loss_fwd_bwd(y, target):
    D = y.shape[1]

    def fn(i, n, yt, tt):
        e = yt - tt
        return e * (1.0 / D), _colsum(e * e)
    dy, sq = _rows("loss", fn, [(y, "row"), (target, "row")], [(D, F32)], tr=512, accs=[((1, D), F32)])
    return sq, dy


def _after(small, token):
    return small + token[0:1, 0:1]


def _local_step(x, p, positions, target, W, P, rest_of_weights, send_layer1, send_mlp0, send_attn):
    consts = _head_consts()
    bd = _block_diag(1.0 / A_HEAD_DIM)
    bd1 = _block_diag(1.0)
    ct, st = _rope_tables(positions, consts)
    gains = jnp.stack([jnp.tile(v, A_HEADS) for g in range(3) for v in (P["attn_q_gain"][g], P["attn_k_gain"][g])])
    pad = LANE - DN_HEADS
    alog_row = jnp.pad(P["dn_a_log"].reshape(1, DN_HEADS), ((0, 0), (0, pad)))
    dt_row = jnp.pad(P["dn_dt_bias"].reshape(1, DN_HEADS), ((0, 0), (0, pad)))
    ogain_row = P["dn_o_gain"].reshape(1, DN_DIM)
    row = lambda name, i: P[name][i:i + 1]

    h0 = _rmsnorm_fwd("mix_norm0", x, row("mix_norm", 0))
    qkv = _mm("attn_qkv", h0, W["attn_w_qkv"], out_dtypes=(BF16,))
    qkvn = _attn_prep(qkv, gains, ct, st, consts, bd)
    os_, lses = zip(*[_attn_fwd(qkvn[g], g) for g in range(3)])
    o_attn = _attn_merge(os_, lses)
    x1, hm0 = _mm("attn_out", o_attn, W["attn_w_o"], epilogue=_res_norm, extras=(x, row("mlp_norm", 0)),
                  out_dtypes=(F32, BF16), tm_pref=FUSED_ROWS)
    W = {**W, **rest_of_weights(x1)}
    x3, h1, sv0 = _mlp_ple_fwd(0, x1, hm0, p[0], row("ple_norm", 0), row("mix_norm", 1),
                               W["w_up"][0], W["w_down"][0], W["w_ple"][0], W["w_ple_gate"][0])
    qkvz = _mm("dn_in_qkvz", h1, W["dn_w_qkvz"])
    ab = _mm("dn_in_ab", h1, W["dn_w_ab"])
    u, q, k, v, gb = _dn_prep(qkvz, ab, W["dn_conv"], alog_row, dt_row)
    o_dn, states = _dn_chunk_fwd(q, k, v, gb)
    on = _dn_post(o_dn, qkvz, ogain_row)
    x4, hm1 = _mm("dn_out", on, W["dn_w_o"], epilogue=_res_norm, extras=(x3, row("mlp_norm", 1)),
                  out_dtypes=(F32, BF16), tm_pref=FUSED_ROWS)
    x6, _, sv1 = _mlp_ple_fwd(1, x4, hm1, p[1], row("ple_norm", 1), None,
                              W["w_up"][1], W["w_down"][1], W["w_ple"][1], W["w_ple_gate"][1])
    sq, dy = _loss_fwd_bwd(x6, target)

    dx4, dx4b, g1 = _mlp_ple_bwd(1, dy, sv1, row("mlp_norm", 1), row("ple_norm", 1),
                                 W["w_up"][1], W["w_down"][1], W["w_ple_gate"][1])
    don = _mm("dn_out_dx", dx4b, W["dn_w_o"], tb=True)
    d_dn_w_o = _mm("dn_out_dw", on, dx4b, ta=True, out_dtypes=(BF16,))
    do_dn, dz, d_ogain = _dn_post_bwd(don, o_dn, qkvz, ogain_row)
    dq, dk, dv, dgb = _dn_chunk_bwd(q, k, v, gb, states, do_dn)
    du, dab, d_alog, d_dt = _dn_prep_bwd(dq, dk, dv, dgb, u, ab, alog_row, dt_row)
    dqkvz, d_conv = _dn_conv_bwd(du, dz, qkvz, W["dn_conv"])
    dh1 = _mm("dn_in_ab_dx", dab, W["dn_w_ab"], tb=True)
    dx3, d_mix1 = _mm("dn_in_qkvz_dx", dqkvz, W["dn_w_qkvz"], tb=True,
                      epilogue=lambda acc, part, xt, g, dres: _norm_bwd(acc + part, xt, g, dres),
                      extras=(dh1, x3, row("mix_norm", 1), dx4), n_colsums=1, tm_pref=FUSED_ROWS)
    d_w_qkvz = _mm("dn_in_qkvz_dw", h1, dqkvz, ta=True, out_dtypes=(BF16,))
    d_w_ab = _mm("dn_in_ab_dw", h1, dab, ta=True, out_dtypes=(BF16,))
    token = send_layer1(dict(
        dn_w_qkvz=d_w_qkvz, dn_w_ab=d_w_ab, dn_conv=jnp.concatenate(d_conv, 0), dn_w_o=d_dn_w_o,
        w_up=g1["w_up"], w_down=g1["w_down"], w_ple=g1["w_ple"], w_ple_gate=g1["w_ple_gate"]))
    dx1, dx1b, g0 = _mlp_ple_bwd(0, dx3, sv0, row("mlp_norm", 0), _after(row("ple_norm", 0), token),
                                 W["w_up"][0], W["w_down"][0], W["w_ple_gate"][0])
    token = send_mlp0(dict(w_up=g0["w_up"], w_down=g0["w_down"], w_ple=g0["w_ple"], w_ple_gate=g0["w_ple_gate"]))
    do_attn = _mm("attn_out_dx", dx1b, W["attn_w_o"], tb=True, epilogue=_add, extras=(_after(jnp.zeros((1, A_WIDTH), F32), token),))
    d_attn_w_o = _mm("attn_out_dw", o_attn, dx1b, ta=True, out_dtypes=(BF16,))
    dos, cs = _attn_merge_bwd(do_attn, os_, lses, bd1)
    grads9 = []
    for g in range(3):
        grads9 += list(_attn_bwd(qkvn[g], g, dos[g], lses[g], cs[g]))
    dqkv, dgains = _attn_prep_bwd(qkv, grads9, gains, ct, st, consts, bd)
    d_attn_w_qkv = _mm("attn_qkv_dw", h0, dqkv, ta=True, out_dtypes=(BF16,))
    token = send_attn(dict(attn_w_qkv=d_attn_w_qkv, attn_w_o=d_attn_w_o))
    dx0, d_mix0 = _mm("attn_qkv_dx", dqkv, W["attn_w_qkv"], tb=True, epilogue=_norm_bwd,
                      extras=(x, _after(row("mix_norm", 0), token), dx1), n_colsums=1, tm_pref=FUSED_ROWS)

    dg = jnp.stack([t.reshape(A_HEADS, A_HEAD_DIM).sum(0) for t in dgains])
    small = dict(
        mix_norm=jnp.concatenate([d_mix0, d_mix1], 0),
        attn_q_gain=dg[0::2][None], attn_k_gain=dg[1::2][None],
        dn_a_log=d_alog[:, :DN_HEADS], dn_dt_bias=d_dt[:, :DN_HEADS], dn_o_gain=d_ogain,
        mlp_norm=jnp.concatenate([g0["mlp_norm"], g1["mlp_norm"]], 0),
        ple_norm=jnp.concatenate([g0["ple_norm"], g1["ple_norm"]], 0),
    )
    return sq, dx0, small


MESH_IDS = pl.DeviceIdType.MESH
ANY = pl.BlockSpec(memory_space=pl.ANY)


def _place():
    return lax.axis_index("x"), lax.axis_index("y"), lax.axis_index("c")


def _sem_scratch(n_streams):
    return [pltpu.SemaphoreType.DMA((n_streams, N_DEV - 1)), pltpu.SemaphoreType.DMA((n_streams, N_DEV - 1)),
            pltpu.SemaphoreType.DMA((n_streams,))]


def _all_gather(name, arrays, streams):
    n_in, n_st = len(arrays), len(streams)
    shapes = [arrays[a].shape if li is None else arrays[a].shape[1:] for a, li in streams]

    def body(*refs):
        in_refs, out_refs, token = refs[:n_in], refs[n_in:n_in + n_st], refs[n_in + n_st]
        send_sems, recv_sems, local_sems = refs[n_in + n_st + 1:]
        token[...] = jnp.zeros_like(token)
        x, y, c = _place()
        me, sibling = (x, y, c), (x, y, 1 - c)
        chips = [(1 - x, y), (x, 1 - y), (1 - x, 1 - y)]

        def copy(s, k, block, to, own=False):
            a, li = streams[s]
            dst = out_refs[s].at[4 * block[0] + 2 * block[1] + block[2]]
            src = (in_refs[a] if li is None else in_refs[a].at[li]) if own else dst
            return pltpu.make_async_remote_copy(src_ref=src, dst_ref=dst, send_sem=send_sems.at[s, k],
                                                recv_sem=recv_sems.at[s, k], device_id=to, device_id_type=MESH_IDS)

        started = []
        for s, (a, li) in enumerate(streams):
            src = in_refs[a] if li is None else in_refs[a].at[li]
            mine = pltpu.make_async_copy(src, out_refs[s].at[4 * x + 2 * y + c], local_sems.at[s])
            mine.start()
            started.append(mine)
        sends = []
        for s in range(n_st):
            first = [copy(s, 0, me, sibling, own=True)]
            first += [copy(s, 1 + j, me, (*chip, c), own=True) for j, chip in enumerate(chips)]
            for cp in first:
                cp.start()
            sends += first
        for j, chip in enumerate(chips):
            for s in range(n_st):
                copy(s, 1 + j, (*chip, c), me).wait_recv()
                fwd = copy(s, 4 + j, (*chip, c), sibling)
                fwd.start()
                sends.append(fwd)
        for s in range(n_st):
            copy(s, 0, sibling, me).wait_recv()
            for j, chip in enumerate(chips):
                copy(s, 4 + j, (*chip, 1 - c), me).wait_recv()
        for cp in sends:
            cp.wait_send()
        for cp in started:
            cp.wait()

    res = pl.pallas_call(
        body, name=name,
        out_shape=[jax.ShapeDtypeStruct((N_DEV,) + tuple(sh), arrays[a].dtype) for sh, (a, _) in zip(shapes, streams)]
        + [jax.ShapeDtypeStruct((SUBLANE, LANE), F32)],
        in_specs=[ANY] * n_in, out_specs=[ANY] * n_st + [pl.BlockSpec(memory_space=pltpu.VMEM)],
        scratch_shapes=_sem_scratch(n_st),
    )(*arrays)
    return list(res[:n_st]), res[n_st]


HBM = pl.BlockSpec(memory_space=pltpu.HBM)
SEM = pl.BlockSpec(memory_space=pltpu.SEMAPHORE)
FLOWS = pltpu.CompilerParams(has_side_effects=pltpu.SideEffectType.DATAFLOW_SIDE_EFFECTING)


def _in_hbm(a):
    return pltpu.with_memory_space_constraint(a, pltpu.HBM)


def _hbm_like(a):
    return pltpu.HBM(a.shape, a.dtype)


def _peers(x, y, c):
    return [(1 - x if k & 4 else x, 1 - y if k & 2 else y, 1 - c if k & 1 else c) for k in range(1, N_DEV)]


def _start_copies(name, n_remote, n_own, make_copies, operands):
    n = len(operands)

    def body(*refs):
        for cp in make_copies(refs[:n], refs[n], refs[n + 1], refs[n + 2]):
            cp.start()
        refs[-1][...] = jnp.zeros_like(refs[-1])

    res = pl.pallas_call(
        body, name=name,
        out_shape=(pltpu.SemaphoreType.DMA((n_remote,)), pltpu.SemaphoreType.DMA((n_remote,)), pltpu.SemaphoreType.DMA((n_own,)),
                   *[_hbm_like(t) for t in operands], jax.ShapeDtypeStruct((SUBLANE, LANE), F32)),
        in_specs=[HBM] * n, out_specs=(SEM, SEM, SEM, *[HBM] * n, pl.BlockSpec(memory_space=pltpu.VMEM)),
        input_output_aliases={i: 3 + i for i in range(n)}, compiler_params=FLOWS,
    )(*[_in_hbm(t) for t in operands])
    return res[:3], list(res[3:3 + n]), res[-1]


def _wait_copies(name, make_waits, sems, operands, after):
    n = len(operands)

    def body(*refs):
        for wait in make_waits(refs[:n], refs[n], refs[n + 1], refs[n + 2]):
            wait()

    res = pl.pallas_call(
        body, name=name, out_shape=tuple(_hbm_like(t) for t in operands),
        in_specs=[HBM] * n + [SEM, SEM, SEM, ANY], out_specs=tuple([HBM] * n),
        input_output_aliases={i: i for i in range(n)}, compiler_params=FLOWS,
    )(*operands, *sems, after)
    return list(res)


def _gather_plan(n_in, streams):
    def block(arr, s):
        a, li = streams[s]
        return arr[a] if li is None else arr[a].at[li]

    def copies(refs, send_sems, recv_sems, own_sems, arrivals=False):
        arr, land = refs[:n_in], refs[n_in:]
        x, y, c = _place()
        me = 4 * x + 2 * y + c
        out = []
        for s in range(len(streams)):
            out.append(("own", pltpu.make_async_copy(block(arr, s), land[s].at[me], own_sems.at[s])))
            for k, (px, py, pc) in enumerate(_peers(x, y, c)):
                out.append(("remote", pltpu.make_async_remote_copy(
                    src_ref=block(arr, s), dst_ref=land[s].at[4 * px + 2 * py + pc if arrivals else me],
                    send_sem=send_sems.at[s * (N_DEV - 1) + k], recv_sem=recv_sems.at[s * (N_DEV - 1) + k],
                    device_id=(px, py, pc), device_id_type=MESH_IDS)))
        return out
    return copies


def _exchange_plan(n_st):
    def copies(refs, send_sems, recv_sems, own_sems, arrivals=False):
        snd, rcv = refs[:n_st], refs[n_st:]
        x, y, c = _place()
        me = 4 * x + 2 * y + c
        out = []
        for s in range(n_st):
            out.append(("own", pltpu.make_async_copy(snd[s].at[me], rcv[s].at[me], own_sems.at[s])))
            for k, (px, py, pc) in enumerate(_peers(x, y, c)):
                peer = 4 * px + 2 * py + pc
                out.append(("remote", pltpu.make_async_remote_copy(
                    src_ref=snd[s].at[peer], dst_ref=rcv[s].at[peer if arrivals else me],
                    send_sem=send_sems.at[s * (N_DEV - 1) + k], recv_sem=recv_sems.at[s * (N_DEV - 1) + k],
                    device_id=(px, py, pc), device_id_type=MESH_IDS)))
        return out
    return copies


def _split_transfer(tag, plan, n_streams, operands):
    sems, operands, token = _start_copies(f"{tag}_start", n_streams * (N_DEV - 1), n_streams,
                                          lambda refs, a, b, o: [cp for _, cp in plan(refs, a, b, o)], operands)

    def waits(refs, a, b, o):
        out = []
        for kind, cp in plan(refs, a, b, o, arrivals=True):
            out += [cp.wait] if kind == "own" else [cp.wait_send, cp.wait_recv]
        return out

    return (lambda after: _wait_copies(f"{tag}_wait", waits, sems, operands, after)), token


def _gather_async(tag, arrays, streams):
    lands = [lax.empty((N_DEV,) + tuple(arrays[a].shape if li is None else arrays[a].shape[1:]), arrays[a].dtype)
             for a, li in streams]
    finish, token = _split_transfer(tag, _gather_plan(len(arrays), streams), len(streams), list(arrays) + lands)
    return (lambda after: finish(after)[len(arrays):]), token


def _exchange_async(tag, sends):
    recvs = [lax.empty(t.shape, t.dtype) for t in sends]
    finish, token = _split_transfer(tag, _exchange_plan(len(sends)), len(sends), list(sends) + recvs)
    return (lambda after: finish(after)[len(sends):]), token


def _dn_in_pieces():
    n = (DN_QKVZ + 2 * DN_HEADS) // N_DEV
    segs = ((0, DN_QKV, 0, 0), (DN_QKV, DN_QKV + 2 * DN_HEADS, 1, 0), (DN_QKV + 2 * DN_HEADS, DN_QKVZ + 2 * DN_HEADS, 0, DN_QKV))
    out = []
    for d in range(N_DEV):
        lo, hi = d * n, (d + 1) * n
        for s0, s1, tgt, t0 in segs:
            a, b = max(lo, s0), min(hi, s1)
            if a < b:
                out.append((d, a - lo, b - lo, tgt, t0 + a - s0))
    return out


def _unpack_cols(name, g):
    _, K, n = g.shape
    tr = 256

    def body(g_ref, o_ref):
        for d in range(N_DEV):
            o_ref[:, d * n:(d + 1) * n] = g_ref[d]

    return pl.pallas_call(
        body, name=name, grid=(K // tr,), in_specs=[pl.BlockSpec((N_DEV, tr, n), lambda i: (0, i, 0))],
        out_specs=pl.BlockSpec((tr, N_DEV * n), lambda i: (i, 0)),
        out_shape=jax.ShapeDtypeStruct((K, N_DEV * n), g.dtype), compiler_params=_cparams(("parallel",)),
    )(g)


def _pack_cols(name, w):
    K, n = w.shape[0], w.shape[1] // N_DEV
    tr = 256

    def body(w_ref, o_ref):
        for d in range(N_DEV):
            o_ref[d] = w_ref[:, d * n:(d + 1) * n]

    return pl.pallas_call(
        body, name=name, grid=(K // tr,), in_specs=[pl.BlockSpec((tr, N_DEV * n), lambda i: (i, 0))],
        out_specs=pl.BlockSpec((N_DEV, tr, n), lambda i: (0, i, 0)),
        out_shape=jax.ShapeDtypeStruct((N_DEV, K, n), w.dtype), compiler_params=_cparams(("parallel",)),
    )(w)


def _unpack_dn_in(g):
    _, K, n = g.shape
    tr = 256

    def body(g_ref, qkvz_ref, ab_ref):
        ab_ref[...] = jnp.zeros_like(ab_ref)
        for d, c0, c1, tgt, t0 in _dn_in_pieces():
            (qkvz_ref, ab_ref)[tgt][:, t0:t0 + c1 - c0] = g_ref[d, :, c0:c1]

    return pl.pallas_call(
        body, name="unpack_dn_in", grid=(K // tr,), in_specs=[pl.BlockSpec((N_DEV, tr, n), lambda i: (0, i, 0))],
        out_specs=[pl.BlockSpec((tr, DN_QKVZ), lambda i: (i, 0)), pl.BlockSpec((tr, LANE), lambda i: (i, 0))],
        out_shape=[jax.ShapeDtypeStruct((K, DN_QKVZ), g.dtype), jax.ShapeDtypeStruct((K, LANE), g.dtype)],
        compiler_params=_cparams(("parallel",)),
    )(g)


def _pack_dn_in(d_qkvz, d_ab):
    K = d_qkvz.shape[0]
    n = (DN_QKVZ + 2 * DN_HEADS) // N_DEV
    tr = 256

    def body(qkvz_ref, ab_ref, o_ref):
        for d, c0, c1, tgt, t0 in _dn_in_pieces():
            o_ref[d, :, c0:c1] = (qkvz_ref, ab_ref)[tgt][:, t0:t0 + c1 - c0]

    return pl.pallas_call(
        body, name="pack_dn_in", grid=(K // tr,),
        in_specs=[pl.BlockSpec((tr, DN_QKVZ), lambda i: (i, 0)), pl.BlockSpec((tr, LANE), lambda i: (i, 0))],
        out_specs=pl.BlockSpec((N_DEV, tr, n), lambda i: (0, i, 0)),
        out_shape=jax.ShapeDtypeStruct((N_DEV, K, n), d_qkvz.dtype), compiler_params=_cparams(("parallel",)),
    )(d_qkvz, d_ab)


ADAMW_ROWS = 256


def _adamw(name, parts, w, m, v):
    R, C = w.shape
    tr = min(R, ADAMW_ROWS)
    assert R % tr == 0 and parts.shape == (N_DEV, R, C)
    c1 = 1.0 - B1 ** STEP
    c2 = 1.0 - B2 ** STEP

    def body(p_ref, w_ref, m_ref, v_ref, g_ref, d_ref, nm_ref, nv_ref):
        g = p_ref[0].astype(F32)
        for dev in range(1, N_DEV):
            g = g + p_ref[dev].astype(F32)
        nm = B1 * m_ref[...] + (1.0 - B1) * g
        nv = B2 * v_ref[...] + (1.0 - B2) * jnp.square(g)
        g_ref[...] = g
        nm_ref[...] = nm
        nv_ref[...] = nv
        d_ref[...] = -LR * ((nm / c1) / (jnp.sqrt(nv / c2) + ADAM_EPS) + WD * w_ref[...])

    blk = pl.BlockSpec((tr, C), lambda i: (i, 0))
    return pl.pallas_call(
        body, name=name, grid=(R // tr,),
        in_specs=[pl.BlockSpec((N_DEV, tr, C), lambda i: (0, i, 0)), blk, blk, blk],
        out_specs=[blk] * 4, out_shape=[jax.ShapeDtypeStruct((R, C), F32)] * 4,
        compiler_params=_cparams(("parallel",)),
    )(parts, w, m, v)


SMALL = ("mix_norm", "attn_q_gain", "attn_k_gain", "dn_a_log", "dn_dt_bias", "dn_o_gain", "mlp_norm", "ple_norm")
WEIGHTS = ("mix_norm", "attn_w_qkv", "attn_q_gain", "attn_k_gain", "attn_w_o", "dn_w_in", "dn_conv", "dn_a_log",
           "dn_dt_bias", "dn_o_gain", "dn_w_o", "mlp_norm", "w_up", "w_down", "ple_norm", "w_ple", "w_ple_gate")


def _to_rows(flat, multiple):
    n = flat.shape[-1]
    rows = -(-n // (LANE * multiple)) * multiple
    return jnp.pad(flat, [(0, rows * LANE - n)]).reshape(rows, LANE)


def _cols_to_devices(w):
    K, N = w.shape
    return jnp.transpose(w.reshape(K, N_DEV, N // N_DEV), (1, 0, 2))


def _cols_from_devices(g):
    _, K, n = g.shape
    return jnp.transpose(g, (1, 0, 2)).reshape(K, N_DEV * n)


SMALL_ROWS = 96


def _pack_small(vals, loss_rows):
    rows = [_to_rows(vals[n].reshape(-1), SUBLANE) for n in SMALL] + [loss_rows]
    buf = jnp.concatenate(rows, 0)
    assert buf.shape == (SMALL_ROWS, LANE)
    return buf


def _unpack_small(buf, like):
    out, r = {}, 0
    for n in SMALL:
        sz = math.prod(like[n].shape)
        out[n] = buf[r:r + -(-sz // LANE)].reshape(-1)[:sz].reshape(like[n].shape)
        r += -(-sz // (LANE * SUBLANE)) * SUBLANE
    return out


def kernel(x, p, positions, mix_norm, attn_w_qkv, attn_q_gain, attn_k_gain, attn_w_o, dn_w_in, dn_conv, dn_a_log, dn_dt_bias, dn_o_gain, dn_w_o, mlp_norm, w_up, w_down, ple_norm, w_ple, w_ple_gate, loss_target, m_mix_norm, m_attn_w_qkv, m_attn_q_gain, m_attn_k_gain, m_attn_w_o, m_dn_w_in, m_dn_conv, m_dn_a_log, m_dn_dt_bias, m_dn_o_gain, m_dn_w_o, m_mlp_norm, m_w_up, m_w_down, m_ple_norm, m_w_ple, m_w_ple_gate, v_mix_norm, v_attn_w_qkv, v_attn_q_gain, v_attn_k_gain, v_attn_w_o, v_dn_w_in, v_dn_conv, v_dn_a_log, v_dn_dt_bias, v_dn_o_gain, v_dn_w_o, v_mlp_norm, v_w_up, v_w_down, v_ple_norm, v_w_ple, v_w_ple_gate):
    w = dict(mix_norm=mix_norm, attn_w_qkv=attn_w_qkv, attn_q_gain=attn_q_gain, attn_k_gain=attn_k_gain, attn_w_o=attn_w_o,
             dn_w_in=dn_w_in, dn_conv=dn_conv, dn_a_log=dn_a_log, dn_dt_bias=dn_dt_bias, dn_o_gain=dn_o_gain, dn_w_o=dn_w_o,
             mlp_norm=mlp_norm, w_up=w_up, w_down=w_down, ple_norm=ple_norm, w_ple=w_ple, w_ple_gate=w_ple_gate)
    m = dict(mix_norm=m_mix_norm, attn_w_qkv=m_attn_w_qkv, attn_q_gain=m_attn_q_gain, attn_k_gain=m_attn_k_gain,
             attn_w_o=m_attn_w_o, dn_w_in=m_dn_w_in, dn_conv=m_dn_conv, dn_a_log=m_dn_a_log, dn_dt_bias=m_dn_dt_bias,
             dn_o_gain=m_dn_o_gain, dn_w_o=m_dn_w_o, mlp_norm=m_mlp_norm, w_up=m_w_up, w_down=m_w_down,
             ple_norm=m_ple_norm, w_ple=m_w_ple, w_ple_gate=m_w_ple_gate)
    v = dict(mix_norm=v_mix_norm, attn_w_qkv=v_attn_w_qkv, attn_q_gain=v_attn_q_gain, attn_k_gain=v_attn_k_gain,
             attn_w_o=v_attn_w_o, dn_w_in=v_dn_w_in, dn_conv=v_dn_conv, dn_a_log=v_dn_a_log, dn_dt_bias=v_dn_dt_bias,
             dn_o_gain=v_dn_o_gain, dn_w_o=v_dn_w_o, mlp_norm=v_mlp_norm, w_up=v_w_up, w_down=v_w_down,
             ple_norm=v_ple_norm, w_ple=v_w_ple, w_ple_gate=v_w_ple_gate)
    S = x.shape[1]

    bf = lambda a: a.astype(BF16)
    rows_to_devices = lambda t: t.reshape(N_DEV, t.shape[0] // N_DEV, t.shape[1])

    (g_qkv, g_ao), token = _all_gather("gather_attn", [bf(attn_w_qkv[0]), bf(attn_w_o[0])], [(0, None), (1, None)])
    rest_shards = [bf(dn_w_in[0]), bf(dn_w_o[0]), bf(w_up), bf(w_down), bf(w_ple), bf(w_ple_gate), _after(dn_conv[0], token)]
    rest_streams = [(0, None), (1, None), (2, 0), (2, 1), (3, 0), (3, 1), (4, 0), (4, 1), (5, 0), (5, 1), (6, None)]
    rest_arrived, token = _gather_async("gather_rest", rest_shards, rest_streams)
    W = dict(attn_w_qkv=_unpack_cols("unpack_attn_qkv", g_qkv), attn_w_o=_cols_from_devices(g_ao))

    def rest_of_weights(after):
        g_in, g_do, g_up0, g_up1, g_dn0, g_dn1, g_pl0, g_pl1, g_gt0, g_gt1, g_conv = rest_arrived(after)
        rest = dict(
            dn_conv=jnp.transpose(g_conv, (1, 0, 2)).reshape(CONV_W, DN_QKV), dn_w_o=g_do.reshape(DN_WIDTH, D_MODEL),
            w_up=[_cols_from_devices(g_up0), _cols_from_devices(g_up1)],
            w_down=[g_dn0.reshape(D_FF, D_MODEL), g_dn1.reshape(D_FF, D_MODEL)],
            w_ple=[_cols_from_devices(g_pl0), _cols_from_devices(g_pl1)],
            w_ple_gate=[g_gt0.reshape(D_MODEL, D_MODEL), g_gt1.reshape(D_MODEL, D_MODEL)])
        rest["dn_w_qkvz"], rest["dn_w_ab"] = _unpack_dn_in(g_in)
        return rest

    pending = {}

    def mlp_sends(g):
        return [_cols_to_devices(g["w_up"]), rows_to_devices(g["w_down"]), _cols_to_devices(g["w_ple"]),
                rows_to_devices(g["w_ple_gate"])]

    def start(tag, sends):
        pending[tag], token = _exchange_async(f"exchange_{tag}", sends)
        return token

    def send_layer1(g):
        conv_send = jnp.transpose(g["dn_conv"].reshape(CONV_W, N_DEV, DN_QKV // N_DEV), (1, 0, 2))
        return start("layer1", [_pack_dn_in(g["dn_w_qkvz"], g["dn_w_ab"]), conv_send, rows_to_devices(g["dn_w_o"])] + mlp_sends(g))

    def send_mlp0(g):
        return start("mlp0", mlp_sends(g))

    def send_attn(g):
        return start("attn", [_pack_cols("pack_attn_qkv", g["attn_w_qkv"]), _cols_to_devices(g["attn_w_o"])])

    P = dict(mix_norm=_after(mix_norm, token), attn_q_gain=attn_q_gain[0], attn_k_gain=attn_k_gain[0], dn_a_log=dn_a_log[0],
             dn_dt_bias=dn_dt_bias[0], dn_o_gain=dn_o_gain[0], mlp_norm=mlp_norm, ple_norm=ple_norm)

    sq, dx0, small_g = _local_step(x[0], p[:, 0], positions.reshape(S, 1), loss_target[0], W, P,
                                   rest_of_weights, send_layer1, send_mlp0, send_attn)

    r_in, r_conv, r_do, r_up1, r_dn1, r_pl1, r_gt1 = pending["layer1"](dx0)
    r_up0, r_dn0, r_pl0, r_gt0 = pending["mlp0"](dx0)
    r_qkv, r_ao = pending["attn"](dx0)
    big = {}
    for n, parts in (("attn_w_qkv", [r_qkv]), ("attn_w_o", [r_ao]), ("dn_w_in", [r_in]), ("dn_conv", [r_conv]),
                     ("dn_w_o", [r_do]), ("w_up", [r_up0, r_up1]), ("w_down", [r_dn0, r_dn1]),
                     ("w_ple", [r_pl0, r_pl1]), ("w_ple_gate", [r_gt0, r_gt1])):
        layers = [_adamw(f"adamw_{n}{l}", pt, w[n][l], m[n][l], v[n][l]) for l, pt in enumerate(parts)]
        big[n] = [jnp.stack([res[k] for res in layers]) for k in range(4)]

    loss_rows = jnp.pad((0.5 / D_MODEL) * jnp.sum(sq, axis=1, keepdims=True), ((0, SUBLANE - 1), (0, LANE - 1)))
    small_like = {n: w[n] for n in SMALL}
    parts_s = _all_gather("gather_small", [_pack_small(small_g, loss_rows)], [(0, None)])[0][0]
    zero_rows = jnp.zeros((SUBLANE, LANE), F32)
    small = _adamw("adamw_small", parts_s, _pack_small(w, zero_rows), _pack_small(m, zero_rows), _pack_small(v, zero_rows))
    loss = small[0][SMALL_ROWS - SUBLANE, 0]
    small = [_unpack_small(b, small_like) for b in small]

    outs = [loss, dx0[None]]
    for k in range(4):
        for n in WEIGHTS:
            outs.append(small[k][n] if n in SMALL else big[n][k])
    return tuple(outs)
```

```python
import functools
import math

import jax
import jax.numpy as jnp
from jax import lax
from jax.experimental import pallas as pl
from jax.experimental.pallas import tpu as pltpu

F32 = jnp.float32
BF16 = jnp.bfloat16
HIGHEST = lax.Precision.HIGHEST

N_DEV = 8
D_MODEL = 1024
EPS = 1e-6
SWA_GROUPS = ((128, 1), (512, 4), (2048, 16))
A_HEADS = 8
A_HEAD_DIM = 64
A_WIDTH = A_HEADS * A_HEAD_DIM
A_QKV = 3 * 3 * A_WIDTH
ROPE_DIM = 16
ROPE_HALF = 8
ROPE_THETA = 500000.0
BAND = 128
DN_HEADS = 8
DN_DIM = 128
DN_WIDTH = DN_HEADS * DN_DIM
CONV_W = 4
CHUNK = 64
D_FF = 4 * D_MODEL
PLE_DIM = 256
LR, B1, B2, ADAM_EPS, WD, STEP = 0.001, 0.9, 0.999, 1e-08, 0.01, 10

VMEM_LIMIT = 56 * 1024 * 1024
MXU_TILE = 1024
MM_SLAB = 256
LANE = 128
SUBLANE = 8


def _cparams(sem):
    return pltpu.CompilerParams(dimension_semantics=sem, vmem_limit_bytes=VMEM_LIMIT)


def _tile(n, pref):
    if n <= pref:
        return n
    t = (pref // LANE) * LANE
    while t >= LANE:
        if n % t == 0:
            return t
        t -= LANE
    raise ValueError(f"no tile for {n}")


def _dot(a, b, ca=1, cb=0, precision=None):
    return lax.dot_general(a, b, (((ca,), (cb,)), ((), ())), precision=precision,
                           preferred_element_type=F32)


def _bdot(a, b, ca=1, cb=0):
    return _dot(a.astype(BF16), b.astype(BF16), ca, cb)


def _mm(name, a, b, *, ta=False, tb=False, epilogue=None, extras=(), out_dtypes=(F32,), n_colsums=0,
        tm_pref=MXU_TILE, tn_pref=1536, tk_pref=MXU_TILE):
    M, K = (a.shape[1], a.shape[0]) if ta else a.shape
    N = b.shape[0] if tb else b.shape[1]
    assert (b.shape[1] if tb else b.shape[0]) == K
    tm, tn, tk = _tile(M, tm_pref), _tile(N, tn_pref), _tile(K, tk_pref)
    nk = K // tk
    n_out = len(out_dtypes)
    n_ext = len(extras)
    assert n_colsums == 0 or tn == N
    sub = min(tm, MM_SLAB)

    def body(*refs):
        a_ref, b_ref = refs[0], refs[1]
        ext = refs[2:2 + n_ext]
        outs = refs[2 + n_ext:2 + n_ext + n_out]
        sums = refs[2 + n_ext + n_out:2 + n_ext + n_out + n_colsums]
        row_tile, k = pl.program_id(0), pl.program_id(2)
        slabs = [slice(s * sub, (s + 1) * sub) for s in range(tm // sub)]

        def product(rows):
            return _bdot(a_ref[:, rows] if ta else a_ref[rows, :], b_ref[...], 0 if ta else 1, 1 if tb else 0)

        def finish(results):
            col_rows = []
            for rows, r in zip(slabs, results):
                res = (r,) if epilogue is None else epilogue(r, *[e[...] if e.shape[0] == 1 else e[rows, :] for e in ext])
                for o, v in zip(outs, res):
                    o[rows, :] = v.astype(o.dtype)
                col_rows.append(res[n_out:])
            for n, o in enumerate(sums):
                v = functools.reduce(lambda x, y: x + y, [c[n] for c in col_rows])

                @pl.when(row_tile == 0)
                def _(o=o, v=v):
                    o[...] = v

                @pl.when(row_tile > 0)
                def _(o=o, v=v):
                    o[...] += v

        if nk == 1:
            finish([product(rows) for rows in slabs])
            return
        acc = refs[-1]

        @pl.when(k == 0)
        def _():
            acc[...] = jnp.zeros_like(acc)

        for rows in slabs:
            acc[rows, :] += product(rows)

        @pl.when(k == nk - 1)
        def _():
            finish([acc[rows, :] for rows in slabs])

    a_spec = pl.BlockSpec((tk, tm), lambda i, j, k: (k, i)) if ta else pl.BlockSpec((tm, tk), lambda i, j, k: (i, k))
    b_spec = pl.BlockSpec((tn, tk), lambda i, j, k: (j, k)) if tb else pl.BlockSpec((tk, tn), lambda i, j, k: (k, j))
    ext_specs = []
    for e in extras:
        if e.shape[0] == 1 and M != 1:
            ext_specs.append(pl.BlockSpec((1, tn), lambda i, j, k: (0, j)))
        else:
            ext_specs.append(pl.BlockSpec((tm, tn), lambda i, j, k: (i, j)))
    out = pl.pallas_call(
        body, name=name,
        grid=(M // tm, N // tn, nk),
        in_specs=[a_spec, b_spec] + ext_specs,
        out_specs=[pl.BlockSpec((tm, tn), lambda i, j, k: (i, j)) for _ in range(n_out)]
        + [pl.BlockSpec((1, tn), lambda i, j, k: (0, 0)) for _ in range(n_colsums)],
        out_shape=[jax.ShapeDtypeStruct((M, N), dt) for dt in out_dtypes]
        + [jax.ShapeDtypeStruct((1, N), F32) for _ in range(n_colsums)],
        scratch_shapes=[pltpu.VMEM((tm, tn), F32)] if nk > 1 else [],
        compiler_params=_cparams(("arbitrary" if n_colsums else "parallel", "parallel", "arbitrary")),
    )(a, b, *extras)
    return out[0] if len(out) == 1 else tuple(out)


def _perm_matrices(tr, d):
    import numpy as np
    old = np.arange(tr)
    p = np.zeros((tr, tr), np.float32)
    p[(old % d) * (tr // d) + old // d, old] = 1.0
    return jnp.asarray(p, BF16), jnp.asarray(p.T, BF16)


def _permute(p, x):
    if x.dtype == BF16:
        return _dot(p, x)
    hi = x.astype(BF16)
    rest = x - hi.astype(F32)
    mid = rest.astype(BF16)
    lo = (rest - mid.astype(F32)).astype(BF16)
    return _dot(p, hi) + _dot(p, mid) + _dot(p, lo)


def _rows(name, fn, ins, outs, *, tr, accs=()):
    ins = [(e[0], e[1]) + (e[2] if len(e) > 2 else (0, e[0].shape[-1])) for e in ins]
    outs = [tuple(o) + (0,) * (3 - len(o)) for o in outs]
    n_rows = next(e[0].shape[0] if e[1] == "row" else e[0].shape[0] * e[0].shape[1] for e in ins if e[1] in ("row", "res"))
    assert n_rows % tr == 0 and tr % SUBLANE == 0
    steps = n_rows // tr
    t8 = tr // SUBLANE
    n8 = n_rows // SUBLANE
    dils = sorted({e[0].shape[0] for e in ins if e[1] == "res" and e[0].shape[0] > 1} | {o[2] for o in outs if o[2] > 1})
    perms = [m for d in dils for m in _perm_matrices(tr, d)]
    ins = ins + [(m, "full", 0, tr) for m in perms]
    n_in, n_out, n_acc = len(ins), len(outs), len(accs)

    def body(*refs):
        i = pl.program_id(0)
        to_res = {d: refs[n_in - len(perms) + 2 * j][...] for j, d in enumerate(dils)}
        to_tok = {d: refs[n_in - len(perms) + 2 * j + 1][...] for j, d in enumerate(dils)}
        tiles = []
        for r, e in zip(refs[:n_in - len(perms)], ins):
            d = e[0].shape[0] if e[1] == "res" else 0
            if d == 0:
                tiles.append(r[...])
            elif d == 1:
                tiles.append(r[0])
            else:
                tiles.append(_permute(to_tok[d], jnp.concatenate([r[j] for j in range(d)], axis=0)))
        vals = fn(i, steps, *tiles)
        if not isinstance(vals, (tuple, list)):
            vals = (vals,)
        assert len(vals) == n_out + n_acc
        for o, v, (_, dt, d) in zip(refs[n_in:n_in + n_out], vals[:n_out], outs):
            if d == 0:
                o[...] = v.astype(o.dtype)
            elif d == 1:
                o[0] = v.astype(o.dtype)
            else:
                y = _permute(to_res[d], v.astype(dt))
                for j in range(d):
                    o[j] = y[j * (tr // d):(j + 1) * (tr // d)].astype(o.dtype)
        if n_acc:
            acc_refs = refs[n_in + n_out:]

            @pl.when(i == 0)
            def _():
                for r in acc_refs:
                    r[...] = jnp.zeros_like(r)

            for r, v in zip(acc_refs, vals[n_out:]):
                r[...] += v.astype(r.dtype)

    in_specs = []
    for a, kind, cb, c in ins:
        if kind == "row":
            in_specs.append(pl.BlockSpec((tr, c), lambda i, cb=cb: (i, cb)))
        elif kind == "full":
            in_specs.append(pl.BlockSpec(a.shape, lambda i, z=(0,) * a.ndim: z))
        elif kind == "prev8":
            in_specs.append(pl.BlockSpec((SUBLANE, c), lambda i, cb=cb: (jnp.maximum(i * t8 - 1, 0), cb)))
        elif kind == "next8":
            in_specs.append(pl.BlockSpec((SUBLANE, c), lambda i, cb=cb: (jnp.minimum((i + 1) * t8, n8 - 1), cb)))
        elif kind == "res":
            d = a.shape[0]
            in_specs.append(pl.BlockSpec((d, tr // d, a.shape[2]), lambda i: (0, i, 0)))
        else:
            raise ValueError(kind)
    out_specs = [pl.BlockSpec((tr, c), lambda i: (i, 0)) if d == 0 else pl.BlockSpec((d, tr // d, c), lambda i: (0, i, 0))
                 for c, _, d in outs]
    out_specs += [pl.BlockSpec(s, lambda i, z=(0,) * len(s): z) for s, _ in accs]
    out_shape = [jax.ShapeDtypeStruct((n_rows, c) if d == 0 else (d, n_rows // d, c), dt) for c, dt, d in outs]
    out_shape += [jax.ShapeDtypeStruct(s, dt) for s, dt in accs]
    res = pl.pallas_call(
        body, name=name, grid=(steps,), in_specs=in_specs, out_specs=out_specs, out_shape=out_shape,
        compiler_params=_cparams(("arbitrary",) if n_acc else ("parallel",)),
    )(*[e[0] for e in ins])
    return res[0] if len(res) == 1 else tuple(res)


def _colsum(x):
    return jnp.sum(x, axis=0, keepdims=True)


def _sum_all(x):
    return jnp.sum(jnp.sum(x, axis=1, keepdims=True), axis=0, keepdims=True)


def _rmsnorm_fwd(name, x, gain):
    def fn(i, n, xt, g):
        r = lax.rsqrt(jnp.mean(xt * xt, axis=-1, keepdims=True) + EPS)
        return (xt * r * g,)
    return _rows(name, fn, [(x, "row"), (gain, "full")], [(x.shape[1], BF16)], tr=512)


FUSED_ROWS = 1024


def _res_norm(acc, res, g):
    x = res + acc
    return x, x * lax.rsqrt(jnp.mean(x * x, axis=-1, keepdims=True) + EPS) * g


def _norm_bwd(dh, x, g, dres):
    r = lax.rsqrt(jnp.mean(x * x, axis=-1, keepdims=True) + EPS)
    xh = x * r
    dxn = dh * g
    dx = dres + r * (dxn - xh * jnp.mean(dxn * xh, axis=-1, keepdims=True))
    return dx, _colsum(dh * xh)


def _norm_bwd_2(dh, x, g, dres):
    dx, dg = _norm_bwd(dh, x, g, dres)
    return dx, dx, dg


def _head_consts():
    import numpy as np
    e = np.arange(A_WIDTH) % A_HEAD_DIM
    inv = (np.float32(ROPE_THETA) ** (-np.arange(0, ROPE_DIM, 2, dtype=np.float32) / np.float32(ROPE_DIM))).astype(np.float32)
    c = np.zeros((8, A_WIDTH), np.float32)
    c[0] = np.where(e < ROPE_DIM, inv[e % ROPE_HALF], 0.0)
    c[1] = np.where(e < ROPE_HALF, -1.0, np.where(e < ROPE_DIM, 1.0, 0.0))
    c[2] = (e < ROPE_HALF).astype(np.float32)
    c[3] = (e < ROPE_DIM).astype(np.float32)
    return jnp.asarray(c)


def _block_diag(scale):
    import numpy as np
    h = np.arange(A_WIDTH) // A_HEAD_DIM
    return jnp.asarray((h[:, None] == h[None, :]).astype(np.float32) * scale, dtype=BF16)


def _seg_sum(x, bd):
    hi = x.astype(BF16)
    lo = (x - hi.astype(F32)).astype(BF16)
    return _dot(hi, bd) + _dot(lo, bd)


def _rope_tables(positions, consts):
    def fn(i, n, pos, c):
        ang = pos.astype(F32) * c[0:1, :LANE]
        return jnp.cos(ang), jnp.sin(ang) * c[1:2, :LANE]
    return _rows("rope_tables", fn, [(positions, "row"), (consts, "full")], [(LANE, F32), (LANE, F32)], tr=512)


def _rope_wide(t):
    return jnp.concatenate([t] * (A_WIDTH // LANE), axis=1)


def _rope_apply(y, ct, st, low):
    rolled = jnp.where(low, pltpu.roll(y, A_WIDTH - ROPE_HALF, 1), pltpu.roll(y, ROPE_HALF, 1))
    return y * ct + rolled * st


def _rope_apply_bwd(dout, ct, st, low, in16):
    t = dout * st
    back = jnp.where(low, pltpu.roll(t, A_WIDTH - ROPE_HALF, 1), jnp.where(in16, pltpu.roll(t, ROPE_HALF, 1), 0.0))
    return dout * ct + back


def _attn_prep(qkv, gains, ct, st, consts, bd):
    def fn(i, n, t, g, c_t, s_t, c, b):
        low = c[2:3, :] > 0.5
        c_t, s_t = _rope_wide(c_t), _rope_wide(s_t)
        groups = []
        for grp in range(3):
            cols = []
            for which in range(3):
                off = (grp * 3 + which) * A_WIDTH
                x = t[:, off:off + A_WIDTH].astype(F32)
                if which == 2:
                    cols.append(x.astype(BF16))
                    continue
                r = lax.rsqrt(_seg_sum(x * x, b) + EPS)
                y = x * r * g[grp * 2 + which:grp * 2 + which + 1, :]
                cols.append(_rope_apply(y, c_t, s_t, low).astype(BF16))
            groups.append(jnp.concatenate(cols, axis=1))
        return tuple(groups)
    return _rows("attn_prep", fn, [(qkv, "row"), (gains, "full"), (ct, "row"), (st, "row"), (consts, "full"), (bd, "full")],
                 [(3 * A_WIDTH, BF16, d) for _, d in SWA_GROUPS], tr=256)


def _band_mask(n):
    row = lax.broadcasted_iota(jnp.int32, (BAND, 2 * BAND), 0)
    col = lax.broadcasted_iota(jnp.int32, (BAND, 2 * BAND), 1)
    dist = row + BAND - col
    return (dist >= 0) & (dist <= BAND) & ((col >= BAND) | (n > 0))


def _attn_fwd(qkvn, grp):
    d, L, _ = qkvn.shape
    nblk = L // BAND
    assert L % BAND == 0 and d == SWA_GROUPS[grp][1]

    def body(q_ref, kc_ref, kp_ref, vc_ref, vp_ref, o_ref, lse_ref):
        n = pl.program_id(1)
        valid = _band_mask(n)
        first = lax.broadcasted_iota(jnp.int32, (BAND, LANE), 1) < A_HEAD_DIM
        pairs = [slice(pr * LANE, (pr + 1) * LANE) for pr in range(A_WIDTH // LANE)]
        halves = (first, jnp.logical_not(first))
        qps = [q_ref[:, sl] for sl in pairs]
        kcats = [jnp.concatenate([kp_ref[:, sl], kc_ref[:, sl]], axis=0) for sl in pairs]
        vcats = [jnp.concatenate([vp_ref[:, sl], vc_ref[:, sl]], axis=0) for sl in pairs]
        heads = [(pr, m) for pr in range(len(pairs)) for m in halves]
        ss = [_dot(jnp.where(m, qps[pr], jnp.zeros_like(qps[pr])), kcats[pr], 1, 1) for pr, m in heads]
        ps, lses = [], []
        for s in ss:
            s = jnp.where(valid, s * (A_HEAD_DIM ** -0.5), -1e30)
            mx = jnp.max(s, axis=-1, keepdims=True)
            e = jnp.exp(s - mx)
            l = jnp.sum(e, axis=-1, keepdims=True)
            ps.append((e / l).astype(BF16))
            lses.append(mx + jnp.log(l))
        os_ = [_dot(p, vcats[pr]) for p, (pr, _) in zip(ps, heads)]
        o_ref[...] = jnp.concatenate([jnp.where(first, os_[2 * pr], os_[2 * pr + 1]) for pr in range(len(pairs))], axis=1)
        lse_ref[...] = jnp.concatenate([jnp.where(first, lses[2 * pr], lses[2 * pr + 1]) for pr in range(len(pairs))], axis=1)

    blk = (None, BAND, A_WIDTH)
    return pl.pallas_call(
        body, name=f"attn_fwd_g{grp}", grid=(d, nblk),
        in_specs=[pl.BlockSpec(blk, lambda r, n: (r, n, 0)),
                  pl.BlockSpec(blk, lambda r, n: (r, n, 1)),
                  pl.BlockSpec(blk, lambda r, n: (r, jnp.maximum(n - 1, 0), 1)),
                  pl.BlockSpec(blk, lambda r, n: (r, n, 2)),
                  pl.BlockSpec(blk, lambda r, n: (r, jnp.maximum(n - 1, 0), 2))],
        out_specs=[pl.BlockSpec(blk, lambda r, n: (r, n, 0)), pl.BlockSpec(blk, lambda r, n: (r, n, 0))],
        out_shape=[jax.ShapeDtypeStruct((d, L, A_WIDTH), F32)] * 2,
        compiler_params=_cparams(("parallel", "parallel")),
    )(qkvn, qkvn, qkvn, qkvn, qkvn)


def _merge_weights(l0, l1, l2):
    mx = jnp.maximum(jnp.maximum(l0, l1), l2)
    e0, e1, e2 = jnp.exp(l0 - mx), jnp.exp(l1 - mx), jnp.exp(l2 - mx)
    inv = 1.0 / (e0 + e1 + e2)
    return e0 * inv, e1 * inv, e2 * inv


def _attn_merge(os_, lses):
    def fn(i, n, o0, o1, o2, l0, l1, l2):
        w0, w1, w2 = _merge_weights(l0, l1, l2)
        return (w0 * o0 + w1 * o1 + w2 * o2,)
    ins = [(a, "res") for a in (*os_, *lses)]
    return _rows("attn_merge", fn, ins, [(A_WIDTH, BF16)], tr=256)


def _attn_merge_bwd(do, os_, lses, bd1):
    def fn(i, n, dot_, o0, o1, o2, l0, l1, l2, b):
        w0, w1, w2 = _merge_weights(l0, l1, l2)
        o = w0 * o0 + w1 * o1 + w2 * o2
        dsum = _seg_sum(dot_ * o, b)
        return (w0 * dot_, w1 * dot_, w2 * dot_, -w0 * dsum, -w1 * dsum, -w2 * dsum)
    ins = [(do, "row")] + [(a, "res") for a in (*os_, *lses)] + [(bd1, "full")]
    res = _rows("attn_merge_bwd", fn, ins, [(A_WIDTH, dt, d) for dt in (BF16, F32) for _, d in SWA_GROUPS], tr=256)
    return res[:3], res[3:]


def _lane_pick(x, lane_idx, lane):
    return jnp.sum(jnp.where(lane_idx == lane, x, 0.0), axis=-1, keepdims=True)


def _attn_bwd(qkvn, grp, do_g, lse, c_g):
    d, L, _ = qkvn.shape
    nblk = L // BAND

    def body(q_ref, kc_ref, kp_ref, vc_ref, vp_ref, do_ref, lse_ref, c_ref, dq_ref, dk_ref, dv_ref, ck, cv_):
        n = pl.program_id(1)

        @pl.when(n == 0)
        def _():
            ck[...] = jnp.zeros_like(ck)
            cv_[...] = jnp.zeros_like(cv_)

        @pl.when(n < nblk)
        def _():
            valid = _band_mask(n)
            lane = lax.broadcasted_iota(jnp.int32, (BAND, LANE), 1)
            first = lane < A_HEAD_DIM
            lane2 = lax.broadcasted_iota(jnp.int32, (2 * BAND, LANE), 1) < A_HEAD_DIM
            pairs = [slice(pr * LANE, (pr + 1) * LANE) for pr in range(A_WIDTH // LANE)]
            halves = (first, jnp.logical_not(first))
            qps = [q_ref[:, sl] for sl in pairs]
            dops = [do_ref[:, sl] for sl in pairs]
            kcats = [jnp.concatenate([kp_ref[:, sl], kc_ref[:, sl]], axis=0) for sl in pairs]
            vcats = [jnp.concatenate([vp_ref[:, sl], vc_ref[:, sl]], axis=0) for sl in pairs]
            heads = [(pr, hh) for pr in range(len(pairs)) for hh in range(2)]
            zero = jnp.zeros_like(qps[0])
            ss = [_dot(jnp.where(halves[hh], qps[pr], zero), kcats[pr], 1, 1) for pr, hh in heads]
            dps = [_dot(jnp.where(halves[hh], dops[pr], zero), vcats[pr], 1, 1) for pr, hh in heads]
            dss, pbs = [], []
            for (pr, hh), s, dp in zip(heads, ss, dps):
                lse_h = _lane_pick(lse_ref[:, pairs[pr]], lane, hh * A_HEAD_DIM)
                c_h = _lane_pick(c_ref[:, pairs[pr]], lane, hh * A_HEAD_DIM)
                p = jnp.where(valid, jnp.exp(s * (A_HEAD_DIM ** -0.5) - lse_h), 0.0)
                dss.append((p * (dp + c_h) * (A_HEAD_DIM ** -0.5)).astype(BF16))
                pbs.append(p.astype(BF16))
            dqs = [_dot(ds, kcats[pr]) for ds, (pr, _) in zip(dss, heads)]
            dks = [_dot(ds, qps[pr], 0, 0) for ds, (pr, _) in zip(dss, heads)]
            dvs = [_dot(pb, dops[pr], 0, 0) for pb, (pr, _) in zip(pbs, heads)]
            for pr, sl in enumerate(pairs):
                dq_ref[:, sl] = jnp.where(first, dqs[2 * pr], dqs[2 * pr + 1])
                dkc = jnp.where(lane2, dks[2 * pr], dks[2 * pr + 1])
                dvc = jnp.where(lane2, dvs[2 * pr], dvs[2 * pr + 1])
                dk_ref[:, sl] = ck[:, sl] + dkc[:BAND]
                dv_ref[:, sl] = cv_[:, sl] + dvc[:BAND]
                ck[:, sl] = dkc[BAND:]
                cv_[:, sl] = dvc[BAND:]

        @pl.when(n == nblk)
        def _():
            dk_ref[...] = ck[...]
            dv_ref[...] = cv_[...]

    blk = (None, BAND, A_WIDTH)
    last = nblk - 1
    qn = lambda n: jnp.minimum(n, last)
    pn = lambda n: jnp.clip(n - 1, 0, last)
    return tuple(pl.pallas_call(
        body, name=f"attn_bwd_g{grp}", grid=(d, nblk + 1),
        in_specs=[pl.BlockSpec(blk, lambda r, n: (r, qn(n), 0)),
                  pl.BlockSpec(blk, lambda r, n: (r, qn(n), 1)),
                  pl.BlockSpec(blk, lambda r, n: (r, pn(n), 1)),
                  pl.BlockSpec(blk, lambda r, n: (r, qn(n), 2)),
                  pl.BlockSpec(blk, lambda r, n: (r, pn(n), 2)),
                  pl.BlockSpec(blk, lambda r, n: (r, qn(n), 0)),
                  pl.BlockSpec(blk, lambda r, n: (r, qn(n), 0)),
                  pl.BlockSpec(blk, lambda r, n: (r, qn(n), 0))],
        out_specs=[pl.BlockSpec(blk, lambda r, n: (r, qn(n), 0)),
                   pl.BlockSpec(blk, lambda r, n: (r, pn(n), 0)),
                   pl.BlockSpec(blk, lambda r, n: (r, pn(n), 0))],
        out_shape=[jax.ShapeDtypeStruct((d, L, A_WIDTH), F32)] * 3,
        scratch_shapes=[pltpu.VMEM((BAND, A_WIDTH), F32), pltpu.VMEM((BAND, A_WIDTH), F32)],
        compiler_params=_cparams(("parallel", "arbitrary")),
    )(qkvn, qkvn, qkvn, qkvn, qkvn, do_g, lse, c_g))


def _attn_prep_bwd(qkv, grads, gains, ct, st, consts, bd):
    def fn(i, n, t, g, c_t, s_t, c, b, *gr):
        low = c[2:3, :] > 0.5
        in16 = c[3:4, :] > 0.5
        c_t, s_t = _rope_wide(c_t), _rope_wide(s_t)
        cols, dgs = [], []
        for grp in range(3):
            for which in range(3):
                dout = gr[grp * 3 + which]
                if which == 2:
                    cols.append(dout.astype(BF16))
                    continue
                off = (grp * 3 + which) * A_WIDTH
                x = t[:, off:off + A_WIDTH].astype(F32)
                gain = g[grp * 2 + which:grp * 2 + which + 1, :]
                r = lax.rsqrt(_seg_sum(x * x, b) + EPS)
                xh = x * r
                dy = _rope_apply_bwd(dout, c_t, s_t, low, in16)
                dyn = dy * gain
                dx = r * (dyn - xh * _seg_sum(dyn * xh, b))
                cols.append(dx.astype(BF16))
                dgs.append(_colsum(dy * xh))
        return (jnp.concatenate(cols, axis=1), *dgs)
    ins = [(qkv, "row"), (gains, "full"), (ct, "row"), (st, "row"), (consts, "full"), (bd, "full")] + [(a, "res") for a in grads]
    res = _rows("attn_prep_bwd", fn, ins, [(A_QKV, BF16)], tr=128, accs=[((1, A_WIDTH), F32)] * 6)
    return res[0], res[1:]


DN_QKV = 3 * DN_WIDTH
DN_QKVZ = DN_QKV + DN_WIDTH


def _sigmoid(x):
    return 1.0 / (1.0 + jnp.exp(-x))


def _softplus(x):
    return jnp.maximum(x, 0.0) + jnp.log(1.0 + jnp.exp(-jnp.abs(x)))


def _conv_taps(xs, w, tr):
    acc = None
    for j in range(CONV_W):
        sh = CONV_W - 1 - j
        term = (pltpu.roll(xs, sh, 0) if sh else xs)[SUBLANE:] * w[j:j + 1, :]
        acc = term if acc is None else acc + term
    return acc


def _dn_prep(qkvz, ab, convw, alog_row, dt_row):
    tr = 256

    def fn(i, n, x, xp, abt, w, al, dt):
        xp = jnp.where(i > 0, xp, 0.0)
        u = _conv_taps(jnp.concatenate([xp, x], axis=0), w, tr)
        y = u * _sigmoid(u)
        qs, ks = [], []
        for h in range(DN_HEADS):
            for dst, base, sc in ((qs, 0, DN_DIM ** -0.5), (ks, DN_WIDTH, 1.0)):
                seg = y[:, base + h * DN_DIM:base + (h + 1) * DN_DIM]
                dst.append(seg * (lax.rsqrt(jnp.sum(seg * seg, axis=-1, keepdims=True) + EPS) * sc))
        lane = lax.broadcasted_iota(jnp.int32, abt.shape, 1)
        g = -jnp.exp(al) * _softplus(abt + dt)
        gb = jnp.where(lane < DN_HEADS, g, jnp.where(lane < 2 * DN_HEADS, _sigmoid(abt), 0.0))
        return u, jnp.concatenate(qs, axis=1), jnp.concatenate(ks, axis=1), y[:, 2 * DN_WIDTH:], gb

    ins = [(qkvz, "row", (0, DN_QKV)), (qkvz, "prev8", (0, DN_QKV)), (ab, "row"), (convw, "full"),
           (alog_row, "full"), (dt_row, "full")]
    return _rows("dn_prep", fn, ins, [(DN_QKV, F32), (DN_WIDTH, F32), (DN_WIDTH, F32), (DN_WIDTH, F32), (LANE, F32)], tr=tr)


def _tri_masks():
    row = lax.broadcasted_iota(jnp.int32, (CHUNK, CHUNK), 0)
    col = lax.broadcasted_iota(jnp.int32, (CHUNK, CHUNK), 1)
    return row >= col, row > col, row == col


def _heads(fn, *lists):
    return [fn(*xs) for xs in zip(*lists)]


def _split(x):
    hi = x.astype(BF16)
    return hi, (x - hi.astype(F32)).astype(BF16)


def _dot3(a, b, ca=1, cb=0):
    (ah, al), (bh, bl) = a, b
    return _dot(ah, bh, ca, cb) + (_dot(ah, bl, ca, cb) + _dot(al, bh, ca, cb))


SPLIT_STEPS = 3


def _unit_lower_inverse(a_list, eye):
    ts = [eye - a for a in a_list]
    parts = [_split(a) for a in a_list]
    for step in range(5):
        if step < SPLIT_STEPS:
            parts = [_split(_dot3(p, p)) for p in parts]
            ts = [t + _dot3(_split(t), p) for t, p in zip(ts, parts)]
        else:
            parts = [(_dot(p[0], p[0]).astype(BF16), None) for p in parts]
            ts = [t + _dot(t.astype(BF16), p[0]) for t, p in zip(ts, parts)]
    return ts


def _dn_terms(qs, ks, vs, gb, solved=None):
    lower, strict, diag = _tri_masks()
    lane = lax.broadcasted_iota(jnp.int32, (CHUNK, LANE), 1)
    is_last = lax.broadcasted_iota(jnp.int32, (CHUNK, 1), 0) == CHUNK - 1
    hs = range(DN_HEADS)
    gc = _dot(lower.astype(F32), gb, precision=HIGHEST)
    gct = jnp.transpose(gc)
    bcol = [_lane_pick(gb, lane, DN_HEADS + h) for h in hs]
    gcol = [_lane_pick(gc, lane, h) for h in hs]
    glast = [jnp.sum(jnp.where(is_last, g, 0.0), axis=0, keepdims=True) for g in gcol]
    decay = [jnp.exp(jnp.where(lower, gcol[h] - gct[h:h + 1, :], -1e30)) for h in hs]
    kb = _heads(lambda k, b: k * b, ks, bcol)
    kk = _heads(lambda x, k: _bdot(x, k, 1, 1), kb, ks)
    qk = _heads(lambda q, k: _bdot(q, k, 1, 1), qs, ks)
    a = _heads(lambda x, d: jnp.where(strict, x * d, 0.0), kk, decay)
    eg = [jnp.exp(g) for g in gcol]
    egl = _heads(lambda gl, g: jnp.exp(gl - g), glast, gcol)
    rhs_w = _heads(lambda x, e: x * e, kb, eg)
    if solved is None:
        t_full = _unit_lower_inverse(a, diag.astype(F32))
        t = [_split(x) for x in t_full]
        u = _heads(lambda tt, v, b: _dot3(tt, _split(v * b)), t, vs, bcol)
        w = _heads(lambda tt, r: _dot3(tt, _split(r)), t, rhs_w)
    else:
        t_full, u, w = solved
        t = [_split(x) for x in t_full]
    return dict(bcol=bcol, decay=decay, kb=kb, a=a, t=t, t_full=t_full, eg=eg, egl=egl, rhs_w=rhs_w, u=u, w=w,
                attn=_heads(lambda x, d: x * d, qk, decay), q_dec=_heads(lambda q, e: q * e, qs, eg),
                k_dec=_heads(lambda k, e: k * e, ks, egl), c_dec=[jnp.exp(g) for g in glast],
                lower=lower, strict=strict, lane=lane, is_last=is_last)


def _head_slices(ref):
    return [ref[:, h * DN_DIM:(h + 1) * DN_DIM] for h in range(DN_HEADS)]


def _dn_chunk_fwd(q, k, v, gb):
    S = q.shape[0]
    N = S // CHUNK

    def body(q_ref, k_ref, v_ref, gb_ref, o_ref, st_ref, t_ref, u_ref, w_ref, state):
        @pl.when(pl.program_id(0) == 0)
        def _():
            state[...] = jnp.zeros_like(state)

        f = _dn_terms(_head_slices(q_ref), _head_slices(k_ref), _head_slices(v_ref), gb_ref[...])
        s = [state[h] for h in range(DN_HEADS)]
        for h in range(DN_HEADS):
            st_ref[0, h] = s[h]
            t_ref[0, h] = f["t_full"][h]
            u_ref[:, h * DN_DIM:(h + 1) * DN_DIM] = f["u"][h]
            w_ref[:, h * DN_DIM:(h + 1) * DN_DIM] = f["w"][h]
        sb = [x.astype(BF16) for x in s]
        v_new = _heads(lambda u, w, x: u - _bdot(w, x), f["u"], f["w"], sb)
        o = _heads(lambda qd, x, at, vn: _bdot(qd, x) + _bdot(at, vn), f["q_dec"], sb, f["attn"], v_new)
        new_s = _heads(lambda x, c, kd, vn: x * c + _bdot(kd, vn, 0, 0), s, f["c_dec"], f["k_dec"], v_new)
        for h in range(DN_HEADS):
            o_ref[:, h * DN_DIM:(h + 1) * DN_DIM] = o[h]
            state[h] = new_s[h]

    blk = pl.BlockSpec((CHUNK, DN_WIDTH), lambda n: (n, 0))
    st_blk = pl.BlockSpec((1, DN_HEADS, DN_DIM, DN_DIM), lambda n: (n, 0, 0, 0))
    t_blk = pl.BlockSpec((1, DN_HEADS, CHUNK, CHUNK), lambda n: (n, 0, 0, 0))
    wide = jax.ShapeDtypeStruct((S, DN_WIDTH), F32)
    o, states, t, u, w = pl.pallas_call(
        body, name="dn_chunk_fwd", grid=(N,),
        in_specs=[blk, blk, blk, pl.BlockSpec((CHUNK, LANE), lambda n: (n, 0))],
        out_specs=[blk, st_blk, t_blk, blk, blk],
        out_shape=[wide, jax.ShapeDtypeStruct((N, DN_HEADS, DN_DIM, DN_DIM), F32),
                   jax.ShapeDtypeStruct((N, DN_HEADS, CHUNK, CHUNK), F32), wide, wide],
        scratch_shapes=[pltpu.VMEM((DN_HEADS, DN_DIM, DN_DIM), F32)],
        compiler_params=_cparams(("arbitrary",)),
    )(q, k, v, gb)
    return o, (states, t, u, w)


def _dn_chunk_bwd(q, k, v, gb, saved, do):
    S = q.shape[0]
    N = S // CHUNK
    states, t_saved, u_saved, w_saved = saved

    def body(q_ref, k_ref, v_ref, gb_ref, st_ref, t_ref, u_ref, w_ref, do_ref, dq_ref, dk_ref, dv_ref, dgb_ref, dstate):
        @pl.when(pl.program_id(0) == 0)
        def _():
            dstate[...] = jnp.zeros_like(dstate)

        hs = range(DN_HEADS)
        qs, ks, vs, dos = (_head_slices(r) for r in (q_ref, k_ref, v_ref, do_ref))
        f = _dn_terms(qs, ks, vs, gb_ref[...], ([t_ref[0, h] for h in hs], _head_slices(u_ref), _head_slices(w_ref)))
        lane, is_last = f["lane"], f["is_last"]
        rowsum = lambda x: jnp.sum(x, axis=-1, keepdims=True)
        s = [st_ref[0, h] for h in hs]
        dsn = [dstate[h] for h in hs]
        sb = [x.astype(BF16) for x in s]
        dsb = [x.astype(BF16) for x in dsn]
        dob = [x.astype(BF16) for x in dos]
        v_new = _heads(lambda u, w, x: u - _bdot(w, x), f["u"], f["w"], sb)
        dv_new = _heads(lambda at, d, kd, x: _bdot(at, d, 0, 0) + _bdot(kd, x), f["attn"], dob, f["k_dec"], dsb)
        dattn = _heads(lambda d, vn: _bdot(d, vn, 1, 1), dob, v_new)
        dq_dec = _heads(lambda d, x: _bdot(d, x, 1, 1), dob, sb)
        dk_dec = _heads(lambda vn, x: _bdot(vn, x, 1, 1), v_new, dsb)
        dw = _heads(lambda dv_, x: -_bdot(dv_, x, 1, 1), dv_new, sb)
        new_ds = _heads(lambda x, c, qd, d, w, dv_: x * c + _bdot(qd, d, 0, 0) - _bdot(w, dv_, 0, 0),
                        dsn, f["c_dec"], f["q_dec"], dob, f["w"], dv_new)
        for h in hs:
            dstate[h] = new_ds[h]
        drhs_u = _heads(lambda tt, x: _dot3(tt, _split(x), 0, 0), f["t"], dv_new)
        drhs_w = _heads(lambda tt, x: _dot3(tt, _split(x), 0, 0), f["t"], dw)
        da = _heads(lambda du_, u, dw_, w: jnp.where(f["strict"], -(_bdot(du_, u, 1, 1) + _bdot(dw_, w, 1, 1)), 0.0),
                    drhs_u, f["u"], drhs_w, f["w"])
        dkk = _heads(lambda x, d: x * d, da, f["decay"])
        dqk = _heads(lambda x, d: x * d, dattn, f["decay"])
        dkb = _heads(lambda x, k_, dw_, e: _bdot(x, k_) + dw_ * e, dkk, ks, drhs_w, f["eg"])
        dq = _heads(lambda x, k_, dqd, e: _bdot(x, k_) + dqd * e, dqk, ks, dq_dec, f["eg"])
        dk = _heads(lambda x, kb_, y, q_, dkd, el, dkb_, b: _bdot(x, kb_, 0, 0) + _bdot(y, q_, 0, 0) + dkd * el + dkb_ * b,
                    dkk, f["kb"], dqk, qs, dk_dec, f["egl"], dkb, f["bcol"])
        m = _heads(lambda x, a_, y, at: x * a_ + y * at, da, f["a"], dattn, f["attn"])
        ones = jnp.ones((CHUNK, LANE), BF16)
        col_m = [(_dot(mh, ones, 0, 0) + _dot(ml, ones, 0, 0))[:, 0:1] for mh, ml in map(_split, m)]
        dgc_all = jnp.zeros((CHUNK, LANE), F32)
        dbeta_all = jnp.zeros((CHUNK, LANE), F32)
        for h in hs:
            dq_ref[:, h * DN_DIM:(h + 1) * DN_DIM] = dq[h]
            dk_ref[:, h * DN_DIM:(h + 1) * DN_DIM] = dk[h]
            dv_ref[:, h * DN_DIM:(h + 1) * DN_DIM] = drhs_u[h] * f["bcol"][h]
            kdec_term = rowsum(dk_dec[h] * f["k_dec"][h])
            dc_dec = _sum_all(dsn[h] * s[h])
            dgc = (rowsum(m[h]) - col_m[h] + rowsum(dq_dec[h] * f["q_dec"][h]) - kdec_term
                   + rowsum(drhs_w[h] * f["rhs_w"][h]))
            last_extra = jnp.sum(kdec_term, axis=0, keepdims=True) + dc_dec * f["c_dec"][h]
            dgc = dgc + jnp.where(is_last, last_extra, 0.0)
            dbeta = rowsum(drhs_u[h] * vs[h]) + rowsum(dkb[h] * ks[h])
            dgc_all = jnp.where(lane == h, dgc, dgc_all)
            dbeta_all = jnp.where(lane == DN_HEADS + h, dbeta, dbeta_all)
        dg_all = _dot(f["lower"].astype(F32), dgc_all, 0, 0, precision=HIGHEST)
        dgb_ref[...] = jnp.where(lane < DN_HEADS, dg_all, dbeta_all)

    rev = lambda n: (N - 1 - n, 0)
    blk = pl.BlockSpec((CHUNK, DN_WIDTH), rev)
    gblk = pl.BlockSpec((CHUNK, LANE), rev)
    st_blk = pl.BlockSpec((1, DN_HEADS, DN_DIM, DN_DIM), lambda n: (N - 1 - n, 0, 0, 0))
    t_blk = pl.BlockSpec((1, DN_HEADS, CHUNK, CHUNK), lambda n: (N - 1 - n, 0, 0, 0))
    return pl.pallas_call(
        body, name="dn_chunk_bwd", grid=(N,),
        in_specs=[blk, blk, blk, gblk, st_blk, t_blk, blk, blk, blk],
        out_specs=[blk, blk, blk, gblk],
        out_shape=[jax.ShapeDtypeStruct((S, DN_WIDTH), F32)] * 3 + [jax.ShapeDtypeStruct((S, LANE), F32)],
        scratch_shapes=[pltpu.VMEM((DN_HEADS, DN_DIM, DN_DIM), F32)],
        compiler_params=_cparams(("arbitrary",)),
    )(q, k, v, gb, states, t_saved, u_saved, w_saved, do)


def _dn_post(o, qkvz, gain_row):
    def fn(i, n, ot, z, g):
        cols = []
        for h in range(DN_HEADS):
            seg = ot[:, h * DN_DIM:(h + 1) * DN_DIM]
            cols.append(seg * lax.rsqrt(jnp.mean(seg * seg, axis=-1, keepdims=True) + EPS) * g)
        return (jnp.concatenate(cols, axis=1) * (z * _sigmoid(z)),)
    return _rows("dn_post", fn, [(o, "row"), (qkvz, "row", (3, DN_WIDTH)), (gain_row, "full")], [(DN_WIDTH, BF16)], tr=512)


def _dn_post_bwd(don, o, qkvz, gain_row):
    def fn(i, n, dy, ot, z, g):
        sg = _sigmoid(z)
        sz = z * sg
        dos, ohs = [], []
        dg = jnp.zeros((1, DN_DIM), F32)
        for h in range(DN_HEADS):
            sl = slice(h * DN_DIM, (h + 1) * DN_DIM)
            seg = ot[:, sl]
            r = lax.rsqrt(jnp.mean(seg * seg, axis=-1, keepdims=True) + EPS)
            oh = seg * r
            dno = dy[:, sl] * sz[:, sl]
            dg = dg + _colsum(dno * oh)
            dn = dno * g
            dos.append(r * (dn - oh * jnp.mean(dn * oh, axis=-1, keepdims=True)))
            ohs.append(oh * g)
        dz = dy * jnp.concatenate(ohs, axis=1) * (sg * (1.0 + z * (1.0 - sg)))
        return jnp.concatenate(dos, axis=1), dz, dg
    ins = [(don, "row"), (o, "row"), (qkvz, "row", (3, DN_WIDTH)), (gain_row, "full")]
    return _rows("dn_post_bwd", fn, ins, [(DN_WIDTH, F32), (DN_WIDTH, F32)], tr=256, accs=[((1, DN_DIM), F32)])


def _dn_prep_bwd(dq, dk, dv, dgb, u, ab, alog_row, dt_row):
    def fn(i, n, dqt, dkt, dvt, dgbt, ut, abt, al, dt):
        sg = _sigmoid(ut)
        y = ut * sg
        dys = []
        for grad, base, sc in ((dqt, 0, DN_DIM ** -0.5), (dkt, DN_WIDTH, 1.0)):
            for h in range(DN_HEADS):
                seg = y[:, base + h * DN_DIM:base + (h + 1) * DN_DIM]
                gr = grad[:, h * DN_DIM:(h + 1) * DN_DIM]
                r = lax.rsqrt(jnp.sum(seg * seg, axis=-1, keepdims=True) + EPS)
                xh = seg * r
                dys.append((r * sc) * (gr - xh * jnp.sum(gr * xh, axis=-1, keepdims=True)))
        dy = jnp.concatenate(dys + [dvt], axis=1)
        du = dy * (sg * (1.0 + ut * (1.0 - sg)))
        lane = lax.broadcasted_iota(jnp.int32, abt.shape, 1)
        is_g = lane < DN_HEADS
        ea = jnp.exp(al)
        x = abt + dt
        slope = -ea * _sigmoid(x)
        gval = -ea * _softplus(x)
        dg = jnp.where(is_g, dgbt, 0.0)
        beta = _sigmoid(abt)
        dab = jnp.where(is_g, dg * slope, jnp.where(lane < 2 * DN_HEADS, dgbt * beta * (1.0 - beta), 0.0))
        return du, dab, _colsum(dg * gval), _colsum(dg * slope)
    ins = [(dq, "row"), (dk, "row"), (dv, "row"), (dgb, "row"), (u, "row"), (ab, "row"), (alog_row, "full"), (dt_row, "full")]
    return _rows("dn_prep_bwd", fn, ins, [(DN_QKV, F32), (LANE, BF16)], tr=256, accs=[((1, LANE), F32)] * 2)


def _dn_conv_bwd(du, dz, qkvz, convw):
    tr = 256

    def fn(i, n, dut, dun, dzt, x, xp, w):
        dun = jnp.where(i < n - 1, dun, 0.0)
        dus = jnp.concatenate([dut, dun], axis=0)
        xs = jnp.concatenate([jnp.where(i > 0, xp, 0.0), x], axis=0)
        dx = None
        dws = []
        for j in range(CONV_W):
            sh = CONV_W - 1 - j
            term = (pltpu.roll(dus, tr + SUBLANE - sh, 0) if sh else dus)[:tr] * w[j:j + 1, :]
            dx = term if dx is None else dx + term
            dws.append(_colsum(dut * (pltpu.roll(xs, sh, 0) if sh else xs)[SUBLANE:]))
        return (jnp.concatenate([dx.astype(BF16), dzt.astype(BF16)], axis=1), *dws)

    ins = [(du, "row"), (du, "next8"), (dz, "row"), (qkvz, "row", (0, DN_QKV)), (qkvz, "prev8", (0, DN_QKV)), (convw, "full")]
    res = _rows("dn_conv_bwd", fn, ins, [(DN_QKVZ, BF16)], tr=tr, accs=[((1, DN_QKV), F32)] * CONV_W)
    return res[0], res[1:]


def _add(acc, r):
    return (r + acc,)


def _mlp_ple_fwd(i, x1, hm, p_i, ple_gain, next_gain, w_up, w_down, w_ple, w_gate):
    u, a = _mm(f"mlp_up{i}", hm, w_up, epilogue=lambda acc: (acc, jnp.square(jnp.maximum(acc, 0.0))),
               out_dtypes=(BF16, BF16))
    x2, hp = _mm(f"mlp_down{i}", a, w_down, epilogue=_res_norm, extras=(x1, ple_gain), out_dtypes=(F32, BF16),
                 tm_pref=FUSED_ROWS)
    pp = _mm(f"ple_proj{i}", p_i, w_ple)

    def gate_epilogue(acc, x2t, ppt, *g):
        gate = _sigmoid(acc)
        x3 = x2t + ppt * gate
        if not g:
            return x3, gate
        return x3, gate, x3 * lax.rsqrt(jnp.mean(x3 * x3, axis=-1, keepdims=True) + EPS) * g[0]

    more = () if next_gain is None else (next_gain,)
    x3, gate, *h_next = _mm(f"ple_gate{i}", hp, w_gate, epilogue=gate_epilogue, extras=(x2, pp) + more,
                            out_dtypes=(F32, F32) + (BF16,) * len(more), tm_pref=FUSED_ROWS)
    return x3, (h_next[0] if more else None), dict(x1=x1, hm=hm, u=u, a=a, x2=x2, hp=hp, pp=pp, gate=gate, p=p_i)


def _mlp_ple_bwd(i, dx3, sv, mlp_gain, ple_gain, w_up, w_down, w_gate):
    def fn(_i, _n, d, g, pp):
        return d * g, d * pp * g * (1.0 - g)
    dpp, dzg = _rows(f"ple_gate_bwd{i}", fn, [(dx3, "row"), (sv["gate"], "row"), (sv["pp"], "row")],
                     [(D_MODEL, BF16), (D_MODEL, BF16)], tr=512)
    d_w_ple = _mm(f"ple_proj_dw{i}", sv["p"], dpp, ta=True, out_dtypes=(BF16,))
    d_w_gate = _mm(f"ple_gate_dw{i}", sv["hp"], dzg, ta=True, out_dtypes=(BF16,))
    dx2, dx2b, d_ple_gain = _mm(f"ple_gate_dx{i}", dzg, w_gate, tb=True, epilogue=_norm_bwd_2,
                                extras=(sv["x2"], ple_gain, dx3), out_dtypes=(F32, BF16), n_colsums=1, tm_pref=FUSED_ROWS)
    d_w_down = _mm(f"mlp_down_dw{i}", sv["a"], dx2b, ta=True, out_dtypes=(BF16,))
    du = _mm(f"mlp_down_dx{i}", dx2b, w_down, tb=True,
             epilogue=lambda acc, ut: (acc * (2.0 * jnp.maximum(ut.astype(F32), 0.0)),), extras=(sv["u"],), out_dtypes=(BF16,))
    d_w_up = _mm(f"mlp_up_dw{i}", sv["hm"], du, ta=True, out_dtypes=(BF16,))
    dx1, dx1b, d_mlp_gain = _mm(f"mlp_up_dx{i}", du, w_up, tb=True, epilogue=_norm_bwd_2,
                                extras=(sv["x1"], mlp_gain, dx2), out_dtypes=(F32, BF16), n_colsums=1, tm_pref=FUSED_ROWS)
    return dx1, dx1b, dict(w_ple=d_w_ple, w_ple_gate=d_w_gate, w_down=d_w_down, w_up=d_w_up,
                           ple_norm=d_ple_gain, mlp_norm=d_mlp_gain)


def _loss_fwd_bwd(y, target):
    D = y.shape[1]

    def fn(i, n, yt, tt):
        e = yt - tt
        return e * (1.0 / D), _colsum(e * e)
    dy, sq = _rows("loss", fn, [(y, "row"), (target, "row")], [(D, F32)], tr=512, accs=[((1, D), F32)])
    return sq, dy


def _after(small, token):
    return small + token[0:1, 0:1]


def _local_step(x, p, positions, target, W, P, rest_of_weights, send_layer1, send_mlp0, send_attn):
    consts = _head_consts()
    bd = _block_diag(1.0 / A_HEAD_DIM)
    bd1 = _block_diag(1.0)
    ct, st = _rope_tables(positions, consts)
    gains = jnp.stack([jnp.tile(v, A_HEADS) for g in range(3) for v in (P["attn_q_gain"][g], P["attn_k_gain"][g])])
    pad = LANE - DN_HEADS
    alog_row = jnp.pad(P["dn_a_log"].reshape(1, DN_HEADS), ((0, 0), (0, pad)))
    dt_row = jnp.pad(P["dn_dt_bias"].reshape(1, DN_HEADS), ((0, 0), (0, pad)))
    ogain_row = P["dn_o_gain"].reshape(1, DN_DIM)
    row = lambda name, i: P[name][i:i + 1]

    h0 = _rmsnorm_fwd("mix_norm0", x, row("mix_norm", 0))
    qkv = _mm("attn_qkv", h0, W["attn_w_qkv"], out_dtypes=(BF16,))
    qkvn = _attn_prep(qkv, gains, ct, st, consts, bd)
    os_, lses = zip(*[_attn_fwd(qkvn[g], g) for g in range(3)])
    o_attn = _attn_merge(os_, lses)
    x1, hm0 = _mm("attn_out", o_attn, W["attn_w_o"], epilogue=_res_norm, extras=(x, row("mlp_norm", 0)),
                  out_dtypes=(F32, BF16), tm_pref=FUSED_ROWS)
    W = {**W, **rest_of_weights(x1)}
    x3, h1, sv0 = _mlp_ple_fwd(0, x1, hm0, p[0], row("ple_norm", 0), row("mix_norm", 1),
                               W["w_up"][0], W["w_down"][0], W["w_ple"][0], W["w_ple_gate"][0])
    qkvz = _mm("dn_in_qkvz", h1, W["dn_w_qkvz"])
    ab = _mm("dn_in_ab", h1, W["dn_w_ab"])
    u, q, k, v, gb = _dn_prep(qkvz, ab, W["dn_conv"], alog_row, dt_row)
    o_dn, states = _dn_chunk_fwd(q, k, v, gb)
    on = _dn_post(o_dn, qkvz, ogain_row)
    x4, hm1 = _mm("dn_out", on, W["dn_w_o"], epilogue=_res_norm, extras=(x3, row("mlp_norm", 1)),
                  out_dtypes=(F32, BF16), tm_pref=FUSED_ROWS)
    x6, _, sv1 = _mlp_ple_fwd(1, x4, hm1, p[1], row("ple_norm", 1), None,
                              W["w_up"][1], W["w_down"][1], W["w_ple"][1], W["w_ple_gate"][1])
    sq, dy = _loss_fwd_bwd(x6, target)

    dx4, dx4b, g1 = _mlp_ple_bwd(1, dy, sv1, row("mlp_norm", 1), row("ple_norm", 1),
                                 W["w_up"][1], W["w_down"][1], W["w_ple_gate"][1])
    don = _mm("dn_out_dx", dx4b, W["dn_w_o"], tb=True)
    d_dn_w_o = _mm("dn_out_dw", on, dx4b, ta=True, out_dtypes=(BF16,))
    do_dn, dz, d_ogain = _dn_post_bwd(don, o_dn, qkvz, ogain_row)
    dq, dk, dv, dgb = _dn_chunk_bwd(q, k, v, gb, states, do_dn)
    du, dab, d_alog, d_dt = _dn_prep_bwd(dq, dk, dv, dgb, u, ab, alog_row, dt_row)
    dqkvz, d_conv = _dn_conv_bwd(du, dz, qkvz, W["dn_conv"])
    dh1 = _mm("dn_in_ab_dx", dab, W["dn_w_ab"], tb=True)
    dx3, d_mix1 = _mm("dn_in_qkvz_dx", dqkvz, W["dn_w_qkvz"], tb=True,
                      epilogue=lambda acc, part, xt, g, dres: _norm_bwd(acc + part, xt, g, dres),
                      extras=(dh1, x3, row("mix_norm", 1), dx4), n_colsums=1, tm_pref=FUSED_ROWS)
    d_w_qkvz = _mm("dn_in_qkvz_dw", h1, dqkvz, ta=True, out_dtypes=(BF16,))
    d_w_ab = _mm("dn_in_ab_dw", h1, dab, ta=True, out_dtypes=(BF16,))
    token = send_layer1(dict(
        dn_w_qkvz=d_w_qkvz, dn_w_ab=d_w_ab, dn_conv=jnp.concatenate(d_conv, 0), dn_w_o=d_dn_w_o,
        w_up=g1["w_up"], w_down=g1["w_down"], w_ple=g1["w_ple"], w_ple_gate=g1["w_ple_gate"]))
    dx1, dx1b, g0 = _mlp_ple_bwd(0, dx3, sv0, row("mlp_norm", 0), _after(row("ple_norm", 0), token),
                                 W["w_up"][0], W["w_down"][0], W["w_ple_gate"][0])
    token = send_mlp0(dict(w_up=g0["w_up"], w_down=g0["w_down"], w_ple=g0["w_ple"], w_ple_gate=g0["w_ple_gate"]))
    do_attn = _mm("attn_out_dx", dx1b, W["attn_w_o"], tb=True, epilogue=_add, extras=(_after(jnp.zeros((1, A_WIDTH), F32), token),))
    d_attn_w_o = _mm("attn_out_dw", o_attn, dx1b, ta=True, out_dtypes=(BF16,))
    dos, cs = _attn_merge_bwd(do_attn, os_, lses, bd1)
    grads9 = []
    for g in range(3):
        grads9 += list(_attn_bwd(qkvn[g], g, dos[g], lses[g], cs[g]))
    dqkv, dgains = _attn_prep_bwd(qkv, grads9, gains, ct, st, consts, bd)
    d_attn_w_qkv = _mm("attn_qkv_dw", h0, dqkv, ta=True, out_dtypes=(BF16,))
    token = send_attn(dict(attn_w_qkv=d_attn_w_qkv, attn_w_o=d_attn_w_o))
    dx0, d_mix0 = _mm("attn_qkv_dx", dqkv, W["attn_w_qkv"], tb=True, epilogue=_norm_bwd,
                      extras=(x, _after(row("mix_norm", 0), token), dx1), n_colsums=1, tm_pref=FUSED_ROWS)

    dg = jnp.stack([t.reshape(A_HEADS, A_HEAD_DIM).sum(0) for t in dgains])
    small = dict(
        mix_norm=jnp.concatenate([d_mix0, d_mix1], 0),
        attn_q_gain=dg[0::2][None], attn_k_gain=dg[1::2][None],
        dn_a_log=d_alog[:, :DN_HEADS], dn_dt_bias=d_dt[:, :DN_HEADS], dn_o_gain=d_ogain,
        mlp_norm=jnp.concatenate([g0["mlp_norm"], g1["mlp_norm"]], 0),
        ple_norm=jnp.concatenate([g0["ple_norm"], g1["ple_norm"]], 0),
    )
    return sq, dx0, small


MESH_IDS = pl.DeviceIdType.MESH
ANY = pl.BlockSpec(memory_space=pl.ANY)


def _place():
    return lax.axis_index("x"), lax.axis_index("y"), lax.axis_index("c")


def _sem_scratch(n_streams):
    return [pltpu.SemaphoreType.DMA((n_streams, N_DEV - 1)), pltpu.SemaphoreType.DMA((n_streams, N_DEV - 1)),
            pltpu.SemaphoreType.DMA((n_streams,))]


def _all_gather(name, arrays, streams):
    n_in, n_st = len(arrays), len(streams)
    shapes = [arrays[a].shape if li is None else arrays[a].shape[1:] for a, li in streams]

    def body(*refs):
        in_refs, out_refs, token = refs[:n_in], refs[n_in:n_in + n_st], refs[n_in + n_st]
        send_sems, recv_sems, local_sems = refs[n_in + n_st + 1:]
        token[...] = jnp.zeros_like(token)
        x, y, c = _place()
        me, sibling = (x, y, c), (x, y, 1 - c)
        chips = [(1 - x, y), (x, 1 - y), (1 - x, 1 - y)]

        def copy(s, k, block, to, own=False):
            a, li = streams[s]
            dst = out_refs[s].at[4 * block[0] + 2 * block[1] + block[2]]
            src = (in_refs[a] if li is None else in_refs[a].at[li]) if own else dst
            return pltpu.make_async_remote_copy(src_ref=src, dst_ref=dst, send_sem=send_sems.at[s, k],
                                                recv_sem=recv_sems.at[s, k], device_id=to, device_id_type=MESH_IDS)

        started = []
        for s, (a, li) in enumerate(streams):
            src = in_refs[a] if li is None else in_refs[a].at[li]
            mine = pltpu.make_async_copy(src, out_refs[s].at[4 * x + 2 * y + c], local_sems.at[s])
            mine.start()
            started.append(mine)
        sends = []
        for s in range(n_st):
            first = [copy(s, 0, me, sibling, own=True)]
            first += [copy(s, 1 + j, me, (*chip, c), own=True) for j, chip in enumerate(chips)]
            for cp in first:
                cp.start()
            sends += first
        for j, chip in enumerate(chips):
            for s in range(n_st):
                copy(s, 1 + j, (*chip, c), me).wait_recv()
                fwd = copy(s, 4 + j, (*chip, c), sibling)
                fwd.start()
                sends.append(fwd)
        for s in range(n_st):
            copy(s, 0, sibling, me).wait_recv()
            for j, chip in enumerate(chips):
                copy(s, 4 + j, (*chip, 1 - c), me).wait_recv()
        for cp in sends:
            cp.wait_send()
        for cp in started:
            cp.wait()

    res = pl.pallas_call(
        body, name=name,
        out_shape=[jax.ShapeDtypeStruct((N_DEV,) + tuple(sh), arrays[a].dtype) for sh, (a, _) in zip(shapes, streams)]
        + [jax.ShapeDtypeStruct((SUBLANE, LANE), F32)],
        in_specs=[ANY] * n_in, out_specs=[ANY] * n_st + [pl.BlockSpec(memory_space=pltpu.VMEM)],
        scratch_shapes=_sem_scratch(n_st),
    )(*arrays)
    return list(res[:n_st]), res[n_st]


HBM = pl.BlockSpec(memory_space=pltpu.HBM)
SEM = pl.BlockSpec(memory_space=pltpu.SEMAPHORE)
FLOWS = pltpu.CompilerParams(has_side_effects=pltpu.SideEffectType.DATAFLOW_SIDE_EFFECTING)


def _in_hbm(a):
    return pltpu.with_memory_space_constraint(a, pltpu.HBM)


def _hbm_like(a):
    return pltpu.HBM(a.shape, a.dtype)


def _peers(x, y, c):
    return [(1 - x if k & 4 else x, 1 - y if k & 2 else y, 1 - c if k & 1 else c) for k in range(1, N_DEV)]


def _start_copies(name, n_remote, n_own, make_copies, operands):
    n = len(operands)

    def body(*refs):
        for cp in make_copies(refs[:n], refs[n], refs[n + 1], refs[n + 2]):
            cp.start()
        refs[-1][...] = jnp.zeros_like(refs[-1])

    res = pl.pallas_call(
        body, name=name,
        out_shape=(pltpu.SemaphoreType.DMA((n_remote,)), pltpu.SemaphoreType.DMA((n_remote,)), pltpu.SemaphoreType.DMA((n_own,)),
                   *[_hbm_like(t) for t in operands], jax.ShapeDtypeStruct((SUBLANE, LANE), F32)),
        in_specs=[HBM] * n, out_specs=(SEM, SEM, SEM, *[HBM] * n, pl.BlockSpec(memory_space=pltpu.VMEM)),
        input_output_aliases={i: 3 + i for i in range(n)}, compiler_params=FLOWS,
    )(*[_in_hbm(t) for t in operands])
    return res[:3], list(res[3:3 + n]), res[-1]


def _wait_copies(name, make_waits, sems, operands, after):
    n = len(operands)

    def body(*refs):
        for wait in make_waits(refs[:n], refs[n], refs[n + 1], refs[n + 2]):
            wait()

    res = pl.pallas_call(
        body, name=name, out_shape=tuple(_hbm_like(t) for t in operands),
        in_specs=[HBM] * n + [SEM, SEM, SEM, ANY], out_specs=tuple([HBM] * n),
        input_output_aliases={i: i for i in range(n)}, compiler_params=FLOWS,
    )(*operands, *sems, after)
    return list(res)


def _gather_plan(n_in, streams):
    def block(arr, s):
        a, li = streams[s]
        return arr[a] if li is None else arr[a].at[li]

    def copies(refs, send_sems, recv_sems, own_sems, arrivals=False):
        arr, land = refs[:n_in], refs[n_in:]
        x, y, c = _place()
        me = 4 * x + 2 * y + c
        out = []
        for s in range(len(streams)):
            out.append(("own", pltpu.make_async_copy(block(arr, s), land[s].at[me], own_sems.at[s])))
            for k, (px, py, pc) in enumerate(_peers(x, y, c)):
                out.append(("remote", pltpu.make_async_remote_copy(
                    src_ref=block(arr, s), dst_ref=land[s].at[4 * px + 2 * py + pc if arrivals else me],
                    send_sem=send_sems.at[s * (N_DEV - 1) + k], recv_sem=recv_sems.at[s * (N_DEV - 1) + k],
                    device_id=(px, py, pc), device_id_type=MESH_IDS)))
        return out
    return copies


def _exchange_plan(n_st):
    def copies(refs, send_sems, recv_sems, own_sems, arrivals=False):
        snd, rcv = refs[:n_st], refs[n_st:]
        x, y, c = _place()
        me = 4 * x + 2 * y + c
        out = []
        for s in range(n_st):
            out.append(("own", pltpu.make_async_copy(snd[s].at[me], rcv[s].at[me], own_sems.at[s])))
            for k, (px, py, pc) in enumerate(_peers(x, y, c)):
                peer = 4 * px + 2 * py + pc
                out.append(("remote", pltpu.make_async_remote_copy(
                    src_ref=snd[s].at[peer], dst_ref=rcv[s].at[peer if arrivals else me],
                    send_sem=send_sems.at[s * (N_DEV - 1) + k], recv_sem=recv_sems.at[s * (N_DEV - 1) + k],
                    device_id=(px, py, pc), device_id_type=MESH_IDS)))
        return out
    return copies


def _split_transfer(tag, plan, n_streams, operands):
    sems, operands, token = _start_copies(f"{tag}_start", n_streams * (N_DEV - 1), n_streams,
                                          lambda refs, a, b, o: [cp for _, cp in plan(refs, a, b, o)], operands)

    def waits(refs, a, b, o):
        out = []
        for kind, cp in plan(refs, a, b, o, arrivals=True):
            out += [cp.wait] if kind == "own" else [cp.wait_send, cp.wait_recv]
        return out

    return (lambda after: _wait_copies(f"{tag}_wait", waits, sems, operands, after)), token


def _gather_async(tag, arrays, streams):
    lands = [lax.empty((N_DEV,) + tuple(arrays[a].shape if li is None else arrays[a].shape[1:]), arrays[a].dtype)
             for a, li in streams]
    finish, token = _split_transfer(tag, _gather_plan(len(arrays), streams), len(streams), list(arrays) + lands)
    return (lambda after: finish(after)[len(arrays):]), token


def _exchange_async(tag, sends):
    recvs = [lax.empty(t.shape, t.dtype) for t in sends]
    finish, token = _split_transfer(tag, _exchange_plan(len(sends)), len(sends), list(sends) + recvs)
    return (lambda after: finish(after)[len(sends):]), token


def _dn_in_pieces():
    n = (DN_QKVZ + 2 * DN_HEADS) // N_DEV
    segs = ((0, DN_QKV, 0, 0), (DN_QKV, DN_QKV + 2 * DN_HEADS, 1, 0), (DN_QKV + 2 * DN_HEADS, DN_QKVZ + 2 * DN_HEADS, 0, DN_QKV))
    out = []
    for d in range(N_DEV):
        lo, hi = d * n, (d + 1) * n
        for s0, s1, tgt, t0 in segs:
            a, b = max(lo, s0), min(hi, s1)
            if a < b:
                out.append((d, a - lo, b - lo, tgt, t0 + a - s0))
    return out


def _unpack_cols(name, g):
    _, K, n = g.shape
    tr = 256

    def body(g_ref, o_ref):
        for d in range(N_DEV):
            o_ref[:, d * n:(d + 1) * n] = g_ref[d]

    return pl.pallas_call(
        body, name=name, grid=(K // tr,), in_specs=[pl.BlockSpec((N_DEV, tr, n), lambda i: (0, i, 0))],
        out_specs=pl.BlockSpec((tr, N_DEV * n), lambda i: (i, 0)),
        out_shape=jax.ShapeDtypeStruct((K, N_DEV * n), g.dtype), compiler_params=_cparams(("parallel",)),
    )(g)


def _pack_cols(name, w):
    K, n = w.shape[0], w.shape[1] // N_DEV
    tr = 256

    def body(w_ref, o_ref):
        for d in range(N_DEV):
            o_ref[d] = w_ref[:, d * n:(d + 1) * n]

    return pl.pallas_call(
        body, name=name, grid=(K // tr,), in_specs=[pl.BlockSpec((tr, N_DEV * n), lambda i: (i, 0))],
        out_specs=pl.BlockSpec((N_DEV, tr, n), lambda i: (0, i, 0)),
        out_shape=jax.ShapeDtypeStruct((N_DEV, K, n), w.dtype), compiler_params=_cparams(("parallel",)),
    )(w)


def _unpack_dn_in(g):
    _, K, n = g.shape
    tr = 256

    def body(g_ref, qkvz_ref, ab_ref):
        ab_ref[...] = jnp.zeros_like(ab_ref)
        for d, c0, c1, tgt, t0 in _dn_in_pieces():
            (qkvz_ref, ab_ref)[tgt][:, t0:t0 + c1 - c0] = g_ref[d, :, c0:c1]

    return pl.pallas_call(
        body, name="unpack_dn_in", grid=(K // tr,), in_specs=[pl.BlockSpec((N_DEV, tr, n), lambda i: (0, i, 0))],
        out_specs=[pl.BlockSpec((tr, DN_QKVZ), lambda i: (i, 0)), pl.BlockSpec((tr, LANE), lambda i: (i, 0))],
        out_shape=[jax.ShapeDtypeStruct((K, DN_QKVZ), g.dtype), jax.ShapeDtypeStruct((K, LANE), g.dtype)],
        compiler_params=_cparams(("parallel",)),
    )(g)


def _pack_dn_in(d_qkvz, d_ab):
    K = d_qkvz.shape[0]
    n = (DN_QKVZ + 2 * DN_HEADS) // N_DEV
    tr = 256

    def body(qkvz_ref, ab_ref, o_ref):
        for d, c0, c1, tgt, t0 in _dn_in_pieces():
            o_ref[d, :, c0:c1] = (qkvz_ref, ab_ref)[tgt][:, t0:t0 + c1 - c0]

    return pl.pallas_call(
        body, name="pack_dn_in", grid=(K // tr,),
        in_specs=[pl.BlockSpec((tr, DN_QKVZ), lambda i: (i, 0)), pl.BlockSpec((tr, LANE), lambda i: (i, 0))],
        out_specs=pl.BlockSpec((N_DEV, tr, n), lambda i: (0, i, 0)),
        out_shape=jax.ShapeDtypeStruct((N_DEV, K, n), d_qkvz.dtype), compiler_params=_cparams(("parallel",)),
    )(d_qkvz, d_ab)


ADAMW_ROWS = 256


def _adamw(name, parts, w, m, v):
    R, C = w.shape
    tr = min(R, ADAMW_ROWS)
    assert R % tr == 0 and parts.shape == (N_DEV, R, C)
    c1 = 1.0 - B1 ** STEP
    c2 = 1.0 - B2 ** STEP

    def body(p_ref, w_ref, m_ref, v_ref, g_ref, d_ref, nm_ref, nv_ref):
        g = p_ref[0].astype(F32)
        for dev in range(1, N_DEV):
            g = g + p_ref[dev].astype(F32)
        nm = B1 * m_ref[...] + (1.0 - B1) * g
        nv = B2 * v_ref[...] + (1.0 - B2) * jnp.square(g)
        g_ref[...] = g
        nm_ref[...] = nm
        nv_ref[...] = nv
        d_ref[...] = -LR * ((nm / c1) / (jnp.sqrt(nv / c2) + ADAM_EPS) + WD * w_ref[...])

    blk = pl.BlockSpec((tr, C), lambda i: (i, 0))
    return pl.pallas_call(
        body, name=name, grid=(R // tr,),
        in_specs=[pl.BlockSpec((N_DEV, tr, C), lambda i: (0, i, 0)), blk, blk, blk],
        out_specs=[blk] * 4, out_shape=[jax.ShapeDtypeStruct((R, C), F32)] * 4,
        compiler_params=_cparams(("parallel",)),
    )(parts, w, m, v)


SMALL = ("mix_norm", "attn_q_gain", "attn_k_gain", "dn_a_log", "dn_dt_bias", "dn_o_gain", "mlp_norm", "ple_norm")
WEIGHTS = ("mix_norm", "attn_w_qkv", "attn_q_gain", "attn_k_gain", "attn_w_o", "dn_w_in", "dn_conv", "dn_a_log",
           "dn_dt_bias", "dn_o_gain", "dn_w_o", "mlp_norm", "w_up", "w_down", "ple_norm", "w_ple", "w_ple_gate")


def _to_rows(flat, multiple):
    n = flat.shape[-1]
    rows = -(-n // (LANE * multiple)) * multiple
    return jnp.pad(flat, [(0, rows * LANE - n)]).reshape(rows, LANE)


def _cols_to_devices(w):
    K, N = w.shape
    return jnp.transpose(w.reshape(K, N_DEV, N // N_DEV), (1, 0, 2))


def _cols_from_devices(g):
    _, K, n = g.shape
    return jnp.transpose(g, (1, 0, 2)).reshape(K, N_DEV * n)


SMALL_ROWS = 96


def _pack_small(vals, loss_rows):
    rows = [_to_rows(vals[n].reshape(-1), SUBLANE) for n in SMALL] + [loss_rows]
    buf = jnp.concatenate(rows, 0)
    assert buf.shape == (SMALL_ROWS, LANE)
    return buf


def _unpack_small(buf, like):
    out, r = {}, 0
    for n in SMALL:
        sz = math.prod(like[n].shape)
        out[n] = buf[r:r + -(-sz // LANE)].reshape(-1)[:sz].reshape(like[n].shape)
        r += -(-sz // (LANE * SUBLANE)) * SUBLANE
    return out


def kernel(x, p, positions, mix_norm, attn_w_qkv, attn_q_gain, attn_k_gain, attn_w_o, dn_w_in, dn_conv, dn_a_log, dn_dt_bias, dn_o_gain, dn_w_o, mlp_norm, w_up, w_down, ple_norm, w_ple, w_ple_gate, loss_target, m_mix_norm, m_attn_w_qkv, m_attn_q_gain, m_attn_k_gain, m_attn_w_o, m_dn_w_in, m_dn_conv, m_dn_a_log, m_dn_dt_bias, m_dn_o_gain, m_dn_w_o, m_mlp_norm, m_w_up, m_w_down, m_ple_norm, m_w_ple, m_w_ple_gate, v_mix_norm, v_attn_w_qkv, v_attn_q_gain, v_attn_k_gain, v_attn_w_o, v_dn_w_in, v_dn_conv, v_dn_a_log, v_dn_dt_bias, v_dn_o_gain, v_dn_w_o, v_mlp_norm, v_w_up, v_w_down, v_ple_norm, v_w_ple, v_w_ple_gate):
    w = dict(mix_norm=mix_norm, attn_w_qkv=attn_w_qkv, attn_q_gain=attn_q_gain, attn_k_gain=attn_k_gain, attn_w_o=attn_w_o,
             dn_w_in=dn_w_in, dn_conv=dn_conv, dn_a_log=dn_a_log, dn_dt_bias=dn_dt_bias, dn_o_gain=dn_o_gain, dn_w_o=dn_w_o,
             mlp_norm=mlp_norm, w_up=w_up, w_down=w_down, ple_norm=ple_norm, w_ple=w_ple, w_ple_gate=w_ple_gate)
    m = dict(mix_norm=m_mix_norm, attn_w_qkv=m_attn_w_qkv, attn_q_gain=m_attn_q_gain, attn_k_gain=m_attn_k_gain,
             attn_w_o=m_attn_w_o, dn_w_in=m_dn_w_in, dn_conv=m_dn_conv, dn_a_log=m_dn_a_log, dn_dt_bias=m_dn_dt_bias,
             dn_o_gain=m_dn_o_gain, dn_w_o=m_dn_w_o, mlp_norm=m_mlp_norm, w_up=m_w_up, w_down=m_w_down,
             ple_norm=m_ple_norm, w_ple=m_w_ple, w_ple_gate=m_w_ple_gate)
    v = dict(mix_norm=v_mix_norm, attn_w_qkv=v_attn_w_qkv, attn_q_gain=v_attn_q_gain, attn_k_gain=v_attn_k_gain,
             attn_w_o=v_attn_w_o, dn_w_in=v_dn_w_in, dn_conv=v_dn_conv, dn_a_log=v_dn_a_log, dn_dt_bias=v_dn_dt_bias,
             dn_o_gain=v_dn_o_gain, dn_w_o=v_dn_w_o, mlp_norm=v_mlp_norm, w_up=v_w_up, w_down=v_w_down,
             ple_norm=v_ple_norm, w_ple=v_w_ple, w_ple_gate=v_w_ple_gate)
    S = x.shape[1]

    bf = lambda a: a.astype(BF16)
    rows_to_devices = lambda t: t.reshape(N_DEV, t.shape[0] // N_DEV, t.shape[1])

    (g_qkv, g_ao), token = _all_gather("gather_attn", [bf(attn_w_qkv[0]), bf(attn_w_o[0])], [(0, None), (1, None)])
    rest_shards = [bf(dn_w_in[0]), bf(dn_w_o[0]), bf(w_up), bf(w_down), bf(w_ple), bf(w_ple_gate), _after(dn_conv[0], token)]
    rest_streams = [(0, None), (1, None), (2, 0), (2, 1), (3, 0), (3, 1), (4, 0), (4, 1), (5, 0), (5, 1), (6, None)]
    rest_arrived, token = _gather_async("gather_rest", rest_shards, rest_streams)
    W = dict(attn_w_qkv=_unpack_cols("unpack_attn_qkv", g_qkv), attn_w_o=_cols_from_devices(g_ao))

    def rest_of_weights(after):
        g_in, g_do, g_up0, g_up1, g_dn0, g_dn1, g_pl0, g_pl1, g_gt0, g_gt1, g_conv = rest_arrived(after)
        rest = dict(
            dn_conv=jnp.transpose(g_conv, (1, 0, 2)).reshape(CONV_W, DN_QKV), dn_w_o=g_do.reshape(DN_WIDTH, D_MODEL),
            w_up=[_cols_from_devices(g_up0), _cols_from_devices(g_up1)],
            w_down=[g_dn0.reshape(D_FF, D_MODEL), g_dn1.reshape(D_FF, D_MODEL)],
            w_ple=[_cols_from_devices(g_pl0), _cols_from_devices(g_pl1)],
            w_ple_gate=[g_gt0.reshape(D_MODEL, D_MODEL), g_gt1.reshape(D_MODEL, D_MODEL)])
        rest["dn_w_qkvz"], rest["dn_w_ab"] = _unpack_dn_in(g_in)
        return rest

    pending = {}

    def mlp_sends(g):
        return [_cols_to_devices(g["w_up"]), rows_to_devices(g["w_down"]), _cols_to_devices(g["w_ple"]),
                rows_to_devices(g["w_ple_gate"])]

    def start(tag, sends):
        pending[tag], token = _exchange_async(f"exchange_{tag}", sends)
        return token

    def send_layer1(g):
        conv_send = jnp.transpose(g["dn_conv"].reshape(CONV_W, N_DEV, DN_QKV // N_DEV), (1, 0, 2))
        return start("layer1", [_pack_dn_in(g["dn_w_qkvz"], g["dn_w_ab"]), conv_send, rows_to_devices(g["dn_w_o"])] + mlp_sends(g))

    def send_mlp0(g):
        return start("mlp0", mlp_sends(g))

    def send_attn(g):
        return start("attn", [_pack_cols("pack_attn_qkv", g["attn_w_qkv"]), _cols_to_devices(g["attn_w_o"])])

    P = dict(mix_norm=_after(mix_norm, token), attn_q_gain=attn_q_gain[0], attn_k_gain=attn_k_gain[0], dn_a_log=dn_a_log[0],
             dn_dt_bias=dn_dt_bias[0], dn_o_gain=dn_o_gain[0], mlp_norm=mlp_norm, ple_norm=ple_norm)

    sq, dx0, small_g = _local_step(x[0], p[:, 0], positions.reshape(S, 1), loss_target[0], W, P,
                                   rest_of_weights, send_layer1, send_mlp0, send_attn)

    r_in, r_conv, r_do, r_up1, r_dn1, r_pl1, r_gt1 = pending["layer1"](dx0)
    r_up0, r_dn0, r_pl0, r_gt0 = pending["mlp0"](dx0)
    r_qkv, r_ao = pending["attn"](dx0)
    big = {}
    for n, parts in (("attn_w_qkv", [r_qkv]), ("attn_w_o", [r_ao]), ("dn_w_in", [r_in]), ("dn_conv", [r_conv]),
                     ("dn_w_o", [r_do]), ("w_up", [r_up0, r_up1]), ("w_down", [r_dn0, r_dn1]),
                     ("w_ple", [r_pl0, r_pl1]), ("w_ple_gate", [r_gt0, r_gt1])):
        layers = [_adamw(f"adamw_{n}{l}", pt, w[n][l], m[n][l], v[n][l]) for l, pt in enumerate(parts)]
        big[n] = [jnp.stack([res[k] for res in layers]) for k in range(4)]

    loss_rows = jnp.pad((0.5 / D_MODEL) * jnp.sum(sq, axis=1, keepdims=True), ((0, SUBLANE - 1), (0, LANE - 1)))
    small_like = {n: w[n] for n in SMALL}
    parts_s = _all_gather("gather_small", [_pack_small(small_g, loss_rows)], [(0, None)])[0][0]
    zero_rows = jnp.zeros((SUBLANE, LANE), F32)
    small = _adamw("adamw_small", parts_s, _pack_small(w, zero_rows), _pack_small(m, zero_rows), _pack_small(v, zero_rows))
    loss = small[0][SMALL_ROWS - SUBLANE, 0]
    small = [_unpack_small(b, small_like) for b in small]

    outs = [loss, dx0[None]]
    for k in range(4):
        for n in WEIGHTS:
            outs.append(small[k][n] if n in SMALL else big[n][k])
    return tuple(outs)
```

```python
import functools
import math

import jax
import jax.numpy as jnp
from jax import lax
from jax.experimental import pallas as pl
from jax.experimental.pallas import tpu as pltpu

F32 = jnp.float32
BF16 = jnp.bfloat16
HIGHEST = lax.Precision.HIGHEST

N_DEV = 8
D_MODEL = 1024
EPS = 1e-6
SWA_GROUPS = ((128, 1), (512, 4), (2048, 16))
A_HEADS = 8
A_HEAD_DIM = 64
A_WIDTH = A_HEADS * A_HEAD_DIM
A_QKV = 3 * 3 * A_WIDTH
ROPE_DIM = 16
ROPE_HALF = 8
ROPE_THETA = 500000.0
BAND = 128
DN_HEADS = 8
DN_DIM = 128
DN_WIDTH = DN_HEADS * DN_DIM
CONV_W = 4
CHUNK = 64
D_FF = 4 * D_MODEL
PLE_DIM = 256
LR, B1, B2, ADAM_EPS, WD, STEP = 0.001, 0.9, 0.999, 1e-08, 0.01, 10

VMEM_LIMIT = 56 * 1024 * 1024
MXU_TILE = 1024
MM_SLAB = 256
LANE = 128
SUBLANE = 8


def _cparams(sem):
    return pltpu.CompilerParams(dimension_semantics=sem, vmem_limit_bytes=VMEM_LIMIT)


def _tile(n, pref):
    if n <= pref:
        return n
    t = (pref // LANE) * LANE
    while t >= LANE:
        if n % t == 0:
            return t
        t -= LANE
    raise ValueError(f"no tile for {n}")


def _dot(a, b, ca=1, cb=0, precision=None):
    return lax.dot_general(a, b, (((ca,), (cb,)), ((), ())), precision=precision,
                           preferred_element_type=F32)


def _bdot(a, b, ca=1, cb=0):
    return _dot(a.astype(BF16), b.astype(BF16), ca, cb)


def _mm(name, a, b, *, ta=False, tb=False, epilogue=None, extras=(), out_dtypes=(F32,), n_colsums=0,
        tm_pref=MXU_TILE, tn_pref=1536, tk_pref=MXU_TILE):
    M, K = (a.shape[1], a.shape[0]) if ta else a.shape
    N = b.shape[0] if tb else b.shape[1]
    assert (b.shape[1] if tb else b.shape[0]) == K
    tm, tn, tk = _tile(M, tm_pref), _tile(N, tn_pref), _tile(K, tk_pref)
    nk = K // tk
    n_out = len(out_dtypes)
    n_ext = len(extras)
    assert n_colsums == 0 or tn == N
    sub = min(tm, MM_SLAB)

    def body(*refs):
        a_ref, b_ref = refs[0], refs[1]
        ext = refs[2:2 + n_ext]
        outs = refs[2 + n_ext:2 + n_ext + n_out]
        sums = refs[2 + n_ext + n_out:2 + n_ext + n_out + n_colsums]
        row_tile, k = pl.program_id(0), pl.program_id(2)
        slabs = [slice(s * sub, (s + 1) * sub) for s in range(tm // sub)]

        def product(rows):
            return _bdot(a_ref[:, rows] if ta else a_ref[rows, :], b_ref[...], 0 if ta else 1, 1 if tb else 0)

        def finish(results):
            col_rows = []
            for rows, r in zip(slabs, results):
                res = (r,) if epilogue is None else epilogue(r, *[e[...] if e.shape[0] == 1 else e[rows, :] for e in ext])
                for o, v in zip(outs, res):
                    o[rows, :] = v.astype(o.dtype)
                col_rows.append(res[n_out:])
            for n, o in enumerate(sums):
                v = functools.reduce(lambda x, y: x + y, [c[n] for c in col_rows])

                @pl.when(row_tile == 0)
                def _(o=o, v=v):
                    o[...] = v

                @pl.when(row_tile > 0)
                def _(o=o, v=v):
                    o[...] += v

        if nk == 1:
            finish([product(rows) for rows in slabs])
            return
        acc = refs[-1]

        @pl.when(k == 0)
        def _():
            acc[...] = jnp.zeros_like(acc)

        for rows in slabs:
            acc[rows, :] += product(rows)

        @pl.when(k == nk - 1)
        def _():
            finish([acc[rows, :] for rows in slabs])

    a_spec = pl.BlockSpec((tk, tm), lambda i, j, k: (k, i)) if ta else pl.BlockSpec((tm, tk), lambda i, j, k: (i, k))
    b_spec = pl.BlockSpec((tn, tk), lambda i, j, k: (j, k)) if tb else pl.BlockSpec((tk, tn), lambda i, j, k: (k, j))
    ext_specs = []
    for e in extras:
        if e.shape[0] == 1 and M != 1:
            ext_specs.append(pl.BlockSpec((1, tn), lambda i, j, k: (0, j)))
        else:
            ext_specs.append(pl.BlockSpec((tm, tn), lambda i, j, k: (i, j)))
    out = pl.pallas_call(
        body, name=name,
        grid=(M // tm, N // tn, nk),
        in_specs=[a_spec, b_spec] + ext_specs,
        out_specs=[pl.BlockSpec((tm, tn), lambda i, j, k: (i, j)) for _ in range(n_out)]
        + [pl.BlockSpec((1, tn), lambda i, j, k: (0, 0)) for _ in range(n_colsums)],
        out_shape=[jax.ShapeDtypeStruct((M, N), dt) for dt in out_dtypes]
        + [jax.ShapeDtypeStruct((1, N), F32) for _ in range(n_colsums)],
        scratch_shapes=[pltpu.VMEM((tm, tn), F32)] if nk > 1 else [],
        compiler_params=_cparams(("arbitrary" if n_colsums else "parallel", "parallel", "arbitrary")),
    )(a, b, *extras)
    return out[0] if len(out) == 1 else tuple(out)


def _perm_matrices(tr, d):
    import numpy as np
    old = np.arange(tr)
    p = np.zeros((tr, tr), np.float32)
    p[(old % d) * (tr // d) + old // d, old] = 1.0
    return jnp.asarray(p, BF16), jnp.asarray(p.T, BF16)


def _permute(p, x):
    if x.dtype == BF16:
        return _dot(p, x)
    hi = x.astype(BF16)
    rest = x - hi.astype(F32)
    mid = rest.astype(BF16)
    lo = (rest - mid.astype(F32)).astype(BF16)
    return _dot(p, hi) + _dot(p, mid) + _dot(p, lo)


def _rows(name, fn, ins, outs, *, tr, accs=()):
    ins = [(e[0], e[1]) + (e[2] if len(e) > 2 else (0, e[0].shape[-1])) for e in ins]
    outs = [tuple(o) + (0,) * (3 - len(o)) for o in outs]
    n_rows = next(e[0].shape[0] if e[1] == "row" else e[0].shape[0] * e[0].shape[1] for e in ins if e[1] in ("row", "res"))
    assert n_rows % tr == 0 and tr % SUBLANE == 0
    steps = n_rows // tr
    t8 = tr // SUBLANE
    n8 = n_rows // SUBLANE
    dils = sorted({e[0].shape[0] for e in ins if e[1] == "res" and e[0].shape[0] > 1} | {o[2] for o in outs if o[2] > 1})
    perms = [m for d in dils for m in _perm_matrices(tr, d)]
    ins = ins + [(m, "full", 0, tr) for m in perms]
    n_in, n_out, n_acc = len(ins), len(outs), len(accs)

    def body(*refs):
        i = pl.program_id(0)
        to_res = {d: refs[n_in - len(perms) + 2 * j][...] for j, d in enumerate(dils)}
        to_tok = {d: refs[n_in - len(perms) + 2 * j + 1][...] for j, d in enumerate(dils)}
        tiles = []
        for r, e in zip(refs[:n_in - len(perms)], ins):
            d = e[0].shape[0] if e[1] == "res" else 0
            if d == 0:
                tiles.append(r[...])
            elif d == 1:
                tiles.append(r[0])
            else:
                tiles.append(_permute(to_tok[d], jnp.concatenate([r[j] for j in range(d)], axis=0)))
        vals = fn(i, steps, *tiles)
        if not isinstance(vals, (tuple, list)):
            vals = (vals,)
        assert len(vals) == n_out + n_acc
        for o, v, (_, dt, d) in zip(refs[n_in:n_in + n_out], vals[:n_out], outs):
            if d == 0:
                o[...] = v.astype(o.dtype)
            elif d == 1:
                o[0] = v.astype(o.dtype)
            else:
                y = _permute(to_res[d], v.astype(dt))
                for j in range(d):
                    o[j] = y[j * (tr // d):(j + 1) * (tr // d)].astype(o.dtype)
        if n_acc:
            acc_refs = refs[n_in + n_out:]

            @pl.when(i == 0)
            def _():
                for r in acc_refs:
                    r[...] = jnp.zeros_like(r)

            for r, v in zip(acc_refs, vals[n_out:]):
                r[...] += v.astype(r.dtype)

    in_specs = []
    for a, kind, cb, c in ins:
        if kind == "row":
            in_specs.append(pl.BlockSpec((tr, c), lambda i, cb=cb: (i, cb)))
        elif kind == "full":
            in_specs.append(pl.BlockSpec(a.shape, lambda i, z=(0,) * a.ndim: z))
        elif kind == "prev8":
            in_specs.append(pl.BlockSpec((SUBLANE, c), lambda i, cb=cb: (jnp.maximum(i * t8 - 1, 0), cb)))
        elif kind == "next8":
            in_specs.append(pl.BlockSpec((SUBLANE, c), lambda i, cb=cb: (jnp.minimum((i + 1) * t8, n8 - 1), cb)))
        elif kind == "res":
            d = a.shape[0]
            in_specs.append(pl.BlockSpec((d, tr // d, a.shape[2]), lambda i: (0, i, 0)))
        else:
            raise ValueError(kind)
    out_specs = [pl.BlockSpec((tr, c), lambda i: (i, 0)) if d == 0 else pl.BlockSpec((d, tr // d, c), lambda i: (0, i, 0))
                 for c, _, d in outs]
    out_specs += [pl.BlockSpec(s, lambda i, z=(0,) * len(s): z) for s, _ in accs]
    out_shape = [jax.ShapeDtypeStruct((n_rows, c) if d == 0 else (d, n_rows // d, c), dt) for c, dt, d in outs]
    out_shape += [jax.ShapeDtypeStruct(s, dt) for s, dt in accs]
    res = pl.pallas_call(
        body, name=name, grid=(steps,), in_specs=in_specs, out_specs=out_specs, out_shape=out_shape,
        compiler_params=_cparams(("arbitrary",) if n_acc else ("parallel",)),
    )(*[e[0] for e in ins])
    return res[0] if len(res) == 1 else tuple(res)


def _colsum(x):
    return jnp.sum(x, axis=0, keepdims=True)


def _sum_all(x):
    return jnp.sum(jnp.sum(x, axis=1, keepdims=True), axis=0, keepdims=True)


def _rmsnorm_fwd(name, x, gain):
    def fn(i, n, xt, g):
        r = lax.rsqrt(jnp.mean(xt * xt, axis=-1, keepdims=True) + EPS)
        return (xt * r * g,)
    return _rows(name, fn, [(x, "row"), (gain, "full")], [(x.shape[1], BF16)], tr=512)


FUSED_ROWS = 1024


def _res_norm(acc, res, g):
    x = res + acc
    return x, x * lax.rsqrt(jnp.mean(x * x, axis=-1, keepdims=True) + EPS) * g


def _norm_bwd(dh, x, g, dres):
    r = lax.rsqrt(jnp.mean(x * x, axis=-1, keepdims=True) + EPS)
    xh = x * r
    dxn = dh * g
    dx = dres + r * (dxn - xh * jnp.mean(dxn * xh, axis=-1, keepdims=True))
    return dx, _colsum(dh * xh)


def _norm_bwd_2(dh, x, g, dres):
    dx, dg = _norm_bwd(dh, x, g, dres)
    return dx, dx, dg


def _head_consts():
    import numpy as np
    e = np.arange(A_WIDTH) % A_HEAD_DIM
    inv = (np.float32(ROPE_THETA) ** (-np.arange(0, ROPE_DIM, 2, dtype=np.float32) / np.float32(ROPE_DIM))).astype(np.float32)
    c = np.zeros((8, A_WIDTH), np.float32)
    c[0] = np.where(e < ROPE_DIM, inv[e % ROPE_HALF], 0.0)
    c[1] = np.where(e < ROPE_HALF, -1.0, np.where(e < ROPE_DIM, 1.0, 0.0))
    c[2] = (e < ROPE_HALF).astype(np.float32)
    c[3] = (e < ROPE_DIM).astype(np.float32)
    return jnp.asarray(c)


def _block_diag(scale):
    import numpy as np
    h = np.arange(A_WIDTH) // A_HEAD_DIM
    return jnp.asarray((h[:, None] == h[None, :]).astype(np.float32) * scale, dtype=BF16)


def _seg_sum(x, bd):
    return _dot(x.astype(BF16), bd)


def _rope_tables(positions, consts):
    def fn(i, n, pos, c):
        ang = pos.astype(F32) * c[0:1, :LANE]
        return jnp.cos(ang), jnp.sin(ang) * c[1:2, :LANE]
    return _rows("rope_tables", fn, [(positions, "row"), (consts, "full")], [(LANE, F32), (LANE, F32)], tr=512)


def _rope_wide(t):
    return jnp.concatenate([t] * (A_WIDTH // LANE), axis=1)


def _rope_apply(y, ct, st, low):
    rolled = jnp.where(low, pltpu.roll(y, A_WIDTH - ROPE_HALF, 1), pltpu.roll(y, ROPE_HALF, 1))
    return y * ct + rolled * st


def _rope_apply_bwd(dout, ct, st, low, in16):
    t = dout * st
    back = jnp.where(low, pltpu.roll(t, A_WIDTH - ROPE_HALF, 1), jnp.where(in16, pltpu.roll(t, ROPE_HALF, 1), 0.0))
    return dout * ct + back


def _attn_prep(qkv, gains, ct, st, consts, bd):
    def fn(i, n, t, g, c_t, s_t, c, b):
        low = c[2:3, :] > 0.5
        c_t, s_t = _rope_wide(c_t), _rope_wide(s_t)
        groups = []
        for grp in range(3):
            cols = []
            for which in range(3):
                off = (grp * 3 + which) * A_WIDTH
                x = t[:, off:off + A_WIDTH].astype(F32)
                if which == 2:
                    cols.append(x.astype(BF16))
                    continue
                r = lax.rsqrt(_seg_sum(x * x, b) + EPS)
                y = x * r * g[grp * 2 + which:grp * 2 + which + 1, :]
                cols.append(_rope_apply(y, c_t, s_t, low).astype(BF16))
            groups.append(jnp.concatenate(cols, axis=1))
        return tuple(groups)
    return _rows("attn_prep", fn, [(qkv, "row"), (gains, "full"), (ct, "row"), (st, "row"), (consts, "full"), (bd, "full")],
                 [(3 * A_WIDTH, BF16, d) for _, d in SWA_GROUPS], tr=256)


def _band_mask(n):
    row = lax.broadcasted_iota(jnp.int32, (BAND, 2 * BAND), 0)
    col = lax.broadcasted_iota(jnp.int32, (BAND, 2 * BAND), 1)
    dist = row + BAND - col
    return (dist >= 0) & (dist <= BAND) & ((col >= BAND) | (n > 0))


def _attn_fwd(qkvn, grp):
    d, L, _ = qkvn.shape
    nblk = L // BAND
    assert L % BAND == 0 and d == SWA_GROUPS[grp][1]

    def body(q_ref, kc_ref, kp_ref, vc_ref, vp_ref, o_ref, lse_ref):
        n = pl.program_id(1)
        valid = _band_mask(n)
        first = lax.broadcasted_iota(jnp.int32, (BAND, LANE), 1) < A_HEAD_DIM
        pairs = [slice(pr * LANE, (pr + 1) * LANE) for pr in range(A_WIDTH // LANE)]
        halves = (first, jnp.logical_not(first))
        qps = [q_ref[:, sl] for sl in pairs]
        kcats = [jnp.concatenate([kp_ref[:, sl], kc_ref[:, sl]], axis=0) for sl in pairs]
        vcats = [jnp.concatenate([vp_ref[:, sl], vc_ref[:, sl]], axis=0) for sl in pairs]
        heads = [(pr, m) for pr in range(len(pairs)) for m in halves]
        ss = [_dot(jnp.where(m, qps[pr], jnp.zeros_like(qps[pr])), kcats[pr], 1, 1) for pr, m in heads]
        ps, lses = [], []
        for s in ss:
            s = jnp.where(valid, s * (A_HEAD_DIM ** -0.5), -1e30)
            mx = jnp.max(s, axis=-1, keepdims=True)
            e = jnp.exp(s - mx)
            l = jnp.sum(e, axis=-1, keepdims=True)
            ps.append((e / l).astype(BF16))
            lses.append(mx + jnp.log(l))
        os_ = [_dot(p, vcats[pr]) for p, (pr, _) in zip(ps, heads)]
        o_ref[...] = jnp.concatenate([jnp.where(first, os_[2 * pr], os_[2 * pr + 1]) for pr in range(len(pairs))], axis=1)
        lse_ref[...] = jnp.concatenate([jnp.where(first, lses[2 * pr], lses[2 * pr + 1]) for pr in range(len(pairs))], axis=1)

    blk = (None, BAND, A_WIDTH)
    return pl.pallas_call(
        body, name=f"attn_fwd_g{grp}", grid=(d, nblk),
        in_specs=[pl.BlockSpec(blk, lambda r, n: (r, n, 0)),
                  pl.BlockSpec(blk, lambda r, n: (r, n, 1)),
                  pl.BlockSpec(blk, lambda r, n: (r, jnp.maximum(n - 1, 0), 1)),
                  pl.BlockSpec(blk, lambda r, n: (r, n, 2)),
                  pl.BlockSpec(blk, lambda r, n: (r, jnp.maximum(n - 1, 0), 2))],
        out_specs=[pl.BlockSpec(blk, lambda r, n: (r, n, 0)), pl.BlockSpec(blk, lambda r, n: (r, n, 0))],
        out_shape=[jax.ShapeDtypeStruct((d, L, A_WIDTH), F32)] * 2,
        compiler_params=_cparams(("parallel", "parallel")),
    )(qkvn, qkvn, qkvn, qkvn, qkvn)


def _merge_weights(l0, l1, l2):
    mx = jnp.maximum(jnp.maximum(l0, l1), l2)
    e0, e1, e2 = jnp.exp(l0 - mx), jnp.exp(l1 - mx), jnp.exp(l2 - mx)
    inv = 1.0 / (e0 + e1 + e2)
    return e0 * inv, e1 * inv, e2 * inv


def _attn_merge(os_, lses):
    def fn(i, n, o0, o1, o2, l0, l1, l2):
        w0, w1, w2 = _merge_weights(l0, l1, l2)
        return (w0 * o0 + w1 * o1 + w2 * o2,)
    ins = [(a, "res") for a in (*os_, *lses)]
    return _rows("attn_merge", fn, ins, [(A_WIDTH, BF16)], tr=256)


def _attn_merge_bwd(do, os_, lses, bd1):
    def fn(i, n, dot_, o0, o1, o2, l0, l1, l2, b):
        w0, w1, w2 = _merge_weights(l0, l1, l2)
        o = w0 * o0 + w1 * o1 + w2 * o2
        dsum = _seg_sum(dot_ * o, b)
        return (w0 * dot_, w1 * dot_, w2 * dot_, -w0 * dsum, -w1 * dsum, -w2 * dsum)
    ins = [(do, "row")] + [(a, "res") for a in (*os_, *lses)] + [(bd1, "full")]
    res = _rows("attn_merge_bwd", fn, ins, [(A_WIDTH, dt, d) for dt in (BF16, F32) for _, d in SWA_GROUPS], tr=256)
    return res[:3], res[3:]


def _lane_pick(x, lane_idx, lane):
    return jnp.sum(jnp.where(lane_idx == lane, x, 0.0), axis=-1, keepdims=True)


def _attn_bwd(qkvn, grp, do_g, lse, c_g):
    d, L, _ = qkvn.shape
    nblk = L // BAND

    def body(q_ref, kc_ref, kp_ref, vc_ref, vp_ref, do_ref, lse_ref, c_ref, dq_ref, dk_ref, dv_ref, ck, cv_):
        n = pl.program_id(1)

        @pl.when(n == 0)
        def _():
            ck[...] = jnp.zeros_like(ck)
            cv_[...] = jnp.zeros_like(cv_)

        @pl.when(n < nblk)
        def _():
            valid = _band_mask(n)
            lane = lax.broadcasted_iota(jnp.int32, (BAND, LANE), 1)
            first = lane < A_HEAD_DIM
            lane2 = lax.broadcasted_iota(jnp.int32, (2 * BAND, LANE), 1) < A_HEAD_DIM
            pairs = [slice(pr * LANE, (pr + 1) * LANE) for pr in range(A_WIDTH // LANE)]
            halves = (first, jnp.logical_not(first))
            qps = [q_ref[:, sl] for sl in pairs]
            dops = [do_ref[:, sl] for sl in pairs]
            kcats = [jnp.concatenate([kp_ref[:, sl], kc_ref[:, sl]], axis=0) for sl in pairs]
            vcats = [jnp.concatenate([vp_ref[:, sl], vc_ref[:, sl]], axis=0) for sl in pairs]
            heads = [(pr, hh) for pr in range(len(pairs)) for hh in range(2)]
            zero = jnp.zeros_like(qps[0])
            ss = [_dot(jnp.where(halves[hh], qps[pr], zero), kcats[pr], 1, 1) for pr, hh in heads]
            dps = [_dot(jnp.where(halves[hh], dops[pr], zero), vcats[pr], 1, 1) for pr, hh in heads]
            dss, pbs = [], []
            for (pr, hh), s, dp in zip(heads, ss, dps):
                lse_h = _lane_pick(lse_ref[:, pairs[pr]], lane, hh * A_HEAD_DIM)
                c_h = _lane_pick(c_ref[:, pairs[pr]], lane, hh * A_HEAD_DIM)
                p = jnp.where(valid, jnp.exp(s * (A_HEAD_DIM ** -0.5) - lse_h), 0.0)
                dss.append((p * (dp + c_h) * (A_HEAD_DIM ** -0.5)).astype(BF16))
                pbs.append(p.astype(BF16))
            dqs = [_dot(ds, kcats[pr]) for ds, (pr, _) in zip(dss, heads)]
            dks = [_dot(ds, qps[pr], 0, 0) for ds, (pr, _) in zip(dss, heads)]
            dvs = [_dot(pb, dops[pr], 0, 0) for pb, (pr, _) in zip(pbs, heads)]
            for pr, sl in enumerate(pairs):
                dq_ref[:, sl] = jnp.where(first, dqs[2 * pr], dqs[2 * pr + 1])
                dkc = jnp.where(lane2, dks[2 * pr], dks[2 * pr + 1])
                dvc = jnp.where(lane2, dvs[2 * pr], dvs[2 * pr + 1])
                dk_ref[:, sl] = ck[:, sl] + dkc[:BAND]
                dv_ref[:, sl] = cv_[:, sl] + dvc[:BAND]
                ck[:, sl] = dkc[BAND:]
                cv_[:, sl] = dvc[BAND:]

        @pl.when(n == nblk)
        def _():
            dk_ref[...] = ck[...]
            dv_ref[...] = cv_[...]

    blk = (None, BAND, A_WIDTH)
    last = nblk - 1
    qn = lambda n: jnp.minimum(n, last)
    pn = lambda n: jnp.clip(n - 1, 0, last)
    return tuple(pl.pallas_call(
        body, name=f"attn_bwd_g{grp}", grid=(d, nblk + 1),
        in_specs=[pl.BlockSpec(blk, lambda r, n: (r, qn(n), 0)),
                  pl.BlockSpec(blk, lambda r, n: (r, qn(n), 1)),
                  pl.BlockSpec(blk, lambda r, n: (r, pn(n), 1)),
                  pl.BlockSpec(blk, lambda r, n: (r, qn(n), 2)),
                  pl.BlockSpec(blk, lambda r, n: (r, pn(n), 2)),
                  pl.BlockSpec(blk, lambda r, n: (r, qn(n), 0)),
                  pl.BlockSpec(blk, lambda r, n: (r, qn(n), 0)),
                  pl.BlockSpec(blk, lambda r, n: (r, qn(n), 0))],
        out_specs=[pl.BlockSpec(blk, lambda r, n: (r, qn(n), 0)),
                   pl.BlockSpec(blk, lambda r, n: (r, pn(n), 0)),
                   pl.BlockSpec(blk, lambda r, n: (r, pn(n), 0))],
        out_shape=[jax.ShapeDtypeStruct((d, L, A_WIDTH), F32)] * 3,
        scratch_shapes=[pltpu.VMEM((BAND, A_WIDTH), F32), pltpu.VMEM((BAND, A_WIDTH), F32)],
        compiler_params=_cparams(("parallel", "arbitrary")),
    )(qkvn, qkvn, qkvn, qkvn, qkvn, do_g, lse, c_g))


def _attn_prep_bwd(qkv, grads, gains, ct, st, consts, bd):
    def fn(i, n, t, g, c_t, s_t, c, b, *gr):
        low = c[2:3, :] > 0.5
        in16 = c[3:4, :] > 0.5
        c_t, s_t = _rope_wide(c_t), _rope_wide(s_t)
        cols, dgs = [], []
        for grp in range(3):
            for which in range(3):
                dout = gr[grp * 3 + which]
                if which == 2:
                    cols.append(dout.astype(BF16))
                    continue
                off = (grp * 3 + which) * A_WIDTH
                x = t[:, off:off + A_WIDTH].astype(F32)
                gain = g[grp * 2 + which:grp * 2 + which + 1, :]
                r = lax.rsqrt(_seg_sum(x * x, b) + EPS)
                xh = x * r
                dy = _rope_apply_bwd(dout, c_t, s_t, low, in16)
                dyn = dy * gain
                dx = r * (dyn - xh * _seg_sum(dyn * xh, b))
                cols.append(dx.astype(BF16))
                dgs.append(_colsum(dy * xh))
        return (jnp.concatenate(cols, axis=1), *dgs)
    ins = [(qkv, "row"), (gains, "full"), (ct, "row"), (st, "row"), (consts, "full"), (bd, "full")] + [(a, "res") for a in grads]
    res = _rows("attn_prep_bwd", fn, ins, [(A_QKV, BF16)], tr=128, accs=[((1, A_WIDTH), F32)] * 6)
    return res[0], res[1:]


DN_QKV = 3 * DN_WIDTH
DN_QKVZ = DN_QKV + DN_WIDTH


def _sigmoid(x):
    return 1.0 / (1.0 + jnp.exp(-x))


def _softplus(x):
    return jnp.maximum(x, 0.0) + jnp.log(1.0 + jnp.exp(-jnp.abs(x)))


def _conv_taps(xs, w, tr):
    acc = None
    for j in range(CONV_W):
        sh = CONV_W - 1 - j
        term = (pltpu.roll(xs, sh, 0) if sh else xs)[SUBLANE:] * w[j:j + 1, :]
        acc = term if acc is None else acc + term
    return acc


def _dn_prep(qkvz, ab, convw, alog_row, dt_row):
    tr = 256

    def fn(i, n, x, xp, abt, w, al, dt):
        xp = jnp.where(i > 0, xp, 0.0)
        u = _conv_taps(jnp.concatenate([xp, x], axis=0), w, tr)
        y = u * _sigmoid(u)
        qs, ks = [], []
        for h in range(DN_HEADS):
            for dst, base, sc in ((qs, 0, DN_DIM ** -0.5), (ks, DN_WIDTH, 1.0)):
                seg = y[:, base + h * DN_DIM:base + (h + 1) * DN_DIM]
                dst.append(seg * (lax.rsqrt(jnp.sum(seg * seg, axis=-1, keepdims=True) + EPS) * sc))
        lane = lax.broadcasted_iota(jnp.int32, abt.shape, 1)
        g = -jnp.exp(al) * _softplus(abt + dt)
        gb = jnp.where(lane < DN_HEADS, g, jnp.where(lane < 2 * DN_HEADS, _sigmoid(abt), 0.0))
        return u, jnp.concatenate(qs, axis=1), jnp.concatenate(ks, axis=1), y[:, 2 * DN_WIDTH:], gb

    ins = [(qkvz, "row", (0, DN_QKV)), (qkvz, "prev8", (0, DN_QKV)), (ab, "row"), (convw, "full"),
           (alog_row, "full"), (dt_row, "full")]
    return _rows("dn_prep", fn, ins, [(DN_QKV, BF16), (DN_WIDTH, F32), (DN_WIDTH, F32), (DN_WIDTH, F32), (LANE, F32)], tr=tr)


def _tri_masks():
    row = lax.broadcasted_iota(jnp.int32, (CHUNK, CHUNK), 0)
    col = lax.broadcasted_iota(jnp.int32, (CHUNK, CHUNK), 1)
    return row >= col, row > col, row == col


def _heads(fn, *lists):
    return [fn(*xs) for xs in zip(*lists)]


def _split(x):
    hi = x.astype(BF16)
    return hi, (x - hi.astype(F32)).astype(BF16)


def _dot3(a, b, ca=1, cb=0):
    (ah, al), (bh, bl) = a, b
    return _dot(ah, bh, ca, cb) + (_dot(ah, bl, ca, cb) + _dot(al, bh, ca, cb))


SPLIT_STEPS = 3


def _unit_lower_inverse(a_list, eye):
    ts = [eye - a for a in a_list]
    parts = [_split(a) for a in a_list]
    for step in range(5):
        if step < SPLIT_STEPS:
            parts = [_split(_dot3(p, p)) for p in parts]
            ts = [t + _dot3(_split(t), p) for t, p in zip(ts, parts)]
        else:
            parts = [(_dot(p[0], p[0]).astype(BF16), None) for p in parts]
            ts = [t + _dot(t.astype(BF16), p[0]) for t, p in zip(ts, parts)]
    return ts


def _dn_terms(qs, ks, vs, gb, solved=None):
    lower, strict, diag = _tri_masks()
    lane = lax.broadcasted_iota(jnp.int32, (CHUNK, LANE), 1)
    is_last = lax.broadcasted_iota(jnp.int32, (CHUNK, 1), 0) == CHUNK - 1
    hs = range(DN_HEADS)
    gc = _dot(lower.astype(F32), gb, precision=HIGHEST)
    gct = jnp.transpose(gc)
    bcol = [_lane_pick(gb, lane, DN_HEADS + h) for h in hs]
    gcol = [_lane_pick(gc, lane, h) for h in hs]
    glast = [jnp.sum(jnp.where(is_last, g, 0.0), axis=0, keepdims=True) for g in gcol]
    decay = [jnp.exp(jnp.where(lower, gcol[h] - gct[h:h + 1, :], -1e30)) for h in hs]
    kb = _heads(lambda k, b: k * b, ks, bcol)
    kk = _heads(lambda x, k: _bdot(x, k, 1, 1), kb, ks)
    qk = _heads(lambda q, k: _bdot(q, k, 1, 1), qs, ks)
    a = _heads(lambda x, d: jnp.where(strict, x * d, 0.0), kk, decay)
    eg = [jnp.exp(g) for g in gcol]
    egl = _heads(lambda gl, g: jnp.exp(gl - g), glast, gcol)
    rhs_w = _heads(lambda x, e: x * e, kb, eg)
    if solved is None:
        t_full = _unit_lower_inverse(a, diag.astype(F32))
        t = [_split(x) for x in t_full]
        u = _heads(lambda tt, v, b: _dot3(tt, _split(v * b)), t, vs, bcol)
        w = _heads(lambda tt, r: _dot3(tt, _split(r)), t, rhs_w)
    else:
        t_full, u, w = solved
        t = [_split(x) for x in t_full]
    return dict(bcol=bcol, decay=decay, kb=kb, a=a, t=t, t_full=t_full, eg=eg, egl=egl, rhs_w=rhs_w, u=u, w=w,
                attn=_heads(lambda x, d: x * d, qk, decay), q_dec=_heads(lambda q, e: q * e, qs, eg),
                k_dec=_heads(lambda k, e: k * e, ks, egl), c_dec=[jnp.exp(g) for g in glast],
                lower=lower, strict=strict, lane=lane, is_last=is_last)


def _head_slices(ref):
    return [ref[:, h * DN_DIM:(h + 1) * DN_DIM] for h in range(DN_HEADS)]


def _dn_chunk_fwd(q, k, v, gb):
    S = q.shape[0]
    N = S // CHUNK

    def body(q_ref, k_ref, v_ref, gb_ref, o_ref, st_ref, t_ref, u_ref, w_ref, state):
        @pl.when(pl.program_id(0) == 0)
        def _():
            state[...] = jnp.zeros_like(state)

        f = _dn_terms(_head_slices(q_ref), _head_slices(k_ref), _head_slices(v_ref), gb_ref[...])
        s = [state[h] for h in range(DN_HEADS)]
        for h in range(DN_HEADS):
            st_ref[0, h] = s[h]
            t_ref[0, h] = f["t_full"][h]
            u_ref[:, h * DN_DIM:(h + 1) * DN_DIM] = f["u"][h]
            w_ref[:, h * DN_DIM:(h + 1) * DN_DIM] = f["w"][h]
        sb = [x.astype(BF16) for x in s]
        v_new = _heads(lambda u, w, x: u - _bdot(w, x), f["u"], f["w"], sb)
        o = _heads(lambda qd, x, at, vn: _bdot(qd, x) + _bdot(at, vn), f["q_dec"], sb, f["attn"], v_new)
        new_s = _heads(lambda x, c, kd, vn: x * c + _bdot(kd, vn, 0, 0), s, f["c_dec"], f["k_dec"], v_new)
        for h in range(DN_HEADS):
            o_ref[:, h * DN_DIM:(h + 1) * DN_DIM] = o[h]
            state[h] = new_s[h]

    blk = pl.BlockSpec((CHUNK, DN_WIDTH), lambda n: (n, 0))
    st_blk = pl.BlockSpec((1, DN_HEADS, DN_DIM, DN_DIM), lambda n: (n, 0, 0, 0))
    t_blk = pl.BlockSpec((1, DN_HEADS, CHUNK, CHUNK), lambda n: (n, 0, 0, 0))
    wide = jax.ShapeDtypeStruct((S, DN_WIDTH), F32)
    o, states, t, u, w = pl.pallas_call(
        body, name="dn_chunk_fwd", grid=(N,),
        in_specs=[blk, blk, blk, pl.BlockSpec((CHUNK, LANE), lambda n: (n, 0))],
        out_specs=[blk, st_blk, t_blk, blk, blk],
        out_shape=[wide, jax.ShapeDtypeStruct((N, DN_HEADS, DN_DIM, DN_DIM), F32),
                   jax.ShapeDtypeStruct((N, DN_HEADS, CHUNK, CHUNK), F32), wide, wide],
        scratch_shapes=[pltpu.VMEM((DN_HEADS, DN_DIM, DN_DIM), F32)],
        compiler_params=_cparams(("arbitrary",)),
    )(q, k, v, gb)
    return o, (states, t, u, w)


def _dn_chunk_bwd(q, k, v, gb, saved, do):
    S = q.shape[0]
    N = S // CHUNK
    states, t_saved, u_saved, w_saved = saved

    def body(q_ref, k_ref, v_ref, gb_ref, st_ref, t_ref, u_ref, w_ref, do_ref, dq_ref, dk_ref, dv_ref, dgb_ref, dstate):
        @pl.when(pl.program_id(0) == 0)
        def _():
            dstate[...] = jnp.zeros_like(dstate)

        hs = range(DN_HEADS)
        qs, ks, vs, dos = (_head_slices(r) for r in (q_ref, k_ref, v_ref, do_ref))
        f = _dn_terms(qs, ks, vs, gb_ref[...], ([t_ref[0, h] for h in hs], _head_slices(u_ref), _head_slices(w_ref)))
        lane, is_last = f["lane"], f["is_last"]
        rowsum = lambda x: jnp.sum(x, axis=-1, keepdims=True)
        s = [st_ref[0, h] for h in hs]
        dsn = [dstate[h] for h in hs]
        sb = [x.astype(BF16) for x in s]
        dsb = [x.astype(BF16) for x in dsn]
        dob = [x.astype(BF16) for x in dos]
        v_new = _heads(lambda u, w, x: u - _bdot(w, x), f["u"], f["w"], sb)
        dv_new = _heads(lambda at, d, kd, x: _bdot(at, d, 0, 0) + _bdot(kd, x), f["attn"], dob, f["k_dec"], dsb)
        dattn = _heads(lambda d, vn: _bdot(d, vn, 1, 1), dob, v_new)
        dq_dec = _heads(lambda d, x: _bdot(d, x, 1, 1), dob, sb)
        dk_dec = _heads(lambda vn, x: _bdot(vn, x, 1, 1), v_new, dsb)
        dw = _heads(lambda dv_, x: -_bdot(dv_, x, 1, 1), dv_new, sb)
        new_ds = _heads(lambda x, c, qd, d, w, dv_: x * c + _bdot(qd, d, 0, 0) - _bdot(w, dv_, 0, 0),
                        dsn, f["c_dec"], f["q_dec"], dob, f["w"], dv_new)
        for h in hs:
            dstate[h] = new_ds[h]
        drhs_u = _heads(lambda tt, x: _dot3(tt, _split(x), 0, 0), f["t"], dv_new)
        drhs_w = _heads(lambda tt, x: _dot3(tt, _split(x), 0, 0), f["t"], dw)
        da = _heads(lambda du_, u, dw_, w: jnp.where(f["strict"], -(_bdot(du_, u, 1, 1) + _bdot(dw_, w, 1, 1)), 0.0),
                    drhs_u, f["u"], drhs_w, f["w"])
        dkk = _heads(lambda x, d: x * d, da, f["decay"])
        dqk = _heads(lambda x, d: x * d, dattn, f["decay"])
        dkb = _heads(lambda x, k_, dw_, e: _bdot(x, k_) + dw_ * e, dkk, ks, drhs_w, f["eg"])
        dq = _heads(lambda x, k_, dqd, e: _bdot(x, k_) + dqd * e, dqk, ks, dq_dec, f["eg"])
        dk = _heads(lambda x, kb_, y, q_, dkd, el, dkb_, b: _bdot(x, kb_, 0, 0) + _bdot(y, q_, 0, 0) + dkd * el + dkb_ * b,
                    dkk, f["kb"], dqk, qs, dk_dec, f["egl"], dkb, f["bcol"])
        m = _heads(lambda x, a_, y, at: x * a_ + y * at, da, f["a"], dattn, f["attn"])
        ones = jnp.ones((CHUNK, LANE), BF16)
        col_m = [(_dot(mh, ones, 0, 0) + _dot(ml, ones, 0, 0))[:, 0:1] for mh, ml in map(_split, m)]
        dgc_all = jnp.zeros((CHUNK, LANE), F32)
        dbeta_all = jnp.zeros((CHUNK, LANE), F32)
        for h in hs:
            dq_ref[:, h * DN_DIM:(h + 1) * DN_DIM] = dq[h]
            dk_ref[:, h * DN_DIM:(h + 1) * DN_DIM] = dk[h]
            dv_ref[:, h * DN_DIM:(h + 1) * DN_DIM] = drhs_u[h] * f["bcol"][h]
            kdec_term = rowsum(dk_dec[h] * f["k_dec"][h])
            dc_dec = _sum_all(dsn[h] * s[h])
            dgc = (rowsum(m[h]) - col_m[h] + rowsum(dq_dec[h] * f["q_dec"][h]) - kdec_term
                   + rowsum(drhs_w[h] * f["rhs_w"][h]))
            last_extra = jnp.sum(kdec_term, axis=0, keepdims=True) + dc_dec * f["c_dec"][h]
            dgc = dgc + jnp.where(is_last, last_extra, 0.0)
            dbeta = rowsum(drhs_u[h] * vs[h]) + rowsum(dkb[h] * ks[h])
            dgc_all = jnp.where(lane == h, dgc, dgc_all)
            dbeta_all = jnp.where(lane == DN_HEADS + h, dbeta, dbeta_all)
        dg_all = _dot(f["lower"].astype(F32), dgc_all, 0, 0, precision=HIGHEST)
        dgb_ref[...] = jnp.where(lane < DN_HEADS, dg_all, dbeta_all)

    rev = lambda n: (N - 1 - n, 0)
    blk = pl.BlockSpec((CHUNK, DN_WIDTH), rev)
    gblk = pl.BlockSpec((CHUNK, LANE), rev)
    st_blk = pl.BlockSpec((1, DN_HEADS, DN_DIM, DN_DIM), lambda n: (N - 1 - n, 0, 0, 0))
    t_blk = pl.BlockSpec((1, DN_HEADS, CHUNK, CHUNK), lambda n: (N - 1 - n, 0, 0, 0))
    return pl.pallas_call(
        body, name="dn_chunk_bwd", grid=(N,),
        in_specs=[blk, blk, blk, gblk, st_blk, t_blk, blk, blk, blk],
        out_specs=[blk, blk, blk, gblk],
        out_shape=[jax.ShapeDtypeStruct((S, DN_WIDTH), F32)] * 3 + [jax.ShapeDtypeStruct((S, LANE), F32)],
        scratch_shapes=[pltpu.VMEM((DN_HEADS, DN_DIM, DN_DIM), F32)],
        compiler_params=_cparams(("arbitrary",)),
    )(q, k, v, gb, states, t_saved, u_saved, w_saved, do)


def _dn_post(o, qkvz, gain_row):
    def fn(i, n, ot, z, g):
        cols = []
        for h in range(DN_HEADS):
            seg = ot[:, h * DN_DIM:(h + 1) * DN_DIM]
            cols.append(seg * lax.rsqrt(jnp.mean(seg * seg, axis=-1, keepdims=True) + EPS) * g)
        return (jnp.concatenate(cols, axis=1) * (z * _sigmoid(z)),)
    return _rows("dn_post", fn, [(o, "row"), (qkvz, "row", (3, DN_WIDTH)), (gain_row, "full")], [(DN_WIDTH, BF16)], tr=512)


def _dn_post_bwd(don, o, qkvz, gain_row):
    def fn(i, n, dy, ot, z, g):
        sg = _sigmoid(z)
        sz = z * sg
        dos, ohs = [], []
        dg = jnp.zeros((1, DN_DIM), F32)
        for h in range(DN_HEADS):
            sl = slice(h * DN_DIM, (h + 1) * DN_DIM)
            seg = ot[:, sl]
            r = lax.rsqrt(jnp.mean(seg * seg, axis=-1, keepdims=True) + EPS)
            oh = seg * r
            dno = dy[:, sl] * sz[:, sl]
            dg = dg + _colsum(dno * oh)
            dn = dno * g
            dos.append(r * (dn - oh * jnp.mean(dn * oh, axis=-1, keepdims=True)))
            ohs.append(oh * g)
        dz = dy * jnp.concatenate(ohs, axis=1) * (sg * (1.0 + z * (1.0 - sg)))
        return jnp.concatenate(dos, axis=1), dz, dg
    ins = [(don, "row"), (o, "row"), (qkvz, "row", (3, DN_WIDTH)), (gain_row, "full")]
    return _rows("dn_post_bwd", fn, ins, [(DN_WIDTH, F32), (DN_WIDTH, F32)], tr=256, accs=[((1, DN_DIM), F32)])


def _dn_prep_bwd(dq, dk, dv, dgb, u, ab, alog_row, dt_row):
    def fn(i, n, dqt, dkt, dvt, dgbt, ut, abt, al, dt):
        ut = ut.astype(F32)
        sg = _sigmoid(ut)
        y = ut * sg
        dys = []
        for grad, base, sc in ((dqt, 0, DN_DIM ** -0.5), (dkt, DN_WIDTH, 1.0)):
            for h in range(DN_HEADS):
                seg = y[:, base + h * DN_DIM:base + (h + 1) * DN_DIM]
                gr = grad[:, h * DN_DIM:(h + 1) * DN_DIM]
                r = lax.rsqrt(jnp.sum(seg * seg, axis=-1, keepdims=True) + EPS)
                xh = seg * r
                dys.append((r * sc) * (gr - xh * jnp.sum(gr * xh, axis=-1, keepdims=True)))
        dy = jnp.concatenate(dys + [dvt], axis=1)
        du = dy * (sg * (1.0 + ut * (1.0 - sg)))
        lane = lax.broadcasted_iota(jnp.int32, abt.shape, 1)
        is_g = lane < DN_HEADS
        ea = jnp.exp(al)
        x = abt + dt
        slope = -ea * _sigmoid(x)
        gval = -ea * _softplus(x)
        dg = jnp.where(is_g, dgbt, 0.0)
        beta = _sigmoid(abt)
        dab = jnp.where(is_g, dg * slope, jnp.where(lane < 2 * DN_HEADS, dgbt * beta * (1.0 - beta), 0.0))
        return du, dab, _colsum(dg * gval), _colsum(dg * slope)
    ins = [(dq, "row"), (dk, "row"), (dv, "row"), (dgb, "row"), (u, "row"), (ab, "row"), (alog_row, "full"), (dt_row, "full")]
    return _rows("dn_prep_bwd", fn, ins, [(DN_QKV, F32), (LANE, BF16)], tr=256, accs=[((1, LANE), F32)] * 2)


def _dn_conv_bwd(du, dz, qkvz, convw):
    tr = 256

    def fn(i, n, dut, dun, dzt, x, xp, w):
        dun = jnp.where(i < n - 1, dun, 0.0)
        dus = jnp.concatenate([dut, dun], axis=0)
        xs = jnp.concatenate([jnp.where(i > 0, xp, 0.0), x], axis=0)
        dx = None
        dws = []
        for j in range(CONV_W):
            sh = CONV_W - 1 - j
            term = (pltpu.roll(dus, tr + SUBLANE - sh, 0) if sh else dus)[:tr] * w[j:j + 1, :]
            dx = term if dx is None else dx + term
            dws.append(_colsum(dut * (pltpu.roll(xs, sh, 0) if sh else xs)[SUBLANE:]))
        return (jnp.concatenate([dx.astype(BF16), dzt.astype(BF16)], axis=1), *dws)

    ins = [(du, "row"), (du, "next8"), (dz, "row"), (qkvz, "row", (0, DN_QKV)), (qkvz, "prev8", (0, DN_QKV)), (convw, "full")]
    res = _rows("dn_conv_bwd", fn, ins, [(DN_QKVZ, BF16)], tr=tr, accs=[((1, DN_QKV), F32)] * CONV_W)
    return res[0], res[1:]


def _add(acc, r):
    return (r + acc,)


def _mlp_ple_fwd(i, x1, hm, p_i, ple_gain, next_gain, w_up, w_down, w_ple, w_gate, target=None):
    u, a = _mm(f"mlp_up{i}", hm, w_up, epilogue=lambda acc: (acc, jnp.square(jnp.maximum(acc, 0.0))),
               out_dtypes=(BF16, BF16))
    x2, hp = _mm(f"mlp_down{i}", a, w_down, epilogue=_res_norm, extras=(x1, ple_gain), out_dtypes=(F32, BF16),
                 tm_pref=FUSED_ROWS)
    pp = _mm(f"ple_proj{i}", p_i, w_ple)

    def gate_epilogue(acc, x2t, ppt, g):
        gate = _sigmoid(acc)
        x3 = x2t + ppt * gate
        return x3, gate, x3 * lax.rsqrt(jnp.mean(x3 * x3, axis=-1, keepdims=True) + EPS) * g

    def loss_epilogue(acc, x2t, ppt, tt):
        gate = _sigmoid(acc)
        err = x2t + ppt * gate - tt
        return err * (1.0 / D_MODEL), gate, _colsum(err * err)

    saved = dict(x1=x1, hm=hm, u=u, a=a, x2=x2, hp=hp, pp=pp, p=p_i)
    if target is None:
        x3, saved["gate"], h_next = _mm(f"ple_gate{i}", hp, w_gate, epilogue=gate_epilogue, extras=(x2, pp, next_gain),
                                        out_dtypes=(F32, F32, BF16), tm_pref=FUSED_ROWS)
        return x3, h_next, saved
    dy, saved["gate"], sq = _mm(f"ple_gate{i}", hp, w_gate, epilogue=loss_epilogue, extras=(x2, pp, target),
                                out_dtypes=(F32, F32), n_colsums=1, tm_pref=FUSED_ROWS)
    return dy, sq, saved


def _mlp_ple_bwd(i, dx3, sv, mlp_gain, ple_gain, w_up, w_down, w_gate):
    def fn(_i, _n, d, g, pp):
        return d * g, d * pp * g * (1.0 - g)
    dpp, dzg = _rows(f"ple_gate_bwd{i}", fn, [(dx3, "row"), (sv["gate"], "row"), (sv["pp"], "row")],
                     [(D_MODEL, BF16), (D_MODEL, BF16)], tr=512)
    d_w_ple = _mm(f"ple_proj_dw{i}", sv["p"], dpp, ta=True, out_dtypes=(BF16,))
    d_w_gate = _mm(f"ple_gate_dw{i}", sv["hp"], dzg, ta=True, out_dtypes=(BF16,))
    dx2, dx2b, d_ple_gain = _mm(f"ple_gate_dx{i}", dzg, w_gate, tb=True, epilogue=_norm_bwd_2,
                                extras=(sv["x2"], ple_gain, dx3), out_dtypes=(F32, BF16), n_colsums=1, tm_pref=FUSED_ROWS)
    d_w_down = _mm(f"mlp_down_dw{i}", sv["a"], dx2b, ta=True, out_dtypes=(BF16,))
    du = _mm(f"mlp_down_dx{i}", dx2b, w_down, tb=True,
             epilogue=lambda acc, ut: (acc * (2.0 * jnp.maximum(ut.astype(F32), 0.0)),), extras=(sv["u"],), out_dtypes=(BF16,))
    d_w_up = _mm(f"mlp_up_dw{i}", sv["hm"], du, ta=True, out_dtypes=(BF16,))
    dx1, dx1b, d_mlp_gain = _mm(f"mlp_up_dx{i}", du, w_up, tb=True, epilogue=_norm_bwd_2,
                                extras=(sv["x1"], mlp_gain, dx2), out_dtypes=(F32, BF16), n_colsums=1, tm_pref=FUSED_ROWS)
    return dx1, dx1b, dict(w_ple=d_w_ple, w_ple_gate=d_w_gate, w_down=d_w_down, w_up=d_w_up,
                           ple_norm=d_ple_gain, mlp_norm=d_mlp_gain)


def _after(small, token):
    return small + token[0:1, 0:1]


def _local_step(x, p, positions, target, W, P, rest_of_weights, send_layer1, send_mlp0, send_attn):
    consts = _head_consts()
    bd = _block_diag(1.0 / A_HEAD_DIM)
    bd1 = _block_diag(1.0)
    ct, st = _rope_tables(positions, consts)
    gains = jnp.stack([jnp.tile(v, A_HEADS) for g in range(3) for v in (P["attn_q_gain"][g], P["attn_k_gain"][g])])
    pad = LANE - DN_HEADS
    alog_row = jnp.pad(P["dn_a_log"].reshape(1, DN_HEADS), ((0, 0), (0, pad)))
    dt_row = jnp.pad(P["dn_dt_bias"].reshape(1, DN_HEADS), ((0, 0), (0, pad)))
    ogain_row = P["dn_o_gain"].reshape(1, DN_DIM)
    row = lambda name, i: P[name][i:i + 1]

    h0 = _rmsnorm_fwd("mix_norm0", x, row("mix_norm", 0))
    qkv = _mm("attn_qkv", h0, W["attn_w_qkv"], out_dtypes=(BF16,))
    qkvn = _attn_prep(qkv, gains, ct, st, consts, bd)
    os_, lses = zip(*[_attn_fwd(qkvn[g], g) for g in range(3)])
    o_attn = _attn_merge(os_, lses)
    x1, hm0 = _mm("attn_out", o_attn, W["attn_w_o"], epilogue=_res_norm, extras=(x, row("mlp_norm", 0)),
                  out_dtypes=(F32, BF16), tm_pref=FUSED_ROWS)
    W = {**W, **rest_of_weights(x1)}
    x3, h1, sv0 = _mlp_ple_fwd(0, x1, hm0, p[0], row("ple_norm", 0), row("mix_norm", 1),
                               W["w_up"][0], W["w_down"][0], W["w_ple"][0], W["w_ple_gate"][0])
    qkvz = _mm("dn_in_qkvz", h1, W["dn_w_qkvz"])
    ab = _mm("dn_in_ab", h1, W["dn_w_ab"])
    u, q, k, v, gb = _dn_prep(qkvz, ab, W["dn_conv"], alog_row, dt_row)
    o_dn, states = _dn_chunk_fwd(q, k, v, gb)
    on = _dn_post(o_dn, qkvz, ogain_row)
    x4, hm1 = _mm("dn_out", on, W["dn_w_o"], epilogue=_res_norm, extras=(x3, row("mlp_norm", 1)),
                  out_dtypes=(F32, BF16), tm_pref=FUSED_ROWS)
    dy, sq, sv1 = _mlp_ple_fwd(1, x4, hm1, p[1], row("ple_norm", 1), None,
                               W["w_up"][1], W["w_down"][1], W["w_ple"][1], W["w_ple_gate"][1], target=target)

    dx4, dx4b, g1 = _mlp_ple_bwd(1, dy, sv1, row("mlp_norm", 1), row("ple_norm", 1),
                                 W["w_up"][1], W["w_down"][1], W["w_ple_gate"][1])
    don = _mm("dn_out_dx", dx4b, W["dn_w_o"], tb=True)
    d_dn_w_o = _mm("dn_out_dw", on, dx4b, ta=True, out_dtypes=(BF16,))
    do_dn, dz, d_ogain = _dn_post_bwd(don, o_dn, qkvz, ogain_row)
    dq, dk, dv, dgb = _dn_chunk_bwd(q, k, v, gb, states, do_dn)
    du, dab, d_alog, d_dt = _dn_prep_bwd(dq, dk, dv, dgb, u, ab, alog_row, dt_row)
    dqkvz, d_conv = _dn_conv_bwd(du, dz, qkvz, W["dn_conv"])
    dh1 = _mm("dn_in_ab_dx", dab, W["dn_w_ab"], tb=True)
    dx3, d_mix1 = _mm("dn_in_qkvz_dx", dqkvz, W["dn_w_qkvz"], tb=True,
                      epilogue=lambda acc, part, xt, g, dres: _norm_bwd(acc + part, xt, g, dres),
                      extras=(dh1, x3, row("mix_norm", 1), dx4), n_colsums=1, tm_pref=FUSED_ROWS)
    d_w_qkvz = _mm("dn_in_qkvz_dw", h1, dqkvz, ta=True, out_dtypes=(BF16,))
    d_w_ab = _mm("dn_in_ab_dw", h1, dab, ta=True, out_dtypes=(BF16,))
    token = send_layer1(dict(
        dn_w_qkvz=d_w_qkvz, dn_w_ab=d_w_ab, dn_conv=jnp.concatenate(d_conv, 0), dn_w_o=d_dn_w_o,
        w_up=g1["w_up"], w_down=g1["w_down"], w_ple=g1["w_ple"], w_ple_gate=g1["w_ple_gate"]))
    dx1, dx1b, g0 = _mlp_ple_bwd(0, dx3, sv0, row("mlp_norm", 0), _after(row("ple_norm", 0), token),
                                 W["w_up"][0], W["w_down"][0], W["w_ple_gate"][0])
    token = send_mlp0(dict(w_up=g0["w_up"], w_down=g0["w_down"], w_ple=g0["w_ple"], w_ple_gate=g0["w_ple_gate"]))
    do_attn = _mm("attn_out_dx", dx1b, W["attn_w_o"], tb=True, epilogue=_add, extras=(_after(jnp.zeros((1, A_WIDTH), F32), token),))
    d_attn_w_o = _mm("attn_out_dw", o_attn, dx1b, ta=True, out_dtypes=(BF16,))
    dos, cs = _attn_merge_bwd(do_attn, os_, lses, bd1)
    grads9 = []
    for g in range(3):
        grads9 += list(_attn_bwd(qkvn[g], g, dos[g], lses[g], cs[g]))
    dqkv, dgains = _attn_prep_bwd(qkv, grads9, gains, ct, st, consts, bd)
    d_attn_w_qkv = _mm("attn_qkv_dw", h0, dqkv, ta=True, out_dtypes=(BF16,))
    token = send_attn(dict(attn_w_qkv=d_attn_w_qkv, attn_w_o=d_attn_w_o))
    dx0, d_mix0 = _mm("attn_qkv_dx", dqkv, W["attn_w_qkv"], tb=True, epilogue=_norm_bwd,
                      extras=(x, _after(row("mix_norm", 0), token), dx1), n_colsums=1, tm_pref=FUSED_ROWS)

    dg = jnp.stack([t.reshape(A_HEADS, A_HEAD_DIM).sum(0) for t in dgains])
    small = dict(
        mix_norm=jnp.concatenate([d_mix0, d_mix1], 0),
        attn_q_gain=dg[0::2][None], attn_k_gain=dg[1::2][None],
        dn_a_log=d_alog[:, :DN_HEADS], dn_dt_bias=d_dt[:, :DN_HEADS], dn_o_gain=d_ogain,
        mlp_norm=jnp.concatenate([g0["mlp_norm"], g1["mlp_norm"]], 0),
        ple_norm=jnp.concatenate([g0["ple_norm"], g1["ple_norm"]], 0),
    )
    return sq, dx0, small


MESH_IDS = pl.DeviceIdType.MESH
ANY = pl.BlockSpec(memory_space=pl.ANY)


def _place():
    return lax.axis_index("x"), lax.axis_index("y"), lax.axis_index("c")


def _sem_scratch(n_streams):
    return [pltpu.SemaphoreType.DMA((n_streams, N_DEV - 1)), pltpu.SemaphoreType.DMA((n_streams, N_DEV - 1)),
            pltpu.SemaphoreType.DMA((n_streams,))]


def _all_gather(name, arrays, streams):
    n_in, n_st = len(arrays), len(streams)
    shapes = [arrays[a].shape if li is None else arrays[a].shape[1:] for a, li in streams]

    def body(*refs):
        in_refs, out_refs, token = refs[:n_in], refs[n_in:n_in + n_st], refs[n_in + n_st]
        send_sems, recv_sems, local_sems = refs[n_in + n_st + 1:]
        token[...] = jnp.zeros_like(token)
        x, y, c = _place()
        me, sibling = (x, y, c), (x, y, 1 - c)
        chips = [(1 - x, y), (x, 1 - y), (1 - x, 1 - y)]

        def copy(s, k, block, to, own=False):
            a, li = streams[s]
            dst = out_refs[s].at[4 * block[0] + 2 * block[1] + block[2]]
            src = (in_refs[a] if li is None else in_refs[a].at[li]) if own else dst
            return pltpu.make_async_remote_copy(src_ref=src, dst_ref=dst, send_sem=send_sems.at[s, k],
                                                recv_sem=recv_sems.at[s, k], device_id=to, device_id_type=MESH_IDS)

        started = []
        for s, (a, li) in enumerate(streams):
            src = in_refs[a] if li is None else in_refs[a].at[li]
            mine = pltpu.make_async_copy(src, out_refs[s].at[4 * x + 2 * y + c], local_sems.at[s])
            mine.start()
            started.append(mine)
        sends = []
        for s in range(n_st):
            first = [copy(s, 0, me, sibling, own=True)]
            first += [copy(s, 1 + j, me, (*chip, c), own=True) for j, chip in enumerate(chips)]
            for cp in first:
                cp.start()
            sends += first
        for j, chip in enumerate(chips):
            for s in range(n_st):
                copy(s, 1 + j, (*chip, c), me).wait_recv()
                fwd = copy(s, 4 + j, (*chip, c), sibling)
                fwd.start()
                sends.append(fwd)
        for s in range(n_st):
            copy(s, 0, sibling, me).wait_recv()
            for j, chip in enumerate(chips):
                copy(s, 4 + j, (*chip, 1 - c), me).wait_recv()
        for cp in sends:
            cp.wait_send()
        for cp in started:
            cp.wait()

    res = pl.pallas_call(
        body, name=name,
        out_shape=[jax.ShapeDtypeStruct((N_DEV,) + tuple(sh), arrays[a].dtype) for sh, (a, _) in zip(shapes, streams)]
        + [jax.ShapeDtypeStruct((SUBLANE, LANE), F32)],
        in_specs=[ANY] * n_in, out_specs=[ANY] * n_st + [pl.BlockSpec(memory_space=pltpu.VMEM)],
        scratch_shapes=_sem_scratch(n_st),
    )(*arrays)
    return list(res[:n_st]), res[n_st]


HBM = pl.BlockSpec(memory_space=pltpu.HBM)
SEM = pl.BlockSpec(memory_space=pltpu.SEMAPHORE)
FLOWS = pltpu.CompilerParams(has_side_effects=pltpu.SideEffectType.DATAFLOW_SIDE_EFFECTING)


def _in_hbm(a):
    return pltpu.with_memory_space_constraint(a, pltpu.HBM)


def _hbm_like(a):
    return pltpu.HBM(a.shape, a.dtype)


def _peers(x, y, c):
    return [(1 - x if k & 4 else x, 1 - y if k & 2 else y, 1 - c if k & 1 else c) for k in range(1, N_DEV)]


def _start_copies(name, n_remote, n_own, make_copies, operands):
    n = len(operands)

    def body(*refs):
        for cp in make_copies(refs[:n], refs[n], refs[n + 1], refs[n + 2]):
            cp.start()
        refs[-1][...] = jnp.zeros_like(refs[-1])

    res = pl.pallas_call(
        body, name=name,
        out_shape=(pltpu.SemaphoreType.DMA((n_remote,)), pltpu.SemaphoreType.DMA((n_remote,)), pltpu.SemaphoreType.DMA((n_own,)),
                   *[_hbm_like(t) for t in operands], jax.ShapeDtypeStruct((SUBLANE, LANE), F32)),
        in_specs=[HBM] * n, out_specs=(SEM, SEM, SEM, *[HBM] * n, pl.BlockSpec(memory_space=pltpu.VMEM)),
        input_output_aliases={i: 3 + i for i in range(n)}, compiler_params=FLOWS,
    )(*[_in_hbm(t) for t in operands])
    return res[:3], list(res[3:3 + n]), res[-1]


def _wait_copies(name, make_waits, sems, operands, after):
    n = len(operands)

    def body(*refs):
        for wait in make_waits(refs[:n], refs[n], refs[n + 1], refs[n + 2]):
            wait()

    res = pl.pallas_call(
        body, name=name, out_shape=tuple(_hbm_like(t) for t in operands),
        in_specs=[HBM] * n + [SEM, SEM, SEM, ANY], out_specs=tuple([HBM] * n),
        input_output_aliases={i: i for i in range(n)}, compiler_params=FLOWS,
    )(*operands, *sems, after)
    return list(res)


def _gather_plan(n_in, streams):
    def block(arr, s):
        a, li = streams[s]
        return arr[a] if li is None else arr[a].at[li]

    def copies(refs, send_sems, recv_sems, own_sems, arrivals=False):
        arr, land = refs[:n_in], refs[n_in:]
        x, y, c = _place()
        me = 4 * x + 2 * y + c
        out = []
        for s in range(len(streams)):
            out.append(("own", pltpu.make_async_copy(block(arr, s), land[s].at[me], own_sems.at[s])))
            for k, (px, py, pc) in enumerate(_peers(x, y, c)):
                out.append(("remote", pltpu.make_async_remote_copy(
                    src_ref=block(arr, s), dst_ref=land[s].at[4 * px + 2 * py + pc if arrivals else me],
                    send_sem=send_sems.at[s * (N_DEV - 1) + k], recv_sem=recv_sems.at[s * (N_DEV - 1) + k],
                    device_id=(px, py, pc), device_id_type=MESH_IDS)))
        return out
    return copies


def _exchange_plan(n_st):
    def copies(refs, send_sems, recv_sems, own_sems, arrivals=False):
        snd, rcv = refs[:n_st], refs[n_st:]
        x, y, c = _place()
        me = 4 * x + 2 * y + c
        out = []
        for s in range(n_st):
            out.append(("own", pltpu.make_async_copy(snd[s].at[me], rcv[s].at[me], own_sems.at[s])))
            for k, (px, py, pc) in enumerate(_peers(x, y, c)):
                peer = 4 * px + 2 * py + pc
                out.append(("remote", pltpu.make_async_remote_copy(
                    src_ref=snd[s].at[peer], dst_ref=rcv[s].at[peer if arrivals else me],
                    send_sem=send_sems.at[s * (N_DEV - 1) + k], recv_sem=recv_sems.at[s * (N_DEV - 1) + k],
                    device_id=(px, py, pc), device_id_type=MESH_IDS)))
        return out
    return copies


def _split_transfer(tag, plan, n_streams, operands):
    sems, operands, token = _start_copies(f"{tag}_start", n_streams * (N_DEV - 1), n_streams,
                                          lambda refs, a, b, o: [cp for _, cp in plan(refs, a, b, o)], operands)

    def waits(refs, a, b, o):
        out = []
        for kind, cp in plan(refs, a, b, o, arrivals=True):
            out += [cp.wait] if kind == "own" else [cp.wait_send, cp.wait_recv]
        return out

    return (lambda after: _wait_copies(f"{tag}_wait", waits, sems, operands, after)), token


def _gather_async(tag, arrays, streams):
    lands = [lax.empty((N_DEV,) + tuple(arrays[a].shape if li is None else arrays[a].shape[1:]), arrays[a].dtype)
             for a, li in streams]
    finish, token = _split_transfer(tag, _gather_plan(len(arrays), streams), len(streams), list(arrays) + lands)
    return (lambda after: finish(after)[len(arrays):]), token


def _exchange_async(tag, sends):
    recvs = [lax.empty(t.shape, t.dtype) for t in sends]
    finish, token = _split_transfer(tag, _exchange_plan(len(sends)), len(sends), list(sends) + recvs)
    return (lambda after: finish(after)[len(sends):]), token


def _dn_in_pieces():
    n = (DN_QKVZ + 2 * DN_HEADS) // N_DEV
    segs = ((0, DN_QKV, 0, 0), (DN_QKV, DN_QKV + 2 * DN_HEADS, 1, 0), (DN_QKV + 2 * DN_HEADS, DN_QKVZ + 2 * DN_HEADS, 0, DN_QKV))
    out = []
    for d in range(N_DEV):
        lo, hi = d * n, (d + 1) * n
        for s0, s1, tgt, t0 in segs:
            a, b = max(lo, s0), min(hi, s1)
            if a < b:
                out.append((d, a - lo, b - lo, tgt, t0 + a - s0))
    return out


def _unpack_cols(name, g):
    _, K, n = g.shape
    tr = 256

    def body(g_ref, o_ref):
        for d in range(N_DEV):
            o_ref[:, d * n:(d + 1) * n] = g_ref[d]

    return pl.pallas_call(
        body, name=name, grid=(K // tr,), in_specs=[pl.BlockSpec((N_DEV, tr, n), lambda i: (0, i, 0))],
        out_specs=pl.BlockSpec((tr, N_DEV * n), lambda i: (i, 0)),
        out_shape=jax.ShapeDtypeStruct((K, N_DEV * n), g.dtype), compiler_params=_cparams(("parallel",)),
    )(g)


def _pack_cols(name, w):
    K, n = w.shape[0], w.shape[1] // N_DEV
    tr = 256

    def body(w_ref, o_ref):
        for d in range(N_DEV):
            o_ref[d] = w_ref[:, d * n:(d + 1) * n]

    return pl.pallas_call(
        body, name=name, grid=(K // tr,), in_specs=[pl.BlockSpec((tr, N_DEV * n), lambda i: (i, 0))],
        out_specs=pl.BlockSpec((N_DEV, tr, n), lambda i: (0, i, 0)),
        out_shape=jax.ShapeDtypeStruct((N_DEV, K, n), w.dtype), compiler_params=_cparams(("parallel",)),
    )(w)


def _unpack_dn_in(g):
    _, K, n = g.shape
    tr = 256

    def body(g_ref, qkvz_ref, ab_ref):
        ab_ref[...] = jnp.zeros_like(ab_ref)
        for d, c0, c1, tgt, t0 in _dn_in_pieces():
            (qkvz_ref, ab_ref)[tgt][:, t0:t0 + c1 - c0] = g_ref[d, :, c0:c1]

    return pl.pallas_call(
        body, name="unpack_dn_in", grid=(K // tr,), in_specs=[pl.BlockSpec((N_DEV, tr, n), lambda i: (0, i, 0))],
        out_specs=[pl.BlockSpec((tr, DN_QKVZ), lambda i: (i, 0)), pl.BlockSpec((tr, LANE), lambda i: (i, 0))],
        out_shape=[jax.ShapeDtypeStruct((K, DN_QKVZ), g.dtype), jax.ShapeDtypeStruct((K, LANE), g.dtype)],
        compiler_params=_cparams(("parallel",)),
    )(g)


def _pack_dn_in(d_qkvz, d_ab):
    K = d_qkvz.shape[0]
    n = (DN_QKVZ + 2 * DN_HEADS) // N_DEV
    tr = 256

    def body(qkvz_ref, ab_ref, o_ref):
        for d, c0, c1, tgt, t0 in _dn_in_pieces():
            o_ref[d, :, c0:c1] = (qkvz_ref, ab_ref)[tgt][:, t0:t0 + c1 - c0]

    return pl.pallas_call(
        body, name="pack_dn_in", grid=(K // tr,),
        in_specs=[pl.BlockSpec((tr, DN_QKVZ), lambda i: (i, 0)), pl.BlockSpec((tr, LANE), lambda i: (i, 0))],
        out_specs=pl.BlockSpec((N_DEV, tr, n), lambda i: (0, i, 0)),
        out_shape=jax.ShapeDtypeStruct((N_DEV, K, n), d_qkvz.dtype), compiler_params=_cparams(("parallel",)),
    )(d_qkvz, d_ab)


ADAMW_ROWS = 256


def _adamw(name, parts, w, m, v):
    R, C = w.shape
    tr = min(R, ADAMW_ROWS)
    assert R % tr == 0 and parts.shape == (N_DEV, R, C)
    c1 = 1.0 - B1 ** STEP
    c2 = 1.0 - B2 ** STEP

    def body(p_ref, w_ref, m_ref, v_ref, g_ref, d_ref, nm_ref, nv_ref):
        g = p_ref[0].astype(F32)
        for dev in range(1, N_DEV):
            g = g + p_ref[dev].astype(F32)
        nm = B1 * m_ref[...] + (1.0 - B1) * g
        nv = B2 * v_ref[...] + (1.0 - B2) * jnp.square(g)
        g_ref[...] = g
        nm_ref[...] = nm
        nv_ref[...] = nv
        d_ref[...] = -LR * ((nm / c1) / (jnp.sqrt(nv / c2) + ADAM_EPS) + WD * w_ref[...])

    blk = pl.BlockSpec((tr, C), lambda i: (i, 0))
    return pl.pallas_call(
        body, name=name, grid=(R // tr,),
        in_specs=[pl.BlockSpec((N_DEV, tr, C), lambda i: (0, i, 0)), blk, blk, blk],
        out_specs=[blk] * 4, out_shape=[jax.ShapeDtypeStruct((R, C), F32)] * 4,
        compiler_params=_cparams(("parallel",)),
    )(parts, w, m, v)


SMALL = ("mix_norm", "attn_q_gain", "attn_k_gain", "dn_a_log", "dn_dt_bias", "dn_o_gain", "mlp_norm", "ple_norm")
WEIGHTS = ("mix_norm", "attn_w_qkv", "attn_q_gain", "attn_k_gain", "attn_w_o", "dn_w_in", "dn_conv", "dn_a_log",
           "dn_dt_bias", "dn_o_gain", "dn_w_o", "mlp_norm", "w_up", "w_down", "ple_norm", "w_ple", "w_ple_gate")


def _to_rows(flat, multiple):
    n = flat.shape[-1]
    rows = -(-n // (LANE * multiple)) * multiple
    return jnp.pad(flat, [(0, rows * LANE - n)]).reshape(rows, LANE)


def _cols_to_devices(w):
    K, N = w.shape
    return jnp.transpose(w.reshape(K, N_DEV, N // N_DEV), (1, 0, 2))


def _cols_from_devices(g):
    _, K, n = g.shape
    return jnp.transpose(g, (1, 0, 2)).reshape(K, N_DEV * n)


SMALL_ROWS = 96


def _pack_small(vals, loss_rows):
    rows = [_to_rows(vals[n].reshape(-1), SUBLANE) for n in SMALL] + [loss_rows]
    buf = jnp.concatenate(rows, 0)
    assert buf.shape == (SMALL_ROWS, LANE)
    return buf


def _unpack_small(buf, like):
    out, r = {}, 0
    for n in SMALL:
        sz = math.prod(like[n].shape)
        out[n] = buf[r:r + -(-sz // LANE)].reshape(-1)[:sz].reshape(like[n].shape)
        r += -(-sz // (LANE * SUBLANE)) * SUBLANE
    return out


def kernel(x, p, positions, mix_norm, attn_w_qkv, attn_q_gain, attn_k_gain, attn_w_o, dn_w_in, dn_conv, dn_a_log, dn_dt_bias, dn_o_gain, dn_w_o, mlp_norm, w_up, w_down, ple_norm, w_ple, w_ple_gate, loss_target, m_mix_norm, m_attn_w_qkv, m_attn_q_gain, m_attn_k_gain, m_attn_w_o, m_dn_w_in, m_dn_conv, m_dn_a_log, m_dn_dt_bias, m_dn_o_gain, m_dn_w_o, m_mlp_norm, m_w_up, m_w_down, m_ple_norm, m_w_ple, m_w_ple_gate, v_mix_norm, v_attn_w_qkv, v_attn_q_gain, v_attn_k_gain, v_attn_w_o, v_dn_w_in, v_dn_conv, v_dn_a_log, v_dn_dt_bias, v_dn_o_gain, v_dn_w_o, v_mlp_norm, v_w_up, v_w_down, v_ple_norm, v_w_ple, v_w_ple_gate):
    w = dict(mix_norm=mix_norm, attn_w_qkv=attn_w_qkv, attn_q_gain=attn_q_gain, attn_k_gain=attn_k_gain, attn_w_o=attn_w_o,
             dn_w_in=dn_w_in, dn_conv=dn_conv, dn_a_log=dn_a_log, dn_dt_bias=dn_dt_bias, dn_o_gain=dn_o_gain, dn_w_o=dn_w_o,
             mlp_norm=mlp_norm, w_up=w_up, w_down=w_down, ple_norm=ple_norm, w_ple=w_ple, w_ple_gate=w_ple_gate)
    m = dict(mix_norm=m_mix_norm, attn_w_qkv=m_attn_w_qkv, attn_q_gain=m_attn_q_gain, attn_k_gain=m_attn_k_gain,
             attn_w_o=m_attn_w_o, dn_w_in=m_dn_w_in, dn_conv=m_dn_conv, dn_a_log=m_dn_a_log, dn_dt_bias=m_dn_dt_bias,
             dn_o_gain=m_dn_o_gain, dn_w_o=m_dn_w_o, mlp_norm=m_mlp_norm, w_up=m_w_up, w_down=m_w_down,
             ple_norm=m_ple_norm, w_ple=m_w_ple, w_ple_gate=m_w_ple_gate)
    v = dict(mix_norm=v_mix_norm, attn_w_qkv=v_attn_w_qkv, attn_q_gain=v_attn_q_gain, attn_k_gain=v_attn_k_gain,
             attn_w_o=v_attn_w_o, dn_w_in=v_dn_w_in, dn_conv=v_dn_conv, dn_a_log=v_dn_a_log, dn_dt_bias=v_dn_dt_bias,
             dn_o_gain=v_dn_o_gain, dn_w_o=v_dn_w_o, mlp_norm=v_mlp_norm, w_up=v_w_up, w_down=v_w_down,
             ple_norm=v_ple_norm, w_ple=v_w_ple, w_ple_gate=v_w_ple_gate)
    S = x.shape[1]

    bf = lambda a: a.astype(BF16)
    rows_to_devices = lambda t: t.reshape(N_DEV, t.shape[0] // N_DEV, t.shape[1])

    (g_qkv, g_ao), token = _all_gather("gather_attn", [bf(attn_w_qkv[0]), bf(attn_w_o[0])], [(0, None), (1, None)])
    rest_shards = [bf(dn_w_in[0]), bf(dn_w_o[0]), bf(w_up), bf(w_down), bf(w_ple), bf(w_ple_gate), _after(dn_conv[0], token)]
    rest_streams = [(0, None), (1, None), (2, 0), (2, 1), (3, 0), (3, 1), (4, 0), (4, 1), (5, 0), (5, 1), (6, None)]
    rest_arrived, token = _gather_async("gather_rest", rest_shards, rest_streams)
    W = dict(attn_w_qkv=_unpack_cols("unpack_attn_qkv", g_qkv), attn_w_o=_cols_from_devices(g_ao))

    def rest_of_weights(after):
        g_in, g_do, g_up0, g_up1, g_dn0, g_dn1, g_pl0, g_pl1, g_gt0, g_gt1, g_conv = rest_arrived(after)
        rest = dict(
            dn_conv=jnp.transpose(g_conv, (1, 0, 2)).reshape(CONV_W, DN_QKV), dn_w_o=g_do.reshape(DN_WIDTH, D_MODEL),
            w_up=[_cols_from_devices(g_up0), _cols_from_devices(g_up1)],
            w_down=[g_dn0.reshape(D_FF, D_MODEL), g_dn1.reshape(D_FF, D_MODEL)],
            w_ple=[_cols_from_devices(g_pl0), _cols_from_devices(g_pl1)],
            w_ple_gate=[g_gt0.reshape(D_MODEL, D_MODEL), g_gt1.reshape(D_MODEL, D_MODEL)])
        rest["dn_w_qkvz"], rest["dn_w_ab"] = _unpack_dn_in(g_in)
        return rest

    pending = {}

    def mlp_sends(g):
        return [_cols_to_devices(g["w_up"]), rows_to_devices(g["w_down"]), _cols_to_devices(g["w_ple"]),
                rows_to_devices(g["w_ple_gate"])]

    def start(tag, sends):
        pending[tag], token = _exchange_async(f"exchange_{tag}", sends)
        return token

    def send_layer1(g):
        conv_send = jnp.transpose(g["dn_conv"].reshape(CONV_W, N_DEV, DN_QKV // N_DEV), (1, 0, 2))
        return start("layer1", [_pack_dn_in(g["dn_w_qkvz"], g["dn_w_ab"]), conv_send, rows_to_devices(g["dn_w_o"])] + mlp_sends(g))

    def send_mlp0(g):
        return start("mlp0", mlp_sends(g))

    def send_attn(g):
        return start("attn", [_pack_cols("pack_attn_qkv", g["attn_w_qkv"]), _cols_to_devices(g["attn_w_o"])])

    P = dict(mix_norm=_after(mix_norm, token), attn_q_gain=attn_q_gain[0], attn_k_gain=attn_k_gain[0], dn_a_log=dn_a_log[0],
             dn_dt_bias=dn_dt_bias[0], dn_o_gain=dn_o_gain[0], mlp_norm=mlp_norm, ple_norm=ple_norm)

    sq, dx0, small_g = _local_step(x[0], p[:, 0], positions.reshape(S, 1), loss_target[0], W, P,
                                   rest_of_weights, send_layer1, send_mlp0, send_attn)

    r_in, r_conv, r_do, r_up1, r_dn1, r_pl1, r_gt1 = pending["layer1"](dx0)
    r_up0, r_dn0, r_pl0, r_gt0 = pending["mlp0"](dx0)
    r_qkv, r_ao = pending["attn"](dx0)
    big = {}
    for n, parts in (("attn_w_qkv", [r_qkv]), ("attn_w_o", [r_ao]), ("dn_w_in", [r_in]), ("dn_conv", [r_conv]),
                     ("dn_w_o", [r_do]), ("w_up", [r_up0, r_up1]), ("w_down", [r_dn0, r_dn1]),
                     ("w_ple", [r_pl0, r_pl1]), ("w_ple_gate", [r_gt0, r_gt1])):
        layers = [_adamw(f"adamw_{n}{l}", pt, w[n][l], m[n][l], v[n][l]) for l, pt in enumerate(parts)]
        big[n] = [jnp.stack([res[k] for res in layers]) for k in range(4)]

    loss_rows = jnp.pad((0.5 / D_MODEL) * jnp.sum(sq, axis=1, keepdims=True), ((0, SUBLANE - 1), (0, LANE - 1)))
    small_like = {n: w[n] for n in SMALL}
    parts_s = _all_gather("gather_small", [_pack_small(small_g, loss_rows)], [(0, None)])[0][0]
    zero_rows = jnp.zeros((SUBLANE, LANE), F32)
    small = _adamw("adamw_small", parts_s, _pack_small(w, zero_rows), _pack_small(m, zero_rows), _pack_small(v, zero_rows))
    loss = small[0][SMALL_ROWS - SUBLANE, 0]
    small = [_unpack_small(b, small_like) for b in small]

    outs = [loss, dx0[None]]
    for k in range(4):
        for n in WEIGHTS:
            outs.append(small[k][n] if n in SMALL else big[n][k])
    return tuple(outs)
```

```python
import functools
import math

import jax
import jax.numpy as jnp
from jax import lax
from jax.experimental import pallas as pl
from jax.experimental.pallas import tpu as pltpu

F32 = jnp.float32
BF16 = jnp.bfloat16
HIGHEST = lax.Precision.HIGHEST

N_DEV = 8
D_MODEL = 1024
EPS = 1e-6
SWA_GROUPS = ((128, 1), (512, 4), (2048, 16))
A_HEADS = 8
A_HEAD_DIM = 64
A_WIDTH = A_HEADS * A_HEAD_DIM
A_QKV = 3 * 3 * A_WIDTH
ROPE_DIM = 16
ROPE_HALF = 8
ROPE_THETA = 500000.0
BAND = 128
DN_HEADS = 8
DN_DIM = 128
DN_WIDTH = DN_HEADS * DN_DIM
CONV_W = 4
CHUNK = 64
D_FF = 4 * D_MODEL
PLE_DIM = 256
LR, B1, B2, ADAM_EPS, WD, STEP = 0.001, 0.9, 0.999, 1e-08, 0.01, 10

VMEM_LIMIT = 56 * 1024 * 1024
MXU_TILE = 1024
MM_SLAB = 256
LANE = 128
SUBLANE = 8


def _cparams(sem):
    return pltpu.CompilerParams(dimension_semantics=sem, vmem_limit_bytes=VMEM_LIMIT)


def _tile(n, pref):
    if n <= pref:
        return n
    t = (pref // LANE) * LANE
    while t >= LANE:
        if n % t == 0:
            return t
        t -= LANE
    raise ValueError(f"no tile for {n}")


def _dot(a, b, ca=1, cb=0, precision=None):
    return lax.dot_general(a, b, (((ca,), (cb,)), ((), ())), precision=precision,
                           preferred_element_type=F32)


def _bdot(a, b, ca=1, cb=0):
    return _dot(a.astype(BF16), b.astype(BF16), ca, cb)


def _mm(name, a, b, *, ta=False, tb=False, epilogue=None, extras=(), out_dtypes=(F32,), n_colsums=0,
        tm_pref=MXU_TILE, tn_pref=1536, tk_pref=MXU_TILE):
    M, K = (a.shape[1], a.shape[0]) if ta else a.shape
    N = b.shape[0] if tb else b.shape[1]
    assert (b.shape[1] if tb else b.shape[0]) == K
    tm, tn, tk = _tile(M, tm_pref), _tile(N, tn_pref), _tile(K, tk_pref)
    nk = K // tk
    n_out = len(out_dtypes)
    n_ext = len(extras)
    assert n_colsums == 0 or tn == N
    sub = min(tm, MM_SLAB)

    def body(*refs):
        a_ref, b_ref = refs[0], refs[1]
        ext = refs[2:2 + n_ext]
        outs = refs[2 + n_ext:2 + n_ext + n_out]
        sums = refs[2 + n_ext + n_out:2 + n_ext + n_out + n_colsums]
        row_tile, k = pl.program_id(0), pl.program_id(2)
        slabs = [slice(s * sub, (s + 1) * sub) for s in range(tm // sub)]

        def product(rows):
            return _bdot(a_ref[:, rows] if ta else a_ref[rows, :], b_ref[...], 0 if ta else 1, 1 if tb else 0)

        def finish(results):
            col_rows = []
            for rows, r in zip(slabs, results):
                res = (r,) if epilogue is None else epilogue(r, *[e[...] if e.shape[0] == 1 else e[rows, :] for e in ext])
                for o, v in zip(outs, res):
                    o[rows, :] = v.astype(o.dtype)
                col_rows.append(res[n_out:])
            for n, o in enumerate(sums):
                v = functools.reduce(lambda x, y: x + y, [c[n] for c in col_rows])

                @pl.when(row_tile == 0)
                def _(o=o, v=v):
                    o[...] = v

                @pl.when(row_tile > 0)
                def _(o=o, v=v):
                    o[...] += v

        if nk == 1:
            finish([product(rows) for rows in slabs])
            return
        acc = refs[-1]

        @pl.when(k == 0)
        def _():
            acc[...] = jnp.zeros_like(acc)

        for rows in slabs:
            acc[rows, :] += product(rows)

        @pl.when(k == nk - 1)
        def _():
            finish([acc[rows, :] for rows in slabs])

    a_spec = pl.BlockSpec((tk, tm), lambda i, j, k: (k, i)) if ta else pl.BlockSpec((tm, tk), lambda i, j, k: (i, k))
    b_spec = pl.BlockSpec((tn, tk), lambda i, j, k: (j, k)) if tb else pl.BlockSpec((tk, tn), lambda i, j, k: (k, j))
    ext_specs = []
    for e in extras:
        if e.shape[0] == 1 and M != 1:
            ext_specs.append(pl.BlockSpec((1, tn), lambda i, j, k: (0, j)))
        else:
            ext_specs.append(pl.BlockSpec((tm, tn), lambda i, j, k: (i, j)))
    out = pl.pallas_call(
        body, name=name,
        grid=(M // tm, N // tn, nk),
        in_specs=[a_spec, b_spec] + ext_specs,
        out_specs=[pl.BlockSpec((tm, tn), lambda i, j, k: (i, j)) for _ in range(n_out)]
        + [pl.BlockSpec((1, tn), lambda i, j, k: (0, 0)) for _ in range(n_colsums)],
        out_shape=[jax.ShapeDtypeStruct((M, N), dt) for dt in out_dtypes]
        + [jax.ShapeDtypeStruct((1, N), F32) for _ in range(n_colsums)],
        scratch_shapes=[pltpu.VMEM((tm, tn), F32)] if nk > 1 else [],
        compiler_params=_cparams(("arbitrary" if n_colsums else "parallel", "parallel", "arbitrary")),
    )(a, b, *extras)
    return out[0] if len(out) == 1 else tuple(out)


def _perm_matrices(tr, d):
    import numpy as np
    old = np.arange(tr)
    p = np.zeros((tr, tr), np.float32)
    p[(old % d) * (tr // d) + old // d, old] = 1.0
    return jnp.asarray(p, BF16), jnp.asarray(p.T, BF16)


def _permute(p, x):
    if x.dtype == BF16:
        return _dot(p, x)
    hi = x.astype(BF16)
    rest = x - hi.astype(F32)
    mid = rest.astype(BF16)
    lo = (rest - mid.astype(F32)).astype(BF16)
    return _dot(p, hi) + _dot(p, mid) + _dot(p, lo)


def _rows(name, fn, ins, outs, *, tr, accs=()):
    ins = [(e[0], e[1]) + (e[2] if len(e) > 2 else (0, e[0].shape[-1])) for e in ins]
    outs = [tuple(o) + (0,) * (3 - len(o)) for o in outs]
    n_rows = next(e[0].shape[0] if e[1] == "row" else e[0].shape[0] * e[0].shape[1] for e in ins if e[1] in ("row", "res"))
    assert n_rows % tr == 0 and tr % SUBLANE == 0
    steps = n_rows // tr
    t8 = tr // SUBLANE
    n8 = n_rows // SUBLANE
    dils = sorted({e[0].shape[0] for e in ins if e[1] == "res" and e[0].shape[0] > 1} | {o[2] for o in outs if o[2] > 1})
    perms = [m for d in dils for m in _perm_matrices(tr, d)]
    ins = ins + [(m, "full", 0, tr) for m in perms]
    n_in, n_out, n_acc = len(ins), len(outs), len(accs)

    def body(*refs):
        i = pl.program_id(0)
        to_res = {d: refs[n_in - len(perms) + 2 * j][...] for j, d in enumerate(dils)}
        to_tok = {d: refs[n_in - len(perms) + 2 * j + 1][...] for j, d in enumerate(dils)}
        tiles = []
        for r, e in zip(refs[:n_in - len(perms)], ins):
            d = e[0].shape[0] if e[1] == "res" else 0
            if d == 0:
                tiles.append(r[...])
            elif d == 1:
                tiles.append(r[0])
            else:
                tiles.append(_permute(to_tok[d], jnp.concatenate([r[j] for j in range(d)], axis=0)))
        vals = fn(i, steps, *tiles)
        if not isinstance(vals, (tuple, list)):
            vals = (vals,)
        assert len(vals) == n_out + n_acc
        for o, v, (_, dt, d) in zip(refs[n_in:n_in + n_out], vals[:n_out], outs):
            if d == 0:
                o[...] = v.astype(o.dtype)
            elif d == 1:
                o[0] = v.astype(o.dtype)
            else:
                y = _permute(to_res[d], v.astype(dt))
                for j in range(d):
                    o[j] = y[j * (tr // d):(j + 1) * (tr // d)].astype(o.dtype)
        if n_acc:
            acc_refs = refs[n_in + n_out:]

            @pl.when(i == 0)
            def _():
                for r in acc_refs:
                    r[...] = jnp.zeros_like(r)

            for r, v in zip(acc_refs, vals[n_out:]):
                r[...] += v.astype(r.dtype)

    in_specs = []
    for a, kind, cb, c in ins:
        if kind == "row":
            in_specs.append(pl.BlockSpec((tr, c), lambda i, cb=cb: (i, cb)))
        elif kind == "full":
            in_specs.append(pl.BlockSpec(a.shape, lambda i, z=(0,) * a.ndim: z))
        elif kind == "prev8":
            in_specs.append(pl.BlockSpec((SUBLANE, c), lambda i, cb=cb: (jnp.maximum(i * t8 - 1, 0), cb)))
        elif kind == "next8":
            in_specs.append(pl.BlockSpec((SUBLANE, c), lambda i, cb=cb: (jnp.minimum((i + 1) * t8, n8 - 1), cb)))
        elif kind == "res":
            d = a.shape[0]
            in_specs.append(pl.BlockSpec((d, tr // d, a.shape[2]), lambda i: (0, i, 0)))
        else:
            raise ValueError(kind)
    out_specs = [pl.BlockSpec((tr, c), lambda i: (i, 0)) if d == 0 else pl.BlockSpec((d, tr // d, c), lambda i: (0, i, 0))
                 for c, _, d in outs]
    out_specs += [pl.BlockSpec(s, lambda i, z=(0,) * len(s): z) for s, _ in accs]
    out_shape = [jax.ShapeDtypeStruct((n_rows, c) if d == 0 else (d, n_rows // d, c), dt) for c, dt, d in outs]
    out_shape += [jax.ShapeDtypeStruct(s, dt) for s, dt in accs]
    res = pl.pallas_call(
        body, name=name, grid=(steps,), in_specs=in_specs, out_specs=out_specs, out_shape=out_shape,
        compiler_params=_cparams(("arbitrary",) if n_acc else ("parallel",)),
    )(*[e[0] for e in ins])
    return res[0] if len(res) == 1 else tuple(res)


def _colsum(x):
    return jnp.sum(x, axis=0, keepdims=True)


def _sum_all(x):
    return jnp.sum(jnp.sum(x, axis=1, keepdims=True), axis=0, keepdims=True)


def _rmsnorm_fwd(name, x, gain):
    def fn(i, n, xt, g):
        r = lax.rsqrt(jnp.mean(xt * xt, axis=-1, keepdims=True) + EPS)
        return (xt * r * g,)
    return _rows(name, fn, [(x, "row"), (gain, "full")], [(x.shape[1], BF16)], tr=512)


FUSED_ROWS = 1024


def _res_norm(acc, res, g):
    x = res + acc
    return x, x * lax.rsqrt(jnp.mean(x * x, axis=-1, keepdims=True) + EPS) * g


def _norm_bwd(dh, x, g, dres):
    r = lax.rsqrt(jnp.mean(x * x, axis=-1, keepdims=True) + EPS)
    xh = x * r
    dxn = dh * g
    dx = dres + r * (dxn - xh * jnp.mean(dxn * xh, axis=-1, keepdims=True))
    return dx, _colsum(dh * xh)


def _norm_bwd_2(dh, x, g, dres):
    dx, dg = _norm_bwd(dh, x, g, dres)
    return dx, dx, dg


def _head_consts():
    import numpy as np
    e = np.arange(A_WIDTH) % A_HEAD_DIM
    inv = (np.float32(ROPE_THETA) ** (-np.arange(0, ROPE_DIM, 2, dtype=np.float32) / np.float32(ROPE_DIM))).astype(np.float32)
    c = np.zeros((8, A_WIDTH), np.float32)
    c[0] = np.where(e < ROPE_DIM, inv[e % ROPE_HALF], 0.0)
    c[1] = np.where(e < ROPE_HALF, -1.0, np.where(e < ROPE_DIM, 1.0, 0.0))
    c[2] = (e < ROPE_HALF).astype(np.float32)
    c[3] = (e < ROPE_DIM).astype(np.float32)
    return jnp.asarray(c)


def _block_diag(scale):
    import numpy as np
    h = np.arange(A_WIDTH) // A_HEAD_DIM
    return jnp.asarray((h[:, None] == h[None, :]).astype(np.float32) * scale, dtype=BF16)


def _seg_sum(x, bd):
    return _dot(x.astype(BF16), bd)


def _rope_tables(positions, consts):
    def fn(i, n, pos, c):
        ang = pos.astype(F32) * c[0:1, :LANE]
        return jnp.cos(ang), jnp.sin(ang) * c[1:2, :LANE]
    return _rows("rope_tables", fn, [(positions, "row"), (consts, "full")], [(LANE, F32), (LANE, F32)], tr=512)


def _rope_wide(t):
    return jnp.concatenate([t] * (A_WIDTH // LANE), axis=1)


def _rope_apply(y, ct, st, low):
    rolled = jnp.where(low, pltpu.roll(y, A_WIDTH - ROPE_HALF, 1), pltpu.roll(y, ROPE_HALF, 1))
    return y * ct + rolled * st


def _rope_apply_bwd(dout, ct, st, low, in16):
    t = dout * st
    back = jnp.where(low, pltpu.roll(t, A_WIDTH - ROPE_HALF, 1), jnp.where(in16, pltpu.roll(t, ROPE_HALF, 1), 0.0))
    return dout * ct + back


def _attn_prep(qkv, gains, ct, st, consts, bd):
    def fn(i, n, t, g, c_t, s_t, c, b):
        low = c[2:3, :] > 0.5
        c_t, s_t = _rope_wide(c_t), _rope_wide(s_t)
        groups = []
        for grp in range(3):
            cols = []
            for which in range(3):
                off = (grp * 3 + which) * A_WIDTH
                x = t[:, off:off + A_WIDTH].astype(F32)
                if which == 2:
                    cols.append(x.astype(BF16))
                    continue
                r = lax.rsqrt(_seg_sum(x * x, b) + EPS)
                y = x * r * g[grp * 2 + which:grp * 2 + which + 1, :]
                cols.append(_rope_apply(y, c_t, s_t, low).astype(BF16))
            groups.append(jnp.concatenate(cols, axis=1))
        return tuple(groups)
    return _rows("attn_prep", fn, [(qkv, "row"), (gains, "full"), (ct, "row"), (st, "row"), (consts, "full"), (bd, "full")],
                 [(3 * A_WIDTH, BF16, d) for _, d in SWA_GROUPS], tr=256)


def _band_mask(n):
    row = lax.broadcasted_iota(jnp.int32, (BAND, 2 * BAND), 0)
    col = lax.broadcasted_iota(jnp.int32, (BAND, 2 * BAND), 1)
    dist = row + BAND - col
    return (dist >= 0) & (dist <= BAND) & ((col >= BAND) | (n > 0))


def _attn_fwd(qkvn, grp):
    d, L, _ = qkvn.shape
    nblk = L // BAND
    assert L % BAND == 0 and d == SWA_GROUPS[grp][1]

    def body(q_ref, kc_ref, kp_ref, vc_ref, vp_ref, o_ref, lse_ref):
        n = pl.program_id(1)
        valid = _band_mask(n)
        first = lax.broadcasted_iota(jnp.int32, (BAND, LANE), 1) < A_HEAD_DIM
        pairs = [slice(pr * LANE, (pr + 1) * LANE) for pr in range(A_WIDTH // LANE)]
        halves = (first, jnp.logical_not(first))
        qps = [q_ref[:, sl] for sl in pairs]
        kcats = [jnp.concatenate([kp_ref[:, sl], kc_ref[:, sl]], axis=0) for sl in pairs]
        vcats = [jnp.concatenate([vp_ref[:, sl], vc_ref[:, sl]], axis=0) for sl in pairs]
        heads = [(pr, m) for pr in range(len(pairs)) for m in halves]
        ss = [_dot(jnp.where(m, qps[pr], jnp.zeros_like(qps[pr])), kcats[pr], 1, 1) for pr, m in heads]
        ps, lses = [], []
        for s in ss:
            s = jnp.where(valid, s * (A_HEAD_DIM ** -0.5), -1e30)
            mx = jnp.max(s, axis=-1, keepdims=True)
            e = jnp.exp(s - mx)
            l = jnp.sum(e, axis=-1, keepdims=True)
            ps.append((e / l).astype(BF16))
            lses.append(mx + jnp.log(l))
        os_ = [_dot(p, vcats[pr]) for p, (pr, _) in zip(ps, heads)]
        o_ref[...] = jnp.concatenate([jnp.where(first, os_[2 * pr], os_[2 * pr + 1]) for pr in range(len(pairs))], axis=1)
        lse_ref[...] = jnp.concatenate([jnp.where(first, lses[2 * pr], lses[2 * pr + 1]) for pr in range(len(pairs))], axis=1)

    blk = (None, BAND, A_WIDTH)
    return pl.pallas_call(
        body, name=f"attn_fwd_g{grp}", grid=(d, nblk),
        in_specs=[pl.BlockSpec(blk, lambda r, n: (r, n, 0)),
                  pl.BlockSpec(blk, lambda r, n: (r, n, 1)),
                  pl.BlockSpec(blk, lambda r, n: (r, jnp.maximum(n - 1, 0), 1)),
                  pl.BlockSpec(blk, lambda r, n: (r, n, 2)),
                  pl.BlockSpec(blk, lambda r, n: (r, jnp.maximum(n - 1, 0), 2))],
        out_specs=[pl.BlockSpec(blk, lambda r, n: (r, n, 0)), pl.BlockSpec(blk, lambda r, n: (r, n, 0))],
        out_shape=[jax.ShapeDtypeStruct((d, L, A_WIDTH), F32)] * 2,
        compiler_params=_cparams(("parallel", "parallel")),
    )(qkvn, qkvn, qkvn, qkvn, qkvn)


def _merge_weights(l0, l1, l2):
    mx = jnp.maximum(jnp.maximum(l0, l1), l2)
    e0, e1, e2 = jnp.exp(l0 - mx), jnp.exp(l1 - mx), jnp.exp(l2 - mx)
    inv = 1.0 / (e0 + e1 + e2)
    return e0 * inv, e1 * inv, e2 * inv


def _attn_merge(os_, lses):
    def fn(i, n, o0, o1, o2, l0, l1, l2):
        w0, w1, w2 = _merge_weights(l0, l1, l2)
        return (w0 * o0 + w1 * o1 + w2 * o2,)
    ins = [(a, "res") for a in (*os_, *lses)]
    return _rows("attn_merge", fn, ins, [(A_WIDTH, BF16)], tr=256)


def _attn_merge_bwd(do, os_, lses, bd1):
    def fn(i, n, dot_, o0, o1, o2, l0, l1, l2, b):
        w0, w1, w2 = _merge_weights(l0, l1, l2)
        o = w0 * o0 + w1 * o1 + w2 * o2
        dsum = _seg_sum(dot_ * o, b)
        return (w0 * dot_, w1 * dot_, w2 * dot_, -w0 * dsum, -w1 * dsum, -w2 * dsum)
    ins = [(do, "row")] + [(a, "res") for a in (*os_, *lses)] + [(bd1, "full")]
    res = _rows("attn_merge_bwd", fn, ins, [(A_WIDTH, dt, d) for dt in (BF16, F32) for _, d in SWA_GROUPS], tr=256)
    return res[:3], res[3:]


def _lane_pick(x, lane_idx, lane):
    return jnp.sum(jnp.where(lane_idx == lane, x, 0.0), axis=-1, keepdims=True)


def _attn_bwd(qkvn, grp, do_g, lse, c_g):
    d, L, _ = qkvn.shape
    nblk = L // BAND

    def body(q_ref, kc_ref, kp_ref, vc_ref, vp_ref, do_ref, lse_ref, c_ref, dq_ref, dk_ref, dv_ref, ck, cv_):
        n = pl.program_id(1)

        @pl.when(n == 0)
        def _():
            ck[...] = jnp.zeros_like(ck)
            cv_[...] = jnp.zeros_like(cv_)

        @pl.when(n < nblk)
        def _():
            valid = _band_mask(n)
            lane = lax.broadcasted_iota(jnp.int32, (BAND, LANE), 1)
            first = lane < A_HEAD_DIM
            lane2 = lax.broadcasted_iota(jnp.int32, (2 * BAND, LANE), 1) < A_HEAD_DIM
            pairs = [slice(pr * LANE, (pr + 1) * LANE) for pr in range(A_WIDTH // LANE)]
            halves = (first, jnp.logical_not(first))
            qps = [q_ref[:, sl] for sl in pairs]
            dops = [do_ref[:, sl] for sl in pairs]
            kcats = [jnp.concatenate([kp_ref[:, sl], kc_ref[:, sl]], axis=0) for sl in pairs]
            vcats = [jnp.concatenate([vp_ref[:, sl], vc_ref[:, sl]], axis=0) for sl in pairs]
            heads = [(pr, hh) for pr in range(len(pairs)) for hh in range(2)]
            zero = jnp.zeros_like(qps[0])
            ss = [_dot(jnp.where(halves[hh], qps[pr], zero), kcats[pr], 1, 1) for pr, hh in heads]
            dps = [_dot(jnp.where(halves[hh], dops[pr], zero), vcats[pr], 1, 1) for pr, hh in heads]
            dss, pbs = [], []
            for (pr, hh), s, dp in zip(heads, ss, dps):
                lse_h = _lane_pick(lse_ref[:, pairs[pr]], lane, hh * A_HEAD_DIM)
                c_h = _lane_pick(c_ref[:, pairs[pr]], lane, hh * A_HEAD_DIM)
                p = jnp.where(valid, jnp.exp(s * (A_HEAD_DIM ** -0.5) - lse_h), 0.0)
                dss.append((p * (dp + c_h) * (A_HEAD_DIM ** -0.5)).astype(BF16))
                pbs.append(p.astype(BF16))
            dqs = [_dot(ds, kcats[pr]) for ds, (pr, _) in zip(dss, heads)]
            dks = [_dot(ds, qps[pr], 0, 0) for ds, (pr, _) in zip(dss, heads)]
            dvs = [_dot(pb, dops[pr], 0, 0) for pb, (pr, _) in zip(pbs, heads)]
            for pr, sl in enumerate(pairs):
                dq_ref[:, sl] = jnp.where(first, dqs[2 * pr], dqs[2 * pr + 1])
                dkc = jnp.where(lane2, dks[2 * pr], dks[2 * pr + 1])
                dvc = jnp.where(lane2, dvs[2 * pr], dvs[2 * pr + 1])
                dk_ref[:, sl] = ck[:, sl] + dkc[:BAND]
                dv_ref[:, sl] = cv_[:, sl] + dvc[:BAND]
                ck[:, sl] = dkc[BAND:]
                cv_[:, sl] = dvc[BAND:]

        @pl.when(n == nblk)
        def _():
            dk_ref[...] = ck[...]
            dv_ref[...] = cv_[...]

    blk = (None, BAND, A_WIDTH)
    last = nblk - 1
    qn = lambda n: jnp.minimum(n, last)
    pn = lambda n: jnp.clip(n - 1, 0, last)
    return tuple(pl.pallas_call(
        body, name=f"attn_bwd_g{grp}", grid=(d, nblk + 1),
        in_specs=[pl.BlockSpec(blk, lambda r, n: (r, qn(n), 0)),
                  pl.BlockSpec(blk, lambda r, n: (r, qn(n), 1)),
                  pl.BlockSpec(blk, lambda r, n: (r, pn(n), 1)),
                  pl.BlockSpec(blk, lambda r, n: (r, qn(n), 2)),
                  pl.BlockSpec(blk, lambda r, n: (r, pn(n), 2)),
                  pl.BlockSpec(blk, lambda r, n: (r, qn(n), 0)),
                  pl.BlockSpec(blk, lambda r, n: (r, qn(n), 0)),
                  pl.BlockSpec(blk, lambda r, n: (r, qn(n), 0))],
        out_specs=[pl.BlockSpec(blk, lambda r, n: (r, qn(n), 0)),
                   pl.BlockSpec(blk, lambda r, n: (r, pn(n), 0)),
                   pl.BlockSpec(blk, lambda r, n: (r, pn(n), 0))],
        out_shape=[jax.ShapeDtypeStruct((d, L, A_WIDTH), F32)] * 3,
        scratch_shapes=[pltpu.VMEM((BAND, A_WIDTH), F32), pltpu.VMEM((BAND, A_WIDTH), F32)],
        compiler_params=_cparams(("parallel", "arbitrary")),
    )(qkvn, qkvn, qkvn, qkvn, qkvn, do_g, lse, c_g))


def _attn_prep_bwd(qkv, grads, gains, ct, st, consts, bd):
    def fn(i, n, t, g, c_t, s_t, c, b, *gr):
        low = c[2:3, :] > 0.5
        in16 = c[3:4, :] > 0.5
        c_t, s_t = _rope_wide(c_t), _rope_wide(s_t)
        cols, dgs = [], []
        for grp in range(3):
            for which in range(3):
                dout = gr[grp * 3 + which]
                if which == 2:
                    cols.append(dout.astype(BF16))
                    continue
                off = (grp * 3 + which) * A_WIDTH
                x = t[:, off:off + A_WIDTH].astype(F32)
                gain = g[grp * 2 + which:grp * 2 + which + 1, :]
                r = lax.rsqrt(_seg_sum(x * x, b) + EPS)
                xh = x * r
                dy = _rope_apply_bwd(dout, c_t, s_t, low, in16)
                dyn = dy * gain
                dx = r * (dyn - xh * _seg_sum(dyn * xh, b))
                cols.append(dx.astype(BF16))
                dgs.append(_colsum(dy * xh))
        return (jnp.concatenate(cols, axis=1), *dgs)
    ins = [(qkv, "row"), (gains, "full"), (ct, "row"), (st, "row"), (consts, "full"), (bd, "full")] + [(a, "res") for a in grads]
    res = _rows("attn_prep_bwd", fn, ins, [(A_QKV, BF16)], tr=128, accs=[((1, A_WIDTH), F32)] * 6)
    return res[0], res[1:]


DN_QKV = 3 * DN_WIDTH
DN_QKVZ = DN_QKV + DN_WIDTH


def _sigmoid(x):
    return 1.0 / (1.0 + jnp.exp(-x))


def _softplus(x):
    return jnp.maximum(x, 0.0) + jnp.log(1.0 + jnp.exp(-jnp.abs(x)))


def _conv_taps(xs, w, tr):
    acc = None
    for j in range(CONV_W):
        sh = CONV_W - 1 - j
        term = (pltpu.roll(xs, sh, 0) if sh else xs)[SUBLANE:] * w[j:j + 1, :]
        acc = term if acc is None else acc + term
    return acc


def _dn_prep(qkvz, ab, convw, alog_row, dt_row):
    tr = 256

    def fn(i, n, x, xp, abt, w, al, dt):
        xp = jnp.where(i > 0, xp, 0.0)
        u = _conv_taps(jnp.concatenate([xp, x], axis=0), w, tr)
        y = u * _sigmoid(u)
        qs, ks = [], []
        for h in range(DN_HEADS):
            for dst, base, sc in ((qs, 0, DN_DIM ** -0.5), (ks, DN_WIDTH, 1.0)):
                seg = y[:, base + h * DN_DIM:base + (h + 1) * DN_DIM]
                dst.append(seg * (lax.rsqrt(jnp.sum(seg * seg, axis=-1, keepdims=True) + EPS) * sc))
        lane = lax.broadcasted_iota(jnp.int32, abt.shape, 1)
        g = -jnp.exp(al) * _softplus(abt + dt)
        gb = jnp.where(lane < DN_HEADS, g, jnp.where(lane < 2 * DN_HEADS, _sigmoid(abt), 0.0))
        return u, jnp.concatenate(qs, axis=1), jnp.concatenate(ks, axis=1), y[:, 2 * DN_WIDTH:], gb

    ins = [(qkvz, "row", (0, DN_QKV)), (qkvz, "prev8", (0, DN_QKV)), (ab, "row"), (convw, "full"),
           (alog_row, "full"), (dt_row, "full")]
    return _rows("dn_prep", fn, ins, [(DN_QKV, BF16), (DN_WIDTH, F32), (DN_WIDTH, F32), (DN_WIDTH, F32), (LANE, F32)], tr=tr)


def _tri_masks():
    row = lax.broadcasted_iota(jnp.int32, (CHUNK, CHUNK), 0)
    col = lax.broadcasted_iota(jnp.int32, (CHUNK, CHUNK), 1)
    return row >= col, row > col, row == col


def _heads(fn, *lists):
    return [fn(*xs) for xs in zip(*lists)]


def _split(x):
    hi = x.astype(BF16)
    return hi, (x - hi.astype(F32)).astype(BF16)


def _dot3(a, b, ca=1, cb=0):
    (ah, al), (bh, bl) = a, b
    return _dot(ah, bh, ca, cb) + (_dot(ah, bl, ca, cb) + _dot(al, bh, ca, cb))


SPLIT_STEPS = 3


def _unit_lower_inverse(a_list, eye):
    ts = [eye - a for a in a_list]
    parts = [_split(a) for a in a_list]
    for step in range(5):
        if step < SPLIT_STEPS:
            parts = [_split(_dot3(p, p)) for p in parts]
            ts = [t + _dot3(_split(t), p) for t, p in zip(ts, parts)]
        else:
            parts = [(_dot(p[0], p[0]).astype(BF16), None) for p in parts]
            ts = [t + _dot(t.astype(BF16), p[0]) for t, p in zip(ts, parts)]
    return ts


def _dn_terms(qs, ks, vs, gb, solved=None):
    lower, strict, diag = _tri_masks()
    lane = lax.broadcasted_iota(jnp.int32, (CHUNK, LANE), 1)
    is_last = lax.broadcasted_iota(jnp.int32, (CHUNK, 1), 0) == CHUNK - 1
    hs = range(DN_HEADS)
    gc = _dot(lower.astype(F32), gb, precision=HIGHEST)
    gct = jnp.transpose(gc)
    bcol = [_lane_pick(gb, lane, DN_HEADS + h) for h in hs]
    gcol = [_lane_pick(gc, lane, h) for h in hs]
    glast = [jnp.sum(jnp.where(is_last, g, 0.0), axis=0, keepdims=True) for g in gcol]
    decay = [jnp.exp(jnp.where(lower, gcol[h] - gct[h:h + 1, :], -1e30)) for h in hs]
    kb = _heads(lambda k, b: k * b, ks, bcol)
    kk = _heads(lambda x, k: _bdot(x, k, 1, 1), kb, ks)
    qk = _heads(lambda q, k: _bdot(q, k, 1, 1), qs, ks)
    a = _heads(lambda x, d: jnp.where(strict, x * d, 0.0), kk, decay)
    eg = [jnp.exp(g) for g in gcol]
    egl = _heads(lambda gl, g: jnp.exp(gl - g), glast, gcol)
    rhs_w = _heads(lambda x, e: x * e, kb, eg)
    if solved is None:
        t_full = _unit_lower_inverse(a, diag.astype(F32))
        t = [_split(x) for x in t_full]
        u = _heads(lambda tt, v, b: _dot3(tt, _split(v * b)), t, vs, bcol)
        w = _heads(lambda tt, r: _dot3(tt, _split(r)), t, rhs_w)
    else:
        t_full, u, w = solved
        t = [_split(x) for x in t_full]
    return dict(bcol=bcol, decay=decay, kb=kb, a=a, t=t, t_full=t_full, eg=eg, egl=egl, rhs_w=rhs_w, u=u, w=w,
                attn=_heads(lambda x, d: x * d, qk, decay), q_dec=_heads(lambda q, e: q * e, qs, eg),
                k_dec=_heads(lambda k, e: k * e, ks, egl), c_dec=[jnp.exp(g) for g in glast],
                lower=lower, strict=strict, lane=lane, is_last=is_last)


def _head_slices(ref):
    return [ref[:, h * DN_DIM:(h + 1) * DN_DIM] for h in range(DN_HEADS)]


def _dn_chunk_fwd(q, k, v, gb):
    S = q.shape[0]
    N = S // CHUNK

    def body(q_ref, k_ref, v_ref, gb_ref, o_ref, st_ref, t_ref, u_ref, w_ref, state):
        @pl.when(pl.program_id(0) == 0)
        def _():
            state[...] = jnp.zeros_like(state)

        f = _dn_terms(_head_slices(q_ref), _head_slices(k_ref), _head_slices(v_ref), gb_ref[...])
        s = [state[h] for h in range(DN_HEADS)]
        for h in range(DN_HEADS):
            st_ref[0, h] = s[h]
            t_ref[0, h] = f["t_full"][h]
            u_ref[:, h * DN_DIM:(h + 1) * DN_DIM] = f["u"][h]
            w_ref[:, h * DN_DIM:(h + 1) * DN_DIM] = f["w"][h]
        sb = [x.astype(BF16) for x in s]
        v_new = _heads(lambda u, w, x: u - _bdot(w, x), f["u"], f["w"], sb)
        o = _heads(lambda qd, x, at, vn: _bdot(qd, x) + _bdot(at, vn), f["q_dec"], sb, f["attn"], v_new)
        new_s = _heads(lambda x, c, kd, vn: x * c + _bdot(kd, vn, 0, 0), s, f["c_dec"], f["k_dec"], v_new)
        for h in range(DN_HEADS):
            o_ref[:, h * DN_DIM:(h + 1) * DN_DIM] = o[h]
            state[h] = new_s[h]

    blk = pl.BlockSpec((CHUNK, DN_WIDTH), lambda n: (n, 0))
    st_blk = pl.BlockSpec((1, DN_HEADS, DN_DIM, DN_DIM), lambda n: (n, 0, 0, 0))
    t_blk = pl.BlockSpec((1, DN_HEADS, CHUNK, CHUNK), lambda n: (n, 0, 0, 0))
    wide = jax.ShapeDtypeStruct((S, DN_WIDTH), F32)
    o, states, t, u, w = pl.pallas_call(
        body, name="dn_chunk_fwd", grid=(N,),
        in_specs=[blk, blk, blk, pl.BlockSpec((CHUNK, LANE), lambda n: (n, 0))],
        out_specs=[blk, st_blk, t_blk, blk, blk],
        out_shape=[wide, jax.ShapeDtypeStruct((N, DN_HEADS, DN_DIM, DN_DIM), F32),
                   jax.ShapeDtypeStruct((N, DN_HEADS, CHUNK, CHUNK), F32), wide, wide],
        scratch_shapes=[pltpu.VMEM((DN_HEADS, DN_DIM, DN_DIM), F32)],
        compiler_params=_cparams(("arbitrary",)),
    )(q, k, v, gb)
    return o, (states, t, u, w)


def _dn_chunk_bwd(q, k, v, gb, saved, do):
    S = q.shape[0]
    N = S // CHUNK
    states, t_saved, u_saved, w_saved = saved

    def body(q_ref, k_ref, v_ref, gb_ref, st_ref, t_ref, u_ref, w_ref, do_ref, dq_ref, dk_ref, dv_ref, dgb_ref, dstate):
        @pl.when(pl.program_id(0) == 0)
        def _():
            dstate[...] = jnp.zeros_like(dstate)

        hs = range(DN_HEADS)
        qs, ks, vs, dos = (_head_slices(r) for r in (q_ref, k_ref, v_ref, do_ref))
        f = _dn_terms(qs, ks, vs, gb_ref[...], ([t_ref[0, h] for h in hs], _head_slices(u_ref), _head_slices(w_ref)))
        lane, is_last = f["lane"], f["is_last"]
        rowsum = lambda x: jnp.sum(x, axis=-1, keepdims=True)
        s = [st_ref[0, h] for h in hs]
        dsn = [dstate[h] for h in hs]
        sb = [x.astype(BF16) for x in s]
        dsb = [x.astype(BF16) for x in dsn]
        dob = [x.astype(BF16) for x in dos]
        v_new = _heads(lambda u, w, x: u - _bdot(w, x), f["u"], f["w"], sb)
        dv_new = _heads(lambda at, d, kd, x: _bdot(at, d, 0, 0) + _bdot(kd, x), f["attn"], dob, f["k_dec"], dsb)
        dattn = _heads(lambda d, vn: _bdot(d, vn, 1, 1), dob, v_new)
        dq_dec = _heads(lambda d, x: _bdot(d, x, 1, 1), dob, sb)
        dk_dec = _heads(lambda vn, x: _bdot(vn, x, 1, 1), v_new, dsb)
        dw = _heads(lambda dv_, x: -_bdot(dv_, x, 1, 1), dv_new, sb)
        new_ds = _heads(lambda x, c, qd, d, w, dv_: x * c + _bdot(qd, d, 0, 0) - _bdot(w, dv_, 0, 0),
                        dsn, f["c_dec"], f["q_dec"], dob, f["w"], dv_new)
        for h in hs:
            dstate[h] = new_ds[h]
        drhs_u = _heads(lambda tt, x: _dot3(tt, _split(x), 0, 0), f["t"], dv_new)
        drhs_w = _heads(lambda tt, x: _dot3(tt, _split(x), 0, 0), f["t"], dw)
        da = _heads(lambda du_, u, dw_, w: jnp.where(f["strict"], -(_bdot(du_, u, 1, 1) + _bdot(dw_, w, 1, 1)), 0.0),
                    drhs_u, f["u"], drhs_w, f["w"])
        dkk = _heads(lambda x, d: x * d, da, f["decay"])
        dqk = _heads(lambda x, d: x * d, dattn, f["decay"])
        dkb = _heads(lambda x, k_, dw_, e: _bdot(x, k_) + dw_ * e, dkk, ks, drhs_w, f["eg"])
        dq = _heads(lambda x, k_, dqd, e: _bdot(x, k_) + dqd * e, dqk, ks, dq_dec, f["eg"])
        dk = _heads(lambda x, kb_, y, q_, dkd, el, dkb_, b: _bdot(x, kb_, 0, 0) + _bdot(y, q_, 0, 0) + dkd * el + dkb_ * b,
                    dkk, f["kb"], dqk, qs, dk_dec, f["egl"], dkb, f["bcol"])
        m = _heads(lambda x, a_, y, at: x * a_ + y * at, da, f["a"], dattn, f["attn"])
        ones = jnp.ones((CHUNK, LANE), BF16)
        col_m = [(_dot(mh, ones, 0, 0) + _dot(ml, ones, 0, 0))[:, 0:1] for mh, ml in map(_split, m)]
        dgc_all = jnp.zeros((CHUNK, LANE), F32)
        dbeta_all = jnp.zeros((CHUNK, LANE), F32)
        for h in hs:
            dq_ref[:, h * DN_DIM:(h + 1) * DN_DIM] = dq[h]
            dk_ref[:, h * DN_DIM:(h + 1) * DN_DIM] = dk[h]
            dv_ref[:, h * DN_DIM:(h + 1) * DN_DIM] = drhs_u[h] * f["bcol"][h]
            kdec_term = rowsum(dk_dec[h] * f["k_dec"][h])
            dc_dec = _sum_all(dsn[h] * s[h])
            dgc = (rowsum(m[h]) - col_m[h] + rowsum(dq_dec[h] * f["q_dec"][h]) - kdec_term
                   + rowsum(drhs_w[h] * f["rhs_w"][h]))
            last_extra = jnp.sum(kdec_term, axis=0, keepdims=True) + dc_dec * f["c_dec"][h]
            dgc = dgc + jnp.where(is_last, last_extra, 0.0)
            dbeta = rowsum(drhs_u[h] * vs[h]) + rowsum(dkb[h] * ks[h])
            dgc_all = jnp.where(lane == h, dgc, dgc_all)
            dbeta_all = jnp.where(lane == DN_HEADS + h, dbeta, dbeta_all)
        dg_all = _dot(f["lower"].astype(F32), dgc_all, 0, 0, precision=HIGHEST)
        dgb_ref[...] = jnp.where(lane < DN_HEADS, dg_all, dbeta_all)

    rev = lambda n: (N - 1 - n, 0)
    blk = pl.BlockSpec((CHUNK, DN_WIDTH), rev)
    gblk = pl.BlockSpec((CHUNK, LANE), rev)
    st_blk = pl.BlockSpec((1, DN_HEADS, DN_DIM, DN_DIM), lambda n: (N - 1 - n, 0, 0, 0))
    t_blk = pl.BlockSpec((1, DN_HEADS, CHUNK, CHUNK), lambda n: (N - 1 - n, 0, 0, 0))
    return pl.pallas_call(
        body, name="dn_chunk_bwd", grid=(N,),
        in_specs=[blk, blk, blk, gblk, st_blk, t_blk, blk, blk, blk],
        out_specs=[blk, blk, blk, gblk],
        out_shape=[jax.ShapeDtypeStruct((S, DN_WIDTH), F32)] * 3 + [jax.ShapeDtypeStruct((S, LANE), F32)],
        scratch_shapes=[pltpu.VMEM((DN_HEADS, DN_DIM, DN_DIM), F32)],
        compiler_params=_cparams(("arbitrary",)),
    )(q, k, v, gb, states, t_saved, u_saved, w_saved, do)


def _dn_post(o, qkvz, gain_row):
    def fn(i, n, ot, z, g):
        cols = []
        for h in range(DN_HEADS):
            seg = ot[:, h * DN_DIM:(h + 1) * DN_DIM]
            cols.append(seg * lax.rsqrt(jnp.mean(seg * seg, axis=-1, keepdims=True) + EPS) * g)
        return (jnp.concatenate(cols, axis=1) * (z * _sigmoid(z)),)
    return _rows("dn_post", fn, [(o, "row"), (qkvz, "row", (3, DN_WIDTH)), (gain_row, "full")], [(DN_WIDTH, BF16)], tr=512)


def _dn_post_bwd(don, o, qkvz, gain_row):
    def fn(i, n, dy, ot, z, g):
        sg = _sigmoid(z)
        sz = z * sg
        dos, ohs = [], []
        dg = jnp.zeros((1, DN_DIM), F32)
        for h in range(DN_HEADS):
            sl = slice(h * DN_DIM, (h + 1) * DN_DIM)
            seg = ot[:, sl]
            r = lax.rsqrt(jnp.mean(seg * seg, axis=-1, keepdims=True) + EPS)
            oh = seg * r
            dno = dy[:, sl] * sz[:, sl]
            dg = dg + _colsum(dno * oh)
            dn = dno * g
            dos.append(r * (dn - oh * jnp.mean(dn * oh, axis=-1, keepdims=True)))
            ohs.append(oh * g)
        dz = dy * jnp.concatenate(ohs, axis=1) * (sg * (1.0 + z * (1.0 - sg)))
        return jnp.concatenate(dos, axis=1), dz, dg
    ins = [(don, "row"), (o, "row"), (qkvz, "row", (3, DN_WIDTH)), (gain_row, "full")]
    return _rows("dn_post_bwd", fn, ins, [(DN_WIDTH, F32), (DN_WIDTH, F32)], tr=256, accs=[((1, DN_DIM), F32)])


def _dn_prep_bwd(dq, dk, dv, dgb, u, ab, alog_row, dt_row):
    def fn(i, n, dqt, dkt, dvt, dgbt, ut, abt, al, dt):
        ut = ut.astype(F32)
        sg = _sigmoid(ut)
        y = ut * sg
        dys = []
        for grad, base, sc in ((dqt, 0, DN_DIM ** -0.5), (dkt, DN_WIDTH, 1.0)):
            for h in range(DN_HEADS):
                seg = y[:, base + h * DN_DIM:base + (h + 1) * DN_DIM]
                gr = grad[:, h * DN_DIM:(h + 1) * DN_DIM]
                r = lax.rsqrt(jnp.sum(seg * seg, axis=-1, keepdims=True) + EPS)
                xh = seg * r
                dys.append((r * sc) * (gr - xh * jnp.sum(gr * xh, axis=-1, keepdims=True)))
        dy = jnp.concatenate(dys + [dvt], axis=1)
        du = dy * (sg * (1.0 + ut * (1.0 - sg)))
        lane = lax.broadcasted_iota(jnp.int32, abt.shape, 1)
        is_g = lane < DN_HEADS
        ea = jnp.exp(al)
        x = abt + dt
        slope = -ea * _sigmoid(x)
        gval = -ea * _softplus(x)
        dg = jnp.where(is_g, dgbt, 0.0)
        beta = _sigmoid(abt)
        dab = jnp.where(is_g, dg * slope, jnp.where(lane < 2 * DN_HEADS, dgbt * beta * (1.0 - beta), 0.0))
        return du, dab, _colsum(dg * gval), _colsum(dg * slope)
    ins = [(dq, "row"), (dk, "row"), (dv, "row"), (dgb, "row"), (u, "row"), (ab, "row"), (alog_row, "full"), (dt_row, "full")]
    return _rows("dn_prep_bwd", fn, ins, [(DN_QKV, F32), (LANE, BF16)], tr=256, accs=[((1, LANE), F32)] * 2)


def _dn_conv_bwd(du, dz, qkvz, convw):
    tr = 256

    def fn(i, n, dut, dun, dzt, x, xp, w):
        dun = jnp.where(i < n - 1, dun, 0.0)
        dus = jnp.concatenate([dut, dun], axis=0)
        xs = jnp.concatenate([jnp.where(i > 0, xp, 0.0), x], axis=0)
        dx = None
        dws = []
        for j in range(CONV_W):
            sh = CONV_W - 1 - j
            term = (pltpu.roll(dus, tr + SUBLANE - sh, 0) if sh else dus)[:tr] * w[j:j + 1, :]
            dx = term if dx is None else dx + term
            dws.append(_colsum(dut * (pltpu.roll(xs, sh, 0) if sh else xs)[SUBLANE:]))
        return (jnp.concatenate([dx.astype(BF16), dzt.astype(BF16)], axis=1), *dws)

    ins = [(du, "row"), (du, "next8"), (dz, "row"), (qkvz, "row", (0, DN_QKV)), (qkvz, "prev8", (0, DN_QKV)), (convw, "full")]
    res = _rows("dn_conv_bwd", fn, ins, [(DN_QKVZ, BF16)], tr=tr, accs=[((1, DN_QKV), F32)] * CONV_W)
    return res[0], res[1:]


def _add(acc, r):
    return (r + acc,)


def _mlp_ple_fwd(i, x1, hm, p_i, ple_gain, next_gain, w_up, w_down, w_ple, w_gate, target=None):
    u, a = _mm(f"mlp_up{i}", hm, w_up, epilogue=lambda acc: (acc, jnp.square(jnp.maximum(acc, 0.0))),
               out_dtypes=(BF16, BF16))
    x2, hp = _mm(f"mlp_down{i}", a, w_down, epilogue=_res_norm, extras=(x1, ple_gain), out_dtypes=(F32, BF16),
                 tm_pref=FUSED_ROWS)
    pp = _mm(f"ple_proj{i}", p_i, w_ple)

    def gate_epilogue(acc, x2t, ppt, g):
        gate = _sigmoid(acc)
        x3 = x2t + ppt * gate
        return x3, gate, x3 * lax.rsqrt(jnp.mean(x3 * x3, axis=-1, keepdims=True) + EPS) * g

    def loss_epilogue(acc, x2t, ppt, tt):
        gate = _sigmoid(acc)
        err = x2t + ppt * gate - tt
        dy = err * (1.0 / D_MODEL)
        return dy, dy * gate, dy * ppt * gate * (1.0 - gate), _colsum(err * err)

    saved = dict(x1=x1, hm=hm, u=u, a=a, x2=x2, hp=hp, pp=pp, p=p_i)
    if target is None:
        x3, saved["gate"], h_next = _mm(f"ple_gate{i}", hp, w_gate, epilogue=gate_epilogue, extras=(x2, pp, next_gain),
                                        out_dtypes=(F32, F32, BF16), tm_pref=FUSED_ROWS)
        return x3, h_next, saved
    dy, saved["dpp"], saved["dzg"], sq = _mm(f"ple_gate{i}", hp, w_gate, epilogue=loss_epilogue, extras=(x2, pp, target),
                                             out_dtypes=(F32, BF16, BF16), n_colsums=1, tm_pref=FUSED_ROWS)
    return dy, sq, saved


def _mlp_ple_bwd(i, dx3, sv, mlp_gain, ple_gain, w_up, w_down, w_gate):
    if "dpp" in sv:
        dpp, dzg = sv["dpp"], sv["dzg"]
    else:
        def fn(_i, _n, d, g, pp):
            return d * g, d * pp * g * (1.0 - g)
        dpp, dzg = _rows(f"ple_gate_bwd{i}", fn, [(dx3, "row"), (sv["gate"], "row"), (sv["pp"], "row")],
                         [(D_MODEL, BF16), (D_MODEL, BF16)], tr=512)
    d_w_ple = _mm(f"ple_proj_dw{i}", sv["p"], dpp, ta=True, out_dtypes=(BF16,))
    d_w_gate = _mm(f"ple_gate_dw{i}", sv["hp"], dzg, ta=True, out_dtypes=(BF16,))
    dx2, dx2b, d_ple_gain = _mm(f"ple_gate_dx{i}", dzg, w_gate, tb=True, epilogue=_norm_bwd_2,
                                extras=(sv["x2"], ple_gain, dx3), out_dtypes=(F32, BF16), n_colsums=1, tm_pref=FUSED_ROWS)
    d_w_down = _mm(f"mlp_down_dw{i}", sv["a"], dx2b, ta=True, out_dtypes=(BF16,))
    du = _mm(f"mlp_down_dx{i}", dx2b, w_down, tb=True,
             epilogue=lambda acc, ut: (acc * (2.0 * jnp.maximum(ut.astype(F32), 0.0)),), extras=(sv["u"],), out_dtypes=(BF16,))
    d_w_up = _mm(f"mlp_up_dw{i}", sv["hm"], du, ta=True, out_dtypes=(BF16,))
    dx1, dx1b, d_mlp_gain = _mm(f"mlp_up_dx{i}", du, w_up, tb=True, epilogue=_norm_bwd_2,
                                extras=(sv["x1"], mlp_gain, dx2), out_dtypes=(F32, BF16), n_colsums=1, tm_pref=FUSED_ROWS)
    return dx1, dx1b, dict(w_ple=d_w_ple, w_ple_gate=d_w_gate, w_down=d_w_down, w_up=d_w_up,
                           ple_norm=d_ple_gain, mlp_norm=d_mlp_gain)


def _after(small, token):
    return small + token[0:1, 0:1]


def _local_step(x, p, positions, target, W, P, rest_of_weights, send_layer1, send_mlp0, send_attn):
    consts = _head_consts()
    bd = _block_diag(1.0 / A_HEAD_DIM)
    bd1 = _block_diag(1.0)
    ct, st = _rope_tables(positions, consts)
    gains = jnp.stack([jnp.tile(v, A_HEADS) for g in range(3) for v in (P["attn_q_gain"][g], P["attn_k_gain"][g])])
    pad = LANE - DN_HEADS
    alog_row = jnp.pad(P["dn_a_log"].reshape(1, DN_HEADS), ((0, 0), (0, pad)))
    dt_row = jnp.pad(P["dn_dt_bias"].reshape(1, DN_HEADS), ((0, 0), (0, pad)))
    ogain_row = P["dn_o_gain"].reshape(1, DN_DIM)
    row = lambda name, i: P[name][i:i + 1]

    h0 = _rmsnorm_fwd("mix_norm0", x, row("mix_norm", 0))
    qkv = _mm("attn_qkv", h0, W["attn_w_qkv"], out_dtypes=(BF16,))
    qkvn = _attn_prep(qkv, gains, ct, st, consts, bd)
    os_, lses = zip(*[_attn_fwd(qkvn[g], g) for g in range(3)])
    o_attn = _attn_merge(os_, lses)
    x1, hm0 = _mm("attn_out", o_attn, W["attn_w_o"], epilogue=_res_norm, extras=(x, row("mlp_norm", 0)),
                  out_dtypes=(F32, BF16), tm_pref=FUSED_ROWS)
    W = {**W, **rest_of_weights(x1)}
    x3, h1, sv0 = _mlp_ple_fwd(0, x1, hm0, p[0], row("ple_norm", 0), row("mix_norm", 1),
                               W["w_up"][0], W["w_down"][0], W["w_ple"][0], W["w_ple_gate"][0])
    qkvz = _mm("dn_in_qkvz", h1, W["dn_w_qkvz"])
    ab = _mm("dn_in_ab", h1, W["dn_w_ab"])
    u, q, k, v, gb = _dn_prep(qkvz, ab, W["dn_conv"], alog_row, dt_row)
    o_dn, states = _dn_chunk_fwd(q, k, v, gb)
    on = _dn_post(o_dn, qkvz, ogain_row)
    x4, hm1 = _mm("dn_out", on, W["dn_w_o"], epilogue=_res_norm, extras=(x3, row("mlp_norm", 1)),
                  out_dtypes=(F32, BF16), tm_pref=FUSED_ROWS)
    dy, sq, sv1 = _mlp_ple_fwd(1, x4, hm1, p[1], row("ple_norm", 1), None,
                               W["w_up"][1], W["w_down"][1], W["w_ple"][1], W["w_ple_gate"][1], target=target)

    dx4, dx4b, g1 = _mlp_ple_bwd(1, dy, sv1, row("mlp_norm", 1), row("ple_norm", 1),
                                 W["w_up"][1], W["w_down"][1], W["w_ple_gate"][1])
    don = _mm("dn_out_dx", dx4b, W["dn_w_o"], tb=True)
    d_dn_w_o = _mm("dn_out_dw", on, dx4b, ta=True, out_dtypes=(BF16,))
    do_dn, dz, d_ogain = _dn_post_bwd(don, o_dn, qkvz, ogain_row)
    dq, dk, dv, dgb = _dn_chunk_bwd(q, k, v, gb, states, do_dn)
    du, dab, d_alog, d_dt = _dn_prep_bwd(dq, dk, dv, dgb, u, ab, alog_row, dt_row)
    dqkvz, d_conv = _dn_conv_bwd(du, dz, qkvz, W["dn_conv"])
    dh1 = _mm("dn_in_ab_dx", dab, W["dn_w_ab"], tb=True)
    dx3, d_mix1 = _mm("dn_in_qkvz_dx", dqkvz, W["dn_w_qkvz"], tb=True,
                      epilogue=lambda acc, part, xt, g, dres: _norm_bwd(acc + part, xt, g, dres),
                      extras=(dh1, x3, row("mix_norm", 1), dx4), n_colsums=1, tm_pref=FUSED_ROWS)
    d_w_qkvz = _mm("dn_in_qkvz_dw", h1, dqkvz, ta=True, out_dtypes=(BF16,))
    d_w_ab = _mm("dn_in_ab_dw", h1, dab, ta=True, out_dtypes=(BF16,))
    token = send_layer1(dict(
        dn_w_qkvz=d_w_qkvz, dn_w_ab=d_w_ab, dn_conv=jnp.concatenate(d_conv, 0), dn_w_o=d_dn_w_o,
        w_up=g1["w_up"], w_down=g1["w_down"], w_ple=g1["w_ple"], w_ple_gate=g1["w_ple_gate"]))
    dx1, dx1b, g0 = _mlp_ple_bwd(0, dx3, sv0, row("mlp_norm", 0), _after(row("ple_norm", 0), token),
                                 W["w_up"][0], W["w_down"][0], W["w_ple_gate"][0])
    token = send_mlp0(dict(w_up=g0["w_up"], w_down=g0["w_down"], w_ple=g0["w_ple"], w_ple_gate=g0["w_ple_gate"]))
    do_attn = _mm("attn_out_dx", dx1b, W["attn_w_o"], tb=True, epilogue=_add, extras=(_after(jnp.zeros((1, A_WIDTH), F32), token),))
    d_attn_w_o = _mm("attn_out_dw", o_attn, dx1b, ta=True, out_dtypes=(BF16,))
    dos, cs = _attn_merge_bwd(do_attn, os_, lses, bd1)
    grads9 = []
    for g in range(3):
        grads9 += list(_attn_bwd(qkvn[g], g, dos[g], lses[g], cs[g]))
    dqkv, dgains = _attn_prep_bwd(qkv, grads9, gains, ct, st, consts, bd)
    d_attn_w_qkv = _mm("attn_qkv_dw", h0, dqkv, ta=True, out_dtypes=(BF16,))
    token = send_attn(dict(attn_w_qkv=d_attn_w_qkv, attn_w_o=d_attn_w_o))
    dx0, d_mix0 = _mm("attn_qkv_dx", dqkv, W["attn_w_qkv"], tb=True, epilogue=_norm_bwd,
                      extras=(x, _after(row("mix_norm", 0), token), dx1), n_colsums=1, tm_pref=FUSED_ROWS)

    dg = jnp.stack([t.reshape(A_HEADS, A_HEAD_DIM).sum(0) for t in dgains])
    small = dict(
        mix_norm=jnp.concatenate([d_mix0, d_mix1], 0),
        attn_q_gain=dg[0::2][None], attn_k_gain=dg[1::2][None],
        dn_a_log=d_alog[:, :DN_HEADS], dn_dt_bias=d_dt[:, :DN_HEADS], dn_o_gain=d_ogain,
        mlp_norm=jnp.concatenate([g0["mlp_norm"], g1["mlp_norm"]], 0),
        ple_norm=jnp.concatenate([g0["ple_norm"], g1["ple_norm"]], 0),
    )
    return sq, dx0, small


MESH_IDS = pl.DeviceIdType.MESH
ANY = pl.BlockSpec(memory_space=pl.ANY)


def _place():
    return lax.axis_index("x"), lax.axis_index("y"), lax.axis_index("c")


def _sem_scratch(n_streams):
    return [pltpu.SemaphoreType.DMA((n_streams, N_DEV - 1)), pltpu.SemaphoreType.DMA((n_streams, N_DEV - 1)),
            pltpu.SemaphoreType.DMA((n_streams,))]


def _all_gather(name, arrays, streams):
    n_in, n_st = len(arrays), len(streams)
    shapes = [arrays[a].shape if li is None else arrays[a].shape[1:] for a, li in streams]

    def body(*refs):
        in_refs, out_refs, token = refs[:n_in], refs[n_in:n_in + n_st], refs[n_in + n_st]
        send_sems, recv_sems, local_sems = refs[n_in + n_st + 1:]
        token[...] = jnp.zeros_like(token)
        x, y, c = _place()
        me, sibling = (x, y, c), (x, y, 1 - c)
        chips = [(1 - x, y), (x, 1 - y), (1 - x, 1 - y)]

        def copy(s, k, block, to, own=False):
            a, li = streams[s]
            dst = out_refs[s].at[4 * block[0] + 2 * block[1] + block[2]]
            src = (in_refs[a] if li is None else in_refs[a].at[li]) if own else dst
            return pltpu.make_async_remote_copy(src_ref=src, dst_ref=dst, send_sem=send_sems.at[s, k],
                                                recv_sem=recv_sems.at[s, k], device_id=to, device_id_type=MESH_IDS)

        started = []
        for s, (a, li) in enumerate(streams):
            src = in_refs[a] if li is None else in_refs[a].at[li]
            mine = pltpu.make_async_copy(src, out_refs[s].at[4 * x + 2 * y + c], local_sems.at[s])
            mine.start()
            started.append(mine)
        sends = []
        for s in range(n_st):
            first = [copy(s, 0, me, sibling, own=True)]
            first += [copy(s, 1 + j, me, (*chip, c), own=True) for j, chip in enumerate(chips)]
            for cp in first:
                cp.start()
            sends += first
        for j, chip in enumerate(chips):
            for s in range(n_st):
                copy(s, 1 + j, (*chip, c), me).wait_recv()
                fwd = copy(s, 4 + j, (*chip, c), sibling)
                fwd.start()
                sends.append(fwd)
        for s in range(n_st):
            copy(s, 0, sibling, me).wait_recv()
            for j, chip in enumerate(chips):
                copy(s, 4 + j, (*chip, 1 - c), me).wait_recv()
        for cp in sends:
            cp.wait_send()
        for cp in started:
            cp.wait()

    res = pl.pallas_call(
        body, name=name,
        out_shape=[jax.ShapeDtypeStruct((N_DEV,) + tuple(sh), arrays[a].dtype) for sh, (a, _) in zip(shapes, streams)]
        + [jax.ShapeDtypeStruct((SUBLANE, LANE), F32)],
        in_specs=[ANY] * n_in, out_specs=[ANY] * n_st + [pl.BlockSpec(memory_space=pltpu.VMEM)],
        scratch_shapes=_sem_scratch(n_st),
    )(*arrays)
    return list(res[:n_st]), res[n_st]


HBM = pl.BlockSpec(memory_space=pltpu.HBM)
SEM = pl.BlockSpec(memory_space=pltpu.SEMAPHORE)
FLOWS = pltpu.CompilerParams(has_side_effects=pltpu.SideEffectType.DATAFLOW_SIDE_EFFECTING)


def _in_hbm(a):
    return pltpu.with_memory_space_constraint(a, pltpu.HBM)


def _hbm_like(a):
    return pltpu.HBM(a.shape, a.dtype)


def _peers(x, y, c):
    return [(1 - x if k & 4 else x, 1 - y if k & 2 else y, 1 - c if k & 1 else c) for k in range(1, N_DEV)]


def _start_copies(name, n_remote, n_own, make_copies, operands):
    n = len(operands)

    def body(*refs):
        for cp in make_copies(refs[:n], refs[n], refs[n + 1], refs[n + 2]):
            cp.start()
        refs[-1][...] = jnp.zeros_like(refs[-1])

    res = pl.pallas_call(
        body, name=name,
        out_shape=(pltpu.SemaphoreType.DMA((n_remote,)), pltpu.SemaphoreType.DMA((n_remote,)), pltpu.SemaphoreType.DMA((n_own,)),
                   *[_hbm_like(t) for t in operands], jax.ShapeDtypeStruct((SUBLANE, LANE), F32)),
        in_specs=[HBM] * n, out_specs=(SEM, SEM, SEM, *[HBM] * n, pl.BlockSpec(memory_space=pltpu.VMEM)),
        input_output_aliases={i: 3 + i for i in range(n)}, compiler_params=FLOWS,
    )(*[_in_hbm(t) for t in operands])
    return res[:3], list(res[3:3 + n]), res[-1]


def _wait_copies(name, make_waits, sems, operands, after):
    n = len(operands)

    def body(*refs):
        for wait in make_waits(refs[:n], refs[n], refs[n + 1], refs[n + 2]):
            wait()

    res = pl.pallas_call(
        body, name=name, out_shape=tuple(_hbm_like(t) for t in operands),
        in_specs=[HBM] * n + [SEM, SEM, SEM, ANY], out_specs=tuple([HBM] * n),
        input_output_aliases={i: i for i in range(n)}, compiler_params=FLOWS,
    )(*operands, *sems, after)
    return list(res)


def _gather_plan(n_in, streams):
    def block(arr, s):
        a, li = streams[s]
        return arr[a] if li is None else arr[a].at[li]

    def copies(refs, send_sems, recv_sems, own_sems, arrivals=False):
        arr, land = refs[:n_in], refs[n_in:]
        x, y, c = _place()
        me = 4 * x + 2 * y + c
        out = []
        for s in range(len(streams)):
            out.append(("own", pltpu.make_async_copy(block(arr, s), land[s].at[me], own_sems.at[s])))
            for k, (px, py, pc) in enumerate(_peers(x, y, c)):
                out.append(("remote", pltpu.make_async_remote_copy(
                    src_ref=block(arr, s), dst_ref=land[s].at[4 * px + 2 * py + pc if arrivals else me],
                    send_sem=send_sems.at[s * (N_DEV - 1) + k], recv_sem=recv_sems.at[s * (N_DEV - 1) + k],
                    device_id=(px, py, pc), device_id_type=MESH_IDS)))
        return out
    return copies


def _exchange_plan(n_st):
    def copies(refs, send_sems, recv_sems, own_sems, arrivals=False):
        snd, rcv = refs[:n_st], refs[n_st:]
        x, y, c = _place()
        me = 4 * x + 2 * y + c
        out = []
        for s in range(n_st):
            out.append(("own", pltpu.make_async_copy(snd[s].at[me], rcv[s].at[me], own_sems.at[s])))
            for k, (px, py, pc) in enumerate(_peers(x, y, c)):
                peer = 4 * px + 2 * py + pc
                out.append(("remote", pltpu.make_async_remote_copy(
                    src_ref=snd[s].at[peer], dst_ref=rcv[s].at[peer if arrivals else me],
                    send_sem=send_sems.at[s * (N_DEV - 1) + k], recv_sem=recv_sems.at[s * (N_DEV - 1) + k],
                    device_id=(px, py, pc), device_id_type=MESH_IDS)))
        return out
    return copies


def _split_transfer(tag, plan, n_streams, operands):
    sems, operands, token = _start_copies(f"{tag}_start", n_streams * (N_DEV - 1), n_streams,
                                          lambda refs, a, b, o: [cp for _, cp in plan(refs, a, b, o)], operands)

    def waits(refs, a, b, o):
        out = []
        for kind, cp in plan(refs, a, b, o, arrivals=True):
            out += [cp.wait] if kind == "own" else [cp.wait_send, cp.wait_recv]
        return out

    return (lambda after: _wait_copies(f"{tag}_wait", waits, sems, operands, after)), token


def _gather_async(tag, arrays, streams):
    lands = [lax.empty((N_DEV,) + tuple(arrays[a].shape if li is None else arrays[a].shape[1:]), arrays[a].dtype)
             for a, li in streams]
    finish, token = _split_transfer(tag, _gather_plan(len(arrays), streams), len(streams), list(arrays) + lands)
    return (lambda after: finish(after)[len(arrays):]), token


def _exchange_async(tag, sends):
    recvs = [lax.empty(t.shape, t.dtype) for t in sends]
    finish, token = _split_transfer(tag, _exchange_plan(len(sends)), len(sends), list(sends) + recvs)
    return (lambda after: finish(after)[len(sends):]), token


def _dn_in_pieces():
    n = (DN_QKVZ + 2 * DN_HEADS) // N_DEV
    segs = ((0, DN_QKV, 0, 0), (DN_QKV, DN_QKV + 2 * DN_HEADS, 1, 0), (DN_QKV + 2 * DN_HEADS, DN_QKVZ + 2 * DN_HEADS, 0, DN_QKV))
    out = []
    for d in range(N_DEV):
        lo, hi = d * n, (d + 1) * n
        for s0, s1, tgt, t0 in segs:
            a, b = max(lo, s0), min(hi, s1)
            if a < b:
                out.append((d, a - lo, b - lo, tgt, t0 + a - s0))
    return out


def _unpack_cols(name, g):
    _, K, n = g.shape
    tr = 256

    def body(g_ref, o_ref):
        for d in range(N_DEV):
            o_ref[:, d * n:(d + 1) * n] = g_ref[d]

    return pl.pallas_call(
        body, name=name, grid=(K // tr,), in_specs=[pl.BlockSpec((N_DEV, tr, n), lambda i: (0, i, 0))],
        out_specs=pl.BlockSpec((tr, N_DEV * n), lambda i: (i, 0)),
        out_shape=jax.ShapeDtypeStruct((K, N_DEV * n), g.dtype), compiler_params=_cparams(("parallel",)),
    )(g)


def _pack_cols(name, w):
    K, n = w.shape[0], w.shape[1] // N_DEV
    tr = 256

    def body(w_ref, o_ref):
        for d in range(N_DEV):
            o_ref[d] = w_ref[:, d * n:(d + 1) * n]

    return pl.pallas_call(
        body, name=name, grid=(K // tr,), in_specs=[pl.BlockSpec((tr, N_DEV * n), lambda i: (i, 0))],
        out_specs=pl.BlockSpec((N_DEV, tr, n), lambda i: (0, i, 0)),
        out_shape=jax.ShapeDtypeStruct((N_DEV, K, n), w.dtype), compiler_params=_cparams(("parallel",)),
    )(w)


def _unpack_dn_in(g):
    _, K, n = g.shape
    tr = 256

    def body(g_ref, qkvz_ref, ab_ref):
        ab_ref[...] = jnp.zeros_like(ab_ref)
        for d, c0, c1, tgt, t0 in _dn_in_pieces():
            (qkvz_ref, ab_ref)[tgt][:, t0:t0 + c1 - c0] = g_ref[d, :, c0:c1]

    return pl.pallas_call(
        body, name="unpack_dn_in", grid=(K // tr,), in_specs=[pl.BlockSpec((N_DEV, tr, n), lambda i: (0, i, 0))],
        out_specs=[pl.BlockSpec((tr, DN_QKVZ), lambda i: (i, 0)), pl.BlockSpec((tr, LANE), lambda i: (i, 0))],
        out_shape=[jax.ShapeDtypeStruct((K, DN_QKVZ), g.dtype), jax.ShapeDtypeStruct((K, LANE), g.dtype)],
        compiler_params=_cparams(("parallel",)),
    )(g)


def _pack_dn_in(d_qkvz, d_ab):
    K = d_qkvz.shape[0]
    n = (DN_QKVZ + 2 * DN_HEADS) // N_DEV
    tr = 256

    def body(qkvz_ref, ab_ref, o_ref):
        for d, c0, c1, tgt, t0 in _dn_in_pieces():
            o_ref[d, :, c0:c1] = (qkvz_ref, ab_ref)[tgt][:, t0:t0 + c1 - c0]

    return pl.pallas_call(
        body, name="pack_dn_in", grid=(K // tr,),
        in_specs=[pl.BlockSpec((tr, DN_QKVZ), lambda i: (i, 0)), pl.BlockSpec((tr, LANE), lambda i: (i, 0))],
        out_specs=pl.BlockSpec((N_DEV, tr, n), lambda i: (0, i, 0)),
        out_shape=jax.ShapeDtypeStruct((N_DEV, K, n), d_qkvz.dtype), compiler_params=_cparams(("parallel",)),
    )(d_qkvz, d_ab)


ADAMW_ROWS = 256


def _adamw(name, parts, w, m, v):
    R, C = w.shape
    tr = min(R, ADAMW_ROWS)
    assert R % tr == 0 and parts.shape == (N_DEV, R, C)
    c1 = 1.0 - B1 ** STEP
    c2 = 1.0 - B2 ** STEP

    def body(p_ref, w_ref, m_ref, v_ref, g_ref, d_ref, nm_ref, nv_ref):
        g = p_ref[0].astype(F32)
        for dev in range(1, N_DEV):
            g = g + p_ref[dev].astype(F32)
        nm = B1 * m_ref[...] + (1.0 - B1) * g
        nv = B2 * v_ref[...] + (1.0 - B2) * jnp.square(g)
        g_ref[...] = g
        nm_ref[...] = nm
        nv_ref[...] = nv
        d_ref[...] = -LR * ((nm / c1) / (jnp.sqrt(nv / c2) + ADAM_EPS) + WD * w_ref[...])

    blk = pl.BlockSpec((tr, C), lambda i: (i, 0))
    return pl.pallas_call(
        body, name=name, grid=(R // tr,),
        in_specs=[pl.BlockSpec((N_DEV, tr, C), lambda i: (0, i, 0)), blk, blk, blk],
        out_specs=[blk] * 4, out_shape=[jax.ShapeDtypeStruct((R, C), F32)] * 4,
        compiler_params=_cparams(("parallel",)),
    )(parts, w, m, v)


SMALL = ("mix_norm", "attn_q_gain", "attn_k_gain", "dn_a_log", "dn_dt_bias", "dn_o_gain", "mlp_norm", "ple_norm")
WEIGHTS = ("mix_norm", "attn_w_qkv", "attn_q_gain", "attn_k_gain", "attn_w_o", "dn_w_in", "dn_conv", "dn_a_log",
           "dn_dt_bias", "dn_o_gain", "dn_w_o", "mlp_norm", "w_up", "w_down", "ple_norm", "w_ple", "w_ple_gate")


def _to_rows(flat, multiple):
    n = flat.shape[-1]
    rows = -(-n // (LANE * multiple)) * multiple
    return jnp.pad(flat, [(0, rows * LANE - n)]).reshape(rows, LANE)


def _cols_to_devices(w):
    K, N = w.shape
    return jnp.transpose(w.reshape(K, N_DEV, N // N_DEV), (1, 0, 2))


def _cols_from_devices(g):
    _, K, n = g.shape
    return jnp.transpose(g, (1, 0, 2)).reshape(K, N_DEV * n)


SMALL_ROWS = 96


def _pack_small(vals, loss_rows):
    rows = [_to_rows(vals[n].reshape(-1), SUBLANE) for n in SMALL] + [loss_rows]
    buf = jnp.concatenate(rows, 0)
    assert buf.shape == (SMALL_ROWS, LANE)
    return buf


def _unpack_small(buf, like):
    out, r = {}, 0
    for n in SMALL:
        sz = math.prod(like[n].shape)
        out[n] = buf[r:r + -(-sz // LANE)].reshape(-1)[:sz].reshape(like[n].shape)
        r += -(-sz // (LANE * SUBLANE)) * SUBLANE
    return out


def kernel(x, p, positions, mix_norm, attn_w_qkv, attn_q_gain, attn_k_gain, attn_w_o, dn_w_in, dn_conv, dn_a_log, dn_dt_bias, dn_o_gain, dn_w_o, mlp_norm, w_up, w_down, ple_norm, w_ple, w_ple_gate, loss_target, m_mix_norm, m_attn_w_qkv, m_attn_q_gain, m_attn_k_gain, m_attn_w_o, m_dn_w_in, m_dn_conv, m_dn_a_log, m_dn_dt_bias, m_dn_o_gain, m_dn_w_o, m_mlp_norm, m_w_up, m_w_down, m_ple_norm, m_w_ple, m_w_ple_gate, v_mix_norm, v_attn_w_qkv, v_attn_q_gain, v_attn_k_gain, v_attn_w_o, v_dn_w_in, v_dn_conv, v_dn_a_log, v_dn_dt_bias, v_dn_o_gain, v_dn_w_o, v_mlp_norm, v_w_up, v_w_down, v_ple_norm, v_w_ple, v_w_ple_gate):
    w = dict(mix_norm=mix_norm, attn_w_qkv=attn_w_qkv, attn_q_gain=attn_q_gain, attn_k_gain=attn_k_gain, attn_w_o=attn_w_o,
             dn_w_in=dn_w_in, dn_conv=dn_conv, dn_a_log=dn_a_log, dn_dt_bias=dn_dt_bias, dn_o_gain=dn_o_gain, dn_w_o=dn_w_o,
             mlp_norm=mlp_norm, w_up=w_up, w_down=w_down, ple_norm=ple_norm, w_ple=w_ple, w_ple_gate=w_ple_gate)
    m = dict(mix_norm=m_mix_norm, attn_w_qkv=m_attn_w_qkv, attn_q_gain=m_attn_q_gain, attn_k_gain=m_attn_k_gain,
             attn_w_o=m_attn_w_o, dn_w_in=m_dn_w_in, dn_conv=m_dn_conv, dn_a_log=m_dn_a_log, dn_dt_bias=m_dn_dt_bias,
             dn_o_gain=m_dn_o_gain, dn_w_o=m_dn_w_o, mlp_norm=m_mlp_norm, w_up=m_w_up, w_down=m_w_down,
             ple_norm=m_ple_norm, w_ple=m_w_ple, w_ple_gate=m_w_ple_gate)
    v = dict(mix_norm=v_mix_norm, attn_w_qkv=v_attn_w_qkv, attn_q_gain=v_attn_q_gain, attn_k_gain=v_attn_k_gain,
             attn_w_o=v_attn_w_o, dn_w_in=v_dn_w_in, dn_conv=v_dn_conv, dn_a_log=v_dn_a_log, dn_dt_bias=v_dn_dt_bias,
             dn_o_gain=v_dn_o_gain, dn_w_o=v_dn_w_o, mlp_norm=v_mlp_norm, w_up=v_w_up, w_down=v_w_down,
             ple_norm=v_ple_norm, w_ple=v_w_ple, w_ple_gate=v_w_ple_gate)
    S = x.shape[1]

    bf = lambda a: a.astype(BF16)
    rows_to_devices = lambda t: t.reshape(N_DEV, t.shape[0] // N_DEV, t.shape[1])

    (g_qkv, g_ao), token = _all_gather("gather_attn", [bf(attn_w_qkv[0]), bf(attn_w_o[0])], [(0, None), (1, None)])
    rest_shards = [bf(dn_w_in[0]), bf(dn_w_o[0]), bf(w_up), bf(w_down), bf(w_ple), bf(w_ple_gate), _after(dn_conv[0], token)]
    rest_streams = [(0, None), (1, None), (2, 0), (2, 1), (3, 0), (3, 1), (4, 0), (4, 1), (5, 0), (5, 1), (6, None)]
    rest_arrived, token = _gather_async("gather_rest", rest_shards, rest_streams)
    W = dict(attn_w_qkv=_unpack_cols("unpack_attn_qkv", g_qkv), attn_w_o=_cols_from_devices(g_ao))

    def rest_of_weights(after):
        g_in, g_do, g_up0, g_up1, g_dn0, g_dn1, g_pl0, g_pl1, g_gt0, g_gt1, g_conv = rest_arrived(after)
        rest = dict(
            dn_conv=jnp.transpose(g_conv, (1, 0, 2)).reshape(CONV_W, DN_QKV), dn_w_o=g_do.reshape(DN_WIDTH, D_MODEL),
            w_up=[_cols_from_devices(g_up0), _cols_from_devices(g_up1)],
            w_down=[g_dn0.reshape(D_FF, D_MODEL), g_dn1.reshape(D_FF, D_MODEL)],
            w_ple=[_cols_from_devices(g_pl0), _cols_from_devices(g_pl1)],
            w_ple_gate=[g_gt0.reshape(D_MODEL, D_MODEL), g_gt1.reshape(D_MODEL, D_MODEL)])
        rest["dn_w_qkvz"], rest["dn_w_ab"] = _unpack_dn_in(g_in)
        return rest

    pending = {}

    def mlp_sends(g):
        return [_cols_to_devices(g["w_up"]), rows_to_devices(g["w_down"]), _cols_to_devices(g["w_ple"]),
                rows_to_devices(g["w_ple_gate"])]

    def start(tag, sends):
        pending[tag], token = _exchange_async(f"exchange_{tag}", sends)
        return token

    def send_layer1(g):
        conv_send = jnp.transpose(g["dn_conv"].reshape(CONV_W, N_DEV, DN_QKV // N_DEV), (1, 0, 2))
        return start("layer1", [_pack_dn_in(g["dn_w_qkvz"], g["dn_w_ab"]), conv_send, rows_to_devices(g["dn_w_o"])] + mlp_sends(g))

    def send_mlp0(g):
        return start("mlp0", mlp_sends(g))

    def send_attn(g):
        return start("attn", [_pack_cols("pack_attn_qkv", g["attn_w_qkv"]), _cols_to_devices(g["attn_w_o"])])

    P = dict(mix_norm=_after(mix_norm, token), attn_q_gain=attn_q_gain[0], attn_k_gain=attn_k_gain[0], dn_a_log=dn_a_log[0],
             dn_dt_bias=dn_dt_bias[0], dn_o_gain=dn_o_gain[0], mlp_norm=mlp_norm, ple_norm=ple_norm)

    sq, dx0, small_g = _local_step(x[0], p[:, 0], positions.reshape(S, 1), loss_target[0], W, P,
                                   rest_of_weights, send_layer1, send_mlp0, send_attn)

    r_in, r_conv, r_do, r_up1, r_dn1, r_pl1, r_gt1 = pending["layer1"](dx0)
    r_up0, r_dn0, r_pl0, r_gt0 = pending["mlp0"](dx0)
    r_qkv, r_ao = pending["attn"](dx0)
    big = {}
    for n, parts in (("attn_w_qkv", [r_qkv]), ("attn_w_o", [r_ao]), ("dn_w_in", [r_in]), ("dn_conv", [r_conv]),
                     ("dn_w_o", [r_do]), ("w_up", [r_up0, r_up1]), ("w_down", [r_dn0, r_dn1]),
                     ("w_ple", [r_pl0, r_pl1]), ("w_ple_gate", [r_gt0, r_gt1])):
        layers = [_adamw(f"adamw_{n}{l}", pt, w[n][l], m[n][l], v[n][l]) for l, pt in enumerate(parts)]
        big[n] = [jnp.stack([res[k] for res in layers]) for k in range(4)]

    loss_rows = jnp.pad((0.5 / D_MODEL) * jnp.sum(sq, axis=1, keepdims=True), ((0, SUBLANE - 1), (0, LANE - 1)))
    small_like = {n: w[n] for n in SMALL}
    parts_s = _all_gather("gather_small", [_pack_small(small_g, loss_rows)], [(0, None)])[0][0]
    zero_rows = jnp.zeros((SUBLANE, LANE), F32)
    small = _adamw("adamw_small", parts_s, _pack_small(w, zero_rows), _pack_small(m, zero_rows), _pack_small(v, zero_rows))
    loss = small[0][SMALL_ROWS - SUBLANE, 0]
    small = [_unpack_small(b, small_like) for b in small]

    outs = [loss, dx0[None]]
    for k in range(4):
        for n in WEIGHTS:
            outs.append(small[k][n] if n in SMALL else big[n][k])
    return tuple(outs)
```

```python
import functools
import math

import jax
import jax.numpy as jnp
from jax import lax
from jax.experimental import pallas as pl
from jax.experimental.pallas import tpu as pltpu

F32 = jnp.float32
BF16 = jnp.bfloat16
HIGHEST = lax.Precision.HIGHEST

N_DEV = 8
D_MODEL = 1024
EPS = 1e-6
SWA_GROUPS = ((128, 1), (512, 4), (2048, 16))
A_HEADS = 8
A_HEAD_DIM = 64
A_WIDTH = A_HEADS * A_HEAD_DIM
A_QKV = 3 * 3 * A_WIDTH
ROPE_DIM = 16
ROPE_HALF = 8
ROPE_THETA = 500000.0
BAND = 128
DN_HEADS = 8
DN_DIM = 128
DN_WIDTH = DN_HEADS * DN_DIM
CONV_W = 4
CHUNK = 64
D_FF = 4 * D_MODEL
PLE_DIM = 256
LR, B1, B2, ADAM_EPS, WD, STEP = 0.001, 0.9, 0.999, 1e-08, 0.01, 10

VMEM_LIMIT = 56 * 1024 * 1024
MXU_TILE = 1024
MM_SLAB = 256
LANE = 128
SUBLANE = 8


def _cparams(sem):
    return pltpu.CompilerParams(dimension_semantics=sem, vmem_limit_bytes=VMEM_LIMIT)


def _tile(n, pref):
    if n <= pref:
        return n
    t = (pref // LANE) * LANE
    while t >= LANE:
        if n % t == 0:
            return t
        t -= LANE
    raise ValueError(f"no tile for {n}")


def _dot(a, b, ca=1, cb=0, precision=None):
    return lax.dot_general(a, b, (((ca,), (cb,)), ((), ())), precision=precision,
                           preferred_element_type=F32)


def _bdot(a, b, ca=1, cb=0):
    return _dot(a.astype(BF16), b.astype(BF16), ca, cb)


def _mm(name, a, b, *, ta=False, tb=False, epilogue=None, extras=(), out_dtypes=(F32,), n_colsums=0,
        tm_pref=MXU_TILE, tn_pref=1536, tk_pref=MXU_TILE):
    M, K = (a.shape[1], a.shape[0]) if ta else a.shape
    N = b.shape[0] if tb else b.shape[1]
    assert (b.shape[1] if tb else b.shape[0]) == K
    tm, tn, tk = _tile(M, tm_pref), _tile(N, tn_pref), _tile(K, tk_pref)
    nk = K // tk
    n_out = len(out_dtypes)
    n_ext = len(extras)
    assert n_colsums == 0 or tn == N
    sub = min(tm, MM_SLAB)

    def body(*refs):
        a_ref, b_ref = refs[0], refs[1]
        ext = refs[2:2 + n_ext]
        outs = refs[2 + n_ext:2 + n_ext + n_out]
        sums = refs[2 + n_ext + n_out:2 + n_ext + n_out + n_colsums]
        row_tile, k = pl.program_id(0), pl.program_id(2)
        slabs = [slice(s * sub, (s + 1) * sub) for s in range(tm // sub)]

        def product(rows):
            return _bdot(a_ref[:, rows] if ta else a_ref[rows, :], b_ref[...], 0 if ta else 1, 1 if tb else 0)

        def finish(results):
            col_rows = []
            for rows, r in zip(slabs, results):
                res = (r,) if epilogue is None else epilogue(r, *[e[...] if e.shape[0] == 1 else e[rows, :] for e in ext])
                for o, v in zip(outs, res):
                    o[rows, :] = v.astype(o.dtype)
                col_rows.append(res[n_out:])
            for n, o in enumerate(sums):
                v = functools.reduce(lambda x, y: x + y, [c[n] for c in col_rows])

                @pl.when(row_tile == 0)
                def _(o=o, v=v):
                    o[...] = v

                @pl.when(row_tile > 0)
                def _(o=o, v=v):
                    o[...] += v

        if nk == 1:
            finish([product(rows) for rows in slabs])
            return
        acc = refs[-1]

        @pl.when(k == 0)
        def _():
            acc[...] = jnp.zeros_like(acc)

        for rows in slabs:
            acc[rows, :] += product(rows)

        @pl.when(k == nk - 1)
        def _():
            finish([acc[rows, :] for rows in slabs])

    a_spec = pl.BlockSpec((tk, tm), lambda i, j, k: (k, i)) if ta else pl.BlockSpec((tm, tk), lambda i, j, k: (i, k))
    b_spec = pl.BlockSpec((tn, tk), lambda i, j, k: (j, k)) if tb else pl.BlockSpec((tk, tn), lambda i, j, k: (k, j))
    ext_specs = []
    for e in extras:
        if e.shape[0] == 1 and M != 1:
            ext_specs.append(pl.BlockSpec((1, tn), lambda i, j, k: (0, j)))
        else:
            ext_specs.append(pl.BlockSpec((tm, tn), lambda i, j, k: (i, j)))
    out = pl.pallas_call(
        body, name=name,
        grid=(M // tm, N // tn, nk),
        in_specs=[a_spec, b_spec] + ext_specs,
        out_specs=[pl.BlockSpec((tm, tn), lambda i, j, k: (i, j)) for _ in range(n_out)]
        + [pl.BlockSpec((1, tn), lambda i, j, k: (0, 0)) for _ in range(n_colsums)],
        out_shape=[jax.ShapeDtypeStruct((M, N), dt) for dt in out_dtypes]
        + [jax.ShapeDtypeStruct((1, N), F32) for _ in range(n_colsums)],
        scratch_shapes=[pltpu.VMEM((tm, tn), F32)] if nk > 1 else [],
        compiler_params=_cparams(("arbitrary" if n_colsums else "parallel", "parallel", "arbitrary")),
    )(a, b, *extras)
    return out[0] if len(out) == 1 else tuple(out)


def _perm_matrices(tr, d):
    import numpy as np
    old = np.arange(tr)
    p = np.zeros((tr, tr), np.float32)
    p[(old % d) * (tr // d) + old // d, old] = 1.0
    return jnp.asarray(p, BF16), jnp.asarray(p.T, BF16)


def _permute(p, x):
    if x.dtype == BF16:
        return _dot(p, x)
    hi = x.astype(BF16)
    rest = x - hi.astype(F32)
    mid = rest.astype(BF16)
    lo = (rest - mid.astype(F32)).astype(BF16)
    return _dot(p, hi) + _dot(p, mid) + _dot(p, lo)


def _rows(name, fn, ins, outs, *, tr, accs=()):
    ins = [(e[0], e[1]) + (e[2] if len(e) > 2 else (0, e[0].shape[-1])) for e in ins]
    outs = [tuple(o) + (0,) * (3 - len(o)) for o in outs]
    n_rows = next(e[0].shape[0] if e[1] == "row" else e[0].shape[0] * e[0].shape[1] for e in ins if e[1] in ("row", "res"))
    assert n_rows % tr == 0 and tr % SUBLANE == 0
    steps = n_rows // tr
    t8 = tr // SUBLANE
    n8 = n_rows // SUBLANE
    dils = sorted({e[0].shape[0] for e in ins if e[1] == "res" and e[0].shape[0] > 1} | {o[2] for o in outs if o[2] > 1})
    perms = [m for d in dils for m in _perm_matrices(tr, d)]
    ins = ins + [(m, "full", 0, tr) for m in perms]
    n_in, n_out, n_acc = len(ins), len(outs), len(accs)

    def body(*refs):
        i = pl.program_id(0)
        to_res = {d: refs[n_in - len(perms) + 2 * j][...] for j, d in enumerate(dils)}
        to_tok = {d: refs[n_in - len(perms) + 2 * j + 1][...] for j, d in enumerate(dils)}
        tiles = []
        for r, e in zip(refs[:n_in - len(perms)], ins):
            d = e[0].shape[0] if e[1] == "res" else 0
            if d == 0:
                tiles.append(r[...])
            elif d == 1:
                tiles.append(r[0])
            else:
                tiles.append(_permute(to_tok[d], jnp.concatenate([r[j] for j in range(d)], axis=0)))
        vals = fn(i, steps, *tiles)
        if not isinstance(vals, (tuple, list)):
            vals = (vals,)
        assert len(vals) == n_out + n_acc
        for o, v, (_, dt, d) in zip(refs[n_in:n_in + n_out], vals[:n_out], outs):
            if d == 0:
                o[...] = v.astype(o.dtype)
            elif d == 1:
                o[0] = v.astype(o.dtype)
            else:
                y = _permute(to_res[d], v.astype(dt))
                for j in range(d):
                    o[j] = y[j * (tr // d):(j + 1) * (tr // d)].astype(o.dtype)
        if n_acc:
            acc_refs = refs[n_in + n_out:]

            @pl.when(i == 0)
            def _():
                for r in acc_refs:
                    r[...] = jnp.zeros_like(r)

            for r, v in zip(acc_refs, vals[n_out:]):
                r[...] += v.astype(r.dtype)

    in_specs = []
    for a, kind, cb, c in ins:
        if kind == "row":
            in_specs.append(pl.BlockSpec((tr, c), lambda i, cb=cb: (i, cb)))
        elif kind == "full":
            in_specs.append(pl.BlockSpec(a.shape, lambda i, z=(0,) * a.ndim: z))
        elif kind == "prev8":
            in_specs.append(pl.BlockSpec((SUBLANE, c), lambda i, cb=cb: (jnp.maximum(i * t8 - 1, 0), cb)))
        elif kind == "next8":
            in_specs.append(pl.BlockSpec((SUBLANE, c), lambda i, cb=cb: (jnp.minimum((i + 1) * t8, n8 - 1), cb)))
        elif kind == "res":
            d = a.shape[0]
            in_specs.append(pl.BlockSpec((d, tr // d, a.shape[2]), lambda i: (0, i, 0)))
        else:
            raise ValueError(kind)
    out_specs = [pl.BlockSpec((tr, c), lambda i: (i, 0)) if d == 0 else pl.BlockSpec((d, tr // d, c), lambda i: (0, i, 0))
                 for c, _, d in outs]
    out_specs += [pl.BlockSpec(s, lambda i, z=(0,) * len(s): z) for s, _ in accs]
    out_shape = [jax.ShapeDtypeStruct((n_rows, c) if d == 0 else (d, n_rows // d, c), dt) for c, dt, d in outs]
    out_shape += [jax.ShapeDtypeStruct(s, dt) for s, dt in accs]
    res = pl.pallas_call(
        body, name=name, grid=(steps,), in_specs=in_specs, out_specs=out_specs, out_shape=out_shape,
        compiler_params=_cparams(("arbitrary",) if n_acc else ("parallel",)),
    )(*[e[0] for e in ins])
    return res[0] if len(res) == 1 else tuple(res)


def _colsum(x):
    return jnp.sum(x, axis=0, keepdims=True)


def _sum_all(x):
    return jnp.sum(jnp.sum(x, axis=1, keepdims=True), axis=0, keepdims=True)


def _rmsnorm_fwd(name, x, gain):
    def fn(i, n, xt, g):
        r = lax.rsqrt(jnp.mean(xt * xt, axis=-1, keepdims=True) + EPS)
        return (xt * r * g,)
    return _rows(name, fn, [(x, "row"), (gain, "full")], [(x.shape[1], BF16)], tr=512)


FUSED_ROWS = 1024


def _res_norm(acc, res, g):
    x = res + acc
    return x, x * lax.rsqrt(jnp.mean(x * x, axis=-1, keepdims=True) + EPS) * g


def _norm_bwd(dh, x, g, dres):
    r = lax.rsqrt(jnp.mean(x * x, axis=-1, keepdims=True) + EPS)
    xh = x * r
    dxn = dh * g
    dx = dres + r * (dxn - xh * jnp.mean(dxn * xh, axis=-1, keepdims=True))
    return dx, _colsum(dh * xh)


def _norm_bwd_2(dh, x, g, dres):
    dx, dg = _norm_bwd(dh, x, g, dres)
    return dx, dx, dg


def _head_consts():
    import numpy as np
    e = np.arange(A_WIDTH) % A_HEAD_DIM
    inv = (np.float32(ROPE_THETA) ** (-np.arange(0, ROPE_DIM, 2, dtype=np.float32) / np.float32(ROPE_DIM))).astype(np.float32)
    c = np.zeros((8, A_WIDTH), np.float32)
    c[0] = np.where(e < ROPE_DIM, inv[e % ROPE_HALF], 0.0)
    c[1] = np.where(e < ROPE_HALF, -1.0, np.where(e < ROPE_DIM, 1.0, 0.0))
    c[2] = (e < ROPE_HALF).astype(np.float32)
    c[3] = (e < ROPE_DIM).astype(np.float32)
    return jnp.asarray(c)


def _block_diag(scale):
    import numpy as np
    h = np.arange(A_WIDTH) // A_HEAD_DIM
    return jnp.asarray((h[:, None] == h[None, :]).astype(np.float32) * scale, dtype=BF16)


def _seg_sum(x, bd):
    return _dot(x.astype(BF16), bd)


def _rope_tables(positions, consts):
    def fn(i, n, pos, c):
        ang = pos.astype(F32) * c[0:1, :LANE]
        return jnp.cos(ang), jnp.sin(ang) * c[1:2, :LANE]
    return _rows("rope_tables", fn, [(positions, "row"), (consts, "full")], [(LANE, F32), (LANE, F32)], tr=512)


def _rope_wide(t):
    return jnp.concatenate([t] * (A_WIDTH // LANE), axis=1)


def _rope_apply(y, ct, st, low):
    rolled = jnp.where(low, pltpu.roll(y, A_WIDTH - ROPE_HALF, 1), pltpu.roll(y, ROPE_HALF, 1))
    return y * ct + rolled * st


def _rope_apply_bwd(dout, ct, st, low, in16):
    t = dout * st
    back = jnp.where(low, pltpu.roll(t, A_WIDTH - ROPE_HALF, 1), jnp.where(in16, pltpu.roll(t, ROPE_HALF, 1), 0.0))
    return dout * ct + back


def _attn_prep(qkv, gains, ct, st, consts, bd):
    def fn(i, n, t, g, c_t, s_t, c, b):
        low = c[2:3, :] > 0.5
        c_t, s_t = _rope_wide(c_t), _rope_wide(s_t)
        groups = []
        for grp in range(3):
            cols = []
            for which in range(3):
                off = (grp * 3 + which) * A_WIDTH
                x = t[:, off:off + A_WIDTH].astype(F32)
                if which == 2:
                    cols.append(x.astype(BF16))
                    continue
                r = lax.rsqrt(_seg_sum(x * x, b) + EPS)
                y = x * r * g[grp * 2 + which:grp * 2 + which + 1, :]
                cols.append(_rope_apply(y, c_t, s_t, low).astype(BF16))
            groups.append(jnp.concatenate(cols, axis=1))
        return tuple(groups)
    return _rows("attn_prep", fn, [(qkv, "row"), (gains, "full"), (ct, "row"), (st, "row"), (consts, "full"), (bd, "full")],
                 [(3 * A_WIDTH, BF16, d) for _, d in SWA_GROUPS], tr=256)


def _band_mask(n):
    row = lax.broadcasted_iota(jnp.int32, (BAND, 2 * BAND), 0)
    col = lax.broadcasted_iota(jnp.int32, (BAND, 2 * BAND), 1)
    dist = row + BAND - col
    return (dist >= 0) & (dist <= BAND) & ((col >= BAND) | (n > 0))


def _attn_fwd(qkvn, grp):
    d, L, _ = qkvn.shape
    nblk = L // BAND
    assert L % BAND == 0 and d == SWA_GROUPS[grp][1]

    def body(q_ref, kc_ref, kp_ref, vc_ref, vp_ref, o_ref, lse_ref):
        n = pl.program_id(1)
        valid = _band_mask(n)
        first = lax.broadcasted_iota(jnp.int32, (BAND, LANE), 1) < A_HEAD_DIM
        pairs = [slice(pr * LANE, (pr + 1) * LANE) for pr in range(A_WIDTH // LANE)]
        halves = (first, jnp.logical_not(first))
        qps = [q_ref[:, sl] for sl in pairs]
        kcats = [jnp.concatenate([kp_ref[:, sl], kc_ref[:, sl]], axis=0) for sl in pairs]
        vcats = [jnp.concatenate([vp_ref[:, sl], vc_ref[:, sl]], axis=0) for sl in pairs]
        heads = [(pr, m) for pr in range(len(pairs)) for m in halves]
        ss = [_dot(jnp.where(m, qps[pr], jnp.zeros_like(qps[pr])), kcats[pr], 1, 1) for pr, m in heads]
        ps, lses = [], []
        for s in ss:
            s = jnp.where(valid, s * (A_HEAD_DIM ** -0.5), -1e30)
            mx = jnp.max(s, axis=-1, keepdims=True)
            e = jnp.exp(s - mx)
            l = jnp.sum(e, axis=-1, keepdims=True)
            ps.append((e / l).astype(BF16))
            lses.append(mx + jnp.log(l))
        os_ = [_dot(p, vcats[pr]) for p, (pr, _) in zip(ps, heads)]
        o_ref[...] = jnp.concatenate([jnp.where(first, os_[2 * pr], os_[2 * pr + 1]) for pr in range(len(pairs))], axis=1)
        lse_ref[...] = jnp.concatenate([jnp.where(first, lses[2 * pr], lses[2 * pr + 1]) for pr in range(len(pairs))], axis=1)

    blk = (None, BAND, A_WIDTH)
    return pl.pallas_call(
        body, name=f"attn_fwd_g{grp}", grid=(d, nblk),
        in_specs=[pl.BlockSpec(blk, lambda r, n: (r, n, 0)),
                  pl.BlockSpec(blk, lambda r, n: (r, n, 1)),
                  pl.BlockSpec(blk, lambda r, n: (r, jnp.maximum(n - 1, 0), 1)),
                  pl.BlockSpec(blk, lambda r, n: (r, n, 2)),
                  pl.BlockSpec(blk, lambda r, n: (r, jnp.maximum(n - 1, 0), 2))],
        out_specs=[pl.BlockSpec(blk, lambda r, n: (r, n, 0)), pl.BlockSpec(blk, lambda r, n: (r, n, 0))],
        out_shape=[jax.ShapeDtypeStruct((d, L, A_WIDTH), F32)] * 2,
        compiler_params=_cparams(("parallel", "parallel")),
    )(qkvn, qkvn, qkvn, qkvn, qkvn)


def _merge_weights(l0, l1, l2):
    mx = jnp.maximum(jnp.maximum(l0, l1), l2)
    e0, e1, e2 = jnp.exp(l0 - mx), jnp.exp(l1 - mx), jnp.exp(l2 - mx)
    inv = 1.0 / (e0 + e1 + e2)
    return e0 * inv, e1 * inv, e2 * inv


def _attn_merge(os_, lses):
    def fn(i, n, o0, o1, o2, l0, l1, l2):
        w0, w1, w2 = _merge_weights(l0, l1, l2)
        return (w0 * o0 + w1 * o1 + w2 * o2,)
    ins = [(a, "res") for a in (*os_, *lses)]
    return _rows("attn_merge", fn, ins, [(A_WIDTH, BF16)], tr=256)


def _attn_merge_bwd(do, os_, lses, bd1):
    def fn(i, n, dot_, o0, o1, o2, l0, l1, l2, b):
        w0, w1, w2 = _merge_weights(l0, l1, l2)
        o = w0 * o0 + w1 * o1 + w2 * o2
        dsum = _seg_sum(dot_ * o, b)
        return (w0 * dot_, w1 * dot_, w2 * dot_, -w0 * dsum, -w1 * dsum, -w2 * dsum)
    ins = [(do, "row")] + [(a, "res") for a in (*os_, *lses)] + [(bd1, "full")]
    res = _rows("attn_merge_bwd", fn, ins, [(A_WIDTH, dt, d) for dt in (BF16, F32) for _, d in SWA_GROUPS], tr=256)
    return res[:3], res[3:]


def _lane_pick(x, lane_idx, lane):
    return jnp.sum(jnp.where(lane_idx == lane, x, 0.0), axis=-1, keepdims=True)


def _attn_bwd(qkvn, grp, do_g, lse, c_g):
    d, L, _ = qkvn.shape
    nblk = L // BAND

    def body(q_ref, kc_ref, kp_ref, vc_ref, vp_ref, do_ref, lse_ref, c_ref, dq_ref, dk_ref, dv_ref, ck, cv_):
        n = pl.program_id(1)

        @pl.when(n == 0)
        def _():
            ck[...] = jnp.zeros_like(ck)
            cv_[...] = jnp.zeros_like(cv_)

        @pl.when(n < nblk)
        def _():
            valid = _band_mask(n)
            lane = lax.broadcasted_iota(jnp.int32, (BAND, LANE), 1)
            first = lane < A_HEAD_DIM
            lane2 = lax.broadcasted_iota(jnp.int32, (2 * BAND, LANE), 1) < A_HEAD_DIM
            pairs = [slice(pr * LANE, (pr + 1) * LANE) for pr in range(A_WIDTH // LANE)]
            halves = (first, jnp.logical_not(first))
            qps = [q_ref[:, sl] for sl in pairs]
            dops = [do_ref[:, sl] for sl in pairs]
            kcats = [jnp.concatenate([kp_ref[:, sl], kc_ref[:, sl]], axis=0) for sl in pairs]
            vcats = [jnp.concatenate([vp_ref[:, sl], vc_ref[:, sl]], axis=0) for sl in pairs]
            heads = [(pr, hh) for pr in range(len(pairs)) for hh in range(2)]
            zero = jnp.zeros_like(qps[0])
            ss = [_dot(jnp.where(halves[hh], qps[pr], zero), kcats[pr], 1, 1) for pr, hh in heads]
            dps = [_dot(jnp.where(halves[hh], dops[pr], zero), vcats[pr], 1, 1) for pr, hh in heads]
            dss, pbs = [], []
            for (pr, hh), s, dp in zip(heads, ss, dps):
                lse_h = _lane_pick(lse_ref[:, pairs[pr]], lane, hh * A_HEAD_DIM)
                c_h = _lane_pick(c_ref[:, pairs[pr]], lane, hh * A_HEAD_DIM)
                p = jnp.where(valid, jnp.exp(s * (A_HEAD_DIM ** -0.5) - lse_h), 0.0)
                dss.append((p * (dp + c_h) * (A_HEAD_DIM ** -0.5)).astype(BF16))
                pbs.append(p.astype(BF16))
            dqs = [_dot(ds, kcats[pr]) for ds, (pr, _) in zip(dss, heads)]
            dks = [_dot(ds, qps[pr], 0, 0) for ds, (pr, _) in zip(dss, heads)]
            dvs = [_dot(pb, dops[pr], 0, 0) for pb, (pr, _) in zip(pbs, heads)]
            for pr, sl in enumerate(pairs):
                dq_ref[:, sl] = jnp.where(first, dqs[2 * pr], dqs[2 * pr + 1])
                dkc = jnp.where(lane2, dks[2 * pr], dks[2 * pr + 1])
                dvc = jnp.where(lane2, dvs[2 * pr], dvs[2 * pr + 1])
                dk_ref[:, sl] = ck[:, sl] + dkc[:BAND]
                dv_ref[:, sl] = cv_[:, sl] + dvc[:BAND]
                ck[:, sl] = dkc[BAND:]
                cv_[:, sl] = dvc[BAND:]

        @pl.when(n == nblk)
        def _():
            dk_ref[...] = ck[...]
            dv_ref[...] = cv_[...]

    blk = (None, BAND, A_WIDTH)
    last = nblk - 1
    qn = lambda n: jnp.minimum(n, last)
    pn = lambda n: jnp.clip(n - 1, 0, last)
    return tuple(pl.pallas_call(
        body, name=f"attn_bwd_g{grp}", grid=(d, nblk + 1),
        in_specs=[pl.BlockSpec(blk, lambda r, n: (r, qn(n), 0)),
                  pl.BlockSpec(blk, lambda r, n: (r, qn(n), 1)),
                  pl.BlockSpec(blk, lambda r, n: (r, pn(n), 1)),
                  pl.BlockSpec(blk, lambda r, n: (r, qn(n), 2)),
                  pl.BlockSpec(blk, lambda r, n: (r, pn(n), 2)),
                  pl.BlockSpec(blk, lambda r, n: (r, qn(n), 0)),
                  pl.BlockSpec(blk, lambda r, n: (r, qn(n), 0)),
                  pl.BlockSpec(blk, lambda r, n: (r, qn(n), 0))],
        out_specs=[pl.BlockSpec(blk, lambda r, n: (r, qn(n), 0)),
                   pl.BlockSpec(blk, lambda r, n: (r, pn(n), 0)),
                   pl.BlockSpec(blk, lambda r, n: (r, pn(n), 0))],
        out_shape=[jax.ShapeDtypeStruct((d, L, A_WIDTH), F32)] * 3,
        scratch_shapes=[pltpu.VMEM((BAND, A_WIDTH), F32), pltpu.VMEM((BAND, A_WIDTH), F32)],
        compiler_params=_cparams(("parallel", "arbitrary")),
    )(qkvn, qkvn, qkvn, qkvn, qkvn, do_g, lse, c_g))


def _attn_prep_bwd(qkv, grads, gains, ct, st, consts, bd):
    def fn(i, n, t, g, c_t, s_t, c, b, *gr):
        low = c[2:3, :] > 0.5
        in16 = c[3:4, :] > 0.5
        c_t, s_t = _rope_wide(c_t), _rope_wide(s_t)
        cols, dgs = [], []
        for grp in range(3):
            for which in range(3):
                dout = gr[grp * 3 + which]
                if which == 2:
                    cols.append(dout.astype(BF16))
                    continue
                off = (grp * 3 + which) * A_WIDTH
                x = t[:, off:off + A_WIDTH].astype(F32)
                gain = g[grp * 2 + which:grp * 2 + which + 1, :]
                r = lax.rsqrt(_seg_sum(x * x, b) + EPS)
                xh = x * r
                dy = _rope_apply_bwd(dout, c_t, s_t, low, in16)
                dyn = dy * gain
                dx = r * (dyn - xh * _seg_sum(dyn * xh, b))
                cols.append(dx.astype(BF16))
                dgs.append(_colsum(dy * xh))
        return (jnp.concatenate(cols, axis=1), *dgs)
    ins = [(qkv, "row"), (gains, "full"), (ct, "row"), (st, "row"), (consts, "full"), (bd, "full")] + [(a, "res") for a in grads]
    res = _rows("attn_prep_bwd", fn, ins, [(A_QKV, BF16)], tr=128, accs=[((1, A_WIDTH), F32)] * 6)
    return res[0], res[1:]


DN_QKV = 3 * DN_WIDTH
DN_QKVZ = DN_QKV + DN_WIDTH


def _sigmoid(x):
    return jax.nn.sigmoid(x)


def _softplus(x):
    return jnp.maximum(x, 0.0) + jnp.log(1.0 + jnp.exp(-jnp.abs(x)))


def _conv_taps(xs, w, tr):
    acc = None
    for j in range(CONV_W):
        sh = CONV_W - 1 - j
        term = (pltpu.roll(xs, sh, 0) if sh else xs)[SUBLANE:] * w[j:j + 1, :]
        acc = term if acc is None else acc + term
    return acc


def _dn_prep(qkvz, ab, convw, alog_row, dt_row):
    tr = 256

    def fn(i, n, x, xp, abt, w, al, dt):
        xp = jnp.where(i > 0, xp, 0.0)
        u = _conv_taps(jnp.concatenate([xp, x], axis=0), w, tr)
        y = u * _sigmoid(u)
        qs, ks = [], []
        for h in range(DN_HEADS):
            for dst, base, sc in ((qs, 0, DN_DIM ** -0.5), (ks, DN_WIDTH, 1.0)):
                seg = y[:, base + h * DN_DIM:base + (h + 1) * DN_DIM]
                dst.append(seg * (lax.rsqrt(jnp.sum(seg * seg, axis=-1, keepdims=True) + EPS) * sc))
        lane = lax.broadcasted_iota(jnp.int32, abt.shape, 1)
        g = -jnp.exp(al) * _softplus(abt + dt)
        gb = jnp.where(lane < DN_HEADS, g, jnp.where(lane < 2 * DN_HEADS, _sigmoid(abt), 0.0))
        return u, jnp.concatenate(qs, axis=1), jnp.concatenate(ks, axis=1), y[:, 2 * DN_WIDTH:], gb

    ins = [(qkvz, "row", (0, DN_QKV)), (qkvz, "prev8", (0, DN_QKV)), (ab, "row"), (convw, "full"),
           (alog_row, "full"), (dt_row, "full")]
    return _rows("dn_prep", fn, ins, [(DN_QKV, BF16), (DN_WIDTH, F32), (DN_WIDTH, F32), (DN_WIDTH, F32), (LANE, F32)], tr=tr)


def _tri_masks():
    row = lax.broadcasted_iota(jnp.int32, (CHUNK, CHUNK), 0)
    col = lax.broadcasted_iota(jnp.int32, (CHUNK, CHUNK), 1)
    return row >= col, row > col, row == col


def _heads(fn, *lists):
    return [fn(*xs) for xs in zip(*lists)]


def _split(x):
    hi = x.astype(BF16)
    return hi, (x - hi.astype(F32)).astype(BF16)


def _dot3(a, b, ca=1, cb=0):
    (ah, al), (bh, bl) = a, b
    return _dot(ah, bh, ca, cb) + (_dot(ah, bl, ca, cb) + _dot(al, bh, ca, cb))


SPLIT_STEPS = 3


def _unit_lower_inverse(a_list, eye):
    ts = [eye - a for a in a_list]
    parts = [_split(a) for a in a_list]
    for step in range(5):
        if step < SPLIT_STEPS:
            parts = [_split(_dot3(p, p)) for p in parts]
            ts = [t + _dot3(_split(t), p) for t, p in zip(ts, parts)]
        else:
            parts = [(_dot(p[0], p[0]).astype(BF16), None) for p in parts]
            ts = [t + _dot(t.astype(BF16), p[0]) for t, p in zip(ts, parts)]
    return ts


def _dn_terms(qs, ks, vs, gb, solved=None):
    lower, strict, diag = _tri_masks()
    lane = lax.broadcasted_iota(jnp.int32, (CHUNK, LANE), 1)
    is_last = lax.broadcasted_iota(jnp.int32, (CHUNK, 1), 0) == CHUNK - 1
    hs = range(DN_HEADS)
    gc = _dot(lower.astype(F32), gb, precision=HIGHEST)
    gct = jnp.transpose(gc)
    bcol = [_lane_pick(gb, lane, DN_HEADS + h) for h in hs]
    gcol = [_lane_pick(gc, lane, h) for h in hs]
    glast = [jnp.sum(jnp.where(is_last, g, 0.0), axis=0, keepdims=True) for g in gcol]
    decay = [jnp.exp(jnp.where(lower, gcol[h] - gct[h:h + 1, :], -1e30)) for h in hs]
    kb = _heads(lambda k, b: k * b, ks, bcol)
    kk = _heads(lambda x, k: _bdot(x, k, 1, 1), kb, ks)
    qk = _heads(lambda q, k: _bdot(q, k, 1, 1), qs, ks)
    a = _heads(lambda x, d: jnp.where(strict, x * d, 0.0), kk, decay)
    eg = [jnp.exp(g) for g in gcol]
    egl = _heads(lambda gl, g: jnp.exp(gl - g), glast, gcol)
    rhs_w = _heads(lambda x, e: x * e, kb, eg)
    if solved is None:
        t_full = _unit_lower_inverse(a, diag.astype(F32))
        t = [_split(x) for x in t_full]
        u = _heads(lambda tt, v, b: _dot3(tt, _split(v * b)), t, vs, bcol)
        w = _heads(lambda tt, r: _dot3(tt, _split(r)), t, rhs_w)
    else:
        t_full, u, w = solved
        t = [_split(x) for x in t_full]
    return dict(bcol=bcol, decay=decay, kb=kb, a=a, t=t, t_full=t_full, eg=eg, egl=egl, rhs_w=rhs_w, u=u, w=w,
                attn=_heads(lambda x, d: x * d, qk, decay), q_dec=_heads(lambda q, e: q * e, qs, eg),
                k_dec=_heads(lambda k, e: k * e, ks, egl), c_dec=[jnp.exp(g) for g in glast],
                lower=lower, strict=strict, lane=lane, is_last=is_last)


def _head_slices(ref):
    return [ref[:, h * DN_DIM:(h + 1) * DN_DIM] for h in range(DN_HEADS)]


def _dn_chunk_fwd(q, k, v, gb):
    S = q.shape[0]
    N = S // CHUNK

    def body(q_ref, k_ref, v_ref, gb_ref, o_ref, st_ref, t_ref, u_ref, w_ref, state):
        @pl.when(pl.program_id(0) == 0)
        def _():
            state[...] = jnp.zeros_like(state)

        f = _dn_terms(_head_slices(q_ref), _head_slices(k_ref), _head_slices(v_ref), gb_ref[...])
        s = [state[h] for h in range(DN_HEADS)]
        for h in range(DN_HEADS):
            st_ref[0, h] = s[h]
            t_ref[0, h] = f["t_full"][h]
            u_ref[:, h * DN_DIM:(h + 1) * DN_DIM] = f["u"][h]
            w_ref[:, h * DN_DIM:(h + 1) * DN_DIM] = f["w"][h]
        sb = [x.astype(BF16) for x in s]
        v_new = _heads(lambda u, w, x: u - _bdot(w, x), f["u"], f["w"], sb)
        o = _heads(lambda qd, x, at, vn: _bdot(qd, x) + _bdot(at, vn), f["q_dec"], sb, f["attn"], v_new)
        new_s = _heads(lambda x, c, kd, vn: x * c + _bdot(kd, vn, 0, 0), s, f["c_dec"], f["k_dec"], v_new)
        for h in range(DN_HEADS):
            o_ref[:, h * DN_DIM:(h + 1) * DN_DIM] = o[h]
            state[h] = new_s[h]

    blk = pl.BlockSpec((CHUNK, DN_WIDTH), lambda n: (n, 0))
    st_blk = pl.BlockSpec((1, DN_HEADS, DN_DIM, DN_DIM), lambda n: (n, 0, 0, 0))
    t_blk = pl.BlockSpec((1, DN_HEADS, CHUNK, CHUNK), lambda n: (n, 0, 0, 0))
    wide = jax.ShapeDtypeStruct((S, DN_WIDTH), F32)
    o, states, t, u, w = pl.pallas_call(
        body, name="dn_chunk_fwd", grid=(N,),
        in_specs=[blk, blk, blk, pl.BlockSpec((CHUNK, LANE), lambda n: (n, 0))],
        out_specs=[blk, st_blk, t_blk, blk, blk],
        out_shape=[wide, jax.ShapeDtypeStruct((N, DN_HEADS, DN_DIM, DN_DIM), F32),
                   jax.ShapeDtypeStruct((N, DN_HEADS, CHUNK, CHUNK), F32), wide, wide],
        scratch_shapes=[pltpu.VMEM((DN_HEADS, DN_DIM, DN_DIM), F32)],
        compiler_params=_cparams(("arbitrary",)),
    )(q, k, v, gb)
    return o, (states, t, u, w)


def _dn_chunk_bwd(q, k, v, gb, saved, do):
    S = q.shape[0]
    N = S // CHUNK
    states, t_saved, u_saved, w_saved = saved

    def body(q_ref, k_ref, v_ref, gb_ref, st_ref, t_ref, u_ref, w_ref, do_ref, dq_ref, dk_ref, dv_ref, dgb_ref, dstate):
        @pl.when(pl.program_id(0) == 0)
        def _():
            dstate[...] = jnp.zeros_like(dstate)

        hs = range(DN_HEADS)
        qs, ks, vs, dos = (_head_slices(r) for r in (q_ref, k_ref, v_ref, do_ref))
        f = _dn_terms(qs, ks, vs, gb_ref[...], ([t_ref[0, h] for h in hs], _head_slices(u_ref), _head_slices(w_ref)))
        lane, is_last = f["lane"], f["is_last"]
        rowsum = lambda x: jnp.sum(x, axis=-1, keepdims=True)
        s = [st_ref[0, h] for h in hs]
        dsn = [dstate[h] for h in hs]
        sb = [x.astype(BF16) for x in s]
        dsb = [x.astype(BF16) for x in dsn]
        dob = [x.astype(BF16) for x in dos]
        v_new = _heads(lambda u, w, x: u - _bdot(w, x), f["u"], f["w"], sb)
        dv_new = _heads(lambda at, d, kd, x: _bdot(at, d, 0, 0) + _bdot(kd, x), f["attn"], dob, f["k_dec"], dsb)
        dattn = _heads(lambda d, vn: _bdot(d, vn, 1, 1), dob, v_new)
        dq_dec = _heads(lambda d, x: _bdot(d, x, 1, 1), dob, sb)
        dk_dec = _heads(lambda vn, x: _bdot(vn, x, 1, 1), v_new, dsb)
        dw = _heads(lambda dv_, x: -_bdot(dv_, x, 1, 1), dv_new, sb)
        new_ds = _heads(lambda x, c, qd, d, w, dv_: x * c + _bdot(qd, d, 0, 0) - _bdot(w, dv_, 0, 0),
                        dsn, f["c_dec"], f["q_dec"], dob, f["w"], dv_new)
        for h in hs:
            dstate[h] = new_ds[h]
        drhs_u = _heads(lambda tt, x: _dot3(tt, _split(x), 0, 0), f["t"], dv_new)
        drhs_w = _heads(lambda tt, x: _dot3(tt, _split(x), 0, 0), f["t"], dw)
        da = _heads(lambda du_, u, dw_, w: jnp.where(f["strict"], -(_bdot(du_, u, 1, 1) + _bdot(dw_, w, 1, 1)), 0.0),
                    drhs_u, f["u"], drhs_w, f["w"])
        dkk = _heads(lambda x, d: x * d, da, f["decay"])
        dqk = _heads(lambda x, d: x * d, dattn, f["decay"])
        dkb = _heads(lambda x, k_, dw_, e: _bdot(x, k_) + dw_ * e, dkk, ks, drhs_w, f["eg"])
        dq = _heads(lambda x, k_, dqd, e: _bdot(x, k_) + dqd * e, dqk, ks, dq_dec, f["eg"])
        dk = _heads(lambda x, kb_, y, q_, dkd, el, dkb_, b: _bdot(x, kb_, 0, 0) + _bdot(y, q_, 0, 0) + dkd * el + dkb_ * b,
                    dkk, f["kb"], dqk, qs, dk_dec, f["egl"], dkb, f["bcol"])
        m = _heads(lambda x, a_, y, at: x * a_ + y * at, da, f["a"], dattn, f["attn"])
        ones = jnp.ones((CHUNK, LANE), BF16)
        col_m = [(_dot(mh, ones, 0, 0) + _dot(ml, ones, 0, 0))[:, 0:1] for mh, ml in map(_split, m)]
        dgc_all = jnp.zeros((CHUNK, LANE), F32)
        dbeta_all = jnp.zeros((CHUNK, LANE), F32)
        for h in hs:
            dq_ref[:, h * DN_DIM:(h + 1) * DN_DIM] = dq[h]
            dk_ref[:, h * DN_DIM:(h + 1) * DN_DIM] = dk[h]
            dv_ref[:, h * DN_DIM:(h + 1) * DN_DIM] = drhs_u[h] * f["bcol"][h]
            kdec_term = rowsum(dk_dec[h] * f["k_dec"][h])
            dc_dec = _sum_all(dsn[h] * s[h])
            dgc = (rowsum(m[h]) - col_m[h] + rowsum(dq_dec[h] * f["q_dec"][h]) - kdec_term
                   + rowsum(drhs_w[h] * f["rhs_w"][h]))
            last_extra = jnp.sum(kdec_term, axis=0, keepdims=True) + dc_dec * f["c_dec"][h]
            dgc = dgc + jnp.where(is_last, last_extra, 0.0)
            dbeta = rowsum(drhs_u[h] * vs[h]) + rowsum(dkb[h] * ks[h])
            dgc_all = jnp.where(lane == h, dgc, dgc_all)
            dbeta_all = jnp.where(lane == DN_HEADS + h, dbeta, dbeta_all)
        dg_all = _dot(f["lower"].astype(F32), dgc_all, 0, 0, precision=HIGHEST)
        dgb_ref[...] = jnp.where(lane < DN_HEADS, dg_all, dbeta_all)

    rev = lambda n: (N - 1 - n, 0)
    blk = pl.BlockSpec((CHUNK, DN_WIDTH), rev)
    gblk = pl.BlockSpec((CHUNK, LANE), rev)
    st_blk = pl.BlockSpec((1, DN_HEADS, DN_DIM, DN_DIM), lambda n: (N - 1 - n, 0, 0, 0))
    t_blk = pl.BlockSpec((1, DN_HEADS, CHUNK, CHUNK), lambda n: (N - 1 - n, 0, 0, 0))
    return pl.pallas_call(
        body, name="dn_chunk_bwd", grid=(N,),
        in_specs=[blk, blk, blk, gblk, st_blk, t_blk, blk, blk, blk],
        out_specs=[blk, blk, blk, gblk],
        out_shape=[jax.ShapeDtypeStruct((S, DN_WIDTH), F32)] * 3 + [jax.ShapeDtypeStruct((S, LANE), F32)],
        scratch_shapes=[pltpu.VMEM((DN_HEADS, DN_DIM, DN_DIM), F32)],
        compiler_params=_cparams(("arbitrary",)),
    )(q, k, v, gb, states, t_saved, u_saved, w_saved, do)


def _dn_post(o, qkvz, gain_row):
    def fn(i, n, ot, z, g):
        cols = []
        for h in range(DN_HEADS):
            seg = ot[:, h * DN_DIM:(h + 1) * DN_DIM]
            cols.append(seg * lax.rsqrt(jnp.mean(seg * seg, axis=-1, keepdims=True) + EPS) * g)
        return (jnp.concatenate(cols, axis=1) * (z * _sigmoid(z)),)
    return _rows("dn_post", fn, [(o, "row"), (qkvz, "row", (3, DN_WIDTH)), (gain_row, "full")], [(DN_WIDTH, BF16)], tr=512)


def _dn_post_bwd(don, o, qkvz, gain_row):
    def fn(i, n, dy, ot, z, g):
        sg = _sigmoid(z)
        sz = z * sg
        dos, ohs = [], []
        dg = jnp.zeros((1, DN_DIM), F32)
        for h in range(DN_HEADS):
            sl = slice(h * DN_DIM, (h + 1) * DN_DIM)
            seg = ot[:, sl]
            r = lax.rsqrt(jnp.mean(seg * seg, axis=-1, keepdims=True) + EPS)
            oh = seg * r
            dno = dy[:, sl] * sz[:, sl]
            dg = dg + _colsum(dno * oh)
            dn = dno * g
            dos.append(r * (dn - oh * jnp.mean(dn * oh, axis=-1, keepdims=True)))
            ohs.append(oh * g)
        dz = dy * jnp.concatenate(ohs, axis=1) * (sg * (1.0 + z * (1.0 - sg)))
        return jnp.concatenate(dos, axis=1), dz, dg
    ins = [(don, "row"), (o, "row"), (qkvz, "row", (3, DN_WIDTH)), (gain_row, "full")]
    return _rows("dn_post_bwd", fn, ins, [(DN_WIDTH, F32), (DN_WIDTH, F32)], tr=256, accs=[((1, DN_DIM), F32)])


def _dn_prep_bwd(dq, dk, dv, dgb, u, ab, alog_row, dt_row):
    def fn(i, n, dqt, dkt, dvt, dgbt, ut, abt, al, dt):
        ut = ut.astype(F32)
        sg = _sigmoid(ut)
        y = ut * sg
        dys = []
        for grad, base, sc in ((dqt, 0, DN_DIM ** -0.5), (dkt, DN_WIDTH, 1.0)):
            for h in range(DN_HEADS):
                seg = y[:, base + h * DN_DIM:base + (h + 1) * DN_DIM]
                gr = grad[:, h * DN_DIM:(h + 1) * DN_DIM]
                r = lax.rsqrt(jnp.sum(seg * seg, axis=-1, keepdims=True) + EPS)
                xh = seg * r
                dys.append((r * sc) * (gr - xh * jnp.sum(gr * xh, axis=-1, keepdims=True)))
        dy = jnp.concatenate(dys + [dvt], axis=1)
        du = dy * (sg * (1.0 + ut * (1.0 - sg)))
        lane = lax.broadcasted_iota(jnp.int32, abt.shape, 1)
        is_g = lane < DN_HEADS
        ea = jnp.exp(al)
        x = abt + dt
        slope = -ea * _sigmoid(x)
        gval = -ea * _softplus(x)
        dg = jnp.where(is_g, dgbt, 0.0)
        beta = _sigmoid(abt)
        dab = jnp.where(is_g, dg * slope, jnp.where(lane < 2 * DN_HEADS, dgbt * beta * (1.0 - beta), 0.0))
        return du, dab, _colsum(dg * gval), _colsum(dg * slope)
    ins = [(dq, "row"), (dk, "row"), (dv, "row"), (dgb, "row"), (u, "row"), (ab, "row"), (alog_row, "full"), (dt_row, "full")]
    return _rows("dn_prep_bwd", fn, ins, [(DN_QKV, F32), (LANE, BF16)], tr=256, accs=[((1, LANE), F32)] * 2)


def _dn_conv_bwd(du, dz, qkvz, convw):
    tr = 256

    def fn(i, n, dut, dun, dzt, x, xp, w):
        dun = jnp.where(i < n - 1, dun, 0.0)
        dus = jnp.concatenate([dut, dun], axis=0)
        xs = jnp.concatenate([jnp.where(i > 0, xp, 0.0), x], axis=0)
        dx = None
        dws = []
        for j in range(CONV_W):
            sh = CONV_W - 1 - j
            term = (pltpu.roll(dus, tr + SUBLANE - sh, 0) if sh else dus)[:tr] * w[j:j + 1, :]
            dx = term if dx is None else dx + term
            dws.append(_colsum(dut * (pltpu.roll(xs, sh, 0) if sh else xs)[SUBLANE:]))
        return (jnp.concatenate([dx.astype(BF16), dzt.astype(BF16)], axis=1), *dws)

    ins = [(du, "row"), (du, "next8"), (dz, "row"), (qkvz, "row", (0, DN_QKV)), (qkvz, "prev8", (0, DN_QKV)), (convw, "full")]
    res = _rows("dn_conv_bwd", fn, ins, [(DN_QKVZ, BF16)], tr=tr, accs=[((1, DN_QKV), F32)] * CONV_W)
    return res[0], res[1:]


def _add(acc, r):
    return (r + acc,)


def _mlp_ple_fwd(i, x1, hm, p_i, ple_gain, next_gain, w_up, w_down, w_ple, w_gate, target=None):
    u, a = _mm(f"mlp_up{i}", hm, w_up, epilogue=lambda acc: (acc, jnp.square(jnp.maximum(acc, 0.0))),
               out_dtypes=(BF16, BF16))
    x2, hp = _mm(f"mlp_down{i}", a, w_down, epilogue=_res_norm, extras=(x1, ple_gain), out_dtypes=(F32, BF16),
                 tm_pref=FUSED_ROWS)
    pp = _mm(f"ple_proj{i}", p_i, w_ple)

    def gate_epilogue(acc, x2t, ppt, g):
        gate = _sigmoid(acc)
        x3 = x2t + ppt * gate
        return x3, gate, x3 * lax.rsqrt(jnp.mean(x3 * x3, axis=-1, keepdims=True) + EPS) * g

    def loss_epilogue(acc, x2t, ppt, tt):
        gate = _sigmoid(acc)
        err = x2t + ppt * gate - tt
        dy = err * (1.0 / D_MODEL)
        return dy, dy * gate, dy * ppt * gate * (1.0 - gate), _colsum(err * err)

    saved = dict(x1=x1, hm=hm, u=u, a=a, x2=x2, hp=hp, pp=pp, p=p_i)
    if target is None:
        x3, saved["gate"], h_next = _mm(f"ple_gate{i}", hp, w_gate, epilogue=gate_epilogue, extras=(x2, pp, next_gain),
                                        out_dtypes=(F32, F32, BF16), tm_pref=FUSED_ROWS)
        return x3, h_next, saved
    dy, saved["dpp"], saved["dzg"], sq = _mm(f"ple_gate{i}", hp, w_gate, epilogue=loss_epilogue, extras=(x2, pp, target),
                                             out_dtypes=(F32, BF16, BF16), n_colsums=1, tm_pref=FUSED_ROWS)
    return dy, sq, saved


def _mlp_ple_bwd(i, dx3, sv, mlp_gain, ple_gain, w_up, w_down, w_gate):
    if "dpp" in sv:
        dpp, dzg = sv["dpp"], sv["dzg"]
    else:
        def fn(_i, _n, d, g, pp):
            return d * g, d * pp * g * (1.0 - g)
        dpp, dzg = _rows(f"ple_gate_bwd{i}", fn, [(dx3, "row"), (sv["gate"], "row"), (sv["pp"], "row")],
                         [(D_MODEL, BF16), (D_MODEL, BF16)], tr=512)
    d_w_ple = _mm(f"ple_proj_dw{i}", sv["p"], dpp, ta=True, out_dtypes=(BF16,))
    d_w_gate = _mm(f"ple_gate_dw{i}", sv["hp"], dzg, ta=True, out_dtypes=(BF16,))
    dx2, dx2b, d_ple_gain = _mm(f"ple_gate_dx{i}", dzg, w_gate, tb=True, epilogue=_norm_bwd_2,
                                extras=(sv["x2"], ple_gain, dx3), out_dtypes=(F32, BF16), n_colsums=1, tm_pref=FUSED_ROWS)
    d_w_down = _mm(f"mlp_down_dw{i}", sv["a"], dx2b, ta=True, out_dtypes=(BF16,))
    du = _mm(f"mlp_down_dx{i}", dx2b, w_down, tb=True,
             epilogue=lambda acc, ut: (acc * (2.0 * jnp.maximum(ut.astype(F32), 0.0)),), extras=(sv["u"],), out_dtypes=(BF16,))
    d_w_up = _mm(f"mlp_up_dw{i}", sv["hm"], du, ta=True, out_dtypes=(BF16,))
    dx1, dx1b, d_mlp_gain = _mm(f"mlp_up_dx{i}", du, w_up, tb=True, epilogue=_norm_bwd_2,
                                extras=(sv["x1"], mlp_gain, dx2), out_dtypes=(F32, BF16), n_colsums=1, tm_pref=FUSED_ROWS)
    return dx1, dx1b, dict(w_ple=d_w_ple, w_ple_gate=d_w_gate, w_down=d_w_down, w_up=d_w_up,
                           ple_norm=d_ple_gain, mlp_norm=d_mlp_gain)


def _after(small, token):
    return small + token[0:1, 0:1]


def _local_step(x, p, positions, target, W, P, rest_of_weights, send_layer1, send_mlp0, send_attn):
    consts = _head_consts()
    bd = _block_diag(1.0 / A_HEAD_DIM)
    bd1 = _block_diag(1.0)
    ct, st = _rope_tables(positions, consts)
    gains = jnp.stack([jnp.tile(v, A_HEADS) for g in range(3) for v in (P["attn_q_gain"][g], P["attn_k_gain"][g])])
    pad = LANE - DN_HEADS
    alog_row = jnp.pad(P["dn_a_log"].reshape(1, DN_HEADS), ((0, 0), (0, pad)))
    dt_row = jnp.pad(P["dn_dt_bias"].reshape(1, DN_HEADS), ((0, 0), (0, pad)))
    ogain_row = P["dn_o_gain"].reshape(1, DN_DIM)
    row = lambda name, i: P[name][i:i + 1]

    h0 = _rmsnorm_fwd("mix_norm0", x, row("mix_norm", 0))
    qkv = _mm("attn_qkv", h0, W["attn_w_qkv"], out_dtypes=(BF16,))
    qkvn = _attn_prep(qkv, gains, ct, st, consts, bd)
    os_, lses = zip(*[_attn_fwd(qkvn[g], g) for g in range(3)])
    o_attn = _attn_merge(os_, lses)
    x1, hm0 = _mm("attn_out", o_attn, W["attn_w_o"], epilogue=_res_norm, extras=(x, row("mlp_norm", 0)),
                  out_dtypes=(F32, BF16), tm_pref=FUSED_ROWS)
    W = {**W, **rest_of_weights(x1)}
    x3, h1, sv0 = _mlp_ple_fwd(0, x1, hm0, p[0], row("ple_norm", 0), row("mix_norm", 1),
                               W["w_up"][0], W["w_down"][0], W["w_ple"][0], W["w_ple_gate"][0])
    qkvz = _mm("dn_in_qkvz", h1, W["dn_w_qkvz"])
    ab = _mm("dn_in_ab", h1, W["dn_w_ab"])
    u, q, k, v, gb = _dn_prep(qkvz, ab, W["dn_conv"], alog_row, dt_row)
    o_dn, states = _dn_chunk_fwd(q, k, v, gb)
    on = _dn_post(o_dn, qkvz, ogain_row)
    x4, hm1 = _mm("dn_out", on, W["dn_w_o"], epilogue=_res_norm, extras=(x3, row("mlp_norm", 1)),
                  out_dtypes=(F32, BF16), tm_pref=FUSED_ROWS)
    dy, sq, sv1 = _mlp_ple_fwd(1, x4, hm1, p[1], row("ple_norm", 1), None,
                               W["w_up"][1], W["w_down"][1], W["w_ple"][1], W["w_ple_gate"][1], target=target)

    dx4, dx4b, g1 = _mlp_ple_bwd(1, dy, sv1, row("mlp_norm", 1), row("ple_norm", 1),
                                 W["w_up"][1], W["w_down"][1], W["w_ple_gate"][1])
    don = _mm("dn_out_dx", dx4b, W["dn_w_o"], tb=True)
    d_dn_w_o = _mm("dn_out_dw", on, dx4b, ta=True, out_dtypes=(BF16,))
    do_dn, dz, d_ogain = _dn_post_bwd(don, o_dn, qkvz, ogain_row)
    dq, dk, dv, dgb = _dn_chunk_bwd(q, k, v, gb, states, do_dn)
    du, dab, d_alog, d_dt = _dn_prep_bwd(dq, dk, dv, dgb, u, ab, alog_row, dt_row)
    dqkvz, d_conv = _dn_conv_bwd(du, dz, qkvz, W["dn_conv"])
    dh1 = _mm("dn_in_ab_dx", dab, W["dn_w_ab"], tb=True)
    dx3, d_mix1 = _mm("dn_in_qkvz_dx", dqkvz, W["dn_w_qkvz"], tb=True,
                      epilogue=lambda acc, part, xt, g, dres: _norm_bwd(acc + part, xt, g, dres),
                      extras=(dh1, x3, row("mix_norm", 1), dx4), n_colsums=1, tm_pref=FUSED_ROWS)
    d_w_qkvz = _mm("dn_in_qkvz_dw", h1, dqkvz, ta=True, out_dtypes=(BF16,))
    d_w_ab = _mm("dn_in_ab_dw", h1, dab, ta=True, out_dtypes=(BF16,))
    token = send_layer1(dict(
        dn_w_qkvz=d_w_qkvz, dn_w_ab=d_w_ab, dn_conv=jnp.concatenate(d_conv, 0), dn_w_o=d_dn_w_o,
        w_up=g1["w_up"], w_down=g1["w_down"], w_ple=g1["w_ple"], w_ple_gate=g1["w_ple_gate"]))
    dx1, dx1b, g0 = _mlp_ple_bwd(0, dx3, sv0, row("mlp_norm", 0), _after(row("ple_norm", 0), token),
                                 W["w_up"][0], W["w_down"][0], W["w_ple_gate"][0])
    token = send_mlp0(dict(w_up=g0["w_up"], w_down=g0["w_down"], w_ple=g0["w_ple"], w_ple_gate=g0["w_ple_gate"]))
    do_attn = _mm("attn_out_dx", dx1b, W["attn_w_o"], tb=True, epilogue=_add, extras=(_after(jnp.zeros((1, A_WIDTH), F32), token),))
    d_attn_w_o = _mm("attn_out_dw", o_attn, dx1b, ta=True, out_dtypes=(BF16,))
    dos, cs = _attn_merge_bwd(do_attn, os_, lses, bd1)
    grads9 = []
    for g in range(3):
        grads9 += list(_attn_bwd(qkvn[g], g, dos[g], lses[g], cs[g]))
    dqkv, dgains = _attn_prep_bwd(qkv, grads9, gains, ct, st, consts, bd)
    d_attn_w_qkv = _mm("attn_qkv_dw", h0, dqkv, ta=True, out_dtypes=(BF16,))
    token = send_attn(dict(attn_w_qkv=d_attn_w_qkv, attn_w_o=d_attn_w_o))
    dx0, d_mix0 = _mm("attn_qkv_dx", dqkv, W["attn_w_qkv"], tb=True, epilogue=_norm_bwd,
                      extras=(x, _after(row("mix_norm", 0), token), dx1), n_colsums=1, tm_pref=FUSED_ROWS)

    dg = jnp.stack([t.reshape(A_HEADS, A_HEAD_DIM).sum(0) for t in dgains])
    small = dict(
        mix_norm=jnp.concatenate([d_mix0, d_mix1], 0),
        attn_q_gain=dg[0::2][None], attn_k_gain=dg[1::2][None],
        dn_a_log=d_alog[:, :DN_HEADS], dn_dt_bias=d_dt[:, :DN_HEADS], dn_o_gain=d_ogain,
        mlp_norm=jnp.concatenate([g0["mlp_norm"], g1["mlp_norm"]], 0),
        ple_norm=jnp.concatenate([g0["ple_norm"], g1["ple_norm"]], 0),
    )
    return sq, dx0, small


MESH_IDS = pl.DeviceIdType.MESH
ANY = pl.BlockSpec(memory_space=pl.ANY)


def _place():
    return lax.axis_index("x"), lax.axis_index("y"), lax.axis_index("c")


def _sem_scratch(n_streams):
    return [pltpu.SemaphoreType.DMA((n_streams, N_DEV - 1)), pltpu.SemaphoreType.DMA((n_streams, N_DEV - 1)),
            pltpu.SemaphoreType.DMA((n_streams,))]


def _all_gather(name, arrays, streams):
    n_in, n_st = len(arrays), len(streams)
    shapes = [arrays[a].shape if li is None else arrays[a].shape[1:] for a, li in streams]

    def body(*refs):
        in_refs, out_refs, token = refs[:n_in], refs[n_in:n_in + n_st], refs[n_in + n_st]
        send_sems, recv_sems, local_sems = refs[n_in + n_st + 1:]
        token[...] = jnp.zeros_like(token)
        x, y, c = _place()
        me, sibling = (x, y, c), (x, y, 1 - c)
        chips = [(1 - x, y), (x, 1 - y), (1 - x, 1 - y)]

        def copy(s, k, block, to, own=False):
            a, li = streams[s]
            dst = out_refs[s].at[4 * block[0] + 2 * block[1] + block[2]]
            src = (in_refs[a] if li is None else in_refs[a].at[li]) if own else dst
            return pltpu.make_async_remote_copy(src_ref=src, dst_ref=dst, send_sem=send_sems.at[s, k],
                                                recv_sem=recv_sems.at[s, k], device_id=to, device_id_type=MESH_IDS)

        started = []
        for s, (a, li) in enumerate(streams):
            src = in_refs[a] if li is None else in_refs[a].at[li]
            mine = pltpu.make_async_copy(src, out_refs[s].at[4 * x + 2 * y + c], local_sems.at[s])
            mine.start()
            started.append(mine)
        sends = []
        for s in range(n_st):
            first = [copy(s, 0, me, sibling, own=True)]
            first += [copy(s, 1 + j, me, (*chip, c), own=True) for j, chip in enumerate(chips)]
            for cp in first:
                cp.start()
            sends += first
        for j, chip in enumerate(chips):
            for s in range(n_st):
                copy(s, 1 + j, (*chip, c), me).wait_recv()
                fwd = copy(s, 4 + j, (*chip, c), sibling)
                fwd.start()
                sends.append(fwd)
        for s in range(n_st):
            copy(s, 0, sibling, me).wait_recv()
            for j, chip in enumerate(chips):
                copy(s, 4 + j, (*chip, 1 - c), me).wait_recv()
        for cp in sends:
            cp.wait_send()
        for cp in started:
            cp.wait()

    res = pl.pallas_call(
        body, name=name,
        out_shape=[jax.ShapeDtypeStruct((N_DEV,) + tuple(sh), arrays[a].dtype) for sh, (a, _) in zip(shapes, streams)]
        + [jax.ShapeDtypeStruct((SUBLANE, LANE), F32)],
        in_specs=[ANY] * n_in, out_specs=[ANY] * n_st + [pl.BlockSpec(memory_space=pltpu.VMEM)],
        scratch_shapes=_sem_scratch(n_st),
    )(*arrays)
    return list(res[:n_st]), res[n_st]


HBM = pl.BlockSpec(memory_space=pltpu.HBM)
SEM = pl.BlockSpec(memory_space=pltpu.SEMAPHORE)
FLOWS = pltpu.CompilerParams(has_side_effects=pltpu.SideEffectType.DATAFLOW_SIDE_EFFECTING)


def _in_hbm(a):
    return pltpu.with_memory_space_constraint(a, pltpu.HBM)


def _hbm_like(a):
    return pltpu.HBM(a.shape, a.dtype)


def _peers(x, y, c):
    return [(1 - x if k & 4 else x, 1 - y if k & 2 else y, 1 - c if k & 1 else c) for k in range(1, N_DEV)]


def _start_copies(name, n_remote, n_own, make_copies, operands):
    n = len(operands)

    def body(*refs):
        for cp in make_copies(refs[:n], refs[n], refs[n + 1], refs[n + 2]):
            cp.start()
        refs[-1][...] = jnp.zeros_like(refs[-1])

    res = pl.pallas_call(
        body, name=name,
        out_shape=(pltpu.SemaphoreType.DMA((n_remote,)), pltpu.SemaphoreType.DMA((n_remote,)), pltpu.SemaphoreType.DMA((n_own,)),
                   *[_hbm_like(t) for t in operands], jax.ShapeDtypeStruct((SUBLANE, LANE), F32)),
        in_specs=[HBM] * n, out_specs=(SEM, SEM, SEM, *[HBM] * n, pl.BlockSpec(memory_space=pltpu.VMEM)),
        input_output_aliases={i: 3 + i for i in range(n)}, compiler_params=FLOWS,
    )(*[_in_hbm(t) for t in operands])
    return res[:3], list(res[3:3 + n]), res[-1]


def _wait_copies(name, make_waits, sems, operands, after):
    n = len(operands)

    def body(*refs):
        for wait in make_waits(refs[:n], refs[n], refs[n + 1], refs[n + 2]):
            wait()

    res = pl.pallas_call(
        body, name=name, out_shape=tuple(_hbm_like(t) for t in operands),
        in_specs=[HBM] * n + [SEM, SEM, SEM, ANY], out_specs=tuple([HBM] * n),
        input_output_aliases={i: i for i in range(n)}, compiler_params=FLOWS,
    )(*operands, *sems, after)
    return list(res)


def _gather_plan(n_in, streams):
    def block(arr, s):
        a, li = streams[s]
        return arr[a] if li is None else arr[a].at[li]

    def copies(refs, send_sems, recv_sems, own_sems, arrivals=False):
        arr, land = refs[:n_in], refs[n_in:]
        x, y, c = _place()
        me = 4 * x + 2 * y + c
        out = []
        for s in range(len(streams)):
            out.append(("own", pltpu.make_async_copy(block(arr, s), land[s].at[me], own_sems.at[s])))
            for k, (px, py, pc) in enumerate(_peers(x, y, c)):
                out.append(("remote", pltpu.make_async_remote_copy(
                    src_ref=block(arr, s), dst_ref=land[s].at[4 * px + 2 * py + pc if arrivals else me],
                    send_sem=send_sems.at[s * (N_DEV - 1) + k], recv_sem=recv_sems.at[s * (N_DEV - 1) + k],
                    device_id=(px, py, pc), device_id_type=MESH_IDS)))
        return out
    return copies


def _exchange_plan(n_st):
    def copies(refs, send_sems, recv_sems, own_sems, arrivals=False):
        snd, rcv = refs[:n_st], refs[n_st:]
        x, y, c = _place()
        me = 4 * x + 2 * y + c
        out = []
        for s in range(n_st):
            out.append(("own", pltpu.make_async_copy(snd[s].at[me], rcv[s].at[me], own_sems.at[s])))
            for k, (px, py, pc) in enumerate(_peers(x, y, c)):
                peer = 4 * px + 2 * py + pc
                out.append(("remote", pltpu.make_async_remote_copy(
                    src_ref=snd[s].at[peer], dst_ref=rcv[s].at[peer if arrivals else me],
                    send_sem=send_sems.at[s * (N_DEV - 1) + k], recv_sem=recv_sems.at[s * (N_DEV - 1) + k],
                    device_id=(px, py, pc), device_id_type=MESH_IDS)))
        return out
    return copies


def _split_transfer(tag, plan, n_streams, operands):
    sems, operands, token = _start_copies(f"{tag}_start", n_streams * (N_DEV - 1), n_streams,
                                          lambda refs, a, b, o: [cp for _, cp in plan(refs, a, b, o)], operands)

    def waits(refs, a, b, o):
        out = []
        for kind, cp in plan(refs, a, b, o, arrivals=True):
            out += [cp.wait] if kind == "own" else [cp.wait_send, cp.wait_recv]
        return out

    return (lambda after: _wait_copies(f"{tag}_wait", waits, sems, operands, after)), token


def _gather_async(tag, arrays, streams):
    lands = [lax.empty((N_DEV,) + tuple(arrays[a].shape if li is None else arrays[a].shape[1:]), arrays[a].dtype)
             for a, li in streams]
    finish, token = _split_transfer(tag, _gather_plan(len(arrays), streams), len(streams), list(arrays) + lands)
    return (lambda after: finish(after)[len(arrays):]), token


def _exchange_async(tag, sends):
    recvs = [lax.empty(t.shape, t.dtype) for t in sends]
    finish, token = _split_transfer(tag, _exchange_plan(len(sends)), len(sends), list(sends) + recvs)
    return (lambda after: finish(after)[len(sends):]), token


def _dn_in_pieces():
    n = (DN_QKVZ + 2 * DN_HEADS) // N_DEV
    segs = ((0, DN_QKV, 0, 0), (DN_QKV, DN_QKV + 2 * DN_HEADS, 1, 0), (DN_QKV + 2 * DN_HEADS, DN_QKVZ + 2 * DN_HEADS, 0, DN_QKV))
    out = []
    for d in range(N_DEV):
        lo, hi = d * n, (d + 1) * n
        for s0, s1, tgt, t0 in segs:
            a, b = max(lo, s0), min(hi, s1)
            if a < b:
                out.append((d, a - lo, b - lo, tgt, t0 + a - s0))
    return out


def _unpack_cols(name, g):
    _, K, n = g.shape
    tr = 256

    def body(g_ref, o_ref):
        for d in range(N_DEV):
            o_ref[:, d * n:(d + 1) * n] = g_ref[d]

    return pl.pallas_call(
        body, name=name, grid=(K // tr,), in_specs=[pl.BlockSpec((N_DEV, tr, n), lambda i: (0, i, 0))],
        out_specs=pl.BlockSpec((tr, N_DEV * n), lambda i: (i, 0)),
        out_shape=jax.ShapeDtypeStruct((K, N_DEV * n), g.dtype), compiler_params=_cparams(("parallel",)),
    )(g)


def _pack_cols(name, w):
    K, n = w.shape[0], w.shape[1] // N_DEV
    tr = 256

    def body(w_ref, o_ref):
        for d in range(N_DEV):
            o_ref[d] = w_ref[:, d * n:(d + 1) * n]

    return pl.pallas_call(
        body, name=name, grid=(K // tr,), in_specs=[pl.BlockSpec((tr, N_DEV * n), lambda i: (i, 0))],
        out_specs=pl.BlockSpec((N_DEV, tr, n), lambda i: (0, i, 0)),
        out_shape=jax.ShapeDtypeStruct((N_DEV, K, n), w.dtype), compiler_params=_cparams(("parallel",)),
    )(w)


def _unpack_dn_in(g):
    _, K, n = g.shape
    tr = 256

    def body(g_ref, qkvz_ref, ab_ref):
        ab_ref[...] = jnp.zeros_like(ab_ref)
        for d, c0, c1, tgt, t0 in _dn_in_pieces():
            (qkvz_ref, ab_ref)[tgt][:, t0:t0 + c1 - c0] = g_ref[d, :, c0:c1]

    return pl.pallas_call(
        body, name="unpack_dn_in", grid=(K // tr,), in_specs=[pl.BlockSpec((N_DEV, tr, n), lambda i: (0, i, 0))],
        out_specs=[pl.BlockSpec((tr, DN_QKVZ), lambda i: (i, 0)), pl.BlockSpec((tr, LANE), lambda i: (i, 0))],
        out_shape=[jax.ShapeDtypeStruct((K, DN_QKVZ), g.dtype), jax.ShapeDtypeStruct((K, LANE), g.dtype)],
        compiler_params=_cparams(("parallel",)),
    )(g)


def _pack_dn_in(d_qkvz, d_ab):
    K = d_qkvz.shape[0]
    n = (DN_QKVZ + 2 * DN_HEADS) // N_DEV
    tr = 256

    def body(qkvz_ref, ab_ref, o_ref):
        for d, c0, c1, tgt, t0 in _dn_in_pieces():
            o_ref[d, :, c0:c1] = (qkvz_ref, ab_ref)[tgt][:, t0:t0 + c1 - c0]

    return pl.pallas_call(
        body, name="pack_dn_in", grid=(K // tr,),
        in_specs=[pl.BlockSpec((tr, DN_QKVZ), lambda i: (i, 0)), pl.BlockSpec((tr, LANE), lambda i: (i, 0))],
        out_specs=pl.BlockSpec((N_DEV, tr, n), lambda i: (0, i, 0)),
        out_shape=jax.ShapeDtypeStruct((N_DEV, K, n), d_qkvz.dtype), compiler_params=_cparams(("parallel",)),
    )(d_qkvz, d_ab)


ADAMW_ROWS = 256


def _adamw(name, parts, w, m, v):
    n_layers, R, C = w.shape
    tr = min(R, ADAMW_ROWS)
    assert R % tr == 0 and len(parts) == n_layers and all(p.shape == (N_DEV, R, C) for p in parts)
    c1 = 1.0 - B1 ** STEP
    c2 = 1.0 - B2 ** STEP

    def body(*refs):
        p_refs = refs[:n_layers]
        w_ref, m_ref, v_ref, g_ref, d_ref, nm_ref, nv_ref = refs[n_layers:]
        layer = pl.program_id(0)
        for li, p_ref in enumerate(p_refs):
            @pl.when(layer == li)
            def _(p_ref=p_ref):
                g = p_ref[0].astype(F32)
                for dev in range(1, N_DEV):
                    g = g + p_ref[dev].astype(F32)
                nm = B1 * m_ref[...] + (1.0 - B1) * g
                nv = B2 * v_ref[...] + (1.0 - B2) * jnp.square(g)
                g_ref[...] = g
                nm_ref[...] = nm
                nv_ref[...] = nv
                d_ref[...] = -LR * ((nm / c1) / (jnp.sqrt(nv / c2) + ADAM_EPS) + WD * w_ref[...])

    blk = pl.BlockSpec((None, tr, C), lambda l, i: (l, i, 0))
    return pl.pallas_call(
        body, name=name, grid=(n_layers, R // tr),
        in_specs=[pl.BlockSpec((N_DEV, tr, C), lambda l, i: (0, i, 0))] * n_layers + [blk, blk, blk],
        out_specs=[blk] * 4, out_shape=[jax.ShapeDtypeStruct((n_layers, R, C), F32)] * 4,
        compiler_params=_cparams(("parallel", "parallel")),
    )(*parts, w, m, v)


SMALL = ("mix_norm", "attn_q_gain", "attn_k_gain", "dn_a_log", "dn_dt_bias", "dn_o_gain", "mlp_norm", "ple_norm")
WEIGHTS = ("mix_norm", "attn_w_qkv", "attn_q_gain", "attn_k_gain", "attn_w_o", "dn_w_in", "dn_conv", "dn_a_log",
           "dn_dt_bias", "dn_o_gain", "dn_w_o", "mlp_norm", "w_up", "w_down", "ple_norm", "w_ple", "w_ple_gate")


def _to_rows(flat, multiple):
    n = flat.shape[-1]
    rows = -(-n // (LANE * multiple)) * multiple
    return jnp.pad(flat, [(0, rows * LANE - n)]).reshape(rows, LANE)


def _cols_to_devices(w):
    K, N = w.shape
    return jnp.transpose(w.reshape(K, N_DEV, N // N_DEV), (1, 0, 2))


def _cols_from_devices(g):
    _, K, n = g.shape
    return jnp.transpose(g, (1, 0, 2)).reshape(K, N_DEV * n)


SMALL_ROWS = 96


def _pack_small(vals, loss_rows):
    rows = [_to_rows(vals[n].reshape(-1), SUBLANE) for n in SMALL] + [loss_rows]
    buf = jnp.concatenate(rows, 0)
    assert buf.shape == (SMALL_ROWS, LANE)
    return buf


def _unpack_small(buf, like):
    out, r = {}, 0
    for n in SMALL:
        sz = math.prod(like[n].shape)
        out[n] = buf[r:r + -(-sz // LANE)].reshape(-1)[:sz].reshape(like[n].shape)
        r += -(-sz // (LANE * SUBLANE)) * SUBLANE
    return out


def kernel(x, p, positions, mix_norm, attn_w_qkv, attn_q_gain, attn_k_gain, attn_w_o, dn_w_in, dn_conv, dn_a_log, dn_dt_bias, dn_o_gain, dn_w_o, mlp_norm, w_up, w_down, ple_norm, w_ple, w_ple_gate, loss_target, m_mix_norm, m_attn_w_qkv, m_attn_q_gain, m_attn_k_gain, m_attn_w_o, m_dn_w_in, m_dn_conv, m_dn_a_log, m_dn_dt_bias, m_dn_o_gain, m_dn_w_o, m_mlp_norm, m_w_up, m_w_down, m_ple_norm, m_w_ple, m_w_ple_gate, v_mix_norm, v_attn_w_qkv, v_attn_q_gain, v_attn_k_gain, v_attn_w_o, v_dn_w_in, v_dn_conv, v_dn_a_log, v_dn_dt_bias, v_dn_o_gain, v_dn_w_o, v_mlp_norm, v_w_up, v_w_down, v_ple_norm, v_w_ple, v_w_ple_gate):
    w = dict(mix_norm=mix_norm, attn_w_qkv=attn_w_qkv, attn_q_gain=attn_q_gain, attn_k_gain=attn_k_gain, attn_w_o=attn_w_o,
             dn_w_in=dn_w_in, dn_conv=dn_conv, dn_a_log=dn_a_log, dn_dt_bias=dn_dt_bias, dn_o_gain=dn_o_gain, dn_w_o=dn_w_o,
             mlp_norm=mlp_norm, w_up=w_up, w_down=w_down, ple_norm=ple_norm, w_ple=w_ple, w_ple_gate=w_ple_gate)
    m = dict(mix_norm=m_mix_norm, attn_w_qkv=m_attn_w_qkv, attn_q_gain=m_attn_q_gain, attn_k_gain=m_attn_k_gain,
             attn_w_o=m_attn_w_o, dn_w_in=m_dn_w_in, dn_conv=m_dn_conv, dn_a_log=m_dn_a_log, dn_dt_bias=m_dn_dt_bias,
             dn_o_gain=m_dn_o_gain, dn_w_o=m_dn_w_o, mlp_norm=m_mlp_norm, w_up=m_w_up, w_down=m_w_down,
             ple_norm=m_ple_norm, w_ple=m_w_ple, w_ple_gate=m_w_ple_gate)
    v = dict(mix_norm=v_mix_norm, attn_w_qkv=v_attn_w_qkv, attn_q_gain=v_attn_q_gain, attn_k_gain=v_attn_k_gain,
             attn_w_o=v_attn_w_o, dn_w_in=v_dn_w_in, dn_conv=v_dn_conv, dn_a_log=v_dn_a_log, dn_dt_bias=v_dn_dt_bias,
             dn_o_gain=v_dn_o_gain, dn_w_o=v_dn_w_o, mlp_norm=v_mlp_norm, w_up=v_w_up, w_down=v_w_down,
             ple_norm=v_ple_norm, w_ple=v_w_ple, w_ple_gate=v_w_ple_gate)
    S = x.shape[1]

    bf = lambda a: a.astype(BF16)
    rows_to_devices = lambda t: t.reshape(N_DEV, t.shape[0] // N_DEV, t.shape[1])

    (g_qkv, g_ao), token = _all_gather("gather_attn", [bf(attn_w_qkv[0]), bf(attn_w_o[0])], [(0, None), (1, None)])
    rest_shards = [bf(dn_w_in[0]), bf(dn_w_o[0]), bf(w_up), bf(w_down), bf(w_ple), bf(w_ple_gate), _after(dn_conv[0], token)]
    rest_streams = [(0, None), (1, None), (2, 0), (2, 1), (3, 0), (3, 1), (4, 0), (4, 1), (5, 0), (5, 1), (6, None)]
    rest_arrived, token = _gather_async("gather_rest", rest_shards, rest_streams)
    W = dict(attn_w_qkv=_unpack_cols("unpack_attn_qkv", g_qkv), attn_w_o=_cols_from_devices(g_ao))

    def rest_of_weights(after):
        g_in, g_do, g_up0, g_up1, g_dn0, g_dn1, g_pl0, g_pl1, g_gt0, g_gt1, g_conv = rest_arrived(after)
        rest = dict(
            dn_conv=jnp.transpose(g_conv, (1, 0, 2)).reshape(CONV_W, DN_QKV), dn_w_o=g_do.reshape(DN_WIDTH, D_MODEL),
            w_up=[_cols_from_devices(g_up0), _cols_from_devices(g_up1)],
            w_down=[g_dn0.reshape(D_FF, D_MODEL), g_dn1.reshape(D_FF, D_MODEL)],
            w_ple=[_cols_from_devices(g_pl0), _cols_from_devices(g_pl1)],
            w_ple_gate=[g_gt0.reshape(D_MODEL, D_MODEL), g_gt1.reshape(D_MODEL, D_MODEL)])
        rest["dn_w_qkvz"], rest["dn_w_ab"] = _unpack_dn_in(g_in)
        return rest

    pending = {}

    def mlp_sends(g):
        return [_cols_to_devices(g["w_up"]), rows_to_devices(g["w_down"]), _cols_to_devices(g["w_ple"]),
                rows_to_devices(g["w_ple_gate"])]

    def start(tag, sends):
        pending[tag], token = _exchange_async(f"exchange_{tag}", sends)
        return token

    def send_layer1(g):
        conv_send = jnp.transpose(g["dn_conv"].reshape(CONV_W, N_DEV, DN_QKV // N_DEV), (1, 0, 2))
        return start("layer1", [_pack_dn_in(g["dn_w_qkvz"], g["dn_w_ab"]), conv_send, rows_to_devices(g["dn_w_o"])] + mlp_sends(g))

    def send_mlp0(g):
        return start("mlp0", mlp_sends(g))

    def send_attn(g):
        return start("attn", [_pack_cols("pack_attn_qkv", g["attn_w_qkv"]), _cols_to_devices(g["attn_w_o"])])

    P = dict(mix_norm=_after(mix_norm, token), attn_q_gain=attn_q_gain[0], attn_k_gain=attn_k_gain[0], dn_a_log=dn_a_log[0],
             dn_dt_bias=dn_dt_bias[0], dn_o_gain=dn_o_gain[0], mlp_norm=mlp_norm, ple_norm=ple_norm)

    sq, dx0, small_g = _local_step(x[0], p[:, 0], positions.reshape(S, 1), loss_target[0], W, P,
                                   rest_of_weights, send_layer1, send_mlp0, send_attn)

    r_in, r_conv, r_do, r_up1, r_dn1, r_pl1, r_gt1 = pending["layer1"](dx0)
    r_up0, r_dn0, r_pl0, r_gt0 = pending["mlp0"](dx0)
    r_qkv, r_ao = pending["attn"](dx0)
    big = {}
    for n, parts in (("attn_w_qkv", [r_qkv]), ("attn_w_o", [r_ao]), ("dn_w_in", [r_in]), ("dn_conv", [r_conv]),
                     ("dn_w_o", [r_do]), ("w_up", [r_up0, r_up1]), ("w_down", [r_dn0, r_dn1]),
                     ("w_ple", [r_pl0, r_pl1]), ("w_ple_gate", [r_gt0, r_gt1])):
        big[n] = _adamw(f"adamw_{n}", parts, w[n], m[n], v[n])

    loss_rows = jnp.pad((0.5 / D_MODEL) * jnp.sum(sq, axis=1, keepdims=True), ((0, SUBLANE - 1), (0, LANE - 1)))
    small_like = {n: w[n] for n in SMALL}
    parts_s = _all_gather("gather_small", [_pack_small(small_g, loss_rows)], [(0, None)])[0][0]
    zero_rows = jnp.zeros((SUBLANE, LANE), F32)
    small = _adamw("adamw_small", [parts_s], _pack_small(w, zero_rows)[None], _pack_small(m, zero_rows)[None],
                   _pack_small(v, zero_rows)[None])
    loss = small[0][0, SMALL_ROWS - SUBLANE, 0]
    small = [_unpack_small(b[0], small_like) for b in small]

    outs = [loss, dx0[None]]
    for k in range(4):
        for n in WEIGHTS:
            outs.append(small[k][n] if n in SMALL else big[n][k])
    return tuple(outs)
```

```python
import functools
import math

import jax
import jax.numpy as jnp
from jax import lax
from jax.experimental import pallas as pl
from jax.experimental.pallas import tpu as pltpu

F32 = jnp.float32
BF16 = jnp.bfloat16
HIGHEST = lax.Precision.HIGHEST

N_DEV = 8
D_MODEL = 1024
EPS = 1e-6
SWA_GROUPS = ((128, 1), (512, 4), (2048, 16))
A_HEADS = 8
A_HEAD_DIM = 64
A_WIDTH = A_HEADS * A_HEAD_DIM
A_QKV = 3 * 3 * A_WIDTH
ROPE_DIM = 16
ROPE_HALF = 8
ROPE_THETA = 500000.0
BAND = 128
DN_HEADS = 8
DN_DIM = 128
DN_WIDTH = DN_HEADS * DN_DIM
CONV_W = 4
CHUNK = 64
D_FF = 4 * D_MODEL
PLE_DIM = 256
LR, B1, B2, ADAM_EPS, WD, STEP = 0.001, 0.9, 0.999, 1e-08, 0.01, 10

VMEM_LIMIT = 56 * 1024 * 1024
MXU_TILE = 1024
MM_SLAB = 256
LANE = 128
SUBLANE = 8


def _cparams(sem):
    return pltpu.CompilerParams(dimension_semantics=sem, vmem_limit_bytes=VMEM_LIMIT)


def _tile(n, pref):
    if n <= pref:
        return n
    t = (pref // LANE) * LANE
    while t >= LANE:
        if n % t == 0:
            return t
        t -= LANE
    raise ValueError(f"no tile for {n}")


def _dot(a, b, ca=1, cb=0, precision=None):
    return lax.dot_general(a, b, (((ca,), (cb,)), ((), ())), precision=precision,
                           preferred_element_type=F32)


def _bdot(a, b, ca=1, cb=0):
    return _dot(a.astype(BF16), b.astype(BF16), ca, cb)


def _mm(name, a, b, *, ta=False, tb=False, epilogue=None, extras=(), out_dtypes=(F32,), n_colsums=0,
        tm_pref=MXU_TILE, tn_pref=1536, tk_pref=MXU_TILE):
    M, K = (a.shape[1], a.shape[0]) if ta else a.shape
    N = b.shape[0] if tb else b.shape[1]
    assert (b.shape[1] if tb else b.shape[0]) == K
    tm, tn, tk = _tile(M, tm_pref), _tile(N, tn_pref), _tile(K, tk_pref)
    nk = K // tk
    n_out = len(out_dtypes)
    n_ext = len(extras)
    assert n_colsums == 0 or tn == N
    sub = min(tm, MM_SLAB)

    def body(*refs):
        a_ref, b_ref = refs[0], refs[1]
        ext = refs[2:2 + n_ext]
        outs = refs[2 + n_ext:2 + n_ext + n_out]
        sums = refs[2 + n_ext + n_out:2 + n_ext + n_out + n_colsums]
        row_tile, k = pl.program_id(0), pl.program_id(2)
        slabs = [slice(s * sub, (s + 1) * sub) for s in range(tm // sub)]

        def product(rows):
            return _bdot(a_ref[:, rows] if ta else a_ref[rows, :], b_ref[...], 0 if ta else 1, 1 if tb else 0)

        def finish(results):
            col_rows = []
            for rows, r in zip(slabs, results):
                res = (r,) if epilogue is None else epilogue(r, *[e[...] if e.shape[0] == 1 else e[rows, :] for e in ext])
                for o, v in zip(outs, res):
                    o[rows, :] = v.astype(o.dtype)
                col_rows.append(res[n_out:])
            for n, o in enumerate(sums):
                v = functools.reduce(lambda x, y: x + y, [c[n] for c in col_rows])

                @pl.when(row_tile == 0)
                def _(o=o, v=v):
                    o[...] = v

                @pl.when(row_tile > 0)
                def _(o=o, v=v):
                    o[...] += v

        if nk == 1:
            finish([product(rows) for rows in slabs])
            return
        acc = refs[-1]

        @pl.when(k == 0)
        def _():
            acc[...] = jnp.zeros_like(acc)

        for rows in slabs:
            acc[rows, :] += product(rows)

        @pl.when(k == nk - 1)
        def _():
            finish([acc[rows, :] for rows in slabs])

    a_spec = pl.BlockSpec((tk, tm), lambda i, j, k: (k, i)) if ta else pl.BlockSpec((tm, tk), lambda i, j, k: (i, k))
    b_spec = pl.BlockSpec((tn, tk), lambda i, j, k: (j, k)) if tb else pl.BlockSpec((tk, tn), lambda i, j, k: (k, j))
    ext_specs = []
    for e in extras:
        if e.shape[0] == 1 and M != 1:
            ext_specs.append(pl.BlockSpec((1, tn), lambda i, j, k: (0, j)))
        else:
            ext_specs.append(pl.BlockSpec((tm, tn), lambda i, j, k: (i, j)))
    out = pl.pallas_call(
        body, name=name,
        grid=(M // tm, N // tn, nk),
        in_specs=[a_spec, b_spec] + ext_specs,
        out_specs=[pl.BlockSpec((tm, tn), lambda i, j, k: (i, j)) for _ in range(n_out)]
        + [pl.BlockSpec((1, tn), lambda i, j, k: (0, 0)) for _ in range(n_colsums)],
        out_shape=[jax.ShapeDtypeStruct((M, N), dt) for dt in out_dtypes]
        + [jax.ShapeDtypeStruct((1, N), F32) for _ in range(n_colsums)],
        scratch_shapes=[pltpu.VMEM((tm, tn), F32)] if nk > 1 else [],
        compiler_params=_cparams(("arbitrary" if n_colsums else "parallel", "parallel", "arbitrary")),
    )(a, b, *extras)
    return out[0] if len(out) == 1 else tuple(out)


def _perm_matrices(tr, d):
    import numpy as np
    old = np.arange(tr)
    p = np.zeros((tr, tr), np.float32)
    p[(old % d) * (tr // d) + old // d, old] = 1.0
    return jnp.asarray(p, BF16), jnp.asarray(p.T, BF16)


def _permute(p, x):
    if x.dtype == BF16:
        return _dot(p, x)
    hi = x.astype(BF16)
    rest = x - hi.astype(F32)
    mid = rest.astype(BF16)
    lo = (rest - mid.astype(F32)).astype(BF16)
    return _dot(p, hi) + _dot(p, mid) + _dot(p, lo)


def _rows(name, fn, ins, outs, *, tr, accs=()):
    ins = [(e[0], e[1]) + (e[2] if len(e) > 2 else (0, e[0].shape[-1])) for e in ins]
    outs = [tuple(o) + (0,) * (3 - len(o)) for o in outs]
    n_rows = next(e[0].shape[0] if e[1] == "row" else e[0].shape[0] * e[0].shape[1] for e in ins if e[1] in ("row", "res"))
    assert n_rows % tr == 0 and tr % SUBLANE == 0
    steps = n_rows // tr
    t8 = tr // SUBLANE
    n8 = n_rows // SUBLANE
    dils = sorted({e[0].shape[0] for e in ins if e[1] == "res" and e[0].shape[0] > 1} | {o[2] for o in outs if o[2] > 1})
    perms = [m for d in dils for m in _perm_matrices(tr, d)]
    ins = ins + [(m, "full", 0, tr) for m in perms]
    n_in, n_out, n_acc = len(ins), len(outs), len(accs)

    def body(*refs):
        i = pl.program_id(0)
        to_res = {d: refs[n_in - len(perms) + 2 * j][...] for j, d in enumerate(dils)}
        to_tok = {d: refs[n_in - len(perms) + 2 * j + 1][...] for j, d in enumerate(dils)}
        tiles = []
        for r, e in zip(refs[:n_in - len(perms)], ins):
            d = e[0].shape[0] if e[1] == "res" else 0
            if d == 0:
                tiles.append(r[...])
            elif d == 1:
                tiles.append(r[0])
            else:
                tiles.append(_permute(to_tok[d], jnp.concatenate([r[j] for j in range(d)], axis=0)))
        vals = fn(i, steps, *tiles)
        if not isinstance(vals, (tuple, list)):
            vals = (vals,)
        assert len(vals) == n_out + n_acc
        for o, v, (_, dt, d) in zip(refs[n_in:n_in + n_out], vals[:n_out], outs):
            if d == 0:
                o[...] = v.astype(o.dtype)
            elif d == 1:
                o[0] = v.astype(o.dtype)
            else:
                y = _permute(to_res[d], v.astype(dt))
                for j in range(d):
                    o[j] = y[j * (tr // d):(j + 1) * (tr // d)].astype(o.dtype)
        if n_acc:
            acc_refs = refs[n_in + n_out:]

            @pl.when(i == 0)
            def _():
                for r in acc_refs:
                    r[...] = jnp.zeros_like(r)

            for r, v in zip(acc_refs, vals[n_out:]):
                r[...] += v.astype(r.dtype)

    in_specs = []
    for a, kind, cb, c in ins:
        if kind == "row":
            in_specs.append(pl.BlockSpec((tr, c), lambda i, cb=cb: (i, cb)))
        elif kind == "full":
            in_specs.append(pl.BlockSpec(a.shape, lambda i, z=(0,) * a.ndim: z))
        elif kind == "prev8":
            in_specs.append(pl.BlockSpec((SUBLANE, c), lambda i, cb=cb: (jnp.maximum(i * t8 - 1, 0), cb)))
        elif kind == "next8":
            in_specs.append(pl.BlockSpec((SUBLANE, c), lambda i, cb=cb: (jnp.minimum((i + 1) * t8, n8 - 1), cb)))
        elif kind == "res":
            d = a.shape[0]
            in_specs.append(pl.BlockSpec((d, tr // d, a.shape[2]), lambda i: (0, i, 0)))
        else:
            raise ValueError(kind)
    out_specs = [pl.BlockSpec((tr, c), lambda i: (i, 0)) if d == 0 else pl.BlockSpec((d, tr // d, c), lambda i: (0, i, 0))
                 for c, _, d in outs]
    out_specs += [pl.BlockSpec(s, lambda i, z=(0,) * len(s): z) for s, _ in accs]
    out_shape = [jax.ShapeDtypeStruct((n_rows, c) if d == 0 else (d, n_rows // d, c), dt) for c, dt, d in outs]
    out_shape += [jax.ShapeDtypeStruct(s, dt) for s, dt in accs]
    res = pl.pallas_call(
        body, name=name, grid=(steps,), in_specs=in_specs, out_specs=out_specs, out_shape=out_shape,
        compiler_params=_cparams(("arbitrary",) if n_acc else ("parallel",)),
    )(*[e[0] for e in ins])
    return res[0] if len(res) == 1 else tuple(res)


def _colsum(x):
    return jnp.sum(x, axis=0, keepdims=True)


def _sum_all(x):
    return jnp.sum(jnp.sum(x, axis=1, keepdims=True), axis=0, keepdims=True)


def _rmsnorm_fwd(name, x, gain):
    def fn(i, n, xt, g):
        r = lax.rsqrt(jnp.mean(xt * xt, axis=-1, keepdims=True) + EPS)
        return (xt * r * g,)
    return _rows(name, fn, [(x, "row"), (gain, "full")], [(x.shape[1], BF16)], tr=512)


FUSED_ROWS = 1024


def _res_norm(acc, res, g):
    x = res + acc
    return x, x * lax.rsqrt(jnp.mean(x * x, axis=-1, keepdims=True) + EPS) * g


def _norm_bwd(dh, x, g, dres):
    r = lax.rsqrt(jnp.mean(x * x, axis=-1, keepdims=True) + EPS)
    xh = x * r
    dxn = dh * g
    dx = dres + r * (dxn - xh * jnp.mean(dxn * xh, axis=-1, keepdims=True))
    return dx, _colsum(dh * xh)


def _norm_bwd_2(dh, x, g, dres):
    dx, dg = _norm_bwd(dh, x, g, dres)
    return dx, dx, dg


def _head_consts():
    import numpy as np
    e = np.arange(A_WIDTH) % A_HEAD_DIM
    inv = (np.float32(ROPE_THETA) ** (-np.arange(0, ROPE_DIM, 2, dtype=np.float32) / np.float32(ROPE_DIM))).astype(np.float32)
    c = np.zeros((8, A_WIDTH), np.float32)
    c[0] = np.where(e < ROPE_DIM, inv[e % ROPE_HALF], 0.0)
    c[1] = np.where(e < ROPE_HALF, -1.0, np.where(e < ROPE_DIM, 1.0, 0.0))
    c[2] = (e < ROPE_HALF).astype(np.float32)
    c[3] = (e < ROPE_DIM).astype(np.float32)
    return jnp.asarray(c)


def _block_diag(scale):
    import numpy as np
    h = np.arange(A_WIDTH) // A_HEAD_DIM
    return jnp.asarray((h[:, None] == h[None, :]).astype(np.float32) * scale, dtype=BF16)


def _seg_sum(x, bd):
    return _dot(x.astype(BF16), bd)


def _rope_tables(positions, consts):
    def fn(i, n, pos, c):
        ang = pos.astype(F32) * c[0:1, :LANE]
        return jnp.cos(ang), jnp.sin(ang) * c[1:2, :LANE]
    return _rows("rope_tables", fn, [(positions, "row"), (consts, "full")], [(LANE, F32), (LANE, F32)], tr=512)


def _rope_wide(t):
    return jnp.concatenate([t] * (A_WIDTH // LANE), axis=1)


def _rope_apply(y, ct, st, low):
    rolled = jnp.where(low, pltpu.roll(y, A_WIDTH - ROPE_HALF, 1), pltpu.roll(y, ROPE_HALF, 1))
    return y * ct + rolled * st


def _rope_apply_bwd(dout, ct, st, low, in16):
    t = dout * st
    back = jnp.where(low, pltpu.roll(t, A_WIDTH - ROPE_HALF, 1), jnp.where(in16, pltpu.roll(t, ROPE_HALF, 1), 0.0))
    return dout * ct + back


def _attn_prep(qkv, gains, ct, st, consts, bd):
    def fn(i, n, t, g, c_t, s_t, c, b):
        low = c[2:3, :] > 0.5
        c_t, s_t = _rope_wide(c_t), _rope_wide(s_t)
        groups = []
        for grp in range(3):
            cols = []
            for which in range(3):
                off = (grp * 3 + which) * A_WIDTH
                x = t[:, off:off + A_WIDTH].astype(F32)
                if which == 2:
                    cols.append(x.astype(BF16))
                    continue
                r = lax.rsqrt(_seg_sum(x * x, b) + EPS)
                y = x * r * g[grp * 2 + which:grp * 2 + which + 1, :]
                cols.append(_rope_apply(y, c_t, s_t, low).astype(BF16))
            groups.append(jnp.concatenate(cols, axis=1))
        return tuple(groups)
    return _rows("attn_prep", fn, [(qkv, "row"), (gains, "full"), (ct, "row"), (st, "row"), (consts, "full"), (bd, "full")],
                 [(3 * A_WIDTH, BF16, d) for _, d in SWA_GROUPS], tr=256)


def _band_mask(n):
    row = lax.broadcasted_iota(jnp.int32, (BAND, 2 * BAND), 0)
    col = lax.broadcasted_iota(jnp.int32, (BAND, 2 * BAND), 1)
    dist = row + BAND - col
    return (dist >= 0) & (dist <= BAND) & ((col >= BAND) | (n > 0))


def _attn_fwd(qkvn, grp):
    d, L, _ = qkvn.shape
    nblk = L // BAND
    assert L % BAND == 0 and d == SWA_GROUPS[grp][1]

    def body(q_ref, kc_ref, kp_ref, vc_ref, vp_ref, o_ref, lse_ref):
        n = pl.program_id(1)
        valid = _band_mask(n)
        first = lax.broadcasted_iota(jnp.int32, (BAND, LANE), 1) < A_HEAD_DIM
        pairs = [slice(pr * LANE, (pr + 1) * LANE) for pr in range(A_WIDTH // LANE)]
        halves = (first, jnp.logical_not(first))
        qps = [q_ref[:, sl] for sl in pairs]
        kcats = [jnp.concatenate([kp_ref[:, sl], kc_ref[:, sl]], axis=0) for sl in pairs]
        vcats = [jnp.concatenate([vp_ref[:, sl], vc_ref[:, sl]], axis=0) for sl in pairs]
        heads = [(pr, m) for pr in range(len(pairs)) for m in halves]
        ss = [_dot(jnp.where(m, qps[pr], jnp.zeros_like(qps[pr])), kcats[pr], 1, 1) for pr, m in heads]
        ps, lses = [], []
        for s in ss:
            s = jnp.where(valid, s * (A_HEAD_DIM ** -0.5), -1e30)
            mx = jnp.max(s, axis=-1, keepdims=True)
            e = jnp.exp(s - mx)
            l = jnp.sum(e, axis=-1, keepdims=True)
            ps.append((e / l).astype(BF16))
            lses.append(mx + jnp.log(l))
        os_ = [_dot(p, vcats[pr]) for p, (pr, _) in zip(ps, heads)]
        o_ref[...] = jnp.concatenate([jnp.where(first, os_[2 * pr], os_[2 * pr + 1]) for pr in range(len(pairs))], axis=1)
        lse_ref[...] = jnp.concatenate([jnp.where(first, lses[2 * pr], lses[2 * pr + 1]) for pr in range(len(pairs))], axis=1)

    blk = (None, BAND, A_WIDTH)
    return pl.pallas_call(
        body, name=f"attn_fwd_g{grp}", grid=(d, nblk),
        in_specs=[pl.BlockSpec(blk, lambda r, n: (r, n, 0)),
                  pl.BlockSpec(blk, lambda r, n: (r, n, 1)),
                  pl.BlockSpec(blk, lambda r, n: (r, jnp.maximum(n - 1, 0), 1)),
                  pl.BlockSpec(blk, lambda r, n: (r, n, 2)),
                  pl.BlockSpec(blk, lambda r, n: (r, jnp.maximum(n - 1, 0), 2))],
        out_specs=[pl.BlockSpec(blk, lambda r, n: (r, n, 0)), pl.BlockSpec(blk, lambda r, n: (r, n, 0))],
        out_shape=[jax.ShapeDtypeStruct((d, L, A_WIDTH), F32)] * 2,
        compiler_params=_cparams(("parallel", "parallel")),
    )(qkvn, qkvn, qkvn, qkvn, qkvn)


def _merge_weights(l0, l1, l2):
    mx = jnp.maximum(jnp.maximum(l0, l1), l2)
    e0, e1, e2 = jnp.exp(l0 - mx), jnp.exp(l1 - mx), jnp.exp(l2 - mx)
    inv = 1.0 / (e0 + e1 + e2)
    return e0 * inv, e1 * inv, e2 * inv


def _attn_merge(os_, lses):
    def fn(i, n, o0, o1, o2, l0, l1, l2):
        w0, w1, w2 = _merge_weights(l0, l1, l2)
        return (w0 * o0 + w1 * o1 + w2 * o2,)
    ins = [(a, "res") for a in (*os_, *lses)]
    return _rows("attn_merge", fn, ins, [(A_WIDTH, BF16)], tr=256)


def _attn_merge_bwd(do, os_, lses, bd1):
    def fn(i, n, dot_, o0, o1, o2, l0, l1, l2, b):
        w0, w1, w2 = _merge_weights(l0, l1, l2)
        o = w0 * o0 + w1 * o1 + w2 * o2
        dsum = _seg_sum(dot_ * o, b)
        return (w0 * dot_, w1 * dot_, w2 * dot_, -w0 * dsum, -w1 * dsum, -w2 * dsum)
    ins = [(do, "row")] + [(a, "res") for a in (*os_, *lses)] + [(bd1, "full")]
    res = _rows("attn_merge_bwd", fn, ins, [(A_WIDTH, dt, d) for dt in (BF16, F32) for _, d in SWA_GROUPS], tr=256)
    return res[:3], res[3:]


def _lane_pick(x, lane_idx, lane):
    return jnp.sum(jnp.where(lane_idx == lane, x, 0.0), axis=-1, keepdims=True)


def _attn_bwd(qkvn, grp, do_g, lse, c_g):
    d, L, _ = qkvn.shape
    nblk = L // BAND

    def body(q_ref, kc_ref, kp_ref, vc_ref, vp_ref, do_ref, lse_ref, c_ref, dq_ref, dk_ref, dv_ref, ck, cv_):
        n = pl.program_id(1)

        @pl.when(n == 0)
        def _():
            ck[...] = jnp.zeros_like(ck)
            cv_[...] = jnp.zeros_like(cv_)

        @pl.when(n < nblk)
        def _():
            valid = _band_mask(n)
            lane = lax.broadcasted_iota(jnp.int32, (BAND, LANE), 1)
            first = lane < A_HEAD_DIM
            lane2 = lax.broadcasted_iota(jnp.int32, (2 * BAND, LANE), 1) < A_HEAD_DIM
            pairs = [slice(pr * LANE, (pr + 1) * LANE) for pr in range(A_WIDTH // LANE)]
            halves = (first, jnp.logical_not(first))
            qps = [q_ref[:, sl] for sl in pairs]
            dops = [do_ref[:, sl] for sl in pairs]
            kcats = [jnp.concatenate([kp_ref[:, sl], kc_ref[:, sl]], axis=0) for sl in pairs]
            vcats = [jnp.concatenate([vp_ref[:, sl], vc_ref[:, sl]], axis=0) for sl in pairs]
            heads = [(pr, hh) for pr in range(len(pairs)) for hh in range(2)]
            zero = jnp.zeros_like(qps[0])
            ss = [_dot(jnp.where(halves[hh], qps[pr], zero), kcats[pr], 1, 1) for pr, hh in heads]
            dps = [_dot(jnp.where(halves[hh], dops[pr], zero), vcats[pr], 1, 1) for pr, hh in heads]
            dss, pbs = [], []
            for (pr, hh), s, dp in zip(heads, ss, dps):
                lse_h = _lane_pick(lse_ref[:, pairs[pr]], lane, hh * A_HEAD_DIM)
                c_h = _lane_pick(c_ref[:, pairs[pr]], lane, hh * A_HEAD_DIM)
                p = jnp.where(valid, jnp.exp(s * (A_HEAD_DIM ** -0.5) - lse_h), 0.0)
                dss.append((p * (dp + c_h) * (A_HEAD_DIM ** -0.5)).astype(BF16))
                pbs.append(p.astype(BF16))
            dqs = [_dot(ds, kcats[pr]) for ds, (pr, _) in zip(dss, heads)]
            dks = [_dot(ds, qps[pr], 0, 0) for ds, (pr, _) in zip(dss, heads)]
            dvs = [_dot(pb, dops[pr], 0, 0) for pb, (pr, _) in zip(pbs, heads)]
            for pr, sl in enumerate(pairs):
                dq_ref[:, sl] = jnp.where(first, dqs[2 * pr], dqs[2 * pr + 1])
                dkc = jnp.where(lane2, dks[2 * pr], dks[2 * pr + 1])
                dvc = jnp.where(lane2, dvs[2 * pr], dvs[2 * pr + 1])
                dk_ref[:, sl] = ck[:, sl] + dkc[:BAND]
                dv_ref[:, sl] = cv_[:, sl] + dvc[:BAND]
                ck[:, sl] = dkc[BAND:]
                cv_[:, sl] = dvc[BAND:]

        @pl.when(n == nblk)
        def _():
            dk_ref[...] = ck[...]
            dv_ref[...] = cv_[...]

    blk = (None, BAND, A_WIDTH)
    last = nblk - 1
    qn = lambda n: jnp.minimum(n, last)
    pn = lambda n: jnp.clip(n - 1, 0, last)
    return tuple(pl.pallas_call(
        body, name=f"attn_bwd_g{grp}", grid=(d, nblk + 1),
        in_specs=[pl.BlockSpec(blk, lambda r, n: (r, qn(n), 0)),
                  pl.BlockSpec(blk, lambda r, n: (r, qn(n), 1)),
                  pl.BlockSpec(blk, lambda r, n: (r, pn(n), 1)),
                  pl.BlockSpec(blk, lambda r, n: (r, qn(n), 2)),
                  pl.BlockSpec(blk, lambda r, n: (r, pn(n), 2)),
                  pl.BlockSpec(blk, lambda r, n: (r, qn(n), 0)),
                  pl.BlockSpec(blk, lambda r, n: (r, qn(n), 0)),
                  pl.BlockSpec(blk, lambda r, n: (r, qn(n), 0))],
        out_specs=[pl.BlockSpec(blk, lambda r, n: (r, qn(n), 0)),
                   pl.BlockSpec(blk, lambda r, n: (r, pn(n), 0)),
                   pl.BlockSpec(blk, lambda r, n: (r, pn(n), 0))],
        out_shape=[jax.ShapeDtypeStruct((d, L, A_WIDTH), F32)] * 3,
        scratch_shapes=[pltpu.VMEM((BAND, A_WIDTH), F32), pltpu.VMEM((BAND, A_WIDTH), F32)],
        compiler_params=_cparams(("parallel", "arbitrary")),
    )(qkvn, qkvn, qkvn, qkvn, qkvn, do_g, lse, c_g))


def _attn_prep_bwd(qkv, grads, gains, ct, st, consts, bd):
    def fn(i, n, t, g, c_t, s_t, c, b, *gr):
        low = c[2:3, :] > 0.5
        in16 = c[3:4, :] > 0.5
        c_t, s_t = _rope_wide(c_t), _rope_wide(s_t)
        cols, dgs = [], []
        for grp in range(3):
            for which in range(3):
                dout = gr[grp * 3 + which]
                if which == 2:
                    cols.append(dout.astype(BF16))
                    continue
                off = (grp * 3 + which) * A_WIDTH
                x = t[:, off:off + A_WIDTH].astype(F32)
                gain = g[grp * 2 + which:grp * 2 + which + 1, :]
                r = lax.rsqrt(_seg_sum(x * x, b) + EPS)
                xh = x * r
                dy = _rope_apply_bwd(dout, c_t, s_t, low, in16)
                dyn = dy * gain
                dx = r * (dyn - xh * _seg_sum(dyn * xh, b))
                cols.append(dx.astype(BF16))
                dgs.append(_colsum(dy * xh))
        return (jnp.concatenate(cols, axis=1), *dgs)
    ins = [(qkv, "row"), (gains, "full"), (ct, "row"), (st, "row"), (consts, "full"), (bd, "full")] + [(a, "res") for a in grads]
    res = _rows("attn_prep_bwd", fn, ins, [(A_QKV, BF16)], tr=128, accs=[((1, A_WIDTH), F32)] * 6)
    return res[0], res[1:]


DN_QKV = 3 * DN_WIDTH
DN_QKVZ = DN_QKV + DN_WIDTH


def _sigmoid(x):
    return jax.nn.sigmoid(x)


def _softplus(x):
    return jnp.maximum(x, 0.0) + jnp.log(1.0 + jnp.exp(-jnp.abs(x)))


def _conv_taps(xs, w, tr):
    acc = None
    for j in range(CONV_W):
        sh = CONV_W - 1 - j
        term = (pltpu.roll(xs, sh, 0) if sh else xs)[SUBLANE:] * w[j:j + 1, :]
        acc = term if acc is None else acc + term
    return acc


def _dn_prep(qkvz, ab, convw, alog_row, dt_row):
    tr = 256

    def fn(i, n, x, xp, abt, w, al, dt):
        xp = jnp.where(i > 0, xp, 0.0)
        u = _conv_taps(jnp.concatenate([xp, x], axis=0), w, tr)
        y = u * _sigmoid(u)
        qs, ks = [], []
        for h in range(DN_HEADS):
            for dst, base, sc in ((qs, 0, DN_DIM ** -0.5), (ks, DN_WIDTH, 1.0)):
                seg = y[:, base + h * DN_DIM:base + (h + 1) * DN_DIM]
                dst.append(seg * (lax.rsqrt(jnp.sum(seg * seg, axis=-1, keepdims=True) + EPS) * sc))
        lane = lax.broadcasted_iota(jnp.int32, abt.shape, 1)
        g = -jnp.exp(al) * _softplus(abt + dt)
        gb = jnp.where(lane < DN_HEADS, g, jnp.where(lane < 2 * DN_HEADS, _sigmoid(abt), 0.0))
        return u, jnp.concatenate(qs, axis=1), jnp.concatenate(ks, axis=1), y[:, 2 * DN_WIDTH:], gb

    ins = [(qkvz, "row", (0, DN_QKV)), (qkvz, "prev8", (0, DN_QKV)), (ab, "row"), (convw, "full"),
           (alog_row, "full"), (dt_row, "full")]
    return _rows("dn_prep", fn, ins, [(DN_QKV, BF16), (DN_WIDTH, F32), (DN_WIDTH, F32), (DN_WIDTH, F32), (LANE, F32)], tr=tr)


def _tri_masks():
    row = lax.broadcasted_iota(jnp.int32, (CHUNK, CHUNK), 0)
    col = lax.broadcasted_iota(jnp.int32, (CHUNK, CHUNK), 1)
    return row >= col, row > col, row == col


def _heads(fn, *lists):
    return [fn(*xs) for xs in zip(*lists)]


def _split(x):
    hi = x.astype(BF16)
    return hi, (x - hi.astype(F32)).astype(BF16)


def _dot3(a, b, ca=1, cb=0):
    (ah, al), (bh, bl) = a, b
    return _dot(ah, bh, ca, cb) + (_dot(ah, bl, ca, cb) + _dot(al, bh, ca, cb))


SPLIT_STEPS = 3


def _unit_lower_inverse(a_list, eye):
    ts = [eye - a for a in a_list]
    parts = [_split(a) for a in a_list]
    for step in range(5):
        if step < SPLIT_STEPS:
            parts = [_split(_dot3(p, p)) for p in parts]
            ts = [t + _dot3(_split(t), p) for t, p in zip(ts, parts)]
        else:
            parts = [(_dot(p[0], p[0]).astype(BF16), None) for p in parts]
            ts = [t + _dot(t.astype(BF16), p[0]) for t, p in zip(ts, parts)]
    return ts


def _dn_terms(qs, ks, vs, gb, solved=None):
    lower, strict, diag = _tri_masks()
    lane = lax.broadcasted_iota(jnp.int32, (CHUNK, LANE), 1)
    is_last = lax.broadcasted_iota(jnp.int32, (CHUNK, 1), 0) == CHUNK - 1
    hs = range(DN_HEADS)
    gc = _dot(lower.astype(F32), gb, precision=HIGHEST)
    gct = jnp.transpose(gc)
    bcol = [_lane_pick(gb, lane, DN_HEADS + h) for h in hs]
    gcol = [_lane_pick(gc, lane, h) for h in hs]
    glast = [jnp.sum(jnp.where(is_last, g, 0.0), axis=0, keepdims=True) for g in gcol]
    decay = [jnp.exp(jnp.where(lower, gcol[h] - gct[h:h + 1, :], -1e30)) for h in hs]
    kb = _heads(lambda k, b: k * b, ks, bcol)
    both = _heads(lambda q, x, k: _bdot(jnp.concatenate([q, x], axis=0), k, 1, 1), qs, kb, ks)
    qk = [x[:CHUNK] for x in both]
    kk = [x[CHUNK:] for x in both]
    a = _heads(lambda x, d: jnp.where(strict, x * d, 0.0), kk, decay)
    eg = [jnp.exp(g) for g in gcol]
    egl = _heads(lambda gl, g: jnp.exp(gl - g), glast, gcol)
    rhs_w = _heads(lambda x, e: x * e, kb, eg)
    if solved is None:
        t_full = _unit_lower_inverse(a, diag.astype(F32))
        t = [_split(x) for x in t_full]
        uw = _heads(lambda tt, v, b, r: _dot3(tt, _split(jnp.concatenate([v * b, r], axis=1))), t, vs, bcol, rhs_w)
        u = [x[:, :DN_DIM] for x in uw]
        w = [x[:, DN_DIM:] for x in uw]
    else:
        t_full, u, w = solved
        t = [_split(x) for x in t_full]
    return dict(bcol=bcol, decay=decay, kb=kb, a=a, t=t, t_full=t_full, eg=eg, egl=egl, rhs_w=rhs_w, u=u, w=w,
                attn=_heads(lambda x, d: x * d, qk, decay), q_dec=_heads(lambda q, e: q * e, qs, eg),
                k_dec=_heads(lambda k, e: k * e, ks, egl), c_dec=[jnp.exp(g) for g in glast],
                lower=lower, strict=strict, lane=lane, is_last=is_last)


def _head_slices(ref):
    return [ref[:, h * DN_DIM:(h + 1) * DN_DIM] for h in range(DN_HEADS)]


def _dn_chunk_fwd(q, k, v, gb):
    S = q.shape[0]
    N = S // CHUNK

    def body(q_ref, k_ref, v_ref, gb_ref, o_ref, st_ref, t_ref, u_ref, w_ref, state):
        @pl.when(pl.program_id(0) == 0)
        def _():
            state[...] = jnp.zeros_like(state)

        f = _dn_terms(_head_slices(q_ref), _head_slices(k_ref), _head_slices(v_ref), gb_ref[...])
        s = [state[h] for h in range(DN_HEADS)]
        for h in range(DN_HEADS):
            st_ref[0, h] = s[h]
            t_ref[0, h] = f["t_full"][h]
            u_ref[:, h * DN_DIM:(h + 1) * DN_DIM] = f["u"][h]
            w_ref[:, h * DN_DIM:(h + 1) * DN_DIM] = f["w"][h]
        sb = [x.astype(BF16) for x in s]
        v_new = _heads(lambda u, w, x: u - _bdot(w, x), f["u"], f["w"], sb)
        o = _heads(lambda qd, x, at, vn: _bdot(qd, x) + _bdot(at, vn), f["q_dec"], sb, f["attn"], v_new)
        new_s = _heads(lambda x, c, kd, vn: x * c + _bdot(kd, vn, 0, 0), s, f["c_dec"], f["k_dec"], v_new)
        for h in range(DN_HEADS):
            o_ref[:, h * DN_DIM:(h + 1) * DN_DIM] = o[h]
            state[h] = new_s[h]

    blk = pl.BlockSpec((CHUNK, DN_WIDTH), lambda n: (n, 0))
    st_blk = pl.BlockSpec((1, DN_HEADS, DN_DIM, DN_DIM), lambda n: (n, 0, 0, 0))
    t_blk = pl.BlockSpec((1, DN_HEADS, CHUNK, CHUNK), lambda n: (n, 0, 0, 0))
    wide = jax.ShapeDtypeStruct((S, DN_WIDTH), F32)
    o, states, t, u, w = pl.pallas_call(
        body, name="dn_chunk_fwd", grid=(N,),
        in_specs=[blk, blk, blk, pl.BlockSpec((CHUNK, LANE), lambda n: (n, 0))],
        out_specs=[blk, st_blk, t_blk, blk, blk],
        out_shape=[wide, jax.ShapeDtypeStruct((N, DN_HEADS, DN_DIM, DN_DIM), F32),
                   jax.ShapeDtypeStruct((N, DN_HEADS, CHUNK, CHUNK), F32), wide, wide],
        scratch_shapes=[pltpu.VMEM((DN_HEADS, DN_DIM, DN_DIM), F32)],
        compiler_params=_cparams(("arbitrary",)),
    )(q, k, v, gb)
    return o, (states, t, u, w)


def _dn_chunk_bwd(q, k, v, gb, saved, do):
    S = q.shape[0]
    N = S // CHUNK
    states, t_saved, u_saved, w_saved = saved

    def body(q_ref, k_ref, v_ref, gb_ref, st_ref, t_ref, u_ref, w_ref, do_ref, dq_ref, dk_ref, dv_ref, dgb_ref, dstate):
        @pl.when(pl.program_id(0) == 0)
        def _():
            dstate[...] = jnp.zeros_like(dstate)

        hs = range(DN_HEADS)
        qs, ks, vs, dos = (_head_slices(r) for r in (q_ref, k_ref, v_ref, do_ref))
        f = _dn_terms(qs, ks, vs, gb_ref[...], ([t_ref[0, h] for h in hs], _head_slices(u_ref), _head_slices(w_ref)))
        lane, is_last = f["lane"], f["is_last"]
        rowsum = lambda x: jnp.sum(x, axis=-1, keepdims=True)
        s = [st_ref[0, h] for h in hs]
        dsn = [dstate[h] for h in hs]
        sb = [x.astype(BF16) for x in s]
        dsb = [x.astype(BF16) for x in dsn]
        dob = [x.astype(BF16) for x in dos]
        v_new = _heads(lambda u, w, x: u - _bdot(w, x), f["u"], f["w"], sb)
        dv_new = _heads(lambda at, d, kd, x: _bdot(at, d, 0, 0) + _bdot(kd, x), f["attn"], dob, f["k_dec"], dsb)
        dattn = _heads(lambda d, vn: _bdot(d, vn, 1, 1), dob, v_new)
        dq_dec = _heads(lambda d, x: _bdot(d, x, 1, 1), dob, sb)
        dk_dec = _heads(lambda vn, x: _bdot(vn, x, 1, 1), v_new, dsb)
        dw = _heads(lambda dv_, x: -_bdot(dv_, x, 1, 1), dv_new, sb)
        new_ds = _heads(lambda x, c, qd, d, w, dv_: x * c + _bdot(qd, d, 0, 0) - _bdot(w, dv_, 0, 0),
                        dsn, f["c_dec"], f["q_dec"], dob, f["w"], dv_new)
        for h in hs:
            dstate[h] = new_ds[h]
        drhs = _heads(lambda tt, x, y: _dot3(tt, _split(jnp.concatenate([x, y], axis=1)), 0, 0), f["t"], dv_new, dw)
        drhs_u = [x[:, :DN_DIM] for x in drhs]
        drhs_w = [x[:, DN_DIM:] for x in drhs]
        da = _heads(lambda du_, u, dw_, w: jnp.where(f["strict"], -(_bdot(du_, u, 1, 1) + _bdot(dw_, w, 1, 1)), 0.0),
                    drhs_u, f["u"], drhs_w, f["w"])
        dkk = _heads(lambda x, d: x * d, da, f["decay"])
        dqk = _heads(lambda x, d: x * d, dattn, f["decay"])
        by_k = _heads(lambda x, y, k_: _bdot(jnp.concatenate([x, y], axis=0), k_), dqk, dkk, ks)
        dq = _heads(lambda x, dqd, e: x[:CHUNK] + dqd * e, by_k, dq_dec, f["eg"])
        dkb = _heads(lambda x, dw_, e: x[CHUNK:] + dw_ * e, by_k, drhs_w, f["eg"])
        dk = _heads(lambda x, kb_, y, q_, dkd, el, dkb_, b: _bdot(x, kb_, 0, 0) + _bdot(y, q_, 0, 0) + dkd * el + dkb_ * b,
                    dkk, f["kb"], dqk, qs, dk_dec, f["egl"], dkb, f["bcol"])
        m = _heads(lambda x, a_, y, at: x * a_ + y * at, da, f["a"], dattn, f["attn"])
        ones = jnp.ones((CHUNK, LANE), BF16)
        col_m = [(_dot(mh, ones, 0, 0) + _dot(ml, ones, 0, 0))[:, 0:1] for mh, ml in map(_split, m)]
        dgc_all = jnp.zeros((CHUNK, LANE), F32)
        dbeta_all = jnp.zeros((CHUNK, LANE), F32)
        for h in hs:
            dq_ref[:, h * DN_DIM:(h + 1) * DN_DIM] = dq[h]
            dk_ref[:, h * DN_DIM:(h + 1) * DN_DIM] = dk[h]
            dv_ref[:, h * DN_DIM:(h + 1) * DN_DIM] = drhs_u[h] * f["bcol"][h]
            kdec_term = rowsum(dk_dec[h] * f["k_dec"][h])
            dc_dec = _sum_all(dsn[h] * s[h])
            dgc = (rowsum(m[h]) - col_m[h] + rowsum(dq_dec[h] * f["q_dec"][h]) - kdec_term
                   + rowsum(drhs_w[h] * f["rhs_w"][h]))
            last_extra = jnp.sum(kdec_term, axis=0, keepdims=True) + dc_dec * f["c_dec"][h]
            dgc = dgc + jnp.where(is_last, last_extra, 0.0)
            dbeta = rowsum(drhs_u[h] * vs[h]) + rowsum(dkb[h] * ks[h])
            dgc_all = jnp.where(lane == h, dgc, dgc_all)
            dbeta_all = jnp.where(lane == DN_HEADS + h, dbeta, dbeta_all)
        dg_all = _dot(f["lower"].astype(F32), dgc_all, 0, 0, precision=HIGHEST)
        dgb_ref[...] = jnp.where(lane < DN_HEADS, dg_all, dbeta_all)

    rev = lambda n: (N - 1 - n, 0)
    blk = pl.BlockSpec((CHUNK, DN_WIDTH), rev)
    gblk = pl.BlockSpec((CHUNK, LANE), rev)
    st_blk = pl.BlockSpec((1, DN_HEADS, DN_DIM, DN_DIM), lambda n: (N - 1 - n, 0, 0, 0))
    t_blk = pl.BlockSpec((1, DN_HEADS, CHUNK, CHUNK), lambda n: (N - 1 - n, 0, 0, 0))
    return pl.pallas_call(
        body, name="dn_chunk_bwd", grid=(N,),
        in_specs=[blk, blk, blk, gblk, st_blk, t_blk, blk, blk, blk],
        out_specs=[blk, blk, blk, gblk],
        out_shape=[jax.ShapeDtypeStruct((S, DN_WIDTH), F32)] * 3 + [jax.ShapeDtypeStruct((S, LANE), F32)],
        scratch_shapes=[pltpu.VMEM((DN_HEADS, DN_DIM, DN_DIM), F32)],
        compiler_params=_cparams(("arbitrary",)),
    )(q, k, v, gb, states, t_saved, u_saved, w_saved, do)


def _dn_post(o, qkvz, gain_row):
    def fn(i, n, ot, z, g):
        cols = []
        for h in range(DN_HEADS):
            seg = ot[:, h * DN_DIM:(h + 1) * DN_DIM]
            cols.append(seg * lax.rsqrt(jnp.mean(seg * seg, axis=-1, keepdims=True) + EPS) * g)
        return (jnp.concatenate(cols, axis=1) * (z * _sigmoid(z)),)
    return _rows("dn_post", fn, [(o, "row"), (qkvz, "row", (3, DN_WIDTH)), (gain_row, "full")], [(DN_WIDTH, BF16)], tr=512)


def _dn_post_bwd(don, o, qkvz, gain_row):
    def fn(i, n, dy, ot, z, g):
        sg = _sigmoid(z)
        sz = z * sg
        dos, ohs = [], []
        dg = jnp.zeros((1, DN_DIM), F32)
        for h in range(DN_HEADS):
            sl = slice(h * DN_DIM, (h + 1) * DN_DIM)
            seg = ot[:, sl]
            r = lax.rsqrt(jnp.mean(seg * seg, axis=-1, keepdims=True) + EPS)
            oh = seg * r
            dno = dy[:, sl] * sz[:, sl]
            dg = dg + _colsum(dno * oh)
            dn = dno * g
            dos.append(r * (dn - oh * jnp.mean(dn * oh, axis=-1, keepdims=True)))
            ohs.append(oh * g)
        dz = dy * jnp.concatenate(ohs, axis=1) * (sg * (1.0 + z * (1.0 - sg)))
        return jnp.concatenate(dos, axis=1), dz, dg
    ins = [(don, "row"), (o, "row"), (qkvz, "row", (3, DN_WIDTH)), (gain_row, "full")]
    return _rows("dn_post_bwd", fn, ins, [(DN_WIDTH, F32), (DN_WIDTH, F32)], tr=256, accs=[((1, DN_DIM), F32)])


def _dn_prep_bwd(dq, dk, dv, dgb, u, ab, alog_row, dt_row):
    def fn(i, n, dqt, dkt, dvt, dgbt, ut, abt, al, dt):
        ut = ut.astype(F32)
        sg = _sigmoid(ut)
        y = ut * sg
        dys = []
        for grad, base, sc in ((dqt, 0, DN_DIM ** -0.5), (dkt, DN_WIDTH, 1.0)):
            for h in range(DN_HEADS):
                seg = y[:, base + h * DN_DIM:base + (h + 1) * DN_DIM]
                gr = grad[:, h * DN_DIM:(h + 1) * DN_DIM]
                r = lax.rsqrt(jnp.sum(seg * seg, axis=-1, keepdims=True) + EPS)
                xh = seg * r
                dys.append((r * sc) * (gr - xh * jnp.sum(gr * xh, axis=-1, keepdims=True)))
        dy = jnp.concatenate(dys + [dvt], axis=1)
        du = dy * (sg * (1.0 + ut * (1.0 - sg)))
        lane = lax.broadcasted_iota(jnp.int32, abt.shape, 1)
        is_g = lane < DN_HEADS
        ea = jnp.exp(al)
        x = abt + dt
        slope = -ea * _sigmoid(x)
        gval = -ea * _softplus(x)
        dg = jnp.where(is_g, dgbt, 0.0)
        beta = _sigmoid(abt)
        dab = jnp.where(is_g, dg * slope, jnp.where(lane < 2 * DN_HEADS, dgbt * beta * (1.0 - beta), 0.0))
        return du, dab, _colsum(dg * gval), _colsum(dg * slope)
    ins = [(dq, "row"), (dk, "row"), (dv, "row"), (dgb, "row"), (u, "row"), (ab, "row"), (alog_row, "full"), (dt_row, "full")]
    return _rows("dn_prep_bwd", fn, ins, [(DN_QKV, F32), (LANE, BF16)], tr=256, accs=[((1, LANE), F32)] * 2)


def _dn_conv_bwd(du, dz, qkvz, convw):
    tr = 256

    def fn(i, n, dut, dun, dzt, x, xp, w):
        dun = jnp.where(i < n - 1, dun, 0.0)
        dus = jnp.concatenate([dut, dun], axis=0)
        xs = jnp.concatenate([jnp.where(i > 0, xp, 0.0), x], axis=0)
        dx = None
        dws = []
        for j in range(CONV_W):
            sh = CONV_W - 1 - j
            term = (pltpu.roll(dus, tr + SUBLANE - sh, 0) if sh else dus)[:tr] * w[j:j + 1, :]
            dx = term if dx is None else dx + term
            dws.append(_colsum(dut * (pltpu.roll(xs, sh, 0) if sh else xs)[SUBLANE:]))
        return (jnp.concatenate([dx.astype(BF16), dzt.astype(BF16)], axis=1), *dws)

    ins = [(du, "row"), (du, "next8"), (dz, "row"), (qkvz, "row", (0, DN_QKV)), (qkvz, "prev8", (0, DN_QKV)), (convw, "full")]
    res = _rows("dn_conv_bwd", fn, ins, [(DN_QKVZ, BF16)], tr=tr, accs=[((1, DN_QKV), F32)] * CONV_W)
    return res[0], res[1:]


def _add(acc, r):
    return (r + acc,)


def _mlp_ple_fwd(i, x1, hm, p_i, ple_gain, next_gain, w_up, w_down, w_ple, w_gate, target=None):
    u, a = _mm(f"mlp_up{i}", hm, w_up, epilogue=lambda acc: (acc, jnp.square(jnp.maximum(acc, 0.0))),
               out_dtypes=(BF16, BF16))
    x2, hp = _mm(f"mlp_down{i}", a, w_down, epilogue=_res_norm, extras=(x1, ple_gain), out_dtypes=(F32, BF16),
                 tm_pref=FUSED_ROWS)
    pp = _mm(f"ple_proj{i}", p_i, w_ple)

    def gate_epilogue(acc, x2t, ppt, g):
        gate = _sigmoid(acc)
        x3 = x2t + ppt * gate
        return x3, gate, x3 * lax.rsqrt(jnp.mean(x3 * x3, axis=-1, keepdims=True) + EPS) * g

    def loss_epilogue(acc, x2t, ppt, tt):
        gate = _sigmoid(acc)
        err = x2t + ppt * gate - tt
        dy = err * (1.0 / D_MODEL)
        return dy, dy * gate, dy * ppt * gate * (1.0 - gate), _colsum(err * err)

    saved = dict(x1=x1, hm=hm, u=u, a=a, x2=x2, hp=hp, pp=pp, p=p_i)
    if target is None:
        x3, saved["gate"], h_next = _mm(f"ple_gate{i}", hp, w_gate, epilogue=gate_epilogue, extras=(x2, pp, next_gain),
                                        out_dtypes=(F32, F32, BF16), tm_pref=FUSED_ROWS)
        return x3, h_next, saved
    dy, saved["dpp"], saved["dzg"], sq = _mm(f"ple_gate{i}", hp, w_gate, epilogue=loss_epilogue, extras=(x2, pp, target),
                                             out_dtypes=(F32, BF16, BF16), n_colsums=1, tm_pref=FUSED_ROWS)
    return dy, sq, saved


def _mlp_ple_bwd(i, dx3, sv, mlp_gain, ple_gain, w_up, w_down, w_gate):
    if "dpp" in sv:
        dpp, dzg = sv["dpp"], sv["dzg"]
    else:
        def fn(_i, _n, d, g, pp):
            return d * g, d * pp * g * (1.0 - g)
        dpp, dzg = _rows(f"ple_gate_bwd{i}", fn, [(dx3, "row"), (sv["gate"], "row"), (sv["pp"], "row")],
                         [(D_MODEL, BF16), (D_MODEL, BF16)], tr=512)
    d_w_ple = _mm(f"ple_proj_dw{i}", sv["p"], dpp, ta=True, out_dtypes=(BF16,))
    d_w_gate = _mm(f"ple_gate_dw{i}", sv["hp"], dzg, ta=True, out_dtypes=(BF16,))
    dx2, dx2b, d_ple_gain = _mm(f"ple_gate_dx{i}", dzg, w_gate, tb=True, epilogue=_norm_bwd_2,
                                extras=(sv["x2"], ple_gain, dx3), out_dtypes=(F32, BF16), n_colsums=1, tm_pref=FUSED_ROWS)
    d_w_down = _mm(f"mlp_down_dw{i}", sv["a"], dx2b, ta=True, out_dtypes=(BF16,))
    du = _mm(f"mlp_down_dx{i}", dx2b, w_down, tb=True,
             epilogue=lambda acc, ut: (acc * (2.0 * jnp.maximum(ut.astype(F32), 0.0)),), extras=(sv["u"],), out_dtypes=(BF16,))
    d_w_up = _mm(f"mlp_up_dw{i}", sv["hm"], du, ta=True, out_dtypes=(BF16,))
    dx1, dx1b, d_mlp_gain = _mm(f"mlp_up_dx{i}", du, w_up, tb=True, epilogue=_norm_bwd_2,
                                extras=(sv["x1"], mlp_gain, dx2), out_dtypes=(F32, BF16), n_colsums=1, tm_pref=FUSED_ROWS)
    return dx1, dx1b, dict(w_ple=d_w_ple, w_ple_gate=d_w_gate, w_down=d_w_down, w_up=d_w_up,
                           ple_norm=d_ple_gain, mlp_norm=d_mlp_gain)


def _after(small, token):
    return small + token[0:1, 0:1]


def _local_step(x, p, positions, target, W, P, rest_of_weights, send_layer1, send_mlp0, send_attn):
    consts = _head_consts()
    bd = _block_diag(1.0 / A_HEAD_DIM)
    bd1 = _block_diag(1.0)
    ct, st = _rope_tables(positions, consts)
    gains = jnp.stack([jnp.tile(v, A_HEADS) for g in range(3) for v in (P["attn_q_gain"][g], P["attn_k_gain"][g])])
    pad = LANE - DN_HEADS
    alog_row = jnp.pad(P["dn_a_log"].reshape(1, DN_HEADS), ((0, 0), (0, pad)))
    dt_row = jnp.pad(P["dn_dt_bias"].reshape(1, DN_HEADS), ((0, 0), (0, pad)))
    ogain_row = P["dn_o_gain"].reshape(1, DN_DIM)
    row = lambda name, i: P[name][i:i + 1]

    h0 = _rmsnorm_fwd("mix_norm0", x, row("mix_norm", 0))
    qkv = _mm("attn_qkv", h0, W["attn_w_qkv"], out_dtypes=(BF16,))
    qkvn = _attn_prep(qkv, gains, ct, st, consts, bd)
    os_, lses = zip(*[_attn_fwd(qkvn[g], g) for g in range(3)])
    o_attn = _attn_merge(os_, lses)
    x1, hm0 = _mm("attn_out", o_attn, W["attn_w_o"], epilogue=_res_norm, extras=(x, row("mlp_norm", 0)),
                  out_dtypes=(F32, BF16), tm_pref=FUSED_ROWS)
    W = {**W, **rest_of_weights(x1)}
    x3, h1, sv0 = _mlp_ple_fwd(0, x1, hm0, p[0], row("ple_norm", 0), row("mix_norm", 1),
                               W["w_up"][0], W["w_down"][0], W["w_ple"][0], W["w_ple_gate"][0])
    qkvz = _mm("dn_in_qkvz", h1, W["dn_w_qkvz"])
    ab = _mm("dn_in_ab", h1, W["dn_w_ab"])
    u, q, k, v, gb = _dn_prep(qkvz, ab, W["dn_conv"], alog_row, dt_row)
    o_dn, states = _dn_chunk_fwd(q, k, v, gb)
    on = _dn_post(o_dn, qkvz, ogain_row)
    x4, hm1 = _mm("dn_out", on, W["dn_w_o"], epilogue=_res_norm, extras=(x3, row("mlp_norm", 1)),
                  out_dtypes=(F32, BF16), tm_pref=FUSED_ROWS)
    dy, sq, sv1 = _mlp_ple_fwd(1, x4, hm1, p[1], row("ple_norm", 1), None,
                               W["w_up"][1], W["w_down"][1], W["w_ple"][1], W["w_ple_gate"][1], target=target)

    dx4, dx4b, g1 = _mlp_ple_bwd(1, dy, sv1, row("mlp_norm", 1), row("ple_norm", 1),
                                 W["w_up"][1], W["w_down"][1], W["w_ple_gate"][1])
    don = _mm("dn_out_dx", dx4b, W["dn_w_o"], tb=True)
    d_dn_w_o = _mm("dn_out_dw", on, dx4b, ta=True, out_dtypes=(BF16,))
    do_dn, dz, d_ogain = _dn_post_bwd(don, o_dn, qkvz, ogain_row)
    dq, dk, dv, dgb = _dn_chunk_bwd(q, k, v, gb, states, do_dn)
    du, dab, d_alog, d_dt = _dn_prep_bwd(dq, dk, dv, dgb, u, ab, alog_row, dt_row)
    dqkvz, d_conv = _dn_conv_bwd(du, dz, qkvz, W["dn_conv"])
    dh1 = _mm("dn_in_ab_dx", dab, W["dn_w_ab"], tb=True)
    dx3, d_mix1 = _mm("dn_in_qkvz_dx", dqkvz, W["dn_w_qkvz"], tb=True,
                      epilogue=lambda acc, part, xt, g, dres: _norm_bwd(acc + part, xt, g, dres),
                      extras=(dh1, x3, row("mix_norm", 1), dx4), n_colsums=1, tm_pref=FUSED_ROWS)
    d_w_qkvz = _mm("dn_in_qkvz_dw", h1, dqkvz, ta=True, out_dtypes=(BF16,))
    d_w_ab = _mm("dn_in_ab_dw", h1, dab, ta=True, out_dtypes=(BF16,))
    token = send_layer1(dict(
        dn_w_qkvz=d_w_qkvz, dn_w_ab=d_w_ab, dn_conv=jnp.concatenate(d_conv, 0), dn_w_o=d_dn_w_o,
        w_up=g1["w_up"], w_down=g1["w_down"], w_ple=g1["w_ple"], w_ple_gate=g1["w_ple_gate"]))
    dx1, dx1b, g0 = _mlp_ple_bwd(0, dx3, sv0, row("mlp_norm", 0), _after(row("ple_norm", 0), token),
                                 W["w_up"][0], W["w_down"][0], W["w_ple_gate"][0])
    token = send_mlp0(dict(w_up=g0["w_up"], w_down=g0["w_down"], w_ple=g0["w_ple"], w_ple_gate=g0["w_ple_gate"]))
    do_attn = _mm("attn_out_dx", dx1b, W["attn_w_o"], tb=True, epilogue=_add, extras=(_after(jnp.zeros((1, A_WIDTH), F32), token),))
    d_attn_w_o = _mm("attn_out_dw", o_attn, dx1b, ta=True, out_dtypes=(BF16,))
    dos, cs = _attn_merge_bwd(do_attn, os_, lses, bd1)
    grads9 = []
    for g in range(3):
        grads9 += list(_attn_bwd(qkvn[g], g, dos[g], lses[g], cs[g]))
    dqkv, dgains = _attn_prep_bwd(qkv, grads9, gains, ct, st, consts, bd)
    d_attn_w_qkv = _mm("attn_qkv_dw", h0, dqkv, ta=True, out_dtypes=(BF16,))
    token = send_attn(dict(attn_w_qkv=d_attn_w_qkv, attn_w_o=d_attn_w_o))
    dx0, d_mix0 = _mm("attn_qkv_dx", dqkv, W["attn_w_qkv"], tb=True, epilogue=_norm_bwd,
                      extras=(x, _after(row("mix_norm", 0), token), dx1), n_colsums=1, tm_pref=FUSED_ROWS)

    dg = jnp.stack([t.reshape(A_HEADS, A_HEAD_DIM).sum(0) for t in dgains])
    small = dict(
        mix_norm=jnp.concatenate([d_mix0, d_mix1], 0),
        attn_q_gain=dg[0::2][None], attn_k_gain=dg[1::2][None],
        dn_a_log=d_alog[:, :DN_HEADS], dn_dt_bias=d_dt[:, :DN_HEADS], dn_o_gain=d_ogain,
        mlp_norm=jnp.concatenate([g0["mlp_norm"], g1["mlp_norm"]], 0),
        ple_norm=jnp.concatenate([g0["ple_norm"], g1["ple_norm"]], 0),
    )
    return sq, dx0, small


MESH_IDS = pl.DeviceIdType.MESH
ANY = pl.BlockSpec(memory_space=pl.ANY)


def _place():
    return lax.axis_index("x"), lax.axis_index("y"), lax.axis_index("c")


def _sem_scratch(n_streams):
    return [pltpu.SemaphoreType.DMA((n_streams, N_DEV - 1)), pltpu.SemaphoreType.DMA((n_streams, N_DEV - 1)),
            pltpu.SemaphoreType.DMA((n_streams,))]


def _all_gather(name, arrays, streams):
    n_in, n_st = len(arrays), len(streams)
    shapes = [arrays[a].shape if li is None else arrays[a].shape[1:] for a, li in streams]

    def body(*refs):
        in_refs, out_refs, token = refs[:n_in], refs[n_in:n_in + n_st], refs[n_in + n_st]
        send_sems, recv_sems, local_sems = refs[n_in + n_st + 1:]
        token[...] = jnp.zeros_like(token)
        x, y, c = _place()
        me, sibling = (x, y, c), (x, y, 1 - c)
        chips = [(1 - x, y), (x, 1 - y), (1 - x, 1 - y)]

        def copy(s, k, block, to, own=False):
            a, li = streams[s]
            dst = out_refs[s].at[4 * block[0] + 2 * block[1] + block[2]]
            src = (in_refs[a] if li is None else in_refs[a].at[li]) if own else dst
            return pltpu.make_async_remote_copy(src_ref=src, dst_ref=dst, send_sem=send_sems.at[s, k],
                                                recv_sem=recv_sems.at[s, k], device_id=to, device_id_type=MESH_IDS)

        started = []
        for s, (a, li) in enumerate(streams):
            src = in_refs[a] if li is None else in_refs[a].at[li]
            mine = pltpu.make_async_copy(src, out_refs[s].at[4 * x + 2 * y + c], local_sems.at[s])
            mine.start()
            started.append(mine)
        sends = []
        for s in range(n_st):
            first = [copy(s, 0, me, sibling, own=True)]
            first += [copy(s, 1 + j, me, (*chip, c), own=True) for j, chip in enumerate(chips)]
            for cp in first:
                cp.start()
            sends += first
        for j, chip in enumerate(chips):
            for s in range(n_st):
                copy(s, 1 + j, (*chip, c), me).wait_recv()
                fwd = copy(s, 4 + j, (*chip, c), sibling)
                fwd.start()
                sends.append(fwd)
        for s in range(n_st):
            copy(s, 0, sibling, me).wait_recv()
            for j, chip in enumerate(chips):
                copy(s, 4 + j, (*chip, 1 - c), me).wait_recv()
        for cp in sends:
            cp.wait_send()
        for cp in started:
            cp.wait()

    res = pl.pallas_call(
        body, name=name,
        out_shape=[jax.ShapeDtypeStruct((N_DEV,) + tuple(sh), arrays[a].dtype) for sh, (a, _) in zip(shapes, streams)]
        + [jax.ShapeDtypeStruct((SUBLANE, LANE), F32)],
        in_specs=[ANY] * n_in, out_specs=[ANY] * n_st + [pl.BlockSpec(memory_space=pltpu.VMEM)],
        scratch_shapes=_sem_scratch(n_st),
    )(*arrays)
    return list(res[:n_st]), res[n_st]


HBM = pl.BlockSpec(memory_space=pltpu.HBM)
SEM = pl.BlockSpec(memory_space=pltpu.SEMAPHORE)
FLOWS = pltpu.CompilerParams(has_side_effects=pltpu.SideEffectType.DATAFLOW_SIDE_EFFECTING)


def _in_hbm(a):
    return pltpu.with_memory_space_constraint(a, pltpu.HBM)


def _hbm_like(a):
    return pltpu.HBM(a.shape, a.dtype)


def _peers(x, y, c):
    return [(1 - x if k & 4 else x, 1 - y if k & 2 else y, 1 - c if k & 1 else c) for k in range(1, N_DEV)]


def _start_copies(name, n_remote, n_own, make_copies, operands):
    n = len(operands)

    def body(*refs):
        for cp in make_copies(refs[:n], refs[n], refs[n + 1], refs[n + 2]):
            cp.start()
        refs[-1][...] = jnp.zeros_like(refs[-1])

    res = pl.pallas_call(
        body, name=name,
        out_shape=(pltpu.SemaphoreType.DMA((n_remote,)), pltpu.SemaphoreType.DMA((n_remote,)), pltpu.SemaphoreType.DMA((n_own,)),
                   *[_hbm_like(t) for t in operands], jax.ShapeDtypeStruct((SUBLANE, LANE), F32)),
        in_specs=[HBM] * n, out_specs=(SEM, SEM, SEM, *[HBM] * n, pl.BlockSpec(memory_space=pltpu.VMEM)),
        input_output_aliases={i: 3 + i for i in range(n)}, compiler_params=FLOWS,
    )(*[_in_hbm(t) for t in operands])
    return res[:3], list(res[3:3 + n]), res[-1]


def _wait_copies(name, make_waits, sems, operands, after):
    n = len(operands)

    def body(*refs):
        for wait in make_waits(refs[:n], refs[n], refs[n + 1], refs[n + 2]):
            wait()

    res = pl.pallas_call(
        body, name=name, out_shape=tuple(_hbm_like(t) for t in operands),
        in_specs=[HBM] * n + [SEM, SEM, SEM, ANY], out_specs=tuple([HBM] * n),
        input_output_aliases={i: i for i in range(n)}, compiler_params=FLOWS,
    )(*operands, *sems, after)
    return list(res)


def _gather_plan(n_in, streams):
    def block(arr, s):
        a, li = streams[s]
        return arr[a] if li is None else arr[a].at[li]

    def copies(refs, send_sems, recv_sems, own_sems, arrivals=False):
        arr, land = refs[:n_in], refs[n_in:]
        x, y, c = _place()
        me = 4 * x + 2 * y + c
        out = []
        for s in range(len(streams)):
            out.append(("own", pltpu.make_async_copy(block(arr, s), land[s].at[me], own_sems.at[s])))
            for k, (px, py, pc) in enumerate(_peers(x, y, c)):
                out.append(("remote", pltpu.make_async_remote_copy(
                    src_ref=block(arr, s), dst_ref=land[s].at[4 * px + 2 * py + pc if arrivals else me],
                    send_sem=send_sems.at[s * (N_DEV - 1) + k], recv_sem=recv_sems.at[s * (N_DEV - 1) + k],
                    device_id=(px, py, pc), device_id_type=MESH_IDS)))
        return out
    return copies


def _exchange_plan(n_st):
    def copies(refs, send_sems, recv_sems, own_sems, arrivals=False):
        snd, rcv = refs[:n_st], refs[n_st:]
        x, y, c = _place()
        me = 4 * x + 2 * y + c
        out = []
        for s in range(n_st):
            out.append(("own", pltpu.make_async_copy(snd[s].at[me], rcv[s].at[me], own_sems.at[s])))
            for k, (px, py, pc) in enumerate(_peers(x, y, c)):
                peer = 4 * px + 2 * py + pc
                out.append(("remote", pltpu.make_async_remote_copy(
                    src_ref=snd[s].at[peer], dst_ref=rcv[s].at[peer if arrivals else me],
                    send_sem=send_sems.at[s * (N_DEV - 1) + k], recv_sem=recv_sems.at[s * (N_DEV - 1) + k],
                    device_id=(px, py, pc), device_id_type=MESH_IDS)))
        return out
    return copies


def _split_transfer(tag, plan, n_streams, operands):
    sems, operands, token = _start_copies(f"{tag}_start", n_streams * (N_DEV - 1), n_streams,
                                          lambda refs, a, b, o: [cp for _, cp in plan(refs, a, b, o)], operands)

    def waits(refs, a, b, o):
        out = []
        for kind, cp in plan(refs, a, b, o, arrivals=True):
            out += [cp.wait] if kind == "own" else [cp.wait_send, cp.wait_recv]
        return out

    return (lambda after: _wait_copies(f"{tag}_wait", waits, sems, operands, after)), token


def _gather_async(tag, arrays, streams):
    lands = [lax.empty((N_DEV,) + tuple(arrays[a].shape if li is None else arrays[a].shape[1:]), arrays[a].dtype)
             for a, li in streams]
    finish, token = _split_transfer(tag, _gather_plan(len(arrays), streams), len(streams), list(arrays) + lands)
    return (lambda after: finish(after)[len(arrays):]), token


def _exchange_async(tag, sends):
    recvs = [lax.empty(t.shape, t.dtype) for t in sends]
    finish, token = _split_transfer(tag, _exchange_plan(len(sends)), len(sends), list(sends) + recvs)
    return (lambda after: finish(after)[len(sends):]), token


def _dn_in_pieces():
    n = (DN_QKVZ + 2 * DN_HEADS) // N_DEV
    segs = ((0, DN_QKV, 0, 0), (DN_QKV, DN_QKV + 2 * DN_HEADS, 1, 0), (DN_QKV + 2 * DN_HEADS, DN_QKVZ + 2 * DN_HEADS, 0, DN_QKV))
    out = []
    for d in range(N_DEV):
        lo, hi = d * n, (d + 1) * n
        for s0, s1, tgt, t0 in segs:
            a, b = max(lo, s0), min(hi, s1)
            if a < b:
                out.append((d, a - lo, b - lo, tgt, t0 + a - s0))
    return out


def _unpack_cols(name, g):
    _, K, n = g.shape
    tr = 256

    def body(g_ref, o_ref):
        for d in range(N_DEV):
            o_ref[:, d * n:(d + 1) * n] = g_ref[d]

    return pl.pallas_call(
        body, name=name, grid=(K // tr,), in_specs=[pl.BlockSpec((N_DEV, tr, n), lambda i: (0, i, 0))],
        out_specs=pl.BlockSpec((tr, N_DEV * n), lambda i: (i, 0)),
        out_shape=jax.ShapeDtypeStruct((K, N_DEV * n), g.dtype), compiler_params=_cparams(("parallel",)),
    )(g)


def _pack_cols(name, w):
    K, n = w.shape[0], w.shape[1] // N_DEV
    tr = 256

    def body(w_ref, o_ref):
        for d in range(N_DEV):
            o_ref[d] = w_ref[:, d * n:(d + 1) * n]

    return pl.pallas_call(
        body, name=name, grid=(K // tr,), in_specs=[pl.BlockSpec((tr, N_DEV * n), lambda i: (i, 0))],
        out_specs=pl.BlockSpec((N_DEV, tr, n), lambda i: (0, i, 0)),
        out_shape=jax.ShapeDtypeStruct((N_DEV, K, n), w.dtype), compiler_params=_cparams(("parallel",)),
    )(w)


def _unpack_dn_in(g):
    _, K, n = g.shape
    tr = 256

    def body(g_ref, qkvz_ref, ab_ref):
        ab_ref[...] = jnp.zeros_like(ab_ref)
        for d, c0, c1, tgt, t0 in _dn_in_pieces():
            (qkvz_ref, ab_ref)[tgt][:, t0:t0 + c1 - c0] = g_ref[d, :, c0:c1]

    return pl.pallas_call(
        body, name="unpack_dn_in", grid=(K // tr,), in_specs=[pl.BlockSpec((N_DEV, tr, n), lambda i: (0, i, 0))],
        out_specs=[pl.BlockSpec((tr, DN_QKVZ), lambda i: (i, 0)), pl.BlockSpec((tr, LANE), lambda i: (i, 0))],
        out_shape=[jax.ShapeDtypeStruct((K, DN_QKVZ), g.dtype), jax.ShapeDtypeStruct((K, LANE), g.dtype)],
        compiler_params=_cparams(("parallel",)),
    )(g)


def _pack_dn_in(d_qkvz, d_ab):
    K = d_qkvz.shape[0]
    n = (DN_QKVZ + 2 * DN_HEADS) // N_DEV
    tr = 256

    def body(qkvz_ref, ab_ref, o_ref):
        for d, c0, c1, tgt, t0 in _dn_in_pieces():
            o_ref[d, :, c0:c1] = (qkvz_ref, ab_ref)[tgt][:, t0:t0 + c1 - c0]

    return pl.pallas_call(
        body, name="pack_dn_in", grid=(K // tr,),
        in_specs=[pl.BlockSpec((tr, DN_QKVZ), lambda i: (i, 0)), pl.BlockSpec((tr, LANE), lambda i: (i, 0))],
        out_specs=pl.BlockSpec((N_DEV, tr, n), lambda i: (0, i, 0)),
        out_shape=jax.ShapeDtypeStruct((N_DEV, K, n), d_qkvz.dtype), compiler_params=_cparams(("parallel",)),
    )(d_qkvz, d_ab)


ADAMW_ROWS = 256


def _adamw(name, parts, w, m, v):
    n_layers, R, C = w.shape
    tr = min(R, ADAMW_ROWS)
    assert R % tr == 0 and len(parts) == n_layers and all(p.shape == (N_DEV, R, C) for p in parts)
    c1 = 1.0 - B1 ** STEP
    c2 = 1.0 - B2 ** STEP

    def body(*refs):
        p_refs = refs[:n_layers]
        w_ref, m_ref, v_ref, g_ref, d_ref, nm_ref, nv_ref = refs[n_layers:]
        layer = pl.program_id(0)
        for li, p_ref in enumerate(p_refs):
            @pl.when(layer == li)
            def _(p_ref=p_ref):
                g = p_ref[0].astype(F32)
                for dev in range(1, N_DEV):
                    g = g + p_ref[dev].astype(F32)
                nm = B1 * m_ref[...] + (1.0 - B1) * g
                nv = B2 * v_ref[...] + (1.0 - B2) * jnp.square(g)
                g_ref[...] = g
                nm_ref[...] = nm
                nv_ref[...] = nv
                d_ref[...] = -LR * ((nm / c1) / (jnp.sqrt(nv / c2) + ADAM_EPS) + WD * w_ref[...])

    blk = pl.BlockSpec((None, tr, C), lambda l, i: (l, i, 0))
    return pl.pallas_call(
        body, name=name, grid=(n_layers, R // tr),
        in_specs=[pl.BlockSpec((N_DEV, tr, C), lambda l, i: (0, i, 0))] * n_layers + [blk, blk, blk],
        out_specs=[blk] * 4, out_shape=[jax.ShapeDtypeStruct((n_layers, R, C), F32)] * 4,
        compiler_params=_cparams(("parallel", "parallel")),
    )(*parts, w, m, v)


SMALL = ("mix_norm", "attn_q_gain", "attn_k_gain", "dn_a_log", "dn_dt_bias", "dn_o_gain", "mlp_norm", "ple_norm")
WEIGHTS = ("mix_norm", "attn_w_qkv", "attn_q_gain", "attn_k_gain", "attn_w_o", "dn_w_in", "dn_conv", "dn_a_log",
           "dn_dt_bias", "dn_o_gain", "dn_w_o", "mlp_norm", "w_up", "w_down", "ple_norm", "w_ple", "w_ple_gate")


def _to_rows(flat, multiple):
    n = flat.shape[-1]
    rows = -(-n // (LANE * multiple)) * multiple
    return jnp.pad(flat, [(0, rows * LANE - n)]).reshape(rows, LANE)


def _cols_to_devices(w):
    K, N = w.shape
    return jnp.transpose(w.reshape(K, N_DEV, N // N_DEV), (1, 0, 2))


def _cols_from_devices(g):
    _, K, n = g.shape
    return jnp.transpose(g, (1, 0, 2)).reshape(K, N_DEV * n)


SMALL_ROWS = 96


def _pack_small(vals, loss_rows):
    rows = [_to_rows(vals[n].reshape(-1), SUBLANE) for n in SMALL] + [loss_rows]
    buf = jnp.concatenate(rows, 0)
    assert buf.shape == (SMALL_ROWS, LANE)
    return buf


def _unpack_small(buf, like):
    out, r = {}, 0
    for n in SMALL:
        sz = math.prod(like[n].shape)
        out[n] = buf[r:r + -(-sz // LANE)].reshape(-1)[:sz].reshape(like[n].shape)
        r += -(-sz // (LANE * SUBLANE)) * SUBLANE
    return out


def kernel(x, p, positions, mix_norm, attn_w_qkv, attn_q_gain, attn_k_gain, attn_w_o, dn_w_in, dn_conv, dn_a_log, dn_dt_bias, dn_o_gain, dn_w_o, mlp_norm, w_up, w_down, ple_norm, w_ple, w_ple_gate, loss_target, m_mix_norm, m_attn_w_qkv, m_attn_q_gain, m_attn_k_gain, m_attn_w_o, m_dn_w_in, m_dn_conv, m_dn_a_log, m_dn_dt_bias, m_dn_o_gain, m_dn_w_o, m_mlp_norm, m_w_up, m_w_down, m_ple_norm, m_w_ple, m_w_ple_gate, v_mix_norm, v_attn_w_qkv, v_attn_q_gain, v_attn_k_gain, v_attn_w_o, v_dn_w_in, v_dn_conv, v_dn_a_log, v_dn_dt_bias, v_dn_o_gain, v_dn_w_o, v_mlp_norm, v_w_up, v_w_down, v_ple_norm, v_w_ple, v_w_ple_gate):
    w = dict(mix_norm=mix_norm, attn_w_qkv=attn_w_qkv, attn_q_gain=attn_q_gain, attn_k_gain=attn_k_gain, attn_w_o=attn_w_o,
             dn_w_in=dn_w_in, dn_conv=dn_conv, dn_a_log=dn_a_log, dn_dt_bias=dn_dt_bias, dn_o_gain=dn_o_gain, dn_w_o=dn_w_o,
             mlp_norm=mlp_norm, w_up=w_up, w_down=w_down, ple_norm=ple_norm, w_ple=w_ple, w_ple_gate=w_ple_gate)
    m = dict(mix_norm=m_mix_norm, attn_w_qkv=m_attn_w_qkv, attn_q_gain=m_attn_q_gain, attn_k_gain=m_attn_k_gain,
             attn_w_o=m_attn_w_o, dn_w_in=m_dn_w_in, dn_conv=m_dn_conv, dn_a_log=m_dn_a_log, dn_dt_bias=m_dn_dt_bias,
             dn_o_gain=m_dn_o_gain, dn_w_o=m_dn_w_o, mlp_norm=m_mlp_norm, w_up=m_w_up, w_down=m_w_down,
             ple_norm=m_ple_norm, w_ple=m_w_ple, w_ple_gate=m_w_ple_gate)
    v = dict(mix_norm=v_mix_norm, attn_w_qkv=v_attn_w_qkv, attn_q_gain=v_attn_q_gain, attn_k_gain=v_attn_k_gain,
             attn_w_o=v_attn_w_o, dn_w_in=v_dn_w_in, dn_conv=v_dn_conv, dn_a_log=v_dn_a_log, dn_dt_bias=v_dn_dt_bias,
             dn_o_gain=v_dn_o_gain, dn_w_o=v_dn_w_o, mlp_norm=v_mlp_norm, w_up=v_w_up, w_down=v_w_down,
             ple_norm=v_ple_norm, w_ple=v_w_ple, w_ple_gate=v_w_ple_gate)
    S = x.shape[1]

    bf = lambda a: a.astype(BF16)
    rows_to_devices = lambda t: t.reshape(N_DEV, t.shape[0] // N_DEV, t.shape[1])

    (g_qkv, g_ao), token = _all_gather("gather_attn", [bf(attn_w_qkv[0]), bf(attn_w_o[0])], [(0, None), (1, None)])
    rest_shards = [bf(dn_w_in[0]), bf(dn_w_o[0]), bf(w_up), bf(w_down), bf(w_ple), bf(w_ple_gate), _after(dn_conv[0], token)]
    rest_streams = [(0, None), (1, None), (2, 0), (2, 1), (3, 0), (3, 1), (4, 0), (4, 1), (5, 0), (5, 1), (6, None)]
    rest_arrived, token = _gather_async("gather_rest", rest_shards, rest_streams)
    W = dict(attn_w_qkv=_unpack_cols("unpack_attn_qkv", g_qkv), attn_w_o=_cols_from_devices(g_ao))

    def rest_of_weights(after):
        g_in, g_do, g_up0, g_up1, g_dn0, g_dn1, g_pl0, g_pl1, g_gt0, g_gt1, g_conv = rest_arrived(after)
        rest = dict(
            dn_conv=jnp.transpose(g_conv, (1, 0, 2)).reshape(CONV_W, DN_QKV), dn_w_o=g_do.reshape(DN_WIDTH, D_MODEL),
            w_up=[_cols_from_devices(g_up0), _cols_from_devices(g_up1)],
            w_down=[g_dn0.reshape(D_FF, D_MODEL), g_dn1.reshape(D_FF, D_MODEL)],
            w_ple=[_cols_from_devices(g_pl0), _cols_from_devices(g_pl1)],
            w_ple_gate=[g_gt0.reshape(D_MODEL, D_MODEL), g_gt1.reshape(D_MODEL, D_MODEL)])
        rest["dn_w_qkvz"], rest["dn_w_ab"] = _unpack_dn_in(g_in)
        return rest

    pending = {}

    def mlp_sends(g):
        return [_cols_to_devices(g["w_up"]), rows_to_devices(g["w_down"]), _cols_to_devices(g["w_ple"]),
                rows_to_devices(g["w_ple_gate"])]

    def start(tag, sends):
        pending[tag], token = _exchange_async(f"exchange_{tag}", sends)
        return token

    def send_layer1(g):
        conv_send = jnp.transpose(g["dn_conv"].reshape(CONV_W, N_DEV, DN_QKV // N_DEV), (1, 0, 2))
        return start("layer1", [_pack_dn_in(g["dn_w_qkvz"], g["dn_w_ab"]), conv_send, rows_to_devices(g["dn_w_o"])] + mlp_sends(g))

    def send_mlp0(g):
        return start("mlp0", mlp_sends(g))

    def send_attn(g):
        return start("attn", [_pack_cols("pack_attn_qkv", g["attn_w_qkv"]), _cols_to_devices(g["attn_w_o"])])

    P = dict(mix_norm=_after(mix_norm, token), attn_q_gain=attn_q_gain[0], attn_k_gain=attn_k_gain[0], dn_a_log=dn_a_log[0],
             dn_dt_bias=dn_dt_bias[0], dn_o_gain=dn_o_gain[0], mlp_norm=mlp_norm, ple_norm=ple_norm)

    sq, dx0, small_g = _local_step(x[0], p[:, 0], positions.reshape(S, 1), loss_target[0], W, P,
                                   rest_of_weights, send_layer1, send_mlp0, send_attn)

    r_in, r_conv, r_do, r_up1, r_dn1, r_pl1, r_gt1 = pending["layer1"](dx0)
    r_up0, r_dn0, r_pl0, r_gt0 = pending["mlp0"](dx0)
    r_qkv, r_ao = pending["attn"](dx0)
    big = {}
    for n, parts in (("attn_w_qkv", [r_qkv]), ("attn_w_o", [r_ao]), ("dn_w_in", [r_in]), ("dn_conv", [r_conv]),
                     ("dn_w_o", [r_do]), ("w_up", [r_up0, r_up1]), ("w_down", [r_dn0, r_dn1]),
                     ("w_ple", [r_pl0, r_pl1]), ("w_ple_gate", [r_gt0, r_gt1])):
        big[n] = _adamw(f"adamw_{n}", parts, w[n], m[n], v[n])

    loss_rows = jnp.pad((0.5 / D_MODEL) * jnp.sum(sq, axis=1, keepdims=True), ((0, SUBLANE - 1), (0, LANE - 1)))
    small_like = {n: w[n] for n in SMALL}
    parts_s = _all_gather("gather_small", [_pack_small(small_g, loss_rows)], [(0, None)])[0][0]
    zero_rows = jnp.zeros((SUBLANE, LANE), F32)
    small = _adamw("adamw_small", [parts_s], _pack_small(w, zero_rows)[None], _pack_small(m, zero_rows)[None],
                   _pack_small(v, zero_rows)[None])
    loss = small[0][0, SMALL_ROWS - SUBLANE, 0]
    small = [_unpack_small(b[0], small_like) for b in small]

    outs = [loss, dx0[None]]
    for k in range(4):
        for n in WEIGHTS:
            outs.append(small[k][n] if n in SMALL else big[n][k])
    return tuple(outs)
```

```python
import functools
import math

import jax
import jax.numpy as jnp
from jax import lax
from jax.experimental import pallas as pl
from jax.experimental.pallas import tpu as pltpu

F32 = jnp.float32
BF16 = jnp.bfloat16
HIGHEST = lax.Precision.HIGHEST

N_DEV = 8
D_MODEL = 1024
EPS = 1e-6
SWA_GROUPS = ((128, 1), (512, 4), (2048, 16))
A_HEADS = 8
A_HEAD_DIM = 64
A_WIDTH = A_HEADS * A_HEAD_DIM
A_QKV = 3 * 3 * A_WIDTH
ROPE_DIM = 16
ROPE_HALF = 8
ROPE_THETA = 500000.0
BAND = 128
DN_HEADS = 8
DN_DIM = 128
DN_WIDTH = DN_HEADS * DN_DIM
CONV_W = 4
CHUNK = 64
D_FF = 4 * D_MODEL
PLE_DIM = 256
LR, B1, B2, ADAM_EPS, WD, STEP = 0.001, 0.9, 0.999, 1e-08, 0.01, 10

VMEM_LIMIT = 56 * 1024 * 1024
MXU_TILE = 1024
MM_SLAB = 256
LANE = 128
SUBLANE = 8


def _cparams(sem):
    return pltpu.CompilerParams(dimension_semantics=sem, vmem_limit_bytes=VMEM_LIMIT)


def _tile(n, pref):
    if n <= pref:
        return n
    t = (pref // LANE) * LANE
    while t >= LANE:
        if n % t == 0:
            return t
        t -= LANE
    raise ValueError(f"no tile for {n}")


def _dot(a, b, ca=1, cb=0, precision=None):
    return lax.dot_general(a, b, (((ca,), (cb,)), ((), ())), precision=precision,
                           preferred_element_type=F32)


def _bdot(a, b, ca=1, cb=0):
    return _dot(a.astype(BF16), b.astype(BF16), ca, cb)


def _mm(name, a, b, *, ta=False, tb=False, epilogue=None, extras=(), out_dtypes=(F32,), n_colsums=0,
        tm_pref=MXU_TILE, tn_pref=1536, tk_pref=MXU_TILE):
    M, K = (a.shape[1], a.shape[0]) if ta else a.shape
    N = b.shape[0] if tb else b.shape[1]
    assert (b.shape[1] if tb else b.shape[0]) == K
    tm, tn, tk = _tile(M, tm_pref), _tile(N, tn_pref), _tile(K, tk_pref)
    nk = K // tk
    n_out = len(out_dtypes)
    n_ext = len(extras)
    assert n_colsums == 0 or tn == N
    sub = min(tm, MM_SLAB)

    def body(*refs):
        a_ref, b_ref = refs[0], refs[1]
        ext = refs[2:2 + n_ext]
        outs = refs[2 + n_ext:2 + n_ext + n_out]
        sums = refs[2 + n_ext + n_out:2 + n_ext + n_out + n_colsums]
        row_tile, k = pl.program_id(0), pl.program_id(2)
        slabs = [slice(s * sub, (s + 1) * sub) for s in range(tm // sub)]

        def product(rows):
            return _bdot(a_ref[:, rows] if ta else a_ref[rows, :], b_ref[...], 0 if ta else 1, 1 if tb else 0)

        def finish(results):
            col_rows = []
            for rows, r in zip(slabs, results):
                res = (r,) if epilogue is None else epilogue(r, *[e[...] if e.shape[0] == 1 else e[rows, :] for e in ext])
                for o, v in zip(outs, res):
                    o[rows, :] = v.astype(o.dtype)
                col_rows.append(res[n_out:])
            for n, o in enumerate(sums):
                v = functools.reduce(lambda x, y: x + y, [c[n] for c in col_rows])

                @pl.when(row_tile == 0)
                def _(o=o, v=v):
                    o[...] = v

                @pl.when(row_tile > 0)
                def _(o=o, v=v):
                    o[...] += v

        if nk == 1:
            finish([product(rows) for rows in slabs])
            return
        acc = refs[-1]

        @pl.when(k == 0)
        def _():
            acc[...] = jnp.zeros_like(acc)

        for rows in slabs:
            acc[rows, :] += product(rows)

        @pl.when(k == nk - 1)
        def _():
            finish([acc[rows, :] for rows in slabs])

    a_spec = pl.BlockSpec((tk, tm), lambda i, j, k: (k, i)) if ta else pl.BlockSpec((tm, tk), lambda i, j, k: (i, k))
    b_spec = pl.BlockSpec((tn, tk), lambda i, j, k: (j, k)) if tb else pl.BlockSpec((tk, tn), lambda i, j, k: (k, j))
    ext_specs = []
    for e in extras:
        if e.shape[0] == 1 and M != 1:
            ext_specs.append(pl.BlockSpec((1, tn), lambda i, j, k: (0, j)))
        else:
            ext_specs.append(pl.BlockSpec((tm, tn), lambda i, j, k: (i, j)))
    out = pl.pallas_call(
        body, name=name,
        grid=(M // tm, N // tn, nk),
        in_specs=[a_spec, b_spec] + ext_specs,
        out_specs=[pl.BlockSpec((tm, tn), lambda i, j, k: (i, j)) for _ in range(n_out)]
        + [pl.BlockSpec((1, tn), lambda i, j, k: (0, 0)) for _ in range(n_colsums)],
        out_shape=[jax.ShapeDtypeStruct((M, N), dt) for dt in out_dtypes]
        + [jax.ShapeDtypeStruct((1, N), F32) for _ in range(n_colsums)],
        scratch_shapes=[pltpu.VMEM((tm, tn), F32)] if nk > 1 else [],
        compiler_params=_cparams(("arbitrary" if n_colsums else "parallel", "parallel", "arbitrary")),
    )(a, b, *extras)
    return out[0] if len(out) == 1 else tuple(out)


def _perm_matrices(tr, d):
    import numpy as np
    old = np.arange(tr)
    p = np.zeros((tr, tr), np.float32)
    p[(old % d) * (tr // d) + old // d, old] = 1.0
    return jnp.asarray(p, BF16), jnp.asarray(p.T, BF16)


def _permute(p, x):
    if x.dtype == BF16:
        return _dot(p, x)
    hi = x.astype(BF16)
    rest = x - hi.astype(F32)
    mid = rest.astype(BF16)
    lo = (rest - mid.astype(F32)).astype(BF16)
    return _dot(p, hi) + _dot(p, mid) + _dot(p, lo)


def _rows(name, fn, ins, outs, *, tr, accs=()):
    ins = [(e[0], e[1]) + (e[2] if len(e) > 2 else (0, e[0].shape[-1])) for e in ins]
    outs = [tuple(o) + (0,) * (3 - len(o)) for o in outs]
    n_rows = next(e[0].shape[0] if e[1] == "row" else e[0].shape[0] * e[0].shape[1] for e in ins if e[1] in ("row", "res"))
    assert n_rows % tr == 0 and tr % SUBLANE == 0
    steps = n_rows // tr
    t8 = tr // SUBLANE
    n8 = n_rows // SUBLANE
    dils = sorted({e[0].shape[0] for e in ins if e[1] == "res" and e[0].shape[0] > 1} | {o[2] for o in outs if o[2] > 1})
    perms = [m for d in dils for m in _perm_matrices(tr, d)]
    ins = ins + [(m, "full", 0, tr) for m in perms]
    n_in, n_out, n_acc = len(ins), len(outs), len(accs)

    def body(*refs):
        i = pl.program_id(0)
        to_res = {d: refs[n_in - len(perms) + 2 * j][...] for j, d in enumerate(dils)}
        to_tok = {d: refs[n_in - len(perms) + 2 * j + 1][...] for j, d in enumerate(dils)}
        tiles = []
        for r, e in zip(refs[:n_in - len(perms)], ins):
            d = e[0].shape[0] if e[1] == "res" else 0
            if d == 0:
                tiles.append(r[...])
            elif d == 1:
                tiles.append(r[0])
            else:
                tiles.append(_permute(to_tok[d], jnp.concatenate([r[j] for j in range(d)], axis=0)))
        vals = fn(i, steps, *tiles)
        if not isinstance(vals, (tuple, list)):
            vals = (vals,)
        assert len(vals) == n_out + n_acc
        for o, v, (_, dt, d) in zip(refs[n_in:n_in + n_out], vals[:n_out], outs):
            if d == 0:
                o[...] = v.astype(o.dtype)
            elif d == 1:
                o[0] = v.astype(o.dtype)
            else:
                y = _permute(to_res[d], v.astype(dt))
                for j in range(d):
                    o[j] = y[j * (tr // d):(j + 1) * (tr // d)].astype(o.dtype)
        if n_acc:
            acc_refs = refs[n_in + n_out:]

            @pl.when(i == 0)
            def _():
                for r in acc_refs:
                    r[...] = jnp.zeros_like(r)

            for r, v in zip(acc_refs, vals[n_out:]):
                r[...] += v.astype(r.dtype)

    in_specs = []
    for a, kind, cb, c in ins:
        if kind == "row":
            in_specs.append(pl.BlockSpec((tr, c), lambda i, cb=cb: (i, cb)))
        elif kind == "full":
            in_specs.append(pl.BlockSpec(a.shape, lambda i, z=(0,) * a.ndim: z))
        elif kind == "prev8":
            in_specs.append(pl.BlockSpec((SUBLANE, c), lambda i, cb=cb: (jnp.maximum(i * t8 - 1, 0), cb)))
        elif kind == "next8":
            in_specs.append(pl.BlockSpec((SUBLANE, c), lambda i, cb=cb: (jnp.minimum((i + 1) * t8, n8 - 1), cb)))
        elif kind == "res":
            d = a.shape[0]
            in_specs.append(pl.BlockSpec((d, tr // d, a.shape[2]), lambda i: (0, i, 0)))
        else:
            raise ValueError(kind)
    out_specs = [pl.BlockSpec((tr, c), lambda i: (i, 0)) if d == 0 else pl.BlockSpec((d, tr // d, c), lambda i: (0, i, 0))
                 for c, _, d in outs]
    out_specs += [pl.BlockSpec(s, lambda i, z=(0,) * len(s): z) for s, _ in accs]
    out_shape = [jax.ShapeDtypeStruct((n_rows, c) if d == 0 else (d, n_rows // d, c), dt) for c, dt, d in outs]
    out_shape += [jax.ShapeDtypeStruct(s, dt) for s, dt in accs]
    res = pl.pallas_call(
        body, name=name, grid=(steps,), in_specs=in_specs, out_specs=out_specs, out_shape=out_shape,
        compiler_params=_cparams(("arbitrary",) if n_acc else ("parallel",)),
    )(*[e[0] for e in ins])
    return res[0] if len(res) == 1 else tuple(res)


def _colsum(x):
    return jnp.sum(x, axis=0, keepdims=True)


def _sum_all(x):
    return jnp.sum(jnp.sum(x, axis=1, keepdims=True), axis=0, keepdims=True)


def _rmsnorm_fwd(name, x, gain):
    def fn(i, n, xt, g):
        r = lax.rsqrt(jnp.mean(xt * xt, axis=-1, keepdims=True) + EPS)
        return (xt * r * g,)
    return _rows(name, fn, [(x, "row"), (gain, "full")], [(x.shape[1], BF16)], tr=512)


FUSED_ROWS = 1024


def _res_norm(acc, res, g):
    x = res + acc
    return x, x * lax.rsqrt(jnp.mean(x * x, axis=-1, keepdims=True) + EPS) * g


def _norm_bwd(dh, x, g, dres):
    r = lax.rsqrt(jnp.mean(x * x, axis=-1, keepdims=True) + EPS)
    xh = x * r
    dxn = dh * g
    dx = dres + r * (dxn - xh * jnp.mean(dxn * xh, axis=-1, keepdims=True))
    return dx, _colsum(dh * xh)


def _norm_bwd_2(dh, x, g, dres):
    dx, dg = _norm_bwd(dh, x, g, dres)
    return dx, dx, dg


def _head_consts():
    import numpy as np
    e = np.arange(A_WIDTH) % A_HEAD_DIM
    inv = (np.float32(ROPE_THETA) ** (-np.arange(0, ROPE_DIM, 2, dtype=np.float32) / np.float32(ROPE_DIM))).astype(np.float32)
    c = np.zeros((8, A_WIDTH), np.float32)
    c[0] = np.where(e < ROPE_DIM, inv[e % ROPE_HALF], 0.0)
    c[1] = np.where(e < ROPE_HALF, -1.0, np.where(e < ROPE_DIM, 1.0, 0.0))
    c[2] = (e < ROPE_HALF).astype(np.float32)
    c[3] = (e < ROPE_DIM).astype(np.float32)
    return jnp.asarray(c)


def _block_diag(scale):
    import numpy as np
    h = np.arange(A_WIDTH) // A_HEAD_DIM
    return jnp.asarray((h[:, None] == h[None, :]).astype(np.float32) * scale, dtype=BF16)


def _seg_sum(x, bd):
    return _dot(x.astype(BF16), bd)


def _rope_tables(positions, consts):
    def fn(i, n, pos, c):
        ang = pos.astype(F32) * c[0:1, :LANE]
        return jnp.cos(ang), jnp.sin(ang) * c[1:2, :LANE]
    return _rows("rope_tables", fn, [(positions, "row"), (consts, "full")], [(LANE, F32), (LANE, F32)], tr=512)


def _rope_wide(t):
    return jnp.concatenate([t] * (A_WIDTH // LANE), axis=1)


def _rope_apply(y, ct, st, low):
    rolled = jnp.where(low, pltpu.roll(y, A_WIDTH - ROPE_HALF, 1), pltpu.roll(y, ROPE_HALF, 1))
    return y * ct + rolled * st


def _rope_apply_bwd(dout, ct, st, low, in16):
    t = dout * st
    back = jnp.where(low, pltpu.roll(t, A_WIDTH - ROPE_HALF, 1), jnp.where(in16, pltpu.roll(t, ROPE_HALF, 1), 0.0))
    return dout * ct + back


def _attn_prep(qkv, gains, ct, st, consts, bd):
    def fn(i, n, t, g, c_t, s_t, c, b):
        low = c[2:3, :] > 0.5
        c_t, s_t = _rope_wide(c_t), _rope_wide(s_t)
        groups = []
        for grp in range(3):
            cols = []
            for which in range(3):
                off = (grp * 3 + which) * A_WIDTH
                x = t[:, off:off + A_WIDTH].astype(F32)
                if which == 2:
                    cols.append(x.astype(BF16))
                    continue
                r = lax.rsqrt(_seg_sum(x * x, b) + EPS)
                y = x * r * g[grp * 2 + which:grp * 2 + which + 1, :]
                cols.append(_rope_apply(y, c_t, s_t, low).astype(BF16))
            groups.append(jnp.concatenate(cols, axis=1))
        return tuple(groups)
    return _rows("attn_prep", fn, [(qkv, "row"), (gains, "full"), (ct, "row"), (st, "row"), (consts, "full"), (bd, "full")],
                 [(3 * A_WIDTH, BF16, d) for _, d in SWA_GROUPS], tr=256)


def _band_mask(n):
    row = lax.broadcasted_iota(jnp.int32, (BAND, 2 * BAND), 0)
    col = lax.broadcasted_iota(jnp.int32, (BAND, 2 * BAND), 1)
    dist = row + BAND - col
    return (dist >= 0) & (dist <= BAND) & ((col >= BAND) | (n > 0))


def _attn_fwd(qkvn, grp):
    d, L, _ = qkvn.shape
    nblk = L // BAND
    assert L % BAND == 0 and d == SWA_GROUPS[grp][1]

    def body(q_ref, kc_ref, kp_ref, vc_ref, vp_ref, o_ref, lse_ref):
        n = pl.program_id(1)
        valid = _band_mask(n)
        first = lax.broadcasted_iota(jnp.int32, (BAND, LANE), 1) < A_HEAD_DIM
        pairs = [slice(pr * LANE, (pr + 1) * LANE) for pr in range(A_WIDTH // LANE)]
        halves = (first, jnp.logical_not(first))
        qps = [q_ref[:, sl] for sl in pairs]
        kcats = [jnp.concatenate([kp_ref[:, sl], kc_ref[:, sl]], axis=0) for sl in pairs]
        vcats = [jnp.concatenate([vp_ref[:, sl], vc_ref[:, sl]], axis=0) for sl in pairs]
        heads = [(pr, m) for pr in range(len(pairs)) for m in halves]
        ss = [_dot(jnp.where(m, qps[pr], jnp.zeros_like(qps[pr])), kcats[pr], 1, 1) for pr, m in heads]
        ps, lses = [], []
        for s in ss:
            s = jnp.where(valid, s * (A_HEAD_DIM ** -0.5), -1e30)
            mx = jnp.max(s, axis=-1, keepdims=True)
            e = jnp.exp(s - mx)
            l = jnp.sum(e, axis=-1, keepdims=True)
            ps.append((e / l).astype(BF16))
            lses.append(mx + jnp.log(l))
        os_ = [_dot(p, vcats[pr]) for p, (pr, _) in zip(ps, heads)]
        o_ref[...] = jnp.concatenate([jnp.where(first, os_[2 * pr], os_[2 * pr + 1]) for pr in range(len(pairs))], axis=1)
        lse_ref[...] = jnp.concatenate([jnp.where(first, lses[2 * pr], lses[2 * pr + 1]) for pr in range(len(pairs))], axis=1)

    blk = (None, BAND, A_WIDTH)
    return pl.pallas_call(
        body, name=f"attn_fwd_g{grp}", grid=(d, nblk),
        in_specs=[pl.BlockSpec(blk, lambda r, n: (r, n, 0)),
                  pl.BlockSpec(blk, lambda r, n: (r, n, 1)),
                  pl.BlockSpec(blk, lambda r, n: (r, jnp.maximum(n - 1, 0), 1)),
                  pl.BlockSpec(blk, lambda r, n: (r, n, 2)),
                  pl.BlockSpec(blk, lambda r, n: (r, jnp.maximum(n - 1, 0), 2))],
        out_specs=[pl.BlockSpec(blk, lambda r, n: (r, n, 0)), pl.BlockSpec(blk, lambda r, n: (r, n, 0))],
        out_shape=[jax.ShapeDtypeStruct((d, L, A_WIDTH), F32)] * 2,
        compiler_params=_cparams(("parallel", "parallel")),
    )(qkvn, qkvn, qkvn, qkvn, qkvn)


def _merge_weights(l0, l1, l2):
    mx = jnp.maximum(jnp.maximum(l0, l1), l2)
    e0, e1, e2 = jnp.exp(l0 - mx), jnp.exp(l1 - mx), jnp.exp(l2 - mx)
    inv = 1.0 / (e0 + e1 + e2)
    return e0 * inv, e1 * inv, e2 * inv


def _attn_merge(os_, lses):
    def fn(i, n, o0, o1, o2, l0, l1, l2):
        w0, w1, w2 = _merge_weights(l0, l1, l2)
        return (w0 * o0 + w1 * o1 + w2 * o2,)
    ins = [(a, "res") for a in (*os_, *lses)]
    return _rows("attn_merge", fn, ins, [(A_WIDTH, BF16)], tr=256)


def _attn_merge_bwd(do, os_, lses, bd1):
    def fn(i, n, dot_, o0, o1, o2, l0, l1, l2, b):
        w0, w1, w2 = _merge_weights(l0, l1, l2)
        o = w0 * o0 + w1 * o1 + w2 * o2
        dsum = _seg_sum(dot_ * o, b)
        return (w0 * dot_, w1 * dot_, w2 * dot_, -w0 * dsum, -w1 * dsum, -w2 * dsum)
    ins = [(do, "row")] + [(a, "res") for a in (*os_, *lses)] + [(bd1, "full")]
    res = _rows("attn_merge_bwd", fn, ins, [(A_WIDTH, dt, d) for dt in (BF16, F32) for _, d in SWA_GROUPS], tr=256)
    return res[:3], res[3:]


def _lane_pick(x, lane_idx, lane):
    return jnp.sum(jnp.where(lane_idx == lane, x, 0.0), axis=-1, keepdims=True)


def _attn_bwd(qkvn, grp, do_g, lse, c_g):
    d, L, _ = qkvn.shape
    nblk = L // BAND

    def body(q_ref, kc_ref, kp_ref, vc_ref, vp_ref, do_ref, lse_ref, c_ref, dq_ref, dk_ref, dv_ref, ck, cv_):
        n = pl.program_id(1)

        @pl.when(n == 0)
        def _():
            ck[...] = jnp.zeros_like(ck)
            cv_[...] = jnp.zeros_like(cv_)

        @pl.when(n < nblk)
        def _():
            valid = _band_mask(n)
            lane = lax.broadcasted_iota(jnp.int32, (BAND, LANE), 1)
            first = lane < A_HEAD_DIM
            lane2 = lax.broadcasted_iota(jnp.int32, (2 * BAND, LANE), 1) < A_HEAD_DIM
            pairs = [slice(pr * LANE, (pr + 1) * LANE) for pr in range(A_WIDTH // LANE)]
            halves = (first, jnp.logical_not(first))
            qps = [q_ref[:, sl] for sl in pairs]
            dops = [do_ref[:, sl] for sl in pairs]
            kcats = [jnp.concatenate([kp_ref[:, sl], kc_ref[:, sl]], axis=0) for sl in pairs]
            vcats = [jnp.concatenate([vp_ref[:, sl], vc_ref[:, sl]], axis=0) for sl in pairs]
            heads = [(pr, hh) for pr in range(len(pairs)) for hh in range(2)]
            zero = jnp.zeros_like(qps[0])
            ss = [_dot(jnp.where(halves[hh], qps[pr], zero), kcats[pr], 1, 1) for pr, hh in heads]
            dps = [_dot(jnp.where(halves[hh], dops[pr], zero), vcats[pr], 1, 1) for pr, hh in heads]
            dss, pbs = [], []
            for (pr, hh), s, dp in zip(heads, ss, dps):
                lse_h = _lane_pick(lse_ref[:, pairs[pr]], lane, hh * A_HEAD_DIM)
                c_h = _lane_pick(c_ref[:, pairs[pr]], lane, hh * A_HEAD_DIM)
                p = jnp.where(valid, jnp.exp(s * (A_HEAD_DIM ** -0.5) - lse_h), 0.0)
                dss.append((p * (dp + c_h) * (A_HEAD_DIM ** -0.5)).astype(BF16))
                pbs.append(p.astype(BF16))
            dqs = [_dot(ds, kcats[pr]) for ds, (pr, _) in zip(dss, heads)]
            dks = [_dot(ds, qps[pr], 0, 0) for ds, (pr, _) in zip(dss, heads)]
            dvs = [_dot(pb, dops[pr], 0, 0) for pb, (pr, _) in zip(pbs, heads)]
            for pr, sl in enumerate(pairs):
                dq_ref[:, sl] = jnp.where(first, dqs[2 * pr], dqs[2 * pr + 1])
                dkc = jnp.where(lane2, dks[2 * pr], dks[2 * pr + 1])
                dvc = jnp.where(lane2, dvs[2 * pr], dvs[2 * pr + 1])
                dk_ref[:, sl] = ck[:, sl] + dkc[:BAND]
                dv_ref[:, sl] = cv_[:, sl] + dvc[:BAND]
                ck[:, sl] = dkc[BAND:]
                cv_[:, sl] = dvc[BAND:]

        @pl.when(n == nblk)
        def _():
            dk_ref[...] = ck[...]
            dv_ref[...] = cv_[...]

    blk = (None, BAND, A_WIDTH)
    last = nblk - 1
    qn = lambda n: jnp.minimum(n, last)
    pn = lambda n: jnp.clip(n - 1, 0, last)
    return tuple(pl.pallas_call(
        body, name=f"attn_bwd_g{grp}", grid=(d, nblk + 1),
        in_specs=[pl.BlockSpec(blk, lambda r, n: (r, qn(n), 0)),
                  pl.BlockSpec(blk, lambda r, n: (r, qn(n), 1)),
                  pl.BlockSpec(blk, lambda r, n: (r, pn(n), 1)),
                  pl.BlockSpec(blk, lambda r, n: (r, qn(n), 2)),
                  pl.BlockSpec(blk, lambda r, n: (r, pn(n), 2)),
                  pl.BlockSpec(blk, lambda r, n: (r, qn(n), 0)),
                  pl.BlockSpec(blk, lambda r, n: (r, qn(n), 0)),
                  pl.BlockSpec(blk, lambda r, n: (r, qn(n), 0))],
        out_specs=[pl.BlockSpec(blk, lambda r, n: (r, qn(n), 0)),
                   pl.BlockSpec(blk, lambda r, n: (r, pn(n), 0)),
                   pl.BlockSpec(blk, lambda r, n: (r, pn(n), 0))],
        out_shape=[jax.ShapeDtypeStruct((d, L, A_WIDTH), F32)] * 3,
        scratch_shapes=[pltpu.VMEM((BAND, A_WIDTH), F32), pltpu.VMEM((BAND, A_WIDTH), F32)],
        compiler_params=_cparams(("parallel", "arbitrary")),
    )(qkvn, qkvn, qkvn, qkvn, qkvn, do_g, lse, c_g))


def _attn_prep_bwd(qkv, grads, gains, ct, st, consts, bd):
    def fn(i, n, t, g, c_t, s_t, c, b, *gr):
        low = c[2:3, :] > 0.5
        in16 = c[3:4, :] > 0.5
        c_t, s_t = _rope_wide(c_t), _rope_wide(s_t)
        cols, dgs = [], []
        for grp in range(3):
            for which in range(3):
                dout = gr[grp * 3 + which]
                if which == 2:
                    cols.append(dout.astype(BF16))
                    continue
                off = (grp * 3 + which) * A_WIDTH
                x = t[:, off:off + A_WIDTH].astype(F32)
                gain = g[grp * 2 + which:grp * 2 + which + 1, :]
                r = lax.rsqrt(_seg_sum(x * x, b) + EPS)
                xh = x * r
                dy = _rope_apply_bwd(dout, c_t, s_t, low, in16)
                dyn = dy * gain
                dx = r * (dyn - xh * _seg_sum(dyn * xh, b))
                cols.append(dx.astype(BF16))
                dgs.append(_colsum(dy * xh))
        return (jnp.concatenate(cols, axis=1), *dgs)
    ins = [(qkv, "row"), (gains, "full"), (ct, "row"), (st, "row"), (consts, "full"), (bd, "full")] + [(a, "res") for a in grads]
    res = _rows("attn_prep_bwd", fn, ins, [(A_QKV, BF16)], tr=128, accs=[((1, A_WIDTH), F32)] * 6)
    return res[0], res[1:]


DN_QKV = 3 * DN_WIDTH
DN_QKVZ = DN_QKV + DN_WIDTH


def _sigmoid(x):
    return jax.nn.sigmoid(x)


def _softplus(x):
    return jnp.maximum(x, 0.0) + jnp.log(1.0 + jnp.exp(-jnp.abs(x)))


def _conv_taps(xs, w, tr):
    acc = None
    for j in range(CONV_W):
        sh = CONV_W - 1 - j
        term = (pltpu.roll(xs, sh, 0) if sh else xs)[SUBLANE:] * w[j:j + 1, :]
        acc = term if acc is None else acc + term
    return acc


def _dn_prep(qkvz, ab, convw, alog_row, dt_row):
    tr = 256

    def fn(i, n, x, xp, abt, w, al, dt):
        xp = jnp.where(i > 0, xp, 0.0)
        u = _conv_taps(jnp.concatenate([xp, x], axis=0), w, tr)
        y = u * _sigmoid(u)
        qs, ks = [], []
        for h in range(DN_HEADS):
            for dst, base, sc in ((qs, 0, DN_DIM ** -0.5), (ks, DN_WIDTH, 1.0)):
                seg = y[:, base + h * DN_DIM:base + (h + 1) * DN_DIM]
                dst.append(seg * (lax.rsqrt(jnp.sum(seg * seg, axis=-1, keepdims=True) + EPS) * sc))
        lane = lax.broadcasted_iota(jnp.int32, abt.shape, 1)
        g = -jnp.exp(al) * _softplus(abt + dt)
        gb = jnp.where(lane < DN_HEADS, g, jnp.where(lane < 2 * DN_HEADS, _sigmoid(abt), 0.0))
        return u, jnp.concatenate(qs, axis=1), jnp.concatenate(ks, axis=1), y[:, 2 * DN_WIDTH:], gb

    ins = [(qkvz, "row", (0, DN_QKV)), (qkvz, "prev8", (0, DN_QKV)), (ab, "row"), (convw, "full"),
           (alog_row, "full"), (dt_row, "full")]
    return _rows("dn_prep", fn, ins, [(DN_QKV, BF16), (DN_WIDTH, F32), (DN_WIDTH, F32), (DN_WIDTH, F32), (LANE, F32)], tr=tr)


def _tri_masks():
    row = lax.broadcasted_iota(jnp.int32, (CHUNK, CHUNK), 0)
    col = lax.broadcasted_iota(jnp.int32, (CHUNK, CHUNK), 1)
    return row >= col, row > col, row == col


def _heads(fn, *lists):
    return [fn(*xs) for xs in zip(*lists)]


def _split(x):
    hi = x.astype(BF16)
    return hi, (x - hi.astype(F32)).astype(BF16)


def _dot3(a, b, ca=1, cb=0):
    (ah, al), (bh, bl) = a, b
    return _dot(ah, bh, ca, cb) + (_dot(ah, bl, ca, cb) + _dot(al, bh, ca, cb))


SPLIT_STEPS = 3


def _unit_lower_inverse(a_list, eye):
    ts = [eye - a for a in a_list]
    parts = [_split(a) for a in a_list]
    for step in range(5):
        if step < SPLIT_STEPS:
            parts = [_split(_dot3(p, p)) for p in parts]
            ts = [t + _dot3(_split(t), p) for t, p in zip(ts, parts)]
        else:
            parts = [(_dot(p[0], p[0]).astype(BF16), None) for p in parts]
            ts = [t + _dot(t.astype(BF16), p[0]) for t, p in zip(ts, parts)]
    return ts


def _dn_terms(qs, ks, vs, gb, solved=None):
    lower, strict, diag = _tri_masks()
    lane = lax.broadcasted_iota(jnp.int32, (CHUNK, LANE), 1)
    is_last = lax.broadcasted_iota(jnp.int32, (CHUNK, 1), 0) == CHUNK - 1
    hs = range(DN_HEADS)
    gc = _dot(lower.astype(F32), gb, precision=HIGHEST)
    gct = jnp.transpose(gc)
    bcol = [_lane_pick(gb, lane, DN_HEADS + h) for h in hs]
    gcol = [_lane_pick(gc, lane, h) for h in hs]
    glast = [jnp.sum(jnp.where(is_last, g, 0.0), axis=0, keepdims=True) for g in gcol]
    decay = [jnp.exp(jnp.where(lower, gcol[h] - gct[h:h + 1, :], -1e30)) for h in hs]
    kb = _heads(lambda k, b: k * b, ks, bcol)
    both = _heads(lambda q, x, k: _bdot(jnp.concatenate([q, x], axis=0), k, 1, 1), qs, kb, ks)
    qk = [x[:CHUNK] for x in both]
    kk = [x[CHUNK:] for x in both]
    a = _heads(lambda x, d: jnp.where(strict, x * d, 0.0), kk, decay)
    eg = [jnp.exp(g) for g in gcol]
    egl = _heads(lambda gl, g: jnp.exp(gl - g), glast, gcol)
    rhs_w = _heads(lambda x, e: x * e, kb, eg)
    if solved is None:
        t_full = _unit_lower_inverse(a, diag.astype(F32))
        t = [_split(x) for x in t_full]
        uw = _heads(lambda tt, v, b, r: _dot3(tt, _split(jnp.concatenate([v * b, r], axis=1))), t, vs, bcol, rhs_w)
        u = [x[:, :DN_DIM] for x in uw]
        w = [x[:, DN_DIM:] for x in uw]
    else:
        t_full, u, w = solved
        t = [_split(x) for x in t_full]
    return dict(bcol=bcol, decay=decay, kb=kb, a=a, t=t, t_full=t_full, eg=eg, egl=egl, rhs_w=rhs_w, u=u, w=w,
                attn=_heads(lambda x, d: x * d, qk, decay), q_dec=_heads(lambda q, e: q * e, qs, eg),
                k_dec=_heads(lambda k, e: k * e, ks, egl), c_dec=[jnp.exp(g) for g in glast],
                lower=lower, strict=strict, lane=lane, is_last=is_last)


def _head_slices(ref):
    return [ref[:, h * DN_DIM:(h + 1) * DN_DIM] for h in range(DN_HEADS)]


def _dn_chunk_fwd(q, k, v, gb):
    S = q.shape[0]
    N = S // CHUNK

    def body(q_ref, k_ref, v_ref, gb_ref, o_ref, st_ref, t_ref, u_ref, w_ref, state):
        @pl.when(pl.program_id(0) == 0)
        def _():
            state[...] = jnp.zeros_like(state)

        f = _dn_terms(_head_slices(q_ref), _head_slices(k_ref), _head_slices(v_ref), gb_ref[...])
        s = [state[h] for h in range(DN_HEADS)]
        for h in range(DN_HEADS):
            st_ref[0, h] = s[h]
            t_ref[0, h] = f["t_full"][h]
            u_ref[:, h * DN_DIM:(h + 1) * DN_DIM] = f["u"][h]
            w_ref[:, h * DN_DIM:(h + 1) * DN_DIM] = f["w"][h]
        sb = [x.astype(BF16) for x in s]
        v_new = _heads(lambda u, w, x: u - _bdot(w, x), f["u"], f["w"], sb)
        o = _heads(lambda qd, x, at, vn: _bdot(qd, x) + _bdot(at, vn), f["q_dec"], sb, f["attn"], v_new)
        new_s = _heads(lambda x, c, kd, vn: x * c + _bdot(kd, vn, 0, 0), s, f["c_dec"], f["k_dec"], v_new)
        for h in range(DN_HEADS):
            o_ref[:, h * DN_DIM:(h + 1) * DN_DIM] = o[h]
            state[h] = new_s[h]

    blk = pl.BlockSpec((CHUNK, DN_WIDTH), lambda n: (n, 0))
    st_blk = pl.BlockSpec((1, DN_HEADS, DN_DIM, DN_DIM), lambda n: (n, 0, 0, 0))
    t_blk = pl.BlockSpec((1, DN_HEADS, CHUNK, CHUNK), lambda n: (n, 0, 0, 0))
    wide = jax.ShapeDtypeStruct((S, DN_WIDTH), F32)
    o, states, t, u, w = pl.pallas_call(
        body, name="dn_chunk_fwd", grid=(N,),
        in_specs=[blk, blk, blk, pl.BlockSpec((CHUNK, LANE), lambda n: (n, 0))],
        out_specs=[blk, st_blk, t_blk, blk, blk],
        out_shape=[wide, jax.ShapeDtypeStruct((N, DN_HEADS, DN_DIM, DN_DIM), F32),
                   jax.ShapeDtypeStruct((N, DN_HEADS, CHUNK, CHUNK), F32), wide, wide],
        scratch_shapes=[pltpu.VMEM((DN_HEADS, DN_DIM, DN_DIM), F32)],
        compiler_params=_cparams(("arbitrary",)),
    )(q, k, v, gb)
    return o, (states, t, u, w)


def _dn_chunk_bwd(q, k, v, gb, saved, do):
    S = q.shape[0]
    N = S // CHUNK
    states, t_saved, u_saved, w_saved = saved

    def body(q_ref, k_ref, v_ref, gb_ref, st_ref, t_ref, u_ref, w_ref, do_ref, dq_ref, dk_ref, dv_ref, dgb_ref, dstate):
        @pl.when(pl.program_id(0) == 0)
        def _():
            dstate[...] = jnp.zeros_like(dstate)

        hs = range(DN_HEADS)
        qs, ks, vs, dos = (_head_slices(r) for r in (q_ref, k_ref, v_ref, do_ref))
        f = _dn_terms(qs, ks, vs, gb_ref[...], ([t_ref[0, h] for h in hs], _head_slices(u_ref), _head_slices(w_ref)))
        lane, is_last = f["lane"], f["is_last"]
        rowsum = lambda x: jnp.sum(x, axis=-1, keepdims=True)
        s = [st_ref[0, h] for h in hs]
        dsn = [dstate[h] for h in hs]
        sb = [x.astype(BF16) for x in s]
        dsb = [x.astype(BF16) for x in dsn]
        dob = [x.astype(BF16) for x in dos]
        v_new = _heads(lambda u, w, x: u - _bdot(w, x), f["u"], f["w"], sb)
        dv_new = _heads(lambda at, d, kd, x: _bdot(at, d, 0, 0) + _bdot(kd, x), f["attn"], dob, f["k_dec"], dsb)
        dattn = _heads(lambda d, vn: _bdot(d, vn, 1, 1), dob, v_new)
        dq_dec = _heads(lambda d, x: _bdot(d, x, 1, 1), dob, sb)
        dk_dec = _heads(lambda vn, x: _bdot(vn, x, 1, 1), v_new, dsb)
        dw = _heads(lambda dv_, x: -_bdot(dv_, x, 1, 1), dv_new, sb)
        new_ds = _heads(lambda x, c, qd, d, w, dv_: x * c + _bdot(qd, d, 0, 0) - _bdot(w, dv_, 0, 0),
                        dsn, f["c_dec"], f["q_dec"], dob, f["w"], dv_new)
        for h in hs:
            dstate[h] = new_ds[h]
        drhs = _heads(lambda tt, x, y: _dot3(tt, _split(jnp.concatenate([x, y], axis=1)), 0, 0), f["t"], dv_new, dw)
        drhs_u = [x[:, :DN_DIM] for x in drhs]
        drhs_w = [x[:, DN_DIM:] for x in drhs]
        da = _heads(lambda du_, u, dw_, w: jnp.where(f["strict"], -(_bdot(du_, u, 1, 1) + _bdot(dw_, w, 1, 1)), 0.0),
                    drhs_u, f["u"], drhs_w, f["w"])
        dkk = _heads(lambda x, d: x * d, da, f["decay"])
        dqk = _heads(lambda x, d: x * d, dattn, f["decay"])
        by_k = _heads(lambda x, y, k_: _bdot(jnp.concatenate([x, y], axis=0), k_), dqk, dkk, ks)
        dq = _heads(lambda x, dqd, e: x[:CHUNK] + dqd * e, by_k, dq_dec, f["eg"])
        dkb = _heads(lambda x, dw_, e: x[CHUNK:] + dw_ * e, by_k, drhs_w, f["eg"])
        dk = _heads(lambda x, kb_, y, q_, dkd, el, dkb_, b: _bdot(x, kb_, 0, 0) + _bdot(y, q_, 0, 0) + dkd * el + dkb_ * b,
                    dkk, f["kb"], dqk, qs, dk_dec, f["egl"], dkb, f["bcol"])
        m = _heads(lambda x, a_, y, at: x * a_ + y * at, da, f["a"], dattn, f["attn"])
        ones = jnp.ones((CHUNK, LANE), BF16)
        col_m = [(_dot(mh, ones, 0, 0) + _dot(ml, ones, 0, 0))[:, 0:1] for mh, ml in map(_split, m)]
        dgc_all = jnp.zeros((CHUNK, LANE), F32)
        dbeta_all = jnp.zeros((CHUNK, LANE), F32)
        for h in hs:
            dq_ref[:, h * DN_DIM:(h + 1) * DN_DIM] = dq[h]
            dk_ref[:, h * DN_DIM:(h + 1) * DN_DIM] = dk[h]
            dv_ref[:, h * DN_DIM:(h + 1) * DN_DIM] = drhs_u[h] * f["bcol"][h]
            kdec_term = rowsum(dk_dec[h] * f["k_dec"][h])
            dc_dec = _sum_all(dsn[h] * s[h])
            dgc = (rowsum(m[h]) - col_m[h] + rowsum(dq_dec[h] * f["q_dec"][h]) - kdec_term
                   + rowsum(drhs_w[h] * f["rhs_w"][h]))
            last_extra = jnp.sum(kdec_term, axis=0, keepdims=True) + dc_dec * f["c_dec"][h]
            dgc = dgc + jnp.where(is_last, last_extra, 0.0)
            dbeta = rowsum(drhs_u[h] * vs[h]) + rowsum(dkb[h] * ks[h])
            dgc_all = jnp.where(lane == h, dgc, dgc_all)
            dbeta_all = jnp.where(lane == DN_HEADS + h, dbeta, dbeta_all)
        dg_all = _dot(f["lower"].astype(F32), dgc_all, 0, 0, precision=HIGHEST)
        dgb_ref[...] = jnp.where(lane < DN_HEADS, dg_all, dbeta_all)

    rev = lambda n: (N - 1 - n, 0)
    blk = pl.BlockSpec((CHUNK, DN_WIDTH), rev)
    gblk = pl.BlockSpec((CHUNK, LANE), rev)
    st_blk = pl.BlockSpec((1, DN_HEADS, DN_DIM, DN_DIM), lambda n: (N - 1 - n, 0, 0, 0))
    t_blk = pl.BlockSpec((1, DN_HEADS, CHUNK, CHUNK), lambda n: (N - 1 - n, 0, 0, 0))
    return pl.pallas_call(
        body, name="dn_chunk_bwd", grid=(N,),
        in_specs=[blk, blk, blk, gblk, st_blk, t_blk, blk, blk, blk],
        out_specs=[blk, blk, blk, gblk],
        out_shape=[jax.ShapeDtypeStruct((S, DN_WIDTH), F32)] * 3 + [jax.ShapeDtypeStruct((S, LANE), F32)],
        scratch_shapes=[pltpu.VMEM((DN_HEADS, DN_DIM, DN_DIM), F32)],
        compiler_params=_cparams(("arbitrary",)),
    )(q, k, v, gb, states, t_saved, u_saved, w_saved, do)


def _dn_post(o, qkvz, gain_row):
    def fn(i, n, ot, z, g):
        cols = []
        for h in range(DN_HEADS):
            seg = ot[:, h * DN_DIM:(h + 1) * DN_DIM]
            cols.append(seg * lax.rsqrt(jnp.mean(seg * seg, axis=-1, keepdims=True) + EPS) * g)
        return (jnp.concatenate(cols, axis=1) * (z * _sigmoid(z)),)
    return _rows("dn_post", fn, [(o, "row"), (qkvz, "row", (3, DN_WIDTH)), (gain_row, "full")], [(DN_WIDTH, BF16)], tr=512)


def _dn_post_bwd(don, o, qkvz, gain_row):
    def fn(i, n, dy, ot, z, g):
        sg = _sigmoid(z)
        sz = z * sg
        dos, ohs = [], []
        dg = jnp.zeros((1, DN_DIM), F32)
        for h in range(DN_HEADS):
            sl = slice(h * DN_DIM, (h + 1) * DN_DIM)
            seg = ot[:, sl]
            r = lax.rsqrt(jnp.mean(seg * seg, axis=-1, keepdims=True) + EPS)
            oh = seg * r
            dno = dy[:, sl] * sz[:, sl]
            dg = dg + _colsum(dno * oh)
            dn = dno * g
            dos.append(r * (dn - oh * jnp.mean(dn * oh, axis=-1, keepdims=True)))
            ohs.append(oh * g)
        dz = dy * jnp.concatenate(ohs, axis=1) * (sg * (1.0 + z * (1.0 - sg)))
        return jnp.concatenate(dos, axis=1), dz, dg
    ins = [(don, "row"), (o, "row"), (qkvz, "row", (3, DN_WIDTH)), (gain_row, "full")]
    return _rows("dn_post_bwd", fn, ins, [(DN_WIDTH, F32), (DN_WIDTH, F32)], tr=256, accs=[((1, DN_DIM), F32)])


def _dn_prep_bwd(dq, dk, dv, dgb, u, ab, alog_row, dt_row):
    def fn(i, n, dqt, dkt, dvt, dgbt, ut, abt, al, dt):
        ut = ut.astype(F32)
        sg = _sigmoid(ut)
        y = ut * sg
        dys = []
        for grad, base, sc in ((dqt, 0, DN_DIM ** -0.5), (dkt, DN_WIDTH, 1.0)):
            for h in range(DN_HEADS):
                seg = y[:, base + h * DN_DIM:base + (h + 1) * DN_DIM]
                gr = grad[:, h * DN_DIM:(h + 1) * DN_DIM]
                r = lax.rsqrt(jnp.sum(seg * seg, axis=-1, keepdims=True) + EPS)
                xh = seg * r
                dys.append((r * sc) * (gr - xh * jnp.sum(gr * xh, axis=-1, keepdims=True)))
        dy = jnp.concatenate(dys + [dvt], axis=1)
        du = dy * (sg * (1.0 + ut * (1.0 - sg)))
        lane = lax.broadcasted_iota(jnp.int32, abt.shape, 1)
        is_g = lane < DN_HEADS
        ea = jnp.exp(al)
        x = abt + dt
        slope = -ea * _sigmoid(x)
        gval = -ea * _softplus(x)
        dg = jnp.where(is_g, dgbt, 0.0)
        beta = _sigmoid(abt)
        dab = jnp.where(is_g, dg * slope, jnp.where(lane < 2 * DN_HEADS, dgbt * beta * (1.0 - beta), 0.0))
        return du, dab, _colsum(dg * gval), _colsum(dg * slope)
    ins = [(dq, "row"), (dk, "row"), (dv, "row"), (dgb, "row"), (u, "row"), (ab, "row"), (alog_row, "full"), (dt_row, "full")]
    return _rows("dn_prep_bwd", fn, ins, [(DN_QKV, F32), (LANE, BF16)], tr=256, accs=[((1, LANE), F32)] * 2)


def _dn_conv_bwd(du, dz, qkvz, convw):
    tr = 256

    def fn(i, n, dut, dun, dzt, x, xp, w):
        dun = jnp.where(i < n - 1, dun, 0.0)
        dus = jnp.concatenate([dut, dun], axis=0)
        xs = jnp.concatenate([jnp.where(i > 0, xp, 0.0), x], axis=0)
        dx = None
        dws = []
        for j in range(CONV_W):
            sh = CONV_W - 1 - j
            term = (pltpu.roll(dus, tr + SUBLANE - sh, 0) if sh else dus)[:tr] * w[j:j + 1, :]
            dx = term if dx is None else dx + term
            dws.append(_colsum(dut * (pltpu.roll(xs, sh, 0) if sh else xs)[SUBLANE:]))
        return (jnp.concatenate([dx.astype(BF16), dzt.astype(BF16)], axis=1), *dws)

    ins = [(du, "row"), (du, "next8"), (dz, "row"), (qkvz, "row", (0, DN_QKV)), (qkvz, "prev8", (0, DN_QKV)), (convw, "full")]
    res = _rows("dn_conv_bwd", fn, ins, [(DN_QKVZ, BF16)], tr=tr, accs=[((1, DN_QKV), F32)] * CONV_W)
    return res[0], res[1:]


def _add(acc, r):
    return (r + acc,)


def _mlp_ple_fwd(i, x1, hm, p_i, ple_gain, next_gain, w_up, w_down, w_ple, w_gate, target=None):
    u, a = _mm(f"mlp_up{i}", hm, w_up, epilogue=lambda acc: (acc, jnp.square(jnp.maximum(acc, 0.0))),
               out_dtypes=(BF16, BF16))
    x2, hp = _mm(f"mlp_down{i}", a, w_down, epilogue=_res_norm, extras=(x1, ple_gain), out_dtypes=(F32, BF16),
                 tm_pref=FUSED_ROWS)
    pp = _mm(f"ple_proj{i}", p_i, w_ple)

    def gate_epilogue(acc, x2t, ppt, g):
        gate = _sigmoid(acc)
        x3 = x2t + ppt * gate
        return x3, gate, x3 * lax.rsqrt(jnp.mean(x3 * x3, axis=-1, keepdims=True) + EPS) * g

    def loss_epilogue(acc, x2t, ppt, tt):
        gate = _sigmoid(acc)
        err = x2t + ppt * gate - tt
        dy = err * (1.0 / D_MODEL)
        return dy, dy * gate, dy * ppt * gate * (1.0 - gate), _colsum(err * err)

    saved = dict(x1=x1, hm=hm, u=u, a=a, x2=x2, hp=hp, pp=pp, p=p_i)
    if target is None:
        x3, saved["gate"], h_next = _mm(f"ple_gate{i}", hp, w_gate, epilogue=gate_epilogue, extras=(x2, pp, next_gain),
                                        out_dtypes=(F32, F32, BF16), tm_pref=FUSED_ROWS)
        return x3, h_next, saved
    dy, saved["dpp"], saved["dzg"], sq = _mm(f"ple_gate{i}", hp, w_gate, epilogue=loss_epilogue, extras=(x2, pp, target),
                                             out_dtypes=(F32, BF16, BF16), n_colsums=1, tm_pref=FUSED_ROWS)
    return dy, sq, saved


def _mlp_ple_bwd(i, dx3, sv, mlp_gain, ple_gain, w_up, w_down, w_gate):
    if "dpp" in sv:
        dpp, dzg = sv["dpp"], sv["dzg"]
    else:
        def fn(_i, _n, d, g, pp):
            return d * g, d * pp * g * (1.0 - g)
        dpp, dzg = _rows(f"ple_gate_bwd{i}", fn, [(dx3, "row"), (sv["gate"], "row"), (sv["pp"], "row")],
                         [(D_MODEL, BF16), (D_MODEL, BF16)], tr=512)
    d_w_ple = _mm(f"ple_proj_dw{i}", sv["p"], dpp, ta=True, out_dtypes=(BF16,))
    d_w_gate = _mm(f"ple_gate_dw{i}", sv["hp"], dzg, ta=True, out_dtypes=(BF16,))
    dx2, dx2b, d_ple_gain = _mm(f"ple_gate_dx{i}", dzg, w_gate, tb=True, epilogue=_norm_bwd_2,
                                extras=(sv["x2"], ple_gain, dx3), out_dtypes=(F32, BF16), n_colsums=1, tm_pref=FUSED_ROWS)
    d_w_down = _mm(f"mlp_down_dw{i}", sv["a"], dx2b, ta=True, out_dtypes=(BF16,))
    du = _mm(f"mlp_down_dx{i}", dx2b, w_down, tb=True,
             epilogue=lambda acc, ut: (acc * (2.0 * jnp.maximum(ut.astype(F32), 0.0)),), extras=(sv["u"],), out_dtypes=(BF16,))
    d_w_up = _mm(f"mlp_up_dw{i}", sv["hm"], du, ta=True, out_dtypes=(BF16,))
    dx1, dx1b, d_mlp_gain = _mm(f"mlp_up_dx{i}", du, w_up, tb=True, epilogue=_norm_bwd_2,
                                extras=(sv["x1"], mlp_gain, dx2), out_dtypes=(F32, BF16), n_colsums=1, tm_pref=FUSED_ROWS)
    return dx1, dx1b, dict(w_ple=d_w_ple, w_ple_gate=d_w_gate, w_down=d_w_down, w_up=d_w_up,
                           ple_norm=d_ple_gain, mlp_norm=d_mlp_gain)


def _after(small, token):
    return small + token[0:1, 0:1]


def _local_step(x, p, positions, target, W, P, rest_of_weights, send_layer1, send_mlp0, send_attn):
    consts = _head_consts()
    bd = _block_diag(1.0 / A_HEAD_DIM)
    bd1 = _block_diag(1.0)
    ct, st = _rope_tables(positions, consts)
    gains = jnp.stack([jnp.tile(v, A_HEADS) for g in range(3) for v in (P["attn_q_gain"][g], P["attn_k_gain"][g])])
    pad = LANE - DN_HEADS
    alog_row = jnp.pad(P["dn_a_log"].reshape(1, DN_HEADS), ((0, 0), (0, pad)))
    dt_row = jnp.pad(P["dn_dt_bias"].reshape(1, DN_HEADS), ((0, 0), (0, pad)))
    ogain_row = P["dn_o_gain"].reshape(1, DN_DIM)
    row = lambda name, i: P[name][i:i + 1]

    h0 = _rmsnorm_fwd("mix_norm0", x, row("mix_norm", 0))
    qkv = _mm("attn_qkv", h0, W["attn_w_qkv"], out_dtypes=(BF16,))
    qkvn = _attn_prep(qkv, gains, ct, st, consts, bd)
    os_, lses = zip(*[_attn_fwd(qkvn[g], g) for g in range(3)])
    o_attn = _attn_merge(os_, lses)
    x1, hm0 = _mm("attn_out", o_attn, W["attn_w_o"], epilogue=_res_norm, extras=(x, row("mlp_norm", 0)),
                  out_dtypes=(F32, BF16), tm_pref=FUSED_ROWS)
    W = {**W, **rest_of_weights(x1)}
    x3, h1, sv0 = _mlp_ple_fwd(0, x1, hm0, p[0], row("ple_norm", 0), row("mix_norm", 1),
                               W["w_up"][0], W["w_down"][0], W["w_ple"][0], W["w_ple_gate"][0])
    qkvz = _mm("dn_in_qkvz", h1, W["dn_w_qkvz"])
    ab = _mm("dn_in_ab", h1, W["dn_w_ab"])
    u, q, k, v, gb = _dn_prep(qkvz, ab, W["dn_conv"], alog_row, dt_row)
    o_dn, states = _dn_chunk_fwd(q, k, v, gb)
    on = _dn_post(o_dn, qkvz, ogain_row)
    x4, hm1 = _mm("dn_out", on, W["dn_w_o"], epilogue=_res_norm, extras=(x3, row("mlp_norm", 1)),
                  out_dtypes=(F32, BF16), tm_pref=FUSED_ROWS)
    dy, sq, sv1 = _mlp_ple_fwd(1, x4, hm1, p[1], row("ple_norm", 1), None,
                               W["w_up"][1], W["w_down"][1], W["w_ple"][1], W["w_ple_gate"][1], target=target)

    dx4, dx4b, g1 = _mlp_ple_bwd(1, dy, sv1, row("mlp_norm", 1), row("ple_norm", 1),
                                 W["w_up"][1], W["w_down"][1], W["w_ple_gate"][1])
    don = _mm("dn_out_dx", dx4b, W["dn_w_o"], tb=True)
    d_dn_w_o = _mm("dn_out_dw", on, dx4b, ta=True, out_dtypes=(BF16,))
    do_dn, dz, d_ogain = _dn_post_bwd(don, o_dn, qkvz, ogain_row)
    dq, dk, dv, dgb = _dn_chunk_bwd(q, k, v, gb, states, do_dn)
    du, dab, d_alog, d_dt = _dn_prep_bwd(dq, dk, dv, dgb, u, ab, alog_row, dt_row)
    dqkvz, d_conv = _dn_conv_bwd(du, dz, qkvz, W["dn_conv"])
    dh1 = _mm("dn_in_ab_dx", dab, W["dn_w_ab"], tb=True)
    dx3, d_mix1 = _mm("dn_in_qkvz_dx", dqkvz, W["dn_w_qkvz"], tb=True,
                      epilogue=lambda acc, part, xt, g, dres: _norm_bwd(acc + part, xt, g, dres),
                      extras=(dh1, x3, row("mix_norm", 1), dx4), n_colsums=1, tm_pref=FUSED_ROWS)
    d_w_qkvz = _mm("dn_in_qkvz_dw", h1, dqkvz, ta=True, out_dtypes=(BF16,))
    d_w_ab = _mm("dn_in_ab_dw", h1, dab, ta=True, out_dtypes=(BF16,))
    token = send_layer1(dict(
        dn_w_qkvz=d_w_qkvz, dn_w_ab=d_w_ab, dn_conv=jnp.concatenate(d_conv, 0), dn_w_o=d_dn_w_o,
        w_up=g1["w_up"], w_down=g1["w_down"], w_ple=g1["w_ple"], w_ple_gate=g1["w_ple_gate"]))
    dx1, dx1b, g0 = _mlp_ple_bwd(0, dx3, sv0, row("mlp_norm", 0), _after(row("ple_norm", 0), token),
                                 W["w_up"][0], W["w_down"][0], W["w_ple_gate"][0])
    token = send_mlp0(dict(w_up=g0["w_up"], w_down=g0["w_down"], w_ple=g0["w_ple"], w_ple_gate=g0["w_ple_gate"]))
    do_attn = _mm("attn_out_dx", dx1b, W["attn_w_o"], tb=True, epilogue=_add, extras=(_after(jnp.zeros((1, A_WIDTH), F32), token),))
    d_attn_w_o = _mm("attn_out_dw", o_attn, dx1b, ta=True, out_dtypes=(BF16,))
    dos, cs = _attn_merge_bwd(do_attn, os_, lses, bd1)
    grads9 = []
    for g in range(3):
        grads9 += list(_attn_bwd(qkvn[g], g, dos[g], lses[g], cs[g]))
    dqkv, dgains = _attn_prep_bwd(qkv, grads9, gains, ct, st, consts, bd)
    d_attn_w_qkv = _mm("attn_qkv_dw", h0, dqkv, ta=True, out_dtypes=(BF16,))
    token = send_attn(dict(attn_w_qkv=d_attn_w_qkv, attn_w_o=d_attn_w_o))
    dx0, d_mix0 = _mm("attn_qkv_dx", dqkv, W["attn_w_qkv"], tb=True, epilogue=_norm_bwd,
                      extras=(x, _after(row("mix_norm", 0), token), dx1), n_colsums=1, tm_pref=FUSED_ROWS,
                      tk_pref=A_QKV // 3)

    dg = jnp.stack([t.reshape(A_HEADS, A_HEAD_DIM).sum(0) for t in dgains])
    small = dict(
        mix_norm=jnp.concatenate([d_mix0, d_mix1], 0),
        attn_q_gain=dg[0::2][None], attn_k_gain=dg[1::2][None],
        dn_a_log=d_alog[:, :DN_HEADS], dn_dt_bias=d_dt[:, :DN_HEADS], dn_o_gain=d_ogain,
        mlp_norm=jnp.concatenate([g0["mlp_norm"], g1["mlp_norm"]], 0),
        ple_norm=jnp.concatenate([g0["ple_norm"], g1["ple_norm"]], 0),
    )
    return sq, dx0, small


MESH_IDS = pl.DeviceIdType.MESH
ANY = pl.BlockSpec(memory_space=pl.ANY)


def _place():
    return lax.axis_index("x"), lax.axis_index("y"), lax.axis_index("c")


def _sem_scratch(n_streams):
    return [pltpu.SemaphoreType.DMA((n_streams, N_DEV - 1)), pltpu.SemaphoreType.DMA((n_streams, N_DEV - 1)),
            pltpu.SemaphoreType.DMA((n_streams,))]


def _all_gather(name, arrays, streams):
    n_in, n_st = len(arrays), len(streams)
    shapes = [arrays[a].shape if li is None else arrays[a].shape[1:] for a, li in streams]

    def body(*refs):
        in_refs, out_refs, token = refs[:n_in], refs[n_in:n_in + n_st], refs[n_in + n_st]
        send_sems, recv_sems, local_sems = refs[n_in + n_st + 1:]
        token[...] = jnp.zeros_like(token)
        x, y, c = _place()
        me, sibling = (x, y, c), (x, y, 1 - c)
        chips = [(1 - x, y), (x, 1 - y), (1 - x, 1 - y)]

        def copy(s, k, block, to, own=False):
            a, li = streams[s]
            dst = out_refs[s].at[4 * block[0] + 2 * block[1] + block[2]]
            src = (in_refs[a] if li is None else in_refs[a].at[li]) if own else dst
            return pltpu.make_async_remote_copy(src_ref=src, dst_ref=dst, send_sem=send_sems.at[s, k],
                                                recv_sem=recv_sems.at[s, k], device_id=to, device_id_type=MESH_IDS)

        started = []
        for s, (a, li) in enumerate(streams):
            src = in_refs[a] if li is None else in_refs[a].at[li]
            mine = pltpu.make_async_copy(src, out_refs[s].at[4 * x + 2 * y + c], local_sems.at[s])
            mine.start()
            started.append(mine)
        sends = []
        for s in range(n_st):
            first = [copy(s, 0, me, sibling, own=True)]
            first += [copy(s, 1 + j, me, (*chip, c), own=True) for j, chip in enumerate(chips)]
            for cp in first:
                cp.start()
            sends += first
        for j, chip in enumerate(chips):
            for s in range(n_st):
                copy(s, 1 + j, (*chip, c), me).wait_recv()
                fwd = copy(s, 4 + j, (*chip, c), sibling)
                fwd.start()
                sends.append(fwd)
        for s in range(n_st):
            copy(s, 0, sibling, me).wait_recv()
            for j, chip in enumerate(chips):
                copy(s, 4 + j, (*chip, 1 - c), me).wait_recv()
        for cp in sends:
            cp.wait_send()
        for cp in started:
            cp.wait()

    res = pl.pallas_call(
        body, name=name,
        out_shape=[jax.ShapeDtypeStruct((N_DEV,) + tuple(sh), arrays[a].dtype) for sh, (a, _) in zip(shapes, streams)]
        + [jax.ShapeDtypeStruct((SUBLANE, LANE), F32)],
        in_specs=[ANY] * n_in, out_specs=[ANY] * n_st + [pl.BlockSpec(memory_space=pltpu.VMEM)],
        scratch_shapes=_sem_scratch(n_st),
    )(*arrays)
    return list(res[:n_st]), res[n_st]


HBM = pl.BlockSpec(memory_space=pltpu.HBM)
SEM = pl.BlockSpec(memory_space=pltpu.SEMAPHORE)
FLOWS = pltpu.CompilerParams(has_side_effects=pltpu.SideEffectType.DATAFLOW_SIDE_EFFECTING)


def _in_hbm(a):
    return pltpu.with_memory_space_constraint(a, pltpu.HBM)


def _hbm_like(a):
    return pltpu.HBM(a.shape, a.dtype)


def _peers(x, y, c):
    return [(1 - x if k & 4 else x, 1 - y if k & 2 else y, 1 - c if k & 1 else c) for k in range(1, N_DEV)]


def _start_copies(name, n_remote, n_own, make_copies, operands):
    n = len(operands)

    def body(*refs):
        for cp in make_copies(refs[:n], refs[n], refs[n + 1], refs[n + 2]):
            cp.start()
        refs[-1][...] = jnp.zeros_like(refs[-1])

    res = pl.pallas_call(
        body, name=name,
        out_shape=(pltpu.SemaphoreType.DMA((n_remote,)), pltpu.SemaphoreType.DMA((n_remote,)), pltpu.SemaphoreType.DMA((n_own,)),
                   *[_hbm_like(t) for t in operands], jax.ShapeDtypeStruct((SUBLANE, LANE), F32)),
        in_specs=[HBM] * n, out_specs=(SEM, SEM, SEM, *[HBM] * n, pl.BlockSpec(memory_space=pltpu.VMEM)),
        input_output_aliases={i: 3 + i for i in range(n)}, compiler_params=FLOWS,
    )(*[_in_hbm(t) for t in operands])
    return res[:3], list(res[3:3 + n]), res[-1]


def _wait_copies(name, make_waits, sems, operands, after):
    n = len(operands)

    def body(*refs):
        for wait in make_waits(refs[:n], refs[n], refs[n + 1], refs[n + 2]):
            wait()

    res = pl.pallas_call(
        body, name=name, out_shape=tuple(_hbm_like(t) for t in operands),
        in_specs=[HBM] * n + [SEM, SEM, SEM, ANY], out_specs=tuple([HBM] * n),
        input_output_aliases={i: i for i in range(n)}, compiler_params=FLOWS,
    )(*operands, *sems, after)
    return list(res)


def _gather_plan(n_in, streams):
    def block(arr, s):
        a, li = streams[s]
        return arr[a] if li is None else arr[a].at[li]

    def copies(refs, send_sems, recv_sems, own_sems, arrivals=False):
        arr, land = refs[:n_in], refs[n_in:]
        x, y, c = _place()
        me = 4 * x + 2 * y + c
        out = []
        for s in range(len(streams)):
            out.append(("own", pltpu.make_async_copy(block(arr, s), land[s].at[me], own_sems.at[s])))
            for k, (px, py, pc) in enumerate(_peers(x, y, c)):
                out.append(("remote", pltpu.make_async_remote_copy(
                    src_ref=block(arr, s), dst_ref=land[s].at[4 * px + 2 * py + pc if arrivals else me],
                    send_sem=send_sems.at[s * (N_DEV - 1) + k], recv_sem=recv_sems.at[s * (N_DEV - 1) + k],
                    device_id=(px, py, pc), device_id_type=MESH_IDS)))
        return out
    return copies


def _exchange_plan(n_st):
    def copies(refs, send_sems, recv_sems, own_sems, arrivals=False):
        snd, rcv = refs[:n_st], refs[n_st:]
        x, y, c = _place()
        me = 4 * x + 2 * y + c
        out = []
        for s in range(n_st):
            out.append(("own", pltpu.make_async_copy(snd[s].at[me], rcv[s].at[me], own_sems.at[s])))
            for k, (px, py, pc) in enumerate(_peers(x, y, c)):
                peer = 4 * px + 2 * py + pc
                out.append(("remote", pltpu.make_async_remote_copy(
                    src_ref=snd[s].at[peer], dst_ref=rcv[s].at[peer if arrivals else me],
                    send_sem=send_sems.at[s * (N_DEV - 1) + k], recv_sem=recv_sems.at[s * (N_DEV - 1) + k],
                    device_id=(px, py, pc), device_id_type=MESH_IDS)))
        return out
    return copies


def _split_transfer(tag, plan, n_streams, operands):
    sems, operands, token = _start_copies(f"{tag}_start", n_streams * (N_DEV - 1), n_streams,
                                          lambda refs, a, b, o: [cp for _, cp in plan(refs, a, b, o)], operands)

    def waits(refs, a, b, o):
        out = []
        for kind, cp in plan(refs, a, b, o, arrivals=True):
            out += [cp.wait] if kind == "own" else [cp.wait_send, cp.wait_recv]
        return out

    return (lambda after: _wait_copies(f"{tag}_wait", waits, sems, operands, after)), token


def _gather_async(tag, arrays, streams):
    lands = [lax.empty((N_DEV,) + tuple(arrays[a].shape if li is None else arrays[a].shape[1:]), arrays[a].dtype)
             for a, li in streams]
    finish, token = _split_transfer(tag, _gather_plan(len(arrays), streams), len(streams), list(arrays) + lands)
    return (lambda after: finish(after)[len(arrays):]), token


def _exchange_async(tag, sends):
    recvs = [lax.empty(t.shape, t.dtype) for t in sends]
    finish, token = _split_transfer(tag, _exchange_plan(len(sends)), len(sends), list(sends) + recvs)
    return (lambda after: finish(after)[len(sends):]), token


def _dn_in_pieces():
    n = (DN_QKVZ + 2 * DN_HEADS) // N_DEV
    segs = ((0, DN_QKV, 0, 0), (DN_QKV, DN_QKV + 2 * DN_HEADS, 1, 0), (DN_QKV + 2 * DN_HEADS, DN_QKVZ + 2 * DN_HEADS, 0, DN_QKV))
    out = []
    for d in range(N_DEV):
        lo, hi = d * n, (d + 1) * n
        for s0, s1, tgt, t0 in segs:
            a, b = max(lo, s0), min(hi, s1)
            if a < b:
                out.append((d, a - lo, b - lo, tgt, t0 + a - s0))
    return out


def _unpack_cols(name, g):
    _, K, n = g.shape
    tr = 256

    def body(g_ref, o_ref):
        for d in range(N_DEV):
            o_ref[:, d * n:(d + 1) * n] = g_ref[d]

    return pl.pallas_call(
        body, name=name, grid=(K // tr,), in_specs=[pl.BlockSpec((N_DEV, tr, n), lambda i: (0, i, 0))],
        out_specs=pl.BlockSpec((tr, N_DEV * n), lambda i: (i, 0)),
        out_shape=jax.ShapeDtypeStruct((K, N_DEV * n), g.dtype), compiler_params=_cparams(("parallel",)),
    )(g)


def _pack_cols(name, w):
    K, n = w.shape[0], w.shape[1] // N_DEV
    tr = 256

    def body(w_ref, o_ref):
        for d in range(N_DEV):
            o_ref[d] = w_ref[:, d * n:(d + 1) * n]

    return pl.pallas_call(
        body, name=name, grid=(K // tr,), in_specs=[pl.BlockSpec((tr, N_DEV * n), lambda i: (i, 0))],
        out_specs=pl.BlockSpec((N_DEV, tr, n), lambda i: (0, i, 0)),
        out_shape=jax.ShapeDtypeStruct((N_DEV, K, n), w.dtype), compiler_params=_cparams(("parallel",)),
    )(w)


def _unpack_dn_in(g):
    _, K, n = g.shape
    tr = 256

    def body(g_ref, qkvz_ref, ab_ref):
        ab_ref[...] = jnp.zeros_like(ab_ref)
        for d, c0, c1, tgt, t0 in _dn_in_pieces():
            (qkvz_ref, ab_ref)[tgt][:, t0:t0 + c1 - c0] = g_ref[d, :, c0:c1]

    return pl.pallas_call(
        body, name="unpack_dn_in", grid=(K // tr,), in_specs=[pl.BlockSpec((N_DEV, tr, n), lambda i: (0, i, 0))],
        out_specs=[pl.BlockSpec((tr, DN_QKVZ), lambda i: (i, 0)), pl.BlockSpec((tr, LANE), lambda i: (i, 0))],
        out_shape=[jax.ShapeDtypeStruct((K, DN_QKVZ), g.dtype), jax.ShapeDtypeStruct((K, LANE), g.dtype)],
        compiler_params=_cparams(("parallel",)),
    )(g)


def _pack_dn_in(d_qkvz, d_ab):
    K = d_qkvz.shape[0]
    n = (DN_QKVZ + 2 * DN_HEADS) // N_DEV
    tr = 256

    def body(qkvz_ref, ab_ref, o_ref):
        for d, c0, c1, tgt, t0 in _dn_in_pieces():
            o_ref[d, :, c0:c1] = (qkvz_ref, ab_ref)[tgt][:, t0:t0 + c1 - c0]

    return pl.pallas_call(
        body, name="pack_dn_in", grid=(K // tr,),
        in_specs=[pl.BlockSpec((tr, DN_QKVZ), lambda i: (i, 0)), pl.BlockSpec((tr, LANE), lambda i: (i, 0))],
        out_specs=pl.BlockSpec((N_DEV, tr, n), lambda i: (0, i, 0)),
        out_shape=jax.ShapeDtypeStruct((N_DEV, K, n), d_qkvz.dtype), compiler_params=_cparams(("parallel",)),
    )(d_qkvz, d_ab)


ADAMW_ROWS = 256


def _adamw(name, parts, w, m, v):
    n_layers, R, C = w.shape
    tr = min(R, ADAMW_ROWS)
    assert R % tr == 0 and len(parts) == n_layers and all(p.shape == (N_DEV, R, C) for p in parts)
    c1 = 1.0 - B1 ** STEP
    c2 = 1.0 - B2 ** STEP

    def body(*refs):
        p_refs = refs[:n_layers]
        w_ref, m_ref, v_ref, g_ref, d_ref, nm_ref, nv_ref = refs[n_layers:]
        layer = pl.program_id(0)
        for li, p_ref in enumerate(p_refs):
            @pl.when(layer == li)
            def _(p_ref=p_ref):
                g = p_ref[0].astype(F32)
                for dev in range(1, N_DEV):
                    g = g + p_ref[dev].astype(F32)
                nm = B1 * m_ref[...] + (1.0 - B1) * g
                nv = B2 * v_ref[...] + (1.0 - B2) * jnp.square(g)
                g_ref[...] = g
                nm_ref[...] = nm
                nv_ref[...] = nv
                d_ref[...] = -LR * ((nm / c1) / (jnp.sqrt(nv / c2) + ADAM_EPS) + WD * w_ref[...])

    blk = pl.BlockSpec((None, tr, C), lambda l, i: (l, i, 0))
    return pl.pallas_call(
        body, name=name, grid=(n_layers, R // tr),
        in_specs=[pl.BlockSpec((N_DEV, tr, C), lambda l, i: (0, i, 0))] * n_layers + [blk, blk, blk],
        out_specs=[blk] * 4, out_shape=[jax.ShapeDtypeStruct((n_layers, R, C), F32)] * 4,
        compiler_params=_cparams(("parallel", "parallel")),
    )(*parts, w, m, v)


SMALL = ("mix_norm", "attn_q_gain", "attn_k_gain", "dn_a_log", "dn_dt_bias", "dn_o_gain", "mlp_norm", "ple_norm")
WEIGHTS = ("mix_norm", "attn_w_qkv", "attn_q_gain", "attn_k_gain", "attn_w_o", "dn_w_in", "dn_conv", "dn_a_log",
           "dn_dt_bias", "dn_o_gain", "dn_w_o", "mlp_norm", "w_up", "w_down", "ple_norm", "w_ple", "w_ple_gate")


def _to_rows(flat, multiple):
    n = flat.shape[-1]
    rows = -(-n // (LANE * multiple)) * multiple
    return jnp.pad(flat, [(0, rows * LANE - n)]).reshape(rows, LANE)


def _cols_to_devices(w):
    K, N = w.shape
    return jnp.transpose(w.reshape(K, N_DEV, N // N_DEV), (1, 0, 2))


def _cols_from_devices(g):
    _, K, n = g.shape
    return jnp.transpose(g, (1, 0, 2)).reshape(K, N_DEV * n)


SMALL_ROWS = 96


def _pack_small(vals, loss_rows):
    rows = [_to_rows(vals[n].reshape(-1), SUBLANE) for n in SMALL] + [loss_rows]
    buf = jnp.concatenate(rows, 0)
    assert buf.shape == (SMALL_ROWS, LANE)
    return buf


def _unpack_small(buf, like):
    out, r = {}, 0
    for n in SMALL:
        sz = math.prod(like[n].shape)
        out[n] = buf[r:r + -(-sz // LANE)].reshape(-1)[:sz].reshape(like[n].shape)
        r += -(-sz // (LANE * SUBLANE)) * SUBLANE
    return out


def kernel(x, p, positions, mix_norm, attn_w_qkv, attn_q_gain, attn_k_gain, attn_w_o, dn_w_in, dn_conv, dn_a_log, dn_dt_bias, dn_o_gain, dn_w_o, mlp_norm, w_up, w_down, ple_norm, w_ple, w_ple_gate, loss_target, m_mix_norm, m_attn_w_qkv, m_attn_q_gain, m_attn_k_gain, m_attn_w_o, m_dn_w_in, m_dn_conv, m_dn_a_log, m_dn_dt_bias, m_dn_o_gain, m_dn_w_o, m_mlp_norm, m_w_up, m_w_down, m_ple_norm, m_w_ple, m_w_ple_gate, v_mix_norm, v_attn_w_qkv, v_attn_q_gain, v_attn_k_gain, v_attn_w_o, v_dn_w_in, v_dn_conv, v_dn_a_log, v_dn_dt_bias, v_dn_o_gain, v_dn_w_o, v_mlp_norm, v_w_up, v_w_down, v_ple_norm, v_w_ple, v_w_ple_gate):
    w = dict(mix_norm=mix_norm, attn_w_qkv=attn_w_qkv, attn_q_gain=attn_q_gain, attn_k_gain=attn_k_gain, attn_w_o=attn_w_o,
             dn_w_in=dn_w_in, dn_conv=dn_conv, dn_a_log=dn_a_log, dn_dt_bias=dn_dt_bias, dn_o_gain=dn_o_gain, dn_w_o=dn_w_o,
             mlp_norm=mlp_norm, w_up=w_up, w_down=w_down, ple_norm=ple_norm, w_ple=w_ple, w_ple_gate=w_ple_gate)
    m = dict(mix_norm=m_mix_norm, attn_w_qkv=m_attn_w_qkv, attn_q_gain=m_attn_q_gain, attn_k_gain=m_attn_k_gain,
             attn_w_o=m_attn_w_o, dn_w_in=m_dn_w_in, dn_conv=m_dn_conv, dn_a_log=m_dn_a_log, dn_dt_bias=m_dn_dt_bias,
             dn_o_gain=m_dn_o_gain, dn_w_o=m_dn_w_o, mlp_norm=m_mlp_norm, w_up=m_w_up, w_down=m_w_down,
             ple_norm=m_ple_norm, w_ple=m_w_ple, w_ple_gate=m_w_ple_gate)
    v = dict(mix_norm=v_mix_norm, attn_w_qkv=v_attn_w_qkv, attn_q_gain=v_attn_q_gain, attn_k_gain=v_attn_k_gain,
             attn_w_o=v_attn_w_o, dn_w_in=v_dn_w_in, dn_conv=v_dn_conv, dn_a_log=v_dn_a_log, dn_dt_bias=v_dn_dt_bias,
             dn_o_gain=v_dn_o_gain, dn_w_o=v_dn_w_o, mlp_norm=v_mlp_norm, w_up=v_w_up, w_down=v_w_down,
             ple_norm=v_ple_norm, w_ple=v_w_ple, w_ple_gate=v_w_ple_gate)
    S = x.shape[1]

    bf = lambda a: a.astype(BF16)
    rows_to_devices = lambda t: t.reshape(N_DEV, t.shape[0] // N_DEV, t.shape[1])

    (g_qkv, g_ao), token = _all_gather("gather_attn", [bf(attn_w_qkv[0]), bf(attn_w_o[0])], [(0, None), (1, None)])
    rest_shards = [bf(dn_w_in[0]), bf(dn_w_o[0]), bf(w_up), bf(w_down), bf(w_ple), bf(w_ple_gate), _after(dn_conv[0], token)]
    rest_streams = [(0, None), (1, None), (2, 0), (2, 1), (3, 0), (3, 1), (4, 0), (4, 1), (5, 0), (5, 1), (6, None)]
    rest_arrived, token = _gather_async("gather_rest", rest_shards, rest_streams)
    W = dict(attn_w_qkv=_unpack_cols("unpack_attn_qkv", g_qkv), attn_w_o=_cols_from_devices(g_ao))

    def rest_of_weights(after):
        g_in, g_do, g_up0, g_up1, g_dn0, g_dn1, g_pl0, g_pl1, g_gt0, g_gt1, g_conv = rest_arrived(after)
        rest = dict(
            dn_conv=jnp.transpose(g_conv, (1, 0, 2)).reshape(CONV_W, DN_QKV), dn_w_o=g_do.reshape(DN_WIDTH, D_MODEL),
            w_up=[_cols_from_devices(g_up0), _cols_from_devices(g_up1)],
            w_down=[g_dn0.reshape(D_FF, D_MODEL), g_dn1.reshape(D_FF, D_MODEL)],
            w_ple=[_cols_from_devices(g_pl0), _cols_from_devices(g_pl1)],
            w_ple_gate=[g_gt0.reshape(D_MODEL, D_MODEL), g_gt1.reshape(D_MODEL, D_MODEL)])
        rest["dn_w_qkvz"], rest["dn_w_ab"] = _unpack_dn_in(g_in)
        return rest

    pending = {}

    def mlp_sends(g):
        return [_cols_to_devices(g["w_up"]), rows_to_devices(g["w_down"]), _cols_to_devices(g["w_ple"]),
                rows_to_devices(g["w_ple_gate"])]

    def start(tag, sends):
        pending[tag], token = _exchange_async(f"exchange_{tag}", sends)
        return token

    def send_layer1(g):
        conv_send = jnp.transpose(g["dn_conv"].reshape(CONV_W, N_DEV, DN_QKV // N_DEV), (1, 0, 2))
        return start("layer1", [_pack_dn_in(g["dn_w_qkvz"], g["dn_w_ab"]), conv_send, rows_to_devices(g["dn_w_o"])] + mlp_sends(g))

    def send_mlp0(g):
        return start("mlp0", mlp_sends(g))

    def send_attn(g):
        return start("attn", [_pack_cols("pack_attn_qkv", g["attn_w_qkv"]), _cols_to_devices(g["attn_w_o"])])

    P = dict(mix_norm=_after(mix_norm, token), attn_q_gain=attn_q_gain[0], attn_k_gain=attn_k_gain[0], dn_a_log=dn_a_log[0],
             dn_dt_bias=dn_dt_bias[0], dn_o_gain=dn_o_gain[0], mlp_norm=mlp_norm, ple_norm=ple_norm)

    sq, dx0, small_g = _local_step(x[0], p[:, 0], positions.reshape(S, 1), loss_target[0], W, P,
                                   rest_of_weights, send_layer1, send_mlp0, send_attn)

    r_in, r_conv, r_do, r_up1, r_dn1, r_pl1, r_gt1 = pending["layer1"](dx0)
    r_up0, r_dn0, r_pl0, r_gt0 = pending["mlp0"](dx0)
    r_qkv, r_ao = pending["attn"](dx0)
    big = {}
    for n, parts in (("attn_w_qkv", [r_qkv]), ("attn_w_o", [r_ao]), ("dn_w_in", [r_in]), ("dn_conv", [r_conv]),
                     ("dn_w_o", [r_do]), ("w_up", [r_up0, r_up1]), ("w_down", [r_dn0, r_dn1]),
                     ("w_ple", [r_pl0, r_pl1]), ("w_ple_gate", [r_gt0, r_gt1])):
        big[n] = _adamw(f"adamw_{n}", parts, w[n], m[n], v[n])

    loss_rows = jnp.pad((0.5 / D_MODEL) * jnp.sum(sq, axis=1, keepdims=True), ((0, SUBLANE - 1), (0, LANE - 1)))
    small_like = {n: w[n] for n in SMALL}
    parts_s = _all_gather("gather_small", [_pack_small(small_g, loss_rows)], [(0, None)])[0][0]
    zero_rows = jnp.zeros((SUBLANE, LANE), F32)
    small = _adamw("adamw_small", [parts_s], _pack_small(w, zero_rows)[None], _pack_small(m, zero_rows)[None],
                   _pack_small(v, zero_rows)[None])
    loss = small[0][0, SMALL_ROWS - SUBLANE, 0]
    small = [_unpack_small(b[0], small_like) for b in small]

    outs = [loss, dx0[None]]
    for k in range(4):
        for n in WEIGHTS:
            outs.append(small[k][n] if n in SMALL else big[n][k])
    return tuple(outs)
```

```python
import functools
import math

import jax
import jax.numpy as jnp
from jax import lax
from jax.experimental import pallas as pl
from jax.experimental.pallas import tpu as pltpu

F32 = jnp.float32
BF16 = jnp.bfloat16
HIGHEST = lax.Precision.HIGHEST

N_DEV = 8
D_MODEL = 1024
EPS = 1e-6
SWA_GROUPS = ((128, 1), (512, 4), (2048, 16))
A_HEADS = 8
A_HEAD_DIM = 64
A_WIDTH = A_HEADS * A_HEAD_DIM
A_QKV = 3 * 3 * A_WIDTH
ROPE_DIM = 16
ROPE_HALF = 8
ROPE_THETA = 500000.0
BAND = 128
DN_HEADS = 8
DN_DIM = 128
DN_WIDTH = DN_HEADS * DN_DIM
CONV_W = 4
CHUNK = 64
D_FF = 4 * D_MODEL
PLE_DIM = 256
LR, B1, B2, ADAM_EPS, WD, STEP = 0.001, 0.9, 0.999, 1e-08, 0.01, 10

VMEM_LIMIT = 56 * 1024 * 1024
MXU_TILE = 1024
MM_SLAB = 256
LANE = 128
SUBLANE = 8


def _cparams(sem):
    return pltpu.CompilerParams(dimension_semantics=sem, vmem_limit_bytes=VMEM_LIMIT)


def _tile(n, pref):
    if n <= pref:
        return n
    t = (pref // LANE) * LANE
    while t >= LANE:
        if n % t == 0:
            return t
        t -= LANE
    raise ValueError(f"no tile for {n}")


def _dot(a, b, ca=1, cb=0, precision=None):
    return lax.dot_general(a, b, (((ca,), (cb,)), ((), ())), precision=precision,
                           preferred_element_type=F32)


def _bdot(a, b, ca=1, cb=0):
    return _dot(a.astype(BF16), b.astype(BF16), ca, cb)


def _mm(name, a, b, *, ta=False, tb=False, epilogue=None, extras=(), out_dtypes=(F32,), n_colsums=0,
        tm_pref=MXU_TILE, tn_pref=1536, tk_pref=2 * MXU_TILE):
    M, K = (a.shape[1], a.shape[0]) if ta else a.shape
    N = b.shape[0] if tb else b.shape[1]
    assert (b.shape[1] if tb else b.shape[0]) == K
    tm, tn, tk = _tile(M, tm_pref), _tile(N, tn_pref), _tile(K, tk_pref)
    nk = K // tk
    n_out = len(out_dtypes)
    n_ext = len(extras)
    assert n_colsums == 0 or tn == N
    sub = min(tm, MM_SLAB)

    def body(*refs):
        a_ref, b_ref = refs[0], refs[1]
        ext = refs[2:2 + n_ext]
        outs = refs[2 + n_ext:2 + n_ext + n_out]
        sums = refs[2 + n_ext + n_out:2 + n_ext + n_out + n_colsums]
        row_tile, k = pl.program_id(0), pl.program_id(2)
        slabs = [slice(s * sub, (s + 1) * sub) for s in range(tm // sub)]

        def product(rows):
            return _bdot(a_ref[:, rows] if ta else a_ref[rows, :], b_ref[...], 0 if ta else 1, 1 if tb else 0)

        def finish(results):
            col_rows = []
            for rows, r in zip(slabs, results):
                res = (r,) if epilogue is None else epilogue(r, *[e[...] if e.shape[0] == 1 else e[rows, :] for e in ext])
                for o, v in zip(outs, res):
                    o[rows, :] = v.astype(o.dtype)
                col_rows.append(res[n_out:])
            for n, o in enumerate(sums):
                v = functools.reduce(lambda x, y: x + y, [c[n] for c in col_rows])

                @pl.when(row_tile == 0)
                def _(o=o, v=v):
                    o[...] = v

                @pl.when(row_tile > 0)
                def _(o=o, v=v):
                    o[...] += v

        if nk == 1:
            finish([product(rows) for rows in slabs])
            return
        acc = refs[-1]

        @pl.when(k == 0)
        def _():
            acc[...] = jnp.zeros_like(acc)

        for rows in slabs:
            acc[rows, :] += product(rows)

        @pl.when(k == nk - 1)
        def _():
            finish([acc[rows, :] for rows in slabs])

    a_spec = pl.BlockSpec((tk, tm), lambda i, j, k: (k, i)) if ta else pl.BlockSpec((tm, tk), lambda i, j, k: (i, k))
    b_spec = pl.BlockSpec((tn, tk), lambda i, j, k: (j, k)) if tb else pl.BlockSpec((tk, tn), lambda i, j, k: (k, j))
    ext_specs = []
    for e in extras:
        if e.shape[0] == 1 and M != 1:
            ext_specs.append(pl.BlockSpec((1, tn), lambda i, j, k: (0, j)))
        else:
            ext_specs.append(pl.BlockSpec((tm, tn), lambda i, j, k: (i, j)))
    out = pl.pallas_call(
        body, name=name,
        grid=(M // tm, N // tn, nk),
        in_specs=[a_spec, b_spec] + ext_specs,
        out_specs=[pl.BlockSpec((tm, tn), lambda i, j, k: (i, j)) for _ in range(n_out)]
        + [pl.BlockSpec((1, tn), lambda i, j, k: (0, 0)) for _ in range(n_colsums)],
        out_shape=[jax.ShapeDtypeStruct((M, N), dt) for dt in out_dtypes]
        + [jax.ShapeDtypeStruct((1, N), F32) for _ in range(n_colsums)],
        scratch_shapes=[pltpu.VMEM((tm, tn), F32)] if nk > 1 else [],
        compiler_params=_cparams(("arbitrary" if n_colsums else "parallel", "parallel", "arbitrary")),
    )(a, b, *extras)
    return out[0] if len(out) == 1 else tuple(out)


def _perm_matrices(tr, d):
    import numpy as np
    old = np.arange(tr)
    p = np.zeros((tr, tr), np.float32)
    p[(old % d) * (tr // d) + old // d, old] = 1.0
    return jnp.asarray(p, BF16), jnp.asarray(p.T, BF16)


def _permute(p, x):
    if x.dtype == BF16:
        return _dot(p, x)
    hi = x.astype(BF16)
    rest = x - hi.astype(F32)
    mid = rest.astype(BF16)
    lo = (rest - mid.astype(F32)).astype(BF16)
    return _dot(p, hi) + _dot(p, mid) + _dot(p, lo)


def _rows(name, fn, ins, outs, *, tr, accs=()):
    ins = [(e[0], e[1]) + (e[2] if len(e) > 2 else (0, e[0].shape[-1])) for e in ins]
    outs = [tuple(o) + (0,) * (3 - len(o)) for o in outs]
    n_rows = next(e[0].shape[0] if e[1] == "row" else e[0].shape[0] * e[0].shape[1] for e in ins if e[1] in ("row", "res"))
    assert n_rows % tr == 0 and tr % SUBLANE == 0
    steps = n_rows // tr
    t8 = tr // SUBLANE
    n8 = n_rows // SUBLANE
    dils = sorted({e[0].shape[0] for e in ins if e[1] == "res" and e[0].shape[0] > 1} | {o[2] for o in outs if o[2] > 1})
    perms = [m for d in dils for m in _perm_matrices(tr, d)]
    ins = ins + [(m, "full", 0, tr) for m in perms]
    n_in, n_out, n_acc = len(ins), len(outs), len(accs)

    def body(*refs):
        i = pl.program_id(0)
        to_res = {d: refs[n_in - len(perms) + 2 * j][...] for j, d in enumerate(dils)}
        to_tok = {d: refs[n_in - len(perms) + 2 * j + 1][...] for j, d in enumerate(dils)}
        tiles = []
        for r, e in zip(refs[:n_in - len(perms)], ins):
            d = e[0].shape[0] if e[1] == "res" else 0
            if d == 0:
                tiles.append(r[...])
            elif d == 1:
                tiles.append(r[0])
            else:
                tiles.append(_permute(to_tok[d], jnp.concatenate([r[j] for j in range(d)], axis=0)))
        vals = fn(i, steps, *tiles)
        if not isinstance(vals, (tuple, list)):
            vals = (vals,)
        assert len(vals) == n_out + n_acc
        for o, v, (_, dt, d) in zip(refs[n_in:n_in + n_out], vals[:n_out], outs):
            if d == 0:
                o[...] = v.astype(o.dtype)
            elif d == 1:
                o[0] = v.astype(o.dtype)
            else:
                y = _permute(to_res[d], v.astype(dt))
                for j in range(d):
                    o[j] = y[j * (tr // d):(j + 1) * (tr // d)].astype(o.dtype)
        if n_acc:
            acc_refs = refs[n_in + n_out:]

            @pl.when(i == 0)
            def _():
                for r in acc_refs:
                    r[...] = jnp.zeros_like(r)

            for r, v in zip(acc_refs, vals[n_out:]):
                r[...] += v.astype(r.dtype)

    in_specs = []
    for a, kind, cb, c in ins:
        if kind == "row":
            in_specs.append(pl.BlockSpec((tr, c), lambda i, cb=cb: (i, cb)))
        elif kind == "full":
            in_specs.append(pl.BlockSpec(a.shape, lambda i, z=(0,) * a.ndim: z))
        elif kind == "prev8":
            in_specs.append(pl.BlockSpec((SUBLANE, c), lambda i, cb=cb: (jnp.maximum(i * t8 - 1, 0), cb)))
        elif kind == "next8":
            in_specs.append(pl.BlockSpec((SUBLANE, c), lambda i, cb=cb: (jnp.minimum((i + 1) * t8, n8 - 1), cb)))
        elif kind == "res":
            d = a.shape[0]
            in_specs.append(pl.BlockSpec((d, tr // d, a.shape[2]), lambda i: (0, i, 0)))
        else:
            raise ValueError(kind)
    out_specs = [pl.BlockSpec((tr, c), lambda i: (i, 0)) if d == 0 else pl.BlockSpec((d, tr // d, c), lambda i: (0, i, 0))
                 for c, _, d in outs]
    out_specs += [pl.BlockSpec(s, lambda i, z=(0,) * len(s): z) for s, _ in accs]
    out_shape = [jax.ShapeDtypeStruct((n_rows, c) if d == 0 else (d, n_rows // d, c), dt) for c, dt, d in outs]
    out_shape += [jax.ShapeDtypeStruct(s, dt) for s, dt in accs]
    res = pl.pallas_call(
        body, name=name, grid=(steps,), in_specs=in_specs, out_specs=out_specs, out_shape=out_shape,
        compiler_params=_cparams(("arbitrary",) if n_acc else ("parallel",)),
    )(*[e[0] for e in ins])
    return res[0] if len(res) == 1 else tuple(res)


def _colsum(x):
    return jnp.sum(x, axis=0, keepdims=True)


def _sum_all(x):
    return jnp.sum(jnp.sum(x, axis=1, keepdims=True), axis=0, keepdims=True)


def _rmsnorm_fwd(name, x, gain):
    def fn(i, n, xt, g):
        r = lax.rsqrt(jnp.mean(xt * xt, axis=-1, keepdims=True) + EPS)
        return (xt * r * g,)
    return _rows(name, fn, [(x, "row"), (gain, "full")], [(x.shape[1], BF16)], tr=512)


FUSED_ROWS = 1024


def _res_norm(acc, res, g):
    x = res + acc
    return x, x * lax.rsqrt(jnp.mean(x * x, axis=-1, keepdims=True) + EPS) * g


def _norm_bwd(dh, x, g, dres):
    r = lax.rsqrt(jnp.mean(x * x, axis=-1, keepdims=True) + EPS)
    xh = x * r
    dxn = dh * g
    dx = dres + r * (dxn - xh * jnp.mean(dxn * xh, axis=-1, keepdims=True))
    return dx, _colsum(dh * xh)


def _norm_bwd_2(dh, x, g, dres):
    dx, dg = _norm_bwd(dh, x, g, dres)
    return dx, dx, dg


def _head_consts():
    import numpy as np
    e = np.arange(A_WIDTH) % A_HEAD_DIM
    inv = (np.float32(ROPE_THETA) ** (-np.arange(0, ROPE_DIM, 2, dtype=np.float32) / np.float32(ROPE_DIM))).astype(np.float32)
    c = np.zeros((8, A_WIDTH), np.float32)
    c[0] = np.where(e < ROPE_DIM, inv[e % ROPE_HALF], 0.0)
    c[1] = np.where(e < ROPE_HALF, -1.0, np.where(e < ROPE_DIM, 1.0, 0.0))
    c[2] = (e < ROPE_HALF).astype(np.float32)
    c[3] = (e < ROPE_DIM).astype(np.float32)
    return jnp.asarray(c)


def _block_diag(scale):
    import numpy as np
    h = np.arange(A_WIDTH) // A_HEAD_DIM
    return jnp.asarray((h[:, None] == h[None, :]).astype(np.float32) * scale, dtype=BF16)


def _seg_sum(x, bd):
    return _dot(x.astype(BF16), bd)


def _rope_tables(positions, consts):
    def fn(i, n, pos, c):
        ang = pos.astype(F32) * c[0:1, :LANE]
        return jnp.cos(ang), jnp.sin(ang) * c[1:2, :LANE]
    return _rows("rope_tables", fn, [(positions, "row"), (consts, "full")], [(LANE, F32), (LANE, F32)], tr=512)


def _rope_wide(t):
    return jnp.concatenate([t] * (A_WIDTH // LANE), axis=1)


def _rope_apply(y, ct, st, low):
    rolled = jnp.where(low, pltpu.roll(y, A_WIDTH - ROPE_HALF, 1), pltpu.roll(y, ROPE_HALF, 1))
    return y * ct + rolled * st


def _rope_apply_bwd(dout, ct, st, low, in16):
    t = dout * st
    back = jnp.where(low, pltpu.roll(t, A_WIDTH - ROPE_HALF, 1), jnp.where(in16, pltpu.roll(t, ROPE_HALF, 1), 0.0))
    return dout * ct + back


def _attn_prep(qkv, gains, ct, st, consts, bd):
    def fn(i, n, t, g, c_t, s_t, c, b):
        low = c[2:3, :] > 0.5
        c_t, s_t = _rope_wide(c_t), _rope_wide(s_t)
        groups = []
        for grp in range(3):
            cols = []
            for which in range(3):
                off = (grp * 3 + which) * A_WIDTH
                x = t[:, off:off + A_WIDTH].astype(F32)
                if which == 2:
                    cols.append(x.astype(BF16))
                    continue
                r = lax.rsqrt(_seg_sum(x * x, b) + EPS)
                y = x * r * g[grp * 2 + which:grp * 2 + which + 1, :]
                cols.append(_rope_apply(y, c_t, s_t, low).astype(BF16))
            groups.append(jnp.concatenate(cols, axis=1))
        return tuple(groups)
    return _rows("attn_prep", fn, [(qkv, "row"), (gains, "full"), (ct, "row"), (st, "row"), (consts, "full"), (bd, "full")],
                 [(3 * A_WIDTH, BF16, d) for _, d in SWA_GROUPS], tr=256)


def _band_mask(n):
    row = lax.broadcasted_iota(jnp.int32, (BAND, 2 * BAND), 0)
    col = lax.broadcasted_iota(jnp.int32, (BAND, 2 * BAND), 1)
    dist = row + BAND - col
    return (dist >= 0) & (dist <= BAND) & ((col >= BAND) | (n > 0))


def _attn_fwd(qkvn, grp):
    d, L, _ = qkvn.shape
    nblk = L // BAND
    assert L % BAND == 0 and d == SWA_GROUPS[grp][1]

    def body(q_ref, kc_ref, kp_ref, vc_ref, vp_ref, o_ref, lse_ref):
        n = pl.program_id(1)
        valid = _band_mask(n)
        first = lax.broadcasted_iota(jnp.int32, (BAND, LANE), 1) < A_HEAD_DIM
        pairs = [slice(pr * LANE, (pr + 1) * LANE) for pr in range(A_WIDTH // LANE)]
        halves = (first, jnp.logical_not(first))
        qps = [q_ref[:, sl] for sl in pairs]
        kcats = [jnp.concatenate([kp_ref[:, sl], kc_ref[:, sl]], axis=0) for sl in pairs]
        vcats = [jnp.concatenate([vp_ref[:, sl], vc_ref[:, sl]], axis=0) for sl in pairs]
        heads = [(pr, m) for pr in range(len(pairs)) for m in halves]
        ss = [_dot(jnp.where(m, qps[pr], jnp.zeros_like(qps[pr])), kcats[pr], 1, 1) for pr, m in heads]
        ps, lses = [], []
        for s in ss:
            s = jnp.where(valid, s * (A_HEAD_DIM ** -0.5), -1e30)
            mx = jnp.max(s, axis=-1, keepdims=True)
            e = jnp.exp(s - mx)
            l = jnp.sum(e, axis=-1, keepdims=True)
            ps.append((e / l).astype(BF16))
            lses.append(mx + jnp.log(l))
        os_ = [_dot(p, vcats[pr]) for p, (pr, _) in zip(ps, heads)]
        o_ref[...] = jnp.concatenate([jnp.where(first, os_[2 * pr], os_[2 * pr + 1]) for pr in range(len(pairs))], axis=1)
        lse_ref[...] = jnp.concatenate([jnp.where(first, lses[2 * pr], lses[2 * pr + 1]) for pr in range(len(pairs))], axis=1)

    blk = (None, BAND, A_WIDTH)
    return pl.pallas_call(
        body, name=f"attn_fwd_g{grp}", grid=(d, nblk),
        in_specs=[pl.BlockSpec(blk, lambda r, n: (r, n, 0)),
                  pl.BlockSpec(blk, lambda r, n: (r, n, 1)),
                  pl.BlockSpec(blk, lambda r, n: (r, jnp.maximum(n - 1, 0), 1)),
                  pl.BlockSpec(blk, lambda r, n: (r, n, 2)),
                  pl.BlockSpec(blk, lambda r, n: (r, jnp.maximum(n - 1, 0), 2))],
        out_specs=[pl.BlockSpec(blk, lambda r, n: (r, n, 0)), pl.BlockSpec(blk, lambda r, n: (r, n, 0))],
        out_shape=[jax.ShapeDtypeStruct((d, L, A_WIDTH), F32)] * 2,
        compiler_params=_cparams(("parallel", "parallel")),
    )(qkvn, qkvn, qkvn, qkvn, qkvn)


def _merge_weights(l0, l1, l2):
    mx = jnp.maximum(jnp.maximum(l0, l1), l2)
    e0, e1, e2 = jnp.exp(l0 - mx), jnp.exp(l1 - mx), jnp.exp(l2 - mx)
    inv = 1.0 / (e0 + e1 + e2)
    return e0 * inv, e1 * inv, e2 * inv


def _attn_merge(os_, lses):
    def fn(i, n, o0, o1, o2, l0, l1, l2):
        w0, w1, w2 = _merge_weights(l0, l1, l2)
        return (w0 * o0 + w1 * o1 + w2 * o2,)
    ins = [(a, "res") for a in (*os_, *lses)]
    return _rows("attn_merge", fn, ins, [(A_WIDTH, BF16)], tr=256)


def _attn_merge_bwd(do, os_, lses, bd1):
    def fn(i, n, dot_, o0, o1, o2, l0, l1, l2, b):
        w0, w1, w2 = _merge_weights(l0, l1, l2)
        o = w0 * o0 + w1 * o1 + w2 * o2
        dsum = _seg_sum(dot_ * o, b)
        return (w0 * dot_, w1 * dot_, w2 * dot_, -w0 * dsum, -w1 * dsum, -w2 * dsum)
    ins = [(do, "row")] + [(a, "res") for a in (*os_, *lses)] + [(bd1, "full")]
    res = _rows("attn_merge_bwd", fn, ins, [(A_WIDTH, dt, d) for dt in (BF16, F32) for _, d in SWA_GROUPS], tr=256)
    return res[:3], res[3:]


def _lane_pick(x, lane_idx, lane):
    return jnp.sum(jnp.where(lane_idx == lane, x, 0.0), axis=-1, keepdims=True)


def _attn_bwd(qkvn, grp, do_g, lse, c_g):
    d, L, _ = qkvn.shape
    nblk = L // BAND

    def body(q_ref, kc_ref, kp_ref, vc_ref, vp_ref, do_ref, lse_ref, c_ref, dq_ref, dk_ref, dv_ref, ck, cv_):
        n = pl.program_id(1)

        @pl.when(n == 0)
        def _():
            ck[...] = jnp.zeros_like(ck)
            cv_[...] = jnp.zeros_like(cv_)

        @pl.when(n < nblk)
        def _():
            valid = _band_mask(n)
            lane = lax.broadcasted_iota(jnp.int32, (BAND, LANE), 1)
            first = lane < A_HEAD_DIM
            lane2 = lax.broadcasted_iota(jnp.int32, (2 * BAND, LANE), 1) < A_HEAD_DIM
            pairs = [slice(pr * LANE, (pr + 1) * LANE) for pr in range(A_WIDTH // LANE)]
            halves = (first, jnp.logical_not(first))
            qps = [q_ref[:, sl] for sl in pairs]
            dops = [do_ref[:, sl] for sl in pairs]
            kcats = [jnp.concatenate([kp_ref[:, sl], kc_ref[:, sl]], axis=0) for sl in pairs]
            vcats = [jnp.concatenate([vp_ref[:, sl], vc_ref[:, sl]], axis=0) for sl in pairs]
            heads = [(pr, hh) for pr in range(len(pairs)) for hh in range(2)]
            zero = jnp.zeros_like(qps[0])
            ss = [_dot(jnp.where(halves[hh], qps[pr], zero), kcats[pr], 1, 1) for pr, hh in heads]
            dps = [_dot(jnp.where(halves[hh], dops[pr], zero), vcats[pr], 1, 1) for pr, hh in heads]
            dss, pbs = [], []
            for (pr, hh), s, dp in zip(heads, ss, dps):
                lse_h = _lane_pick(lse_ref[:, pairs[pr]], lane, hh * A_HEAD_DIM)
                c_h = _lane_pick(c_ref[:, pairs[pr]], lane, hh * A_HEAD_DIM)
                p = jnp.where(valid, jnp.exp(s * (A_HEAD_DIM ** -0.5) - lse_h), 0.0)
                dss.append((p * (dp + c_h) * (A_HEAD_DIM ** -0.5)).astype(BF16))
                pbs.append(p.astype(BF16))
            dqs = [_dot(ds, kcats[pr]) for ds, (pr, _) in zip(dss, heads)]
            dks = [_dot(ds, qps[pr], 0, 0) for ds, (pr, _) in zip(dss, heads)]
            dvs = [_dot(pb, dops[pr], 0, 0) for pb, (pr, _) in zip(pbs, heads)]
            for pr, sl in enumerate(pairs):
                dq_ref[:, sl] = jnp.where(first, dqs[2 * pr], dqs[2 * pr + 1])
                dkc = jnp.where(lane2, dks[2 * pr], dks[2 * pr + 1])
                dvc = jnp.where(lane2, dvs[2 * pr], dvs[2 * pr + 1])
                dk_ref[:, sl] = ck[:, sl] + dkc[:BAND]
                dv_ref[:, sl] = cv_[:, sl] + dvc[:BAND]
                ck[:, sl] = dkc[BAND:]
                cv_[:, sl] = dvc[BAND:]

        @pl.when(n == nblk)
        def _():
            dk_ref[...] = ck[...]
            dv_ref[...] = cv_[...]

    blk = (None, BAND, A_WIDTH)
    last = nblk - 1
    qn = lambda n: jnp.minimum(n, last)
    pn = lambda n: jnp.clip(n - 1, 0, last)
    return tuple(pl.pallas_call(
        body, name=f"attn_bwd_g{grp}", grid=(d, nblk + 1),
        in_specs=[pl.BlockSpec(blk, lambda r, n: (r, qn(n), 0)),
                  pl.BlockSpec(blk, lambda r, n: (r, qn(n), 1)),
                  pl.BlockSpec(blk, lambda r, n: (r, pn(n), 1)),
                  pl.BlockSpec(blk, lambda r, n: (r, qn(n), 2)),
                  pl.BlockSpec(blk, lambda r, n: (r, pn(n), 2)),
                  pl.BlockSpec(blk, lambda r, n: (r, qn(n), 0)),
                  pl.BlockSpec(blk, lambda r, n: (r, qn(n), 0)),
                  pl.BlockSpec(blk, lambda r, n: (r, qn(n), 0))],
        out_specs=[pl.BlockSpec(blk, lambda r, n: (r, qn(n), 0)),
                   pl.BlockSpec(blk, lambda r, n: (r, pn(n), 0)),
                   pl.BlockSpec(blk, lambda r, n: (r, pn(n), 0))],
        out_shape=[jax.ShapeDtypeStruct((d, L, A_WIDTH), F32)] * 3,
        scratch_shapes=[pltpu.VMEM((BAND, A_WIDTH), F32), pltpu.VMEM((BAND, A_WIDTH), F32)],
        compiler_params=_cparams(("parallel", "arbitrary")),
    )(qkvn, qkvn, qkvn, qkvn, qkvn, do_g, lse, c_g))


def _attn_prep_bwd(qkv, grads, gains, ct, st, consts, bd):
    def fn(i, n, t, g, c_t, s_t, c, b, *gr):
        low = c[2:3, :] > 0.5
        in16 = c[3:4, :] > 0.5
        c_t, s_t = _rope_wide(c_t), _rope_wide(s_t)
        cols, dgs = [], []
        for grp in range(3):
            for which in range(3):
                dout = gr[grp * 3 + which]
                if which == 2:
                    cols.append(dout.astype(BF16))
                    continue
                off = (grp * 3 + which) * A_WIDTH
                x = t[:, off:off + A_WIDTH].astype(F32)
                gain = g[grp * 2 + which:grp * 2 + which + 1, :]
                r = lax.rsqrt(_seg_sum(x * x, b) + EPS)
                xh = x * r
                dy = _rope_apply_bwd(dout, c_t, s_t, low, in16)
                dyn = dy * gain
                dx = r * (dyn - xh * _seg_sum(dyn * xh, b))
                cols.append(dx.astype(BF16))
                dgs.append(_colsum(dy * xh))
        return (jnp.concatenate(cols, axis=1), *dgs)
    ins = [(qkv, "row"), (gains, "full"), (ct, "row"), (st, "row"), (consts, "full"), (bd, "full")] + [(a, "res") for a in grads]
    res = _rows("attn_prep_bwd", fn, ins, [(A_QKV, BF16)], tr=128, accs=[((1, A_WIDTH), F32)] * 6)
    return res[0], res[1:]


DN_QKV = 3 * DN_WIDTH
DN_QKVZ = DN_QKV + DN_WIDTH


def _sigmoid(x):
    return jax.nn.sigmoid(x)


def _softplus(x):
    return jnp.maximum(x, 0.0) + jnp.log(1.0 + jnp.exp(-jnp.abs(x)))


def _conv_taps(xs, w, tr):
    acc = None
    for j in range(CONV_W):
        sh = CONV_W - 1 - j
        term = (pltpu.roll(xs, sh, 0) if sh else xs)[SUBLANE:] * w[j:j + 1, :]
        acc = term if acc is None else acc + term
    return acc


def _dn_prep(qkvz, ab, convw, alog_row, dt_row):
    tr = 256

    def fn(i, n, x, xp, abt, w, al, dt):
        xp = jnp.where(i > 0, xp, 0.0)
        u = _conv_taps(jnp.concatenate([xp, x], axis=0), w, tr)
        y = u * _sigmoid(u)
        qs, ks = [], []
        for h in range(DN_HEADS):
            for dst, base, sc in ((qs, 0, DN_DIM ** -0.5), (ks, DN_WIDTH, 1.0)):
                seg = y[:, base + h * DN_DIM:base + (h + 1) * DN_DIM]
                dst.append(seg * (lax.rsqrt(jnp.sum(seg * seg, axis=-1, keepdims=True) + EPS) * sc))
        lane = lax.broadcasted_iota(jnp.int32, abt.shape, 1)
        g = -jnp.exp(al) * _softplus(abt + dt)
        gb = jnp.where(lane < DN_HEADS, g, jnp.where(lane < 2 * DN_HEADS, _sigmoid(abt), 0.0))
        return u, jnp.concatenate(qs, axis=1), jnp.concatenate(ks, axis=1), y[:, 2 * DN_WIDTH:], gb

    ins = [(qkvz, "row", (0, DN_QKV)), (qkvz, "prev8", (0, DN_QKV)), (ab, "row"), (convw, "full"),
           (alog_row, "full"), (dt_row, "full")]
    return _rows("dn_prep", fn, ins, [(DN_QKV, BF16), (DN_WIDTH, F32), (DN_WIDTH, F32), (DN_WIDTH, F32), (LANE, F32)], tr=tr)


def _tri_masks():
    row = lax.broadcasted_iota(jnp.int32, (CHUNK, CHUNK), 0)
    col = lax.broadcasted_iota(jnp.int32, (CHUNK, CHUNK), 1)
    return row >= col, row > col, row == col


def _heads(fn, *lists):
    return [fn(*xs) for xs in zip(*lists)]


def _split(x):
    hi = x.astype(BF16)
    return hi, (x - hi.astype(F32)).astype(BF16)


def _dot3(a, b, ca=1, cb=0):
    (ah, al), (bh, bl) = a, b
    return _dot(ah, bh, ca, cb) + (_dot(ah, bl, ca, cb) + _dot(al, bh, ca, cb))


SPLIT_STEPS = 3


def _unit_lower_inverse(a_list, eye):
    ts = [eye - a for a in a_list]
    parts = [_split(a) for a in a_list]
    for step in range(5):
        if step < SPLIT_STEPS:
            parts = [_split(_dot3(p, p)) for p in parts]
            ts = [t + _dot3(_split(t), p) for t, p in zip(ts, parts)]
        else:
            parts = [(_dot(p[0], p[0]).astype(BF16), None) for p in parts]
            ts = [t + _dot(t.astype(BF16), p[0]) for t, p in zip(ts, parts)]
    return ts


def _dn_terms(qs, ks, vs, gb, solved=None):
    lower, strict, diag = _tri_masks()
    lane = lax.broadcasted_iota(jnp.int32, (CHUNK, LANE), 1)
    is_last = lax.broadcasted_iota(jnp.int32, (CHUNK, 1), 0) == CHUNK - 1
    hs = range(DN_HEADS)
    gc = _dot(lower.astype(F32), gb, precision=HIGHEST)
    gct = jnp.transpose(gc)
    bcol = [_lane_pick(gb, lane, DN_HEADS + h) for h in hs]
    gcol = [_lane_pick(gc, lane, h) for h in hs]
    glast = [jnp.sum(jnp.where(is_last, g, 0.0), axis=0, keepdims=True) for g in gcol]
    decay = [jnp.exp(jnp.where(lower, gcol[h] - gct[h:h + 1, :], -1e30)) for h in hs]
    kb = _heads(lambda k, b: k * b, ks, bcol)
    both = _heads(lambda q, x, k: _bdot(jnp.concatenate([q, x], axis=0), k, 1, 1), qs, kb, ks)
    qk = [x[:CHUNK] for x in both]
    kk = [x[CHUNK:] for x in both]
    a = _heads(lambda x, d: jnp.where(strict, x * d, 0.0), kk, decay)
    eg = [jnp.exp(g) for g in gcol]
    egl = _heads(lambda gl, g: jnp.exp(gl - g), glast, gcol)
    rhs_w = _heads(lambda x, e: x * e, kb, eg)
    if solved is None:
        t_full = _unit_lower_inverse(a, diag.astype(F32))
        t = [_split(x) for x in t_full]
        uw = _heads(lambda tt, v, b, r: _dot3(tt, _split(jnp.concatenate([v * b, r], axis=1))), t, vs, bcol, rhs_w)
        u = [x[:, :DN_DIM] for x in uw]
        w = [x[:, DN_DIM:] for x in uw]
    else:
        t_full, u, w = solved
        t = [_split(x) for x in t_full]
    return dict(bcol=bcol, decay=decay, kb=kb, a=a, t=t, t_full=t_full, eg=eg, egl=egl, rhs_w=rhs_w, u=u, w=w,
                attn=_heads(lambda x, d: x * d, qk, decay), q_dec=_heads(lambda q, e: q * e, qs, eg),
                k_dec=_heads(lambda k, e: k * e, ks, egl), c_dec=[jnp.exp(g) for g in glast],
                lower=lower, strict=strict, lane=lane, is_last=is_last)


def _head_slices(ref):
    return [ref[:, h * DN_DIM:(h + 1) * DN_DIM] for h in range(DN_HEADS)]


def _dn_chunk_fwd(q, k, v, gb):
    S = q.shape[0]
    N = S // CHUNK

    def body(q_ref, k_ref, v_ref, gb_ref, o_ref, st_ref, t_ref, u_ref, w_ref, state):
        @pl.when(pl.program_id(0) == 0)
        def _():
            state[...] = jnp.zeros_like(state)

        f = _dn_terms(_head_slices(q_ref), _head_slices(k_ref), _head_slices(v_ref), gb_ref[...])
        s = [state[h] for h in range(DN_HEADS)]
        for h in range(DN_HEADS):
            st_ref[0, h] = s[h]
            t_ref[0, h] = f["t_full"][h]
            u_ref[:, h * DN_DIM:(h + 1) * DN_DIM] = f["u"][h]
            w_ref[:, h * DN_DIM:(h + 1) * DN_DIM] = f["w"][h]
        sb = [x.astype(BF16) for x in s]
        v_new = _heads(lambda u, w, x: u - _bdot(w, x), f["u"], f["w"], sb)
        o = _heads(lambda qd, x, at, vn: _bdot(qd, x) + _bdot(at, vn), f["q_dec"], sb, f["attn"], v_new)
        new_s = _heads(lambda x, c, kd, vn: x * c + _bdot(kd, vn, 0, 0), s, f["c_dec"], f["k_dec"], v_new)
        for h in range(DN_HEADS):
            o_ref[:, h * DN_DIM:(h + 1) * DN_DIM] = o[h]
            state[h] = new_s[h]

    blk = pl.BlockSpec((CHUNK, DN_WIDTH), lambda n: (n, 0))
    st_blk = pl.BlockSpec((1, DN_HEADS, DN_DIM, DN_DIM), lambda n: (n, 0, 0, 0))
    t_blk = pl.BlockSpec((1, DN_HEADS, CHUNK, CHUNK), lambda n: (n, 0, 0, 0))
    wide = jax.ShapeDtypeStruct((S, DN_WIDTH), F32)
    o, states, t, u, w = pl.pallas_call(
        body, name="dn_chunk_fwd", grid=(N,),
        in_specs=[blk, blk, blk, pl.BlockSpec((CHUNK, LANE), lambda n: (n, 0))],
        out_specs=[blk, st_blk, t_blk, blk, blk],
        out_shape=[wide, jax.ShapeDtypeStruct((N, DN_HEADS, DN_DIM, DN_DIM), F32),
                   jax.ShapeDtypeStruct((N, DN_HEADS, CHUNK, CHUNK), F32), wide, wide],
        scratch_shapes=[pltpu.VMEM((DN_HEADS, DN_DIM, DN_DIM), F32)],
        compiler_params=_cparams(("arbitrary",)),
    )(q, k, v, gb)
    return o, (states, t, u, w)


def _dn_chunk_bwd(q, k, v, gb, saved, do):
    S = q.shape[0]
    N = S // CHUNK
    states, t_saved, u_saved, w_saved = saved

    def body(q_ref, k_ref, v_ref, gb_ref, st_ref, t_ref, u_ref, w_ref, do_ref, dq_ref, dk_ref, dv_ref, dgb_ref, dstate):
        @pl.when(pl.program_id(0) == 0)
        def _():
            dstate[...] = jnp.zeros_like(dstate)

        hs = range(DN_HEADS)
        qs, ks, vs, dos = (_head_slices(r) for r in (q_ref, k_ref, v_ref, do_ref))
        f = _dn_terms(qs, ks, vs, gb_ref[...], ([t_ref[0, h] for h in hs], _head_slices(u_ref), _head_slices(w_ref)))
        lane, is_last = f["lane"], f["is_last"]
        rowsum = lambda x: jnp.sum(x, axis=-1, keepdims=True)
        s = [st_ref[0, h] for h in hs]
        dsn = [dstate[h] for h in hs]
        sb = [x.astype(BF16) for x in s]
        dsb = [x.astype(BF16) for x in dsn]
        dob = [x.astype(BF16) for x in dos]
        v_new = _heads(lambda u, w, x: u - _bdot(w, x), f["u"], f["w"], sb)
        dv_new = _heads(lambda at, d, kd, x: _bdot(at, d, 0, 0) + _bdot(kd, x), f["attn"], dob, f["k_dec"], dsb)
        dattn = _heads(lambda d, vn: _bdot(d, vn, 1, 1), dob, v_new)
        dq_dec = _heads(lambda d, x: _bdot(d, x, 1, 1), dob, sb)
        dk_dec = _heads(lambda vn, x: _bdot(vn, x, 1, 1), v_new, dsb)
        dw = _heads(lambda dv_, x: -_bdot(dv_, x, 1, 1), dv_new, sb)
        new_ds = _heads(lambda x, c, qd, d, w, dv_: x * c + _bdot(qd, d, 0, 0) - _bdot(w, dv_, 0, 0),
                        dsn, f["c_dec"], f["q_dec"], dob, f["w"], dv_new)
        for h in hs:
            dstate[h] = new_ds[h]
        drhs = _heads(lambda tt, x, y: _dot3(tt, _split(jnp.concatenate([x, y], axis=1)), 0, 0), f["t"], dv_new, dw)
        drhs_u = [x[:, :DN_DIM] for x in drhs]
        drhs_w = [x[:, DN_DIM:] for x in drhs]
        da = _heads(lambda du_, u, dw_, w: jnp.where(f["strict"], -(_bdot(du_, u, 1, 1) + _bdot(dw_, w, 1, 1)), 0.0),
                    drhs_u, f["u"], drhs_w, f["w"])
        dkk = _heads(lambda x, d: x * d, da, f["decay"])
        dqk = _heads(lambda x, d: x * d, dattn, f["decay"])
        by_k = _heads(lambda x, y, k_: _bdot(jnp.concatenate([x, y], axis=0), k_), dqk, dkk, ks)
        dq = _heads(lambda x, dqd, e: x[:CHUNK] + dqd * e, by_k, dq_dec, f["eg"])
        dkb = _heads(lambda x, dw_, e: x[CHUNK:] + dw_ * e, by_k, drhs_w, f["eg"])
        dk = _heads(lambda x, kb_, y, q_, dkd, el, dkb_, b: _bdot(x, kb_, 0, 0) + _bdot(y, q_, 0, 0) + dkd * el + dkb_ * b,
                    dkk, f["kb"], dqk, qs, dk_dec, f["egl"], dkb, f["bcol"])
        m = _heads(lambda x, a_, y, at: x * a_ + y * at, da, f["a"], dattn, f["attn"])
        ones = jnp.ones((CHUNK, LANE), BF16)
        col_m = [(_dot(mh, ones, 0, 0) + _dot(ml, ones, 0, 0))[:, 0:1] for mh, ml in map(_split, m)]
        dgc_all = jnp.zeros((CHUNK, LANE), F32)
        dbeta_all = jnp.zeros((CHUNK, LANE), F32)
        for h in hs:
            dq_ref[:, h * DN_DIM:(h + 1) * DN_DIM] = dq[h]
            dk_ref[:, h * DN_DIM:(h + 1) * DN_DIM] = dk[h]
            dv_ref[:, h * DN_DIM:(h + 1) * DN_DIM] = drhs_u[h] * f["bcol"][h]
            kdec_term = rowsum(dk_dec[h] * f["k_dec"][h])
            dc_dec = _sum_all(dsn[h] * s[h])
            dgc = (rowsum(m[h]) - col_m[h] + rowsum(dq_dec[h] * f["q_dec"][h]) - kdec_term
                   + rowsum(drhs_w[h] * f["rhs_w"][h]))
            last_extra = jnp.sum(kdec_term, axis=0, keepdims=True) + dc_dec * f["c_dec"][h]
            dgc = dgc + jnp.where(is_last, last_extra, 0.0)
            dbeta = rowsum(drhs_u[h] * vs[h]) + rowsum(dkb[h] * ks[h])
            dgc_all = jnp.where(lane == h, dgc, dgc_all)
            dbeta_all = jnp.where(lane == DN_HEADS + h, dbeta, dbeta_all)
        dg_all = _dot(f["lower"].astype(F32), dgc_all, 0, 0, precision=HIGHEST)
        dgb_ref[...] = jnp.where(lane < DN_HEADS, dg_all, dbeta_all)

    rev = lambda n: (N - 1 - n, 0)
    blk = pl.BlockSpec((CHUNK, DN_WIDTH), rev)
    gblk = pl.BlockSpec((CHUNK, LANE), rev)
    st_blk = pl.BlockSpec((1, DN_HEADS, DN_DIM, DN_DIM), lambda n: (N - 1 - n, 0, 0, 0))
    t_blk = pl.BlockSpec((1, DN_HEADS, CHUNK, CHUNK), lambda n: (N - 1 - n, 0, 0, 0))
    return pl.pallas_call(
        body, name="dn_chunk_bwd", grid=(N,),
        in_specs=[blk, blk, blk, gblk, st_blk, t_blk, blk, blk, blk],
        out_specs=[blk, blk, blk, gblk],
        out_shape=[jax.ShapeDtypeStruct((S, DN_WIDTH), F32)] * 3 + [jax.ShapeDtypeStruct((S, LANE), F32)],
        scratch_shapes=[pltpu.VMEM((DN_HEADS, DN_DIM, DN_DIM), F32)],
        compiler_params=_cparams(("arbitrary",)),
    )(q, k, v, gb, states, t_saved, u_saved, w_saved, do)


def _dn_post(o, qkvz, gain_row):
    def fn(i, n, ot, z, g):
        cols = []
        for h in range(DN_HEADS):
            seg = ot[:, h * DN_DIM:(h + 1) * DN_DIM]
            cols.append(seg * lax.rsqrt(jnp.mean(seg * seg, axis=-1, keepdims=True) + EPS) * g)
        return (jnp.concatenate(cols, axis=1) * (z * _sigmoid(z)),)
    return _rows("dn_post", fn, [(o, "row"), (qkvz, "row", (3, DN_WIDTH)), (gain_row, "full")], [(DN_WIDTH, BF16)], tr=512)


def _dn_post_bwd(don, o, qkvz, gain_row):
    def fn(i, n, dy, ot, z, g):
        sg = _sigmoid(z)
        sz = z * sg
        dos, ohs = [], []
        dg = jnp.zeros((1, DN_DIM), F32)
        for h in range(DN_HEADS):
            sl = slice(h * DN_DIM, (h + 1) * DN_DIM)
            seg = ot[:, sl]
            r = lax.rsqrt(jnp.mean(seg * seg, axis=-1, keepdims=True) + EPS)
            oh = seg * r
            dno = dy[:, sl] * sz[:, sl]
            dg = dg + _colsum(dno * oh)
            dn = dno * g
            dos.append(r * (dn - oh * jnp.mean(dn * oh, axis=-1, keepdims=True)))
            ohs.append(oh * g)
        dz = dy * jnp.concatenate(ohs, axis=1) * (sg * (1.0 + z * (1.0 - sg)))
        return jnp.concatenate(dos, axis=1), dz, dg
    ins = [(don, "row"), (o, "row"), (qkvz, "row", (3, DN_WIDTH)), (gain_row, "full")]
    return _rows("dn_post_bwd", fn, ins, [(DN_WIDTH, F32), (DN_WIDTH, F32)], tr=256, accs=[((1, DN_DIM), F32)])


def _dn_prep_bwd(dq, dk, dv, dgb, u, ab, alog_row, dt_row):
    def fn(i, n, dqt, dkt, dvt, dgbt, ut, abt, al, dt):
        ut = ut.astype(F32)
        sg = _sigmoid(ut)
        y = ut * sg
        dys = []
        for grad, base, sc in ((dqt, 0, DN_DIM ** -0.5), (dkt, DN_WIDTH, 1.0)):
            for h in range(DN_HEADS):
                seg = y[:, base + h * DN_DIM:base + (h + 1) * DN_DIM]
                gr = grad[:, h * DN_DIM:(h + 1) * DN_DIM]
                r = lax.rsqrt(jnp.sum(seg * seg, axis=-1, keepdims=True) + EPS)
                xh = seg * r
                dys.append((r * sc) * (gr - xh * jnp.sum(gr * xh, axis=-1, keepdims=True)))
        dy = jnp.concatenate(dys + [dvt], axis=1)
        du = dy * (sg * (1.0 + ut * (1.0 - sg)))
        lane = lax.broadcasted_iota(jnp.int32, abt.shape, 1)
        is_g = lane < DN_HEADS
        ea = jnp.exp(al)
        x = abt + dt
        slope = -ea * _sigmoid(x)
        gval = -ea * _softplus(x)
        dg = jnp.where(is_g, dgbt, 0.0)
        beta = _sigmoid(abt)
        dab = jnp.where(is_g, dg * slope, jnp.where(lane < 2 * DN_HEADS, dgbt * beta * (1.0 - beta), 0.0))
        return du, dab, _colsum(dg * gval), _colsum(dg * slope)
    ins = [(dq, "row"), (dk, "row"), (dv, "row"), (dgb, "row"), (u, "row"), (ab, "row"), (alog_row, "full"), (dt_row, "full")]
    return _rows("dn_prep_bwd", fn, ins, [(DN_QKV, F32), (LANE, BF16)], tr=256, accs=[((1, LANE), F32)] * 2)


def _dn_conv_bwd(du, dz, qkvz, convw):
    tr = 256

    def fn(i, n, dut, dun, dzt, x, xp, w):
        dun = jnp.where(i < n - 1, dun, 0.0)
        dus = jnp.concatenate([dut, dun], axis=0)
        xs = jnp.concatenate([jnp.where(i > 0, xp, 0.0), x], axis=0)
        dx = None
        dws = []
        for j in range(CONV_W):
            sh = CONV_W - 1 - j
            term = (pltpu.roll(dus, tr + SUBLANE - sh, 0) if sh else dus)[:tr] * w[j:j + 1, :]
            dx = term if dx is None else dx + term
            dws.append(_colsum(dut * (pltpu.roll(xs, sh, 0) if sh else xs)[SUBLANE:]))
        return (jnp.concatenate([dx.astype(BF16), dzt.astype(BF16)], axis=1), *dws)

    ins = [(du, "row"), (du, "next8"), (dz, "row"), (qkvz, "row", (0, DN_QKV)), (qkvz, "prev8", (0, DN_QKV)), (convw, "full")]
    res = _rows("dn_conv_bwd", fn, ins, [(DN_QKVZ, BF16)], tr=tr, accs=[((1, DN_QKV), F32)] * CONV_W)
    return res[0], res[1:]


def _add(acc, r):
    return (r + acc,)


def _mlp_ple_fwd(i, x1, hm, p_i, ple_gain, next_gain, w_up, w_down, w_ple, w_gate, target=None):
    u, a = _mm(f"mlp_up{i}", hm, w_up, epilogue=lambda acc: (acc, jnp.square(jnp.maximum(acc, 0.0))),
               out_dtypes=(BF16, BF16))
    x2, hp = _mm(f"mlp_down{i}", a, w_down, epilogue=_res_norm, extras=(x1, ple_gain), out_dtypes=(F32, BF16),
                 tm_pref=FUSED_ROWS)
    pp = _mm(f"ple_proj{i}", p_i, w_ple)

    def gate_epilogue(acc, x2t, ppt, g):
        gate = _sigmoid(acc)
        x3 = x2t + ppt * gate
        return x3, gate, x3 * lax.rsqrt(jnp.mean(x3 * x3, axis=-1, keepdims=True) + EPS) * g

    def loss_epilogue(acc, x2t, ppt, tt):
        gate = _sigmoid(acc)
        err = x2t + ppt * gate - tt
        dy = err * (1.0 / D_MODEL)
        return dy, dy * gate, dy * ppt * gate * (1.0 - gate), _colsum(err * err)

    saved = dict(x1=x1, hm=hm, u=u, a=a, x2=x2, hp=hp, pp=pp, p=p_i)
    if target is None:
        x3, saved["gate"], h_next = _mm(f"ple_gate{i}", hp, w_gate, epilogue=gate_epilogue, extras=(x2, pp, next_gain),
                                        out_dtypes=(F32, F32, BF16), tm_pref=FUSED_ROWS)
        return x3, h_next, saved
    dy, saved["dpp"], saved["dzg"], sq = _mm(f"ple_gate{i}", hp, w_gate, epilogue=loss_epilogue, extras=(x2, pp, target),
                                             out_dtypes=(F32, BF16, BF16), n_colsums=1, tm_pref=FUSED_ROWS)
    return dy, sq, saved


def _mlp_ple_bwd(i, dx3, sv, mlp_gain, ple_gain, w_up, w_down, w_gate):
    if "dpp" in sv:
        dpp, dzg = sv["dpp"], sv["dzg"]
    else:
        def fn(_i, _n, d, g, pp):
            return d * g, d * pp * g * (1.0 - g)
        dpp, dzg = _rows(f"ple_gate_bwd{i}", fn, [(dx3, "row"), (sv["gate"], "row"), (sv["pp"], "row")],
                         [(D_MODEL, BF16), (D_MODEL, BF16)], tr=512)
    d_w_ple = _mm(f"ple_proj_dw{i}", sv["p"], dpp, ta=True, out_dtypes=(BF16,))
    d_w_gate = _mm(f"ple_gate_dw{i}", sv["hp"], dzg, ta=True, out_dtypes=(BF16,))
    dx2, dx2b, d_ple_gain = _mm(f"ple_gate_dx{i}", dzg, w_gate, tb=True, epilogue=_norm_bwd_2,
                                extras=(sv["x2"], ple_gain, dx3), out_dtypes=(F32, BF16), n_colsums=1, tm_pref=FUSED_ROWS)
    d_w_down = _mm(f"mlp_down_dw{i}", sv["a"], dx2b, ta=True, out_dtypes=(BF16,))
    du = _mm(f"mlp_down_dx{i}", dx2b, w_down, tb=True,
             epilogue=lambda acc, ut: (acc * (2.0 * jnp.maximum(ut.astype(F32), 0.0)),), extras=(sv["u"],), out_dtypes=(BF16,))
    d_w_up = _mm(f"mlp_up_dw{i}", sv["hm"], du, ta=True, out_dtypes=(BF16,))
    dx1, dx1b, d_mlp_gain = _mm(f"mlp_up_dx{i}", du, w_up, tb=True, epilogue=_norm_bwd_2,
                                extras=(sv["x1"], mlp_gain, dx2), out_dtypes=(F32, BF16), n_colsums=1, tm_pref=FUSED_ROWS)
    return dx1, dx1b, dict(w_ple=d_w_ple, w_ple_gate=d_w_gate, w_down=d_w_down, w_up=d_w_up,
                           ple_norm=d_ple_gain, mlp_norm=d_mlp_gain)


def _after(small, token):
    return small + token[0:1, 0:1]


def _local_step(x, p, positions, target, W, P, rest_of_weights, send_layer1, send_mlp0, send_attn):
    consts = _head_consts()
    bd = _block_diag(1.0 / A_HEAD_DIM)
    bd1 = _block_diag(1.0)
    ct, st = _rope_tables(positions, consts)
    gains = jnp.stack([jnp.tile(v, A_HEADS) for g in range(3) for v in (P["attn_q_gain"][g], P["attn_k_gain"][g])])
    pad = LANE - DN_HEADS
    alog_row = jnp.pad(P["dn_a_log"].reshape(1, DN_HEADS), ((0, 0), (0, pad)))
    dt_row = jnp.pad(P["dn_dt_bias"].reshape(1, DN_HEADS), ((0, 0), (0, pad)))
    ogain_row = P["dn_o_gain"].reshape(1, DN_DIM)
    row = lambda name, i: P[name][i:i + 1]

    h0 = _rmsnorm_fwd("mix_norm0", x, row("mix_norm", 0))
    qkv = _mm("attn_qkv", h0, W["attn_w_qkv"], out_dtypes=(BF16,))
    qkvn = _attn_prep(qkv, gains, ct, st, consts, bd)
    os_, lses = zip(*[_attn_fwd(qkvn[g], g) for g in range(3)])
    o_attn = _attn_merge(os_, lses)
    x1, hm0 = _mm("attn_out", o_attn, W["attn_w_o"], epilogue=_res_norm, extras=(x, row("mlp_norm", 0)),
                  out_dtypes=(F32, BF16), tm_pref=FUSED_ROWS)
    W = {**W, **rest_of_weights(x1)}
    x3, h1, sv0 = _mlp_ple_fwd(0, x1, hm0, p[0], row("ple_norm", 0), row("mix_norm", 1),
                               W["w_up"][0], W["w_down"][0], W["w_ple"][0], W["w_ple_gate"][0])
    qkvz = _mm("dn_in_qkvz", h1, W["dn_w_qkvz"])
    ab = _mm("dn_in_ab", h1, W["dn_w_ab"])
    u, q, k, v, gb = _dn_prep(qkvz, ab, W["dn_conv"], alog_row, dt_row)
    o_dn, states = _dn_chunk_fwd(q, k, v, gb)
    on = _dn_post(o_dn, qkvz, ogain_row)
    x4, hm1 = _mm("dn_out", on, W["dn_w_o"], epilogue=_res_norm, extras=(x3, row("mlp_norm", 1)),
                  out_dtypes=(F32, BF16), tm_pref=FUSED_ROWS)
    dy, sq, sv1 = _mlp_ple_fwd(1, x4, hm1, p[1], row("ple_norm", 1), None,
                               W["w_up"][1], W["w_down"][1], W["w_ple"][1], W["w_ple_gate"][1], target=target)

    dx4, dx4b, g1 = _mlp_ple_bwd(1, dy, sv1, row("mlp_norm", 1), row("ple_norm", 1),
                                 W["w_up"][1], W["w_down"][1], W["w_ple_gate"][1])
    don = _mm("dn_out_dx", dx4b, W["dn_w_o"], tb=True)
    d_dn_w_o = _mm("dn_out_dw", on, dx4b, ta=True, out_dtypes=(BF16,))
    do_dn, dz, d_ogain = _dn_post_bwd(don, o_dn, qkvz, ogain_row)
    dq, dk, dv, dgb = _dn_chunk_bwd(q, k, v, gb, states, do_dn)
    du, dab, d_alog, d_dt = _dn_prep_bwd(dq, dk, dv, dgb, u, ab, alog_row, dt_row)
    dqkvz, d_conv = _dn_conv_bwd(du, dz, qkvz, W["dn_conv"])
    dh1 = _mm("dn_in_ab_dx", dab, W["dn_w_ab"], tb=True)
    dx3, d_mix1 = _mm("dn_in_qkvz_dx", dqkvz, W["dn_w_qkvz"], tb=True,
                      epilogue=lambda acc, part, xt, g, dres: _norm_bwd(acc + part, xt, g, dres),
                      extras=(dh1, x3, row("mix_norm", 1), dx4), n_colsums=1, tm_pref=FUSED_ROWS, tk_pref=MXU_TILE)
    d_w_qkvz = _mm("dn_in_qkvz_dw", h1, dqkvz, ta=True, out_dtypes=(BF16,))
    d_w_ab = _mm("dn_in_ab_dw", h1, dab, ta=True, out_dtypes=(BF16,))
    token = send_layer1(dict(
        dn_w_qkvz=d_w_qkvz, dn_w_ab=d_w_ab, dn_conv=jnp.concatenate(d_conv, 0), dn_w_o=d_dn_w_o,
        w_up=g1["w_up"], w_down=g1["w_down"], w_ple=g1["w_ple"], w_ple_gate=g1["w_ple_gate"]))
    dx1, dx1b, g0 = _mlp_ple_bwd(0, dx3, sv0, row("mlp_norm", 0), _after(row("ple_norm", 0), token),
                                 W["w_up"][0], W["w_down"][0], W["w_ple_gate"][0])
    token = send_mlp0(dict(w_up=g0["w_up"], w_down=g0["w_down"], w_ple=g0["w_ple"], w_ple_gate=g0["w_ple_gate"]))
    do_attn = _mm("attn_out_dx", dx1b, W["attn_w_o"], tb=True, epilogue=_add, extras=(_after(jnp.zeros((1, A_WIDTH), F32), token),))
    d_attn_w_o = _mm("attn_out_dw", o_attn, dx1b, ta=True, out_dtypes=(BF16,))
    dos, cs = _attn_merge_bwd(do_attn, os_, lses, bd1)
    grads9 = []
    for g in range(3):
        grads9 += list(_attn_bwd(qkvn[g], g, dos[g], lses[g], cs[g]))
    dqkv, dgains = _attn_prep_bwd(qkv, grads9, gains, ct, st, consts, bd)
    d_attn_w_qkv = _mm("attn_qkv_dw", h0, dqkv, ta=True, out_dtypes=(BF16,))
    token = send_attn(dict(attn_w_qkv=d_attn_w_qkv, attn_w_o=d_attn_w_o))
    dx0, d_mix0 = _mm("attn_qkv_dx", dqkv, W["attn_w_qkv"], tb=True, epilogue=_norm_bwd,
                      extras=(x, _after(row("mix_norm", 0), token), dx1), n_colsums=1, tm_pref=FUSED_ROWS,
                      tk_pref=A_QKV // 3)

    dg = jnp.stack([t.reshape(A_HEADS, A_HEAD_DIM).sum(0) for t in dgains])
    small = dict(
        mix_norm=jnp.concatenate([d_mix0, d_mix1], 0),
        attn_q_gain=dg[0::2][None], attn_k_gain=dg[1::2][None],
        dn_a_log=d_alog[:, :DN_HEADS], dn_dt_bias=d_dt[:, :DN_HEADS], dn_o_gain=d_ogain,
        mlp_norm=jnp.concatenate([g0["mlp_norm"], g1["mlp_norm"]], 0),
        ple_norm=jnp.concatenate([g0["ple_norm"], g1["ple_norm"]], 0),
    )
    return sq, dx0, small


MESH_IDS = pl.DeviceIdType.MESH
ANY = pl.BlockSpec(memory_space=pl.ANY)


def _place():
    return lax.axis_index("x"), lax.axis_index("y"), lax.axis_index("c")


def _sem_scratch(n_streams):
    return [pltpu.SemaphoreType.DMA((n_streams, N_DEV - 1)), pltpu.SemaphoreType.DMA((n_streams, N_DEV - 1)),
            pltpu.SemaphoreType.DMA((n_streams,))]


def _all_gather(name, arrays, streams):
    n_in, n_st = len(arrays), len(streams)
    shapes = [arrays[a].shape if li is None else arrays[a].shape[1:] for a, li in streams]

    def body(*refs):
        in_refs, out_refs, token = refs[:n_in], refs[n_in:n_in + n_st], refs[n_in + n_st]
        send_sems, recv_sems, local_sems = refs[n_in + n_st + 1:]
        token[...] = jnp.zeros_like(token)
        x, y, c = _place()
        me, sibling = (x, y, c), (x, y, 1 - c)
        chips = [(1 - x, y), (x, 1 - y), (1 - x, 1 - y)]

        def copy(s, k, block, to, own=False):
            a, li = streams[s]
            dst = out_refs[s].at[4 * block[0] + 2 * block[1] + block[2]]
            src = (in_refs[a] if li is None else in_refs[a].at[li]) if own else dst
            return pltpu.make_async_remote_copy(src_ref=src, dst_ref=dst, send_sem=send_sems.at[s, k],
                                                recv_sem=recv_sems.at[s, k], device_id=to, device_id_type=MESH_IDS)

        started = []
        for s, (a, li) in enumerate(streams):
            src = in_refs[a] if li is None else in_refs[a].at[li]
            mine = pltpu.make_async_copy(src, out_refs[s].at[4 * x + 2 * y + c], local_sems.at[s])
            mine.start()
            started.append(mine)
        sends = []
        for s in range(n_st):
            first = [copy(s, 0, me, sibling, own=True)]
            first += [copy(s, 1 + j, me, (*chip, c), own=True) for j, chip in enumerate(chips)]
            for cp in first:
                cp.start()
            sends += first
        for j, chip in enumerate(chips):
            for s in range(n_st):
                copy(s, 1 + j, (*chip, c), me).wait_recv()
                fwd = copy(s, 4 + j, (*chip, c), sibling)
                fwd.start()
                sends.append(fwd)
        for s in range(n_st):
            copy(s, 0, sibling, me).wait_recv()
            for j, chip in enumerate(chips):
                copy(s, 4 + j, (*chip, 1 - c), me).wait_recv()
        for cp in sends:
            cp.wait_send()
        for cp in started:
            cp.wait()

    res = pl.pallas_call(
        body, name=name,
        out_shape=[jax.ShapeDtypeStruct((N_DEV,) + tuple(sh), arrays[a].dtype) for sh, (a, _) in zip(shapes, streams)]
        + [jax.ShapeDtypeStruct((SUBLANE, LANE), F32)],
        in_specs=[ANY] * n_in, out_specs=[ANY] * n_st + [pl.BlockSpec(memory_space=pltpu.VMEM)],
        scratch_shapes=_sem_scratch(n_st),
    )(*arrays)
    return list(res[:n_st]), res[n_st]


HBM = pl.BlockSpec(memory_space=pltpu.HBM)
SEM = pl.BlockSpec(memory_space=pltpu.SEMAPHORE)
FLOWS = pltpu.CompilerParams(has_side_effects=pltpu.SideEffectType.DATAFLOW_SIDE_EFFECTING)


def _in_hbm(a):
    return pltpu.with_memory_space_constraint(a, pltpu.HBM)


def _hbm_like(a):
    return pltpu.HBM(a.shape, a.dtype)


def _peers(x, y, c):
    return [(1 - x if k & 4 else x, 1 - y if k & 2 else y, 1 - c if k & 1 else c) for k in range(1, N_DEV)]


def _start_copies(name, n_remote, n_own, make_copies, operands):
    n = len(operands)

    def body(*refs):
        for cp in make_copies(refs[:n], refs[n], refs[n + 1], refs[n + 2]):
            cp.start()
        refs[-1][...] = jnp.zeros_like(refs[-1])

    res = pl.pallas_call(
        body, name=name,
        out_shape=(pltpu.SemaphoreType.DMA((n_remote,)), pltpu.SemaphoreType.DMA((n_remote,)), pltpu.SemaphoreType.DMA((n_own,)),
                   *[_hbm_like(t) for t in operands], jax.ShapeDtypeStruct((SUBLANE, LANE), F32)),
        in_specs=[HBM] * n, out_specs=(SEM, SEM, SEM, *[HBM] * n, pl.BlockSpec(memory_space=pltpu.VMEM)),
        input_output_aliases={i: 3 + i for i in range(n)}, compiler_params=FLOWS,
    )(*[_in_hbm(t) for t in operands])
    return res[:3], list(res[3:3 + n]), res[-1]


def _wait_copies(name, make_waits, sems, operands, after):
    n = len(operands)

    def body(*refs):
        for wait in make_waits(refs[:n], refs[n], refs[n + 1], refs[n + 2]):
            wait()

    res = pl.pallas_call(
        body, name=name, out_shape=tuple(_hbm_like(t) for t in operands),
        in_specs=[HBM] * n + [SEM, SEM, SEM, ANY], out_specs=tuple([HBM] * n),
        input_output_aliases={i: i for i in range(n)}, compiler_params=FLOWS,
    )(*operands, *sems, after)
    return list(res)


def _gather_plan(n_in, streams):
    def block(arr, s):
        a, li = streams[s]
        return arr[a] if li is None else arr[a].at[li]

    def copies(refs, send_sems, recv_sems, own_sems, arrivals=False):
        arr, land = refs[:n_in], refs[n_in:]
        x, y, c = _place()
        me = 4 * x + 2 * y + c
        out = []
        for s in range(len(streams)):
            out.append(("own", pltpu.make_async_copy(block(arr, s), land[s].at[me], own_sems.at[s])))
            for k, (px, py, pc) in enumerate(_peers(x, y, c)):
                out.append(("remote", pltpu.make_async_remote_copy(
                    src_ref=block(arr, s), dst_ref=land[s].at[4 * px + 2 * py + pc if arrivals else me],
                    send_sem=send_sems.at[s * (N_DEV - 1) + k], recv_sem=recv_sems.at[s * (N_DEV - 1) + k],
                    device_id=(px, py, pc), device_id_type=MESH_IDS)))
        return out
    return copies


def _exchange_plan(n_st):
    def copies(refs, send_sems, recv_sems, own_sems, arrivals=False):
        snd, rcv = refs[:n_st], refs[n_st:]
        x, y, c = _place()
        me = 4 * x + 2 * y + c
        out = []
        for s in range(n_st):
            out.append(("own", pltpu.make_async_copy(snd[s].at[me], rcv[s].at[me], own_sems.at[s])))
            for k, (px, py, pc) in enumerate(_peers(x, y, c)):
                peer = 4 * px + 2 * py + pc
                out.append(("remote", pltpu.make_async_remote_copy(
                    src_ref=snd[s].at[peer], dst_ref=rcv[s].at[peer if arrivals else me],
                    send_sem=send_sems.at[s * (N_DEV - 1) + k], recv_sem=recv_sems.at[s * (N_DEV - 1) + k],
                    device_id=(px, py, pc), device_id_type=MESH_IDS)))
        return out
    return copies


def _split_transfer(tag, plan, n_streams, operands):
    sems, operands, token = _start_copies(f"{tag}_start", n_streams * (N_DEV - 1), n_streams,
                                          lambda refs, a, b, o: [cp for _, cp in plan(refs, a, b, o)], operands)

    def waits(refs, a, b, o):
        out = []
        for kind, cp in plan(refs, a, b, o, arrivals=True):
            out += [cp.wait] if kind == "own" else [cp.wait_send, cp.wait_recv]
        return out

    return (lambda after: _wait_copies(f"{tag}_wait", waits, sems, operands, after)), token


def _gather_async(tag, arrays, streams):
    lands = [lax.empty((N_DEV,) + tuple(arrays[a].shape if li is None else arrays[a].shape[1:]), arrays[a].dtype)
             for a, li in streams]
    finish, token = _split_transfer(tag, _gather_plan(len(arrays), streams), len(streams), list(arrays) + lands)
    return (lambda after: finish(after)[len(arrays):]), token


def _exchange_async(tag, sends):
    recvs = [lax.empty(t.shape, t.dtype) for t in sends]
    finish, token = _split_transfer(tag, _exchange_plan(len(sends)), len(sends), list(sends) + recvs)
    return (lambda after: finish(after)[len(sends):]), token


def _dn_in_pieces():
    n = (DN_QKVZ + 2 * DN_HEADS) // N_DEV
    segs = ((0, DN_QKV, 0, 0), (DN_QKV, DN_QKV + 2 * DN_HEADS, 1, 0), (DN_QKV + 2 * DN_HEADS, DN_QKVZ + 2 * DN_HEADS, 0, DN_QKV))
    out = []
    for d in range(N_DEV):
        lo, hi = d * n, (d + 1) * n
        for s0, s1, tgt, t0 in segs:
            a, b = max(lo, s0), min(hi, s1)
            if a < b:
                out.append((d, a - lo, b - lo, tgt, t0 + a - s0))
    return out


def _unpack_cols(name, g):
    _, K, n = g.shape
    tr = 256

    def body(g_ref, o_ref):
        for d in range(N_DEV):
            o_ref[:, d * n:(d + 1) * n] = g_ref[d]

    return pl.pallas_call(
        body, name=name, grid=(K // tr,), in_specs=[pl.BlockSpec((N_DEV, tr, n), lambda i: (0, i, 0))],
        out_specs=pl.BlockSpec((tr, N_DEV * n), lambda i: (i, 0)),
        out_shape=jax.ShapeDtypeStruct((K, N_DEV * n), g.dtype), compiler_params=_cparams(("parallel",)),
    )(g)


def _pack_cols(name, w):
    K, n = w.shape[0], w.shape[1] // N_DEV
    tr = 256

    def body(w_ref, o_ref):
        for d in range(N_DEV):
            o_ref[d] = w_ref[:, d * n:(d + 1) * n]

    return pl.pallas_call(
        body, name=name, grid=(K // tr,), in_specs=[pl.BlockSpec((tr, N_DEV * n), lambda i: (i, 0))],
        out_specs=pl.BlockSpec((N_DEV, tr, n), lambda i: (0, i, 0)),
        out_shape=jax.ShapeDtypeStruct((N_DEV, K, n), w.dtype), compiler_params=_cparams(("parallel",)),
    )(w)


def _unpack_dn_in(g):
    _, K, n = g.shape
    tr = 256

    def body(g_ref, qkvz_ref, ab_ref):
        ab_ref[...] = jnp.zeros_like(ab_ref)
        for d, c0, c1, tgt, t0 in _dn_in_pieces():
            (qkvz_ref, ab_ref)[tgt][:, t0:t0 + c1 - c0] = g_ref[d, :, c0:c1]

    return pl.pallas_call(
        body, name="unpack_dn_in", grid=(K // tr,), in_specs=[pl.BlockSpec((N_DEV, tr, n), lambda i: (0, i, 0))],
        out_specs=[pl.BlockSpec((tr, DN_QKVZ), lambda i: (i, 0)), pl.BlockSpec((tr, LANE), lambda i: (i, 0))],
        out_shape=[jax.ShapeDtypeStruct((K, DN_QKVZ), g.dtype), jax.ShapeDtypeStruct((K, LANE), g.dtype)],
        compiler_params=_cparams(("parallel",)),
    )(g)


def _pack_dn_in(d_qkvz, d_ab):
    K = d_qkvz.shape[0]
    n = (DN_QKVZ + 2 * DN_HEADS) // N_DEV
    tr = 256

    def body(qkvz_ref, ab_ref, o_ref):
        for d, c0, c1, tgt, t0 in _dn_in_pieces():
            o_ref[d, :, c0:c1] = (qkvz_ref, ab_ref)[tgt][:, t0:t0 + c1 - c0]

    return pl.pallas_call(
        body, name="pack_dn_in", grid=(K // tr,),
        in_specs=[pl.BlockSpec((tr, DN_QKVZ), lambda i: (i, 0)), pl.BlockSpec((tr, LANE), lambda i: (i, 0))],
        out_specs=pl.BlockSpec((N_DEV, tr, n), lambda i: (0, i, 0)),
        out_shape=jax.ShapeDtypeStruct((N_DEV, K, n), d_qkvz.dtype), compiler_params=_cparams(("parallel",)),
    )(d_qkvz, d_ab)


ADAMW_ROWS = 256


def _adamw(name, parts, w, m, v):
    n_layers, R, C = w.shape
    tr = min(R, ADAMW_ROWS)
    assert R % tr == 0 and len(parts) == n_layers and all(p.shape == (N_DEV, R, C) for p in parts)
    c1 = 1.0 - B1 ** STEP
    c2 = 1.0 - B2 ** STEP

    def body(*refs):
        p_refs = refs[:n_layers]
        w_ref, m_ref, v_ref, g_ref, d_ref, nm_ref, nv_ref = refs[n_layers:]
        layer = pl.program_id(0)
        for li, p_ref in enumerate(p_refs):
            @pl.when(layer == li)
            def _(p_ref=p_ref):
                g = p_ref[0].astype(F32)
                for dev in range(1, N_DEV):
                    g = g + p_ref[dev].astype(F32)
                nm = B1 * m_ref[...] + (1.0 - B1) * g
                nv = B2 * v_ref[...] + (1.0 - B2) * jnp.square(g)
                g_ref[...] = g
                nm_ref[...] = nm
                nv_ref[...] = nv
                d_ref[...] = -LR * ((nm / c1) / (jnp.sqrt(nv / c2) + ADAM_EPS) + WD * w_ref[...])

    blk = pl.BlockSpec((None, tr, C), lambda l, i: (l, i, 0))
    return pl.pallas_call(
        body, name=name, grid=(n_layers, R // tr),
        in_specs=[pl.BlockSpec((N_DEV, tr, C), lambda l, i: (0, i, 0))] * n_layers + [blk, blk, blk],
        out_specs=[blk] * 4, out_shape=[jax.ShapeDtypeStruct((n_layers, R, C), F32)] * 4,
        compiler_params=_cparams(("parallel", "parallel")),
    )(*parts, w, m, v)


SMALL = ("mix_norm", "attn_q_gain", "attn_k_gain", "dn_a_log", "dn_dt_bias", "dn_o_gain", "mlp_norm", "ple_norm")
WEIGHTS = ("mix_norm", "attn_w_qkv", "attn_q_gain", "attn_k_gain", "attn_w_o", "dn_w_in", "dn_conv", "dn_a_log",
           "dn_dt_bias", "dn_o_gain", "dn_w_o", "mlp_norm", "w_up", "w_down", "ple_norm", "w_ple", "w_ple_gate")


def _to_rows(flat, multiple):
    n = flat.shape[-1]
    rows = -(-n // (LANE * multiple)) * multiple
    return jnp.pad(flat, [(0, rows * LANE - n)]).reshape(rows, LANE)


def _cols_to_devices(w):
    K, N = w.shape
    return jnp.transpose(w.reshape(K, N_DEV, N // N_DEV), (1, 0, 2))


def _cols_from_devices(g):
    _, K, n = g.shape
    return jnp.transpose(g, (1, 0, 2)).reshape(K, N_DEV * n)


SMALL_ROWS = 96


def _pack_small(vals, loss_rows):
    rows = [_to_rows(vals[n].reshape(-1), SUBLANE) for n in SMALL] + [loss_rows]
    buf = jnp.concatenate(rows, 0)
    assert buf.shape == (SMALL_ROWS, LANE)
    return buf


def _unpack_small(buf, like):
    out, r = {}, 0
    for n in SMALL:
        sz = math.prod(like[n].shape)
        out[n] = buf[r:r + -(-sz // LANE)].reshape(-1)[:sz].reshape(like[n].shape)
        r += -(-sz // (LANE * SUBLANE)) * SUBLANE
    return out


def kernel(x, p, positions, mix_norm, attn_w_qkv, attn_q_gain, attn_k_gain, attn_w_o, dn_w_in, dn_conv, dn_a_log, dn_dt_bias, dn_o_gain, dn_w_o, mlp_norm, w_up, w_down, ple_norm, w_ple, w_ple_gate, loss_target, m_mix_norm, m_attn_w_qkv, m_attn_q_gain, m_attn_k_gain, m_attn_w_o, m_dn_w_in, m_dn_conv, m_dn_a_log, m_dn_dt_bias, m_dn_o_gain, m_dn_w_o, m_mlp_norm, m_w_up, m_w_down, m_ple_norm, m_w_ple, m_w_ple_gate, v_mix_norm, v_attn_w_qkv, v_attn_q_gain, v_attn_k_gain, v_attn_w_o, v_dn_w_in, v_dn_conv, v_dn_a_log, v_dn_dt_bias, v_dn_o_gain, v_dn_w_o, v_mlp_norm, v_w_up, v_w_down, v_ple_norm, v_w_ple, v_w_ple_gate):
    w = dict(mix_norm=mix_norm, attn_w_qkv=attn_w_qkv, attn_q_gain=attn_q_gain, attn_k_gain=attn_k_gain, attn_w_o=attn_w_o,
             dn_w_in=dn_w_in, dn_conv=dn_conv, dn_a_log=dn_a_log, dn_dt_bias=dn_dt_bias, dn_o_gain=dn_o_gain, dn_w_o=dn_w_o,
             mlp_norm=mlp_norm, w_up=w_up, w_down=w_down, ple_norm=ple_norm, w_ple=w_ple, w_ple_gate=w_ple_gate)
    m = dict(mix_norm=m_mix_norm, attn_w_qkv=m_attn_w_qkv, attn_q_gain=m_attn_q_gain, attn_k_gain=m_attn_k_gain,
             attn_w_o=m_attn_w_o, dn_w_in=m_dn_w_in, dn_conv=m_dn_conv, dn_a_log=m_dn_a_log, dn_dt_bias=m_dn_dt_bias,
             dn_o_gain=m_dn_o_gain, dn_w_o=m_dn_w_o, mlp_norm=m_mlp_norm, w_up=m_w_up, w_down=m_w_down,
             ple_norm=m_ple_norm, w_ple=m_w_ple, w_ple_gate=m_w_ple_gate)
    v = dict(mix_norm=v_mix_norm, attn_w_qkv=v_attn_w_qkv, attn_q_gain=v_attn_q_gain, attn_k_gain=v_attn_k_gain,
             attn_w_o=v_attn_w_o, dn_w_in=v_dn_w_in, dn_conv=v_dn_conv, dn_a_log=v_dn_a_log, dn_dt_bias=v_dn_dt_bias,
             dn_o_gain=v_dn_o_gain, dn_w_o=v_dn_w_o, mlp_norm=v_mlp_norm, w_up=v_w_up, w_down=v_w_down,
             ple_norm=v_ple_norm, w_ple=v_w_ple, w_ple_gate=v_w_ple_gate)
    S = x.shape[1]

    bf = lambda a: a.astype(BF16)
    rows_to_devices = lambda t: t.reshape(N_DEV, t.shape[0] // N_DEV, t.shape[1])

    (g_qkv, g_ao), token = _all_gather("gather_attn", [bf(attn_w_qkv[0]), bf(attn_w_o[0])], [(0, None), (1, None)])
    rest_shards = [bf(dn_w_in[0]), bf(dn_w_o[0]), bf(w_up), bf(w_down), bf(w_ple), bf(w_ple_gate), _after(dn_conv[0], token)]
    rest_streams = [(0, None), (1, None), (2, 0), (2, 1), (3, 0), (3, 1), (4, 0), (4, 1), (5, 0), (5, 1), (6, None)]
    rest_arrived, token = _gather_async("gather_rest", rest_shards, rest_streams)
    W = dict(attn_w_qkv=_unpack_cols("unpack_attn_qkv", g_qkv), attn_w_o=_cols_from_devices(g_ao))

    def rest_of_weights(after):
        g_in, g_do, g_up0, g_up1, g_dn0, g_dn1, g_pl0, g_pl1, g_gt0, g_gt1, g_conv = rest_arrived(after)
        rest = dict(
            dn_conv=jnp.transpose(g_conv, (1, 0, 2)).reshape(CONV_W, DN_QKV), dn_w_o=g_do.reshape(DN_WIDTH, D_MODEL),
            w_up=[_cols_from_devices(g_up0), _cols_from_devices(g_up1)],
            w_down=[g_dn0.reshape(D_FF, D_MODEL), g_dn1.reshape(D_FF, D_MODEL)],
            w_ple=[_cols_from_devices(g_pl0), _cols_from_devices(g_pl1)],
            w_ple_gate=[g_gt0.reshape(D_MODEL, D_MODEL), g_gt1.reshape(D_MODEL, D_MODEL)])
        rest["dn_w_qkvz"], rest["dn_w_ab"] = _unpack_dn_in(g_in)
        return rest

    pending = {}

    def mlp_sends(g):
        return [_cols_to_devices(g["w_up"]), rows_to_devices(g["w_down"]), _cols_to_devices(g["w_ple"]),
                rows_to_devices(g["w_ple_gate"])]

    def start(tag, sends):
        pending[tag], token = _exchange_async(f"exchange_{tag}", sends)
        return token

    def send_layer1(g):
        conv_send = jnp.transpose(g["dn_conv"].reshape(CONV_W, N_DEV, DN_QKV // N_DEV), (1, 0, 2))
        return start("layer1", [_pack_dn_in(g["dn_w_qkvz"], g["dn_w_ab"]), conv_send, rows_to_devices(g["dn_w_o"])] + mlp_sends(g))

    def send_mlp0(g):
        return start("mlp0", mlp_sends(g))

    def send_attn(g):
        return start("attn", [_pack_cols("pack_attn_qkv", g["attn_w_qkv"]), _cols_to_devices(g["attn_w_o"])])

    P = dict(mix_norm=_after(mix_norm, token), attn_q_gain=attn_q_gain[0], attn_k_gain=attn_k_gain[0], dn_a_log=dn_a_log[0],
             dn_dt_bias=dn_dt_bias[0], dn_o_gain=dn_o_gain[0], mlp_norm=mlp_norm, ple_norm=ple_norm)

    sq, dx0, small_g = _local_step(x[0], p[:, 0], positions.reshape(S, 1), loss_target[0], W, P,
                                   rest_of_weights, send_layer1, send_mlp0, send_attn)

    r_in, r_conv, r_do, r_up1, r_dn1, r_pl1, r_gt1 = pending["layer1"](dx0)
    r_up0, r_dn0, r_pl0, r_gt0 = pending["mlp0"](dx0)
    r_qkv, r_ao = pending["attn"](dx0)
    big = {}
    for n, parts in (("attn_w_qkv", [r_qkv]), ("attn_w_o", [r_ao]), ("dn_w_in", [r_in]), ("dn_conv", [r_conv]),
                     ("dn_w_o", [r_do]), ("w_up", [r_up0, r_up1]), ("w_down", [r_dn0, r_dn1]),
                     ("w_ple", [r_pl0, r_pl1]), ("w_ple_gate", [r_gt0, r_gt1])):
        big[n] = _adamw(f"adamw_{n}", parts, w[n], m[n], v[n])

    loss_rows = jnp.pad((0.5 / D_MODEL) * jnp.sum(sq, axis=1, keepdims=True), ((0, SUBLANE - 1), (0, LANE - 1)))
    small_like = {n: w[n] for n in SMALL}
    parts_s = _all_gather("gather_small", [_pack_small(small_g, loss_rows)], [(0, None)])[0][0]
    zero_rows = jnp.zeros((SUBLANE, LANE), F32)
    small = _adamw("adamw_small", [parts_s], _pack_small(w, zero_rows)[None], _pack_small(m, zero_rows)[None],
                   _pack_small(v, zero_rows)[None])
    loss = small[0][0, SMALL_ROWS - SUBLANE, 0]
    small = [_unpack_small(b[0], small_like) for b in small]

    outs = [loss, dx0[None]]
    for k in range(4):
        for n in WEIGHTS:
            outs.append(small[k][n] if n in SMALL else big[n][k])
    return tuple(outs)
```

```python
import functools
import math

import jax
import jax.numpy as jnp
from jax import lax
from jax.experimental import pallas as pl
from jax.experimental.pallas import tpu as pltpu

F32 = jnp.float32
BF16 = jnp.bfloat16
HIGHEST = lax.Precision.HIGHEST

N_DEV = 8
D_MODEL = 1024
EPS = 1e-6
SWA_GROUPS = ((128, 1), (512, 4), (2048, 16))
A_HEADS = 8
A_HEAD_DIM = 64
A_WIDTH = A_HEADS * A_HEAD_DIM
A_QKV = 3 * 3 * A_WIDTH
ROPE_DIM = 16
ROPE_HALF = 8
ROPE_THETA = 500000.0
BAND = 128
DN_HEADS = 8
DN_DIM = 128
DN_WIDTH = DN_HEADS * DN_DIM
CONV_W = 4
CHUNK = 64
D_FF = 4 * D_MODEL
PLE_DIM = 256
LR, B1, B2, ADAM_EPS, WD, STEP = 0.001, 0.9, 0.999, 1e-08, 0.01, 10

VMEM_LIMIT = 56 * 1024 * 1024
MXU_TILE = 1024
DW_DEPTH = 4096
MM_SLAB = 256
LANE = 128
SUBLANE = 8


def _cparams(sem):
    return pltpu.CompilerParams(dimension_semantics=sem, vmem_limit_bytes=VMEM_LIMIT)


def _tile(n, pref):
    if n <= pref:
        return n
    t = (pref // LANE) * LANE
    while t >= LANE:
        if n % t == 0:
            return t
        t -= LANE
    raise ValueError(f"no tile for {n}")


def _dot(a, b, ca=1, cb=0, precision=None):
    return lax.dot_general(a, b, (((ca,), (cb,)), ((), ())), precision=precision,
                           preferred_element_type=F32)


def _bdot(a, b, ca=1, cb=0):
    return _dot(a.astype(BF16), b.astype(BF16), ca, cb)


def _mm(name, a, b, *, ta=False, tb=False, epilogue=None, extras=(), out_dtypes=(F32,), n_colsums=0,
        tm_pref=MXU_TILE, tn_pref=1536, tk_pref=2 * MXU_TILE):
    M, K = (a.shape[1], a.shape[0]) if ta else a.shape
    N = b.shape[0] if tb else b.shape[1]
    assert (b.shape[1] if tb else b.shape[0]) == K
    tm, tn, tk = _tile(M, tm_pref), _tile(N, tn_pref), _tile(K, tk_pref)
    nk = K // tk
    n_out = len(out_dtypes)
    n_ext = len(extras)
    assert n_colsums == 0 or tn == N
    sub = min(tm, MM_SLAB)

    def body(*refs):
        a_ref, b_ref = refs[0], refs[1]
        ext = refs[2:2 + n_ext]
        outs = refs[2 + n_ext:2 + n_ext + n_out]
        sums = refs[2 + n_ext + n_out:2 + n_ext + n_out + n_colsums]
        row_tile, k = pl.program_id(0), pl.program_id(2)
        slabs = [slice(s * sub, (s + 1) * sub) for s in range(tm // sub)]

        def product(rows):
            return _bdot(a_ref[:, rows] if ta else a_ref[rows, :], b_ref[...], 0 if ta else 1, 1 if tb else 0)

        def finish(results):
            col_rows = []
            for rows, r in zip(slabs, results):
                res = (r,) if epilogue is None else epilogue(r, *[e[...] if e.shape[0] == 1 else e[rows, :] for e in ext])
                for o, v in zip(outs, res):
                    o[rows, :] = v.astype(o.dtype)
                col_rows.append(res[n_out:])
            for n, o in enumerate(sums):
                v = functools.reduce(lambda x, y: x + y, [c[n] for c in col_rows])

                @pl.when(row_tile == 0)
                def _(o=o, v=v):
                    o[...] = v

                @pl.when(row_tile > 0)
                def _(o=o, v=v):
                    o[...] += v

        if nk == 1:
            finish([product(rows) for rows in slabs])
            return
        acc = refs[-1]

        @pl.when(k == 0)
        def _():
            acc[...] = jnp.zeros_like(acc)

        for rows in slabs:
            acc[rows, :] += product(rows)

        @pl.when(k == nk - 1)
        def _():
            finish([acc[rows, :] for rows in slabs])

    a_spec = pl.BlockSpec((tk, tm), lambda i, j, k: (k, i)) if ta else pl.BlockSpec((tm, tk), lambda i, j, k: (i, k))
    b_spec = pl.BlockSpec((tn, tk), lambda i, j, k: (j, k)) if tb else pl.BlockSpec((tk, tn), lambda i, j, k: (k, j))
    ext_specs = []
    for e in extras:
        if e.shape[0] == 1 and M != 1:
            ext_specs.append(pl.BlockSpec((1, tn), lambda i, j, k: (0, j)))
        else:
            ext_specs.append(pl.BlockSpec((tm, tn), lambda i, j, k: (i, j)))
    out = pl.pallas_call(
        body, name=name,
        grid=(M // tm, N // tn, nk),
        in_specs=[a_spec, b_spec] + ext_specs,
        out_specs=[pl.BlockSpec((tm, tn), lambda i, j, k: (i, j)) for _ in range(n_out)]
        + [pl.BlockSpec((1, tn), lambda i, j, k: (0, 0)) for _ in range(n_colsums)],
        out_shape=[jax.ShapeDtypeStruct((M, N), dt) for dt in out_dtypes]
        + [jax.ShapeDtypeStruct((1, N), F32) for _ in range(n_colsums)],
        scratch_shapes=[pltpu.VMEM((tm, tn), F32)] if nk > 1 else [],
        compiler_params=_cparams(("arbitrary" if n_colsums else "parallel", "parallel", "arbitrary")),
    )(a, b, *extras)
    return out[0] if len(out) == 1 else tuple(out)


def _perm_matrices(tr, d):
    import numpy as np
    old = np.arange(tr)
    p = np.zeros((tr, tr), np.float32)
    p[(old % d) * (tr // d) + old // d, old] = 1.0
    return jnp.asarray(p, BF16), jnp.asarray(p.T, BF16)


def _permute(p, x):
    if x.dtype == BF16:
        return _dot(p, x)
    hi = x.astype(BF16)
    rest = x - hi.astype(F32)
    mid = rest.astype(BF16)
    lo = (rest - mid.astype(F32)).astype(BF16)
    return _dot(p, hi) + _dot(p, mid) + _dot(p, lo)


def _rows(name, fn, ins, outs, *, tr, accs=()):
    ins = [(e[0], e[1]) + (e[2] if len(e) > 2 else (0, e[0].shape[-1])) for e in ins]
    outs = [tuple(o) + (0,) * (3 - len(o)) for o in outs]
    n_rows = next(e[0].shape[0] if e[1] == "row" else e[0].shape[0] * e[0].shape[1] for e in ins if e[1] in ("row", "res"))
    assert n_rows % tr == 0 and tr % SUBLANE == 0
    steps = n_rows // tr
    t8 = tr // SUBLANE
    n8 = n_rows // SUBLANE
    dils = sorted({e[0].shape[0] for e in ins if e[1] == "res" and e[0].shape[0] > 1} | {o[2] for o in outs if o[2] > 1})
    perms = [m for d in dils for m in _perm_matrices(tr, d)]
    ins = ins + [(m, "full", 0, tr) for m in perms]
    n_in, n_out, n_acc = len(ins), len(outs), len(accs)

    def body(*refs):
        i = pl.program_id(0)
        to_res = {d: refs[n_in - len(perms) + 2 * j][...] for j, d in enumerate(dils)}
        to_tok = {d: refs[n_in - len(perms) + 2 * j + 1][...] for j, d in enumerate(dils)}
        tiles = []
        for r, e in zip(refs[:n_in - len(perms)], ins):
            d = e[0].shape[0] if e[1] == "res" else 0
            if d == 0:
                tiles.append(r[...])
            elif d == 1:
                tiles.append(r[0])
            else:
                tiles.append(_permute(to_tok[d], jnp.concatenate([r[j] for j in range(d)], axis=0)))
        vals = fn(i, steps, *tiles)
        if not isinstance(vals, (tuple, list)):
            vals = (vals,)
        assert len(vals) == n_out + n_acc
        for o, v, (_, dt, d) in zip(refs[n_in:n_in + n_out], vals[:n_out], outs):
            if d == 0:
                o[...] = v.astype(o.dtype)
            elif d == 1:
                o[0] = v.astype(o.dtype)
            else:
                y = _permute(to_res[d], v.astype(dt))
                for j in range(d):
                    o[j] = y[j * (tr // d):(j + 1) * (tr // d)].astype(o.dtype)
        if n_acc:
            acc_refs = refs[n_in + n_out:]

            @pl.when(i == 0)
            def _():
                for r in acc_refs:
                    r[...] = jnp.zeros_like(r)

            for r, v in zip(acc_refs, vals[n_out:]):
                r[...] += v.astype(r.dtype)

    in_specs = []
    for a, kind, cb, c in ins:
        if kind == "row":
            in_specs.append(pl.BlockSpec((tr, c), lambda i, cb=cb: (i, cb)))
        elif kind == "full":
            in_specs.append(pl.BlockSpec(a.shape, lambda i, z=(0,) * a.ndim: z))
        elif kind == "prev8":
            in_specs.append(pl.BlockSpec((SUBLANE, c), lambda i, cb=cb: (jnp.maximum(i * t8 - 1, 0), cb)))
        elif kind == "next8":
            in_specs.append(pl.BlockSpec((SUBLANE, c), lambda i, cb=cb: (jnp.minimum((i + 1) * t8, n8 - 1), cb)))
        elif kind == "res":
            d = a.shape[0]
            in_specs.append(pl.BlockSpec((d, tr // d, a.shape[2]), lambda i: (0, i, 0)))
        else:
            raise ValueError(kind)
    out_specs = [pl.BlockSpec((tr, c), lambda i: (i, 0)) if d == 0 else pl.BlockSpec((d, tr // d, c), lambda i: (0, i, 0))
                 for c, _, d in outs]
    out_specs += [pl.BlockSpec(s, lambda i, z=(0,) * len(s): z) for s, _ in accs]
    out_shape = [jax.ShapeDtypeStruct((n_rows, c) if d == 0 else (d, n_rows // d, c), dt) for c, dt, d in outs]
    out_shape += [jax.ShapeDtypeStruct(s, dt) for s, dt in accs]
    res = pl.pallas_call(
        body, name=name, grid=(steps,), in_specs=in_specs, out_specs=out_specs, out_shape=out_shape,
        compiler_params=_cparams(("arbitrary",) if n_acc else ("parallel",)),
    )(*[e[0] for e in ins])
    return res[0] if len(res) == 1 else tuple(res)


def _colsum(x):
    return jnp.sum(x, axis=0, keepdims=True)


def _sum_all(x):
    return jnp.sum(jnp.sum(x, axis=1, keepdims=True), axis=0, keepdims=True)


def _rmsnorm_fwd(name, x, gain):
    def fn(i, n, xt, g):
        r = lax.rsqrt(jnp.mean(xt * xt, axis=-1, keepdims=True) + EPS)
        return (xt * r * g,)
    return _rows(name, fn, [(x, "row"), (gain, "full")], [(x.shape[1], BF16)], tr=512)


FUSED_ROWS = 1024


def _res_norm(acc, res, g):
    x = res + acc
    return x, x * lax.rsqrt(jnp.mean(x * x, axis=-1, keepdims=True) + EPS) * g


def _norm_bwd(dh, x, g, dres):
    r = lax.rsqrt(jnp.mean(x * x, axis=-1, keepdims=True) + EPS)
    xh = x * r
    dxn = dh * g
    dx = dres + r * (dxn - xh * jnp.mean(dxn * xh, axis=-1, keepdims=True))
    return dx, _colsum(dh * xh)


def _norm_bwd_2(dh, x, g, dres):
    dx, dg = _norm_bwd(dh, x, g, dres)
    return dx, dx, dg


def _head_consts():
    import numpy as np
    e = np.arange(A_WIDTH) % A_HEAD_DIM
    inv = (np.float32(ROPE_THETA) ** (-np.arange(0, ROPE_DIM, 2, dtype=np.float32) / np.float32(ROPE_DIM))).astype(np.float32)
    c = np.zeros((8, A_WIDTH), np.float32)
    c[0] = np.where(e < ROPE_DIM, inv[e % ROPE_HALF], 0.0)
    c[1] = np.where(e < ROPE_HALF, -1.0, np.where(e < ROPE_DIM, 1.0, 0.0))
    c[2] = (e < ROPE_HALF).astype(np.float32)
    c[3] = (e < ROPE_DIM).astype(np.float32)
    return jnp.asarray(c)


def _block_diag(scale):
    import numpy as np
    h = np.arange(A_WIDTH) // A_HEAD_DIM
    return jnp.asarray((h[:, None] == h[None, :]).astype(np.float32) * scale, dtype=BF16)


def _seg_sum(x, bd):
    return _dot(x.astype(BF16), bd)


def _rope_tables(positions, consts):
    def fn(i, n, pos, c):
        ang = pos.astype(F32) * c[0:1, :LANE]
        return jnp.cos(ang), jnp.sin(ang) * c[1:2, :LANE]
    return _rows("rope_tables", fn, [(positions, "row"), (consts, "full")], [(LANE, F32), (LANE, F32)], tr=512)


def _rope_wide(t):
    return jnp.concatenate([t] * (A_WIDTH // LANE), axis=1)


def _rope_apply(y, ct, st, low):
    rolled = jnp.where(low, pltpu.roll(y, A_WIDTH - ROPE_HALF, 1), pltpu.roll(y, ROPE_HALF, 1))
    return y * ct + rolled * st


def _rope_apply_bwd(dout, ct, st, low, in16):
    t = dout * st
    back = jnp.where(low, pltpu.roll(t, A_WIDTH - ROPE_HALF, 1), jnp.where(in16, pltpu.roll(t, ROPE_HALF, 1), 0.0))
    return dout * ct + back


def _attn_prep(qkv, gains, ct, st, consts, bd):
    def fn(i, n, t, g, c_t, s_t, c, b):
        low = c[2:3, :] > 0.5
        c_t, s_t = _rope_wide(c_t), _rope_wide(s_t)
        groups = []
        for grp in range(3):
            cols = []
            for which in range(3):
                off = (grp * 3 + which) * A_WIDTH
                x = t[:, off:off + A_WIDTH].astype(F32)
                if which == 2:
                    cols.append(x.astype(BF16))
                    continue
                r = lax.rsqrt(_seg_sum(x * x, b) + EPS)
                y = x * r * g[grp * 2 + which:grp * 2 + which + 1, :]
                cols.append(_rope_apply(y, c_t, s_t, low).astype(BF16))
            groups.append(jnp.concatenate(cols, axis=1))
        return tuple(groups)
    return _rows("attn_prep", fn, [(qkv, "row"), (gains, "full"), (ct, "row"), (st, "row"), (consts, "full"), (bd, "full")],
                 [(3 * A_WIDTH, BF16, d) for _, d in SWA_GROUPS], tr=256)


def _band_mask(n):
    row = lax.broadcasted_iota(jnp.int32, (BAND, 2 * BAND), 0)
    col = lax.broadcasted_iota(jnp.int32, (BAND, 2 * BAND), 1)
    dist = row + BAND - col
    return (dist >= 0) & (dist <= BAND) & ((col >= BAND) | (n > 0))


def _attn_fwd(qkvn, grp):
    d, L, _ = qkvn.shape
    nblk = L // BAND
    assert L % BAND == 0 and d == SWA_GROUPS[grp][1]

    def body(q_ref, kc_ref, kp_ref, vc_ref, vp_ref, o_ref, lse_ref):
        n = pl.program_id(1)
        valid = _band_mask(n)
        first = lax.broadcasted_iota(jnp.int32, (BAND, LANE), 1) < A_HEAD_DIM
        pairs = [slice(pr * LANE, (pr + 1) * LANE) for pr in range(A_WIDTH // LANE)]
        halves = (first, jnp.logical_not(first))
        qps = [q_ref[:, sl] for sl in pairs]
        kcats = [jnp.concatenate([kp_ref[:, sl], kc_ref[:, sl]], axis=0) for sl in pairs]
        vcats = [jnp.concatenate([vp_ref[:, sl], vc_ref[:, sl]], axis=0) for sl in pairs]
        heads = [(pr, m) for pr in range(len(pairs)) for m in halves]
        ss = [_dot(jnp.where(m, qps[pr], jnp.zeros_like(qps[pr])), kcats[pr], 1, 1) for pr, m in heads]
        ps, lses = [], []
        for s in ss:
            s = jnp.where(valid, s * (A_HEAD_DIM ** -0.5), -1e30)
            mx = jnp.max(s, axis=-1, keepdims=True)
            e = jnp.exp(s - mx)
            l = jnp.sum(e, axis=-1, keepdims=True)
            ps.append((e / l).astype(BF16))
            lses.append(mx + jnp.log(l))
        os_ = [_dot(p, vcats[pr]) for p, (pr, _) in zip(ps, heads)]
        o_ref[...] = jnp.concatenate([jnp.where(first, os_[2 * pr], os_[2 * pr + 1]) for pr in range(len(pairs))], axis=1)
        lse_ref[...] = jnp.concatenate([jnp.where(first, lses[2 * pr], lses[2 * pr + 1]) for pr in range(len(pairs))], axis=1)

    blk = (None, BAND, A_WIDTH)
    return pl.pallas_call(
        body, name=f"attn_fwd_g{grp}", grid=(d, nblk),
        in_specs=[pl.BlockSpec(blk, lambda r, n: (r, n, 0)),
                  pl.BlockSpec(blk, lambda r, n: (r, n, 1)),
                  pl.BlockSpec(blk, lambda r, n: (r, jnp.maximum(n - 1, 0), 1)),
                  pl.BlockSpec(blk, lambda r, n: (r, n, 2)),
                  pl.BlockSpec(blk, lambda r, n: (r, jnp.maximum(n - 1, 0), 2))],
        out_specs=[pl.BlockSpec(blk, lambda r, n: (r, n, 0)), pl.BlockSpec(blk, lambda r, n: (r, n, 0))],
        out_shape=[jax.ShapeDtypeStruct((d, L, A_WIDTH), F32)] * 2,
        compiler_params=_cparams(("parallel", "parallel")),
    )(qkvn, qkvn, qkvn, qkvn, qkvn)


def _merge_weights(l0, l1, l2):
    mx = jnp.maximum(jnp.maximum(l0, l1), l2)
    e0, e1, e2 = jnp.exp(l0 - mx), jnp.exp(l1 - mx), jnp.exp(l2 - mx)
    inv = 1.0 / (e0 + e1 + e2)
    return e0 * inv, e1 * inv, e2 * inv


def _attn_merge(os_, lses):
    def fn(i, n, o0, o1, o2, l0, l1, l2):
        w0, w1, w2 = _merge_weights(l0, l1, l2)
        return (w0 * o0 + w1 * o1 + w2 * o2,)
    ins = [(a, "res") for a in (*os_, *lses)]
    return _rows("attn_merge", fn, ins, [(A_WIDTH, BF16)], tr=256)


def _attn_merge_bwd(do, os_, lses, bd1):
    def fn(i, n, dot_, o0, o1, o2, l0, l1, l2, b):
        w0, w1, w2 = _merge_weights(l0, l1, l2)
        o = w0 * o0 + w1 * o1 + w2 * o2
        dsum = _seg_sum(dot_ * o, b)
        return (w0 * dot_, w1 * dot_, w2 * dot_, -w0 * dsum, -w1 * dsum, -w2 * dsum)
    ins = [(do, "row")] + [(a, "res") for a in (*os_, *lses)] + [(bd1, "full")]
    res = _rows("attn_merge_bwd", fn, ins, [(A_WIDTH, dt, d) for dt in (BF16, F32) for _, d in SWA_GROUPS], tr=256)
    return res[:3], res[3:]


def _lane_pick(x, lane_idx, lane):
    return jnp.sum(jnp.where(lane_idx == lane, x, 0.0), axis=-1, keepdims=True)


def _attn_bwd(qkvn, grp, do_g, lse, c_g):
    d, L, _ = qkvn.shape
    nblk = L // BAND

    def body(q_ref, kc_ref, kp_ref, vc_ref, vp_ref, do_ref, lse_ref, c_ref, dq_ref, dk_ref, dv_ref, ck, cv_):
        n = pl.program_id(1)

        @pl.when(n == 0)
        def _():
            ck[...] = jnp.zeros_like(ck)
            cv_[...] = jnp.zeros_like(cv_)

        @pl.when(n < nblk)
        def _():
            valid = _band_mask(n)
            lane = lax.broadcasted_iota(jnp.int32, (BAND, LANE), 1)
            first = lane < A_HEAD_DIM
            lane2 = lax.broadcasted_iota(jnp.int32, (2 * BAND, LANE), 1) < A_HEAD_DIM
            pairs = [slice(pr * LANE, (pr + 1) * LANE) for pr in range(A_WIDTH // LANE)]
            halves = (first, jnp.logical_not(first))
            qps = [q_ref[:, sl] for sl in pairs]
            dops = [do_ref[:, sl] for sl in pairs]
            kcats = [jnp.concatenate([kp_ref[:, sl], kc_ref[:, sl]], axis=0) for sl in pairs]
            vcats = [jnp.concatenate([vp_ref[:, sl], vc_ref[:, sl]], axis=0) for sl in pairs]
            heads = [(pr, hh) for pr in range(len(pairs)) for hh in range(2)]
            zero = jnp.zeros_like(qps[0])
            ss = [_dot(jnp.where(halves[hh], qps[pr], zero), kcats[pr], 1, 1) for pr, hh in heads]
            dps = [_dot(jnp.where(halves[hh], dops[pr], zero), vcats[pr], 1, 1) for pr, hh in heads]
            dss, pbs = [], []
            for (pr, hh), s, dp in zip(heads, ss, dps):
                lse_h = _lane_pick(lse_ref[:, pairs[pr]], lane, hh * A_HEAD_DIM)
                c_h = _lane_pick(c_ref[:, pairs[pr]], lane, hh * A_HEAD_DIM)
                p = jnp.where(valid, jnp.exp(s * (A_HEAD_DIM ** -0.5) - lse_h), 0.0)
                dss.append((p * (dp + c_h) * (A_HEAD_DIM ** -0.5)).astype(BF16))
                pbs.append(p.astype(BF16))
            dqs = [_dot(ds, kcats[pr]) for ds, (pr, _) in zip(dss, heads)]
            dks = [_dot(ds, qps[pr], 0, 0) for ds, (pr, _) in zip(dss, heads)]
            dvs = [_dot(pb, dops[pr], 0, 0) for pb, (pr, _) in zip(pbs, heads)]
            for pr, sl in enumerate(pairs):
                dq_ref[:, sl] = jnp.where(first, dqs[2 * pr], dqs[2 * pr + 1])
                dkc = jnp.where(lane2, dks[2 * pr], dks[2 * pr + 1])
                dvc = jnp.where(lane2, dvs[2 * pr], dvs[2 * pr + 1])
                dk_ref[:, sl] = ck[:, sl] + dkc[:BAND]
                dv_ref[:, sl] = cv_[:, sl] + dvc[:BAND]
                ck[:, sl] = dkc[BAND:]
                cv_[:, sl] = dvc[BAND:]

        @pl.when(n == nblk)
        def _():
            dk_ref[...] = ck[...]
            dv_ref[...] = cv_[...]

    blk = (None, BAND, A_WIDTH)
    last = nblk - 1
    qn = lambda n: jnp.minimum(n, last)
    pn = lambda n: jnp.clip(n - 1, 0, last)
    return tuple(pl.pallas_call(
        body, name=f"attn_bwd_g{grp}", grid=(d, nblk + 1),
        in_specs=[pl.BlockSpec(blk, lambda r, n: (r, qn(n), 0)),
                  pl.BlockSpec(blk, lambda r, n: (r, qn(n), 1)),
                  pl.BlockSpec(blk, lambda r, n: (r, pn(n), 1)),
                  pl.BlockSpec(blk, lambda r, n: (r, qn(n), 2)),
                  pl.BlockSpec(blk, lambda r, n: (r, pn(n), 2)),
                  pl.BlockSpec(blk, lambda r, n: (r, qn(n), 0)),
                  pl.BlockSpec(blk, lambda r, n: (r, qn(n), 0)),
                  pl.BlockSpec(blk, lambda r, n: (r, qn(n), 0))],
        out_specs=[pl.BlockSpec(blk, lambda r, n: (r, qn(n), 0)),
                   pl.BlockSpec(blk, lambda r, n: (r, pn(n), 0)),
                   pl.BlockSpec(blk, lambda r, n: (r, pn(n), 0))],
        out_shape=[jax.ShapeDtypeStruct((d, L, A_WIDTH), F32)] * 3,
        scratch_shapes=[pltpu.VMEM((BAND, A_WIDTH), F32), pltpu.VMEM((BAND, A_WIDTH), F32)],
        compiler_params=_cparams(("parallel", "arbitrary")),
    )(qkvn, qkvn, qkvn, qkvn, qkvn, do_g, lse, c_g))


def _attn_prep_bwd(qkv, grads, gains, ct, st, consts, bd):
    def fn(i, n, t, g, c_t, s_t, c, b, *gr):
        low = c[2:3, :] > 0.5
        in16 = c[3:4, :] > 0.5
        c_t, s_t = _rope_wide(c_t), _rope_wide(s_t)
        cols, dgs = [], []
        for grp in range(3):
            for which in range(3):
                dout = gr[grp * 3 + which]
                if which == 2:
                    cols.append(dout.astype(BF16))
                    continue
                off = (grp * 3 + which) * A_WIDTH
                x = t[:, off:off + A_WIDTH].astype(F32)
                gain = g[grp * 2 + which:grp * 2 + which + 1, :]
                r = lax.rsqrt(_seg_sum(x * x, b) + EPS)
                xh = x * r
                dy = _rope_apply_bwd(dout, c_t, s_t, low, in16)
                dyn = dy * gain
                dx = r * (dyn - xh * _seg_sum(dyn * xh, b))
                cols.append(dx.astype(BF16))
                dgs.append(_colsum(dy * xh))
        return (jnp.concatenate(cols, axis=1), *dgs)
    ins = [(qkv, "row"), (gains, "full"), (ct, "row"), (st, "row"), (consts, "full"), (bd, "full")] + [(a, "res") for a in grads]
    res = _rows("attn_prep_bwd", fn, ins, [(A_QKV, BF16)], tr=128, accs=[((1, A_WIDTH), F32)] * 6)
    return res[0], res[1:]


DN_QKV = 3 * DN_WIDTH
DN_QKVZ = DN_QKV + DN_WIDTH


def _sigmoid(x):
    return jax.nn.sigmoid(x)


def _softplus(x):
    return jnp.maximum(x, 0.0) + jnp.log(1.0 + jnp.exp(-jnp.abs(x)))


def _conv_taps(xs, w, tr):
    acc = None
    for j in range(CONV_W):
        sh = CONV_W - 1 - j
        term = (pltpu.roll(xs, sh, 0) if sh else xs)[SUBLANE:] * w[j:j + 1, :]
        acc = term if acc is None else acc + term
    return acc


def _dn_prep(qkvz, ab, convw, alog_row, dt_row):
    tr = 256

    def fn(i, n, x, xp, abt, w, al, dt):
        xp = jnp.where(i > 0, xp, 0.0)
        u = _conv_taps(jnp.concatenate([xp, x], axis=0), w, tr)
        y = u * _sigmoid(u)
        qs, ks = [], []
        for h in range(DN_HEADS):
            for dst, base, sc in ((qs, 0, DN_DIM ** -0.5), (ks, DN_WIDTH, 1.0)):
                seg = y[:, base + h * DN_DIM:base + (h + 1) * DN_DIM]
                dst.append(seg * (lax.rsqrt(jnp.sum(seg * seg, axis=-1, keepdims=True) + EPS) * sc))
        lane = lax.broadcasted_iota(jnp.int32, abt.shape, 1)
        g = -jnp.exp(al) * _softplus(abt + dt)
        gb = jnp.where(lane < DN_HEADS, g, jnp.where(lane < 2 * DN_HEADS, _sigmoid(abt), 0.0))
        return u, jnp.concatenate(qs, axis=1), jnp.concatenate(ks, axis=1), y[:, 2 * DN_WIDTH:], gb

    ins = [(qkvz, "row", (0, DN_QKV)), (qkvz, "prev8", (0, DN_QKV)), (ab, "row"), (convw, "full"),
           (alog_row, "full"), (dt_row, "full")]
    return _rows("dn_prep", fn, ins, [(DN_QKV, BF16), (DN_WIDTH, F32), (DN_WIDTH, F32), (DN_WIDTH, F32), (LANE, F32)], tr=tr)


def _tri_masks():
    row = lax.broadcasted_iota(jnp.int32, (CHUNK, CHUNK), 0)
    col = lax.broadcasted_iota(jnp.int32, (CHUNK, CHUNK), 1)
    return row >= col, row > col, row == col


def _heads(fn, *lists):
    return [fn(*xs) for xs in zip(*lists)]


def _split(x):
    hi = x.astype(BF16)
    return hi, (x - hi.astype(F32)).astype(BF16)


def _dot3(a, b, ca=1, cb=0):
    (ah, al), (bh, bl) = a, b
    return _dot(ah, bh, ca, cb) + (_dot(ah, bl, ca, cb) + _dot(al, bh, ca, cb))


SPLIT_STEPS = 3


def _unit_lower_inverse(a_list, eye):
    ts = [eye - a for a in a_list]
    parts = [_split(a) for a in a_list]
    for step in range(5):
        if step < SPLIT_STEPS:
            parts = [_split(_dot3(p, p)) for p in parts]
            ts = [t + _dot3(_split(t), p) for t, p in zip(ts, parts)]
        else:
            parts = [(_dot(p[0], p[0]).astype(BF16), None) for p in parts]
            ts = [t + _dot(t.astype(BF16), p[0]) for t, p in zip(ts, parts)]
    return ts


def _dn_terms(qs, ks, vs, gb, solved=None):
    lower, strict, diag = _tri_masks()
    lane = lax.broadcasted_iota(jnp.int32, (CHUNK, LANE), 1)
    is_last = lax.broadcasted_iota(jnp.int32, (CHUNK, 1), 0) == CHUNK - 1
    hs = range(DN_HEADS)
    gc = _dot(lower.astype(F32), gb, precision=HIGHEST)
    gct = jnp.transpose(gc)
    bcol = [_lane_pick(gb, lane, DN_HEADS + h) for h in hs]
    gcol = [_lane_pick(gc, lane, h) for h in hs]
    glast = [jnp.sum(jnp.where(is_last, g, 0.0), axis=0, keepdims=True) for g in gcol]
    decay = [jnp.exp(jnp.where(lower, gcol[h] - gct[h:h + 1, :], -1e30)) for h in hs]
    kb = _heads(lambda k, b: k * b, ks, bcol)
    both = _heads(lambda q, x, k: _bdot(jnp.concatenate([q, x], axis=0), k, 1, 1), qs, kb, ks)
    qk = [x[:CHUNK] for x in both]
    kk = [x[CHUNK:] for x in both]
    a = _heads(lambda x, d: jnp.where(strict, x * d, 0.0), kk, decay)
    eg = [jnp.exp(g) for g in gcol]
    egl = _heads(lambda gl, g: jnp.exp(gl - g), glast, gcol)
    rhs_w = _heads(lambda x, e: x * e, kb, eg)
    if solved is None:
        t_full = _unit_lower_inverse(a, diag.astype(F32))
        t = [_split(x) for x in t_full]
        uw = _heads(lambda tt, v, b, r: _dot3(tt, _split(jnp.concatenate([v * b, r], axis=1))), t, vs, bcol, rhs_w)
        u = [x[:, :DN_DIM] for x in uw]
        w = [x[:, DN_DIM:] for x in uw]
    else:
        t_full, u, w = solved
        t = [_split(x) for x in t_full]
    return dict(bcol=bcol, decay=decay, kb=kb, a=a, t=t, t_full=t_full, eg=eg, egl=egl, rhs_w=rhs_w, u=u, w=w,
                attn=_heads(lambda x, d: x * d, qk, decay), q_dec=_heads(lambda q, e: q * e, qs, eg),
                k_dec=_heads(lambda k, e: k * e, ks, egl), c_dec=[jnp.exp(g) for g in glast],
                lower=lower, strict=strict, lane=lane, is_last=is_last)


def _head_slices(ref):
    return [ref[:, h * DN_DIM:(h + 1) * DN_DIM] for h in range(DN_HEADS)]


def _dn_chunk_fwd(q, k, v, gb):
    S = q.shape[0]
    N = S // CHUNK

    def body(q_ref, k_ref, v_ref, gb_ref, o_ref, st_ref, t_ref, u_ref, w_ref, state):
        @pl.when(pl.program_id(0) == 0)
        def _():
            state[...] = jnp.zeros_like(state)

        f = _dn_terms(_head_slices(q_ref), _head_slices(k_ref), _head_slices(v_ref), gb_ref[...])
        s = [state[h] for h in range(DN_HEADS)]
        for h in range(DN_HEADS):
            st_ref[0, h] = s[h]
            t_ref[0, h] = f["t_full"][h]
            u_ref[:, h * DN_DIM:(h + 1) * DN_DIM] = f["u"][h]
            w_ref[:, h * DN_DIM:(h + 1) * DN_DIM] = f["w"][h]
        sb = [x.astype(BF16) for x in s]
        v_new = _heads(lambda u, w, x: u - _bdot(w, x), f["u"], f["w"], sb)
        o = _heads(lambda qd, x, at, vn: _bdot(qd, x) + _bdot(at, vn), f["q_dec"], sb, f["attn"], v_new)
        new_s = _heads(lambda x, c, kd, vn: x * c + _bdot(kd, vn, 0, 0), s, f["c_dec"], f["k_dec"], v_new)
        for h in range(DN_HEADS):
            o_ref[:, h * DN_DIM:(h + 1) * DN_DIM] = o[h]
            state[h] = new_s[h]

    blk = pl.BlockSpec((CHUNK, DN_WIDTH), lambda n: (n, 0))
    st_blk = pl.BlockSpec((1, DN_HEADS, DN_DIM, DN_DIM), lambda n: (n, 0, 0, 0))
    t_blk = pl.BlockSpec((1, DN_HEADS, CHUNK, CHUNK), lambda n: (n, 0, 0, 0))
    wide = jax.ShapeDtypeStruct((S, DN_WIDTH), F32)
    o, states, t, u, w = pl.pallas_call(
        body, name="dn_chunk_fwd", grid=(N,),
        in_specs=[blk, blk, blk, pl.BlockSpec((CHUNK, LANE), lambda n: (n, 0))],
        out_specs=[blk, st_blk, t_blk, blk, blk],
        out_shape=[wide, jax.ShapeDtypeStruct((N, DN_HEADS, DN_DIM, DN_DIM), F32),
                   jax.ShapeDtypeStruct((N, DN_HEADS, CHUNK, CHUNK), F32), wide, wide],
        scratch_shapes=[pltpu.VMEM((DN_HEADS, DN_DIM, DN_DIM), F32)],
        compiler_params=_cparams(("arbitrary",)),
    )(q, k, v, gb)
    return o, (states, t, u, w)


def _dn_chunk_bwd(q, k, v, gb, saved, do):
    S = q.shape[0]
    N = S // CHUNK
    states, t_saved, u_saved, w_saved = saved

    def body(q_ref, k_ref, v_ref, gb_ref, st_ref, t_ref, u_ref, w_ref, do_ref, dq_ref, dk_ref, dv_ref, dgb_ref, dstate):
        @pl.when(pl.program_id(0) == 0)
        def _():
            dstate[...] = jnp.zeros_like(dstate)

        hs = range(DN_HEADS)
        qs, ks, vs, dos = (_head_slices(r) for r in (q_ref, k_ref, v_ref, do_ref))
        f = _dn_terms(qs, ks, vs, gb_ref[...], ([t_ref[0, h] for h in hs], _head_slices(u_ref), _head_slices(w_ref)))
        lane, is_last = f["lane"], f["is_last"]
        rowsum = lambda x: jnp.sum(x, axis=-1, keepdims=True)
        s = [st_ref[0, h] for h in hs]
        dsn = [dstate[h] for h in hs]
        sb = [x.astype(BF16) for x in s]
        dsb = [x.astype(BF16) for x in dsn]
        dob = [x.astype(BF16) for x in dos]
        v_new = _heads(lambda u, w, x: u - _bdot(w, x), f["u"], f["w"], sb)
        dv_new = _heads(lambda at, d, kd, x: _bdot(at, d, 0, 0) + _bdot(kd, x), f["attn"], dob, f["k_dec"], dsb)
        dattn = _heads(lambda d, vn: _bdot(d, vn, 1, 1), dob, v_new)
        dq_dec = _heads(lambda d, x: _bdot(d, x, 1, 1), dob, sb)
        dk_dec = _heads(lambda vn, x: _bdot(vn, x, 1, 1), v_new, dsb)
        dw = _heads(lambda dv_, x: -_bdot(dv_, x, 1, 1), dv_new, sb)
        new_ds = _heads(lambda x, c, qd, d, w, dv_: x * c + _bdot(qd, d, 0, 0) - _bdot(w, dv_, 0, 0),
                        dsn, f["c_dec"], f["q_dec"], dob, f["w"], dv_new)
        for h in hs:
            dstate[h] = new_ds[h]
        drhs = _heads(lambda tt, x, y: _dot3(tt, _split(jnp.concatenate([x, y], axis=1)), 0, 0), f["t"], dv_new, dw)
        drhs_u = [x[:, :DN_DIM] for x in drhs]
        drhs_w = [x[:, DN_DIM:] for x in drhs]
        da = _heads(lambda du_, u, dw_, w: jnp.where(f["strict"], -(_bdot(du_, u, 1, 1) + _bdot(dw_, w, 1, 1)), 0.0),
                    drhs_u, f["u"], drhs_w, f["w"])
        dkk = _heads(lambda x, d: x * d, da, f["decay"])
        dqk = _heads(lambda x, d: x * d, dattn, f["decay"])
        by_k = _heads(lambda x, y, k_: _bdot(jnp.concatenate([x, y], axis=0), k_), dqk, dkk, ks)
        dq = _heads(lambda x, dqd, e: x[:CHUNK] + dqd * e, by_k, dq_dec, f["eg"])
        dkb = _heads(lambda x, dw_, e: x[CHUNK:] + dw_ * e, by_k, drhs_w, f["eg"])
        dk = _heads(lambda x, kb_, y, q_, dkd, el, dkb_, b: _bdot(x, kb_, 0, 0) + _bdot(y, q_, 0, 0) + dkd * el + dkb_ * b,
                    dkk, f["kb"], dqk, qs, dk_dec, f["egl"], dkb, f["bcol"])
        m = _heads(lambda x, a_, y, at: x * a_ + y * at, da, f["a"], dattn, f["attn"])
        ones = jnp.ones((CHUNK, LANE), BF16)
        col_m = [(_dot(mh, ones, 0, 0) + _dot(ml, ones, 0, 0))[:, 0:1] for mh, ml in map(_split, m)]
        dgc_all = jnp.zeros((CHUNK, LANE), F32)
        dbeta_all = jnp.zeros((CHUNK, LANE), F32)
        for h in hs:
            dq_ref[:, h * DN_DIM:(h + 1) * DN_DIM] = dq[h]
            dk_ref[:, h * DN_DIM:(h + 1) * DN_DIM] = dk[h]
            dv_ref[:, h * DN_DIM:(h + 1) * DN_DIM] = drhs_u[h] * f["bcol"][h]
            kdec_term = rowsum(dk_dec[h] * f["k_dec"][h])
            dc_dec = _sum_all(dsn[h] * s[h])
            dgc = (rowsum(m[h]) - col_m[h] + rowsum(dq_dec[h] * f["q_dec"][h]) - kdec_term
                   + rowsum(drhs_w[h] * f["rhs_w"][h]))
            last_extra = jnp.sum(kdec_term, axis=0, keepdims=True) + dc_dec * f["c_dec"][h]
            dgc = dgc + jnp.where(is_last, last_extra, 0.0)
            dbeta = rowsum(drhs_u[h] * vs[h]) + rowsum(dkb[h] * ks[h])
            dgc_all = jnp.where(lane == h, dgc, dgc_all)
            dbeta_all = jnp.where(lane == DN_HEADS + h, dbeta, dbeta_all)
        dg_all = _dot(f["lower"].astype(F32), dgc_all, 0, 0, precision=HIGHEST)
        dgb_ref[...] = jnp.where(lane < DN_HEADS, dg_all, dbeta_all)

    rev = lambda n: (N - 1 - n, 0)
    blk = pl.BlockSpec((CHUNK, DN_WIDTH), rev)
    gblk = pl.BlockSpec((CHUNK, LANE), rev)
    st_blk = pl.BlockSpec((1, DN_HEADS, DN_DIM, DN_DIM), lambda n: (N - 1 - n, 0, 0, 0))
    t_blk = pl.BlockSpec((1, DN_HEADS, CHUNK, CHUNK), lambda n: (N - 1 - n, 0, 0, 0))
    return pl.pallas_call(
        body, name="dn_chunk_bwd", grid=(N,),
        in_specs=[blk, blk, blk, gblk, st_blk, t_blk, blk, blk, blk],
        out_specs=[blk, blk, blk, gblk],
        out_shape=[jax.ShapeDtypeStruct((S, DN_WIDTH), F32)] * 3 + [jax.ShapeDtypeStruct((S, LANE), F32)],
        scratch_shapes=[pltpu.VMEM((DN_HEADS, DN_DIM, DN_DIM), F32)],
        compiler_params=_cparams(("arbitrary",)),
    )(q, k, v, gb, states, t_saved, u_saved, w_saved, do)


def _dn_post(o, qkvz, gain_row):
    def fn(i, n, ot, z, g):
        cols = []
        for h in range(DN_HEADS):
            seg = ot[:, h * DN_DIM:(h + 1) * DN_DIM]
            cols.append(seg * lax.rsqrt(jnp.mean(seg * seg, axis=-1, keepdims=True) + EPS) * g)
        return (jnp.concatenate(cols, axis=1) * (z * _sigmoid(z)),)
    return _rows("dn_post", fn, [(o, "row"), (qkvz, "row", (3, DN_WIDTH)), (gain_row, "full")], [(DN_WIDTH, BF16)], tr=512)


def _dn_post_bwd(don, o, qkvz, gain_row):
    def fn(i, n, dy, ot, z, g):
        sg = _sigmoid(z)
        sz = z * sg
        dos, ohs = [], []
        dg = jnp.zeros((1, DN_DIM), F32)
        for h in range(DN_HEADS):
            sl = slice(h * DN_DIM, (h + 1) * DN_DIM)
            seg = ot[:, sl]
            r = lax.rsqrt(jnp.mean(seg * seg, axis=-1, keepdims=True) + EPS)
            oh = seg * r
            dno = dy[:, sl] * sz[:, sl]
            dg = dg + _colsum(dno * oh)
            dn = dno * g
            dos.append(r * (dn - oh * jnp.mean(dn * oh, axis=-1, keepdims=True)))
            ohs.append(oh * g)
        dz = dy * jnp.concatenate(ohs, axis=1) * (sg * (1.0 + z * (1.0 - sg)))
        return jnp.concatenate(dos, axis=1), dz, dg
    ins = [(don, "row"), (o, "row"), (qkvz, "row", (3, DN_WIDTH)), (gain_row, "full")]
    return _rows("dn_post_bwd", fn, ins, [(DN_WIDTH, F32), (DN_WIDTH, F32)], tr=256, accs=[((1, DN_DIM), F32)])


def _dn_prep_bwd(dq, dk, dv, dgb, u, ab, alog_row, dt_row):
    def fn(i, n, dqt, dkt, dvt, dgbt, ut, abt, al, dt):
        ut = ut.astype(F32)
        sg = _sigmoid(ut)
        y = ut * sg
        dys = []
        for grad, base, sc in ((dqt, 0, DN_DIM ** -0.5), (dkt, DN_WIDTH, 1.0)):
            for h in range(DN_HEADS):
                seg = y[:, base + h * DN_DIM:base + (h + 1) * DN_DIM]
                gr = grad[:, h * DN_DIM:(h + 1) * DN_DIM]
                r = lax.rsqrt(jnp.sum(seg * seg, axis=-1, keepdims=True) + EPS)
                xh = seg * r
                dys.append((r * sc) * (gr - xh * jnp.sum(gr * xh, axis=-1, keepdims=True)))
        dy = jnp.concatenate(dys + [dvt], axis=1)
        du = dy * (sg * (1.0 + ut * (1.0 - sg)))
        lane = lax.broadcasted_iota(jnp.int32, abt.shape, 1)
        is_g = lane < DN_HEADS
        ea = jnp.exp(al)
        x = abt + dt
        slope = -ea * _sigmoid(x)
        gval = -ea * _softplus(x)
        dg = jnp.where(is_g, dgbt, 0.0)
        beta = _sigmoid(abt)
        dab = jnp.where(is_g, dg * slope, jnp.where(lane < 2 * DN_HEADS, dgbt * beta * (1.0 - beta), 0.0))
        return du, dab, _colsum(dg * gval), _colsum(dg * slope)
    ins = [(dq, "row"), (dk, "row"), (dv, "row"), (dgb, "row"), (u, "row"), (ab, "row"), (alog_row, "full"), (dt_row, "full")]
    return _rows("dn_prep_bwd", fn, ins, [(DN_QKV, F32), (LANE, BF16)], tr=256, accs=[((1, LANE), F32)] * 2)


def _dn_conv_bwd(du, dz, qkvz, convw):
    tr = 256

    def fn(i, n, dut, dun, dzt, x, xp, w):
        dun = jnp.where(i < n - 1, dun, 0.0)
        dus = jnp.concatenate([dut, dun], axis=0)
        xs = jnp.concatenate([jnp.where(i > 0, xp, 0.0), x], axis=0)
        dx = None
        dws = []
        for j in range(CONV_W):
            sh = CONV_W - 1 - j
            term = (pltpu.roll(dus, tr + SUBLANE - sh, 0) if sh else dus)[:tr] * w[j:j + 1, :]
            dx = term if dx is None else dx + term
            dws.append(_colsum(dut * (pltpu.roll(xs, sh, 0) if sh else xs)[SUBLANE:]))
        return (jnp.concatenate([dx.astype(BF16), dzt.astype(BF16)], axis=1), *dws)

    ins = [(du, "row"), (du, "next8"), (dz, "row"), (qkvz, "row", (0, DN_QKV)), (qkvz, "prev8", (0, DN_QKV)), (convw, "full")]
    res = _rows("dn_conv_bwd", fn, ins, [(DN_QKVZ, BF16)], tr=tr, accs=[((1, DN_QKV), F32)] * CONV_W)
    return res[0], res[1:]


def _add(acc, r):
    return (r + acc,)


def _mlp_ple_fwd(i, x1, hm, p_i, ple_gain, next_gain, w_up, w_down, w_ple, w_gate, target=None):
    u, a = _mm(f"mlp_up{i}", hm, w_up, epilogue=lambda acc: (acc, jnp.square(jnp.maximum(acc, 0.0))),
               out_dtypes=(BF16, BF16))
    x2, hp = _mm(f"mlp_down{i}", a, w_down, epilogue=_res_norm, extras=(x1, ple_gain), out_dtypes=(F32, BF16),
                 tm_pref=FUSED_ROWS)
    pp = _mm(f"ple_proj{i}", p_i, w_ple)

    def gate_epilogue(acc, x2t, ppt, g):
        gate = _sigmoid(acc)
        x3 = x2t + ppt * gate
        return x3, gate, x3 * lax.rsqrt(jnp.mean(x3 * x3, axis=-1, keepdims=True) + EPS) * g

    def loss_epilogue(acc, x2t, ppt, tt):
        gate = _sigmoid(acc)
        err = x2t + ppt * gate - tt
        dy = err * (1.0 / D_MODEL)
        return dy, dy * gate, dy * ppt * gate * (1.0 - gate), _colsum(err * err)

    saved = dict(x1=x1, hm=hm, u=u, a=a, x2=x2, hp=hp, pp=pp, p=p_i)
    if target is None:
        x3, saved["gate"], h_next = _mm(f"ple_gate{i}", hp, w_gate, epilogue=gate_epilogue, extras=(x2, pp, next_gain),
                                        out_dtypes=(F32, F32, BF16), tm_pref=FUSED_ROWS)
        return x3, h_next, saved
    dy, saved["dpp"], saved["dzg"], sq = _mm(f"ple_gate{i}", hp, w_gate, epilogue=loss_epilogue, extras=(x2, pp, target),
                                             out_dtypes=(F32, BF16, BF16), n_colsums=1, tm_pref=FUSED_ROWS)
    return dy, sq, saved


def _mlp_ple_bwd(i, dx3, sv, mlp_gain, ple_gain, w_up, w_down, w_gate):
    if "dpp" in sv:
        dpp, dzg = sv["dpp"], sv["dzg"]
    else:
        def fn(_i, _n, d, g, pp):
            return d * g, d * pp * g * (1.0 - g)
        dpp, dzg = _rows(f"ple_gate_bwd{i}", fn, [(dx3, "row"), (sv["gate"], "row"), (sv["pp"], "row")],
                         [(D_MODEL, BF16), (D_MODEL, BF16)], tr=512)
    d_w_ple = _mm(f"ple_proj_dw{i}", sv["p"], dpp, ta=True, out_dtypes=(BF16,))
    d_w_gate = _mm(f"ple_gate_dw{i}", sv["hp"], dzg, ta=True, out_dtypes=(BF16,), tk_pref=DW_DEPTH)
    dx2, dx2b, d_ple_gain = _mm(f"ple_gate_dx{i}", dzg, w_gate, tb=True, epilogue=_norm_bwd_2,
                                extras=(sv["x2"], ple_gain, dx3), out_dtypes=(F32, BF16), n_colsums=1, tm_pref=FUSED_ROWS)
    d_w_down = _mm(f"mlp_down_dw{i}", sv["a"], dx2b, ta=True, out_dtypes=(BF16,), tk_pref=DW_DEPTH)
    du = _mm(f"mlp_down_dx{i}", dx2b, w_down, tb=True,
             epilogue=lambda acc, ut: (acc * (2.0 * jnp.maximum(ut.astype(F32), 0.0)),), extras=(sv["u"],), out_dtypes=(BF16,))
    d_w_up = _mm(f"mlp_up_dw{i}", sv["hm"], du, ta=True, out_dtypes=(BF16,), tn_pref=MXU_TILE, tk_pref=DW_DEPTH)
    dx1, dx1b, d_mlp_gain = _mm(f"mlp_up_dx{i}", du, w_up, tb=True, epilogue=_norm_bwd_2,
                                extras=(sv["x1"], mlp_gain, dx2), out_dtypes=(F32, BF16), n_colsums=1, tm_pref=FUSED_ROWS)
    return dx1, dx1b, dict(w_ple=d_w_ple, w_ple_gate=d_w_gate, w_down=d_w_down, w_up=d_w_up,
                           ple_norm=d_ple_gain, mlp_norm=d_mlp_gain)


def _after(small, token):
    return small + token[0:1, 0:1]


def _local_step(x, p, positions, target, W, P, rest_of_weights, send_layer1, send_mlp0, send_attn):
    consts = _head_consts()
    bd = _block_diag(1.0 / A_HEAD_DIM)
    bd1 = _block_diag(1.0)
    ct, st = _rope_tables(positions, consts)
    gains = jnp.stack([jnp.tile(v, A_HEADS) for g in range(3) for v in (P["attn_q_gain"][g], P["attn_k_gain"][g])])
    pad = LANE - DN_HEADS
    alog_row = jnp.pad(P["dn_a_log"].reshape(1, DN_HEADS), ((0, 0), (0, pad)))
    dt_row = jnp.pad(P["dn_dt_bias"].reshape(1, DN_HEADS), ((0, 0), (0, pad)))
    ogain_row = P["dn_o_gain"].reshape(1, DN_DIM)
    row = lambda name, i: P[name][i:i + 1]

    h0 = _rmsnorm_fwd("mix_norm0", x, row("mix_norm", 0))
    qkv = _mm("attn_qkv", h0, W["attn_w_qkv"], out_dtypes=(BF16,))
    qkvn = _attn_prep(qkv, gains, ct, st, consts, bd)
    os_, lses = zip(*[_attn_fwd(qkvn[g], g) for g in range(3)])
    o_attn = _attn_merge(os_, lses)
    x1, hm0 = _mm("attn_out", o_attn, W["attn_w_o"], epilogue=_res_norm, extras=(x, row("mlp_norm", 0)),
                  out_dtypes=(F32, BF16), tm_pref=FUSED_ROWS)
    W = {**W, **rest_of_weights(x1)}
    x3, h1, sv0 = _mlp_ple_fwd(0, x1, hm0, p[0], row("ple_norm", 0), row("mix_norm", 1),
                               W["w_up"][0], W["w_down"][0], W["w_ple"][0], W["w_ple_gate"][0])
    qkvz = _mm("dn_in_qkvz", h1, W["dn_w_qkvz"])
    ab = _mm("dn_in_ab", h1, W["dn_w_ab"])
    u, q, k, v, gb = _dn_prep(qkvz, ab, W["dn_conv"], alog_row, dt_row)
    o_dn, states = _dn_chunk_fwd(q, k, v, gb)
    on = _dn_post(o_dn, qkvz, ogain_row)
    x4, hm1 = _mm("dn_out", on, W["dn_w_o"], epilogue=_res_norm, extras=(x3, row("mlp_norm", 1)),
                  out_dtypes=(F32, BF16), tm_pref=FUSED_ROWS)
    dy, sq, sv1 = _mlp_ple_fwd(1, x4, hm1, p[1], row("ple_norm", 1), None,
                               W["w_up"][1], W["w_down"][1], W["w_ple"][1], W["w_ple_gate"][1], target=target)

    dx4, dx4b, g1 = _mlp_ple_bwd(1, dy, sv1, row("mlp_norm", 1), row("ple_norm", 1),
                                 W["w_up"][1], W["w_down"][1], W["w_ple_gate"][1])
    don = _mm("dn_out_dx", dx4b, W["dn_w_o"], tb=True)
    d_dn_w_o = _mm("dn_out_dw", on, dx4b, ta=True, out_dtypes=(BF16,), tk_pref=DW_DEPTH)
    do_dn, dz, d_ogain = _dn_post_bwd(don, o_dn, qkvz, ogain_row)
    dq, dk, dv, dgb = _dn_chunk_bwd(q, k, v, gb, states, do_dn)
    du, dab, d_alog, d_dt = _dn_prep_bwd(dq, dk, dv, dgb, u, ab, alog_row, dt_row)
    dqkvz, d_conv = _dn_conv_bwd(du, dz, qkvz, W["dn_conv"])
    dh1 = _mm("dn_in_ab_dx", dab, W["dn_w_ab"], tb=True)
    dx3, d_mix1 = _mm("dn_in_qkvz_dx", dqkvz, W["dn_w_qkvz"], tb=True,
                      epilogue=lambda acc, part, xt, g, dres: _norm_bwd(acc + part, xt, g, dres),
                      extras=(dh1, x3, row("mix_norm", 1), dx4), n_colsums=1, tm_pref=FUSED_ROWS, tk_pref=MXU_TILE)
    d_w_qkvz = _mm("dn_in_qkvz_dw", h1, dqkvz, ta=True, out_dtypes=(BF16,), tn_pref=MXU_TILE, tk_pref=DW_DEPTH)
    d_w_ab = _mm("dn_in_ab_dw", h1, dab, ta=True, out_dtypes=(BF16,))
    token = send_layer1(dict(
        dn_w_qkvz=d_w_qkvz, dn_w_ab=d_w_ab, dn_conv=jnp.concatenate(d_conv, 0), dn_w_o=d_dn_w_o,
        w_up=g1["w_up"], w_down=g1["w_down"], w_ple=g1["w_ple"], w_ple_gate=g1["w_ple_gate"]))
    dx1, dx1b, g0 = _mlp_ple_bwd(0, dx3, sv0, row("mlp_norm", 0), _after(row("ple_norm", 0), token),
                                 W["w_up"][0], W["w_down"][0], W["w_ple_gate"][0])
    token = send_mlp0(dict(w_up=g0["w_up"], w_down=g0["w_down"], w_ple=g0["w_ple"], w_ple_gate=g0["w_ple_gate"]))
    do_attn = _mm("attn_out_dx", dx1b, W["attn_w_o"], tb=True, epilogue=_add, extras=(_after(jnp.zeros((1, A_WIDTH), F32), token),))
    d_attn_w_o = _mm("attn_out_dw", o_attn, dx1b, ta=True, out_dtypes=(BF16,))
    dos, cs = _attn_merge_bwd(do_attn, os_, lses, bd1)
    grads9 = []
    for g in range(3):
        grads9 += list(_attn_bwd(qkvn[g], g, dos[g], lses[g], cs[g]))
    dqkv, dgains = _attn_prep_bwd(qkv, grads9, gains, ct, st, consts, bd)
    d_attn_w_qkv = _mm("attn_qkv_dw", h0, dqkv, ta=True, out_dtypes=(BF16,))
    token = send_attn(dict(attn_w_qkv=d_attn_w_qkv, attn_w_o=d_attn_w_o))
    dx0, d_mix0 = _mm("attn_qkv_dx", dqkv, W["attn_w_qkv"], tb=True, epilogue=_norm_bwd,
                      extras=(x, _after(row("mix_norm", 0), token), dx1), n_colsums=1, tm_pref=FUSED_ROWS,
                      tk_pref=A_QKV // 3)

    dg = jnp.stack([t.reshape(A_HEADS, A_HEAD_DIM).sum(0) for t in dgains])
    small = dict(
        mix_norm=jnp.concatenate([d_mix0, d_mix1], 0),
        attn_q_gain=dg[0::2][None], attn_k_gain=dg[1::2][None],
        dn_a_log=d_alog[:, :DN_HEADS], dn_dt_bias=d_dt[:, :DN_HEADS], dn_o_gain=d_ogain,
        mlp_norm=jnp.concatenate([g0["mlp_norm"], g1["mlp_norm"]], 0),
        ple_norm=jnp.concatenate([g0["ple_norm"], g1["ple_norm"]], 0),
    )
    return sq, dx0, small


MESH_IDS = pl.DeviceIdType.MESH
ANY = pl.BlockSpec(memory_space=pl.ANY)


def _place():
    return lax.axis_index("x"), lax.axis_index("y"), lax.axis_index("c")


def _sem_scratch(n_streams):
    return [pltpu.SemaphoreType.DMA((n_streams, N_DEV - 1)), pltpu.SemaphoreType.DMA((n_streams, N_DEV - 1)),
            pltpu.SemaphoreType.DMA((n_streams,))]


def _all_gather(name, arrays, streams):
    n_in, n_st = len(arrays), len(streams)
    shapes = [arrays[a].shape if li is None else arrays[a].shape[1:] for a, li in streams]

    def body(*refs):
        in_refs, out_refs, token = refs[:n_in], refs[n_in:n_in + n_st], refs[n_in + n_st]
        send_sems, recv_sems, local_sems = refs[n_in + n_st + 1:]
        token[...] = jnp.zeros_like(token)
        x, y, c = _place()
        me, sibling = (x, y, c), (x, y, 1 - c)
        chips = [(1 - x, y), (x, 1 - y), (1 - x, 1 - y)]

        def copy(s, k, block, to, own=False):
            a, li = streams[s]
            dst = out_refs[s].at[4 * block[0] + 2 * block[1] + block[2]]
            src = (in_refs[a] if li is None else in_refs[a].at[li]) if own else dst
            return pltpu.make_async_remote_copy(src_ref=src, dst_ref=dst, send_sem=send_sems.at[s, k],
                                                recv_sem=recv_sems.at[s, k], device_id=to, device_id_type=MESH_IDS)

        started = []
        for s, (a, li) in enumerate(streams):
            src = in_refs[a] if li is None else in_refs[a].at[li]
            mine = pltpu.make_async_copy(src, out_refs[s].at[4 * x + 2 * y + c], local_sems.at[s])
            mine.start()
            started.append(mine)
        sends = []
        for s in range(n_st):
            first = [copy(s, 0, me, sibling, own=True)]
            first += [copy(s, 1 + j, me, (*chip, c), own=True) for j, chip in enumerate(chips)]
            for cp in first:
                cp.start()
            sends += first
        for j, chip in enumerate(chips):
            for s in range(n_st):
                copy(s, 1 + j, (*chip, c), me).wait_recv()
                fwd = copy(s, 4 + j, (*chip, c), sibling)
                fwd.start()
                sends.append(fwd)
        for s in range(n_st):
            copy(s, 0, sibling, me).wait_recv()
            for j, chip in enumerate(chips):
                copy(s, 4 + j, (*chip, 1 - c), me).wait_recv()
        for cp in sends:
            cp.wait_send()
        for cp in started:
            cp.wait()

    res = pl.pallas_call(
        body, name=name,
        out_shape=[jax.ShapeDtypeStruct((N_DEV,) + tuple(sh), arrays[a].dtype) for sh, (a, _) in zip(shapes, streams)]
        + [jax.ShapeDtypeStruct((SUBLANE, LANE), F32)],
        in_specs=[ANY] * n_in, out_specs=[ANY] * n_st + [pl.BlockSpec(memory_space=pltpu.VMEM)],
        scratch_shapes=_sem_scratch(n_st),
    )(*arrays)
    return list(res[:n_st]), res[n_st]


HBM = pl.BlockSpec(memory_space=pltpu.HBM)
SEM = pl.BlockSpec(memory_space=pltpu.SEMAPHORE)
FLOWS = pltpu.CompilerParams(has_side_effects=pltpu.SideEffectType.DATAFLOW_SIDE_EFFECTING)


def _in_hbm(a):
    return pltpu.with_memory_space_constraint(a, pltpu.HBM)


def _hbm_like(a):
    return pltpu.HBM(a.shape, a.dtype)


def _peers(x, y, c):
    return [(1 - x if k & 4 else x, 1 - y if k & 2 else y, 1 - c if k & 1 else c) for k in range(1, N_DEV)]


def _start_copies(name, n_remote, n_own, make_copies, operands):
    n = len(operands)

    def body(*refs):
        for cp in make_copies(refs[:n], refs[n], refs[n + 1], refs[n + 2]):
            cp.start()
        refs[-1][...] = jnp.zeros_like(refs[-1])

    res = pl.pallas_call(
        body, name=name,
        out_shape=(pltpu.SemaphoreType.DMA((n_remote,)), pltpu.SemaphoreType.DMA((n_remote,)), pltpu.SemaphoreType.DMA((n_own,)),
                   *[_hbm_like(t) for t in operands], jax.ShapeDtypeStruct((SUBLANE, LANE), F32)),
        in_specs=[HBM] * n, out_specs=(SEM, SEM, SEM, *[HBM] * n, pl.BlockSpec(memory_space=pltpu.VMEM)),
        input_output_aliases={i: 3 + i for i in range(n)}, compiler_params=FLOWS,
    )(*[_in_hbm(t) for t in operands])
    return res[:3], list(res[3:3 + n]), res[-1]


def _wait_copies(name, make_waits, sems, operands, after):
    n = len(operands)

    def body(*refs):
        for wait in make_waits(refs[:n], refs[n], refs[n + 1], refs[n + 2]):
            wait()

    res = pl.pallas_call(
        body, name=name, out_shape=tuple(_hbm_like(t) for t in operands),
        in_specs=[HBM] * n + [SEM, SEM, SEM, ANY], out_specs=tuple([HBM] * n),
        input_output_aliases={i: i for i in range(n)}, compiler_params=FLOWS,
    )(*operands, *sems, after)
    return list(res)


def _gather_plan(n_in, streams):
    def block(arr, s):
        a, li = streams[s]
        return arr[a] if li is None else arr[a].at[li]

    def copies(refs, send_sems, recv_sems, own_sems, arrivals=False):
        arr, land = refs[:n_in], refs[n_in:]
        x, y, c = _place()
        me = 4 * x + 2 * y + c
        out = []
        for s in range(len(streams)):
            out.append(("own", pltpu.make_async_copy(block(arr, s), land[s].at[me], own_sems.at[s])))
            for k, (px, py, pc) in enumerate(_peers(x, y, c)):
                out.append(("remote", pltpu.make_async_remote_copy(
                    src_ref=block(arr, s), dst_ref=land[s].at[4 * px + 2 * py + pc if arrivals else me],
                    send_sem=send_sems.at[s * (N_DEV - 1) + k], recv_sem=recv_sems.at[s * (N_DEV - 1) + k],
                    device_id=(px, py, pc), device_id_type=MESH_IDS)))
        return out
    return copies


def _exchange_plan(n_st):
    def copies(refs, send_sems, recv_sems, own_sems, arrivals=False):
        snd, rcv = refs[:n_st], refs[n_st:]
        x, y, c = _place()
        me = 4 * x + 2 * y + c
        out = []
        for s in range(n_st):
            out.append(("own", pltpu.make_async_copy(snd[s].at[me], rcv[s].at[me], own_sems.at[s])))
            for k, (px, py, pc) in enumerate(_peers(x, y, c)):
                peer = 4 * px + 2 * py + pc
                out.append(("remote", pltpu.make_async_remote_copy(
                    src_ref=snd[s].at[peer], dst_ref=rcv[s].at[peer if arrivals else me],
                    send_sem=send_sems.at[s * (N_DEV - 1) + k], recv_sem=recv_sems.at[s * (N_DEV - 1) + k],
                    device_id=(px, py, pc), device_id_type=MESH_IDS)))
        return out
    return copies


def _split_transfer(tag, plan, n_streams, operands):
    sems, operands, token = _start_copies(f"{tag}_start", n_streams * (N_DEV - 1), n_streams,
                                          lambda refs, a, b, o: [cp for _, cp in plan(refs, a, b, o)], operands)

    def waits(refs, a, b, o):
        out = []
        for kind, cp in plan(refs, a, b, o, arrivals=True):
            out += [cp.wait] if kind == "own" else [cp.wait_send, cp.wait_recv]
        return out

    return (lambda after: _wait_copies(f"{tag}_wait", waits, sems, operands, after)), token


def _gather_async(tag, arrays, streams):
    lands = [lax.empty((N_DEV,) + tuple(arrays[a].shape if li is None else arrays[a].shape[1:]), arrays[a].dtype)
             for a, li in streams]
    finish, token = _split_transfer(tag, _gather_plan(len(arrays), streams), len(streams), list(arrays) + lands)
    return (lambda after: finish(after)[len(arrays):]), token


def _exchange_async(tag, sends):
    recvs = [lax.empty(t.shape, t.dtype) for t in sends]
    finish, token = _split_transfer(tag, _exchange_plan(len(sends)), len(sends), list(sends) + recvs)
    return (lambda after: finish(after)[len(sends):]), token


def _dn_in_pieces():
    n = (DN_QKVZ + 2 * DN_HEADS) // N_DEV
    segs = ((0, DN_QKV, 0, 0), (DN_QKV, DN_QKV + 2 * DN_HEADS, 1, 0), (DN_QKV + 2 * DN_HEADS, DN_QKVZ + 2 * DN_HEADS, 0, DN_QKV))
    out = []
    for d in range(N_DEV):
        lo, hi = d * n, (d + 1) * n
        for s0, s1, tgt, t0 in segs:
            a, b = max(lo, s0), min(hi, s1)
            if a < b:
                out.append((d, a - lo, b - lo, tgt, t0 + a - s0))
    return out


def _unpack_cols(name, g):
    _, K, n = g.shape
    tr = 256

    def body(g_ref, o_ref):
        for d in range(N_DEV):
            o_ref[:, d * n:(d + 1) * n] = g_ref[d]

    return pl.pallas_call(
        body, name=name, grid=(K // tr,), in_specs=[pl.BlockSpec((N_DEV, tr, n), lambda i: (0, i, 0))],
        out_specs=pl.BlockSpec((tr, N_DEV * n), lambda i: (i, 0)),
        out_shape=jax.ShapeDtypeStruct((K, N_DEV * n), g.dtype), compiler_params=_cparams(("parallel",)),
    )(g)


def _pack_cols(name, w):
    K, n = w.shape[0], w.shape[1] // N_DEV
    tr = 256

    def body(w_ref, o_ref):
        for d in range(N_DEV):
            o_ref[d] = w_ref[:, d * n:(d + 1) * n]

    return pl.pallas_call(
        body, name=name, grid=(K // tr,), in_specs=[pl.BlockSpec((tr, N_DEV * n), lambda i: (i, 0))],
        out_specs=pl.BlockSpec((N_DEV, tr, n), lambda i: (0, i, 0)),
        out_shape=jax.ShapeDtypeStruct((N_DEV, K, n), w.dtype), compiler_params=_cparams(("parallel",)),
    )(w)


def _unpack_dn_in(g):
    _, K, n = g.shape
    tr = 256

    def body(g_ref, qkvz_ref, ab_ref):
        ab_ref[...] = jnp.zeros_like(ab_ref)
        for d, c0, c1, tgt, t0 in _dn_in_pieces():
            (qkvz_ref, ab_ref)[tgt][:, t0:t0 + c1 - c0] = g_ref[d, :, c0:c1]

    return pl.pallas_call(
        body, name="unpack_dn_in", grid=(K // tr,), in_specs=[pl.BlockSpec((N_DEV, tr, n), lambda i: (0, i, 0))],
        out_specs=[pl.BlockSpec((tr, DN_QKVZ), lambda i: (i, 0)), pl.BlockSpec((tr, LANE), lambda i: (i, 0))],
        out_shape=[jax.ShapeDtypeStruct((K, DN_QKVZ), g.dtype), jax.ShapeDtypeStruct((K, LANE), g.dtype)],
        compiler_params=_cparams(("parallel",)),
    )(g)


def _pack_dn_in(d_qkvz, d_ab):
    K = d_qkvz.shape[0]
    n = (DN_QKVZ + 2 * DN_HEADS) // N_DEV
    tr = 256

    def body(qkvz_ref, ab_ref, o_ref):
        for d, c0, c1, tgt, t0 in _dn_in_pieces():
            o_ref[d, :, c0:c1] = (qkvz_ref, ab_ref)[tgt][:, t0:t0 + c1 - c0]

    return pl.pallas_call(
        body, name="pack_dn_in", grid=(K // tr,),
        in_specs=[pl.BlockSpec((tr, DN_QKVZ), lambda i: (i, 0)), pl.BlockSpec((tr, LANE), lambda i: (i, 0))],
        out_specs=pl.BlockSpec((N_DEV, tr, n), lambda i: (0, i, 0)),
        out_shape=jax.ShapeDtypeStruct((N_DEV, K, n), d_qkvz.dtype), compiler_params=_cparams(("parallel",)),
    )(d_qkvz, d_ab)


ADAMW_ROWS = 256


def _adamw(name, parts, w, m, v):
    n_layers, R, C = w.shape
    tr = min(R, ADAMW_ROWS)
    assert R % tr == 0 and len(parts) == n_layers and all(p.shape == (N_DEV, R, C) for p in parts)
    c1 = 1.0 - B1 ** STEP
    c2 = 1.0 - B2 ** STEP

    def body(*refs):
        p_refs = refs[:n_layers]
        w_ref, m_ref, v_ref, g_ref, d_ref, nm_ref, nv_ref = refs[n_layers:]
        layer = pl.program_id(0)
        for li, p_ref in enumerate(p_refs):
            @pl.when(layer == li)
            def _(p_ref=p_ref):
                g = p_ref[0].astype(F32)
                for dev in range(1, N_DEV):
                    g = g + p_ref[dev].astype(F32)
                nm = B1 * m_ref[...] + (1.0 - B1) * g
                nv = B2 * v_ref[...] + (1.0 - B2) * jnp.square(g)
                g_ref[...] = g
                nm_ref[...] = nm
                nv_ref[...] = nv
                d_ref[...] = -LR * ((nm / c1) / (jnp.sqrt(nv / c2) + ADAM_EPS) + WD * w_ref[...])

    blk = pl.BlockSpec((None, tr, C), lambda l, i: (l, i, 0))
    return pl.pallas_call(
        body, name=name, grid=(n_layers, R // tr),
        in_specs=[pl.BlockSpec((N_DEV, tr, C), lambda l, i: (0, i, 0))] * n_layers + [blk, blk, blk],
        out_specs=[blk] * 4, out_shape=[jax.ShapeDtypeStruct((n_layers, R, C), F32)] * 4,
        compiler_params=_cparams(("parallel", "parallel")),
    )(*parts, w, m, v)


SMALL = ("mix_norm", "attn_q_gain", "attn_k_gain", "dn_a_log", "dn_dt_bias", "dn_o_gain", "mlp_norm", "ple_norm")
WEIGHTS = ("mix_norm", "attn_w_qkv", "attn_q_gain", "attn_k_gain", "attn_w_o", "dn_w_in", "dn_conv", "dn_a_log",
           "dn_dt_bias", "dn_o_gain", "dn_w_o", "mlp_norm", "w_up", "w_down", "ple_norm", "w_ple", "w_ple_gate")


def _to_rows(flat, multiple):
    n = flat.shape[-1]
    rows = -(-n // (LANE * multiple)) * multiple
    return jnp.pad(flat, [(0, rows * LANE - n)]).reshape(rows, LANE)


def _cols_to_devices(w):
    K, N = w.shape
    return jnp.transpose(w.reshape(K, N_DEV, N // N_DEV), (1, 0, 2))


def _cols_from_devices(g):
    _, K, n = g.shape
    return jnp.transpose(g, (1, 0, 2)).reshape(K, N_DEV * n)


SMALL_ROWS = 96


def _pack_small(vals, loss_rows):
    rows = [_to_rows(vals[n].reshape(-1), SUBLANE) for n in SMALL] + [loss_rows]
    buf = jnp.concatenate(rows, 0)
    assert buf.shape == (SMALL_ROWS, LANE)
    return buf


def _unpack_small(buf, like):
    out, r = {}, 0
    for n in SMALL:
        sz = math.prod(like[n].shape)
        out[n] = buf[r:r + -(-sz // LANE)].reshape(-1)[:sz].reshape(like[n].shape)
        r += -(-sz // (LANE * SUBLANE)) * SUBLANE
    return out


def kernel(x, p, positions, mix_norm, attn_w_qkv, attn_q_gain, attn_k_gain, attn_w_o, dn_w_in, dn_conv, dn_a_log, dn_dt_bias, dn_o_gain, dn_w_o, mlp_norm, w_up, w_down, ple_norm, w_ple, w_ple_gate, loss_target, m_mix_norm, m_attn_w_qkv, m_attn_q_gain, m_attn_k_gain, m_attn_w_o, m_dn_w_in, m_dn_conv, m_dn_a_log, m_dn_dt_bias, m_dn_o_gain, m_dn_w_o, m_mlp_norm, m_w_up, m_w_down, m_ple_norm, m_w_ple, m_w_ple_gate, v_mix_norm, v_attn_w_qkv, v_attn_q_gain, v_attn_k_gain, v_attn_w_o, v_dn_w_in, v_dn_conv, v_dn_a_log, v_dn_dt_bias, v_dn_o_gain, v_dn_w_o, v_mlp_norm, v_w_up, v_w_down, v_ple_norm, v_w_ple, v_w_ple_gate):
    w = dict(mix_norm=mix_norm, attn_w_qkv=attn_w_qkv, attn_q_gain=attn_q_gain, attn_k_gain=attn_k_gain, attn_w_o=attn_w_o,
             dn_w_in=dn_w_in, dn_conv=dn_conv, dn_a_log=dn_a_log, dn_dt_bias=dn_dt_bias, dn_o_gain=dn_o_gain, dn_w_o=dn_w_o,
             mlp_norm=mlp_norm, w_up=w_up, w_down=w_down, ple_norm=ple_norm, w_ple=w_ple, w_ple_gate=w_ple_gate)
    m = dict(mix_norm=m_mix_norm, attn_w_qkv=m_attn_w_qkv, attn_q_gain=m_attn_q_gain, attn_k_gain=m_attn_k_gain,
             attn_w_o=m_attn_w_o, dn_w_in=m_dn_w_in, dn_conv=m_dn_conv, dn_a_log=m_dn_a_log, dn_dt_bias=m_dn_dt_bias,
             dn_o_gain=m_dn_o_gain, dn_w_o=m_dn_w_o, mlp_norm=m_mlp_norm, w_up=m_w_up, w_down=m_w_down,
             ple_norm=m_ple_norm, w_ple=m_w_ple, w_ple_gate=m_w_ple_gate)
    v = dict(mix_norm=v_mix_norm, attn_w_qkv=v_attn_w_qkv, attn_q_gain=v_attn_q_gain, attn_k_gain=v_attn_k_gain,
             attn_w_o=v_attn_w_o, dn_w_in=v_dn_w_in, dn_conv=v_dn_conv, dn_a_log=v_dn_a_log, dn_dt_bias=v_dn_dt_bias,
             dn_o_gain=v_dn_o_gain, dn_w_o=v_dn_w_o, mlp_norm=v_mlp_norm, w_up=v_w_up, w_down=v_w_down,
             ple_norm=v_ple_norm, w_ple=v_w_ple, w_ple_gate=v_w_ple_gate)
    S = x.shape[1]

    bf = lambda a: a.astype(BF16)
    rows_to_devices = lambda t: t.reshape(N_DEV, t.shape[0] // N_DEV, t.shape[1])

    (g_qkv, g_ao), token = _all_gather("gather_attn", [bf(attn_w_qkv[0]), bf(attn_w_o[0])], [(0, None), (1, None)])
    rest_shards = [bf(dn_w_in[0]), bf(dn_w_o[0]), bf(w_up), bf(w_down), bf(w_ple), bf(w_ple_gate), _after(dn_conv[0], token)]
    rest_streams = [(0, None), (1, None), (2, 0), (2, 1), (3, 0), (3, 1), (4, 0), (4, 1), (5, 0), (5, 1), (6, None)]
    rest_arrived, token = _gather_async("gather_rest", rest_shards, rest_streams)
    W = dict(attn_w_qkv=_unpack_cols("unpack_attn_qkv", g_qkv), attn_w_o=_cols_from_devices(g_ao))

    def rest_of_weights(after):
        g_in, g_do, g_up0, g_up1, g_dn0, g_dn1, g_pl0, g_pl1, g_gt0, g_gt1, g_conv = rest_arrived(after)
        rest = dict(
            dn_conv=jnp.transpose(g_conv, (1, 0, 2)).reshape(CONV_W, DN_QKV), dn_w_o=g_do.reshape(DN_WIDTH, D_MODEL),
            w_up=[_cols_from_devices(g_up0), _cols_from_devices(g_up1)],
            w_down=[g_dn0.reshape(D_FF, D_MODEL), g_dn1.reshape(D_FF, D_MODEL)],
            w_ple=[_cols_from_devices(g_pl0), _cols_from_devices(g_pl1)],
            w_ple_gate=[g_gt0.reshape(D_MODEL, D_MODEL), g_gt1.reshape(D_MODEL, D_MODEL)])
        rest["dn_w_qkvz"], rest["dn_w_ab"] = _unpack_dn_in(g_in)
        return rest

    pending = {}

    def mlp_sends(g):
        return [_cols_to_devices(g["w_up"]), rows_to_devices(g["w_down"]), _cols_to_devices(g["w_ple"]),
                rows_to_devices(g["w_ple_gate"])]

    def start(tag, sends):
        pending[tag], token = _exchange_async(f"exchange_{tag}", sends)
        return token

    def send_layer1(g):
        conv_send = jnp.transpose(g["dn_conv"].reshape(CONV_W, N_DEV, DN_QKV // N_DEV), (1, 0, 2))
        return start("layer1", [_pack_dn_in(g["dn_w_qkvz"], g["dn_w_ab"]), conv_send, rows_to_devices(g["dn_w_o"])] + mlp_sends(g))

    def send_mlp0(g):
        return start("mlp0", mlp_sends(g))

    def send_attn(g):
        return start("attn", [_pack_cols("pack_attn_qkv", g["attn_w_qkv"]), _cols_to_devices(g["attn_w_o"])])

    P = dict(mix_norm=_after(mix_norm, token), attn_q_gain=attn_q_gain[0], attn_k_gain=attn_k_gain[0], dn_a_log=dn_a_log[0],
             dn_dt_bias=dn_dt_bias[0], dn_o_gain=dn_o_gain[0], mlp_norm=mlp_norm, ple_norm=ple_norm)

    sq, dx0, small_g = _local_step(x[0], p[:, 0], positions.reshape(S, 1), loss_target[0], W, P,
                                   rest_of_weights, send_layer1, send_mlp0, send_attn)

    r_in, r_conv, r_do, r_up1, r_dn1, r_pl1, r_gt1 = pending["layer1"](dx0)
    r_up0, r_dn0, r_pl0, r_gt0 = pending["mlp0"](dx0)
    r_qkv, r_ao = pending["attn"](dx0)
    big = {}
    for n, parts in (("attn_w_qkv", [r_qkv]), ("attn_w_o", [r_ao]), ("dn_w_in", [r_in]), ("dn_conv", [r_conv]),
                     ("dn_w_o", [r_do]), ("w_up", [r_up0, r_up1]), ("w_down", [r_dn0, r_dn1]),
                     ("w_ple", [r_pl0, r_pl1]), ("w_ple_gate", [r_gt0, r_gt1])):
        big[n] = _adamw(f"adamw_{n}", parts, w[n], m[n], v[n])

    loss_rows = jnp.pad((0.5 / D_MODEL) * jnp.sum(sq, axis=1, keepdims=True), ((0, SUBLANE - 1), (0, LANE - 1)))
    small_like = {n: w[n] for n in SMALL}
    parts_s = _all_gather("gather_small", [_pack_small(small_g, loss_rows)], [(0, None)])[0][0]
    zero_rows = jnp.zeros((SUBLANE, LANE), F32)
    small = _adamw("adamw_small", [parts_s], _pack_small(w, zero_rows)[None], _pack_small(m, zero_rows)[None],
                   _pack_small(v, zero_rows)[None])
    loss = small[0][0, SMALL_ROWS - SUBLANE, 0]
    small = [_unpack_small(b[0], small_like) for b in small]

    outs = [loss, dx0[None]]
    for k in range(4):
        for n in WEIGHTS:
            outs.append(small[k][n] if n in SMALL else big[n][k])
    return tuple(outs)
```

```python
import functools
import math

import jax
import jax.numpy as jnp
from jax import lax
from jax.experimental import pallas as pl
from jax.experimental.pallas import tpu as pltpu

F32 = jnp.float32
BF16 = jnp.bfloat16
HIGHEST = lax.Precision.HIGHEST

N_DEV = 8
D_MODEL = 1024
EPS = 1e-6
SWA_GROUPS = ((128, 1), (512, 4), (2048, 16))
A_HEADS = 8
A_HEAD_DIM = 64
A_WIDTH = A_HEADS * A_HEAD_DIM
A_QKV = 3 * 3 * A_WIDTH
ROPE_DIM = 16
ROPE_HALF = 8
ROPE_THETA = 500000.0
BAND = 128
DN_HEADS = 8
DN_DIM = 128
DN_WIDTH = DN_HEADS * DN_DIM
CONV_W = 4
CHUNK = 64
D_FF = 4 * D_MODEL
PLE_DIM = 256
LR, B1, B2, ADAM_EPS, WD, STEP = 0.001, 0.9, 0.999, 1e-08, 0.01, 10

VMEM_LIMIT = 56 * 1024 * 1024
MXU_TILE = 1024
MM_SLAB = 256
LANE = 128
SUBLANE = 8


def _cparams(sem):
    return pltpu.CompilerParams(dimension_semantics=sem, vmem_limit_bytes=VMEM_LIMIT)


def _tile(n, pref):
    if n <= pref:
        return n
    t = (pref // LANE) * LANE
    while t >= LANE:
        if n % t == 0:
            return t
        t -= LANE
    raise ValueError(f"no tile for {n}")


def _dot(a, b, ca=1, cb=0, precision=None):
    return lax.dot_general(a, b, (((ca,), (cb,)), ((), ())), precision=precision,
                           preferred_element_type=F32)


def _bdot(a, b, ca=1, cb=0):
    return _dot(a.astype(BF16), b.astype(BF16), ca, cb)


def _mm(name, a, b, *, ta=False, tb=False, epilogue=None, extras=(), out_dtypes=(F32,), n_colsums=0,
        tm_pref=MXU_TILE, tn_pref=1536, tk_pref=2 * MXU_TILE):
    M, K = (a.shape[1], a.shape[0]) if ta else a.shape
    N = b.shape[0] if tb else b.shape[1]
    assert (b.shape[1] if tb else b.shape[0]) == K
    tm, tn, tk = _tile(M, tm_pref), _tile(N, tn_pref), _tile(K, tk_pref)
    nk = K // tk
    n_out = len(out_dtypes)
    n_ext = len(extras)
    assert n_colsums == 0 or tn == N
    sub = min(tm, MM_SLAB)

    def body(*refs):
        a_ref, b_ref = refs[0], refs[1]
        ext = refs[2:2 + n_ext]
        outs = refs[2 + n_ext:2 + n_ext + n_out]
        sums = refs[2 + n_ext + n_out:2 + n_ext + n_out + n_colsums]
        row_tile, k = pl.program_id(0), pl.program_id(2)
        slabs = [slice(s * sub, (s + 1) * sub) for s in range(tm // sub)]

        def product(rows):
            return _bdot(a_ref[:, rows] if ta else a_ref[rows, :], b_ref[...], 0 if ta else 1, 1 if tb else 0)

        def finish(results):
            col_rows = []
            for rows, r in zip(slabs, results):
                res = (r,) if epilogue is None else epilogue(r, *[e[...] if e.shape[0] == 1 else e[rows, :] for e in ext])
                for o, v in zip(outs, res):
                    o[rows, :] = v.astype(o.dtype)
                col_rows.append(res[n_out:])
            for n, o in enumerate(sums):
                v = functools.reduce(lambda x, y: x + y, [c[n] for c in col_rows])

                @pl.when(row_tile == 0)
                def _(o=o, v=v):
                    o[...] = v

                @pl.when(row_tile > 0)
                def _(o=o, v=v):
                    o[...] += v

        if nk == 1:
            finish([product(rows) for rows in slabs])
            return
        acc = refs[-1]

        @pl.when(k == 0)
        def _():
            acc[...] = jnp.zeros_like(acc)

        for rows in slabs:
            acc[rows, :] += product(rows)

        @pl.when(k == nk - 1)
        def _():
            finish([acc[rows, :] for rows in slabs])

    a_spec = pl.BlockSpec((tk, tm), lambda i, j, k: (k, i)) if ta else pl.BlockSpec((tm, tk), lambda i, j, k: (i, k))
    b_spec = pl.BlockSpec((tn, tk), lambda i, j, k: (j, k)) if tb else pl.BlockSpec((tk, tn), lambda i, j, k: (k, j))
    ext_specs = []
    for e in extras:
        if e.shape[0] == 1 and M != 1:
            ext_specs.append(pl.BlockSpec((1, tn), lambda i, j, k: (0, j)))
        else:
            ext_specs.append(pl.BlockSpec((tm, tn), lambda i, j, k: (i, j)))
    out = pl.pallas_call(
        body, name=name,
        grid=(M // tm, N // tn, nk),
        in_specs=[a_spec, b_spec] + ext_specs,
        out_specs=[pl.BlockSpec((tm, tn), lambda i, j, k: (i, j)) for _ in range(n_out)]
        + [pl.BlockSpec((1, tn), lambda i, j, k: (0, 0)) for _ in range(n_colsums)],
        out_shape=[jax.ShapeDtypeStruct((M, N), dt) for dt in out_dtypes]
        + [jax.ShapeDtypeStruct((1, N), F32) for _ in range(n_colsums)],
        scratch_shapes=[pltpu.VMEM((tm, tn), F32)] if nk > 1 else [],
        compiler_params=_cparams(("arbitrary" if n_colsums else "parallel", "parallel", "arbitrary")),
    )(a, b, *extras)
    return out[0] if len(out) == 1 else tuple(out)


def _perm_matrices(tr, d):
    import numpy as np
    old = np.arange(tr)
    p = np.zeros((tr, tr), np.float32)
    p[(old % d) * (tr // d) + old // d, old] = 1.0
    return jnp.asarray(p, BF16), jnp.asarray(p.T, BF16)


def _permute(p, x, parts=3):
    if x.dtype == BF16:
        return _dot(p, x)
    hi = x.astype(BF16)
    rest = x - hi.astype(F32)
    mid = rest.astype(BF16)
    if parts == 2:
        return _dot(p, hi) + _dot(p, mid)
    lo = (rest - mid.astype(F32)).astype(BF16)
    return _dot(p, hi) + _dot(p, mid) + _dot(p, lo)


def _rows(name, fn, ins, outs, *, tr, accs=(), perm_parts=3):
    ins = [(e[0], e[1]) + (e[2] if len(e) > 2 else (0, e[0].shape[-1])) for e in ins]
    outs = [tuple(o) + (0,) * (3 - len(o)) for o in outs]
    n_rows = next(e[0].shape[0] if e[1] == "row" else e[0].shape[0] * e[0].shape[1] for e in ins if e[1] in ("row", "res"))
    assert n_rows % tr == 0 and tr % SUBLANE == 0
    steps = n_rows // tr
    t8 = tr // SUBLANE
    n8 = n_rows // SUBLANE
    dils = sorted({e[0].shape[0] for e in ins if e[1] == "res" and e[0].shape[0] > 1} | {o[2] for o in outs if o[2] > 1})
    perms = [m for d in dils for m in _perm_matrices(tr, d)]
    ins = ins + [(m, "full", 0, tr) for m in perms]
    n_in, n_out, n_acc = len(ins), len(outs), len(accs)

    def body(*refs):
        i = pl.program_id(0)
        to_res = {d: refs[n_in - len(perms) + 2 * j][...] for j, d in enumerate(dils)}
        to_tok = {d: refs[n_in - len(perms) + 2 * j + 1][...] for j, d in enumerate(dils)}
        tiles = []
        for r, e in zip(refs[:n_in - len(perms)], ins):
            d = e[0].shape[0] if e[1] == "res" else 0
            if d == 0:
                tiles.append(r[...])
            elif d == 1:
                tiles.append(r[0])
            else:
                tiles.append(_permute(to_tok[d], jnp.concatenate([r[j] for j in range(d)], axis=0), perm_parts))
        vals = fn(i, steps, *tiles)
        if not isinstance(vals, (tuple, list)):
            vals = (vals,)
        assert len(vals) == n_out + n_acc
        for o, v, (_, dt, d) in zip(refs[n_in:n_in + n_out], vals[:n_out], outs):
            if d == 0:
                o[...] = v.astype(o.dtype)
            elif d == 1:
                o[0] = v.astype(o.dtype)
            else:
                y = _permute(to_res[d], v.astype(dt))
                for j in range(d):
                    o[j] = y[j * (tr // d):(j + 1) * (tr // d)].astype(o.dtype)
        if n_acc:
            acc_refs = refs[n_in + n_out:]

            @pl.when(i == 0)
            def _():
                for r in acc_refs:
                    r[...] = jnp.zeros_like(r)

            for r, v in zip(acc_refs, vals[n_out:]):
                r[...] += v.astype(r.dtype)

    in_specs = []
    for a, kind, cb, c in ins:
        if kind == "row":
            in_specs.append(pl.BlockSpec((tr, c), lambda i, cb=cb: (i, cb)))
        elif kind == "full":
            in_specs.append(pl.BlockSpec(a.shape, lambda i, z=(0,) * a.ndim: z))
        elif kind == "prev8":
            in_specs.append(pl.BlockSpec((SUBLANE, c), lambda i, cb=cb: (jnp.maximum(i * t8 - 1, 0), cb)))
        elif kind == "next8":
            in_specs.append(pl.BlockSpec((SUBLANE, c), lambda i, cb=cb: (jnp.minimum((i + 1) * t8, n8 - 1), cb)))
        elif kind == "res":
            d = a.shape[0]
            in_specs.append(pl.BlockSpec((d, tr // d, a.shape[2]), lambda i: (0, i, 0)))
        else:
            raise ValueError(kind)
    out_specs = [pl.BlockSpec((tr, c), lambda i: (i, 0)) if d == 0 else pl.BlockSpec((d, tr // d, c), lambda i: (0, i, 0))
                 for c, _, d in outs]
    out_specs += [pl.BlockSpec(s, lambda i, z=(0,) * len(s): z) for s, _ in accs]
    out_shape = [jax.ShapeDtypeStruct((n_rows, c) if d == 0 else (d, n_rows // d, c), dt) for c, dt, d in outs]
    out_shape += [jax.ShapeDtypeStruct(s, dt) for s, dt in accs]
    res = pl.pallas_call(
        body, name=name, grid=(steps,), in_specs=in_specs, out_specs=out_specs, out_shape=out_shape,
        compiler_params=_cparams(("arbitrary",) if n_acc else ("parallel",)),
    )(*[e[0] for e in ins])
    return res[0] if len(res) == 1 else tuple(res)


def _colsum(x):
    return jnp.sum(x, axis=0, keepdims=True)


def _sum_all(x):
    return jnp.sum(jnp.sum(x, axis=1, keepdims=True), axis=0, keepdims=True)


def _rmsnorm_fwd(name, x, gain):
    def fn(i, n, xt, g):
        r = lax.rsqrt(jnp.mean(xt * xt, axis=-1, keepdims=True) + EPS)
        return (xt * r * g,)
    return _rows(name, fn, [(x, "row"), (gain, "full")], [(x.shape[1], BF16)], tr=512)


FUSED_ROWS = 1024


def _res_norm(acc, res, g):
    x = res + acc
    return x, x * lax.rsqrt(jnp.mean(x * x, axis=-1, keepdims=True) + EPS) * g


def _norm_bwd(dh, x, g, dres):
    r = lax.rsqrt(jnp.mean(x * x, axis=-1, keepdims=True) + EPS)
    xh = x * r
    dxn = dh * g
    dx = dres + r * (dxn - xh * jnp.mean(dxn * xh, axis=-1, keepdims=True))
    return dx, _colsum(dh * xh)


def _norm_bwd_2(dh, x, g, dres):
    dx, dg = _norm_bwd(dh, x, g, dres)
    return dx, dx, dg


def _head_consts():
    import numpy as np
    e = np.arange(A_WIDTH) % A_HEAD_DIM
    inv = (np.float32(ROPE_THETA) ** (-np.arange(0, ROPE_DIM, 2, dtype=np.float32) / np.float32(ROPE_DIM))).astype(np.float32)
    c = np.zeros((8, A_WIDTH), np.float32)
    c[0] = np.where(e < ROPE_DIM, inv[e % ROPE_HALF], 0.0)
    c[1] = np.where(e < ROPE_HALF, -1.0, np.where(e < ROPE_DIM, 1.0, 0.0))
    c[2] = (e < ROPE_HALF).astype(np.float32)
    c[3] = (e < ROPE_DIM).astype(np.float32)
    return jnp.asarray(c)


def _block_diag(scale):
    import numpy as np
    h = np.arange(A_WIDTH) // A_HEAD_DIM
    return jnp.asarray((h[:, None] == h[None, :]).astype(np.float32) * scale, dtype=BF16)


def _seg_sum(x, bd):
    return _dot(x.astype(BF16), bd)


def _rope_tables(positions, consts):
    def fn(i, n, pos, c):
        ang = pos.astype(F32) * c[0:1, :LANE]
        return jnp.cos(ang), jnp.sin(ang) * c[1:2, :LANE]
    return _rows("rope_tables", fn, [(positions, "row"), (consts, "full")], [(LANE, F32), (LANE, F32)], tr=512)


def _rope_wide(t):
    return jnp.concatenate([t] * (A_WIDTH // LANE), axis=1)


def _rope_apply(y, ct, st, low):
    rolled = jnp.where(low, pltpu.roll(y, A_WIDTH - ROPE_HALF, 1), pltpu.roll(y, ROPE_HALF, 1))
    return y * ct + rolled * st


def _rope_apply_bwd(dout, ct, st, low, in16):
    t = dout * st
    back = jnp.where(low, pltpu.roll(t, A_WIDTH - ROPE_HALF, 1), jnp.where(in16, pltpu.roll(t, ROPE_HALF, 1), 0.0))
    return dout * ct + back


def _attn_prep(qkv, gains, ct, st, consts, bd):
    def fn(i, n, t, g, c_t, s_t, c, b):
        low = c[2:3, :] > 0.5
        c_t, s_t = _rope_wide(c_t), _rope_wide(s_t)
        groups = []
        for grp in range(3):
            cols = []
            for which in range(3):
                off = (grp * 3 + which) * A_WIDTH
                x = t[:, off:off + A_WIDTH].astype(F32)
                if which == 2:
                    cols.append(x.astype(BF16))
                    continue
                r = lax.rsqrt(_seg_sum(x * x, b) + EPS)
                y = x * r * g[grp * 2 + which:grp * 2 + which + 1, :]
                cols.append(_rope_apply(y, c_t, s_t, low).astype(BF16))
            groups.append(jnp.concatenate(cols, axis=1))
        return tuple(groups)
    return _rows("attn_prep", fn, [(qkv, "row"), (gains, "full"), (ct, "row"), (st, "row"), (consts, "full"), (bd, "full")],
                 [(3 * A_WIDTH, BF16, d) for _, d in SWA_GROUPS], tr=256)


def _band_mask(n):
    row = lax.broadcasted_iota(jnp.int32, (BAND, 2 * BAND), 0)
    col = lax.broadcasted_iota(jnp.int32, (BAND, 2 * BAND), 1)
    dist = row + BAND - col
    return (dist >= 0) & (dist <= BAND) & ((col >= BAND) | (n > 0))


def _attn_fwd(qkvn, grp):
    d, L, _ = qkvn.shape
    nblk = L // BAND
    assert L % BAND == 0 and d == SWA_GROUPS[grp][1]

    def body(q_ref, kc_ref, kp_ref, vc_ref, vp_ref, o_ref, lse_ref):
        n = pl.program_id(1)
        valid = _band_mask(n)
        first = lax.broadcasted_iota(jnp.int32, (BAND, LANE), 1) < A_HEAD_DIM
        pairs = [slice(pr * LANE, (pr + 1) * LANE) for pr in range(A_WIDTH // LANE)]
        halves = (first, jnp.logical_not(first))
        qps = [q_ref[:, sl] for sl in pairs]
        kcats = [jnp.concatenate([kp_ref[:, sl], kc_ref[:, sl]], axis=0) for sl in pairs]
        vcats = [jnp.concatenate([vp_ref[:, sl], vc_ref[:, sl]], axis=0) for sl in pairs]
        heads = [(pr, m) for pr in range(len(pairs)) for m in halves]
        ss = [_dot(jnp.where(m, qps[pr], jnp.zeros_like(qps[pr])), kcats[pr], 1, 1) for pr, m in heads]
        ps, lses = [], []
        for s in ss:
            s = jnp.where(valid, s * (A_HEAD_DIM ** -0.5), -1e30)
            mx = jnp.max(s, axis=-1, keepdims=True)
            e = jnp.exp(s - mx)
            l = jnp.sum(e, axis=-1, keepdims=True)
            ps.append((e / l).astype(BF16))
            lses.append(mx + jnp.log(l))
        os_ = [_dot(p, vcats[pr]) for p, (pr, _) in zip(ps, heads)]
        o_ref[...] = jnp.concatenate([jnp.where(first, os_[2 * pr], os_[2 * pr + 1]) for pr in range(len(pairs))], axis=1)
        lse_ref[...] = jnp.concatenate([jnp.where(first, lses[2 * pr], lses[2 * pr + 1]) for pr in range(len(pairs))], axis=1)

    blk = (None, BAND, A_WIDTH)
    return pl.pallas_call(
        body, name=f"attn_fwd_g{grp}", grid=(d, nblk),
        in_specs=[pl.BlockSpec(blk, lambda r, n: (r, n, 0)),
                  pl.BlockSpec(blk, lambda r, n: (r, n, 1)),
                  pl.BlockSpec(blk, lambda r, n: (r, jnp.maximum(n - 1, 0), 1)),
                  pl.BlockSpec(blk, lambda r, n: (r, n, 2)),
                  pl.BlockSpec(blk, lambda r, n: (r, jnp.maximum(n - 1, 0), 2))],
        out_specs=[pl.BlockSpec(blk, lambda r, n: (r, n, 0)), pl.BlockSpec(blk, lambda r, n: (r, n, 0))],
        out_shape=[jax.ShapeDtypeStruct((d, L, A_WIDTH), F32)] * 2,
        compiler_params=_cparams(("parallel", "parallel")),
    )(qkvn, qkvn, qkvn, qkvn, qkvn)


def _merge_weights(l0, l1, l2):
    mx = jnp.maximum(jnp.maximum(l0, l1), l2)
    e0, e1, e2 = jnp.exp(l0 - mx), jnp.exp(l1 - mx), jnp.exp(l2 - mx)
    inv = 1.0 / (e0 + e1 + e2)
    return e0 * inv, e1 * inv, e2 * inv


def _attn_merge(os_, lses):
    def fn(i, n, o0, o1, o2, l0, l1, l2):
        w0, w1, w2 = _merge_weights(l0, l1, l2)
        return (w0 * o0 + w1 * o1 + w2 * o2,)
    ins = [(a, "res") for a in (*os_, *lses)]
    return _rows("attn_merge", fn, ins, [(A_WIDTH, BF16)], tr=256)


def _attn_merge_bwd(do, os_, lses, bd1):
    def fn(i, n, dot_, o0, o1, o2, l0, l1, l2, b):
        w0, w1, w2 = _merge_weights(l0, l1, l2)
        o = w0 * o0 + w1 * o1 + w2 * o2
        dsum = _seg_sum(dot_ * o, b)
        return (w0 * dot_, w1 * dot_, w2 * dot_, -w0 * dsum, -w1 * dsum, -w2 * dsum)
    ins = [(do, "row")] + [(a, "res") for a in (*os_, *lses)] + [(bd1, "full")]
    res = _rows("attn_merge_bwd", fn, ins, [(A_WIDTH, dt, d) for dt in (BF16, F32) for _, d in SWA_GROUPS], tr=256)
    return res[:3], res[3:]


def _lane_pick(x, lane_idx, lane):
    return jnp.sum(jnp.where(lane_idx == lane, x, 0.0), axis=-1, keepdims=True)


def _attn_bwd(qkvn, grp, do_g, lse, c_g):
    d, L, _ = qkvn.shape
    nblk = L // BAND

    def body(q_ref, kc_ref, kp_ref, vc_ref, vp_ref, do_ref, lse_ref, c_ref, dq_ref, dk_ref, dv_ref, ck, cv_):
        n = pl.program_id(1)

        @pl.when(n == 0)
        def _():
            ck[...] = jnp.zeros_like(ck)
            cv_[...] = jnp.zeros_like(cv_)

        @pl.when(n < nblk)
        def _():
            valid = _band_mask(n)
            lane = lax.broadcasted_iota(jnp.int32, (BAND, LANE), 1)
            first = lane < A_HEAD_DIM
            lane2 = lax.broadcasted_iota(jnp.int32, (2 * BAND, LANE), 1) < A_HEAD_DIM
            pairs = [slice(pr * LANE, (pr + 1) * LANE) for pr in range(A_WIDTH // LANE)]
            halves = (first, jnp.logical_not(first))
            qps = [q_ref[:, sl] for sl in pairs]
            dops = [do_ref[:, sl] for sl in pairs]
            kcats = [jnp.concatenate([kp_ref[:, sl], kc_ref[:, sl]], axis=0) for sl in pairs]
            vcats = [jnp.concatenate([vp_ref[:, sl], vc_ref[:, sl]], axis=0) for sl in pairs]
            heads = [(pr, hh) for pr in range(len(pairs)) for hh in range(2)]
            zero = jnp.zeros_like(qps[0])
            ss = [_dot(jnp.where(halves[hh], qps[pr], zero), kcats[pr], 1, 1) for pr, hh in heads]
            dps = [_dot(jnp.where(halves[hh], dops[pr], zero), vcats[pr], 1, 1) for pr, hh in heads]
            dss, pbs = [], []
            for (pr, hh), s, dp in zip(heads, ss, dps):
                lse_h = _lane_pick(lse_ref[:, pairs[pr]], lane, hh * A_HEAD_DIM)
                c_h = _lane_pick(c_ref[:, pairs[pr]], lane, hh * A_HEAD_DIM)
                p = jnp.where(valid, jnp.exp(s * (A_HEAD_DIM ** -0.5) - lse_h), 0.0)
                dss.append((p * (dp + c_h) * (A_HEAD_DIM ** -0.5)).astype(BF16))
                pbs.append(p.astype(BF16))
            dqs = [_dot(ds, kcats[pr]) for ds, (pr, _) in zip(dss, heads)]
            dks = [_dot(ds, qps[pr], 0, 0) for ds, (pr, _) in zip(dss, heads)]
            dvs = [_dot(pb, dops[pr], 0, 0) for pb, (pr, _) in zip(pbs, heads)]
            for pr, sl in enumerate(pairs):
                dq_ref[:, sl] = jnp.where(first, dqs[2 * pr], dqs[2 * pr + 1])
                dkc = jnp.where(lane2, dks[2 * pr], dks[2 * pr + 1])
                dvc = jnp.where(lane2, dvs[2 * pr], dvs[2 * pr + 1])
                dk_ref[:, sl] = ck[:, sl] + dkc[:BAND]
                dv_ref[:, sl] = cv_[:, sl] + dvc[:BAND]
                ck[:, sl] = dkc[BAND:]
                cv_[:, sl] = dvc[BAND:]

        @pl.when(n == nblk)
        def _():
            dk_ref[...] = ck[...]
            dv_ref[...] = cv_[...]

    blk = (None, BAND, A_WIDTH)
    last = nblk - 1
    qn = lambda n: jnp.minimum(n, last)
    pn = lambda n: jnp.clip(n - 1, 0, last)
    return tuple(pl.pallas_call(
        body, name=f"attn_bwd_g{grp}", grid=(d, nblk + 1),
        in_specs=[pl.BlockSpec(blk, lambda r, n: (r, qn(n), 0)),
                  pl.BlockSpec(blk, lambda r, n: (r, qn(n), 1)),
                  pl.BlockSpec(blk, lambda r, n: (r, pn(n), 1)),
                  pl.BlockSpec(blk, lambda r, n: (r, qn(n), 2)),
                  pl.BlockSpec(blk, lambda r, n: (r, pn(n), 2)),
                  pl.BlockSpec(blk, lambda r, n: (r, qn(n), 0)),
                  pl.BlockSpec(blk, lambda r, n: (r, qn(n), 0)),
                  pl.BlockSpec(blk, lambda r, n: (r, qn(n), 0))],
        out_specs=[pl.BlockSpec(blk, lambda r, n: (r, qn(n), 0)),
                   pl.BlockSpec(blk, lambda r, n: (r, pn(n), 0)),
                   pl.BlockSpec(blk, lambda r, n: (r, pn(n), 0))],
        out_shape=[jax.ShapeDtypeStruct((d, L, A_WIDTH), F32)] * 3,
        scratch_shapes=[pltpu.VMEM((BAND, A_WIDTH), F32), pltpu.VMEM((BAND, A_WIDTH), F32)],
        compiler_params=_cparams(("parallel", "arbitrary")),
    )(qkvn, qkvn, qkvn, qkvn, qkvn, do_g, lse, c_g))


def _attn_prep_bwd(qkv, grads, gains, ct, st, consts, bd):
    def fn(i, n, t, g, c_t, s_t, c, b, *gr):
        low = c[2:3, :] > 0.5
        in16 = c[3:4, :] > 0.5
        c_t, s_t = _rope_wide(c_t), _rope_wide(s_t)
        cols, dgs = [], []
        for grp in range(3):
            for which in range(3):
                dout = gr[grp * 3 + which]
                if which == 2:
                    cols.append(dout.astype(BF16))
                    continue
                off = (grp * 3 + which) * A_WIDTH
                x = t[:, off:off + A_WIDTH].astype(F32)
                gain = g[grp * 2 + which:grp * 2 + which + 1, :]
                r = lax.rsqrt(_seg_sum(x * x, b) + EPS)
                xh = x * r
                dy = _rope_apply_bwd(dout, c_t, s_t, low, in16)
                dyn = dy * gain
                dx = r * (dyn - xh * _seg_sum(dyn * xh, b))
                cols.append(dx.astype(BF16))
                dgs.append(_colsum(dy * xh))
        return (jnp.concatenate(cols, axis=1), *dgs)
    ins = [(qkv, "row"), (gains, "full"), (ct, "row"), (st, "row"), (consts, "full"), (bd, "full")] + [(a, "res") for a in grads]
    res = _rows("attn_prep_bwd", fn, ins, [(A_QKV, BF16)], tr=128, accs=[((1, A_WIDTH), F32)] * 6, perm_parts=2)
    return res[0], res[1:]


DN_QKV = 3 * DN_WIDTH
DN_QKVZ = DN_QKV + DN_WIDTH


def _sigmoid(x):
    return jax.nn.sigmoid(x)


def _softplus(x):
    return jnp.maximum(x, 0.0) + jnp.log(1.0 + jnp.exp(-jnp.abs(x)))


def _conv_taps(xs, w, tr):
    acc = None
    for j in range(CONV_W):
        sh = CONV_W - 1 - j
        term = (pltpu.roll(xs, sh, 0) if sh else xs)[SUBLANE:] * w[j:j + 1, :]
        acc = term if acc is None else acc + term
    return acc


def _dn_prep(qkvz, ab, convw, alog_row, dt_row):
    tr = 256

    def fn(i, n, x, xp, abt, w, al, dt):
        xp = jnp.where(i > 0, xp, 0.0)
        u = _conv_taps(jnp.concatenate([xp, x], axis=0), w, tr)
        y = u * _sigmoid(u)
        qs, ks = [], []
        for h in range(DN_HEADS):
            for dst, base, sc in ((qs, 0, DN_DIM ** -0.5), (ks, DN_WIDTH, 1.0)):
                seg = y[:, base + h * DN_DIM:base + (h + 1) * DN_DIM]
                dst.append(seg * (lax.rsqrt(jnp.sum(seg * seg, axis=-1, keepdims=True) + EPS) * sc))
        lane = lax.broadcasted_iota(jnp.int32, abt.shape, 1)
        g = -jnp.exp(al) * _softplus(abt + dt)
        gb = jnp.where(lane < DN_HEADS, g, jnp.where(lane < 2 * DN_HEADS, _sigmoid(abt), 0.0))
        return u, jnp.concatenate(qs, axis=1), jnp.concatenate(ks, axis=1), y[:, 2 * DN_WIDTH:], gb

    ins = [(qkvz, "row", (0, DN_QKV)), (qkvz, "prev8", (0, DN_QKV)), (ab, "row"), (convw, "full"),
           (alog_row, "full"), (dt_row, "full")]
    return _rows("dn_prep", fn, ins, [(DN_QKV, BF16), (DN_WIDTH, F32), (DN_WIDTH, F32), (DN_WIDTH, F32), (LANE, F32)], tr=tr)


def _tri_masks():
    row = lax.broadcasted_iota(jnp.int32, (CHUNK, CHUNK), 0)
    col = lax.broadcasted_iota(jnp.int32, (CHUNK, CHUNK), 1)
    return row >= col, row > col, row == col


def _heads(fn, *lists):
    return [fn(*xs) for xs in zip(*lists)]


def _split(x):
    hi = x.astype(BF16)
    return hi, (x - hi.astype(F32)).astype(BF16)


def _dot3(a, b, ca=1, cb=0):
    (ah, al), (bh, bl) = a, b
    return _dot(ah, bh, ca, cb) + (_dot(ah, bl, ca, cb) + _dot(al, bh, ca, cb))


SPLIT_STEPS = 3


def _unit_lower_inverse(a_list, eye):
    ts = [eye - a for a in a_list]
    parts = [_split(a) for a in a_list]
    for step in range(5):
        if step < SPLIT_STEPS:
            parts = [_split(_dot3(p, p)) for p in parts]
            ts = [t + _dot3(_split(t), p) for t, p in zip(ts, parts)]
        else:
            parts = [(_dot(p[0], p[0]).astype(BF16), None) for p in parts]
            ts = [t + _dot(t.astype(BF16), p[0]) for t, p in zip(ts, parts)]
    return ts


def _dn_terms(qs, ks, vs, gb, solved=None):
    lower, strict, diag = _tri_masks()
    lane = lax.broadcasted_iota(jnp.int32, (CHUNK, LANE), 1)
    is_last = lax.broadcasted_iota(jnp.int32, (CHUNK, 1), 0) == CHUNK - 1
    hs = range(DN_HEADS)
    gc = _dot(lower.astype(F32), gb, precision=HIGHEST)
    gct = jnp.transpose(gc)
    bcol = [_lane_pick(gb, lane, DN_HEADS + h) for h in hs]
    gcol = [_lane_pick(gc, lane, h) for h in hs]
    glast = [jnp.sum(jnp.where(is_last, g, 0.0), axis=0, keepdims=True) for g in gcol]
    decay = [jnp.exp(jnp.where(lower, gcol[h] - gct[h:h + 1, :], -1e30)) for h in hs]
    kb = _heads(lambda k, b: k * b, ks, bcol)
    both = _heads(lambda q, x, k: _bdot(jnp.concatenate([q, x], axis=0), k, 1, 1), qs, kb, ks)
    qk = [x[:CHUNK] for x in both]
    kk = [x[CHUNK:] for x in both]
    a = _heads(lambda x, d: jnp.where(strict, x * d, 0.0), kk, decay)
    eg = [jnp.exp(g) for g in gcol]
    egl = _heads(lambda gl, g: jnp.exp(gl - g), glast, gcol)
    rhs_w = _heads(lambda x, e: x * e, kb, eg)
    if solved is None:
        t_full = _unit_lower_inverse(a, diag.astype(F32))
        t = [_split(x) for x in t_full]
        uw = _heads(lambda tt, v, b, r: _dot3(tt, _split(jnp.concatenate([v * b, r], axis=1))), t, vs, bcol, rhs_w)
        u = [x[:, :DN_DIM] for x in uw]
        w = [x[:, DN_DIM:] for x in uw]
    else:
        t_full, u, w = solved
        t = [_split(x) for x in t_full]
    return dict(bcol=bcol, decay=decay, kb=kb, a=a, t=t, t_full=t_full, eg=eg, egl=egl, rhs_w=rhs_w, u=u, w=w,
                attn=_heads(lambda x, d: x * d, qk, decay), q_dec=_heads(lambda q, e: q * e, qs, eg),
                k_dec=_heads(lambda k, e: k * e, ks, egl), c_dec=[jnp.exp(g) for g in glast],
                lower=lower, strict=strict, lane=lane, is_last=is_last)


def _head_slices(ref):
    return [ref[:, h * DN_DIM:(h + 1) * DN_DIM] for h in range(DN_HEADS)]


def _dn_chunk_fwd(q, k, v, gb):
    S = q.shape[0]
    N = S // CHUNK

    def body(q_ref, k_ref, v_ref, gb_ref, o_ref, st_ref, t_ref, u_ref, w_ref, state):
        @pl.when(pl.program_id(0) == 0)
        def _():
            state[...] = jnp.zeros_like(state)

        f = _dn_terms(_head_slices(q_ref), _head_slices(k_ref), _head_slices(v_ref), gb_ref[...])
        s = [state[h] for h in range(DN_HEADS)]
        for h in range(DN_HEADS):
            st_ref[0, h] = s[h]
            t_ref[0, h] = f["t_full"][h]
            u_ref[:, h * DN_DIM:(h + 1) * DN_DIM] = f["u"][h]
            w_ref[:, h * DN_DIM:(h + 1) * DN_DIM] = f["w"][h]
        sb = [x.astype(BF16) for x in s]
        v_new = _heads(lambda u, w, x: u - _bdot(w, x), f["u"], f["w"], sb)
        o = _heads(lambda qd, x, at, vn: _bdot(qd, x) + _bdot(at, vn), f["q_dec"], sb, f["attn"], v_new)
        new_s = _heads(lambda x, c, kd, vn: x * c + _bdot(kd, vn, 0, 0), s, f["c_dec"], f["k_dec"], v_new)
        for h in range(DN_HEADS):
            o_ref[:, h * DN_DIM:(h + 1) * DN_DIM] = o[h]
            state[h] = new_s[h]

    blk = pl.BlockSpec((CHUNK, DN_WIDTH), lambda n: (n, 0))
    st_blk = pl.BlockSpec((1, DN_HEADS, DN_DIM, DN_DIM), lambda n: (n, 0, 0, 0))
    t_blk = pl.BlockSpec((1, DN_HEADS, CHUNK, CHUNK), lambda n: (n, 0, 0, 0))
    wide = jax.ShapeDtypeStruct((S, DN_WIDTH), F32)
    o, states, t, u, w = pl.pallas_call(
        body, name="dn_chunk_fwd", grid=(N,),
        in_specs=[blk, blk, blk, pl.BlockSpec((CHUNK, LANE), lambda n: (n, 0))],
        out_specs=[blk, st_blk, t_blk, blk, blk],
        out_shape=[wide, jax.ShapeDtypeStruct((N, DN_HEADS, DN_DIM, DN_DIM), F32),
                   jax.ShapeDtypeStruct((N, DN_HEADS, CHUNK, CHUNK), F32), wide, wide],
        scratch_shapes=[pltpu.VMEM((DN_HEADS, DN_DIM, DN_DIM), F32)],
        compiler_params=_cparams(("arbitrary",)),
    )(q, k, v, gb)
    return o, (states, t, u, w)


def _dn_chunk_bwd(q, k, v, gb, saved, do):
    S = q.shape[0]
    N = S // CHUNK
    states, t_saved, u_saved, w_saved = saved

    def body(q_ref, k_ref, v_ref, gb_ref, st_ref, t_ref, u_ref, w_ref, do_ref, dq_ref, dk_ref, dv_ref, dgb_ref, dstate):
        @pl.when(pl.program_id(0) == 0)
        def _():
            dstate[...] = jnp.zeros_like(dstate)

        hs = range(DN_HEADS)
        qs, ks, vs, dos = (_head_slices(r) for r in (q_ref, k_ref, v_ref, do_ref))
        f = _dn_terms(qs, ks, vs, gb_ref[...], ([t_ref[0, h] for h in hs], _head_slices(u_ref), _head_slices(w_ref)))
        lane, is_last = f["lane"], f["is_last"]
        rowsum = lambda x: jnp.sum(x, axis=-1, keepdims=True)
        s = [st_ref[0, h] for h in hs]
        dsn = [dstate[h] for h in hs]
        sb = [x.astype(BF16) for x in s]
        dsb = [x.astype(BF16) for x in dsn]
        dob = [x.astype(BF16) for x in dos]
        v_new = _heads(lambda u, w, x: u - _bdot(w, x), f["u"], f["w"], sb)
        dv_new = _heads(lambda at, d, kd, x: _bdot(at, d, 0, 0) + _bdot(kd, x), f["attn"], dob, f["k_dec"], dsb)
        dattn = _heads(lambda d, vn: _bdot(d, vn, 1, 1), dob, v_new)
        dq_dec = _heads(lambda d, x: _bdot(d, x, 1, 1), dob, sb)
        dk_dec = _heads(lambda vn, x: _bdot(vn, x, 1, 1), v_new, dsb)
        dw = _heads(lambda dv_, x: -_bdot(dv_, x, 1, 1), dv_new, sb)
        new_ds = _heads(lambda x, c, qd, d, w, dv_: x * c + _bdot(qd, d, 0, 0) - _bdot(w, dv_, 0, 0),
                        dsn, f["c_dec"], f["q_dec"], dob, f["w"], dv_new)
        for h in hs:
            dstate[h] = new_ds[h]
        drhs = _heads(lambda tt, x, y: _dot3(tt, _split(jnp.concatenate([x, y], axis=1)), 0, 0), f["t"], dv_new, dw)
        drhs_u = [x[:, :DN_DIM] for x in drhs]
        drhs_w = [x[:, DN_DIM:] for x in drhs]
        da = _heads(lambda du_, u, dw_, w: jnp.where(f["strict"], -(_bdot(du_, u, 1, 1) + _bdot(dw_, w, 1, 1)), 0.0),
                    drhs_u, f["u"], drhs_w, f["w"])
        dkk = _heads(lambda x, d: x * d, da, f["decay"])
        dqk = _heads(lambda x, d: x * d, dattn, f["decay"])
        by_k = _heads(lambda x, y, k_: _bdot(jnp.concatenate([x, y], axis=0), k_), dqk, dkk, ks)
        dq = _heads(lambda x, dqd, e: x[:CHUNK] + dqd * e, by_k, dq_dec, f["eg"])
        dkb = _heads(lambda x, dw_, e: x[CHUNK:] + dw_ * e, by_k, drhs_w, f["eg"])
        dk = _heads(lambda x, kb_, y, q_, dkd, el, dkb_, b: _bdot(x, kb_, 0, 0) + _bdot(y, q_, 0, 0) + dkd * el + dkb_ * b,
                    dkk, f["kb"], dqk, qs, dk_dec, f["egl"], dkb, f["bcol"])
        m = _heads(lambda x, a_, y, at: x * a_ + y * at, da, f["a"], dattn, f["attn"])
        ones = jnp.ones((CHUNK, LANE), BF16)
        col_m = [(_dot(mh, ones, 0, 0) + _dot(ml, ones, 0, 0))[:, 0:1] for mh, ml in map(_split, m)]
        dgc_all = jnp.zeros((CHUNK, LANE), F32)
        dbeta_all = jnp.zeros((CHUNK, LANE), F32)
        for h in hs:
            dq_ref[:, h * DN_DIM:(h + 1) * DN_DIM] = dq[h]
            dk_ref[:, h * DN_DIM:(h + 1) * DN_DIM] = dk[h]
            dv_ref[:, h * DN_DIM:(h + 1) * DN_DIM] = drhs_u[h] * f["bcol"][h]
            kdec_term = rowsum(dk_dec[h] * f["k_dec"][h])
            dc_dec = _sum_all(dsn[h] * s[h])
            dgc = (rowsum(m[h]) - col_m[h] + rowsum(dq_dec[h] * f["q_dec"][h]) - kdec_term
                   + rowsum(drhs_w[h] * f["rhs_w"][h]))
            last_extra = jnp.sum(kdec_term, axis=0, keepdims=True) + dc_dec * f["c_dec"][h]
            dgc = dgc + jnp.where(is_last, last_extra, 0.0)
            dbeta = rowsum(drhs_u[h] * vs[h]) + rowsum(dkb[h] * ks[h])
            dgc_all = jnp.where(lane == h, dgc, dgc_all)
            dbeta_all = jnp.where(lane == DN_HEADS + h, dbeta, dbeta_all)
        dg_all = _dot(f["lower"].astype(F32), dgc_all, 0, 0, precision=HIGHEST)
        dgb_ref[...] = jnp.where(lane < DN_HEADS, dg_all, dbeta_all)

    rev = lambda n: (N - 1 - n, 0)
    blk = pl.BlockSpec((CHUNK, DN_WIDTH), rev)
    gblk = pl.BlockSpec((CHUNK, LANE), rev)
    st_blk = pl.BlockSpec((1, DN_HEADS, DN_DIM, DN_DIM), lambda n: (N - 1 - n, 0, 0, 0))
    t_blk = pl.BlockSpec((1, DN_HEADS, CHUNK, CHUNK), lambda n: (N - 1 - n, 0, 0, 0))
    return pl.pallas_call(
        body, name="dn_chunk_bwd", grid=(N,),
        in_specs=[blk, blk, blk, gblk, st_blk, t_blk, blk, blk, blk],
        out_specs=[blk, blk, blk, gblk],
        out_shape=[jax.ShapeDtypeStruct((S, DN_WIDTH), F32)] * 3 + [jax.ShapeDtypeStruct((S, LANE), F32)],
        scratch_shapes=[pltpu.VMEM((DN_HEADS, DN_DIM, DN_DIM), F32)],
        compiler_params=_cparams(("arbitrary",)),
    )(q, k, v, gb, states, t_saved, u_saved, w_saved, do)


def _dn_post(o, qkvz, gain_row):
    def fn(i, n, ot, z, g):
        cols = []
        for h in range(DN_HEADS):
            seg = ot[:, h * DN_DIM:(h + 1) * DN_DIM]
            cols.append(seg * lax.rsqrt(jnp.mean(seg * seg, axis=-1, keepdims=True) + EPS) * g)
        return (jnp.concatenate(cols, axis=1) * (z * _sigmoid(z)),)
    return _rows("dn_post", fn, [(o, "row"), (qkvz, "row", (3, DN_WIDTH)), (gain_row, "full")], [(DN_WIDTH, BF16)], tr=512)


def _dn_post_bwd(don, o, qkvz, gain_row):
    def fn(i, n, dy, ot, z, g):
        sg = _sigmoid(z)
        sz = z * sg
        dos, ohs = [], []
        dg = jnp.zeros((1, DN_DIM), F32)
        for h in range(DN_HEADS):
            sl = slice(h * DN_DIM, (h + 1) * DN_DIM)
            seg = ot[:, sl]
            r = lax.rsqrt(jnp.mean(seg * seg, axis=-1, keepdims=True) + EPS)
            oh = seg * r
            dno = dy[:, sl] * sz[:, sl]
            dg = dg + _colsum(dno * oh)
            dn = dno * g
            dos.append(r * (dn - oh * jnp.mean(dn * oh, axis=-1, keepdims=True)))
            ohs.append(oh * g)
        dz = dy * jnp.concatenate(ohs, axis=1) * (sg * (1.0 + z * (1.0 - sg)))
        return jnp.concatenate(dos, axis=1), dz, dg
    ins = [(don, "row"), (o, "row"), (qkvz, "row", (3, DN_WIDTH)), (gain_row, "full")]
    return _rows("dn_post_bwd", fn, ins, [(DN_WIDTH, F32), (DN_WIDTH, F32)], tr=256, accs=[((1, DN_DIM), F32)])


def _dn_prep_bwd(dq, dk, dv, dgb, u, ab, alog_row, dt_row):
    def fn(i, n, dqt, dkt, dvt, dgbt, ut, abt, al, dt):
        ut = ut.astype(F32)
        sg = _sigmoid(ut)
        y = ut * sg
        dys = []
        for grad, base, sc in ((dqt, 0, DN_DIM ** -0.5), (dkt, DN_WIDTH, 1.0)):
            for h in range(DN_HEADS):
                seg = y[:, base + h * DN_DIM:base + (h + 1) * DN_DIM]
                gr = grad[:, h * DN_DIM:(h + 1) * DN_DIM]
                r = lax.rsqrt(jnp.sum(seg * seg, axis=-1, keepdims=True) + EPS)
                xh = seg * r
                dys.append((r * sc) * (gr - xh * jnp.sum(gr * xh, axis=-1, keepdims=True)))
        dy = jnp.concatenate(dys + [dvt], axis=1)
        du = dy * (sg * (1.0 + ut * (1.0 - sg)))
        lane = lax.broadcasted_iota(jnp.int32, abt.shape, 1)
        is_g = lane < DN_HEADS
        ea = jnp.exp(al)
        x = abt + dt
        slope = -ea * _sigmoid(x)
        gval = -ea * _softplus(x)
        dg = jnp.where(is_g, dgbt, 0.0)
        beta = _sigmoid(abt)
        dab = jnp.where(is_g, dg * slope, jnp.where(lane < 2 * DN_HEADS, dgbt * beta * (1.0 - beta), 0.0))
        return du, dab, _colsum(dg * gval), _colsum(dg * slope)
    ins = [(dq, "row"), (dk, "row"), (dv, "row"), (dgb, "row"), (u, "row"), (ab, "row"), (alog_row, "full"), (dt_row, "full")]
    return _rows("dn_prep_bwd", fn, ins, [(DN_QKV, F32), (LANE, BF16)], tr=256, accs=[((1, LANE), F32)] * 2)


def _dn_conv_bwd(du, dz, qkvz, convw):
    tr = 256

    def fn(i, n, dut, dun, dzt, x, xp, w):
        dun = jnp.where(i < n - 1, dun, 0.0)
        dus = jnp.concatenate([dut, dun], axis=0)
        xs = jnp.concatenate([jnp.where(i > 0, xp, 0.0), x], axis=0)
        dx = None
        dws = []
        for j in range(CONV_W):
            sh = CONV_W - 1 - j
            term = (pltpu.roll(dus, tr + SUBLANE - sh, 0) if sh else dus)[:tr] * w[j:j + 1, :]
            dx = term if dx is None else dx + term
            dws.append(_colsum(dut * (pltpu.roll(xs, sh, 0) if sh else xs)[SUBLANE:]))
        return (jnp.concatenate([dx.astype(BF16), dzt.astype(BF16)], axis=1), *dws)

    ins = [(du, "row"), (du, "next8"), (dz, "row"), (qkvz, "row", (0, DN_QKV)), (qkvz, "prev8", (0, DN_QKV)), (convw, "full")]
    res = _rows("dn_conv_bwd", fn, ins, [(DN_QKVZ, BF16)], tr=tr, accs=[((1, DN_QKV), F32)] * CONV_W)
    return res[0], res[1:]


def _add(acc, r):
    return (r + acc,)


def _mlp_ple_fwd(i, x1, hm, p_i, ple_gain, next_gain, w_up, w_down, w_ple, w_gate, target=None):
    u, a = _mm(f"mlp_up{i}", hm, w_up, epilogue=lambda acc: (acc, jnp.square(jnp.maximum(acc, 0.0))),
               out_dtypes=(BF16, BF16))
    x2, hp = _mm(f"mlp_down{i}", a, w_down, epilogue=_res_norm, extras=(x1, ple_gain), out_dtypes=(F32, BF16),
                 tm_pref=FUSED_ROWS)
    pp = _mm(f"ple_proj{i}", p_i, w_ple)

    def gate_epilogue(acc, x2t, ppt, g):
        gate = _sigmoid(acc)
        x3 = x2t + ppt * gate
        return x3, gate, x3 * lax.rsqrt(jnp.mean(x3 * x3, axis=-1, keepdims=True) + EPS) * g

    def loss_epilogue(acc, x2t, ppt, tt):
        gate = _sigmoid(acc)
        err = x2t + ppt * gate - tt
        dy = err * (1.0 / D_MODEL)
        return dy, dy * gate, dy * ppt * gate * (1.0 - gate), _colsum(err * err)

    saved = dict(x1=x1, hm=hm, u=u, a=a, x2=x2, hp=hp, pp=pp, p=p_i)
    if target is None:
        x3, saved["gate"], h_next = _mm(f"ple_gate{i}", hp, w_gate, epilogue=gate_epilogue, extras=(x2, pp, next_gain),
                                        out_dtypes=(F32, F32, BF16), tm_pref=FUSED_ROWS)
        return x3, h_next, saved
    dy, saved["dpp"], saved["dzg"], sq = _mm(f"ple_gate{i}", hp, w_gate, epilogue=loss_epilogue, extras=(x2, pp, target),
                                             out_dtypes=(F32, BF16, BF16), n_colsums=1, tm_pref=FUSED_ROWS)
    return dy, sq, saved


def _mlp_ple_bwd(i, dx3, sv, mlp_gain, ple_gain, w_up, w_down, w_gate):
    if "dpp" in sv:
        dpp, dzg = sv["dpp"], sv["dzg"]
    else:
        def fn(_i, _n, d, g, pp):
            return d * g, d * pp * g * (1.0 - g)
        dpp, dzg = _rows(f"ple_gate_bwd{i}", fn, [(dx3, "row"), (sv["gate"], "row"), (sv["pp"], "row")],
                         [(D_MODEL, BF16), (D_MODEL, BF16)], tr=512)
    d_w_ple = _mm(f"ple_proj_dw{i}", sv["p"], dpp, ta=True, out_dtypes=(BF16,))
    d_w_gate = _mm(f"ple_gate_dw{i}", sv["hp"], dzg, ta=True, out_dtypes=(BF16,))
    dx2, dx2b, d_ple_gain = _mm(f"ple_gate_dx{i}", dzg, w_gate, tb=True, epilogue=_norm_bwd_2,
                                extras=(sv["x2"], ple_gain, dx3), out_dtypes=(F32, BF16), n_colsums=1, tm_pref=FUSED_ROWS)
    d_w_down = _mm(f"mlp_down_dw{i}", sv["a"], dx2b, ta=True, out_dtypes=(BF16,))
    du = _mm(f"mlp_down_dx{i}", dx2b, w_down, tb=True,
             epilogue=lambda acc, ut: (acc * (2.0 * jnp.maximum(ut.astype(F32), 0.0)),), extras=(sv["u"],), out_dtypes=(BF16,))
    d_w_up = _mm(f"mlp_up_dw{i}", sv["hm"], du, ta=True, out_dtypes=(BF16,))
    dx1, dx1b, d_mlp_gain = _mm(f"mlp_up_dx{i}", du, w_up, tb=True, epilogue=_norm_bwd_2,
                                extras=(sv["x1"], mlp_gain, dx2), out_dtypes=(F32, BF16), n_colsums=1, tm_pref=FUSED_ROWS)
    return dx1, dx1b, dict(w_ple=d_w_ple, w_ple_gate=d_w_gate, w_down=d_w_down, w_up=d_w_up,
                           ple_norm=d_ple_gain, mlp_norm=d_mlp_gain)


def _after(small, token):
    return small + token[0:1, 0:1]


def _local_step(x, p, positions, target, W, P, rest_of_weights, send_layer1, send_mlp0, send_attn):
    consts = _head_consts()
    bd = _block_diag(1.0 / A_HEAD_DIM)
    bd1 = _block_diag(1.0)
    ct, st = _rope_tables(positions, consts)
    gains = jnp.stack([jnp.tile(v, A_HEADS) for g in range(3) for v in (P["attn_q_gain"][g], P["attn_k_gain"][g])])
    pad = LANE - DN_HEADS
    alog_row = jnp.pad(P["dn_a_log"].reshape(1, DN_HEADS), ((0, 0), (0, pad)))
    dt_row = jnp.pad(P["dn_dt_bias"].reshape(1, DN_HEADS), ((0, 0), (0, pad)))
    ogain_row = P["dn_o_gain"].reshape(1, DN_DIM)
    row = lambda name, i: P[name][i:i + 1]

    h0 = _rmsnorm_fwd("mix_norm0", x, row("mix_norm", 0))
    qkv = _mm("attn_qkv", h0, W["attn_w_qkv"], out_dtypes=(BF16,))
    qkvn = _attn_prep(qkv, gains, ct, st, consts, bd)
    os_, lses = zip(*[_attn_fwd(qkvn[g], g) for g in range(3)])
    o_attn = _attn_merge(os_, lses)
    x1, hm0 = _mm("attn_out", o_attn, W["attn_w_o"], epilogue=_res_norm, extras=(x, row("mlp_norm", 0)),
                  out_dtypes=(F32, BF16), tm_pref=FUSED_ROWS)
    W = {**W, **rest_of_weights(x1)}
    x3, h1, sv0 = _mlp_ple_fwd(0, x1, hm0, p[0], row("ple_norm", 0), row("mix_norm", 1),
                               W["w_up"][0], W["w_down"][0], W["w_ple"][0], W["w_ple_gate"][0])
    qkvz = _mm("dn_in_qkvz", h1, W["dn_w_qkvz"])
    ab = _mm("dn_in_ab", h1, W["dn_w_ab"])
    u, q, k, v, gb = _dn_prep(qkvz, ab, W["dn_conv"], alog_row, dt_row)
    o_dn, states = _dn_chunk_fwd(q, k, v, gb)
    on = _dn_post(o_dn, qkvz, ogain_row)
    x4, hm1 = _mm("dn_out", on, W["dn_w_o"], epilogue=_res_norm, extras=(x3, row("mlp_norm", 1)),
                  out_dtypes=(F32, BF16), tm_pref=FUSED_ROWS)
    dy, sq, sv1 = _mlp_ple_fwd(1, x4, hm1, p[1], row("ple_norm", 1), None,
                               W["w_up"][1], W["w_down"][1], W["w_ple"][1], W["w_ple_gate"][1], target=target)

    dx4, dx4b, g1 = _mlp_ple_bwd(1, dy, sv1, row("mlp_norm", 1), row("ple_norm", 1),
                                 W["w_up"][1], W["w_down"][1], W["w_ple_gate"][1])
    don = _mm("dn_out_dx", dx4b, W["dn_w_o"], tb=True)
    d_dn_w_o = _mm("dn_out_dw", on, dx4b, ta=True, out_dtypes=(BF16,))
    do_dn, dz, d_ogain = _dn_post_bwd(don, o_dn, qkvz, ogain_row)
    dq, dk, dv, dgb = _dn_chunk_bwd(q, k, v, gb, states, do_dn)
    du, dab, d_alog, d_dt = _dn_prep_bwd(dq, dk, dv, dgb, u, ab, alog_row, dt_row)
    dqkvz, d_conv = _dn_conv_bwd(du, dz, qkvz, W["dn_conv"])
    dh1 = _mm("dn_in_ab_dx", dab, W["dn_w_ab"], tb=True)
    dx3, d_mix1 = _mm("dn_in_qkvz_dx", dqkvz, W["dn_w_qkvz"], tb=True,
                      epilogue=lambda acc, part, xt, g, dres: _norm_bwd(acc + part, xt, g, dres),
                      extras=(dh1, x3, row("mix_norm", 1), dx4), n_colsums=1, tm_pref=FUSED_ROWS, tk_pref=MXU_TILE)
    d_w_qkvz = _mm("dn_in_qkvz_dw", h1, dqkvz, ta=True, out_dtypes=(BF16,))
    d_w_ab = _mm("dn_in_ab_dw", h1, dab, ta=True, out_dtypes=(BF16,))
    token = send_layer1(dict(
        dn_w_qkvz=d_w_qkvz, dn_w_ab=d_w_ab, dn_conv=jnp.concatenate(d_conv, 0), dn_w_o=d_dn_w_o,
        w_up=g1["w_up"], w_down=g1["w_down"], w_ple=g1["w_ple"], w_ple_gate=g1["w_ple_gate"]))
    dx1, dx1b, g0 = _mlp_ple_bwd(0, dx3, sv0, row("mlp_norm", 0), _after(row("ple_norm", 0), token),
                                 W["w_up"][0], W["w_down"][0], W["w_ple_gate"][0])
    token = send_mlp0(dict(w_up=g0["w_up"], w_down=g0["w_down"], w_ple=g0["w_ple"], w_ple_gate=g0["w_ple_gate"]))
    do_attn = _mm("attn_out_dx", dx1b, W["attn_w_o"], tb=True, epilogue=_add, extras=(_after(jnp.zeros((1, A_WIDTH), F32), token),))
    d_attn_w_o = _mm("attn_out_dw", o_attn, dx1b, ta=True, out_dtypes=(BF16,))
    dos, cs = _attn_merge_bwd(do_attn, os_, lses, bd1)
    grads9 = []
    for g in range(3):
        grads9 += list(_attn_bwd(qkvn[g], g, dos[g], lses[g], cs[g]))
    dqkv, dgains = _attn_prep_bwd(qkv, grads9, gains, ct, st, consts, bd)
    d_attn_w_qkv = _mm("attn_qkv_dw", h0, dqkv, ta=True, out_dtypes=(BF16,))
    token = send_attn(dict(attn_w_qkv=d_attn_w_qkv, attn_w_o=d_attn_w_o))
    dx0, d_mix0 = _mm("attn_qkv_dx", dqkv, W["attn_w_qkv"], tb=True, epilogue=_norm_bwd,
                      extras=(x, _after(row("mix_norm", 0), token), dx1), n_colsums=1, tm_pref=FUSED_ROWS,
                      tk_pref=A_QKV // 3)

    dg = jnp.stack([t.reshape(A_HEADS, A_HEAD_DIM).sum(0) for t in dgains])
    small = dict(
        mix_norm=jnp.concatenate([d_mix0, d_mix1], 0),
        attn_q_gain=dg[0::2][None], attn_k_gain=dg[1::2][None],
        dn_a_log=d_alog[:, :DN_HEADS], dn_dt_bias=d_dt[:, :DN_HEADS], dn_o_gain=d_ogain,
        mlp_norm=jnp.concatenate([g0["mlp_norm"], g1["mlp_norm"]], 0),
        ple_norm=jnp.concatenate([g0["ple_norm"], g1["ple_norm"]], 0),
    )
    return sq, dx0, small


MESH_IDS = pl.DeviceIdType.MESH
ANY = pl.BlockSpec(memory_space=pl.ANY)


def _place():
    return lax.axis_index("x"), lax.axis_index("y"), lax.axis_index("c")


def _sem_scratch(n_streams):
    return [pltpu.SemaphoreType.DMA((n_streams, N_DEV - 1)), pltpu.SemaphoreType.DMA((n_streams, N_DEV - 1)),
            pltpu.SemaphoreType.DMA((n_streams,))]


def _all_gather(name, arrays, streams):
    n_in, n_st = len(arrays), len(streams)
    shapes = [arrays[a].shape if li is None else arrays[a].shape[1:] for a, li in streams]

    def body(*refs):
        in_refs, out_refs, token = refs[:n_in], refs[n_in:n_in + n_st], refs[n_in + n_st]
        send_sems, recv_sems, local_sems = refs[n_in + n_st + 1:]
        token[...] = jnp.zeros_like(token)
        x, y, c = _place()
        me, sibling = (x, y, c), (x, y, 1 - c)
        chips = [(1 - x, y), (x, 1 - y), (1 - x, 1 - y)]

        def copy(s, k, block, to, own=False):
            a, li = streams[s]
            dst = out_refs[s].at[4 * block[0] + 2 * block[1] + block[2]]
            src = (in_refs[a] if li is None else in_refs[a].at[li]) if own else dst
            return pltpu.make_async_remote_copy(src_ref=src, dst_ref=dst, send_sem=send_sems.at[s, k],
                                                recv_sem=recv_sems.at[s, k], device_id=to, device_id_type=MESH_IDS)

        started = []
        for s, (a, li) in enumerate(streams):
            src = in_refs[a] if li is None else in_refs[a].at[li]
            mine = pltpu.make_async_copy(src, out_refs[s].at[4 * x + 2 * y + c], local_sems.at[s])
            mine.start()
            started.append(mine)
        sends = []
        for s in range(n_st):
            first = [copy(s, 0, me, sibling, own=True)]
            first += [copy(s, 1 + j, me, (*chip, c), own=True) for j, chip in enumerate(chips)]
            for cp in first:
                cp.start()
            sends += first
        for j, chip in enumerate(chips):
            for s in range(n_st):
                copy(s, 1 + j, (*chip, c), me).wait_recv()
                fwd = copy(s, 4 + j, (*chip, c), sibling)
                fwd.start()
                sends.append(fwd)
        for s in range(n_st):
            copy(s, 0, sibling, me).wait_recv()
            for j, chip in enumerate(chips):
                copy(s, 4 + j, (*chip, 1 - c), me).wait_recv()
        for cp in sends:
            cp.wait_send()
        for cp in started:
            cp.wait()

    res = pl.pallas_call(
        body, name=name,
        out_shape=[jax.ShapeDtypeStruct((N_DEV,) + tuple(sh), arrays[a].dtype) for sh, (a, _) in zip(shapes, streams)]
        + [jax.ShapeDtypeStruct((SUBLANE, LANE), F32)],
        in_specs=[ANY] * n_in, out_specs=[ANY] * n_st + [pl.BlockSpec(memory_space=pltpu.VMEM)],
        scratch_shapes=_sem_scratch(n_st),
    )(*arrays)
    return list(res[:n_st]), res[n_st]


HBM = pl.BlockSpec(memory_space=pltpu.HBM)
SEM = pl.BlockSpec(memory_space=pltpu.SEMAPHORE)
FLOWS = pltpu.CompilerParams(has_side_effects=pltpu.SideEffectType.DATAFLOW_SIDE_EFFECTING)


def _in_hbm(a):
    return pltpu.with_memory_space_constraint(a, pltpu.HBM)


def _hbm_like(a):
    return pltpu.HBM(a.shape, a.dtype)


def _peers(x, y, c):
    return [(1 - x if k & 4 else x, 1 - y if k & 2 else y, 1 - c if k & 1 else c) for k in range(1, N_DEV)]


def _start_copies(name, n_remote, n_own, make_copies, operands):
    n = len(operands)

    def body(*refs):
        for cp in make_copies(refs[:n], refs[n], refs[n + 1], refs[n + 2]):
            cp.start()
        refs[-1][...] = jnp.zeros_like(refs[-1])

    res = pl.pallas_call(
        body, name=name,
        out_shape=(pltpu.SemaphoreType.DMA((n_remote,)), pltpu.SemaphoreType.DMA((n_remote,)), pltpu.SemaphoreType.DMA((n_own,)),
                   *[_hbm_like(t) for t in operands], jax.ShapeDtypeStruct((SUBLANE, LANE), F32)),
        in_specs=[HBM] * n, out_specs=(SEM, SEM, SEM, *[HBM] * n, pl.BlockSpec(memory_space=pltpu.VMEM)),
        input_output_aliases={i: 3 + i for i in range(n)}, compiler_params=FLOWS,
    )(*[_in_hbm(t) for t in operands])
    return res[:3], list(res[3:3 + n]), res[-1]


def _wait_copies(name, make_waits, sems, operands, after):
    n = len(operands)

    def body(*refs):
        for wait in make_waits(refs[:n], refs[n], refs[n + 1], refs[n + 2]):
            wait()

    res = pl.pallas_call(
        body, name=name, out_shape=tuple(_hbm_like(t) for t in operands),
        in_specs=[HBM] * n + [SEM, SEM, SEM, ANY], out_specs=tuple([HBM] * n),
        input_output_aliases={i: i for i in range(n)}, compiler_params=FLOWS,
    )(*operands, *sems, after)
    return list(res)


def _gather_plan(n_in, streams):
    def block(arr, s):
        a, li = streams[s]
        return arr[a] if li is None else arr[a].at[li]

    def copies(refs, send_sems, recv_sems, own_sems, arrivals=False):
        arr, land = refs[:n_in], refs[n_in:]
        x, y, c = _place()
        me = 4 * x + 2 * y + c
        out = []
        for s in range(len(streams)):
            out.append(("own", pltpu.make_async_copy(block(arr, s), land[s].at[me], own_sems.at[s])))
            for k, (px, py, pc) in enumerate(_peers(x, y, c)):
                out.append(("remote", pltpu.make_async_remote_copy(
                    src_ref=block(arr, s), dst_ref=land[s].at[4 * px + 2 * py + pc if arrivals else me],
                    send_sem=send_sems.at[s * (N_DEV - 1) + k], recv_sem=recv_sems.at[s * (N_DEV - 1) + k],
                    device_id=(px, py, pc), device_id_type=MESH_IDS)))
        return out
    return copies


def _exchange_plan(n_st):
    def copies(refs, send_sems, recv_sems, own_sems, arrivals=False):
        snd, rcv = refs[:n_st], refs[n_st:]
        x, y, c = _place()
        me = 4 * x + 2 * y + c
        out = []
        for s in range(n_st):
            out.append(("own", pltpu.make_async_copy(snd[s].at[me], rcv[s].at[me], own_sems.at[s])))
            for k, (px, py, pc) in enumerate(_peers(x, y, c)):
                peer = 4 * px + 2 * py + pc
                out.append(("remote", pltpu.make_async_remote_copy(
                    src_ref=snd[s].at[peer], dst_ref=rcv[s].at[peer if arrivals else me],
                    send_sem=send_sems.at[s * (N_DEV - 1) + k], recv_sem=recv_sems.at[s * (N_DEV - 1) + k],
                    device_id=(px, py, pc), device_id_type=MESH_IDS)))
        return out
    return copies


def _split_transfer(tag, plan, n_streams, operands):
    sems, operands, token = _start_copies(f"{tag}_start", n_streams * (N_DEV - 1), n_streams,
                                          lambda refs, a, b, o: [cp for _, cp in plan(refs, a, b, o)], operands)

    def waits(refs, a, b, o):
        out = []
        for kind, cp in plan(refs, a, b, o, arrivals=True):
            out += [cp.wait] if kind == "own" else [cp.wait_send, cp.wait_recv]
        return out

    return (lambda after: _wait_copies(f"{tag}_wait", waits, sems, operands, after)), token


def _gather_async(tag, arrays, streams):
    lands = [lax.empty((N_DEV,) + tuple(arrays[a].shape if li is None else arrays[a].shape[1:]), arrays[a].dtype)
             for a, li in streams]
    finish, token = _split_transfer(tag, _gather_plan(len(arrays), streams), len(streams), list(arrays) + lands)
    return (lambda after: finish(after)[len(arrays):]), token


def _exchange_async(tag, sends):
    recvs = [lax.empty(t.shape, t.dtype) for t in sends]
    finish, token = _split_transfer(tag, _exchange_plan(len(sends)), len(sends), list(sends) + recvs)
    return (lambda after: finish(after)[len(sends):]), token


def _dn_in_pieces():
    n = (DN_QKVZ + 2 * DN_HEADS) // N_DEV
    segs = ((0, DN_QKV, 0, 0), (DN_QKV, DN_QKV + 2 * DN_HEADS, 1, 0), (DN_QKV + 2 * DN_HEADS, DN_QKVZ + 2 * DN_HEADS, 0, DN_QKV))
    out = []
    for d in range(N_DEV):
        lo, hi = d * n, (d + 1) * n
        for s0, s1, tgt, t0 in segs:
            a, b = max(lo, s0), min(hi, s1)
            if a < b:
                out.append((d, a - lo, b - lo, tgt, t0 + a - s0))
    return out


def _unpack_cols(name, g):
    _, K, n = g.shape
    tr = 256

    def body(g_ref, o_ref):
        for d in range(N_DEV):
            o_ref[:, d * n:(d + 1) * n] = g_ref[d]

    return pl.pallas_call(
        body, name=name, grid=(K // tr,), in_specs=[pl.BlockSpec((N_DEV, tr, n), lambda i: (0, i, 0))],
        out_specs=pl.BlockSpec((tr, N_DEV * n), lambda i: (i, 0)),
        out_shape=jax.ShapeDtypeStruct((K, N_DEV * n), g.dtype), compiler_params=_cparams(("parallel",)),
    )(g)


def _pack_cols(name, w):
    K, n = w.shape[0], w.shape[1] // N_DEV
    tr = 256

    def body(w_ref, o_ref):
        for d in range(N_DEV):
            o_ref[d] = w_ref[:, d * n:(d + 1) * n]

    return pl.pallas_call(
        body, name=name, grid=(K // tr,), in_specs=[pl.BlockSpec((tr, N_DEV * n), lambda i: (i, 0))],
        out_specs=pl.BlockSpec((N_DEV, tr, n), lambda i: (0, i, 0)),
        out_shape=jax.ShapeDtypeStruct((N_DEV, K, n), w.dtype), compiler_params=_cparams(("parallel",)),
    )(w)


def _unpack_dn_in(g):
    _, K, n = g.shape
    tr = 256

    def body(g_ref, qkvz_ref, ab_ref):
        ab_ref[...] = jnp.zeros_like(ab_ref)
        for d, c0, c1, tgt, t0 in _dn_in_pieces():
            (qkvz_ref, ab_ref)[tgt][:, t0:t0 + c1 - c0] = g_ref[d, :, c0:c1]

    return pl.pallas_call(
        body, name="unpack_dn_in", grid=(K // tr,), in_specs=[pl.BlockSpec((N_DEV, tr, n), lambda i: (0, i, 0))],
        out_specs=[pl.BlockSpec((tr, DN_QKVZ), lambda i: (i, 0)), pl.BlockSpec((tr, LANE), lambda i: (i, 0))],
        out_shape=[jax.ShapeDtypeStruct((K, DN_QKVZ), g.dtype), jax.ShapeDtypeStruct((K, LANE), g.dtype)],
        compiler_params=_cparams(("parallel",)),
    )(g)


def _pack_dn_in(d_qkvz, d_ab):
    K = d_qkvz.shape[0]
    n = (DN_QKVZ + 2 * DN_HEADS) // N_DEV
    tr = 256

    def body(qkvz_ref, ab_ref, o_ref):
        for d, c0, c1, tgt, t0 in _dn_in_pieces():
            o_ref[d, :, c0:c1] = (qkvz_ref, ab_ref)[tgt][:, t0:t0 + c1 - c0]

    return pl.pallas_call(
        body, name="pack_dn_in", grid=(K // tr,),
        in_specs=[pl.BlockSpec((tr, DN_QKVZ), lambda i: (i, 0)), pl.BlockSpec((tr, LANE), lambda i: (i, 0))],
        out_specs=pl.BlockSpec((N_DEV, tr, n), lambda i: (0, i, 0)),
        out_shape=jax.ShapeDtypeStruct((N_DEV, K, n), d_qkvz.dtype), compiler_params=_cparams(("parallel",)),
    )(d_qkvz, d_ab)


ADAMW_ROWS = 256


def _adamw(name, parts, w, m, v):
    n_layers, R, C = w.shape
    tr = min(R, ADAMW_ROWS)
    assert R % tr == 0 and len(parts) == n_layers and all(p.shape == (N_DEV, R, C) for p in parts)
    c1 = 1.0 - B1 ** STEP
    c2 = 1.0 - B2 ** STEP

    def body(*refs):
        p_refs = refs[:n_layers]
        w_ref, m_ref, v_ref, g_ref, d_ref, nm_ref, nv_ref = refs[n_layers:]
        layer = pl.program_id(0)
        for li, p_ref in enumerate(p_refs):
            @pl.when(layer == li)
            def _(p_ref=p_ref):
                g = p_ref[0].astype(F32)
                for dev in range(1, N_DEV):
                    g = g + p_ref[dev].astype(F32)
                nm = B1 * m_ref[...] + (1.0 - B1) * g
                nv = B2 * v_ref[...] + (1.0 - B2) * jnp.square(g)
                g_ref[...] = g
                nm_ref[...] = nm
                nv_ref[...] = nv
                d_ref[...] = -LR * ((nm / c1) / (jnp.sqrt(nv / c2) + ADAM_EPS) + WD * w_ref[...])

    blk = pl.BlockSpec((None, tr, C), lambda l, i: (l, i, 0))
    return pl.pallas_call(
        body, name=name, grid=(n_layers, R // tr),
        in_specs=[pl.BlockSpec((N_DEV, tr, C), lambda l, i: (0, i, 0))] * n_layers + [blk, blk, blk],
        out_specs=[blk] * 4, out_shape=[jax.ShapeDtypeStruct((n_layers, R, C), F32)] * 4,
        compiler_params=_cparams(("parallel", "parallel")),
    )(*parts, w, m, v)


SMALL = ("mix_norm", "attn_q_gain", "attn_k_gain", "dn_a_log", "dn_dt_bias", "dn_o_gain", "mlp_norm", "ple_norm")
WEIGHTS = ("mix_norm", "attn_w_qkv", "attn_q_gain", "attn_k_gain", "attn_w_o", "dn_w_in", "dn_conv", "dn_a_log",
           "dn_dt_bias", "dn_o_gain", "dn_w_o", "mlp_norm", "w_up", "w_down", "ple_norm", "w_ple", "w_ple_gate")


def _to_rows(flat, multiple):
    n = flat.shape[-1]
    rows = -(-n // (LANE * multiple)) * multiple
    return jnp.pad(flat, [(0, rows * LANE - n)]).reshape(rows, LANE)


def _cols_to_devices(w):
    K, N = w.shape
    return jnp.transpose(w.reshape(K, N_DEV, N // N_DEV), (1, 0, 2))


def _cols_from_devices(g):
    _, K, n = g.shape
    return jnp.transpose(g, (1, 0, 2)).reshape(K, N_DEV * n)


SMALL_ROWS = 96


def _pack_small(vals, loss_rows):
    rows = [_to_rows(vals[n].reshape(-1), SUBLANE) for n in SMALL] + [loss_rows]
    buf = jnp.concatenate(rows, 0)
    assert buf.shape == (SMALL_ROWS, LANE)
    return buf


def _unpack_small(buf, like):
    out, r = {}, 0
    for n in SMALL:
        sz = math.prod(like[n].shape)
        out[n] = buf[r:r + -(-sz // LANE)].reshape(-1)[:sz].reshape(like[n].shape)
        r += -(-sz // (LANE * SUBLANE)) * SUBLANE
    return out


def kernel(x, p, positions, mix_norm, attn_w_qkv, attn_q_gain, attn_k_gain, attn_w_o, dn_w_in, dn_conv, dn_a_log, dn_dt_bias, dn_o_gain, dn_w_o, mlp_norm, w_up, w_down, ple_norm, w_ple, w_ple_gate, loss_target, m_mix_norm, m_attn_w_qkv, m_attn_q_gain, m_attn_k_gain, m_attn_w_o, m_dn_w_in, m_dn_conv, m_dn_a_log, m_dn_dt_bias, m_dn_o_gain, m_dn_w_o, m_mlp_norm, m_w_up, m_w_down, m_ple_norm, m_w_ple, m_w_ple_gate, v_mix_norm, v_attn_w_qkv, v_attn_q_gain, v_attn_k_gain, v_attn_w_o, v_dn_w_in, v_dn_conv, v_dn_a_log, v_dn_dt_bias, v_dn_o_gain, v_dn_w_o, v_mlp_norm, v_w_up, v_w_down, v_ple_norm, v_w_ple, v_w_ple_gate):
    w = dict(mix_norm=mix_norm, attn_w_qkv=attn_w_qkv, attn_q_gain=attn_q_gain, attn_k_gain=attn_k_gain, attn_w_o=attn_w_o,
             dn_w_in=dn_w_in, dn_conv=dn_conv, dn_a_log=dn_a_log, dn_dt_bias=dn_dt_bias, dn_o_gain=dn_o_gain, dn_w_o=dn_w_o,
             mlp_norm=mlp_norm, w_up=w_up, w_down=w_down, ple_norm=ple_norm, w_ple=w_ple, w_ple_gate=w_ple_gate)
    m = dict(mix_norm=m_mix_norm, attn_w_qkv=m_attn_w_qkv, attn_q_gain=m_attn_q_gain, attn_k_gain=m_attn_k_gain,
             attn_w_o=m_attn_w_o, dn_w_in=m_dn_w_in, dn_conv=m_dn_conv, dn_a_log=m_dn_a_log, dn_dt_bias=m_dn_dt_bias,
             dn_o_gain=m_dn_o_gain, dn_w_o=m_dn_w_o, mlp_norm=m_mlp_norm, w_up=m_w_up, w_down=m_w_down,
             ple_norm=m_ple_norm, w_ple=m_w_ple, w_ple_gate=m_w_ple_gate)
    v = dict(mix_norm=v_mix_norm, attn_w_qkv=v_attn_w_qkv, attn_q_gain=v_attn_q_gain, attn_k_gain=v_attn_k_gain,
             attn_w_o=v_attn_w_o, dn_w_in=v_dn_w_in, dn_conv=v_dn_conv, dn_a_log=v_dn_a_log, dn_dt_bias=v_dn_dt_bias,
             dn_o_gain=v_dn_o_gain, dn_w_o=v_dn_w_o, mlp_norm=v_mlp_norm, w_up=v_w_up, w_down=v_w_down,
             ple_norm=v_ple_norm, w_ple=v_w_ple, w_ple_gate=v_w_ple_gate)
    S = x.shape[1]

    bf = lambda a: a.astype(BF16)
    rows_to_devices = lambda t: t.reshape(N_DEV, t.shape[0] // N_DEV, t.shape[1])

    (g_qkv, g_ao), token = _all_gather("gather_attn", [bf(attn_w_qkv[0]), bf(attn_w_o[0])], [(0, None), (1, None)])
    rest_shards = [bf(dn_w_in[0]), bf(dn_w_o[0]), bf(w_up), bf(w_down), bf(w_ple), bf(w_ple_gate), _after(dn_conv[0], token)]
    rest_streams = [(0, None), (1, None), (2, 0), (2, 1), (3, 0), (3, 1), (4, 0), (4, 1), (5, 0), (5, 1), (6, None)]
    rest_arrived, token = _gather_async("gather_rest", rest_shards, rest_streams)
    W = dict(attn_w_qkv=_unpack_cols("unpack_attn_qkv", g_qkv), attn_w_o=_cols_from_devices(g_ao))

    def rest_of_weights(after):
        g_in, g_do, g_up0, g_up1, g_dn0, g_dn1, g_pl0, g_pl1, g_gt0, g_gt1, g_conv = rest_arrived(after)
        rest = dict(
            dn_conv=jnp.transpose(g_conv, (1, 0, 2)).reshape(CONV_W, DN_QKV), dn_w_o=g_do.reshape(DN_WIDTH, D_MODEL),
            w_up=[_cols_from_devices(g_up0), _cols_from_devices(g_up1)],
            w_down=[g_dn0.reshape(D_FF, D_MODEL), g_dn1.reshape(D_FF, D_MODEL)],
            w_ple=[_cols_from_devices(g_pl0), _cols_from_devices(g_pl1)],
            w_ple_gate=[g_gt0.reshape(D_MODEL, D_MODEL), g_gt1.reshape(D_MODEL, D_MODEL)])
        rest["dn_w_qkvz"], rest["dn_w_ab"] = _unpack_dn_in(g_in)
        return rest

    pending = {}

    def mlp_sends(g):
        return [_cols_to_devices(g["w_up"]), rows_to_devices(g["w_down"]), _cols_to_devices(g["w_ple"]),
                rows_to_devices(g["w_ple_gate"])]

    def start(tag, sends):
        pending[tag], token = _exchange_async(f"exchange_{tag}", sends)
        return token

    def send_layer1(g):
        conv_send = jnp.transpose(g["dn_conv"].reshape(CONV_W, N_DEV, DN_QKV // N_DEV), (1, 0, 2))
        return start("layer1", [_pack_dn_in(g["dn_w_qkvz"], g["dn_w_ab"]), conv_send, rows_to_devices(g["dn_w_o"])] + mlp_sends(g))

    def send_mlp0(g):
        return start("mlp0", mlp_sends(g))

    def send_attn(g):
        return start("attn", [_pack_cols("pack_attn_qkv", g["attn_w_qkv"]), _cols_to_devices(g["attn_w_o"])])

    P = dict(mix_norm=_after(mix_norm, token), attn_q_gain=attn_q_gain[0], attn_k_gain=attn_k_gain[0], dn_a_log=dn_a_log[0],
             dn_dt_bias=dn_dt_bias[0], dn_o_gain=dn_o_gain[0], mlp_norm=mlp_norm, ple_norm=ple_norm)

    sq, dx0, small_g = _local_step(x[0], p[:, 0], positions.reshape(S, 1), loss_target[0], W, P,
                                   rest_of_weights, send_layer1, send_mlp0, send_attn)

    r_in, r_conv, r_do, r_up1, r_dn1, r_pl1, r_gt1 = pending["layer1"](dx0)
    r_up0, r_dn0, r_pl0, r_gt0 = pending["mlp0"](dx0)
    r_qkv, r_ao = pending["attn"](dx0)
    big = {}
    for n, parts in (("attn_w_qkv", [r_qkv]), ("attn_w_o", [r_ao]), ("dn_w_in", [r_in]), ("dn_conv", [r_conv]),
                     ("dn_w_o", [r_do]), ("w_up", [r_up0, r_up1]), ("w_down", [r_dn0, r_dn1]),
                     ("w_ple", [r_pl0, r_pl1]), ("w_ple_gate", [r_gt0, r_gt1])):
        big[n] = _adamw(f"adamw_{n}", parts, w[n], m[n], v[n])

    loss_rows = jnp.pad((0.5 / D_MODEL) * jnp.sum(sq, axis=1, keepdims=True), ((0, SUBLANE - 1), (0, LANE - 1)))
    small_like = {n: w[n] for n in SMALL}
    parts_s = _all_gather("gather_small", [_pack_small(small_g, loss_rows)], [(0, None)])[0][0]
    zero_rows = jnp.zeros((SUBLANE, LANE), F32)
    small = _adamw("adamw_small", [parts_s], _pack_small(w, zero_rows)[None], _pack_small(m, zero_rows)[None],
                   _pack_small(v, zero_rows)[None])
    loss = small[0][0, SMALL_ROWS - SUBLANE, 0]
    small = [_unpack_small(b[0], small_like) for b in small]

    outs = [loss, dx0[None]]
    for k in range(4):
        for n in WEIGHTS:
            outs.append(small[k][n] if n in SMALL else big[n][k])
    return tuple(outs)
```
